```python
import jax, jax.numpy as jnp
from jax import lax
import numpy as np

D_MODEL = 2048
BATCH = 8
SEQ = 2048
DEPTH = 1

MLA_HEADS = 16
MLA_Q_LORA = 512
MLA_KV_LORA = 512
MLA_NOPE_DIM = 128
MLA_ROPE_DIM = 64
MLA_V_DIM = 128
ROPE_THETA = 10000.0
Q_BLOCK = 128
HGRN_HEADS = 16
HGRN_EXPAND = 128
HGRN_HEAD_DIM = D_MODEL // HGRN_HEADS
HGRN_CHUNK = 64
D_FF = 5632
MACARON_SCALE = 0.5
NORM_EPS = 1e-6

MLA_WIDTH = MLA_HEADS * MLA_V_DIM
HGRN_KEY_WIDTH = HGRN_HEADS * HGRN_EXPAND
HGRN_VAL_WIDTH = HGRN_HEADS * HGRN_HEAD_DIM
IN_SPLITS = (MLA_Q_LORA, MLA_KV_LORA, MLA_ROPE_DIM, HGRN_KEY_WIDTH, HGRN_KEY_WIDTH,
             HGRN_VAL_WIDTH, HGRN_VAL_WIDTH, D_MODEL, D_MODEL)
IN_COLS = 13376

kernel_name = "hybrid_mla_hgrn2_macaron_sandwich"


def rmsnorm(x, w):
    x32 = x.astype(jnp.float32)
    y = x32 * lax.rsqrt(jnp.mean(x32 * x32, axis=-1, keepdims=True) + NORM_EPS)
    return (y * w.astype(jnp.float32)).astype(x.dtype)


def swiglu(h, w_gate, w_up, w_down):
    return (jax.nn.silu(h @ w_gate) * (h @ w_up)) @ w_down


def rope(t, cos, sin):
    half = t.shape[-1] // 2
    t1, t2 = t[..., :half], t[..., half:]
    return jnp.concatenate([t1 * cos - t2 * sin, t1 * sin + t2 * cos], axis=-1).astype(t.dtype)


def split_columns(proj):
    parts, start = [], 0
    for width in IN_SPLITS:
        parts.append(proj[..., start:start + width])
        start += width
    return parts


def mla_branch(c_q, c_kv, k_rope, positions, q_norm, w_q_up, kv_norm, w_kv_up, w_o):
    B, S, _ = c_q.shape
    q = (rmsnorm(c_q, q_norm) @ w_q_up).reshape(B, S, MLA_HEADS, MLA_NOPE_DIM + MLA_ROPE_DIM)
    q_nope, q_rope = q[..., :MLA_NOPE_DIM], q[..., MLA_NOPE_DIM:]
    kv = (rmsnorm(c_kv, kv_norm) @ w_kv_up).reshape(B, S, MLA_HEADS, MLA_NOPE_DIM + MLA_V_DIM)
    k_nope, v = kv[..., :MLA_NOPE_DIM], kv[..., MLA_NOPE_DIM:]
    half = MLA_ROPE_DIM // 2
    inv_freq = ROPE_THETA ** (-jnp.arange(half, dtype=jnp.float32) / half)
    ang = positions.astype(jnp.float32)[..., None] * inv_freq
    cos, sin = jnp.cos(ang), jnp.sin(ang)
    q_rope = rope(q_rope, cos[:, :, None, :], sin[:, :, None, :])
    k_rope = rope(k_rope, cos, sin)
    scale = (MLA_NOPE_DIM + MLA_ROPE_DIM) ** -0.5
    nb = S // Q_BLOCK
    key_idx = jnp.arange(S)

    def blocks(t):
        return t.reshape((B, nb, Q_BLOCK) + t.shape[2:]).swapaxes(0, 1)

    def attend(args):
        qn, qr, blk = args
        q_idx = blk * Q_BLOCK + jnp.arange(Q_BLOCK)
        s = (jnp.einsum('bqhd,bkhd->bhqk', qn, k_nope)
             + jnp.einsum('bqhr,bkr->bhqk', qr, k_rope)).astype(jnp.float32) * scale
        s = jnp.where(key_idx[None, :] <= q_idx[:, None], s, -jnp.inf)
        p = jax.nn.softmax(s, axis=-1).astype(v.dtype)
        return jnp.einsum('bhqk,bkhd->bqhd', p, v)

    o = lax.map(attend, (blocks(q_nope), blocks(q_rope), jnp.arange(nb)))
    o = o.swapaxes(0, 1).reshape(B, S, MLA_WIDTH)
    return o @ w_o


def hgrn2_branch(hq, hf, hi, hg, lower_bound, out_norm, w_o):
    B, S, _ = hq.shape
    C = HGRN_CHUNK
    nc = S // C

    def heads(t, d):
        return t.reshape(B, S, HGRN_HEADS, d).transpose(0, 2, 1, 3).astype(jnp.float32)

    q = heads(jax.nn.silu(hq), HGRN_EXPAND)
    lb = lower_bound.astype(jnp.float32)
    f_gate = lb + (1.0 - lb) * jax.nn.sigmoid(hf.astype(jnp.float32))
    k = heads(1.0 - f_gate, HGRN_EXPAND)
    log_f = heads(jnp.log(f_gate), HGRN_EXPAND)
    v = heads(hi, HGRN_HEAD_DIM)

    def chunks(t):
        return t.reshape(B, HGRN_HEADS, nc, C, t.shape[-1]).transpose(2, 0, 1, 3, 4)

    causal = jnp.tril(jnp.ones((C, C), dtype=bool))

    def step(state, inp):
        qc, kc, vc, gc = inp
        b = jnp.cumsum(gc, axis=2)
        inter = jnp.einsum('bhtk,bhkv->bhtv', qc * jnp.exp(b), state)
        diff = b[:, :, :, None, :] - b[:, :, None, :, :]
        decay = jnp.exp(jnp.where(causal[:, :, None], diff, -jnp.inf))
        scores = jnp.einsum('bhtk,bhtsk,bhsk->bhts', qc, decay, kc)
        intra = jnp.einsum('bhts,bhsv->bhtv', scores, vc)
        b_last = b[:, :, -1:, :]
        new_state = (jnp.exp(b_last[:, :, 0, :])[..., None] * state
                     + jnp.einsum('bhsk,bhsv->bhkv', kc * jnp.exp(b_last - b), vc))
        return new_state, inter + intra

    s0 = jnp.zeros((B, HGRN_HEADS, HGRN_EXPAND, HGRN_HEAD_DIM), jnp.float32)
    _, o = lax.scan(step, s0, (chunks(q), chunks(k), chunks(v), chunks(log_f)))
    o = o.transpose(1, 0, 3, 2, 4).reshape(B, S, HGRN_HEADS, HGRN_HEAD_DIM)
    o = rmsnorm(o, out_norm) * jax.nn.silu(hg.reshape(B, S, HGRN_HEADS, HGRN_HEAD_DIM).astype(jnp.float32))
    return o.reshape(B, S, HGRN_VAL_WIDTH).astype(hq.dtype) @ w_o


def token_mixer(h, positions, w_in, mla_q_norm, mla_w_q_up, mla_kv_norm, mla_w_kv_up, mla_w_o,
                lower_bound, hgrn_out_norm, hgrn_w_o, w_out):
    proj = h @ w_in
    c_q, c_kv, k_rope, hq, hf, hi, hg, gate_a, gate_b = split_columns(proj)
    y_a = mla_branch(c_q, c_kv, k_rope, positions, mla_q_norm, mla_w_q_up, mla_kv_norm, mla_w_kv_up, mla_w_o)
    y_b = hgrn2_branch(hq, hf, hi, hg, lower_bound, hgrn_out_norm, hgrn_w_o)
    merged = jax.nn.sigmoid(gate_a) * y_a + jax.nn.sigmoid(gate_b) * y_b
    return merged @ w_out


def _fwd_setup_inputs(seed: int = 0) -> dict:
    key = jax.random.key(seed)
    ks = jax.random.split(key, 28)

    def w(k, shape, fan_in):
        return jax.random.normal(k, shape, jnp.float32) * (fan_in ** -0.5)

    def gain(k, shape):
        return 1.0 + 0.05 * jax.random.normal(k, shape, jnp.float32)

    x = jax.random.normal(ks[0], (BATCH, SEQ, D_MODEL), jnp.float32)
    offsets = jax.random.randint(ks[1], (BATCH, 1), 0, 4096, dtype=jnp.int32)
    positions = offsets + jnp.arange(SEQ, dtype=jnp.int32)[None, :]
    return {
        "x": x,
        "positions": positions,
        "ffn1_norm_pre": gain(ks[2], (DEPTH, D_MODEL)),
        "ffn1_w_gate": w(ks[3], (DEPTH, D_MODEL, D_FF), D_MODEL),
        "ffn1_w_up": w(ks[4], (DEPTH, D_MODEL, D_FF), D_MODEL),
        "ffn1_w_down": w(ks[5], (DEPTH, D_FF, D_MODEL), D_FF),
        "ffn1_norm_post": gain(ks[6], (DEPTH, D_MODEL)),
        "mix_norm_pre": gain(ks[7], (DEPTH, D_MODEL)),
        "w_in": w(ks[8], (DEPTH, D_MODEL, IN_COLS), D_MODEL),
        "mla_q_norm": gain(ks[9], (DEPTH, MLA_Q_LORA)),
        "mla_w_q_up": w(ks[10], (DEPTH, MLA_Q_LORA, MLA_HEADS * (MLA_NOPE_DIM + MLA_ROPE_DIM)), MLA_Q_LORA),
        "mla_kv_norm": gain(ks[11], (DEPTH, MLA_KV_LORA)),
        "mla_w_kv_up": w(ks[12], (DEPTH, MLA_KV_LORA, MLA_HEADS * (MLA_NOPE_DIM + MLA_V_DIM)), MLA_KV_LORA),
        "mla_w_o": w(ks[13], (DEPTH, MLA_WIDTH, D_MODEL), MLA_WIDTH),
        "hgrn_lb_logits": 0.5 * jax.random.normal(ks[14], (DEPTH + 1, HGRN_KEY_WIDTH), jnp.float32),
        "hgrn_out_norm": gain(ks[15], (DEPTH, HGRN_HEAD_DIM)),
        "hgrn_w_o": w(ks[16], (DEPTH, HGRN_VAL_WIDTH, D_MODEL), HGRN_VAL_WIDTH),
        "w_out": w(ks[17], (DEPTH, D_MODEL, D_MODEL), D_MODEL),
        "mix_norm_post": gain(ks[18], (DEPTH, D_MODEL)),
        "ffn2_norm_pre": gain(ks[19], (DEPTH, D_MODEL)),
        "ffn2_w_gate": w(ks[20], (DEPTH, D_MODEL, D_FF), D_MODEL),
        "ffn2_w_up": w(ks[21], (DEPTH, D_MODEL, D_FF), D_MODEL),
        "ffn2_w_down": w(ks[22], (DEPTH, D_FF, D_MODEL), D_FF),
        "ffn2_norm_post": gain(ks[23], (DEPTH, D_MODEL)),
    }


def _fwd_reference(x, positions, ffn1_norm_pre, ffn1_w_gate, ffn1_w_up, ffn1_w_down, ffn1_norm_post,
              mix_norm_pre, w_in, mla_q_norm, mla_w_q_up, mla_kv_norm, mla_w_kv_up, mla_w_o,
              hgrn_lb_logits, hgrn_out_norm, hgrn_w_o, w_out, mix_norm_post,
              ffn2_norm_pre, ffn2_w_gate, ffn2_w_up, ffn2_w_down, ffn2_norm_post):
    lb_table = jnp.cumsum(jax.nn.softmax(hgrn_lb_logits.astype(jnp.float32), axis=0), axis=0)
    for l in range(DEPTH):
        h = rmsnorm(x, ffn1_norm_pre[l])
        x = x + MACARON_SCALE * rmsnorm(swiglu(h, ffn1_w_gate[l], ffn1_w_up[l], ffn1_w_down[l]), ffn1_norm_post[l])
        h = rmsnorm(x, mix_norm_pre[l])
        y = token_mixer(h, positions, w_in[l], mla_q_norm[l], mla_w_q_up[l], mla_kv_norm[l], mla_w_kv_up[l],
                        mla_w_o[l], lb_table[l], hgrn_out_norm[l], hgrn_w_o[l], w_out[l])
        x = x + rmsnorm(y, mix_norm_post[l])
        h = rmsnorm(x, ffn2_norm_pre[l])
        x = x + MACARON_SCALE * rmsnorm(swiglu(h, ffn2_w_gate[l], ffn2_w_up[l], ffn2_w_down[l]), ffn2_norm_post[l])
    return x


import jax as _jax
import jax.numpy as _jnp

TWIN_FORMAT = 'train_step'
FWD_PARAMS = ['x', 'positions', 'ffn1_norm_pre', 'ffn1_w_gate', 'ffn1_w_up', 'ffn1_w_down', 'ffn1_norm_post', 'mix_norm_pre', 'w_in', 'mla_q_norm', 'mla_w_q_up', 'mla_kv_norm', 'mla_w_kv_up', 'mla_w_o', 'hgrn_lb_logits', 'hgrn_out_norm', 'hgrn_w_o', 'w_out', 'mix_norm_post', 'ffn2_norm_pre', 'ffn2_w_gate', 'ffn2_w_up', 'ffn2_w_down', 'ffn2_norm_post']
TWIN_WEIGHTS = ['ffn1_norm_pre', 'ffn1_w_gate', 'ffn1_w_up', 'ffn1_w_down', 'ffn1_norm_post', 'mix_norm_pre', 'w_in', 'mla_q_norm', 'mla_w_q_up', 'mla_kv_norm', 'mla_w_kv_up', 'mla_w_o', 'hgrn_lb_logits', 'hgrn_out_norm', 'hgrn_w_o', 'w_out', 'mix_norm_post', 'ffn2_norm_pre', 'ffn2_w_gate', 'ffn2_w_up', 'ffn2_w_down', 'ffn2_norm_post']
TWIN_DIFF_INPUT = 'x'
TWIN_INPUTS = ['x', 'positions', 'ffn1_norm_pre', 'ffn1_w_gate', 'ffn1_w_up', 'ffn1_w_down', 'ffn1_norm_post', 'mix_norm_pre', 'w_in', 'mla_q_norm', 'mla_w_q_up', 'mla_kv_norm', 'mla_w_kv_up', 'mla_w_o', 'hgrn_lb_logits', 'hgrn_out_norm', 'hgrn_w_o', 'w_out', 'mix_norm_post', 'ffn2_norm_pre', 'ffn2_w_gate', 'ffn2_w_up', 'ffn2_w_down', 'ffn2_norm_post', 'loss_target', 'm_ffn1_norm_pre', 'm_ffn1_w_gate', 'm_ffn1_w_up', 'm_ffn1_w_down', 'm_ffn1_norm_post', 'm_mix_norm_pre', 'm_w_in', 'm_mla_q_norm', 'm_mla_w_q_up', 'm_mla_kv_norm', 'm_mla_w_kv_up', 'm_mla_w_o', 'm_hgrn_lb_logits', 'm_hgrn_out_norm', 'm_hgrn_w_o', 'm_w_out', 'm_mix_norm_post', 'm_ffn2_norm_pre', 'm_ffn2_w_gate', 'm_ffn2_w_up', 'm_ffn2_w_down', 'm_ffn2_norm_post', 'v_ffn1_norm_pre', 'v_ffn1_w_gate', 'v_ffn1_w_up', 'v_ffn1_w_down', 'v_ffn1_norm_post', 'v_mix_norm_pre', 'v_w_in', 'v_mla_q_norm', 'v_mla_w_q_up', 'v_mla_kv_norm', 'v_mla_w_kv_up', 'v_mla_w_o', 'v_hgrn_lb_logits', 'v_hgrn_out_norm', 'v_hgrn_w_o', 'v_w_out', 'v_mix_norm_post', 'v_ffn2_norm_pre', 'v_ffn2_w_gate', 'v_ffn2_w_up', 'v_ffn2_w_down', 'v_ffn2_norm_post']
TWIN_OUTPUTS = ['loss', 'grad_x', 'grad_ffn1_norm_pre', 'grad_ffn1_w_gate', 'grad_ffn1_w_up', 'grad_ffn1_w_down', 'grad_ffn1_norm_post', 'grad_mix_norm_pre', 'grad_w_in', 'grad_mla_q_norm', 'grad_mla_w_q_up', 'grad_mla_kv_norm', 'grad_mla_w_kv_up', 'grad_mla_w_o', 'grad_hgrn_lb_logits', 'grad_hgrn_out_norm', 'grad_hgrn_w_o', 'grad_w_out', 'grad_mix_norm_post', 'grad_ffn2_norm_pre', 'grad_ffn2_w_gate', 'grad_ffn2_w_up', 'grad_ffn2_w_down', 'grad_ffn2_norm_post', 'delta_ffn1_norm_pre', 'delta_ffn1_w_gate', 'delta_ffn1_w_up', 'delta_ffn1_w_down', 'delta_ffn1_norm_post', 'delta_mix_norm_pre', 'delta_w_in', 'delta_mla_q_norm', 'delta_mla_w_q_up', 'delta_mla_kv_norm', 'delta_mla_w_kv_up', 'delta_mla_w_o', 'delta_hgrn_lb_logits', 'delta_hgrn_out_norm', 'delta_hgrn_w_o', 'delta_w_out', 'delta_mix_norm_post', 'delta_ffn2_norm_pre', 'delta_ffn2_w_gate', 'delta_ffn2_w_up', 'delta_ffn2_w_down', 'delta_ffn2_norm_post', 'new_m_ffn1_norm_pre', 'new_m_ffn1_w_gate', 'new_m_ffn1_w_up', 'new_m_ffn1_w_down', 'new_m_ffn1_norm_post', 'new_m_mix_norm_pre', 'new_m_w_in', 'new_m_mla_q_norm', 'new_m_mla_w_q_up', 'new_m_mla_kv_norm', 'new_m_mla_w_kv_up', 'new_m_mla_w_o', 'new_m_hgrn_lb_logits', 'new_m_hgrn_out_norm', 'new_m_hgrn_w_o', 'new_m_w_out', 'new_m_mix_norm_post', 'new_m_ffn2_norm_pre', 'new_m_ffn2_w_gate', 'new_m_ffn2_w_up', 'new_m_ffn2_w_down', 'new_m_ffn2_norm_post', 'new_v_ffn1_norm_pre', 'new_v_ffn1_w_gate', 'new_v_ffn1_w_up', 'new_v_ffn1_w_down', 'new_v_ffn1_norm_post', 'new_v_mix_norm_pre', 'new_v_w_in', 'new_v_mla_q_norm', 'new_v_mla_w_q_up', 'new_v_mla_kv_norm', 'new_v_mla_w_kv_up', 'new_v_mla_w_o', 'new_v_hgrn_lb_logits', 'new_v_hgrn_out_norm', 'new_v_hgrn_w_o', 'new_v_w_out', 'new_v_mix_norm_post', 'new_v_ffn2_norm_pre', 'new_v_ffn2_w_gate', 'new_v_ffn2_w_up', 'new_v_ffn2_w_down', 'new_v_ffn2_norm_post']
TWIN_LEAF_KINDS = {'loss': 'loss', 'grad_x': 'grad_x', 'grad_ffn1_norm_pre': 'grad_w', 'grad_ffn1_w_gate': 'grad_w', 'grad_ffn1_w_up': 'grad_w', 'grad_ffn1_w_down': 'grad_w', 'grad_ffn1_norm_post': 'grad_w', 'grad_mix_norm_pre': 'grad_w', 'grad_w_in': 'grad_w', 'grad_mla_q_norm': 'grad_w', 'grad_mla_w_q_up': 'grad_w', 'grad_mla_kv_norm': 'grad_w', 'grad_mla_w_kv_up': 'grad_w', 'grad_mla_w_o': 'grad_w', 'grad_hgrn_lb_logits': 'grad_w', 'grad_hgrn_out_norm': 'grad_w', 'grad_hgrn_w_o': 'grad_w', 'grad_w_out': 'grad_w', 'grad_mix_norm_post': 'grad_w', 'grad_ffn2_norm_pre': 'grad_w', 'grad_ffn2_w_gate': 'grad_w', 'grad_ffn2_w_up': 'grad_w', 'grad_ffn2_w_down': 'grad_w', 'grad_ffn2_norm_post': 'grad_w', 'delta_ffn1_norm_pre': 'delta_w', 'delta_ffn1_w_gate': 'delta_w', 'delta_ffn1_w_up': 'delta_w', 'delta_ffn1_w_down': 'delta_w', 'delta_ffn1_norm_post': 'delta_w', 'delta_mix_norm_pre': 'delta_w', 'delta_w_in': 'delta_w', 'delta_mla_q_norm': 'delta_w', 'delta_mla_w_q_up': 'delta_w', 'delta_mla_kv_norm': 'delta_w', 'delta_mla_w_kv_up': 'delta_w', 'delta_mla_w_o': 'delta_w', 'delta_hgrn_lb_logits': 'delta_w', 'delta_hgrn_out_norm': 'delta_w', 'delta_hgrn_w_o': 'delta_w', 'delta_w_out': 'delta_w', 'delta_mix_norm_post': 'delta_w', 'delta_ffn2_norm_pre': 'delta_w', 'delta_ffn2_w_gate': 'delta_w', 'delta_ffn2_w_up': 'delta_w', 'delta_ffn2_w_down': 'delta_w', 'delta_ffn2_norm_post': 'delta_w', 'new_m_ffn1_norm_pre': 'new_m', 'new_m_ffn1_w_gate': 'new_m', 'new_m_ffn1_w_up': 'new_m', 'new_m_ffn1_w_down': 'new_m', 'new_m_ffn1_norm_post': 'new_m', 'new_m_mix_norm_pre': 'new_m', 'new_m_w_in': 'new_m', 'new_m_mla_q_norm': 'new_m', 'new_m_mla_w_q_up': 'new_m', 'new_m_mla_kv_norm': 'new_m', 'new_m_mla_w_kv_up': 'new_m', 'new_m_mla_w_o': 'new_m', 'new_m_hgrn_lb_logits': 'new_m', 'new_m_hgrn_out_norm': 'new_m', 'new_m_hgrn_w_o': 'new_m', 'new_m_w_out': 'new_m', 'new_m_mix_norm_post': 'new_m', 'new_m_ffn2_norm_pre': 'new_m', 'new_m_ffn2_w_gate': 'new_m', 'new_m_ffn2_w_up': 'new_m', 'new_m_ffn2_w_down': 'new_m', 'new_m_ffn2_norm_post': 'new_m', 'new_v_ffn1_norm_pre': 'new_v', 'new_v_ffn1_w_gate': 'new_v', 'new_v_ffn1_w_up': 'new_v', 'new_v_ffn1_w_down': 'new_v', 'new_v_ffn1_norm_post': 'new_v', 'new_v_mix_norm_pre': 'new_v', 'new_v_w_in': 'new_v', 'new_v_mla_q_norm': 'new_v', 'new_v_mla_w_q_up': 'new_v', 'new_v_mla_kv_norm': 'new_v', 'new_v_mla_w_kv_up': 'new_v', 'new_v_mla_w_o': 'new_v', 'new_v_hgrn_lb_logits': 'new_v', 'new_v_hgrn_out_norm': 'new_v', 'new_v_hgrn_w_o': 'new_v', 'new_v_w_out': 'new_v', 'new_v_mix_norm_post': 'new_v', 'new_v_ffn2_norm_pre': 'new_v', 'new_v_ffn2_w_gate': 'new_v', 'new_v_ffn2_w_up': 'new_v', 'new_v_ffn2_w_down': 'new_v', 'new_v_ffn2_norm_post': 'new_v'}


def _forward(args):
    return _fwd_reference(*[args[k] for k in FWD_PARAMS])


def _output_shape():
    out = _jax.eval_shape(lambda: _forward(_fwd_setup_inputs(0)))
    return out.shape, out.dtype

N_MICROBATCH = 1
ADAM_LR = 0.001
ADAM_B1 = 0.9
ADAM_B2 = 0.999
ADAM_EPS = 1e-08
ADAM_WD = 0.01
ADAM_STEP = 10
PER_EXAMPLE_BATCH_AXIS = {'x': 0, 'positions': 0, 'loss_target': 0}
SHARED_INPUTS = []
_WEIGHT_DTYPES = {'ffn1_norm_pre': _jnp.float32, 'ffn1_w_gate': _jnp.float32, 'ffn1_w_up': _jnp.float32, 'ffn1_w_down': _jnp.float32, 'ffn1_norm_post': _jnp.float32, 'mix_norm_pre': _jnp.float32, 'w_in': _jnp.float32, 'mla_q_norm': _jnp.float32, 'mla_w_q_up': _jnp.float32, 'mla_kv_norm': _jnp.float32, 'mla_w_kv_up': _jnp.float32, 'mla_w_o': _jnp.float32, 'hgrn_lb_logits': _jnp.float32, 'hgrn_out_norm': _jnp.float32, 'hgrn_w_o': _jnp.float32, 'w_out': _jnp.float32, 'mix_norm_post': _jnp.float32, 'ffn2_norm_pre': _jnp.float32, 'ffn2_w_gate': _jnp.float32, 'ffn2_w_up': _jnp.float32, 'ffn2_w_down': _jnp.float32, 'ffn2_norm_post': _jnp.float32}
MOMENT_SCALE = {'ffn1_norm_pre': 1.444829e-01, 'ffn1_w_gate': 6.184598e-02, 'ffn1_w_up': 6.097784e-02, 'ffn1_w_down': 1.013101e-01, 'ffn1_norm_post': 1.982005e+00, 'mix_norm_pre': 1.778089e-01, 'w_in': 6.780635e-02, 'mla_q_norm': 6.101802e-02, 'mla_w_q_up': 2.561523e-02, 'mla_kv_norm': 9.420927e-02, 'mla_w_kv_up': 3.164256e-02, 'mla_w_o': 3.634993e-02, 'hgrn_lb_logits': 9.619725e-03, 'hgrn_out_norm': 5.153614e-01, 'hgrn_w_o': 1.201494e-01, 'w_out': 1.210989e-01, 'mix_norm_post': 8.000418e+00, 'ffn2_norm_pre': 8.353358e-02, 'ffn2_w_gate': 3.523030e-02, 'ffn2_w_up': 4.080574e-02, 'ffn2_w_down': 6.767219e-02, 'ffn2_norm_post': 2.005499e+00}


def _to_microbatches(a, axis):
    t = _jnp.moveaxis(a, axis, 0)
    t = t.reshape((N_MICROBATCH, t.shape[0] // N_MICROBATCH) + t.shape[1:])
    return _jnp.moveaxis(t, 1, axis + 1)


def setup_inputs(seed: int = 0) -> dict:
    inp = _fwd_setup_inputs(seed)
    key = _jax.random.fold_in(_jax.random.key(seed), 7919)
    shape, _ = _output_shape()
    out = dict(inp)
    out["loss_target"] = _jax.random.normal(_jax.random.fold_in(key, 0), shape, _jnp.float32)
    for i, name in enumerate(TWIN_WEIGHTS):
        w = inp[name].astype(_jnp.float32)
        if MOMENT_SCALE is None:
            s = _jnp.sqrt(_jnp.mean(_jnp.square(w)) + 1e-30)
        else:
            s = MOMENT_SCALE[name]
        km, kv = _jax.random.split(_jax.random.fold_in(key, i + 1))
        out[name] = w
        out["m_" + name] = s * _jax.random.normal(km, w.shape, _jnp.float32)
        out["v_" + name] = (s * s) * _jax.random.uniform(kv, w.shape, _jnp.float32, 0.5, 1.5)
    if N_MICROBATCH > 1:
        for name, axis in PER_EXAMPLE_BATCH_AXIS.items():
            out[name] = _to_microbatches(out[name], axis)
    return {'x': out['x'], 'positions': out['positions'], 'ffn1_norm_pre': out['ffn1_norm_pre'], 'ffn1_w_gate': out['ffn1_w_gate'], 'ffn1_w_up': out['ffn1_w_up'], 'ffn1_w_down': out['ffn1_w_down'], 'ffn1_norm_post': out['ffn1_norm_post'], 'mix_norm_pre': out['mix_norm_pre'], 'w_in': out['w_in'], 'mla_q_norm': out['mla_q_norm'], 'mla_w_q_up': out['mla_w_q_up'], 'mla_kv_norm': out['mla_kv_norm'], 'mla_w_kv_up': out['mla_w_kv_up'], 'mla_w_o': out['mla_w_o'], 'hgrn_lb_logits': out['hgrn_lb_logits'], 'hgrn_out_norm': out['hgrn_out_norm'], 'hgrn_w_o': out['hgrn_w_o'], 'w_out': out['w_out'], 'mix_norm_post': out['mix_norm_post'], 'ffn2_norm_pre': out['ffn2_norm_pre'], 'ffn2_w_gate': out['ffn2_w_gate'], 'ffn2_w_up': out['ffn2_w_up'], 'ffn2_w_down': out['ffn2_w_down'], 'ffn2_norm_post': out['ffn2_norm_post'], 'loss_target': out['loss_target'], 'm_ffn1_norm_pre': out['m_ffn1_norm_pre'], 'm_ffn1_w_gate': out['m_ffn1_w_gate'], 'm_ffn1_w_up': out['m_ffn1_w_up'], 'm_ffn1_w_down': out['m_ffn1_w_down'], 'm_ffn1_norm_post': out['m_ffn1_norm_post'], 'm_mix_norm_pre': out['m_mix_norm_pre'], 'm_w_in': out['m_w_in'], 'm_mla_q_norm': out['m_mla_q_norm'], 'm_mla_w_q_up': out['m_mla_w_q_up'], 'm_mla_kv_norm': out['m_mla_kv_norm'], 'm_mla_w_kv_up': out['m_mla_w_kv_up'], 'm_mla_w_o': out['m_mla_w_o'], 'm_hgrn_lb_logits': out['m_hgrn_lb_logits'], 'm_hgrn_out_norm': out['m_hgrn_out_norm'], 'm_hgrn_w_o': out['m_hgrn_w_o'], 'm_w_out': out['m_w_out'], 'm_mix_norm_post': out['m_mix_norm_post'], 'm_ffn2_norm_pre': out['m_ffn2_norm_pre'], 'm_ffn2_w_gate': out['m_ffn2_w_gate'], 'm_ffn2_w_up': out['m_ffn2_w_up'], 'm_ffn2_w_down': out['m_ffn2_w_down'], 'm_ffn2_norm_post': out['m_ffn2_norm_post'], 'v_ffn1_norm_pre': out['v_ffn1_norm_pre'], 'v_ffn1_w_gate': out['v_ffn1_w_gate'], 'v_ffn1_w_up': out['v_ffn1_w_up'], 'v_ffn1_w_down': out['v_ffn1_w_down'], 'v_ffn1_norm_post': out['v_ffn1_norm_post'], 'v_mix_norm_pre': out['v_mix_norm_pre'], 'v_w_in': out['v_w_in'], 'v_mla_q_norm': out['v_mla_q_norm'], 'v_mla_w_q_up': out['v_mla_w_q_up'], 'v_mla_kv_norm': out['v_mla_kv_norm'], 'v_mla_w_kv_up': out['v_mla_w_kv_up'], 'v_mla_w_o': out['v_mla_w_o'], 'v_hgrn_lb_logits': out['v_hgrn_lb_logits'], 'v_hgrn_out_norm': out['v_hgrn_out_norm'], 'v_hgrn_w_o': out['v_hgrn_w_o'], 'v_w_out': out['v_w_out'], 'v_mix_norm_post': out['v_mix_norm_post'], 'v_ffn2_norm_pre': out['v_ffn2_norm_pre'], 'v_ffn2_w_gate': out['v_ffn2_w_gate'], 'v_ffn2_w_up': out['v_ffn2_w_up'], 'v_ffn2_w_down': out['v_ffn2_w_down'], 'v_ffn2_norm_post': out['v_ffn2_norm_post']}


def _loss(weights, diff, rest, loss_target):
    with _jax.named_scope("forward"):
        args = {**rest, TWIN_DIFF_INPUT: diff, **{k: w.astype(_WEIGHT_DTYPES[k]) for k, w in weights.items()}}
        y = _forward(args)
    with _jax.named_scope("loss_head"):
        err = _jnp.square(y.astype(_jnp.float32) - loss_target)
        return 0.5 * _jnp.sum(_jnp.mean(err, axis=-1)) if err.ndim else 0.5 * err


def _adamw(w, g, m, v):
    m = ADAM_B1 * m + (1.0 - ADAM_B1) * g
    v = ADAM_B2 * v + (1.0 - ADAM_B2) * _jnp.square(g)
    m_hat = m / (1.0 - ADAM_B1 ** ADAM_STEP)
    v_hat = v / (1.0 - ADAM_B2 ** ADAM_STEP)
    delta = -ADAM_LR * (m_hat / (_jnp.sqrt(v_hat) + ADAM_EPS) + ADAM_WD * w)
    return delta, m, v


def reference(x, positions, ffn1_norm_pre, ffn1_w_gate, ffn1_w_up, ffn1_w_down, ffn1_norm_post, mix_norm_pre, w_in, mla_q_norm, mla_w_q_up, mla_kv_norm, mla_w_kv_up, mla_w_o, hgrn_lb_logits, hgrn_out_norm, hgrn_w_o, w_out, mix_norm_post, ffn2_norm_pre, ffn2_w_gate, ffn2_w_up, ffn2_w_down, ffn2_norm_post, loss_target, m_ffn1_norm_pre, m_ffn1_w_gate, m_ffn1_w_up, m_ffn1_w_down, m_ffn1_norm_post, m_mix_norm_pre, m_w_in, m_mla_q_norm, m_mla_w_q_up, m_mla_kv_norm, m_mla_w_kv_up, m_mla_w_o, m_hgrn_lb_logits, m_hgrn_out_norm, m_hgrn_w_o, m_w_out, m_mix_norm_post, m_ffn2_norm_pre, m_ffn2_w_gate, m_ffn2_w_up, m_ffn2_w_down, m_ffn2_norm_post, v_ffn1_norm_pre, v_ffn1_w_gate, v_ffn1_w_up, v_ffn1_w_down, v_ffn1_norm_post, v_mix_norm_pre, v_w_in, v_mla_q_norm, v_mla_w_q_up, v_mla_kv_norm, v_mla_w_kv_up, v_mla_w_o, v_hgrn_lb_logits, v_hgrn_out_norm, v_hgrn_w_o, v_w_out, v_mix_norm_post, v_ffn2_norm_pre, v_ffn2_w_gate, v_ffn2_w_up, v_ffn2_w_down, v_ffn2_norm_post):
    given = dict(x=x, positions=positions, ffn1_norm_pre=ffn1_norm_pre, ffn1_w_gate=ffn1_w_gate, ffn1_w_up=ffn1_w_up, ffn1_w_down=ffn1_w_down, ffn1_norm_post=ffn1_norm_post, mix_norm_pre=mix_norm_pre, w_in=w_in, mla_q_norm=mla_q_norm, mla_w_q_up=mla_w_q_up, mla_kv_norm=mla_kv_norm, mla_w_kv_up=mla_w_kv_up, mla_w_o=mla_w_o, hgrn_lb_logits=hgrn_lb_logits, hgrn_out_norm=hgrn_out_norm, hgrn_w_o=hgrn_w_o, w_out=w_out, mix_norm_post=mix_norm_post, ffn2_norm_pre=ffn2_norm_pre, ffn2_w_gate=ffn2_w_gate, ffn2_w_up=ffn2_w_up, ffn2_w_down=ffn2_w_down, ffn2_norm_post=ffn2_norm_post, loss_target=loss_target, m_ffn1_norm_pre=m_ffn1_norm_pre, m_ffn1_w_gate=m_ffn1_w_gate, m_ffn1_w_up=m_ffn1_w_up, m_ffn1_w_down=m_ffn1_w_down, m_ffn1_norm_post=m_ffn1_norm_post, m_mix_norm_pre=m_mix_norm_pre, m_w_in=m_w_in, m_mla_q_norm=m_mla_q_norm, m_mla_w_q_up=m_mla_w_q_up, m_mla_kv_norm=m_mla_kv_norm, m_mla_w_kv_up=m_mla_w_kv_up, m_mla_w_o=m_mla_w_o, m_hgrn_lb_logits=m_hgrn_lb_logits, m_hgrn_out_norm=m_hgrn_out_norm, m_hgrn_w_o=m_hgrn_w_o, m_w_out=m_w_out, m_mix_norm_post=m_mix_norm_post, m_ffn2_norm_pre=m_ffn2_norm_pre, m_ffn2_w_gate=m_ffn2_w_gate, m_ffn2_w_up=m_ffn2_w_up, m_ffn2_w_down=m_ffn2_w_down, m_ffn2_norm_post=m_ffn2_norm_post, v_ffn1_norm_pre=v_ffn1_norm_pre, v_ffn1_w_gate=v_ffn1_w_gate, v_ffn1_w_up=v_ffn1_w_up, v_ffn1_w_down=v_ffn1_w_down, v_ffn1_norm_post=v_ffn1_norm_post, v_mix_norm_pre=v_mix_norm_pre, v_w_in=v_w_in, v_mla_q_norm=v_mla_q_norm, v_mla_w_q_up=v_mla_w_q_up, v_mla_kv_norm=v_mla_kv_norm, v_mla_w_kv_up=v_mla_w_kv_up, v_mla_w_o=v_mla_w_o, v_hgrn_lb_logits=v_hgrn_lb_logits, v_hgrn_out_norm=v_hgrn_out_norm, v_hgrn_w_o=v_hgrn_w_o, v_w_out=v_w_out, v_mix_norm_post=v_mix_norm_post, v_ffn2_norm_pre=v_ffn2_norm_pre, v_ffn2_w_gate=v_ffn2_w_gate, v_ffn2_w_up=v_ffn2_w_up, v_ffn2_w_down=v_ffn2_w_down, v_ffn2_norm_post=v_ffn2_norm_post)
    weights = {n: given[n] for n in TWIN_WEIGHTS}
    shared = {n: given[n] for n in SHARED_INPUTS}
    per_example = {n: given[n] for n in ['x', 'positions']}
    grad_fn = _jax.value_and_grad(_loss, argnums=(0, 1))

    def one_microbatch(ex, loss_target):
        ex = dict(ex)
        diff = ex.pop(TWIN_DIFF_INPUT)
        return grad_fn(weights, diff, {**shared, **ex}, loss_target)

    if N_MICROBATCH == 1:
        loss, (grad_w, grad_x) = one_microbatch(per_example, given["loss_target"])
    else:
        def body(carry, xs):
            loss_sum, grad_sum = carry
            l_k, (gw_k, gx_k) = one_microbatch(xs[0], xs[1])
            with _jax.named_scope("update"):
                return (loss_sum + l_k, _jax.tree.map(_jnp.add, grad_sum, gw_k)), gx_k

        init = (_jnp.zeros((), _jnp.float32), _jax.tree.map(_jnp.zeros_like, weights))
        (loss, grad_w), grad_x = _jax.lax.scan(body, init, (per_example, given["loss_target"]))
    with _jax.named_scope("update"):
        delta_w, new_m, new_v = {}, {}, {}
        for n in TWIN_WEIGHTS:
            delta_w[n], new_m[n], new_v[n] = _adamw(weights[n], grad_w[n], given["m_" + n], given["v_" + n])
    return (loss, grad_x, *[grad_w[n] for n in TWIN_WEIGHTS], *[delta_w[n] for n in TWIN_WEIGHTS],
            *[new_m[n] for n in TWIN_WEIGHTS], *[new_v[n] for n in TWIN_WEIGHTS])
```

```python
import functools

import jax
import jax.numpy as jnp
from jax import lax
from jax.experimental import pallas as pl
from jax.experimental.pallas import tpu as pltpu

F32 = jnp.float32
BF16 = jnp.bfloat16
MESH = pl.DeviceIdType.MESH

NORM_EPS = 1e-6
MACARON_SCALE = 0.5
ROPE_THETA = 10000.0
HEAD = 128
ROPE = 64
QGROUP = 2 * HEAD
SUB = 16
ADAM_LR, ADAM_B1, ADAM_B2, ADAM_EPS, ADAM_WD, ADAM_STEP = 0.001, 0.9, 0.999, 1e-08, 0.01, 10

LANE = 128
VMEM_LIMIT = 48 * 1024 * 1024
PACK_W = 1024
N_CHUNK = 3

BIG_WEIGHTS = ("ffn1_w_gate", "ffn1_w_up", "ffn1_w_down", "w_in", "mla_w_q_up", "mla_w_kv_up",
               "mla_w_o", "hgrn_w_o", "w_out", "ffn2_w_gate", "ffn2_w_up", "ffn2_w_down")
COL_SHARDED = ("ffn1_w_gate", "ffn1_w_up", "w_in", "mla_w_q_up", "mla_w_kv_up", "ffn2_w_gate", "ffn2_w_up")
SMALL_WEIGHTS = ("ffn1_norm_pre", "ffn1_norm_post", "mix_norm_pre", "mla_q_norm", "mla_kv_norm",
                 "hgrn_lb_logits", "hgrn_out_norm", "mix_norm_post", "ffn2_norm_pre", "ffn2_norm_post")
ALL_WEIGHTS = ("ffn1_norm_pre", "ffn1_w_gate", "ffn1_w_up", "ffn1_w_down", "ffn1_norm_post", "mix_norm_pre",
               "w_in", "mla_q_norm", "mla_w_q_up", "mla_kv_norm", "mla_w_kv_up", "mla_w_o", "hgrn_lb_logits",
               "hgrn_out_norm", "hgrn_w_o", "w_out", "mix_norm_post", "ffn2_norm_pre", "ffn2_w_gate",
               "ffn2_w_up", "ffn2_w_down", "ffn2_norm_post")


def _params(*sem):
    return pltpu.CompilerParams(dimension_semantics=sem or None, vmem_limit_bytes=VMEM_LIMIT)


def _pick(n, cap):
    if n <= cap:
        return n
    best = None
    for t in range(LANE, cap + 1, LANE):
        if n % t == 0:
            best = t
    assert best is not None, (n, cap)
    return best


def _row_tile(n, row_bytes, budget=1 << 20):
    best = None
    for t in range(8, n + 1, 8):
        if n % t == 0 and t * row_bytes <= budget:
            best = t
    return n if best is None else best


def _sigmoid(x):
    return 1.0 / (1.0 + jnp.exp(-x))


def _silu(x):
    return x * _sigmoid(x)


def _dsilu(x):
    s = _sigmoid(x)
    return s * (1.0 + x * (1.0 - s))


def _mm(pairs, *, name, trans_b=False, out_dtype=F32):
    a0, b0 = pairs[0]
    m, kdim = a0.shape
    n = b0.shape[0] if trans_b else b0.shape[1]
    tm, tn, tk = _pick(m, 1024), _pick(n, 1536), _pick(kdim, 1024)
    nk = kdim // tk
    npair = len(pairs)

    def body(*refs):
        ins, o_ref, acc_ref = refs[:2 * npair], refs[2 * npair], refs[2 * npair + 1]
        k = pl.program_id(2)

        @pl.when(k == 0)
        def _():
            acc_ref[...] = jnp.zeros_like(acc_ref)

        for p in range(npair):
            a = ins[2 * p][...].astype(BF16)
            b = ins[2 * p + 1][...].astype(BF16)
            dims = (((1,), (1,)), ((), ())) if trans_b else (((1,), (0,)), ((), ()))
            acc_ref[...] += lax.dot_general(a, b, dims, preferred_element_type=F32)

        @pl.when(k == nk - 1)
        def _():
            o_ref[...] = acc_ref[...].astype(out_dtype)

    a_spec = pl.BlockSpec((tm, tk), lambda i, j, k: (i, k))
    b_spec = pl.BlockSpec((tn, tk), lambda i, j, k: (j, k)) if trans_b else pl.BlockSpec((tk, tn), lambda i, j, k: (k, j))
    flat = [t for pair in pairs for t in pair]
    return pl.pallas_call(
        body, name=name, grid=(m // tm, n // tn, nk),
        in_specs=[a_spec, b_spec] * npair,
        out_specs=pl.BlockSpec((tm, tn), lambda i, j, k: (i, j)),
        out_shape=jax.ShapeDtypeStruct((m, n), out_dtype),
        scratch_shapes=[pltpu.VMEM((tm, tn), F32)],
        compiler_params=_params("parallel", "parallel", "arbitrary"),
    )(*flat)


def _norm_fwd(y, w, *, name, resid=None, scale=1.0, out_dtype=F32):
    t, d = y.shape
    tr = _pick(t, 256)

    def body(*refs):
        if resid is None:
            y_ref, w_ref, o_ref = refs
        else:
            y_ref, w_ref, r_ref, o_ref = refs
        yv = y_ref[...]
        out = yv * lax.rsqrt(jnp.mean(yv * yv, axis=-1, keepdims=True) + NORM_EPS) * w_ref[...]
        if resid is not None:
            out = r_ref[...] + scale * out
        o_ref[...] = out.astype(out_dtype)

    row = pl.BlockSpec((tr, d), lambda i: (i, 0))
    wspec = pl.BlockSpec((1, d), lambda i: (0, 0))
    ins, specs = [y, w], [row, wspec]
    if resid is not None:
        ins.append(resid)
        specs.append(row)
    return pl.pallas_call(
        body, name=name, grid=(t // tr,), in_specs=specs, out_specs=row,
        out_shape=jax.ShapeDtypeStruct((t, d), out_dtype), compiler_params=_params("parallel"),
    )(*ins)


def _norm_bwd(x, w, dy, *, name, scale=1.0, dres=None):
    t, d = x.shape
    tr = _pick(t, 256)

    def body(*refs):
        if dres is None:
            x_ref, w_ref, dy_ref, dx_ref, dw_ref = refs
        else:
            x_ref, w_ref, dy_ref, dr_ref, dx_ref, dw_ref = refs

        @pl.when(pl.program_id(0) == 0)
        def _():
            dw_ref[...] = jnp.zeros_like(dw_ref)

        xv = x_ref[...]
        r = lax.rsqrt(jnp.mean(xv * xv, axis=-1, keepdims=True) + NORM_EPS)
        xhat = xv * r
        dyv = dy_ref[...].astype(F32) * scale
        dw_ref[...] += jnp.sum(dyv * xhat, axis=0, keepdims=True)
        t_ = dyv * w_ref[...]
        dx = r * (t_ - xhat * jnp.mean(t_ * xhat, axis=-1, keepdims=True))
        if dres is not None:
            dx = dx + dr_ref[...]
        dx_ref[...] = dx

    row = pl.BlockSpec((tr, d), lambda i: (i, 0))
    wspec = pl.BlockSpec((1, d), lambda i: (0, 0))
    ins, specs = [x, w, dy], [row, wspec, row]
    if dres is not None:
        ins.append(dres)
        specs.append(row)
    return pl.pallas_call(
        body, name=name, grid=(t // tr,), in_specs=specs, out_specs=(row, wspec),
        out_shape=(jax.ShapeDtypeStruct((t, d), F32), jax.ShapeDtypeStruct((1, d), F32)),
        compiler_params=_params("arbitrary"),
    )(*ins)


def _elementwise(fn, ins, out_dtypes, *, name):
    t, d = ins[0].shape
    tc = _pick(d, 2048)
    tr = _row_tile(t, tc * 4)
    nout = len(out_dtypes)

    def body(*refs):
        outs = fn(*[r[...] for r in refs[:len(ins)]])
        for o_ref, o in zip(refs[len(ins):], outs):
            o_ref[...] = o.astype(o_ref.dtype)

    spec = pl.BlockSpec((tr, tc), lambda i, j: (i, j))
    return pl.pallas_call(
        body, name=name, grid=(t // tr, d // tc), in_specs=[spec] * len(ins), out_specs=[spec] * nout,
        out_shape=[jax.ShapeDtypeStruct((t, d), dt) for dt in out_dtypes],
        compiler_params=_params("parallel", "parallel"),
    )(*ins)


def _swiglu_fwd(g, u, *, name):
    return _elementwise(lambda gv, uv: (_silu(gv) * uv,), [g, u], [BF16], name=name)[0]


def _swiglu_bwd(da, g, u, *, name):
    return _elementwise(lambda dav, gv, uv: (dav * uv * _dsilu(gv), dav * _silu(gv)), [da, g, u], [BF16, BF16], name=name)


def _merge_fwd(ga, gb, ya, yb, *, name):
    return _elementwise(lambda a, b, p, q: (_sigmoid(a) * p + _sigmoid(b) * q,), [ga, gb, ya, yb], [BF16], name=name)[0]


def _merge_bwd(dm, ga, gb, ya, yb, *, name):
    def fn(dmv, a, b, p, q):
        sa, sb = _sigmoid(a), _sigmoid(b)
        return dmv * p * sa * (1.0 - sa), dmv * q * sb * (1.0 - sb), dmv * sa, dmv * sb

    return _elementwise(fn, [dm, ga, gb, ya, yb], [BF16, BF16, BF16, BF16], name=name)


def _loss_head(xo, target, *, name):
    t, d = xo.shape
    tr = _pick(t, 256)

    def body(x_ref, t_ref, dx_ref, l_ref):
        @pl.when(pl.program_id(0) == 0)
        def _():
            l_ref[...] = jnp.zeros_like(l_ref)

        err = x_ref[...] - t_ref[...]
        dx_ref[...] = err * (1.0 / d)
        l_ref[...] += 0.5 * jnp.sum(jnp.mean(err * err, axis=-1, keepdims=True), axis=0, keepdims=True)

    row = pl.BlockSpec((tr, d), lambda i: (i, 0))
    dx, l = pl.pallas_call(
        body, name=name, grid=(t // tr,), in_specs=[row, row],
        out_specs=(row, pl.BlockSpec((1, 1), lambda i: (0, 0))),
        out_shape=(jax.ShapeDtypeStruct((t, d), F32), jax.ShapeDtypeStruct((1, 1), F32)),
        compiler_params=_params("arbitrary"),
    )(xo, target)
    return dx, l[0, 0]


def _rope(xin, tabs, *, name, group, backward, out_dtype):
    t, wdt = xin.shape
    ngroup = wdt // group
    tr = _pick(t, 256)
    cos_t, nsin_t, sin_t = tabs

    def body(x_ref, c_ref, n_ref, s_ref, o_ref):
        xv = x_ref[...]
        rot = xv[:, group - LANE:]
        if backward:
            out = rot * c_ref[...] + pltpu.roll(rot * n_ref[...], 32, 1) + pltpu.roll(rot * s_ref[...], LANE - 32, 1)
        else:
            out = rot * c_ref[...] + pltpu.roll(rot, LANE - 32, 1) * n_ref[...] + pltpu.roll(rot, 32, 1) * s_ref[...]
        if group > LANE:
            out = jnp.concatenate([xv[:, :group - LANE], out], axis=1)
        o_ref[...] = out.astype(out_dtype)

    xspec = pl.BlockSpec((tr, group), lambda i, g: (i, g))
    tspec = pl.BlockSpec((tr, LANE), lambda i, g: (i, 0))
    return pl.pallas_call(
        body, name=name, grid=(t // tr, ngroup), in_specs=[xspec, tspec, tspec, tspec], out_specs=xspec,
        out_shape=jax.ShapeDtypeStruct((t, wdt), out_dtype), compiler_params=_params("parallel", "parallel"),
    )(xin, cos_t, nsin_t, sin_t)


def _scores(q, kv, kr, qi, tq, scale):
    kcat = jnp.concatenate([kv[:, :HEAD], kr], axis=1)
    s = lax.dot_general(q, kcat, (((1,), (1,)), ((), ())), preferred_element_type=F32) * scale
    row = qi * tq + lax.broadcasted_iota(jnp.int32, s.shape, 0)
    col = lax.broadcasted_iota(jnp.int32, s.shape, 1)
    s = jnp.where(col <= row, s, -jnp.inf)
    p = jnp.exp(s - jnp.max(s, axis=-1, keepdims=True))
    return p / jnp.sum(p, axis=-1, keepdims=True), kcat


def _attn_fwd(qcat, kv, kr, *, name, scale):
    t = qcat.shape[0]
    nh = qcat.shape[1] // QGROUP
    tq = _pick(t, 256)

    def body(q_ref, kv_ref, kr_ref, o_ref):
        kvv = kv_ref[...]
        p, _ = _scores(q_ref[...], kvv, kr_ref[...], pl.program_id(1), tq, scale)
        o_ref[...] = jnp.dot(p.astype(BF16), kvv[:, HEAD:], preferred_element_type=F32).astype(BF16)

    return pl.pallas_call(
        body, name=name, grid=(nh, t // tq),
        in_specs=[pl.BlockSpec((tq, QGROUP), lambda h, i: (i, h)), pl.BlockSpec((t, QGROUP), lambda h, i: (0, h)),
                  pl.BlockSpec((t, LANE), lambda h, i: (0, 0))],
        out_specs=pl.BlockSpec((tq, HEAD), lambda h, i: (i, h)),
        out_shape=jax.ShapeDtypeStruct((t, nh * HEAD), BF16), compiler_params=_params("parallel", "parallel"),
    )(qcat, kv, kr)


def _attn_bwd(qcat, kv, kr, do, *, name, scale):
    t = qcat.shape[0]
    nh = qcat.shape[1] // QGROUP
    tq = _pick(t, 256)
    nq = t // tq

    def body(q_ref, kv_ref, kr_ref, do_ref, dq_ref, dkv_ref, dkr_ref, dk_acc, dv_acc):
        h, i = pl.program_id(0), pl.program_id(1)

        @pl.when(i == 0)
        def _():
            dk_acc[...] = jnp.zeros_like(dk_acc)
            dv_acc[...] = jnp.zeros_like(dv_acc)

        @pl.when((i == 0) & (h == 0))
        def _():
            dkr_ref[...] = jnp.zeros_like(dkr_ref)

        q = q_ref[...]
        kvv = kv_ref[...]
        dov = do_ref[...].astype(BF16)
        p, kcat = _scores(q, kvv, kr_ref[...], i, tq, scale)
        dp = lax.dot_general(dov, kvv[:, HEAD:], (((1,), (1,)), ((), ())), preferred_element_type=F32)
        ds = (p * (dp - jnp.sum(p * dp, axis=-1, keepdims=True)) * scale).astype(BF16)
        dq_ref[...] = jnp.dot(ds, kcat, preferred_element_type=F32)
        dk_acc[...] += lax.dot_general(ds, q, (((0,), (0,)), ((), ())), preferred_element_type=F32)
        dv_acc[...] += lax.dot_general(p.astype(BF16), dov, (((0,), (0,)), ((), ())), preferred_element_type=F32)

        @pl.when(i == nq - 1)
        def _():
            dk = dk_acc[...]
            dkv_ref[...] = jnp.concatenate([dk[:, :HEAD], dv_acc[...]], axis=1)
            dkr_ref[...] += dk[:, HEAD:]

    return pl.pallas_call(
        body, name=name, grid=(nh, nq),
        in_specs=[pl.BlockSpec((tq, QGROUP), lambda h, i: (i, h)), pl.BlockSpec((t, QGROUP), lambda h, i: (0, h)),
                  pl.BlockSpec((t, LANE), lambda h, i: (0, 0)), pl.BlockSpec((tq, HEAD), lambda h, i: (i, h))],
        out_specs=(pl.BlockSpec((tq, QGROUP), lambda h, i: (i, h)), pl.BlockSpec((t, QGROUP), lambda h, i: (0, h)),
                   pl.BlockSpec((t, LANE), lambda h, i: (0, 0))),
        out_shape=(jax.ShapeDtypeStruct((t, nh * QGROUP), F32), jax.ShapeDtypeStruct((t, nh * QGROUP), F32),
                   jax.ShapeDtypeStruct((t, LANE), F32)),
        scratch_shapes=[pltpu.VMEM((t, QGROUP), F32), pltpu.VMEM((t, HEAD), F32)],
        compiler_params=_params("arbitrary", "arbitrary"),
    )(qcat, kv, kr, do)


def _split3(x):
    hi = x.astype(BF16)
    r1 = x - hi.astype(F32)
    mid = r1.astype(BF16)
    lo = (r1 - mid.astype(F32)).astype(BF16)
    return hi, mid, lo


def _tri_matmul(mask, x):
    m = mask.astype(BF16)
    return sum(jnp.dot(m, part, preferred_element_type=F32) for part in _split3(x))


def _sub_cumsum(g, tb):
    row = lax.broadcasted_iota(jnp.int32, (tb, tb), 0)
    col = lax.broadcasted_iota(jnp.int32, (tb, tb), 1)
    return _tri_matmul(jnp.where((col <= row) & (col // SUB == row // SUB), 1.0, 0.0), g)


def _sub_suffix_prefix(after, before, tb):
    row = lax.broadcasted_iota(jnp.int32, (tb, tb), 0)
    col = lax.broadcasted_iota(jnp.int32, (tb, tb), 1)
    same = col // SUB == row // SUB
    return (_tri_matmul(jnp.where((col >= row) & same, 1.0, 0.0), after)
            + _tri_matmul(jnp.where((col < row) & same, 1.0, 0.0), before))


def _lower_bound(logits):
    mx = jnp.max(logits, axis=0, keepdims=True)
    e = jnp.exp(logits - mx)
    return e[0:1, :] / jnp.sum(e, axis=0, keepdims=True)


def _hgrn_fwd(hq, hf, hi, hg, logits, out_norm, *, name):
    t, wdt = hq.shape
    nh = wdt // HEAD
    tb = _pick(t, 128)
    ns = tb // SUB

    def body(hq_ref, hf_ref, hi_ref, hg_ref, lg_ref, w_ref, o_ref, yb_ref, st_ref, s_ref, q_s, k_s, b_s):
        @pl.when(pl.program_id(1) == 0)
        def _():
            s_ref[...] = jnp.zeros_like(s_ref)

        lb = _lower_bound(lg_ref[...])
        f = lb + (1.0 - lb) * _sigmoid(hf_ref[...])
        q_s[...] = _silu(hq_ref[...])
        k_s[...] = 1.0 - f
        b_s[...] = _sub_cumsum(jnp.log(f), tb)
        rowid = lax.broadcasted_iota(jnp.int32, (SUB, HEAD), 0)

        def sub(c, carry):
            rows = pl.ds(pl.multiple_of(c * SUB, SUB), SUB)
            qc, kc, bc, vc = q_s[rows, :], k_s[rows, :], b_s[rows, :], hi_ref[rows, :]
            st = s_ref[...]
            st_ref[0, c] = st
            bl = bc[SUB - 1:SUB, :]
            oc = lax.dot_general((qc * jnp.exp(bc)).astype(BF16), st.astype(BF16), (((1,), (1,)), ((), ())),
                                 preferred_element_type=F32)
            for s in range(SUB):
                e = jnp.where(rowid >= s, jnp.exp(bc - bc[s:s + 1, :]), 0.0)
                a = jnp.sum(qc * e * kc[s:s + 1, :], axis=1, keepdims=True)
                oc = oc + a * vc[s:s + 1, :]
            o_ref[rows, :] = oc
            kd = kc * jnp.exp(bl - bc)
            s_ref[...] = jnp.exp(bl) * st + lax.dot_general(vc.astype(BF16), kd.astype(BF16), (((0,), (0,)), ((), ())),
                                                             preferred_element_type=F32)
            return carry

        lax.fori_loop(0, ns, sub, 0)
        o = o_ref[...]
        r = lax.rsqrt(jnp.mean(o * o, axis=-1, keepdims=True) + NORM_EPS)
        yb_ref[...] = (o * r * w_ref[...] * _silu(hg_ref[...])).astype(BF16)

    blk = pl.BlockSpec((tb, HEAD), lambda h, j: (j, h))
    return pl.pallas_call(
        body, name=name, grid=(nh, t // tb),
        in_specs=[blk, blk, blk, blk, pl.BlockSpec((2, HEAD), lambda h, j: (0, h)), pl.BlockSpec((1, HEAD), lambda h, j: (0, 0))],
        out_specs=(blk, blk, pl.BlockSpec((1, ns, HEAD, HEAD), lambda h, j: (h, j, 0, 0))),
        out_shape=(jax.ShapeDtypeStruct((t, wdt), F32), jax.ShapeDtypeStruct((t, wdt), BF16),
                   jax.ShapeDtypeStruct((nh, t // SUB, HEAD, HEAD), F32)),
        scratch_shapes=[pltpu.VMEM((HEAD, HEAD), F32)] + [pltpu.VMEM((tb, HEAD), F32)] * 3,
        compiler_params=_params("parallel", "arbitrary"),
    )(hq, hf, hi, hg, logits, out_norm)


def _hgrn_bwd(hq, hf, hi, hg, o_raw, dyb, states, logits, out_norm, *, name):
    t, wdt = hq.shape
    nh = wdt // HEAD
    tb = _pick(t, 128)
    ns = tb // SUB
    nb = t // tb

    def body(hq_ref, hf_ref, hi_ref, hg_ref, o_ref, dy_ref, st_ref, lg_ref, w_ref,
             dhq_ref, dhf_ref, dhi_ref, dhg_ref, dlb_ref, dw_ref,
             ds_ref, q_s, k_s, b_s, do_s, dq_s, dk_s, dv_s, after_s, before_s, thru_s):
        @pl.when(pl.program_id(1) == 0)
        def _():
            ds_ref[...] = jnp.zeros_like(ds_ref)
            dlb_ref[...] = jnp.zeros_like(dlb_ref)
            dw_ref[...] = jnp.zeros_like(dw_ref)

        lb = _lower_bound(lg_ref[...])
        hqv, hgv = hq_ref[...], hg_ref[...]
        sig = _sigmoid(hf_ref[...])
        f = lb + (1.0 - lb) * sig
        q_s[...] = _silu(hqv)
        k_s[...] = 1.0 - f
        b_s[...] = _sub_cumsum(jnp.log(f), tb)

        o = o_ref[...]
        r = lax.rsqrt(jnp.mean(o * o, axis=-1, keepdims=True) + NORM_EPS)
        nrm = o * r
        w = w_ref[...]
        dy = dy_ref[...].astype(F32)
        dhg_ref[...] = (dy * nrm * w * _dsilu(hgv)).astype(BF16)
        dnw = dy * _silu(hgv)
        dw_ref[0] += jnp.sum(dnw * nrm, axis=0, keepdims=True)
        tt = dnw * w
        do_s[...] = r * (tt - nrm * jnp.mean(tt * nrm, axis=-1, keepdims=True))
        rowid = lax.broadcasted_iota(jnp.int32, (SUB, HEAD), 0)

        def sub(cc, carry):
            c = ns - 1 - cc
            rows = pl.ds(pl.multiple_of(c * SUB, SUB), SUB)
            qc, kc, bc, vc, doc = q_s[rows, :], k_s[rows, :], b_s[rows, :], hi_ref[rows, :], do_s[rows, :]
            st = st_ref[0, c]
            dst = ds_ref[...]
            bl = bc[SUB - 1:SUB, :]
            eb = jnp.exp(bc)
            ekd = jnp.exp(bl - bc)
            qe, kd = qc * eb, kc * ekd
            dob, vcb = doc.astype(BF16), vc.astype(BF16)
            dq_st = jnp.dot(dob, st.astype(BF16), preferred_element_type=F32) * eb
            dk_st = jnp.dot(vcb, dst.astype(BF16), preferred_element_type=F32) * ekd
            dv = lax.dot_general(kd.astype(BF16), dst.astype(BF16), (((1,), (1,)), ((), ())), preferred_element_type=F32)
            dq_in = jnp.zeros_like(qc)
            dk_in = jnp.zeros_like(qc)
            for s in range(SUB):
                e = jnp.where(rowid >= s, jnp.exp(bc - bc[s:s + 1, :]), 0.0)
                ek = e * kc[s:s + 1, :]
                a = jnp.sum(qc * ek, axis=1, keepdims=True)
                da = jnp.sum(doc * vc[s:s + 1, :], axis=1, keepdims=True)
                dq_in = dq_in + da * ek
                dk_in = dk_in + jnp.where(rowid == s, jnp.sum(da * e * qc, axis=0, keepdims=True), 0.0)
                dv = dv + jnp.where(rowid == s, jnp.sum(a * doc, axis=0, keepdims=True), 0.0)
            ebl = jnp.exp(bl)
            ds_ref[...] = ebl * dst + lax.dot_general(dob, qe.astype(BF16), (((0,), (0,)), ((), ())),
                                                      preferred_element_type=F32)
            dq_s[rows, :] = dq_st + dq_in
            dk_s[rows, :] = dk_st + dk_in
            dv_s[rows, :] = dv
            after_s[rows, :] = qc * (dq_st + dq_in) - kc * dk_in
            before_s[rows, :] = kc * dk_st
            thru_s[rows, :] = jnp.broadcast_to(ebl * jnp.sum(st * dst, axis=0, keepdims=True), (SUB, HEAD))
            return carry

        lax.fori_loop(0, ns, sub, 0)
        dg = _sub_suffix_prefix(after_s[...], before_s[...], tb) + thru_s[...]
        dhq_ref[...] = (dq_s[...] * _dsilu(hqv)).astype(BF16)
        dft = dg / f - dk_s[...]
        dhf_ref[...] = (dft * (1.0 - lb) * sig * (1.0 - sig)).astype(BF16)
        dlb_ref[0] += jnp.sum(dft * (1.0 - sig), axis=0, keepdims=True)
        dhi_ref[...] = dv_s[...].astype(BF16)

    blk = pl.BlockSpec((tb, HEAD), lambda h, j: (nb - 1 - j, h))
    vec = pl.BlockSpec((1, 1, HEAD), lambda h, j: (h, 0, 0))
    tok = jax.ShapeDtypeStruct((t, wdt), BF16)
    per_head = jax.ShapeDtypeStruct((nh, 1, HEAD), F32)
    return pl.pallas_call(
        body, name=name, grid=(nh, nb),
        in_specs=[blk] * 6 + [pl.BlockSpec((1, ns, HEAD, HEAD), lambda h, j: (h, nb - 1 - j, 0, 0)),
                              pl.BlockSpec((2, HEAD), lambda h, j: (0, h)), pl.BlockSpec((1, HEAD), lambda h, j: (0, 0))],
        out_specs=(blk, blk, blk, blk, vec, vec),
        out_shape=(tok, tok, tok, tok, per_head, per_head),
        scratch_shapes=[pltpu.VMEM((HEAD, HEAD), F32)] + [pltpu.VMEM((tb, HEAD), F32)] * 10,
        compiler_params=_params("arbitrary", "arbitrary"),
    )(hq, hf, hi, hg, o_raw, dyb, states, logits, out_norm)


def _lb_logits_grad(logits, dlb, *, name):
    def body(lg_ref, d_ref, o_ref):
        lg = lg_ref[...]
        e = jnp.exp(lg - jnp.max(lg, axis=0, keepdims=True))
        p = e / jnp.sum(e, axis=0, keepdims=True)
        d = d_ref[...]
        rowid = lax.broadcasted_iota(jnp.int32, lg.shape, 0)
        dp = jnp.where(rowid == 0, d, 0.0)
        o_ref[...] = p * (dp - jnp.sum(p * dp, axis=0, keepdims=True))

    return pl.pallas_call(body, name=name, out_shape=jax.ShapeDtypeStruct(logits.shape, F32))(logits, dlb)


def _adamw(w, g, m, v, *, name):
    r, c = w.shape
    tc = _pick(c, 2048) if c % LANE == 0 else c
    tr = _row_tile(r, tc * 4)

    def body(w_ref, g_ref, m_ref, v_ref, d_ref, nm_ref, nv_ref):
        gv = g_ref[...]
        nm = ADAM_B1 * m_ref[...] + (1.0 - ADAM_B1) * gv
        nv = ADAM_B2 * v_ref[...] + (1.0 - ADAM_B2) * (gv * gv)
        m_hat = nm / (1.0 - ADAM_B1 ** ADAM_STEP)
        v_hat = nv / (1.0 - ADAM_B2 ** ADAM_STEP)
        d_ref[...] = -ADAM_LR * (m_hat / (jnp.sqrt(v_hat) + ADAM_EPS) + ADAM_WD * w_ref[...])
        nm_ref[...] = nm
        nv_ref[...] = nv

    spec = pl.BlockSpec((tr, tc), lambda i, j: (i, j))
    shp = jax.ShapeDtypeStruct((r, c), F32)
    return pl.pallas_call(
        body, name=name, grid=(r // tr, c // tc), in_specs=[spec] * 4, out_specs=[spec] * 3,
        out_shape=[shp, shp, shp], compiler_params=_params("parallel", "parallel"),
    )(w, g, m, v)


def _coords():
    return lax.axis_index("x"), lax.axis_index("y"), lax.axis_index("c")


def _other_chips(x, y):
    return [(1 - x, y), (x, 1 - y), (1 - x, 1 - y)]


ANY = pl.BlockSpec(memory_space=pl.ANY)


def _gather_weights(flat, *, name):
    def body(src, out, send_sems, recv_sems, fsend_sems, frecv_sems, local_sem):
        x, y, c = _coords()
        me = 2 * x + y
        chips = _other_chips(x, y)
        local = pltpu.make_async_copy(src, out.at[me], local_sem)
        local.start()

        def ici(j, ch, to):
            return pltpu.make_async_remote_copy(
                src_ref=src.at[c, ch], dst_ref=out.at[me, c, ch], send_sem=send_sems.at[j, ch],
                recv_sem=recv_sems.at[j, ch], device_id=to, device_id_type=MESH)

        def landed(j, ch, half):
            px, py = chips[j]
            return out.at[2 * px + py, half, ch]

        def arrive(j, ch):
            return pltpu.make_async_remote_copy(
                src_ref=src.at[c, ch], dst_ref=landed(j, ch, c), send_sem=send_sems.at[j, ch],
                recv_sem=recv_sems.at[j, ch], device_id=(x, y, c), device_id_type=MESH)

        def forward(j, ch, half):
            return pltpu.make_async_remote_copy(
                src_ref=landed(j, ch, half), dst_ref=landed(j, ch, half), send_sem=fsend_sems.at[j, ch],
                recv_sem=frecv_sems.at[j, ch], device_id=(x, y, 1 - c), device_id_type=MESH)

        sends = [ici(j, ch, (*chips[j], c)) for ch in range(N_CHUNK) for j in range(3)]
        for cp in sends:
            cp.start()
        for ch in range(N_CHUNK):
            for j in range(3):
                arrive(j, ch).wait_recv()
                forward(j, ch, c).start()
        for ch in range(N_CHUNK):
            for j in range(3):
                forward(j, ch, 1 - c).wait_recv()
        for ch in range(N_CHUNK):
            for j in range(3):
                forward(j, ch, c).wait_send()
        for cp in sends:
            cp.wait_send()
        local.wait()

    sem = pltpu.SemaphoreType.DMA((3, N_CHUNK))
    return pl.pallas_call(
        body, name=name, in_specs=[ANY], out_specs=ANY,
        out_shape=jax.ShapeDtypeStruct((4,) + flat.shape, flat.dtype),
        scratch_shapes=[sem, sem, sem, sem, pltpu.SemaphoreType.DMA(())],
    )(flat)


def _swap_with_sibling(parts, *, name):
    n = parts.shape[0]

    def body(src, out, send_sems, recv_sems):
        x, y, c = _coords()
        copies = [pltpu.make_async_remote_copy(
            src_ref=src.at[k, 1 - c], dst_ref=out.at[k], send_sem=send_sems.at[k], recv_sem=recv_sems.at[k],
            device_id=(x, y, 1 - c), device_id_type=MESH) for k in range(n)]
        for cp in copies:
            cp.start()
        for cp in copies:
            cp.wait()

    return pl.pallas_call(
        body, name=name, in_specs=[ANY], out_specs=ANY,
        out_shape=jax.ShapeDtypeStruct((n,) + parts.shape[2:], parts.dtype),
        scratch_shapes=[pltpu.SemaphoreType.DMA((n,)), pltpu.SemaphoreType.DMA((n,))],
    )(parts)


def _exchange_chips(parts, *, name):
    def body(src, out, send_sems, recv_sems):
        x, y, c = _coords()
        copies = [pltpu.make_async_remote_copy(
            src_ref=src.at[2 * px + py], dst_ref=out.at[j], send_sem=send_sems.at[j], recv_sem=recv_sems.at[j],
            device_id=(px, py, c), device_id_type=MESH) for j, (px, py) in enumerate(_other_chips(x, y))]
        for cp in copies:
            cp.start()
        for cp in copies:
            cp.wait()

    return pl.pallas_call(
        body, name=name, in_specs=[ANY], out_specs=ANY,
        out_shape=jax.ShapeDtypeStruct((3,) + parts.shape[1:], parts.dtype),
        scratch_shapes=[pltpu.SemaphoreType.DMA((3,)), pltpu.SemaphoreType.DMA((3,))],
    )(parts)


def _join_halves(half, *, name):
    def body(src, out, send_sem, recv_sem, local_sem):
        x, y, c = _coords()
        local = pltpu.make_async_copy(src, out.at[c], local_sem)
        local.start()
        cp = pltpu.make_async_remote_copy(src_ref=src, dst_ref=out.at[c], send_sem=send_sem, recv_sem=recv_sem,
                                          device_id=(x, y, 1 - c), device_id_type=MESH)
        cp.start()
        cp.wait()
        local.wait()

    return pl.pallas_call(
        body, name=name, in_specs=[ANY], out_specs=ANY,
        out_shape=jax.ShapeDtypeStruct((2,) + half.shape, half.dtype),
        scratch_shapes=[pltpu.SemaphoreType.DMA(()), pltpu.SemaphoreType.DMA(()), pltpu.SemaphoreType.DMA(())],
    )(half)


def _add_half(parts, recv, sel, *, name):
    n, _, rows, wdt = parts.shape
    tr = _row_tile(rows, wdt * 4)

    def body(sel_ref, p_ref, r_ref, o_ref):
        o_ref[...] = p_ref[0] + r_ref[...]

    return pl.pallas_call(
        body, name=name, out_shape=jax.ShapeDtypeStruct((n, rows, wdt), F32),
        grid_spec=pltpu.PrefetchScalarGridSpec(
            num_scalar_prefetch=1, grid=(n, rows // tr),
            in_specs=[pl.BlockSpec((1, 1, tr, wdt), lambda k, i, s: (k, s[0], i, 0)),
                      pl.BlockSpec((1, tr, wdt), lambda k, i, s: (k, i, 0))],
            out_specs=pl.BlockSpec((1, tr, wdt), lambda k, i, s: (k, i, 0))),
        compiler_params=_params("parallel", "parallel"),
    )(sel, parts, recv)


def _add_chips(parts, recv, sel, *, name):
    _, rows, wdt = parts.shape
    tr = _row_tile(rows, wdt * 4)

    def body(sel_ref, p_ref, r_ref, o_ref):
        o_ref[...] = ((p_ref[0] + r_ref[0]) + r_ref[1]) + r_ref[2]

    return pl.pallas_call(
        body, name=name, out_shape=jax.ShapeDtypeStruct((rows, wdt), F32),
        grid_spec=pltpu.PrefetchScalarGridSpec(
            num_scalar_prefetch=1, grid=(rows // tr,),
            in_specs=[pl.BlockSpec((1, tr, wdt), lambda i, s: (s[0], i, 0)),
                      pl.BlockSpec((3, tr, wdt), lambda i, s: (0, i, 0))],
            out_specs=pl.BlockSpec((tr, wdt), lambda i, s: (i, 0))),
        compiler_params=_params("parallel"),
    )(sel, parts, recv)


def _all_reduce_small(vec, *, name):
    n = vec.shape[1]

    def body(v_ref, o_ref, buf, send_sems, recv_sems):
        x, y, c = _coords()
        me = 4 * x + 2 * y + c
        buf[me] = v_ref[...]
        copies = []
        for m in range(1, 8):
            peer = (x ^ ((m >> 2) & 1), y ^ ((m >> 1) & 1), c ^ (m & 1))
            copies.append(pltpu.make_async_remote_copy(
                src_ref=v_ref, dst_ref=buf.at[me], send_sem=send_sems.at[m - 1], recv_sem=recv_sems.at[m - 1],
                device_id=peer, device_id_type=MESH))
        for cp in copies:
            cp.start()
        for cp in copies:
            cp.wait()
        acc = buf[0]
        for d in range(1, 8):
            acc = acc + buf[d]
        o_ref[...] = acc

    return pl.pallas_call(
        body, name=name, out_shape=jax.ShapeDtypeStruct((1, n), F32),
        in_specs=[pl.BlockSpec(memory_space=pltpu.VMEM)], out_specs=pl.BlockSpec(memory_space=pltpu.VMEM),
        scratch_shapes=[pltpu.VMEM((8, 1, n), F32), pltpu.SemaphoreType.DMA((7,)), pltpu.SemaphoreType.DMA((7,))],
    )(vec)


def _pack_rows(total):
    assert total % (2 * N_CHUNK * PACK_W) == 0, total
    return total // (2 * N_CHUNK * PACK_W)


def _unshard(name, slots):
    parts = [slots[k] for k in range(4)]
    return jnp.concatenate(parts, axis=1 if name in COL_SHARDED else 0)


def _shard_of(name, full, k):
    if name in COL_SHARDED:
        n = full.shape[1] // 4
        return full[:, k * n:(k + 1) * n]
    n = full.shape[0] // 4
    return full[k * n:(k + 1) * n, :]


def _ffn_fwd(x, n_pre, wg, wu, wd, n_post, tag):
    h = _norm_fwd(x, n_pre, name=f"{tag}_norm_pre", out_dtype=BF16)
    g = _mm([(h, wg)], name=f"{tag}_gate")
    u = _mm([(h, wu)], name=f"{tag}_up")
    a = _swiglu_fwd(g, u, name=f"{tag}_swiglu")
    yv = _mm([(a, wd)], name=f"{tag}_down")
    out = _norm_fwd(yv, n_post, name=f"{tag}_norm_post", resid=x, scale=MACARON_SCALE)
    return out, (x, h, g, u, a, yv)


def _ffn_bwd(dout, saved, n_pre, wg, wu, wd, n_post, tag):
    x, h, g, u, a, yv = saved
    dy, dn_post = _norm_bwd(yv, n_post, dout, name=f"{tag}_norm_post_bwd", scale=MACARON_SCALE)
    da = _mm([(dy, wd)], name=f"{tag}_down_dx", trans_b=True)
    dwd = _mm([(a.T, dy)], name=f"{tag}_down_dw")
    dg, du = _swiglu_bwd(da, g, u, name=f"{tag}_swiglu_bwd")
    dh = _mm([(dg, wg), (du, wu)], name=f"{tag}_up_dx", trans_b=True)
    ht = h.T
    dwg = _mm([(ht, dg)], name=f"{tag}_gate_dw")
    dwu = _mm([(ht, du)], name=f"{tag}_up_dw")
    dx, dn_pre = _norm_bwd(x, n_pre, dh, name=f"{tag}_norm_pre_bwd", dres=dout)
    return dx, dn_pre, dwg, dwu, dwd, dn_post


def _rope_tables(positions):
    half = ROPE // 2
    inv_freq = ROPE_THETA ** (-jnp.arange(half, dtype=F32) / half)
    ang = positions.astype(F32)[:, None] * inv_freq
    cos, sin = jnp.cos(ang), jnp.sin(ang)
    z = jnp.zeros_like(cos)
    z2 = jnp.zeros((positions.shape[0], LANE - ROPE), F32)
    return (jnp.concatenate([cos, cos, z2], axis=1), jnp.concatenate([-sin, z, z2], axis=1),
            jnp.concatenate([z, sin, z2], axis=1))


def kernel(x, positions, ffn1_norm_pre, ffn1_w_gate, ffn1_w_up, ffn1_w_down, ffn1_norm_post, mix_norm_pre, w_in, mla_q_norm, mla_w_q_up, mla_kv_norm, mla_w_kv_up, mla_w_o, hgrn_lb_logits, hgrn_out_norm, hgrn_w_o, w_out, mix_norm_post, ffn2_norm_pre, ffn2_w_gate, ffn2_w_up, ffn2_w_down, ffn2_norm_post, loss_target, m_ffn1_norm_pre, m_ffn1_w_gate, m_ffn1_w_up, m_ffn1_w_down, m_ffn1_norm_post, m_mix_norm_pre, m_w_in, m_mla_q_norm, m_mla_w_q_up, m_mla_kv_norm, m_mla_w_kv_up, m_mla_w_o, m_hgrn_lb_logits, m_hgrn_out_norm, m_hgrn_w_o, m_w_out, m_mix_norm_post, m_ffn2_norm_pre, m_ffn2_w_gate, m_ffn2_w_up, m_ffn2_w_down, m_ffn2_norm_post, v_ffn1_norm_pre, v_ffn1_w_gate, v_ffn1_w_up, v_ffn1_w_down, v_ffn1_norm_post, v_mix_norm_pre, v_w_in, v_mla_q_norm, v_mla_w_q_up, v_mla_kv_norm, v_mla_w_kv_up, v_mla_w_o, v_hgrn_lb_logits, v_hgrn_out_norm, v_hgrn_w_o, v_w_out, v_mix_norm_post, v_ffn2_norm_pre, v_ffn2_w_gate, v_ffn2_w_up, v_ffn2_w_down, v_ffn2_norm_post):
    given = dict(locals())
    wts = {n: given[n] for n in ALL_WEIGHTS}
    mom = {n: given["m_" + n] for n in ALL_WEIGHTS}
    var = {n: given["v_" + n] for n in ALL_WEIGHTS}
    xin = x[0]
    target = loss_target[0]
    t, d = xin.shape
    cx, cy, cc = _coords()

    shard_shapes = {n: wts[n].shape[1:] for n in BIG_WEIGHTS}
    sizes = {n: shard_shapes[n][0] * shard_shapes[n][1] for n in BIG_WEIGHTS}
    total = sum(sizes.values())
    rows = _pack_rows(total)
    flat = jnp.concatenate([wts[n][0].astype(BF16).reshape(-1) for n in BIG_WEIGHTS])
    gathered = _gather_weights(flat.reshape(2, N_CHUNK, rows, PACK_W), name="gather_weights").reshape(4, total)
    full, off = {}, 0
    for n in BIG_WEIGHTS:
        full[n] = _unshard(n, gathered[:, off:off + sizes[n]].reshape((4,) + shard_shapes[n]))
        off += sizes[n]

    q_lora = mla_q_norm.shape[1]
    kv_lora = mla_kv_norm.shape[1]
    nh_mla = full["mla_w_kv_up"].shape[1] // QGROUP
    rope_at = q_lora + kv_lora
    w_in_full = full["w_in"]
    w_in_p = jnp.concatenate([w_in_full[:, :rope_at + ROPE], jnp.zeros((d, LANE - ROPE), BF16), w_in_full[:, rope_at + ROPE:]], axis=1)
    w_q_p = jnp.pad(full["mla_w_q_up"].reshape(q_lora, nh_mla, HEAD + ROPE), ((0, 0), (0, 0), (0, QGROUP - HEAD - ROPE))
                    ).reshape(q_lora, nh_mla * QGROUP)
    wkh = full["hgrn_w_o"].shape[0]
    tabs = _rope_tables(positions[0])
    tabs_bwd = tabs
    scale = (HEAD + ROPE) ** -0.5

    x1, saved1 = _ffn_fwd(xin, ffn1_norm_pre, full["ffn1_w_gate"], full["ffn1_w_up"], full["ffn1_w_down"], ffn1_norm_post, "ffn1")

    h2 = _norm_fwd(x1, mix_norm_pre, name="mix_norm_pre", out_dtype=BF16)
    proj = _mm([(h2, w_in_p)], name="mix_in")
    o0 = rope_at + LANE
    c_q, c_kv, k_r = proj[:, :q_lora], proj[:, q_lora:rope_at], proj[:, rope_at:o0]
    hq, hf, hi, hg = (proj[:, o0 + i * wkh:o0 + (i + 1) * wkh] for i in range(4))
    o1 = o0 + 4 * wkh
    gate_a, gate_b = proj[:, o1:o1 + d], proj[:, o1 + d:o1 + 2 * d]

    cqn = _norm_fwd(c_q, mla_q_norm, name="mla_q_norm", out_dtype=BF16)
    ckvn = _norm_fwd(c_kv, mla_kv_norm, name="mla_kv_norm", out_dtype=BF16)
    qp = _mm([(cqn, w_q_p)], name="mla_q_up")
    kvb = _mm([(ckvn, full["mla_w_kv_up"])], name="mla_kv_up", out_dtype=BF16)
    qcat = _rope(qp, tabs, name="rope_q", group=QGROUP, backward=False, out_dtype=BF16)
    krot = _rope(k_r, tabs, name="rope_k", group=LANE, backward=False, out_dtype=BF16)
    o_mla = _attn_fwd(qcat, kvb, krot, name="mla_attention", scale=scale)
    y_a = _mm([(o_mla, full["mla_w_o"])], name="mla_out")

    o_raw, yb, states = _hgrn_fwd(hq, hf, hi, hg, hgrn_lb_logits, hgrn_out_norm, name="hgrn_scan")
    y_b = _mm([(yb, full["hgrn_w_o"])], name="hgrn_out")

    merged = _merge_fwd(gate_a, gate_b, y_a, y_b, name="mix_merge")
    y_mix = _mm([(merged, full["w_out"])], name="mix_out")
    x2 = _norm_fwd(y_mix, mix_norm_post, name="mix_norm_post", resid=x1, scale=1.0)

    x3, saved2 = _ffn_fwd(x2, ffn2_norm_pre, full["ffn2_w_gate"], full["ffn2_w_up"], full["ffn2_w_down"], ffn2_norm_post, "ffn2")
    dx3, loss_local = _loss_head(x3, target, name="loss_head")

    grads = {}
    dx2, grads["ffn2_norm_pre"], grads["ffn2_w_gate"], grads["ffn2_w_up"], grads["ffn2_w_down"], grads["ffn2_norm_post"] = _ffn_bwd(
        dx3, saved2, ffn2_norm_pre, full["ffn2_w_gate"], full["ffn2_w_up"], full["ffn2_w_down"], ffn2_norm_post, "ffn2")

    dy_mix, grads["mix_norm_post"] = _norm_bwd(y_mix, mix_norm_post, dx2, name="mix_norm_post_bwd")
    dmerged = _mm([(dy_mix, full["w_out"])], name="mix_out_dx", trans_b=True)
    grads["w_out"] = _mm([(merged.T, dy_mix)], name="mix_out_dw")
    dga, dgb, dy_a, dy_b = _merge_bwd(dmerged, gate_a, gate_b, y_a, y_b, name="mix_merge_bwd")

    do_mla = _mm([(dy_a, full["mla_w_o"])], name="mla_out_dx", trans_b=True)
    grads["mla_w_o"] = _mm([(o_mla.T, dy_a)], name="mla_out_dw")
    dqcat, dkv, dkr = _attn_bwd(qcat, kvb, krot, do_mla, name="mla_attention_bwd", scale=scale)
    dqp = _rope(dqcat, tabs_bwd, name="rope_q_bwd", group=QGROUP, backward=True, out_dtype=BF16)
    dk_r = _rope(dkr, tabs_bwd, name="rope_k_bwd", group=LANE, backward=True, out_dtype=BF16)
    dcqn = _mm([(dqp, w_q_p)], name="mla_q_up_dx", trans_b=True)
    dwq_p = _mm([(cqn.T, dqp)], name="mla_q_up_dw")
    grads["mla_w_q_up"] = dwq_p.reshape(q_lora, nh_mla, QGROUP)[:, :, :HEAD + ROPE].reshape(q_lora, nh_mla * (HEAD + ROPE))
    dkvb = dkv.astype(BF16)
    dckvn = _mm([(dkvb, full["mla_w_kv_up"])], name="mla_kv_up_dx", trans_b=True)
    grads["mla_w_kv_up"] = _mm([(ckvn.T, dkvb)], name="mla_kv_up_dw")
    dc_q, grads["mla_q_norm"] = _norm_bwd(c_q, mla_q_norm, dcqn, name="mla_q_norm_bwd")
    dc_kv, grads["mla_kv_norm"] = _norm_bwd(c_kv, mla_kv_norm, dckvn, name="mla_kv_norm_bwd")

    dyb = _mm([(dy_b, full["hgrn_w_o"])], name="hgrn_out_dx", trans_b=True)
    grads["hgrn_w_o"] = _mm([(yb.T, dy_b)], name="hgrn_out_dw")
    dhq, dhf, dhi, dhg, dlb_h, dnorm_h = _hgrn_bwd(hq, hf, hi, hg, o_raw, dyb, states, hgrn_lb_logits, hgrn_out_norm, name="hgrn_scan_bwd")

    dproj = jnp.concatenate([dc_q.astype(BF16), dc_kv.astype(BF16), dk_r, dhq, dhf, dhi, dhg, dga, dgb], axis=1)
    dh2 = _mm([(dproj, w_in_p)], name="mix_in_dx", trans_b=True)
    dw_in_p = _mm([(h2.T, dproj)], name="mix_in_dw")
    grads["w_in"] = jnp.concatenate([dw_in_p[:, :rope_at + ROPE], dw_in_p[:, o0:]], axis=1)
    dx1, grads["mix_norm_pre"] = _norm_bwd(x1, mix_norm_pre, dh2, name="mix_norm_pre_bwd", dres=dx2)

    dx0, grads["ffn1_norm_pre"], grads["ffn1_w_gate"], grads["ffn1_w_up"], grads["ffn1_w_down"], grads["ffn1_norm_post"] = _ffn_bwd(
        dx1, saved1, ffn1_norm_pre, full["ffn1_w_gate"], full["ffn1_w_up"], full["ffn1_w_down"], ffn1_norm_post, "ffn1")

    gpack = jnp.stack([jnp.concatenate([_shard_of(n, grads[n], k).reshape(-1) for n in BIG_WEIGHTS]) for k in range(4)])
    half_rows = total // (2 * PACK_W)
    gpack = gpack.reshape(4, 2, half_rows, PACK_W)
    from_sibling = _swap_with_sibling(gpack, name="grad_swap_sibling")
    chip_sum = _add_half(gpack, from_sibling, jnp.reshape(cc, (1,)).astype(jnp.int32), name="grad_add_sibling")
    from_chips = _exchange_chips(chip_sum, name="grad_exchange_chips")
    my_half = _add_chips(chip_sum, from_chips, jnp.reshape(2 * cx + cy, (1,)).astype(jnp.int32), name="grad_add_chips")
    gflat = _join_halves(my_half, name="grad_join_halves").reshape(total)
    off = 0
    for n in BIG_WEIGHTS:
        grads[n] = gflat[off:off + sizes[n]].reshape(shard_shapes[n])
        off += sizes[n]

    dlb = dlb_h.reshape(1, -1)
    dnorm = jnp.sum(dnorm_h, axis=0)
    small = {**{n: grads[n] for n in SMALL_WEIGHTS if n not in ("hgrn_lb_logits", "hgrn_out_norm")},
             "hgrn_lb_logits": dlb, "hgrn_out_norm": dnorm}
    vec = jnp.concatenate([small[n] for n in SMALL_WEIGHTS], axis=1)
    vec = _all_reduce_small(vec, name="grad_all_reduce_small")
    off = 0
    for n in SMALL_WEIGHTS:
        w_n = small[n].shape[1]
        grads[n] = vec[:, off:off + w_n]
        off += w_n
    grads["hgrn_lb_logits"] = _lb_logits_grad(hgrn_lb_logits, grads["hgrn_lb_logits"], name="lb_logits_grad")

    deltas, new_m, new_v = {}, {}, {}
    for n in ALL_WEIGHTS:
        w_n = wts[n]
        shp = w_n.shape
        two_d = (lambda a: a[0]) if n in BIG_WEIGHTS else (lambda a: a)
        dl, nm, nv = _adamw(two_d(w_n), grads[n], two_d(mom[n]), two_d(var[n]), name=f"adamw_{n}")
        grads[n] = grads[n].reshape(shp)
        deltas[n], new_m[n], new_v[n] = dl.reshape(shp), nm.reshape(shp), nv.reshape(shp)

    loss = lax.psum(loss_local, ("x", "y", "c"))
    dx_out = dx0.reshape(x.shape)
    return (loss, dx_out, *[grads[n] for n in ALL_WEIGHTS], *[deltas[n] for n in ALL_WEIGHTS],
            *[new_m[n] for n in ALL_WEIGHTS], *[new_v[n] for n in ALL_WEIGHTS])
```

```python
import functools

import jax
import jax.numpy as jnp
from jax import lax
from jax.experimental import pallas as pl
from jax.experimental.pallas import tpu as pltpu

F32 = jnp.float32
BF16 = jnp.bfloat16
MESH = pl.DeviceIdType.MESH

NORM_EPS = 1e-6
MACARON_SCALE = 0.5
ROPE_THETA = 10000.0
HEAD = 128
ROPE = 64
QGROUP = 2 * HEAD
SUB = 16
ADAM_LR, ADAM_B1, ADAM_B2, ADAM_EPS, ADAM_WD, ADAM_STEP = 0.001, 0.9, 0.999, 1e-08, 0.01, 10

LANE = 128
VMEM_LIMIT = 48 * 1024 * 1024
PACK_W = 1024
N_CHUNK = 3

BIG_WEIGHTS = ("ffn1_w_gate", "ffn1_w_up", "ffn1_w_down", "w_in", "mla_w_q_up", "mla_w_kv_up",
               "mla_w_o", "hgrn_w_o", "w_out", "ffn2_w_gate", "ffn2_w_up", "ffn2_w_down")
COL_SHARDED = ("ffn1_w_gate", "ffn1_w_up", "w_in", "mla_w_q_up", "mla_w_kv_up", "ffn2_w_gate", "ffn2_w_up")
SMALL_WEIGHTS = ("ffn1_norm_pre", "ffn1_norm_post", "mix_norm_pre", "mla_q_norm", "mla_kv_norm",
                 "hgrn_lb_logits", "hgrn_out_norm", "mix_norm_post", "ffn2_norm_pre", "ffn2_norm_post")
ALL_WEIGHTS = ("ffn1_norm_pre", "ffn1_w_gate", "ffn1_w_up", "ffn1_w_down", "ffn1_norm_post", "mix_norm_pre",
               "w_in", "mla_q_norm", "mla_w_q_up", "mla_kv_norm", "mla_w_kv_up", "mla_w_o", "hgrn_lb_logits",
               "hgrn_out_norm", "hgrn_w_o", "w_out", "mix_norm_post", "ffn2_norm_pre", "ffn2_w_gate",
               "ffn2_w_up", "ffn2_w_down", "ffn2_norm_post")


def _params(*sem):
    return pltpu.CompilerParams(dimension_semantics=sem or None, vmem_limit_bytes=VMEM_LIMIT)


def _pick(n, cap, offset=0):
    if n <= cap and offset % n == 0:
        return n
    best = None
    for t in range(LANE, min(n, cap) + 1, LANE):
        if n % t == 0 and offset % t == 0:
            best = t
    assert best is not None, (n, cap, offset)
    return best


def _row_tile(n, row_bytes, budget=1 << 20):
    best = None
    for t in range(8, n + 1, 8):
        if n % t == 0 and t * row_bytes <= budget:
            best = t
    return n if best is None else best


def _sigmoid(x):
    return 1.0 / (1.0 + jnp.exp(-x))


def _silu(x):
    return x * _sigmoid(x)


def _dsilu(x):
    s = _sigmoid(x)
    return s * (1.0 + x * (1.0 - s))


def _mm(pairs, *, name, mode="nn", out_dtype=F32, into=None):
    def view(b):
        return b if isinstance(b, tuple) else (b, 0, b.shape[0])

    pairs = [(a, view(b)) for a, b in pairs]
    a0, (b0, b_off, b_rows) = pairs[0]
    if mode == "nn":
        (m, kdim), n = a0.shape, b0.shape[1]
    elif mode == "nt":
        (m, kdim), n = a0.shape, b_rows
    else:
        (kdim, m), n = a0.shape, b0.shape[1]
    out_off = 0 if into is None else into[1]
    tm = _pick(m, 1536 if mode == "tn" else 1024, out_off)
    tn = _pick(n, 1024 if mode == "tn" else 1536, b_off if mode == "nt" else 0)
    tk = _pick(kdim, 1024, b_off if mode == "nn" else 0)
    nk = kdim // tk
    npair = len(pairs)
    dims = {"nn": (((1,), (0,)), ((), ())), "nt": (((1,), (1,)), ((), ())), "tn": (((0,), (0,)), ((), ()))}[mode]

    def body(*refs):
        ins, o_ref, acc_ref = refs[:2 * npair], refs[-2], refs[-1]
        k = pl.program_id(2)

        @pl.when(k == 0)
        def _():
            acc_ref[...] = jnp.zeros_like(acc_ref)

        for p in range(npair):
            a = ins[2 * p][...].astype(BF16)
            b = ins[2 * p + 1][...].astype(BF16)
            acc_ref[...] += lax.dot_general(a, b, dims, preferred_element_type=F32)

        @pl.when(k == nk - 1)
        def _():
            o_ref[...] = acc_ref[...].astype(o_ref.dtype)

    a_spec = pl.BlockSpec((tk, tm), lambda i, j, k: (k, i)) if mode == "tn" else pl.BlockSpec((tm, tk), lambda i, j, k: (i, k))
    in_specs, flat = [], []
    for a, (b, off, _) in pairs:
        if mode == "nt":
            b_spec = pl.BlockSpec((tn, tk), lambda i, j, k, o=off // tn: (j + o, k))
        else:
            b_spec = pl.BlockSpec((tk, tn), lambda i, j, k, o=off // tk: (k + o, j))
        in_specs += [a_spec, b_spec]
        flat += [a, b]
    if into is None:
        out_shape, aliases = jax.ShapeDtypeStruct((m, n), out_dtype), {}
    else:
        out_shape, aliases = jax.ShapeDtypeStruct(into[0].shape, into[0].dtype), {len(flat): 0}
        in_specs.append(pl.BlockSpec(memory_space=pl.ANY))
        flat.append(into[0])
    return pl.pallas_call(
        body, name=name, grid=(m // tm, n // tn, nk),
        in_specs=in_specs,
        out_specs=pl.BlockSpec((tm, tn), lambda i, j, k, o=out_off // tm: (i + o, j)),
        out_shape=out_shape, input_output_aliases=aliases,
        scratch_shapes=[pltpu.VMEM((tm, tn), F32)],
        compiler_params=_params("parallel", "parallel", "arbitrary"),
    )(*flat)


def _norm_fwd(y, w, *, name, resid=None, scale=1.0, out_dtype=F32, col=0):
    t, d = y.shape[0], w.shape[1]
    tr = _pick(t, 256)
    assert col % d == 0

    def body(*refs):
        if resid is None:
            y_ref, w_ref, o_ref = refs
        else:
            y_ref, w_ref, r_ref, o_ref = refs
        yv = y_ref[...]
        out = yv * lax.rsqrt(jnp.mean(yv * yv, axis=-1, keepdims=True) + NORM_EPS) * w_ref[...]
        if resid is not None:
            out = r_ref[...] + scale * out
        o_ref[...] = out.astype(out_dtype)

    row = pl.BlockSpec((tr, d), lambda i: (i, 0))
    wspec = pl.BlockSpec((1, d), lambda i: (0, 0))
    ins, specs = [y, w], [pl.BlockSpec((tr, d), lambda i: (i, col // d)), wspec]
    if resid is not None:
        ins.append(resid)
        specs.append(row)
    return pl.pallas_call(
        body, name=name, grid=(t // tr,), in_specs=specs, out_specs=row,
        out_shape=jax.ShapeDtypeStruct((t, d), out_dtype), compiler_params=_params("parallel"),
    )(*ins)


def _norm_bwd(x, w, dy, *, name, scale=1.0, dres=None, col=0, dx_dtype=F32):
    t, d = x.shape[0], w.shape[1]
    tr = _pick(t, 256)
    assert col % d == 0

    def body(*refs):
        if dres is None:
            x_ref, w_ref, dy_ref, dx_ref, dw_ref = refs
        else:
            x_ref, w_ref, dy_ref, dr_ref, dx_ref, dw_ref = refs

        @pl.when(pl.program_id(0) == 0)
        def _():
            dw_ref[...] = jnp.zeros_like(dw_ref)

        xv = x_ref[...]
        r = lax.rsqrt(jnp.mean(xv * xv, axis=-1, keepdims=True) + NORM_EPS)
        xhat = xv * r
        dyv = dy_ref[...].astype(F32) * scale
        dw_ref[...] += jnp.sum(dyv * xhat, axis=0, keepdims=True)
        t_ = dyv * w_ref[...]
        dx = r * (t_ - xhat * jnp.mean(t_ * xhat, axis=-1, keepdims=True))
        if dres is not None:
            dx = dx + dr_ref[...]
        dx_ref[...] = dx.astype(dx_dtype)

    row = pl.BlockSpec((tr, d), lambda i: (i, 0))
    wspec = pl.BlockSpec((1, d), lambda i: (0, 0))
    ins, specs = [x, w, dy], [pl.BlockSpec((tr, d), lambda i: (i, col // d)), wspec, row]
    if dres is not None:
        ins.append(dres)
        specs.append(row)
    return pl.pallas_call(
        body, name=name, grid=(t // tr,), in_specs=specs, out_specs=(row, wspec),
        out_shape=(jax.ShapeDtypeStruct((t, d), dx_dtype), jax.ShapeDtypeStruct((1, d), F32)),
        compiler_params=_params("arbitrary"),
    )(*ins)


def _elementwise(fn, ins, out_dtypes, *, name, width=None, cols=None):
    t = ins[0].shape[0]
    d = ins[0].shape[1] if width is None else width
    cols = [0] * len(ins) if cols is None else cols
    tc = _pick(d, 2048)
    for c in cols:
        tc = _pick(d, tc, c)
    tr = _row_tile(t, tc * 4)
    nout = len(out_dtypes)

    def body(*refs):
        outs = fn(*[r[...].astype(F32) for r in refs[:len(ins)]])
        for o_ref, o in zip(refs[len(ins):], outs):
            o_ref[...] = o.astype(o_ref.dtype)

    spec = pl.BlockSpec((tr, tc), lambda i, j: (i, j))
    in_specs = [pl.BlockSpec((tr, tc), lambda i, j, o=c // tc: (i, j + o)) for c in cols]
    return pl.pallas_call(
        body, name=name, grid=(t // tr, d // tc), in_specs=in_specs, out_specs=[spec] * nout,
        out_shape=[jax.ShapeDtypeStruct((t, d), dt) for dt in out_dtypes],
        compiler_params=_params("parallel", "parallel"),
    )(*ins)


def _swiglu_fwd(g, u, *, name):
    return _elementwise(lambda gv, uv: (_silu(gv) * uv,), [g, u], [BF16], name=name)[0]


def _swiglu_bwd(da, g, u, *, name):
    return _elementwise(lambda dav, gv, uv: (dav * uv * _dsilu(gv), dav * _silu(gv)), [da, g, u], [BF16, BF16], name=name)


def _merge_fwd(proj, col_a, col_b, ya, yb, *, name):
    return _elementwise(lambda a, b, p, q: (_sigmoid(a) * p + _sigmoid(b) * q,), [proj, proj, ya, yb], [BF16],
                        name=name, width=ya.shape[1], cols=[col_a, col_b, 0, 0])[0]


def _merge_bwd(dm, proj, col_a, col_b, ya, yb, *, name):
    def fn(dmv, a, b, p, q):
        sa, sb = _sigmoid(a), _sigmoid(b)
        return dmv * p * sa * (1.0 - sa), dmv * q * sb * (1.0 - sb), dmv * sa, dmv * sb

    return _elementwise(fn, [dm, proj, proj, ya, yb], [BF16, BF16, BF16, BF16], name=name, width=ya.shape[1],
                        cols=[0, col_a, col_b, 0, 0])


def _loss_head(xo, target, *, name):
    t, d = xo.shape
    tr = _pick(t, 256)

    def body(x_ref, t_ref, dx_ref, l_ref):
        @pl.when(pl.program_id(0) == 0)
        def _():
            l_ref[...] = jnp.zeros_like(l_ref)

        err = x_ref[...] - t_ref[...]
        dx_ref[...] = err * (1.0 / d)
        l_ref[...] += 0.5 * jnp.sum(jnp.mean(err * err, axis=-1, keepdims=True), axis=0, keepdims=True)

    row = pl.BlockSpec((tr, d), lambda i: (i, 0))
    dx, l = pl.pallas_call(
        body, name=name, grid=(t // tr,), in_specs=[row, row],
        out_specs=(row, pl.BlockSpec((1, 1), lambda i: (0, 0))),
        out_shape=(jax.ShapeDtypeStruct((t, d), F32), jax.ShapeDtypeStruct((1, 1), F32)),
        compiler_params=_params("arbitrary"),
    )(xo, target)
    return dx, l[0, 0]


def _rope(xin, tabs, *, name, group, backward, out_dtype, col=0, ngroup=None):
    t = xin.shape[0]
    ngroup = xin.shape[1] // group if ngroup is None else ngroup
    wdt = ngroup * group
    tr = _pick(t, 256)
    assert col % group == 0
    cos_t, nsin_t, sin_t = tabs

    def body(x_ref, c_ref, n_ref, s_ref, o_ref):
        xv = x_ref[...].astype(F32)
        rot = xv[:, group - LANE:]
        if backward:
            out = rot * c_ref[...] + pltpu.roll(rot * n_ref[...], 32, 1) + pltpu.roll(rot * s_ref[...], LANE - 32, 1)
        else:
            out = rot * c_ref[...] + pltpu.roll(rot, LANE - 32, 1) * n_ref[...] + pltpu.roll(rot, 32, 1) * s_ref[...]
        if group > LANE:
            out = jnp.concatenate([xv[:, :group - LANE], out], axis=1)
        o_ref[...] = out.astype(out_dtype)

    xspec = pl.BlockSpec((tr, group), lambda i, g: (i, g))
    tspec = pl.BlockSpec((tr, LANE), lambda i, g: (i, 0))
    return pl.pallas_call(
        body, name=name, grid=(t // tr, ngroup),
        in_specs=[pl.BlockSpec((tr, group), lambda i, g: (i, g + col // group)), tspec, tspec, tspec], out_specs=xspec,
        out_shape=jax.ShapeDtypeStruct((t, wdt), out_dtype), compiler_params=_params("parallel", "parallel"),
    )(xin, cos_t, nsin_t, sin_t)


def _scores(q, kv, kr, qi, tq, scale):
    kcat = jnp.concatenate([kv[:, :HEAD], kr], axis=1)
    s = lax.dot_general(q, kcat, (((1,), (1,)), ((), ())), preferred_element_type=F32) * scale
    row = qi * tq + lax.broadcasted_iota(jnp.int32, s.shape, 0)
    col = lax.broadcasted_iota(jnp.int32, s.shape, 1)
    s = jnp.where(col <= row, s, -jnp.inf)
    p = jnp.exp(s - jnp.max(s, axis=-1, keepdims=True))
    return p / jnp.sum(p, axis=-1, keepdims=True), kcat


def _attn_fwd(qcat, kv, kr, *, name, scale):
    t = qcat.shape[0]
    nh = qcat.shape[1] // QGROUP
    tq = _pick(t, 256)

    def body(q_ref, kv_ref, kr_ref, o_ref):
        kvv = kv_ref[...]
        p, _ = _scores(q_ref[...], kvv, kr_ref[...], pl.program_id(1), tq, scale)
        o_ref[...] = jnp.dot(p.astype(BF16), kvv[:, HEAD:], preferred_element_type=F32).astype(BF16)

    return pl.pallas_call(
        body, name=name, grid=(nh, t // tq),
        in_specs=[pl.BlockSpec((tq, QGROUP), lambda h, i: (i, h)), pl.BlockSpec((t, QGROUP), lambda h, i: (0, h)),
                  pl.BlockSpec((t, LANE), lambda h, i: (0, 0))],
        out_specs=pl.BlockSpec((tq, HEAD), lambda h, i: (i, h)),
        out_shape=jax.ShapeDtypeStruct((t, nh * HEAD), BF16), compiler_params=_params("parallel", "parallel"),
    )(qcat, kv, kr)


def _attn_bwd(qcat, kv, kr, do, *, name, scale):
    t = qcat.shape[0]
    nh = qcat.shape[1] // QGROUP
    tq = _pick(t, 256)
    nq = t // tq

    def body(q_ref, kv_ref, kr_ref, do_ref, dq_ref, dkv_ref, dkr_ref, dk_acc, dv_acc):
        h, i = pl.program_id(0), pl.program_id(1)

        @pl.when(i == 0)
        def _():
            dk_acc[...] = jnp.zeros_like(dk_acc)
            dv_acc[...] = jnp.zeros_like(dv_acc)

        @pl.when((i == 0) & (h == 0))
        def _():
            dkr_ref[...] = jnp.zeros_like(dkr_ref)

        q = q_ref[...]
        kvv = kv_ref[...]
        dov = do_ref[...].astype(BF16)
        p, kcat = _scores(q, kvv, kr_ref[...], i, tq, scale)
        dp = lax.dot_general(dov, kvv[:, HEAD:], (((1,), (1,)), ((), ())), preferred_element_type=F32)
        ds = (p * (dp - jnp.sum(p * dp, axis=-1, keepdims=True)) * scale).astype(BF16)
        dq_ref[...] = jnp.dot(ds, kcat, preferred_element_type=F32)
        dk_acc[...] += lax.dot_general(ds, q, (((0,), (0,)), ((), ())), preferred_element_type=F32)
        dv_acc[...] += lax.dot_general(p.astype(BF16), dov, (((0,), (0,)), ((), ())), preferred_element_type=F32)

        @pl.when(i == nq - 1)
        def _():
            dk = dk_acc[...]
            dkv_ref[...] = jnp.concatenate([dk[:, :HEAD], dv_acc[...]], axis=1)
            dkr_ref[...] += dk[:, HEAD:]

    return pl.pallas_call(
        body, name=name, grid=(nh, nq),
        in_specs=[pl.BlockSpec((tq, QGROUP), lambda h, i: (i, h)), pl.BlockSpec((t, QGROUP), lambda h, i: (0, h)),
                  pl.BlockSpec((t, LANE), lambda h, i: (0, 0)), pl.BlockSpec((tq, HEAD), lambda h, i: (i, h))],
        out_specs=(pl.BlockSpec((tq, QGROUP), lambda h, i: (i, h)), pl.BlockSpec((t, QGROUP), lambda h, i: (0, h)),
                   pl.BlockSpec((t, LANE), lambda h, i: (0, 0))),
        out_shape=(jax.ShapeDtypeStruct((t, nh * QGROUP), F32), jax.ShapeDtypeStruct((t, nh * QGROUP), F32),
                   jax.ShapeDtypeStruct((t, LANE), F32)),
        scratch_shapes=[pltpu.VMEM((t, QGROUP), F32), pltpu.VMEM((t, HEAD), F32)],
        compiler_params=_params("arbitrary", "arbitrary"),
    )(qcat, kv, kr, do)


def _split3(x):
    hi = x.astype(BF16)
    r1 = x - hi.astype(F32)
    mid = r1.astype(BF16)
    lo = (r1 - mid.astype(F32)).astype(BF16)
    return hi, mid, lo


def _tri_matmul(mask, x):
    m = mask.astype(BF16)
    return sum(jnp.dot(m, part, preferred_element_type=F32) for part in _split3(x))


def _sub_cumsum(g, tb):
    row = lax.broadcasted_iota(jnp.int32, (tb, tb), 0)
    col = lax.broadcasted_iota(jnp.int32, (tb, tb), 1)
    return _tri_matmul(jnp.where((col <= row) & (col // SUB == row // SUB), 1.0, 0.0), g)


def _sub_suffix_prefix(after, before, tb):
    row = lax.broadcasted_iota(jnp.int32, (tb, tb), 0)
    col = lax.broadcasted_iota(jnp.int32, (tb, tb), 1)
    same = col // SUB == row // SUB
    return (_tri_matmul(jnp.where((col >= row) & same, 1.0, 0.0), after)
            + _tri_matmul(jnp.where((col < row) & same, 1.0, 0.0), before))


def _lower_bound(logits):
    mx = jnp.max(logits, axis=0, keepdims=True)
    e = jnp.exp(logits - mx)
    return e[0:1, :] / jnp.sum(e, axis=0, keepdims=True)


def _hgrn_fwd(proj, cols, wdt, logits, out_norm, *, name):
    t = proj.shape[0]
    nh = wdt // HEAD
    tb = _pick(t, 128)
    ns = tb // SUB

    def body(hq_ref, hf_ref, hi_ref, hg_ref, lg_ref, w_ref, o_ref, yb_ref, st_ref, s_ref, q_s, k_s, b_s):
        @pl.when(pl.program_id(1) == 0)
        def _():
            s_ref[...] = jnp.zeros_like(s_ref)

        lb = _lower_bound(lg_ref[...])
        f = lb + (1.0 - lb) * _sigmoid(hf_ref[...])
        q_s[...] = _silu(hq_ref[...])
        k_s[...] = 1.0 - f
        b_s[...] = _sub_cumsum(jnp.log(f), tb)
        rowid = lax.broadcasted_iota(jnp.int32, (SUB, HEAD), 0)

        def sub(c, carry):
            rows = pl.ds(pl.multiple_of(c * SUB, SUB), SUB)
            qc, kc, bc, vc = q_s[rows, :], k_s[rows, :], b_s[rows, :], hi_ref[rows, :]
            st = s_ref[...]
            st_ref[0, c] = st
            bl = bc[SUB - 1:SUB, :]
            oc = lax.dot_general((qc * jnp.exp(bc)).astype(BF16), st.astype(BF16), (((1,), (1,)), ((), ())),
                                 preferred_element_type=F32)
            for s in range(SUB):
                e = jnp.where(rowid >= s, jnp.exp(bc - bc[s:s + 1, :]), 0.0)
                a = jnp.sum(qc * e * kc[s:s + 1, :], axis=1, keepdims=True)
                oc = oc + a * vc[s:s + 1, :]
            o_ref[rows, :] = oc
            kd = kc * jnp.exp(bl - bc)
            s_ref[...] = jnp.exp(bl) * st + lax.dot_general(vc.astype(BF16), kd.astype(BF16), (((0,), (0,)), ((), ())),
                                                             preferred_element_type=F32)
            return carry

        lax.fori_loop(0, ns, sub, 0)
        o = o_ref[...]
        r = lax.rsqrt(jnp.mean(o * o, axis=-1, keepdims=True) + NORM_EPS)
        yb_ref[...] = (o * r * w_ref[...] * _silu(hg_ref[...])).astype(BF16)

    blk = pl.BlockSpec((tb, HEAD), lambda h, j: (j, h))
    return pl.pallas_call(
        body, name=name, grid=(nh, t // tb),
        in_specs=[pl.BlockSpec((tb, HEAD), lambda h, j, o=c // HEAD: (j, h + o)) for c in cols]
        + [pl.BlockSpec((2, HEAD), lambda h, j: (0, h)), pl.BlockSpec((1, HEAD), lambda h, j: (0, 0))],
        out_specs=(blk, blk, pl.BlockSpec((1, ns, HEAD, HEAD), lambda h, j: (h, j, 0, 0))),
        out_shape=(jax.ShapeDtypeStruct((t, wdt), F32), jax.ShapeDtypeStruct((t, wdt), BF16),
                   jax.ShapeDtypeStruct((nh, t // SUB, HEAD, HEAD), F32)),
        scratch_shapes=[pltpu.VMEM((HEAD, HEAD), F32)] + [pltpu.VMEM((tb, HEAD), F32)] * 3,
        compiler_params=_params("parallel", "arbitrary"),
    )(proj, proj, proj, proj, logits, out_norm)


def _hgrn_bwd(proj, cols, wdt, o_raw, dyb, states, logits, out_norm, *, name):
    t = proj.shape[0]
    nh = wdt // HEAD
    tb = _pick(t, 128)
    ns = tb // SUB
    nb = t // tb

    def body(hq_ref, hf_ref, hi_ref, hg_ref, o_ref, dy_ref, st_ref, lg_ref, w_ref,
             dhq_ref, dhf_ref, dhi_ref, dhg_ref, dlb_ref, dw_ref,
             ds_ref, q_s, k_s, b_s, do_s, dq_s, dk_s, dv_s, after_s, before_s, thru_s):
        @pl.when(pl.program_id(1) == 0)
        def _():
            ds_ref[...] = jnp.zeros_like(ds_ref)
            dlb_ref[...] = jnp.zeros_like(dlb_ref)
            dw_ref[...] = jnp.zeros_like(dw_ref)

        lb = _lower_bound(lg_ref[...])
        hqv, hgv = hq_ref[...], hg_ref[...]
        sig = _sigmoid(hf_ref[...])
        f = lb + (1.0 - lb) * sig
        q_s[...] = _silu(hqv)
        k_s[...] = 1.0 - f
        b_s[...] = _sub_cumsum(jnp.log(f), tb)

        o = o_ref[...]
        r = lax.rsqrt(jnp.mean(o * o, axis=-1, keepdims=True) + NORM_EPS)
        nrm = o * r
        w = w_ref[...]
        dy = dy_ref[...].astype(F32)
        dhg_ref[...] = (dy * nrm * w * _dsilu(hgv)).astype(BF16)
        dnw = dy * _silu(hgv)
        dw_ref[0] += jnp.sum(dnw * nrm, axis=0, keepdims=True)
        tt = dnw * w
        do_s[...] = r * (tt - nrm * jnp.mean(tt * nrm, axis=-1, keepdims=True))
        rowid = lax.broadcasted_iota(jnp.int32, (SUB, HEAD), 0)

        def sub(cc, carry):
            c = ns - 1 - cc
            rows = pl.ds(pl.multiple_of(c * SUB, SUB), SUB)
            qc, kc, bc, vc, doc = q_s[rows, :], k_s[rows, :], b_s[rows, :], hi_ref[rows, :], do_s[rows, :]
            st = st_ref[0, c]
            dst = ds_ref[...]
            bl = bc[SUB - 1:SUB, :]
            eb = jnp.exp(bc)
            ekd = jnp.exp(bl - bc)
            qe, kd = qc * eb, kc * ekd
            dob, vcb = doc.astype(BF16), vc.astype(BF16)
            dq_st = jnp.dot(dob, st.astype(BF16), preferred_element_type=F32) * eb
            dk_st = jnp.dot(vcb, dst.astype(BF16), preferred_element_type=F32) * ekd
            dv = lax.dot_general(kd.astype(BF16), dst.astype(BF16), (((1,), (1,)), ((), ())), preferred_element_type=F32)
            dq_in = jnp.zeros_like(qc)
            dk_in = jnp.zeros_like(qc)
            for s in range(SUB):
                e = jnp.where(rowid >= s, jnp.exp(bc - bc[s:s + 1, :]), 0.0)
                ek = e * kc[s:s + 1, :]
                a = jnp.sum(qc * ek, axis=1, keepdims=True)
                da = jnp.sum(doc * vc[s:s + 1, :], axis=1, keepdims=True)
                dq_in = dq_in + da * ek
                dk_in = dk_in + jnp.where(rowid == s, jnp.sum(da * e * qc, axis=0, keepdims=True), 0.0)
                dv = dv + jnp.where(rowid == s, jnp.sum(a * doc, axis=0, keepdims=True), 0.0)
            ebl = jnp.exp(bl)
            ds_ref[...] = ebl * dst + lax.dot_general(dob, qe.astype(BF16), (((0,), (0,)), ((), ())),
                                                      preferred_element_type=F32)
            dq_s[rows, :] = dq_st + dq_in
            dk_s[rows, :] = dk_st + dk_in
            dv_s[rows, :] = dv
            after_s[rows, :] = qc * (dq_st + dq_in) - kc * dk_in
            before_s[rows, :] = kc * dk_st
            thru_s[rows, :] = jnp.broadcast_to(ebl * jnp.sum(st * dst, axis=0, keepdims=True), (SUB, HEAD))
            return carry

        lax.fori_loop(0, ns, sub, 0)
        dg = _sub_suffix_prefix(after_s[...], before_s[...], tb) + thru_s[...]
        dhq_ref[...] = (dq_s[...] * _dsilu(hqv)).astype(BF16)
        dft = dg / f - dk_s[...]
        dhf_ref[...] = (dft * (1.0 - lb) * sig * (1.0 - sig)).astype(BF16)
        dlb_ref[0] += jnp.sum(dft * (1.0 - sig), axis=0, keepdims=True)
        dhi_ref[...] = dv_s[...].astype(BF16)

    blk = pl.BlockSpec((tb, HEAD), lambda h, j: (nb - 1 - j, h))
    vec = pl.BlockSpec((1, 1, HEAD), lambda h, j: (h, 0, 0))
    tok = jax.ShapeDtypeStruct((t, wdt), BF16)
    per_head = jax.ShapeDtypeStruct((nh, 1, HEAD), F32)
    return pl.pallas_call(
        body, name=name, grid=(nh, nb),
        in_specs=[pl.BlockSpec((tb, HEAD), lambda h, j, o=c // HEAD: (nb - 1 - j, h + o)) for c in cols]
        + [blk, blk] + [pl.BlockSpec((1, ns, HEAD, HEAD), lambda h, j: (h, nb - 1 - j, 0, 0)),
                              pl.BlockSpec((2, HEAD), lambda h, j: (0, h)), pl.BlockSpec((1, HEAD), lambda h, j: (0, 0))],
        out_specs=(blk, blk, blk, blk, vec, vec),
        out_shape=(tok, tok, tok, tok, per_head, per_head),
        scratch_shapes=[pltpu.VMEM((HEAD, HEAD), F32)] + [pltpu.VMEM((tb, HEAD), F32)] * 10,
        compiler_params=_params("arbitrary", "arbitrary"),
    )(proj, proj, proj, proj, o_raw, dyb, states, logits, out_norm)


def _lb_logits_grad(logits, dlb, *, name):
    def body(lg_ref, d_ref, o_ref):
        lg = lg_ref[...]
        e = jnp.exp(lg - jnp.max(lg, axis=0, keepdims=True))
        p = e / jnp.sum(e, axis=0, keepdims=True)
        d = d_ref[...]
        rowid = lax.broadcasted_iota(jnp.int32, lg.shape, 0)
        dp = jnp.where(rowid == 0, d, 0.0)
        o_ref[...] = p * (dp - jnp.sum(p * dp, axis=0, keepdims=True))

    return pl.pallas_call(body, name=name, out_shape=jax.ShapeDtypeStruct(logits.shape, F32))(logits, dlb)


def _adamw(w, g, m, v, *, name):
    r, c = w.shape
    tc = _pick(c, 2048) if c % LANE == 0 else c
    tr = _row_tile(r, tc * 4)

    def body(w_ref, g_ref, m_ref, v_ref, d_ref, nm_ref, nv_ref):
        gv = g_ref[...]
        nm = ADAM_B1 * m_ref[...] + (1.0 - ADAM_B1) * gv
        nv = ADAM_B2 * v_ref[...] + (1.0 - ADAM_B2) * (gv * gv)
        m_hat = nm / (1.0 - ADAM_B1 ** ADAM_STEP)
        v_hat = nv / (1.0 - ADAM_B2 ** ADAM_STEP)
        d_ref[...] = -ADAM_LR * (m_hat / (jnp.sqrt(v_hat) + ADAM_EPS) + ADAM_WD * w_ref[...])
        nm_ref[...] = nm
        nv_ref[...] = nv

    spec = pl.BlockSpec((tr, tc), lambda i, j: (i, j))
    shp = jax.ShapeDtypeStruct((r, c), F32)
    return pl.pallas_call(
        body, name=name, grid=(r // tr, c // tc), in_specs=[spec] * 4, out_specs=[spec] * 3,
        out_shape=[shp, shp, shp], compiler_params=_params("parallel", "parallel"),
    )(w, g, m, v)


def _coords():
    return lax.axis_index("x"), lax.axis_index("y"), lax.axis_index("c")


def _other_chips(x, y):
    return [(1 - x, y), (x, 1 - y), (1 - x, 1 - y)]


ANY = pl.BlockSpec(memory_space=pl.ANY)


class _Layout:
    def __init__(self, d, dff, in_cols, q_lora, kv_lora, nh):
        assert q_lora == kv_lora and nh % 4 == 0 and dff % 4 == 0 and in_cols % 4 == 0 and d % 4 == 0
        self.d, self.dff, self.q_lora, self.nh = d, dff, q_lora, nh
        self.head = q_lora + kv_lora + ROPE
        self.pad = d - self.head
        nff, ncol, r_o, hps = dff // 4, in_cols // 4, d // 4, nh // 4
        assert self.head <= ncol
        names = ("ffn1_w_gate", "ffn1_w_up", "ffn1_w_down", "ffn2_w_gate", "ffn2_w_up", "ffn2_w_down")
        self.off = {n: i * dff for i, n in enumerate(names)}
        self.off.update(w_in=6 * dff, mla_w_o=6 * dff + 7 * d, hgrn_w_o=6 * dff + 8 * d, w_out=6 * dff + 9 * d)
        self.rows_wide = 6 * dff + 10 * d
        self.off_q, self.off_kv, self.rows_narrow = 0, nh * QGROUP, 2 * nh * QGROUP
        self.loff = {n: i * nff for i, n in enumerate(names)}
        self.loff.update(w_in=6 * nff, mla_w_o=6 * nff + ncol, hgrn_w_o=6 * nff + ncol + r_o, w_out=6 * nff + ncol + 2 * r_o)
        self.lrows = {n: nff for n in names}
        self.lrows.update(w_in=ncol, mla_w_o=r_o, hgrn_w_o=r_o, w_out=r_o)
        self.lrows_wide = 6 * nff + ncol + 3 * r_o
        self.lrows_narrow = hps * (HEAD + ROPE) + hps * QGROUP
        self.loff_q, self.loff_kv = 0, hps * (HEAD + ROPE)
        self.nff, self.ncol, self.r_o, self.hps = nff, ncol, r_o, hps

    def segments(self, k):
        first = lambda a, b: jnp.where(k == 0, a, b) if not isinstance(k, int) else (a if k == 0 else b)
        segs = []
        for n in ("ffn1_w_gate", "ffn1_w_up", "ffn1_w_down"):
            segs.append((0, self.loff[n], self.nff, self.off[n] + self.nff * k, 0))
        w_in = self.off["w_in"]
        segs.append((0, self.loff["w_in"], self.head, w_in + first(0, self.ncol * k + self.pad), 1))
        segs.append((0, self.loff["w_in"] + self.head, self.ncol - self.head, w_in + self.ncol * k + self.d, 1))
        for n in ("mla_w_o", "hgrn_w_o", "w_out"):
            segs.append((0, self.loff[n], self.r_o, self.off[n] + self.r_o * k, 1))
        for hh in range(self.hps):
            segs.append((1, (HEAD + ROPE) * hh, HEAD + ROPE, QGROUP * (self.hps * k + hh), 2))
        segs.append((1, self.loff_kv, self.hps * QGROUP, self.off_kv + self.hps * QGROUP * k, 2))
        for n in ("ffn2_w_gate", "ffn2_w_up", "ffn2_w_down"):
            segs.append((0, self.loff[n], self.nff, self.off[n] + self.nff * k, 3))
        return segs

    def stream_rows(self, stream):
        return sum(s[2] for s in self.segments(0) if s[4] == stream)


N_STREAM = 4


def _half(ref, row, rows, half, width):
    return ref.at[pl.ds(row, rows), pl.ds(half * (width // 2), width // 2)]


def _wait_bytes_of(ref_like, send_sem, recv_sem, me, *, send):
    cp = pltpu.make_async_remote_copy(src_ref=ref_like, dst_ref=ref_like, send_sem=send_sem, recv_sem=recv_sem,
                                      device_id=me, device_id_type=MESH)
    if send:
        cp.wait_send()
    else:
        cp.wait_recv()


def _gather_weights(lwide, lnarrow, zwide, znarrow, lay, *, name):
    d, ql = lay.d, lay.q_lora

    def body(l0, l1, z0, z1, w0, w1, send, recv, fsend, frecv, lsem):
        x, y, c = _coords()
        me_chip = 2 * x + y
        chips = _other_chips(x, y)
        src, dst, width = (l0, l1), (w0, w1), (d, ql)

        for a, lrow, rows, drow, _ in lay.segments(me_chip):
            pltpu.make_async_copy(src[a].at[pl.ds(lrow, rows)], dst[a].at[pl.ds(drow, rows)], lsem.at[a]).start()
        pltpu.make_async_copy(z0, w0.at[pl.ds(lay.off["w_in"] + lay.head, lay.pad)], lsem.at[0]).start()
        for g in range(lay.nh):
            pltpu.make_async_copy(z1, w1.at[pl.ds(QGROUP * g + HEAD + ROPE, QGROUP - HEAD - ROPE)], lsem.at[1]).start()

        for j, (px, py) in enumerate(chips):
            for a, lrow, rows, drow, st in lay.segments(me_chip):
                pltpu.make_async_remote_copy(
                    src_ref=_half(src[a], lrow, rows, c, width[a]), dst_ref=_half(dst[a], drow, rows, c, width[a]),
                    send_sem=send.at[j, st], recv_sem=recv.at[j, st], device_id=(px, py, c), device_id_type=MESH).start()

        def total(st):
            a = 1 if st == 2 else 0
            return _half(dst[a], 0, lay.stream_rows(st), 0, width[a])

        for st in range(N_STREAM):
            for j, (px, py) in enumerate(chips):
                _wait_bytes_of(total(st), send.at[j, st], recv.at[j, st], (x, y, c), send=False)
                for a, lrow, rows, drow, s2 in lay.segments(2 * px + py):
                    if s2 == st:
                        blk = _half(dst[a], drow, rows, c, width[a])
                        pltpu.make_async_remote_copy(
                            src_ref=blk, dst_ref=blk, send_sem=fsend.at[j, st], recv_sem=frecv.at[j, st],
                            device_id=(x, y, 1 - c), device_id_type=MESH).start()
        for st in range(N_STREAM):
            for j in range(3):
                _wait_bytes_of(total(st), fsend.at[j, st], frecv.at[j, st], (x, y, c), send=False)
        for st in range(N_STREAM):
            for j in range(3):
                _wait_bytes_of(total(st), fsend.at[j, st], frecv.at[j, st], (x, y, c), send=True)
                _wait_bytes_of(total(st), send.at[j, st], recv.at[j, st], (x, y, c), send=True)
        own0 = w0.at[pl.ds(0, lay.lrows_wide + lay.pad)]
        pltpu.make_async_copy(own0, own0, lsem.at[0]).wait()
        own1 = w1.at[pl.ds(0, lay.lrows_narrow + lay.nh * (QGROUP - HEAD - ROPE))]
        pltpu.make_async_copy(own1, own1, lsem.at[1]).wait()

    sem = pltpu.SemaphoreType.DMA((3, N_STREAM))
    return pl.pallas_call(
        body, name=name, in_specs=[ANY] * 4, out_specs=[ANY] * 2,
        out_shape=[jax.ShapeDtypeStruct((lay.rows_wide, d), BF16), jax.ShapeDtypeStruct((lay.rows_narrow, ql), BF16)],
        scratch_shapes=[sem, sem, sem, sem, pltpu.SemaphoreType.DMA((2,))],
    )(lwide, lnarrow, zwide, znarrow)


def _swap_halves(gwide, gnarrow, *, name):
    def body(g0, g1, r0, r1, send_sems, recv_sems):
        x, y, c = _coords()
        copies = []
        for a, (g, r) in enumerate(((g0, r0), (g1, r1))):
            hw = g.shape[1] // 2
            copies.append(pltpu.make_async_remote_copy(
                src_ref=g.at[:, pl.ds((1 - c) * hw, hw)], dst_ref=r, send_sem=send_sems.at[a], recv_sem=recv_sems.at[a],
                device_id=(x, y, 1 - c), device_id_type=MESH))
        for cp in copies:
            cp.start()
        for cp in copies:
            cp.wait()

    return pl.pallas_call(
        body, name=name, in_specs=[ANY] * 2, out_specs=[ANY] * 2,
        out_shape=[jax.ShapeDtypeStruct((g.shape[0], g.shape[1] // 2), g.dtype) for g in (gwide, gnarrow)],
        scratch_shapes=[pltpu.SemaphoreType.DMA((2,)), pltpu.SemaphoreType.DMA((2,))],
    )(gwide, gnarrow)


def _add_sibling(g, recv, sel, *, name):
    rows, hw = recv.shape
    tr = _row_tile(rows, hw * 4)

    def body(sel_ref, g_ref, r_ref, o_ref):
        o_ref[...] = (g_ref[...] + r_ref[...]).astype(BF16)

    return pl.pallas_call(
        body, name=name, out_shape=jax.ShapeDtypeStruct((rows, hw), BF16),
        grid_spec=pltpu.PrefetchScalarGridSpec(
            num_scalar_prefetch=1, grid=(rows // tr,),
            in_specs=[pl.BlockSpec((tr, hw), lambda i, s: (i, s[0])), pl.BlockSpec((tr, hw), lambda i, s: (i, 0))],
            out_specs=pl.BlockSpec((tr, hw), lambda i, s: (i, 0))),
        compiler_params=_params("parallel"),
    )(sel, g, recv)


def _exchange_chips(swide, snarrow, lay, *, name):
    def body(s0, s1, r0, r1, send, recv, lsem):
        x, y, c = _coords()
        me_chip = 2 * x + y
        src, dst = (s0, s1), (r0, r1)
        for a, lrow, rows, drow, _ in lay.segments(me_chip):
            pltpu.make_async_copy(src[a].at[pl.ds(drow, rows)], dst[a].at[me_chip, pl.ds(lrow, rows)], lsem.at[a]).start()
        for j, (px, py) in enumerate(_other_chips(x, y)):
            for a, lrow, rows, drow, _ in lay.segments(2 * px + py):
                pltpu.make_async_remote_copy(
                    src_ref=src[a].at[pl.ds(drow, rows)], dst_ref=dst[a].at[me_chip, pl.ds(lrow, rows)],
                    send_sem=send.at[j, a], recv_sem=recv.at[j, a], device_id=(px, py, c), device_id_type=MESH).start()
        for a in range(2):
            for j in range(3):
                _wait_bytes_of(dst[a].at[0], send.at[j, a], recv.at[j, a], (x, y, c), send=False)
        for a in range(2):
            for j in range(3):
                _wait_bytes_of(dst[a].at[0], send.at[j, a], recv.at[j, a], (x, y, c), send=True)
            pltpu.make_async_copy(dst[a].at[0], dst[a].at[0], lsem.at[a]).wait()

    sem = pltpu.SemaphoreType.DMA((3, 2))
    return pl.pallas_call(
        body, name=name, in_specs=[ANY] * 2, out_specs=[ANY] * 2,
        out_shape=[jax.ShapeDtypeStruct((4, lay.lrows_wide, swide.shape[1]), BF16),
                   jax.ShapeDtypeStruct((4, lay.lrows_narrow, snarrow.shape[1]), BF16)],
        scratch_shapes=[sem, sem, pltpu.SemaphoreType.DMA((2,))],
    )(swide, snarrow)


def _add_chips(parts, *, name):
    _, rows, wdt = parts.shape
    tr = _row_tile(rows, wdt * 4)

    def body(p_ref, o_ref):
        o_ref[...] = ((p_ref[0].astype(F32) + p_ref[1].astype(F32)) + p_ref[2].astype(F32)) + p_ref[3].astype(F32)

    return pl.pallas_call(
        body, name=name, grid=(rows // tr,), in_specs=[pl.BlockSpec((4, tr, wdt), lambda i: (0, i, 0))],
        out_specs=pl.BlockSpec((tr, wdt), lambda i: (i, 0)), out_shape=jax.ShapeDtypeStruct((rows, wdt), F32),
        compiler_params=_params("parallel"),
    )(parts)


def _join_halves(hwide, hnarrow, *, name):
    def body(h0, h1, f0, f1, send_sems, recv_sems, lsem):
        x, y, c = _coords()
        copies, local = [], []
        for a, (h, f) in enumerate(((h0, f0), (h1, f1))):
            hw = h.shape[1]
            mine = f.at[:, pl.ds(c * hw, hw)]
            local.append(pltpu.make_async_copy(h, mine, lsem.at[a]))
            copies.append(pltpu.make_async_remote_copy(
                src_ref=h, dst_ref=mine, send_sem=send_sems.at[a], recv_sem=recv_sems.at[a],
                device_id=(x, y, 1 - c), device_id_type=MESH))
        for cp in local + copies:
            cp.start()
        for cp in copies + local:
            cp.wait()

    sem = pltpu.SemaphoreType.DMA((2,))
    return pl.pallas_call(
        body, name=name, in_specs=[ANY] * 2, out_specs=[ANY] * 2,
        out_shape=[jax.ShapeDtypeStruct((h.shape[0], 2 * h.shape[1]), h.dtype) for h in (hwide, hnarrow)],
        scratch_shapes=[sem, sem, sem],
    )(hwide, hnarrow)


def _all_reduce_small(vec, *, name):
    n = vec.shape[1]

    def body(v_ref, o_ref, buf, send_sems, recv_sems):
        x, y, c = _coords()
        me = 4 * x + 2 * y + c
        buf[me] = v_ref[...]
        copies = []
        for m in range(1, 8):
            peer = (x ^ ((m >> 2) & 1), y ^ ((m >> 1) & 1), c ^ (m & 1))
            copies.append(pltpu.make_async_remote_copy(
                src_ref=v_ref, dst_ref=buf.at[me], send_sem=send_sems.at[m - 1], recv_sem=recv_sems.at[m - 1],
                device_id=peer, device_id_type=MESH))
        for cp in copies:
            cp.start()
        for cp in copies:
            cp.wait()
        acc = buf[0]
        for d in range(1, 8):
            acc = acc + buf[d]
        o_ref[...] = acc

    return pl.pallas_call(
        body, name=name, out_shape=jax.ShapeDtypeStruct((1, n), F32),
        in_specs=[pl.BlockSpec(memory_space=pltpu.VMEM)], out_specs=pl.BlockSpec(memory_space=pltpu.VMEM),
        scratch_shapes=[pltpu.VMEM((8, 1, n), F32), pltpu.SemaphoreType.DMA((7,)), pltpu.SemaphoreType.DMA((7,))],
    )(vec)


def _ffn_fwd(x, n_pre, n_post, wide, lay, tag):
    wg, wu, wd = ((wide, lay.off[f"{tag}_w_{p}"], lay.dff) for p in ("gate", "up", "down"))
    h = _norm_fwd(x, n_pre, name=f"{tag}_norm_pre", out_dtype=BF16)
    g = _mm([(h, wg)], name=f"{tag}_gate", mode="nt")
    u = _mm([(h, wu)], name=f"{tag}_up", mode="nt")
    a = _swiglu_fwd(g, u, name=f"{tag}_swiglu")
    yv = _mm([(a, wd)], name=f"{tag}_down", mode="nn")
    out = _norm_fwd(yv, n_post, name=f"{tag}_norm_post", resid=x, scale=MACARON_SCALE)
    return out, (x, h, g, u, a, yv)


def _ffn_bwd(dout, saved, n_pre, n_post, wide, gwide, lay, tag):
    x, h, g, u, a, yv = saved
    og, ou, od = (lay.off[f"{tag}_w_{p}"] for p in ("gate", "up", "down"))
    dy, dn_post = _norm_bwd(yv, n_post, dout, name=f"{tag}_norm_post_bwd", scale=MACARON_SCALE)
    da = _mm([(dy, (wide, od, lay.dff))], name=f"{tag}_down_dx", mode="nt")
    gwide = _mm([(a, dy)], name=f"{tag}_down_dw", mode="tn", into=(gwide, od))
    dg, du = _swiglu_bwd(da, g, u, name=f"{tag}_swiglu_bwd")
    dh = _mm([(dg, (wide, og, lay.dff)), (du, (wide, ou, lay.dff))], name=f"{tag}_up_dx", mode="nn")
    gwide = _mm([(dg, h)], name=f"{tag}_gate_dw", mode="tn", into=(gwide, og))
    gwide = _mm([(du, h)], name=f"{tag}_up_dw", mode="tn", into=(gwide, ou))
    dx, dn_pre = _norm_bwd(x, n_pre, dh, name=f"{tag}_norm_pre_bwd", dres=dout)
    return dx, dn_pre, dn_post, gwide


def _rope_tables(positions):
    half = ROPE // 2
    inv_freq = ROPE_THETA ** (-jnp.arange(half, dtype=F32) / half)
    ang = positions.astype(F32)[:, None] * inv_freq
    cos, sin = jnp.cos(ang), jnp.sin(ang)
    z = jnp.zeros_like(cos)
    z2 = jnp.zeros((positions.shape[0], LANE - ROPE), F32)
    return (jnp.concatenate([cos, cos, z2], axis=1), jnp.concatenate([-sin, z, z2], axis=1),
            jnp.concatenate([z, sin, z2], axis=1))


def kernel(x, positions, ffn1_norm_pre, ffn1_w_gate, ffn1_w_up, ffn1_w_down, ffn1_norm_post, mix_norm_pre, w_in, mla_q_norm, mla_w_q_up, mla_kv_norm, mla_w_kv_up, mla_w_o, hgrn_lb_logits, hgrn_out_norm, hgrn_w_o, w_out, mix_norm_post, ffn2_norm_pre, ffn2_w_gate, ffn2_w_up, ffn2_w_down, ffn2_norm_post, loss_target, m_ffn1_norm_pre, m_ffn1_w_gate, m_ffn1_w_up, m_ffn1_w_down, m_ffn1_norm_post, m_mix_norm_pre, m_w_in, m_mla_q_norm, m_mla_w_q_up, m_mla_kv_norm, m_mla_w_kv_up, m_mla_w_o, m_hgrn_lb_logits, m_hgrn_out_norm, m_hgrn_w_o, m_w_out, m_mix_norm_post, m_ffn2_norm_pre, m_ffn2_w_gate, m_ffn2_w_up, m_ffn2_w_down, m_ffn2_norm_post, v_ffn1_norm_pre, v_ffn1_w_gate, v_ffn1_w_up, v_ffn1_w_down, v_ffn1_norm_post, v_mix_norm_pre, v_w_in, v_mla_q_norm, v_mla_w_q_up, v_mla_kv_norm, v_mla_w_kv_up, v_mla_w_o, v_hgrn_lb_logits, v_hgrn_out_norm, v_hgrn_w_o, v_w_out, v_mix_norm_post, v_ffn2_norm_pre, v_ffn2_w_gate, v_ffn2_w_up, v_ffn2_w_down, v_ffn2_norm_post):
    given = dict(locals())
    wts = {n: given[n] for n in ALL_WEIGHTS}
    mom = {n: given["m_" + n] for n in ALL_WEIGHTS}
    var = {n: given["v_" + n] for n in ALL_WEIGHTS}
    xin = x[0]
    target = loss_target[0]
    t, d = xin.shape
    cx, cy, cc = _coords()

    q_lora, kv_lora = mla_q_norm.shape[1], mla_kv_norm.shape[1]
    nh_mla = 4 * mla_w_kv_up.shape[2] // QGROUP
    lay = _Layout(d, 4 * ffn1_w_gate.shape[2], 4 * w_in.shape[2], q_lora, kv_lora, nh_mla)
    col_sharded = lambda n: wts[n][0].T.astype(BF16)
    row_sharded = lambda n: wts[n][0].astype(BF16)
    lwide = jnp.concatenate([col_sharded("ffn1_w_gate"), col_sharded("ffn1_w_up"), row_sharded("ffn1_w_down"),
                             col_sharded("ffn2_w_gate"), col_sharded("ffn2_w_up"), row_sharded("ffn2_w_down"),
                             col_sharded("w_in"), row_sharded("mla_w_o"), row_sharded("hgrn_w_o"), row_sharded("w_out")])
    lnarrow = jnp.concatenate([col_sharded("mla_w_q_up"), col_sharded("mla_w_kv_up")])
    wide, narrow = _gather_weights(lwide, lnarrow, jnp.zeros((lay.pad, d), BF16),
                                   jnp.zeros((QGROUP - HEAD - ROPE, q_lora), BF16), lay, name="gather_weights")
    w_in_v = (wide, lay.off["w_in"], 7 * d)
    w_q_v = (narrow, lay.off_q, nh_mla * QGROUP)
    w_kv_v = (narrow, lay.off_kv, nh_mla * QGROUP)
    w_o_v = {n: (wide, lay.off[n], d) for n in ("mla_w_o", "hgrn_w_o", "w_out")}
    col_kr = q_lora + kv_lora
    hgrn_cols = [d, 2 * d, 3 * d, 4 * d]
    col_ga, col_gb = 5 * d, 6 * d
    tabs = _rope_tables(positions[0])
    scale = (HEAD + ROPE) ** -0.5

    x1, saved1 = _ffn_fwd(xin, ffn1_norm_pre, ffn1_norm_post, wide, lay, "ffn1")

    h2 = _norm_fwd(x1, mix_norm_pre, name="mix_norm_pre", out_dtype=BF16)
    proj = _mm([(h2, w_in_v)], name="mix_in", mode="nt")
    cqn = _norm_fwd(proj, mla_q_norm, name="mla_q_norm", out_dtype=BF16, col=0)
    ckvn = _norm_fwd(proj, mla_kv_norm, name="mla_kv_norm", out_dtype=BF16, col=q_lora)
    qp = _mm([(cqn, w_q_v)], name="mla_q_up", mode="nt")
    kvb = _mm([(ckvn, w_kv_v)], name="mla_kv_up", mode="nt", out_dtype=BF16)
    qcat = _rope(qp, tabs, name="rope_q", group=QGROUP, backward=False, out_dtype=BF16)
    krot = _rope(proj, tabs, name="rope_k", group=LANE, backward=False, out_dtype=BF16, col=col_kr, ngroup=1)
    o_mla = _attn_fwd(qcat, kvb, krot, name="mla_attention", scale=scale)
    y_a = _mm([(o_mla, w_o_v["mla_w_o"])], name="mla_out", mode="nn")

    o_raw, yb, states = _hgrn_fwd(proj, hgrn_cols, d, hgrn_lb_logits, hgrn_out_norm, name="hgrn_scan")
    y_b = _mm([(yb, w_o_v["hgrn_w_o"])], name="hgrn_out", mode="nn")

    merged = _merge_fwd(proj, col_ga, col_gb, y_a, y_b, name="mix_merge")
    y_mix = _mm([(merged, w_o_v["w_out"])], name="mix_out", mode="nn")
    x2 = _norm_fwd(y_mix, mix_norm_post, name="mix_norm_post", resid=x1, scale=1.0)

    x3, saved2 = _ffn_fwd(x2, ffn2_norm_pre, ffn2_norm_post, wide, lay, "ffn2")
    dx3, loss_local = _loss_head(x3, target, name="loss_head")

    grads = {}
    gwide = lax.empty((lay.rows_wide, d), F32)
    gnarrow = lax.empty((lay.rows_narrow, q_lora), F32)
    dx2, grads["ffn2_norm_pre"], grads["ffn2_norm_post"], gwide = _ffn_bwd(
        dx3, saved2, ffn2_norm_pre, ffn2_norm_post, wide, gwide, lay, "ffn2")

    dy_mix, grads["mix_norm_post"] = _norm_bwd(y_mix, mix_norm_post, dx2, name="mix_norm_post_bwd")
    dmerged = _mm([(dy_mix, w_o_v["w_out"])], name="mix_out_dx", mode="nt")
    gwide = _mm([(merged, dy_mix)], name="mix_out_dw", mode="tn", into=(gwide, lay.off["w_out"]))
    dga, dgb, dy_a, dy_b = _merge_bwd(dmerged, proj, col_ga, col_gb, y_a, y_b, name="mix_merge_bwd")

    do_mla = _mm([(dy_a, w_o_v["mla_w_o"])], name="mla_out_dx", mode="nt")
    gwide = _mm([(o_mla, dy_a)], name="mla_out_dw", mode="tn", into=(gwide, lay.off["mla_w_o"]))
    dqcat, dkv, dkr = _attn_bwd(qcat, kvb, krot, do_mla, name="mla_attention_bwd", scale=scale)
    dqp = _rope(dqcat, tabs, name="rope_q_bwd", group=QGROUP, backward=True, out_dtype=BF16)
    dk_r = _rope(dkr, tabs, name="rope_k_bwd", group=LANE, backward=True, out_dtype=BF16)
    dcqn = _mm([(dqp, w_q_v)], name="mla_q_up_dx", mode="nn")
    gnarrow = _mm([(dqp, cqn)], name="mla_q_up_dw", mode="tn", into=(gnarrow, lay.off_q))
    dkvb = dkv.astype(BF16)
    dckvn = _mm([(dkvb, w_kv_v)], name="mla_kv_up_dx", mode="nn")
    gnarrow = _mm([(dkvb, ckvn)], name="mla_kv_up_dw", mode="tn", into=(gnarrow, lay.off_kv))
    dc_q, grads["mla_q_norm"] = _norm_bwd(proj, mla_q_norm, dcqn, name="mla_q_norm_bwd", col=0, dx_dtype=BF16)
    dc_kv, grads["mla_kv_norm"] = _norm_bwd(proj, mla_kv_norm, dckvn, name="mla_kv_norm_bwd", col=q_lora, dx_dtype=BF16)

    dyb = _mm([(dy_b, w_o_v["hgrn_w_o"])], name="hgrn_out_dx", mode="nt")
    gwide = _mm([(yb, dy_b)], name="hgrn_out_dw", mode="tn", into=(gwide, lay.off["hgrn_w_o"]))
    dhq, dhf, dhi, dhg, dlb_h, dnorm_h = _hgrn_bwd(proj, hgrn_cols, d, o_raw, dyb, states, hgrn_lb_logits, hgrn_out_norm,
                                                   name="hgrn_scan_bwd")

    dproj = jnp.concatenate([dc_q, dc_kv, dk_r, jnp.zeros((t, d - col_kr - LANE), BF16), dhq, dhf, dhi, dhg, dga, dgb], axis=1)
    dh2 = _mm([(dproj, w_in_v)], name="mix_in_dx", mode="nn")
    gwide = _mm([(dproj, h2)], name="mix_in_dw", mode="tn", into=(gwide, lay.off["w_in"]))
    dx1, grads["mix_norm_pre"] = _norm_bwd(x1, mix_norm_pre, dh2, name="mix_norm_pre_bwd", dres=dx2)

    dx0, grads["ffn1_norm_pre"], grads["ffn1_norm_post"], gwide = _ffn_bwd(
        dx1, saved1, ffn1_norm_pre, ffn1_norm_post, wide, gwide, lay, "ffn1")

    sel = jnp.reshape(cc, (1,)).astype(jnp.int32)
    rwide, rnarrow = _swap_halves(gwide, gnarrow, name="grad_swap_sibling")
    swide = _add_sibling(gwide, rwide, sel, name="grad_add_sibling_wide")
    snarrow = _add_sibling(gnarrow, rnarrow, sel, name="grad_add_sibling_narrow")
    pwide, pnarrow = _exchange_chips(swide, snarrow, lay, name="grad_exchange_chips")
    fwide, fnarrow = _join_halves(_add_chips(pwide, name="grad_add_chips_wide"),
                                  _add_chips(pnarrow, name="grad_add_chips_narrow"), name="grad_join_halves")
    for n in BIG_WEIGHTS:
        if n == "mla_w_q_up":
            g_n = fnarrow[lay.loff_q:lay.loff_kv]
        elif n == "mla_w_kv_up":
            g_n = fnarrow[lay.loff_kv:]
        else:
            g_n = fwide[lay.loff[n]:lay.loff[n] + lay.lrows[n]]
        grads[n] = g_n.T if n in COL_SHARDED else g_n

    dlb = dlb_h.reshape(1, -1)
    dnorm = jnp.sum(dnorm_h, axis=0)
    small = {**{n: grads[n] for n in SMALL_WEIGHTS if n not in ("hgrn_lb_logits", "hgrn_out_norm")},
             "hgrn_lb_logits": dlb, "hgrn_out_norm": dnorm}
    vec = jnp.concatenate([small[n] for n in SMALL_WEIGHTS], axis=1)
    vec = _all_reduce_small(vec, name="grad_all_reduce_small")
    off = 0
    for n in SMALL_WEIGHTS:
        w_n = small[n].shape[1]
        grads[n] = vec[:, off:off + w_n]
        off += w_n
    grads["hgrn_lb_logits"] = _lb_logits_grad(hgrn_lb_logits, grads["hgrn_lb_logits"], name="lb_logits_grad")

    deltas, new_m, new_v = {}, {}, {}
    for n in ALL_WEIGHTS:
        w_n = wts[n]
        shp = w_n.shape
        two_d = (lambda a: a[0]) if n in BIG_WEIGHTS else (lambda a: a)
        dl, nm, nv = _adamw(two_d(w_n), grads[n], two_d(mom[n]), two_d(var[n]), name=f"adamw_{n}")
        grads[n] = grads[n].reshape(shp)
        deltas[n], new_m[n], new_v[n] = dl.reshape(shp), nm.reshape(shp), nv.reshape(shp)

    loss = lax.psum(loss_local, ("x", "y", "c"))
    dx_out = dx0.reshape(x.shape)
    return (loss, dx_out, *[grads[n] for n in ALL_WEIGHTS], *[deltas[n] for n in ALL_WEIGHTS],
            *[new_m[n] for n in ALL_WEIGHTS], *[new_v[n] for n in ALL_WEIGHTS])
```

```python
import functools

import jax
import jax.numpy as jnp
from jax import lax
from jax.experimental import pallas as pl
from jax.experimental.pallas import tpu as pltpu

F32 = jnp.float32
BF16 = jnp.bfloat16
MESH = pl.DeviceIdType.MESH

NORM_EPS = 1e-6
MACARON_SCALE = 0.5
ROPE_THETA = 10000.0
HEAD = 128
ROPE = 64
QGROUP = 2 * HEAD
SUB = 16
ADAM_LR, ADAM_B1, ADAM_B2, ADAM_EPS, ADAM_WD, ADAM_STEP = 0.001, 0.9, 0.999, 1e-08, 0.01, 10

LANE = 128
VMEM_LIMIT = 48 * 1024 * 1024
MM_TILE = 1024

BIG_WEIGHTS = ("ffn1_w_gate", "ffn1_w_up", "ffn1_w_down", "w_in", "mla_w_q_up", "mla_w_kv_up",
               "mla_w_o", "hgrn_w_o", "w_out", "ffn2_w_gate", "ffn2_w_up", "ffn2_w_down")
COL_SHARDED = ("ffn1_w_gate", "ffn1_w_up", "w_in", "mla_w_q_up", "mla_w_kv_up", "ffn2_w_gate", "ffn2_w_up")
SMALL_WEIGHTS = ("ffn1_norm_pre", "ffn1_norm_post", "mix_norm_pre", "mla_q_norm", "mla_kv_norm",
                 "hgrn_lb_logits", "hgrn_out_norm", "mix_norm_post", "ffn2_norm_pre", "ffn2_norm_post")
ALL_WEIGHTS = ("ffn1_norm_pre", "ffn1_w_gate", "ffn1_w_up", "ffn1_w_down", "ffn1_norm_post", "mix_norm_pre",
               "w_in", "mla_q_norm", "mla_w_q_up", "mla_kv_norm", "mla_w_kv_up", "mla_w_o", "hgrn_lb_logits",
               "hgrn_out_norm", "hgrn_w_o", "w_out", "mix_norm_post", "ffn2_norm_pre", "ffn2_w_gate",
               "ffn2_w_up", "ffn2_w_down", "ffn2_norm_post")


def _params(*sem):
    return pltpu.CompilerParams(dimension_semantics=sem or None, vmem_limit_bytes=VMEM_LIMIT)


def _pick(n, cap, offset=0):
    if n <= cap and offset % n == 0:
        return n
    best = None
    for t in range(LANE, min(n, cap) + 1, LANE):
        if n % t == 0 and offset % t == 0:
            best = t
    assert best is not None, (n, cap, offset)
    return best


def _row_tile(n, row_bytes, budget=1 << 20):
    best = None
    for t in range(8, n + 1, 8):
        if n % t == 0 and t * row_bytes <= budget:
            best = t
    return n if best is None else best


def _sigmoid(x):
    return 1.0 / (1.0 + jnp.exp(-x))


def _silu(x):
    return x * _sigmoid(x)


def _dsilu(x):
    s = _sigmoid(x)
    return s * (1.0 + x * (1.0 - s))


def _mm(pairs, *, name, mode="nn", out_dtype=F32, into=None):
    halves = isinstance(pairs[0][1], tuple)
    assert halves or mode == "tn"
    pairs = [(a, b if halves else (b, 0, b.shape[0])) for a, b in pairs]
    a0, (b0, b_off, b_rows) = pairs[0]
    hw = b0.shape[2] if halves else (into[0].shape[2] if into is not None else None)
    if mode == "nn":
        (m, kdim), n = a0.shape, 2 * hw
    elif mode == "nt":
        (m, kdim), n = a0.shape, b_rows
        assert kdim == 2 * hw
    else:
        (kdim, m), n = a0.shape, b0.shape[1]
    out_off = 0 if into is None else into[1]
    tm = _pick(m, MM_TILE, out_off)
    tn = hw if (mode == "nn" or into is not None) else _pick(n, MM_TILE, b_off if mode == "nt" else 0)
    tk = hw if mode == "nt" else _pick(kdim, MM_TILE, b_off if mode == "nn" else 0)
    assert n % tn == 0 and kdim % tk == 0
    nk = kdim // tk
    npair = len(pairs)
    dims = {"nn": (((1,), (0,)), ((), ())), "nt": (((1,), (1,)), ((), ())), "tn": (((0,), (0,)), ((), ()))}[mode]

    def body(*refs):
        ins, o_ref, acc_ref = refs[:2 * npair], refs[-2], refs[-1]
        k = pl.program_id(2)

        @pl.when(k == 0)
        def _():
            acc_ref[...] = jnp.zeros_like(acc_ref)

        for p in range(npair):
            a = ins[2 * p][...].astype(BF16)
            b = ins[2 * p + 1][...].astype(BF16)
            acc_ref[...] += lax.dot_general(a, b, dims, preferred_element_type=F32)

        @pl.when(k == nk - 1)
        def _():
            o_ref[...] = acc_ref[...].astype(o_ref.dtype)

    a_spec = pl.BlockSpec((tk, tm), lambda i, j, k: (k, i)) if mode == "tn" else pl.BlockSpec((tm, tk), lambda i, j, k: (i, k))
    in_specs, flat = [], []
    for a, (b, off, _) in pairs:
        if mode == "nt":
            b_spec = pl.BlockSpec((None, tn, tk), lambda i, j, k, o=off // tn: (k, j + o, 0))
        elif mode == "nn":
            b_spec = pl.BlockSpec((None, tk, tn), lambda i, j, k, o=off // tk: (j, k + o, 0))
        else:
            b_spec = pl.BlockSpec((tk, tn), lambda i, j, k: (k, j))
        in_specs += [a_spec, b_spec]
        flat += [a, b]
    if into is None:
        out_shape, aliases = jax.ShapeDtypeStruct((m, n), out_dtype), {}
        out_spec = pl.BlockSpec((tm, tn), lambda i, j, k: (i, j))
    else:
        out_shape, aliases = jax.ShapeDtypeStruct(into[0].shape, into[0].dtype), {len(flat): 0}
        out_spec = pl.BlockSpec((None, tm, tn), lambda i, j, k, o=out_off // tm: (j, i + o, 0))
        in_specs.append(pl.BlockSpec(memory_space=pl.ANY))
        flat.append(into[0])
    return pl.pallas_call(
        body, name=name, grid=(m // tm, n // tn, nk),
        in_specs=in_specs,
        out_specs=out_spec,
        out_shape=out_shape, input_output_aliases=aliases,
        scratch_shapes=[pltpu.VMEM((tm, tn), F32)],
        compiler_params=_params("parallel", "parallel", "arbitrary"),
    )(*flat)


def _norm_fwd(y, w, *, name, resid=None, scale=1.0, out_dtype=F32, col=0):
    t, d = y.shape[0], w.shape[1]
    tr = _pick(t, 256)
    assert col % d == 0

    def body(*refs):
        if resid is None:
            y_ref, w_ref, o_ref = refs
        else:
            y_ref, w_ref, r_ref, o_ref = refs
        yv = y_ref[...]
        out = yv * lax.rsqrt(jnp.mean(yv * yv, axis=-1, keepdims=True) + NORM_EPS) * w_ref[...]
        if resid is not None:
            out = r_ref[...] + scale * out
        o_ref[...] = out.astype(out_dtype)

    row = pl.BlockSpec((tr, d), lambda i: (i, 0))
    wspec = pl.BlockSpec((1, d), lambda i: (0, 0))
    ins, specs = [y, w], [pl.BlockSpec((tr, d), lambda i: (i, col // d)), wspec]
    if resid is not None:
        ins.append(resid)
        specs.append(row)
    return pl.pallas_call(
        body, name=name, grid=(t // tr,), in_specs=specs, out_specs=row,
        out_shape=jax.ShapeDtypeStruct((t, d), out_dtype), compiler_params=_params("parallel"),
    )(*ins)


def _norm_bwd(x, w, dy, *, name, scale=1.0, dres=None, col=0, dx_dtype=F32):
    t, d = x.shape[0], w.shape[1]
    tr = _pick(t, 256)
    assert col % d == 0

    def body(*refs):
        if dres is None:
            x_ref, w_ref, dy_ref, dx_ref, dw_ref = refs
        else:
            x_ref, w_ref, dy_ref, dr_ref, dx_ref, dw_ref = refs

        @pl.when(pl.program_id(0) == 0)
        def _():
            dw_ref[...] = jnp.zeros_like(dw_ref)

        xv = x_ref[...]
        r = lax.rsqrt(jnp.mean(xv * xv, axis=-1, keepdims=True) + NORM_EPS)
        xhat = xv * r
        dyv = dy_ref[...].astype(F32) * scale
        dw_ref[...] += jnp.sum(dyv * xhat, axis=0, keepdims=True)
        t_ = dyv * w_ref[...]
        dx = r * (t_ - xhat * jnp.mean(t_ * xhat, axis=-1, keepdims=True))
        if dres is not None:
            dx = dx + dr_ref[...]
        dx_ref[...] = dx.astype(dx_dtype)

    row = pl.BlockSpec((tr, d), lambda i: (i, 0))
    wspec = pl.BlockSpec((1, d), lambda i: (0, 0))
    ins, specs = [x, w, dy], [pl.BlockSpec((tr, d), lambda i: (i, col // d)), wspec, row]
    if dres is not None:
        ins.append(dres)
        specs.append(row)
    return pl.pallas_call(
        body, name=name, grid=(t // tr,), in_specs=specs, out_specs=(row, wspec),
        out_shape=(jax.ShapeDtypeStruct((t, d), dx_dtype), jax.ShapeDtypeStruct((1, d), F32)),
        compiler_params=_params("arbitrary"),
    )(*ins)


def _elementwise(fn, ins, out_dtypes, *, name, width=None, cols=None):
    t = ins[0].shape[0]
    d = ins[0].shape[1] if width is None else width
    cols = [0] * len(ins) if cols is None else cols
    tc = _pick(d, 2048)
    for c in cols:
        tc = _pick(d, tc, c)
    tr = _row_tile(t, tc * 4)
    nout = len(out_dtypes)

    def body(*refs):
        outs = fn(*[r[...].astype(F32) for r in refs[:len(ins)]])
        for o_ref, o in zip(refs[len(ins):], outs):
            o_ref[...] = o.astype(o_ref.dtype)

    spec = pl.BlockSpec((tr, tc), lambda i, j: (i, j))
    in_specs = [pl.BlockSpec((tr, tc), lambda i, j, o=c // tc: (i, j + o)) for c in cols]
    return pl.pallas_call(
        body, name=name, grid=(t // tr, d // tc), in_specs=in_specs, out_specs=[spec] * nout,
        out_shape=[jax.ShapeDtypeStruct((t, d), dt) for dt in out_dtypes],
        compiler_params=_params("parallel", "parallel"),
    )(*ins)


def _swiglu_fwd(g, u, *, name):
    return _elementwise(lambda gv, uv: (_silu(gv) * uv,), [g, u], [BF16], name=name)[0]


def _swiglu_bwd(da, g, u, *, name):
    return _elementwise(lambda dav, gv, uv: (dav * uv * _dsilu(gv), dav * _silu(gv)), [da, g, u], [BF16, BF16], name=name)


def _merge_fwd(proj, col_a, col_b, ya, yb, *, name):
    return _elementwise(lambda a, b, p, q: (_sigmoid(a) * p + _sigmoid(b) * q,), [proj, proj, ya, yb], [BF16],
                        name=name, width=ya.shape[1], cols=[col_a, col_b, 0, 0])[0]


def _merge_bwd(dm, proj, col_a, col_b, ya, yb, *, name):
    def fn(dmv, a, b, p, q):
        sa, sb = _sigmoid(a), _sigmoid(b)
        return dmv * p * sa * (1.0 - sa), dmv * q * sb * (1.0 - sb), dmv * sa, dmv * sb

    return _elementwise(fn, [dm, proj, proj, ya, yb], [BF16, BF16, BF16, BF16], name=name, width=ya.shape[1],
                        cols=[0, col_a, col_b, 0, 0])


def _loss_head(xo, target, *, name):
    t, d = xo.shape
    tr = _pick(t, 256)

    def body(x_ref, t_ref, dx_ref, l_ref):
        @pl.when(pl.program_id(0) == 0)
        def _():
            l_ref[...] = jnp.zeros_like(l_ref)

        err = x_ref[...] - t_ref[...]
        dx_ref[...] = err * (1.0 / d)
        l_ref[...] += 0.5 * jnp.sum(jnp.mean(err * err, axis=-1, keepdims=True), axis=0, keepdims=True)

    row = pl.BlockSpec((tr, d), lambda i: (i, 0))
    dx, l = pl.pallas_call(
        body, name=name, grid=(t // tr,), in_specs=[row, row],
        out_specs=(row, pl.BlockSpec((1, 1), lambda i: (0, 0))),
        out_shape=(jax.ShapeDtypeStruct((t, d), F32), jax.ShapeDtypeStruct((1, 1), F32)),
        compiler_params=_params("arbitrary"),
    )(xo, target)
    return dx, l[0, 0]


def _rope(xin, tabs, *, name, group, backward, out_dtype, col=0, ngroup=None):
    t = xin.shape[0]
    ngroup = xin.shape[1] // group if ngroup is None else ngroup
    wdt = ngroup * group
    tr = _pick(t, 256)
    assert col % group == 0
    cos_t, nsin_t, sin_t = tabs

    def body(x_ref, c_ref, n_ref, s_ref, o_ref):
        xv = x_ref[...].astype(F32)
        rot = xv[:, group - LANE:]
        if backward:
            out = rot * c_ref[...] + pltpu.roll(rot * n_ref[...], 32, 1) + pltpu.roll(rot * s_ref[...], LANE - 32, 1)
        else:
            out = rot * c_ref[...] + pltpu.roll(rot, LANE - 32, 1) * n_ref[...] + pltpu.roll(rot, 32, 1) * s_ref[...]
        if group > LANE:
            out = jnp.concatenate([xv[:, :group - LANE], out], axis=1)
        o_ref[...] = out.astype(out_dtype)

    xspec = pl.BlockSpec((tr, group), lambda i, g: (i, g))
    tspec = pl.BlockSpec((tr, LANE), lambda i, g: (i, 0))
    return pl.pallas_call(
        body, name=name, grid=(t // tr, ngroup),
        in_specs=[pl.BlockSpec((tr, group), lambda i, g: (i, g + col // group)), tspec, tspec, tspec], out_specs=xspec,
        out_shape=jax.ShapeDtypeStruct((t, wdt), out_dtype), compiler_params=_params("parallel", "parallel"),
    )(xin, cos_t, nsin_t, sin_t)


def _scores(q, kv, kr, qi, tq, scale):
    kcat = jnp.concatenate([kv[:, :HEAD], kr], axis=1)
    s = lax.dot_general(q, kcat, (((1,), (1,)), ((), ())), preferred_element_type=F32) * scale
    row = qi * tq + lax.broadcasted_iota(jnp.int32, s.shape, 0)
    col = lax.broadcasted_iota(jnp.int32, s.shape, 1)
    s = jnp.where(col <= row, s, -jnp.inf)
    p = jnp.exp(s - jnp.max(s, axis=-1, keepdims=True))
    return p / jnp.sum(p, axis=-1, keepdims=True), kcat


def _attn_fwd(qcat, kv, kr, *, name, scale):
    t = qcat.shape[0]
    nh = qcat.shape[1] // QGROUP
    tq = _pick(t, 256)

    def body(q_ref, kv_ref, kr_ref, o_ref):
        kvv = kv_ref[...]
        p, _ = _scores(q_ref[...], kvv, kr_ref[...], pl.program_id(1), tq, scale)
        o_ref[...] = jnp.dot(p.astype(BF16), kvv[:, HEAD:], preferred_element_type=F32).astype(BF16)

    return pl.pallas_call(
        body, name=name, grid=(nh, t // tq),
        in_specs=[pl.BlockSpec((tq, QGROUP), lambda h, i: (i, h)), pl.BlockSpec((t, QGROUP), lambda h, i: (0, h)),
                  pl.BlockSpec((t, LANE), lambda h, i: (0, 0))],
        out_specs=pl.BlockSpec((tq, HEAD), lambda h, i: (i, h)),
        out_shape=jax.ShapeDtypeStruct((t, nh * HEAD), BF16), compiler_params=_params("parallel", "parallel"),
    )(qcat, kv, kr)


def _attn_bwd(qcat, kv, kr, do, *, name, scale):
    t = qcat.shape[0]
    nh = qcat.shape[1] // QGROUP
    tq = _pick(t, 256)
    nq = t // tq

    def body(q_ref, kv_ref, kr_ref, do_ref, dq_ref, dkv_ref, dkr_ref, dk_acc, dv_acc):
        h, i = pl.program_id(0), pl.program_id(1)

        @pl.when(i == 0)
        def _():
            dk_acc[...] = jnp.zeros_like(dk_acc)
            dv_acc[...] = jnp.zeros_like(dv_acc)

        @pl.when((i == 0) & (h == 0))
        def _():
            dkr_ref[...] = jnp.zeros_like(dkr_ref)

        q = q_ref[...]
        kvv = kv_ref[...]
        dov = do_ref[...].astype(BF16)
        p, kcat = _scores(q, kvv, kr_ref[...], i, tq, scale)
        dp = lax.dot_general(dov, kvv[:, HEAD:], (((1,), (1,)), ((), ())), preferred_element_type=F32)
        ds = (p * (dp - jnp.sum(p * dp, axis=-1, keepdims=True)) * scale).astype(BF16)
        dq_ref[...] = jnp.dot(ds, kcat, preferred_element_type=F32)
        dk_acc[...] += lax.dot_general(ds, q, (((0,), (0,)), ((), ())), preferred_element_type=F32)
        dv_acc[...] += lax.dot_general(p.astype(BF16), dov, (((0,), (0,)), ((), ())), preferred_element_type=F32)

        @pl.when(i == nq - 1)
        def _():
            dk = dk_acc[...]
            dkv_ref[...] = jnp.concatenate([dk[:, :HEAD], dv_acc[...]], axis=1)
            dkr_ref[...] += dk[:, HEAD:]

    return pl.pallas_call(
        body, name=name, grid=(nh, nq),
        in_specs=[pl.BlockSpec((tq, QGROUP), lambda h, i: (i, h)), pl.BlockSpec((t, QGROUP), lambda h, i: (0, h)),
                  pl.BlockSpec((t, LANE), lambda h, i: (0, 0)), pl.BlockSpec((tq, HEAD), lambda h, i: (i, h))],
        out_specs=(pl.BlockSpec((tq, QGROUP), lambda h, i: (i, h)), pl.BlockSpec((t, QGROUP), lambda h, i: (0, h)),
                   pl.BlockSpec((t, LANE), lambda h, i: (0, 0))),
        out_shape=(jax.ShapeDtypeStruct((t, nh * QGROUP), F32), jax.ShapeDtypeStruct((t, nh * QGROUP), F32),
                   jax.ShapeDtypeStruct((t, LANE), F32)),
        scratch_shapes=[pltpu.VMEM((t, QGROUP), F32), pltpu.VMEM((t, HEAD), F32)],
        compiler_params=_params("arbitrary", "arbitrary"),
    )(qcat, kv, kr, do)


def _split3(x):
    hi = x.astype(BF16)
    r1 = x - hi.astype(F32)
    mid = r1.astype(BF16)
    lo = (r1 - mid.astype(F32)).astype(BF16)
    return hi, mid, lo


def _tri_matmul(mask, x):
    m = mask.astype(BF16)
    return sum(jnp.dot(m, part, preferred_element_type=F32) for part in _split3(x))


def _sub_cumsum(g, tb):
    row = lax.broadcasted_iota(jnp.int32, (tb, tb), 0)
    col = lax.broadcasted_iota(jnp.int32, (tb, tb), 1)
    return _tri_matmul(jnp.where((col <= row) & (col // SUB == row // SUB), 1.0, 0.0), g)


def _sub_suffix_prefix(after, before, tb):
    row = lax.broadcasted_iota(jnp.int32, (tb, tb), 0)
    col = lax.broadcasted_iota(jnp.int32, (tb, tb), 1)
    same = col // SUB == row // SUB
    return (_tri_matmul(jnp.where((col >= row) & same, 1.0, 0.0), after)
            + _tri_matmul(jnp.where((col < row) & same, 1.0, 0.0), before))


def _lower_bound(logits):
    mx = jnp.max(logits, axis=0, keepdims=True)
    e = jnp.exp(logits - mx)
    return e[0:1, :] / jnp.sum(e, axis=0, keepdims=True)


def _hgrn_fwd(proj, cols, wdt, logits, out_norm, *, name):
    t = proj.shape[0]
    nh = wdt // HEAD
    tb = _pick(t, 128)
    ns = tb // SUB

    def body(hq_ref, hf_ref, hi_ref, hg_ref, lg_ref, w_ref, o_ref, yb_ref, st_ref, s_ref, q_s, k_s, b_s):
        @pl.when(pl.program_id(1) == 0)
        def _():
            s_ref[...] = jnp.zeros_like(s_ref)

        lb = _lower_bound(lg_ref[...])
        f = lb + (1.0 - lb) * _sigmoid(hf_ref[...])
        q_s[...] = _silu(hq_ref[...])
        k_s[...] = 1.0 - f
        b_s[...] = _sub_cumsum(jnp.log(f), tb)
        rowid = lax.broadcasted_iota(jnp.int32, (SUB, HEAD), 0)

        def sub(c, carry):
            rows = pl.ds(pl.multiple_of(c * SUB, SUB), SUB)
            qc, kc, bc, vc = q_s[rows, :], k_s[rows, :], b_s[rows, :], hi_ref[rows, :]
            st = s_ref[...]
            st_ref[0, c] = st
            bl = bc[SUB - 1:SUB, :]
            oc = lax.dot_general((qc * jnp.exp(bc)).astype(BF16), st.astype(BF16), (((1,), (1,)), ((), ())),
                                 preferred_element_type=F32)
            for s in range(SUB):
                e = jnp.where(rowid >= s, jnp.exp(bc - bc[s:s + 1, :]), 0.0)
                a = jnp.sum(qc * e * kc[s:s + 1, :], axis=1, keepdims=True)
                oc = oc + a * vc[s:s + 1, :]
            o_ref[rows, :] = oc
            kd = kc * jnp.exp(bl - bc)
            s_ref[...] = jnp.exp(bl) * st + lax.dot_general(vc.astype(BF16), kd.astype(BF16), (((0,), (0,)), ((), ())),
                                                             preferred_element_type=F32)
            return carry

        lax.fori_loop(0, ns, sub, 0)
        o = o_ref[...]
        r = lax.rsqrt(jnp.mean(o * o, axis=-1, keepdims=True) + NORM_EPS)
        yb_ref[...] = (o * r * w_ref[...] * _silu(hg_ref[...])).astype(BF16)

    blk = pl.BlockSpec((tb, HEAD), lambda h, j: (j, h))
    return pl.pallas_call(
        body, name=name, grid=(nh, t // tb),
        in_specs=[pl.BlockSpec((tb, HEAD), lambda h, j, o=c // HEAD: (j, h + o)) for c in cols]
        + [pl.BlockSpec((2, HEAD), lambda h, j: (0, h)), pl.BlockSpec((1, HEAD), lambda h, j: (0, 0))],
        out_specs=(blk, blk, pl.BlockSpec((1, ns, HEAD, HEAD), lambda h, j: (h, j, 0, 0))),
        out_shape=(jax.ShapeDtypeStruct((t, wdt), F32), jax.ShapeDtypeStruct((t, wdt), BF16),
                   jax.ShapeDtypeStruct((nh, t // SUB, HEAD, HEAD), F32)),
        scratch_shapes=[pltpu.VMEM((HEAD, HEAD), F32)] + [pltpu.VMEM((tb, HEAD), F32)] * 3,
        compiler_params=_params("parallel", "arbitrary"),
    )(proj, proj, proj, proj, logits, out_norm)


def _hgrn_bwd(proj, cols, wdt, o_raw, dyb, states, logits, out_norm, *, name):
    t = proj.shape[0]
    nh = wdt // HEAD
    tb = _pick(t, 128)
    ns = tb // SUB
    nb = t // tb

    def body(hq_ref, hf_ref, hi_ref, hg_ref, o_ref, dy_ref, st_ref, lg_ref, w_ref,
             dhq_ref, dhf_ref, dhi_ref, dhg_ref, dlb_ref, dw_ref,
             ds_ref, q_s, k_s, b_s, do_s, dq_s, dk_s, dv_s, after_s, before_s, thru_s):
        @pl.when(pl.program_id(1) == 0)
        def _():
            ds_ref[...] = jnp.zeros_like(ds_ref)
            dlb_ref[...] = jnp.zeros_like(dlb_ref)
            dw_ref[...] = jnp.zeros_like(dw_ref)

        lb = _lower_bound(lg_ref[...])
        hqv, hgv = hq_ref[...], hg_ref[...]
        sig = _sigmoid(hf_ref[...])
        f = lb + (1.0 - lb) * sig
        q_s[...] = _silu(hqv)
        k_s[...] = 1.0 - f
        b_s[...] = _sub_cumsum(jnp.log(f), tb)

        o = o_ref[...]
        r = lax.rsqrt(jnp.mean(o * o, axis=-1, keepdims=True) + NORM_EPS)
        nrm = o * r
        w = w_ref[...]
        dy = dy_ref[...].astype(F32)
        dhg_ref[...] = (dy * nrm * w * _dsilu(hgv)).astype(BF16)
        dnw = dy * _silu(hgv)
        dw_ref[0] += jnp.sum(dnw * nrm, axis=0, keepdims=True)
        tt = dnw * w
        do_s[...] = r * (tt - nrm * jnp.mean(tt * nrm, axis=-1, keepdims=True))
        rowid = lax.broadcasted_iota(jnp.int32, (SUB, HEAD), 0)

        def sub(cc, carry):
            c = ns - 1 - cc
            rows = pl.ds(pl.multiple_of(c * SUB, SUB), SUB)
            qc, kc, bc, vc, doc = q_s[rows, :], k_s[rows, :], b_s[rows, :], hi_ref[rows, :], do_s[rows, :]
            st = st_ref[0, c]
            dst = ds_ref[...]
            bl = bc[SUB - 1:SUB, :]
            eb = jnp.exp(bc)
            ekd = jnp.exp(bl - bc)
            qe, kd = qc * eb, kc * ekd
            dob, vcb = doc.astype(BF16), vc.astype(BF16)
            dq_st = jnp.dot(dob, st.astype(BF16), preferred_element_type=F32) * eb
            dk_st = jnp.dot(vcb, dst.astype(BF16), preferred_element_type=F32) * ekd
            dv = lax.dot_general(kd.astype(BF16), dst.astype(BF16), (((1,), (1,)), ((), ())), preferred_element_type=F32)
            dq_in = jnp.zeros_like(qc)
            dk_in = jnp.zeros_like(qc)
            for s in range(SUB):
                e = jnp.where(rowid >= s, jnp.exp(bc - bc[s:s + 1, :]), 0.0)
                ek = e * kc[s:s + 1, :]
                a = jnp.sum(qc * ek, axis=1, keepdims=True)
                da = jnp.sum(doc * vc[s:s + 1, :], axis=1, keepdims=True)
                dq_in = dq_in + da * ek
                dk_in = dk_in + jnp.where(rowid == s, jnp.sum(da * e * qc, axis=0, keepdims=True), 0.0)
                dv = dv + jnp.where(rowid == s, jnp.sum(a * doc, axis=0, keepdims=True), 0.0)
            ebl = jnp.exp(bl)
            ds_ref[...] = ebl * dst + lax.dot_general(dob, qe.astype(BF16), (((0,), (0,)), ((), ())),
                                                      preferred_element_type=F32)
            dq_s[rows, :] = dq_st + dq_in
            dk_s[rows, :] = dk_st + dk_in
            dv_s[rows, :] = dv
            after_s[rows, :] = qc * (dq_st + dq_in) - kc * dk_in
            before_s[rows, :] = kc * dk_st
            thru_s[rows, :] = jnp.broadcast_to(ebl * jnp.sum(st * dst, axis=0, keepdims=True), (SUB, HEAD))
            return carry

        lax.fori_loop(0, ns, sub, 0)
        dg = _sub_suffix_prefix(after_s[...], before_s[...], tb) + thru_s[...]
        dhq_ref[...] = (dq_s[...] * _dsilu(hqv)).astype(BF16)
        dft = dg / f - dk_s[...]
        dhf_ref[...] = (dft * (1.0 - lb) * sig * (1.0 - sig)).astype(BF16)
        dlb_ref[0] += jnp.sum(dft * (1.0 - sig), axis=0, keepdims=True)
        dhi_ref[...] = dv_s[...].astype(BF16)

    blk = pl.BlockSpec((tb, HEAD), lambda h, j: (nb - 1 - j, h))
    vec = pl.BlockSpec((1, 1, HEAD), lambda h, j: (h, 0, 0))
    tok = jax.ShapeDtypeStruct((t, wdt), BF16)
    per_head = jax.ShapeDtypeStruct((nh, 1, HEAD), F32)
    return pl.pallas_call(
        body, name=name, grid=(nh, nb),
        in_specs=[pl.BlockSpec((tb, HEAD), lambda h, j, o=c // HEAD: (nb - 1 - j, h + o)) for c in cols]
        + [blk, blk] + [pl.BlockSpec((1, ns, HEAD, HEAD), lambda h, j: (h, nb - 1 - j, 0, 0)),
                              pl.BlockSpec((2, HEAD), lambda h, j: (0, h)), pl.BlockSpec((1, HEAD), lambda h, j: (0, 0))],
        out_specs=(blk, blk, blk, blk, vec, vec),
        out_shape=(tok, tok, tok, tok, per_head, per_head),
        scratch_shapes=[pltpu.VMEM((HEAD, HEAD), F32)] + [pltpu.VMEM((tb, HEAD), F32)] * 10,
        compiler_params=_params("arbitrary", "arbitrary"),
    )(proj, proj, proj, proj, o_raw, dyb, states, logits, out_norm)


def _lb_logits_grad(logits, dlb, *, name):
    def body(lg_ref, d_ref, o_ref):
        lg = lg_ref[...]
        e = jnp.exp(lg - jnp.max(lg, axis=0, keepdims=True))
        p = e / jnp.sum(e, axis=0, keepdims=True)
        d = d_ref[...]
        rowid = lax.broadcasted_iota(jnp.int32, lg.shape, 0)
        dp = jnp.where(rowid == 0, d, 0.0)
        o_ref[...] = p * (dp - jnp.sum(p * dp, axis=0, keepdims=True))

    return pl.pallas_call(body, name=name, out_shape=jax.ShapeDtypeStruct(logits.shape, F32))(logits, dlb)


def _adamw(w, g, m, v, *, name):
    r, c = w.shape
    tc = _pick(c, 2048) if c % LANE == 0 else c
    tr = _row_tile(r, tc * 4)

    def body(w_ref, g_ref, m_ref, v_ref, d_ref, nm_ref, nv_ref):
        gv = g_ref[...]
        nm = ADAM_B1 * m_ref[...] + (1.0 - ADAM_B1) * gv
        nv = ADAM_B2 * v_ref[...] + (1.0 - ADAM_B2) * (gv * gv)
        m_hat = nm / (1.0 - ADAM_B1 ** ADAM_STEP)
        v_hat = nv / (1.0 - ADAM_B2 ** ADAM_STEP)
        d_ref[...] = -ADAM_LR * (m_hat / (jnp.sqrt(v_hat) + ADAM_EPS) + ADAM_WD * w_ref[...])
        nm_ref[...] = nm
        nv_ref[...] = nv

    spec = pl.BlockSpec((tr, tc), lambda i, j: (i, j))
    shp = jax.ShapeDtypeStruct((r, c), F32)
    return pl.pallas_call(
        body, name=name, grid=(r // tr, c // tc), in_specs=[spec] * 4, out_specs=[spec] * 3,
        out_shape=[shp, shp, shp], compiler_params=_params("parallel", "parallel"),
    )(w, g, m, v)


def _coords():
    return lax.axis_index("x"), lax.axis_index("y"), lax.axis_index("c")


def _other_chips(x, y):
    return [(1 - x, y), (x, 1 - y), (1 - x, 1 - y)]


ANY = pl.BlockSpec(memory_space=pl.ANY)


class _Layout:
    def __init__(self, d, dff, in_cols, q_lora, kv_lora, nh):
        assert q_lora == kv_lora and nh % 4 == 0 and dff % 4 == 0 and in_cols % 4 == 0 and d % 4 == 0
        self.d, self.dff, self.q_lora, self.nh = d, dff, q_lora, nh
        self.head = q_lora + kv_lora + ROPE
        self.pad = d - self.head
        self.dffp = -(-dff // MM_TILE) * MM_TILE
        self.ffn_pad = self.dffp - dff
        dffp = self.dffp
        nff, ncol, r_o, hps = dff // 4, in_cols // 4, d // 4, nh // 4
        assert self.head <= ncol
        names = ("ffn1_w_gate", "ffn1_w_up", "ffn1_w_down", "ffn2_w_gate", "ffn2_w_up", "ffn2_w_down")
        self.ffn_names = names
        self.off = {n: i * dffp for i, n in enumerate(names)}
        self.off.update(w_in=6 * dffp, mla_w_o=6 * dffp + 7 * d, hgrn_w_o=6 * dffp + 8 * d, w_out=6 * dffp + 9 * d)
        self.rows_wide = 6 * dffp + 10 * d
        self.off_q, self.off_kv, self.rows_narrow = 0, nh * QGROUP, 2 * nh * QGROUP
        self.loff = {n: i * nff for i, n in enumerate(names)}
        self.loff.update(w_in=6 * nff, mla_w_o=6 * nff + ncol, hgrn_w_o=6 * nff + ncol + r_o, w_out=6 * nff + ncol + 2 * r_o)
        self.lrows = {n: nff for n in names}
        self.lrows.update(w_in=ncol, mla_w_o=r_o, hgrn_w_o=r_o, w_out=r_o)
        self.lrows_wide = 6 * nff + ncol + 3 * r_o
        self.lrows_narrow = hps * (HEAD + ROPE) + hps * QGROUP
        self.loff_q, self.loff_kv = 0, hps * (HEAD + ROPE)
        self.nff, self.ncol, self.r_o, self.hps = nff, ncol, r_o, hps

    def segments(self, k):
        first = lambda a, b: jnp.where(k == 0, a, b) if not isinstance(k, int) else (a if k == 0 else b)
        segs = []
        for n in ("ffn1_w_gate", "ffn1_w_up", "ffn1_w_down"):
            segs.append((0, self.loff[n], self.nff, self.off[n] + self.nff * k, 0))
        w_in = self.off["w_in"]
        segs.append((0, self.loff["w_in"], self.head, w_in + first(0, self.ncol * k + self.pad), 1))
        segs.append((0, self.loff["w_in"] + self.head, self.ncol - self.head, w_in + self.ncol * k + self.d, 1))
        for n in ("mla_w_o", "hgrn_w_o", "w_out"):
            segs.append((0, self.loff[n], self.r_o, self.off[n] + self.r_o * k, 1))
        for hh in range(self.hps):
            segs.append((1, (HEAD + ROPE) * hh, HEAD + ROPE, QGROUP * (self.hps * k + hh), 2))
        segs.append((1, self.loff_kv, self.hps * QGROUP, self.off_kv + self.hps * QGROUP * k, 2))
        for n in ("ffn2_w_gate", "ffn2_w_up", "ffn2_w_down"):
            segs.append((0, self.loff[n], self.nff, self.off[n] + self.nff * k, 3))
        return segs

    def stream_rows(self, stream):
        return sum(s[2] for s in self.segments(0) if s[4] == stream)


N_STREAM = 4


def _half(ref, row, rows, half):
    return ref.at[half, pl.ds(row, rows)]


def _both(ref, row, rows):
    return ref.at[:, pl.ds(row, rows)]


def _wait_bytes_of(ref_like, send_sem, recv_sem, me, *, send):
    cp = pltpu.make_async_remote_copy(src_ref=ref_like, dst_ref=ref_like, send_sem=send_sem, recv_sem=recv_sem,
                                      device_id=me, device_id_type=MESH)
    if send:
        cp.wait_send()
    else:
        cp.wait_recv()


def _gather_weights(lwide, lnarrow, zwide, znarrow, lay, *, name):
    d, ql = lay.d, lay.q_lora
    qpad = QGROUP - HEAD - ROPE

    def body(l0, l1, z0, z1, w0, w1, send, recv, fsend, frecv, lsem):
        x, y, c = _coords()
        me_chip = 2 * x + y
        chips = _other_chips(x, y)
        src, dst = (l0, l1), (w0, w1)

        for a, lrow, rows, drow, _ in lay.segments(me_chip):
            pltpu.make_async_copy(_both(src[a], lrow, rows), _both(dst[a], drow, rows), lsem.at[a]).start()
        pltpu.make_async_copy(_both(z0, 0, lay.pad), _both(w0, lay.off["w_in"] + lay.head, lay.pad), lsem.at[0]).start()
        if lay.ffn_pad:
            for n in lay.ffn_names:
                pltpu.make_async_copy(_both(z0, 0, lay.ffn_pad), _both(w0, lay.off[n] + lay.dff, lay.ffn_pad), lsem.at[0]).start()
        for g in range(lay.nh):
            pltpu.make_async_copy(z1, _both(w1, QGROUP * g + HEAD + ROPE, qpad), lsem.at[1]).start()

        for j, (px, py) in enumerate(chips):
            for a, lrow, rows, drow, st in lay.segments(me_chip):
                pltpu.make_async_remote_copy(
                    src_ref=_half(src[a], lrow, rows, c), dst_ref=_half(dst[a], drow, rows, c),
                    send_sem=send.at[j, st], recv_sem=recv.at[j, st], device_id=(px, py, c), device_id_type=MESH).start()

        def total(st):
            return _half(dst[1 if st == 2 else 0], 0, lay.stream_rows(st), 0)

        for st in range(N_STREAM):
            for j, (px, py) in enumerate(chips):
                _wait_bytes_of(total(st), send.at[j, st], recv.at[j, st], (x, y, c), send=False)
                for a, lrow, rows, drow, s2 in lay.segments(2 * px + py):
                    if s2 == st:
                        blk = _half(dst[a], drow, rows, c)
                        pltpu.make_async_remote_copy(
                            src_ref=blk, dst_ref=blk, send_sem=fsend.at[j, st], recv_sem=frecv.at[j, st],
                            device_id=(x, y, 1 - c), device_id_type=MESH).start()
        for st in range(N_STREAM):
            for j in range(3):
                _wait_bytes_of(total(st), fsend.at[j, st], frecv.at[j, st], (x, y, c), send=False)
        for st in range(N_STREAM):
            for j in range(3):
                _wait_bytes_of(total(st), fsend.at[j, st], frecv.at[j, st], (x, y, c), send=True)
                _wait_bytes_of(total(st), send.at[j, st], recv.at[j, st], (x, y, c), send=True)
        own0 = _both(w0, 0, lay.lrows_wide + lay.pad + 6 * lay.ffn_pad)
        pltpu.make_async_copy(own0, own0, lsem.at[0]).wait()
        own1 = _both(w1, 0, lay.lrows_narrow + lay.nh * qpad)
        pltpu.make_async_copy(own1, own1, lsem.at[1]).wait()

    sem = pltpu.SemaphoreType.DMA((3, N_STREAM))
    return pl.pallas_call(
        body, name=name, in_specs=[ANY] * 4, out_specs=[ANY] * 2,
        out_shape=[jax.ShapeDtypeStruct((2, lay.rows_wide, d // 2), BF16),
                   jax.ShapeDtypeStruct((2, lay.rows_narrow, ql // 2), BF16)],
        scratch_shapes=[sem, sem, sem, sem, pltpu.SemaphoreType.DMA((2,))],
    )(lwide, lnarrow, zwide, znarrow)


def _swap_halves(gwide, gnarrow, *, name):
    def body(g0, g1, r0, r1, send_sems, recv_sems):
        x, y, c = _coords()
        copies = []
        for a, (g, r) in enumerate(((g0, r0), (g1, r1))):
            copies.append(pltpu.make_async_remote_copy(
                src_ref=g.at[1 - c], dst_ref=r, send_sem=send_sems.at[a], recv_sem=recv_sems.at[a],
                device_id=(x, y, 1 - c), device_id_type=MESH))
        for cp in copies:
            cp.start()
        for cp in copies:
            cp.wait()

    return pl.pallas_call(
        body, name=name, in_specs=[ANY] * 2, out_specs=[ANY] * 2,
        out_shape=[jax.ShapeDtypeStruct(g.shape[1:], g.dtype) for g in (gwide, gnarrow)],
        scratch_shapes=[pltpu.SemaphoreType.DMA((2,)), pltpu.SemaphoreType.DMA((2,))],
    )(gwide, gnarrow)


def _add_sibling(g, recv, sel, *, name):
    rows, hw = recv.shape
    tr = _row_tile(rows, hw * 4)

    def body(sel_ref, g_ref, r_ref, o_ref):
        o_ref[...] = (g_ref[...] + r_ref[...]).astype(BF16)

    return pl.pallas_call(
        body, name=name, out_shape=jax.ShapeDtypeStruct((rows, hw), BF16),
        grid_spec=pltpu.PrefetchScalarGridSpec(
            num_scalar_prefetch=1, grid=(rows // tr,),
            in_specs=[pl.BlockSpec((None, tr, hw), lambda i, s: (s[0], i, 0)), pl.BlockSpec((tr, hw), lambda i, s: (i, 0))],
            out_specs=pl.BlockSpec((tr, hw), lambda i, s: (i, 0))),
        compiler_params=_params("parallel"),
    )(sel, g, recv)


def _exchange_chips(swide, snarrow, lay, *, name):
    def body(s0, s1, r0, r1, send, recv, lsem):
        x, y, c = _coords()
        me_chip = 2 * x + y
        src, dst = (s0, s1), (r0, r1)
        for a, lrow, rows, drow, _ in lay.segments(me_chip):
            pltpu.make_async_copy(src[a].at[pl.ds(drow, rows)], dst[a].at[me_chip, pl.ds(lrow, rows)], lsem.at[a]).start()
        for j, (px, py) in enumerate(_other_chips(x, y)):
            for a, lrow, rows, drow, _ in lay.segments(2 * px + py):
                pltpu.make_async_remote_copy(
                    src_ref=src[a].at[pl.ds(drow, rows)], dst_ref=dst[a].at[me_chip, pl.ds(lrow, rows)],
                    send_sem=send.at[j, a], recv_sem=recv.at[j, a], device_id=(px, py, c), device_id_type=MESH).start()
        for a in range(2):
            for j in range(3):
                _wait_bytes_of(dst[a].at[0], send.at[j, a], recv.at[j, a], (x, y, c), send=False)
        for a in range(2):
            for j in range(3):
                _wait_bytes_of(dst[a].at[0], send.at[j, a], recv.at[j, a], (x, y, c), send=True)
            pltpu.make_async_copy(dst[a].at[0], dst[a].at[0], lsem.at[a]).wait()

    sem = pltpu.SemaphoreType.DMA((3, 2))
    return pl.pallas_call(
        body, name=name, in_specs=[ANY] * 2, out_specs=[ANY] * 2,
        out_shape=[jax.ShapeDtypeStruct((4, lay.lrows_wide, swide.shape[1]), BF16),
                   jax.ShapeDtypeStruct((4, lay.lrows_narrow, snarrow.shape[1]), BF16)],
        scratch_shapes=[sem, sem, pltpu.SemaphoreType.DMA((2,))],
    )(swide, snarrow)


def _add_chips(parts, *, name):
    _, rows, wdt = parts.shape
    tr = _row_tile(rows, wdt * 4)

    def body(p_ref, o_ref):
        o_ref[...] = ((p_ref[0].astype(F32) + p_ref[1].astype(F32)) + p_ref[2].astype(F32)) + p_ref[3].astype(F32)

    return pl.pallas_call(
        body, name=name, grid=(rows // tr,), in_specs=[pl.BlockSpec((4, tr, wdt), lambda i: (0, i, 0))],
        out_specs=pl.BlockSpec((tr, wdt), lambda i: (i, 0)), out_shape=jax.ShapeDtypeStruct((rows, wdt), F32),
        compiler_params=_params("parallel"),
    )(parts)


def _join_halves(hwide, hnarrow, *, name):
    def body(h0, h1, f0, f1, send_sems, recv_sems, lsem):
        x, y, c = _coords()
        copies, local = [], []
        for a, (h, f) in enumerate(((h0, f0), (h1, f1))):
            mine = f.at[c]
            local.append(pltpu.make_async_copy(h, mine, lsem.at[a]))
            copies.append(pltpu.make_async_remote_copy(
                src_ref=h, dst_ref=mine, send_sem=send_sems.at[a], recv_sem=recv_sems.at[a],
                device_id=(x, y, 1 - c), device_id_type=MESH))
        for cp in local + copies:
            cp.start()
        for cp in copies + local:
            cp.wait()

    sem = pltpu.SemaphoreType.DMA((2,))
    return pl.pallas_call(
        body, name=name, in_specs=[ANY] * 2, out_specs=[ANY] * 2,
        out_shape=[jax.ShapeDtypeStruct((2,) + h.shape, h.dtype) for h in (hwide, hnarrow)],
        scratch_shapes=[sem, sem, sem],
    )(hwide, hnarrow)


def _all_reduce_small(vec, *, name):
    n = vec.shape[1]

    def body(v_ref, o_ref, buf, send_sems, recv_sems):
        x, y, c = _coords()
        me = 4 * x + 2 * y + c
        buf[me] = v_ref[...]
        copies = []
        for m in range(1, 8):
            peer = (x ^ ((m >> 2) & 1), y ^ ((m >> 1) & 1), c ^ (m & 1))
            copies.append(pltpu.make_async_remote_copy(
                src_ref=v_ref, dst_ref=buf.at[me], send_sem=send_sems.at[m - 1], recv_sem=recv_sems.at[m - 1],
                device_id=peer, device_id_type=MESH))
        for cp in copies:
            cp.start()
        for cp in copies:
            cp.wait()
        acc = buf[0]
        for d in range(1, 8):
            acc = acc + buf[d]
        o_ref[...] = acc

    return pl.pallas_call(
        body, name=name, out_shape=jax.ShapeDtypeStruct((1, n), F32),
        in_specs=[pl.BlockSpec(memory_space=pltpu.VMEM)], out_specs=pl.BlockSpec(memory_space=pltpu.VMEM),
        scratch_shapes=[pltpu.VMEM((8, 1, n), F32), pltpu.SemaphoreType.DMA((7,)), pltpu.SemaphoreType.DMA((7,))],
    )(vec)


def _ffn_fwd(x, n_pre, n_post, wide, lay, tag):
    wg, wu, wd = ((wide, lay.off[f"{tag}_w_{p}"], lay.dffp) for p in ("gate", "up", "down"))
    h = _norm_fwd(x, n_pre, name=f"{tag}_norm_pre", out_dtype=BF16)
    g = _mm([(h, wg)], name=f"{tag}_gate", mode="nt")
    u = _mm([(h, wu)], name=f"{tag}_up", mode="nt")
    a = _swiglu_fwd(g, u, name=f"{tag}_swiglu")
    yv = _mm([(a, wd)], name=f"{tag}_down", mode="nn")
    out = _norm_fwd(yv, n_post, name=f"{tag}_norm_post", resid=x, scale=MACARON_SCALE)
    return out, (x, h, g, u, a, yv)


def _ffn_bwd(dout, saved, n_pre, n_post, wide, gwide, lay, tag):
    x, h, g, u, a, yv = saved
    og, ou, od = (lay.off[f"{tag}_w_{p}"] for p in ("gate", "up", "down"))
    dy, dn_post = _norm_bwd(yv, n_post, dout, name=f"{tag}_norm_post_bwd", scale=MACARON_SCALE)
    da = _mm([(dy, (wide, od, lay.dffp))], name=f"{tag}_down_dx", mode="nt")
    gwide = _mm([(a, dy)], name=f"{tag}_down_dw", mode="tn", into=(gwide, od))
    dg, du = _swiglu_bwd(da, g, u, name=f"{tag}_swiglu_bwd")
    dh = _mm([(dg, (wide, og, lay.dffp)), (du, (wide, ou, lay.dffp))], name=f"{tag}_up_dx", mode="nn")
    gwide = _mm([(dg, h)], name=f"{tag}_gate_dw", mode="tn", into=(gwide, og))
    gwide = _mm([(du, h)], name=f"{tag}_up_dw", mode="tn", into=(gwide, ou))
    dx, dn_pre = _norm_bwd(x, n_pre, dh, name=f"{tag}_norm_pre_bwd", dres=dout)
    return dx, dn_pre, dn_post, gwide


def _rope_tables(positions):
    half = ROPE // 2
    inv_freq = ROPE_THETA ** (-jnp.arange(half, dtype=F32) / half)
    ang = positions.astype(F32)[:, None] * inv_freq
    cos, sin = jnp.cos(ang), jnp.sin(ang)
    z = jnp.zeros_like(cos)
    z2 = jnp.zeros((positions.shape[0], LANE - ROPE), F32)
    return (jnp.concatenate([cos, cos, z2], axis=1), jnp.concatenate([-sin, z, z2], axis=1),
            jnp.concatenate([z, sin, z2], axis=1))


def kernel(x, positions, ffn1_norm_pre, ffn1_w_gate, ffn1_w_up, ffn1_w_down, ffn1_norm_post, mix_norm_pre, w_in, mla_q_norm, mla_w_q_up, mla_kv_norm, mla_w_kv_up, mla_w_o, hgrn_lb_logits, hgrn_out_norm, hgrn_w_o, w_out, mix_norm_post, ffn2_norm_pre, ffn2_w_gate, ffn2_w_up, ffn2_w_down, ffn2_norm_post, loss_target, m_ffn1_norm_pre, m_ffn1_w_gate, m_ffn1_w_up, m_ffn1_w_down, m_ffn1_norm_post, m_mix_norm_pre, m_w_in, m_mla_q_norm, m_mla_w_q_up, m_mla_kv_norm, m_mla_w_kv_up, m_mla_w_o, m_hgrn_lb_logits, m_hgrn_out_norm, m_hgrn_w_o, m_w_out, m_mix_norm_post, m_ffn2_norm_pre, m_ffn2_w_gate, m_ffn2_w_up, m_ffn2_w_down, m_ffn2_norm_post, v_ffn1_norm_pre, v_ffn1_w_gate, v_ffn1_w_up, v_ffn1_w_down, v_ffn1_norm_post, v_mix_norm_pre, v_w_in, v_mla_q_norm, v_mla_w_q_up, v_mla_kv_norm, v_mla_w_kv_up, v_mla_w_o, v_hgrn_lb_logits, v_hgrn_out_norm, v_hgrn_w_o, v_w_out, v_mix_norm_post, v_ffn2_norm_pre, v_ffn2_w_gate, v_ffn2_w_up, v_ffn2_w_down, v_ffn2_norm_post):
    given = dict(locals())
    wts = {n: given[n] for n in ALL_WEIGHTS}
    mom = {n: given["m_" + n] for n in ALL_WEIGHTS}
    var = {n: given["v_" + n] for n in ALL_WEIGHTS}
    xin = x[0]
    target = loss_target[0]
    t, d = xin.shape
    cx, cy, cc = _coords()

    q_lora, kv_lora = mla_q_norm.shape[1], mla_kv_norm.shape[1]
    nh_mla = 4 * mla_w_kv_up.shape[2] // QGROUP
    lay = _Layout(d, 4 * ffn1_w_gate.shape[2], 4 * w_in.shape[2], q_lora, kv_lora, nh_mla)
    col_sharded = lambda n: wts[n][0].T.astype(BF16)
    row_sharded = lambda n: wts[n][0].astype(BF16)
    lwide = jnp.concatenate([col_sharded("ffn1_w_gate"), col_sharded("ffn1_w_up"), row_sharded("ffn1_w_down"),
                             col_sharded("ffn2_w_gate"), col_sharded("ffn2_w_up"), row_sharded("ffn2_w_down"),
                             col_sharded("w_in"), row_sharded("mla_w_o"), row_sharded("hgrn_w_o"), row_sharded("w_out")])
    lnarrow = jnp.concatenate([col_sharded("mla_w_q_up"), col_sharded("mla_w_kv_up")])
    halves = lambda a: a.reshape(a.shape[0], 2, a.shape[1] // 2).transpose(1, 0, 2)
    wide, narrow = _gather_weights(halves(lwide), halves(lnarrow), jnp.zeros((2, max(lay.pad, lay.ffn_pad), d // 2), BF16),
                                   jnp.zeros((2, QGROUP - HEAD - ROPE, q_lora // 2), BF16), lay, name="gather_weights")
    w_in_v = (wide, lay.off["w_in"], 7 * d)
    w_q_v = (narrow, lay.off_q, nh_mla * QGROUP)
    w_kv_v = (narrow, lay.off_kv, nh_mla * QGROUP)
    w_o_v = {n: (wide, lay.off[n], d) for n in ("mla_w_o", "hgrn_w_o", "w_out")}
    col_kr = q_lora + kv_lora
    hgrn_cols = [d, 2 * d, 3 * d, 4 * d]
    col_ga, col_gb = 5 * d, 6 * d
    tabs = _rope_tables(positions[0])
    scale = (HEAD + ROPE) ** -0.5

    x1, saved1 = _ffn_fwd(xin, ffn1_norm_pre, ffn1_norm_post, wide, lay, "ffn1")

    h2 = _norm_fwd(x1, mix_norm_pre, name="mix_norm_pre", out_dtype=BF16)
    proj = _mm([(h2, w_in_v)], name="mix_in", mode="nt")
    cqn = _norm_fwd(proj, mla_q_norm, name="mla_q_norm", out_dtype=BF16, col=0)
    ckvn = _norm_fwd(proj, mla_kv_norm, name="mla_kv_norm", out_dtype=BF16, col=q_lora)
    qp = _mm([(cqn, w_q_v)], name="mla_q_up", mode="nt")
    kvb = _mm([(ckvn, w_kv_v)], name="mla_kv_up", mode="nt", out_dtype=BF16)
    qcat = _rope(qp, tabs, name="rope_q", group=QGROUP, backward=False, out_dtype=BF16)
    krot = _rope(proj, tabs, name="rope_k", group=LANE, backward=False, out_dtype=BF16, col=col_kr, ngroup=1)
    o_mla = _attn_fwd(qcat, kvb, krot, name="mla_attention", scale=scale)
    y_a = _mm([(o_mla, w_o_v["mla_w_o"])], name="mla_out", mode="nn")

    o_raw, yb, states = _hgrn_fwd(proj, hgrn_cols, d, hgrn_lb_logits, hgrn_out_norm, name="hgrn_scan")
    y_b = _mm([(yb, w_o_v["hgrn_w_o"])], name="hgrn_out", mode="nn")

    merged = _merge_fwd(proj, col_ga, col_gb, y_a, y_b, name="mix_merge")
    y_mix = _mm([(merged, w_o_v["w_out"])], name="mix_out", mode="nn")
    x2 = _norm_fwd(y_mix, mix_norm_post, name="mix_norm_post", resid=x1, scale=1.0)

    x3, saved2 = _ffn_fwd(x2, ffn2_norm_pre, ffn2_norm_post, wide, lay, "ffn2")
    dx3, loss_local = _loss_head(x3, target, name="loss_head")

    grads = {}
    gwide = lax.empty((2, lay.rows_wide, d // 2), F32)
    gnarrow = lax.empty((2, lay.rows_narrow, q_lora // 2), F32)
    dx2, grads["ffn2_norm_pre"], grads["ffn2_norm_post"], gwide = _ffn_bwd(
        dx3, saved2, ffn2_norm_pre, ffn2_norm_post, wide, gwide, lay, "ffn2")

    dy_mix, grads["mix_norm_post"] = _norm_bwd(y_mix, mix_norm_post, dx2, name="mix_norm_post_bwd")
    dmerged = _mm([(dy_mix, w_o_v["w_out"])], name="mix_out_dx", mode="nt")
    gwide = _mm([(merged, dy_mix)], name="mix_out_dw", mode="tn", into=(gwide, lay.off["w_out"]))
    dga, dgb, dy_a, dy_b = _merge_bwd(dmerged, proj, col_ga, col_gb, y_a, y_b, name="mix_merge_bwd")

    do_mla = _mm([(dy_a, w_o_v["mla_w_o"])], name="mla_out_dx", mode="nt")
    gwide = _mm([(o_mla, dy_a)], name="mla_out_dw", mode="tn", into=(gwide, lay.off["mla_w_o"]))
    dqcat, dkv, dkr = _attn_bwd(qcat, kvb, krot, do_mla, name="mla_attention_bwd", scale=scale)
    dqp = _rope(dqcat, tabs, name="rope_q_bwd", group=QGROUP, backward=True, out_dtype=BF16)
    dk_r = _rope(dkr, tabs, name="rope_k_bwd", group=LANE, backward=True, out_dtype=BF16)
    dcqn = _mm([(dqp, w_q_v)], name="mla_q_up_dx", mode="nn")
    gnarrow = _mm([(dqp, cqn)], name="mla_q_up_dw", mode="tn", into=(gnarrow, lay.off_q))
    dkvb = dkv.astype(BF16)
    dckvn = _mm([(dkvb, w_kv_v)], name="mla_kv_up_dx", mode="nn")
    gnarrow = _mm([(dkvb, ckvn)], name="mla_kv_up_dw", mode="tn", into=(gnarrow, lay.off_kv))
    dc_q, grads["mla_q_norm"] = _norm_bwd(proj, mla_q_norm, dcqn, name="mla_q_norm_bwd", col=0, dx_dtype=BF16)
    dc_kv, grads["mla_kv_norm"] = _norm_bwd(proj, mla_kv_norm, dckvn, name="mla_kv_norm_bwd", col=q_lora, dx_dtype=BF16)

    dyb = _mm([(dy_b, w_o_v["hgrn_w_o"])], name="hgrn_out_dx", mode="nt")
    gwide = _mm([(yb, dy_b)], name="hgrn_out_dw", mode="tn", into=(gwide, lay.off["hgrn_w_o"]))
    dhq, dhf, dhi, dhg, dlb_h, dnorm_h = _hgrn_bwd(proj, hgrn_cols, d, o_raw, dyb, states, hgrn_lb_logits, hgrn_out_norm,
                                                   name="hgrn_scan_bwd")

    dproj = jnp.concatenate([dc_q, dc_kv, dk_r, jnp.zeros((t, d - col_kr - LANE), BF16), dhq, dhf, dhi, dhg, dga, dgb], axis=1)
    dh2 = _mm([(dproj, w_in_v)], name="mix_in_dx", mode="nn")
    gwide = _mm([(dproj, h2)], name="mix_in_dw", mode="tn", into=(gwide, lay.off["w_in"]))
    dx1, grads["mix_norm_pre"] = _norm_bwd(x1, mix_norm_pre, dh2, name="mix_norm_pre_bwd", dres=dx2)

    dx0, grads["ffn1_norm_pre"], grads["ffn1_norm_post"], gwide = _ffn_bwd(
        dx1, saved1, ffn1_norm_pre, ffn1_norm_post, wide, gwide, lay, "ffn1")

    sel = jnp.reshape(cc, (1,)).astype(jnp.int32)
    rwide, rnarrow = _swap_halves(gwide, gnarrow, name="grad_swap_sibling")
    swide = _add_sibling(gwide, rwide, sel, name="grad_add_sibling_wide")
    snarrow = _add_sibling(gnarrow, rnarrow, sel, name="grad_add_sibling_narrow")
    pwide, pnarrow = _exchange_chips(swide, snarrow, lay, name="grad_exchange_chips")
    fwide, fnarrow = _join_halves(_add_chips(pwide, name="grad_add_chips_wide"),
                                  _add_chips(pnarrow, name="grad_add_chips_narrow"), name="grad_join_halves")
    for n in BIG_WEIGHTS:
        if n == "mla_w_q_up":
            g_n = fnarrow[:, lay.loff_q:lay.loff_kv]
        elif n == "mla_w_kv_up":
            g_n = fnarrow[:, lay.loff_kv:]
        else:
            g_n = fwide[:, lay.loff[n]:lay.loff[n] + lay.lrows[n]]
        rows_n, hw_n = g_n.shape[1:]
        grads[n] = (g_n.transpose(0, 2, 1).reshape(2 * hw_n, rows_n) if n in COL_SHARDED
                    else g_n.transpose(1, 0, 2).reshape(rows_n, 2 * hw_n))

    dlb = dlb_h.reshape(1, -1)
    dnorm = jnp.sum(dnorm_h, axis=0)
    small = {**{n: grads[n] for n in SMALL_WEIGHTS if n not in ("hgrn_lb_logits", "hgrn_out_norm")},
             "hgrn_lb_logits": dlb, "hgrn_out_norm": dnorm}
    vec = jnp.concatenate([small[n] for n in SMALL_WEIGHTS], axis=1)
    vec = _all_reduce_small(vec, name="grad_all_reduce_small")
    off = 0
    for n in SMALL_WEIGHTS:
        w_n = small[n].shape[1]
        grads[n] = vec[:, off:off + w_n]
        off += w_n
    grads["hgrn_lb_logits"] = _lb_logits_grad(hgrn_lb_logits, grads["hgrn_lb_logits"], name="lb_logits_grad")

    deltas, new_m, new_v = {}, {}, {}
    for n in ALL_WEIGHTS:
        w_n = wts[n]
        shp = w_n.shape
        two_d = (lambda a: a[0]) if n in BIG_WEIGHTS else (lambda a: a)
        dl, nm, nv = _adamw(two_d(w_n), grads[n], two_d(mom[n]), two_d(var[n]), name=f"adamw_{n}")
        grads[n] = grads[n].reshape(shp)
        deltas[n], new_m[n], new_v[n] = dl.reshape(shp), nm.reshape(shp), nv.reshape(shp)

    loss = lax.psum(loss_local, ("x", "y", "c"))
    dx_out = dx0.reshape(x.shape)
    return (loss, dx_out, *[grads[n] for n in ALL_WEIGHTS], *[deltas[n] for n in ALL_WEIGHTS],
            *[new_m[n] for n in ALL_WEIGHTS], *[new_v[n] for n in ALL_WEIGHTS])
```

```python
import functools

import jax
import jax.numpy as jnp
from jax import lax
from jax.experimental import pallas as pl
from jax.experimental.pallas import tpu as pltpu

F32 = jnp.float32
BF16 = jnp.bfloat16
MESH = pl.DeviceIdType.MESH

NORM_EPS = 1e-6
MACARON_SCALE = 0.5
ROPE_THETA = 10000.0
HEAD = 128
ROPE = 64
QGROUP = 2 * HEAD
SUB = 16
ADAM_LR, ADAM_B1, ADAM_B2, ADAM_EPS, ADAM_WD, ADAM_STEP = 0.001, 0.9, 0.999, 1e-08, 0.01, 10

LANE = 128
VMEM_LIMIT = 48 * 1024 * 1024
MM_TILE = 1024
MM_TILE_WIDE = 1536

BIG_WEIGHTS = ("ffn1_w_gate", "ffn1_w_up", "ffn1_w_down", "w_in", "mla_w_q_up", "mla_w_kv_up",
               "mla_w_o", "hgrn_w_o", "w_out", "ffn2_w_gate", "ffn2_w_up", "ffn2_w_down")
COL_SHARDED = ("ffn1_w_gate", "ffn1_w_up", "w_in", "mla_w_q_up", "mla_w_kv_up", "ffn2_w_gate", "ffn2_w_up")
SMALL_WEIGHTS = ("ffn1_norm_pre", "ffn1_norm_post", "mix_norm_pre", "mla_q_norm", "mla_kv_norm",
                 "hgrn_lb_logits", "hgrn_out_norm", "mix_norm_post", "ffn2_norm_pre", "ffn2_norm_post")
ALL_WEIGHTS = ("ffn1_norm_pre", "ffn1_w_gate", "ffn1_w_up", "ffn1_w_down", "ffn1_norm_post", "mix_norm_pre",
               "w_in", "mla_q_norm", "mla_w_q_up", "mla_kv_norm", "mla_w_kv_up", "mla_w_o", "hgrn_lb_logits",
               "hgrn_out_norm", "hgrn_w_o", "w_out", "mix_norm_post", "ffn2_norm_pre", "ffn2_w_gate",
               "ffn2_w_up", "ffn2_w_down", "ffn2_norm_post")


def _params(*sem):
    return pltpu.CompilerParams(dimension_semantics=sem or None, vmem_limit_bytes=VMEM_LIMIT)


def _pick(n, cap, offset=0):
    if n <= cap and offset % n == 0:
        return n
    best = None
    for t in range(LANE, min(n, cap) + 1, LANE):
        if n % t == 0 and offset % t == 0:
            best = t
    assert best is not None, (n, cap, offset)
    return best


def _row_tile(n, row_bytes, budget=1 << 20):
    best = None
    for t in range(8, n + 1, 8):
        if n % t == 0 and t * row_bytes <= budget:
            best = t
    return n if best is None else best


def _sigmoid(x):
    return 1.0 / (1.0 + jnp.exp(-x))


def _silu(x):
    return x * _sigmoid(x)


def _dsilu(x):
    s = _sigmoid(x)
    return s * (1.0 + x * (1.0 - s))


def _mm(pairs, *, name, mode="nn", out_dtype=F32, into=None):
    halves = isinstance(pairs[0][1], tuple)
    assert halves or mode == "tn"
    pairs = [(a, b if halves else (b, 0, b.shape[0])) for a, b in pairs]
    a0, (b0, b_off, b_rows) = pairs[0]
    hw = b0.shape[2] if halves else (into[0].shape[2] if into is not None else None)
    if mode == "nn":
        (m, kdim), n = a0.shape, 2 * hw
    elif mode == "nt":
        (m, kdim), n = a0.shape, b_rows
        assert kdim == 2 * hw
    else:
        (kdim, m), n = a0.shape, b0.shape[1]
    out_off = 0 if into is None else into[1]
    tm = _pick(m, MM_TILE_WIDE if mode == "tn" else MM_TILE, out_off)
    tn = hw if (mode == "nn" or into is not None) else _pick(n, MM_TILE_WIDE, b_off if mode == "nt" else 0)
    tk = hw if mode == "nt" else _pick(kdim, MM_TILE, b_off if mode == "nn" else 0)
    assert n % tn == 0 and kdim % tk == 0
    nk = kdim // tk
    npair = len(pairs)
    dims = {"nn": (((1,), (0,)), ((), ())), "nt": (((1,), (1,)), ((), ())), "tn": (((0,), (0,)), ((), ()))}[mode]

    def body(*refs):
        ins, o_ref, acc_ref = refs[:2 * npair], refs[-2], refs[-1]
        k = pl.program_id(2)

        @pl.when(k == 0)
        def _():
            acc_ref[...] = jnp.zeros_like(acc_ref)

        for p in range(npair):
            a = ins[2 * p][...].astype(BF16)
            b = ins[2 * p + 1][...].astype(BF16)
            acc_ref[...] += lax.dot_general(a, b, dims, preferred_element_type=F32)

        @pl.when(k == nk - 1)
        def _():
            o_ref[...] = acc_ref[...].astype(o_ref.dtype)

    a_spec = pl.BlockSpec((tk, tm), lambda i, j, k: (k, i)) if mode == "tn" else pl.BlockSpec((tm, tk), lambda i, j, k: (i, k))
    in_specs, flat = [], []
    for a, (b, off, _) in pairs:
        if mode == "nt":
            b_spec = pl.BlockSpec((None, tn, tk), lambda i, j, k, o=off // tn: (k, j + o, 0))
        elif mode == "nn":
            b_spec = pl.BlockSpec((None, tk, tn), lambda i, j, k, o=off // tk: (j, k + o, 0))
        else:
            b_spec = pl.BlockSpec((tk, tn), lambda i, j, k: (k, j))
        in_specs += [a_spec, b_spec]
        flat += [a, b]
    if into is None:
        out_shape, aliases = jax.ShapeDtypeStruct((m, n), out_dtype), {}
        out_spec = pl.BlockSpec((tm, tn), lambda i, j, k: (i, j))
    else:
        out_shape, aliases = jax.ShapeDtypeStruct(into[0].shape, into[0].dtype), {len(flat): 0}
        out_spec = pl.BlockSpec((None, tm, tn), lambda i, j, k, o=out_off // tm: (j, i + o, 0))
        in_specs.append(pl.BlockSpec(memory_space=pl.ANY))
        flat.append(into[0])
    return pl.pallas_call(
        body, name=name, grid=(m // tm, n // tn, nk),
        in_specs=in_specs,
        out_specs=out_spec,
        out_shape=out_shape, input_output_aliases=aliases,
        scratch_shapes=[pltpu.VMEM((tm, tn), F32)],
        compiler_params=_params("parallel", "parallel", "arbitrary"),
    )(*flat)


def _norm_fwd(y, w, *, name, resid=None, scale=1.0, out_dtype=F32, col=0):
    t, d = y.shape[0], w.shape[1]
    tr = _pick(t, 256)
    assert col % d == 0

    def body(*refs):
        if resid is None:
            y_ref, w_ref, o_ref = refs
        else:
            y_ref, w_ref, r_ref, o_ref = refs
        yv = y_ref[...]
        out = yv * lax.rsqrt(jnp.mean(yv * yv, axis=-1, keepdims=True) + NORM_EPS) * w_ref[...]
        if resid is not None:
            out = r_ref[...] + scale * out
        o_ref[...] = out.astype(out_dtype)

    row = pl.BlockSpec((tr, d), lambda i: (i, 0))
    wspec = pl.BlockSpec((1, d), lambda i: (0, 0))
    ins, specs = [y, w], [pl.BlockSpec((tr, d), lambda i: (i, col // d)), wspec]
    if resid is not None:
        ins.append(resid)
        specs.append(row)
    return pl.pallas_call(
        body, name=name, grid=(t // tr,), in_specs=specs, out_specs=row,
        out_shape=jax.ShapeDtypeStruct((t, d), out_dtype), compiler_params=_params("parallel"),
    )(*ins)


def _norm_bwd(x, w, dy, *, name, scale=1.0, dres=None, col=0, dx_dtype=F32):
    t, d = x.shape[0], w.shape[1]
    tr = _pick(t, 256)
    assert col % d == 0

    def body(*refs):
        if dres is None:
            x_ref, w_ref, dy_ref, dx_ref, dw_ref = refs
        else:
            x_ref, w_ref, dy_ref, dr_ref, dx_ref, dw_ref = refs

        @pl.when(pl.program_id(0) == 0)
        def _():
            dw_ref[...] = jnp.zeros_like(dw_ref)

        xv = x_ref[...]
        r = lax.rsqrt(jnp.mean(xv * xv, axis=-1, keepdims=True) + NORM_EPS)
        xhat = xv * r
        dyv = dy_ref[...].astype(F32) * scale
        dw_ref[...] += jnp.sum(dyv * xhat, axis=0, keepdims=True)
        t_ = dyv * w_ref[...]
        dx = r * (t_ - xhat * jnp.mean(t_ * xhat, axis=-1, keepdims=True))
        if dres is not None:
            dx = dx + dr_ref[...]
        dx_ref[...] = dx.astype(dx_dtype)

    row = pl.BlockSpec((tr, d), lambda i: (i, 0))
    wspec = pl.BlockSpec((1, d), lambda i: (0, 0))
    ins, specs = [x, w, dy], [pl.BlockSpec((tr, d), lambda i: (i, col // d)), wspec, row]
    if dres is not None:
        ins.append(dres)
        specs.append(row)
    return pl.pallas_call(
        body, name=name, grid=(t // tr,), in_specs=specs, out_specs=(row, wspec),
        out_shape=(jax.ShapeDtypeStruct((t, d), dx_dtype), jax.ShapeDtypeStruct((1, d), F32)),
        compiler_params=_params("arbitrary"),
    )(*ins)


def _elementwise(fn, ins, out_dtypes, *, name, width=None, cols=None):
    t = ins[0].shape[0]
    d = ins[0].shape[1] if width is None else width
    cols = [0] * len(ins) if cols is None else cols
    tc = _pick(d, 2048)
    for c in cols:
        tc = _pick(d, tc, c)
    tr = _row_tile(t, tc * 4)
    nout = len(out_dtypes)

    def body(*refs):
        outs = fn(*[r[...].astype(F32) for r in refs[:len(ins)]])
        for o_ref, o in zip(refs[len(ins):], outs):
            o_ref[...] = o.astype(o_ref.dtype)

    spec = pl.BlockSpec((tr, tc), lambda i, j: (i, j))
    in_specs = [pl.BlockSpec((tr, tc), lambda i, j, o=c // tc: (i, j + o)) for c in cols]
    return pl.pallas_call(
        body, name=name, grid=(t // tr, d // tc), in_specs=in_specs, out_specs=[spec] * nout,
        out_shape=[jax.ShapeDtypeStruct((t, d), dt) for dt in out_dtypes],
        compiler_params=_params("parallel", "parallel"),
    )(*ins)


def _swiglu_fwd(g, u, *, name):
    return _elementwise(lambda gv, uv: (_silu(gv) * uv,), [g, u], [BF16], name=name)[0]


def _swiglu_bwd(da, g, u, *, name):
    return _elementwise(lambda dav, gv, uv: (dav * uv * _dsilu(gv), dav * _silu(gv)), [da, g, u], [BF16, BF16], name=name)


def _merge_fwd(proj, col_a, col_b, ya, yb, *, name):
    return _elementwise(lambda a, b, p, q: (_sigmoid(a) * p + _sigmoid(b) * q,), [proj, proj, ya, yb], [BF16],
                        name=name, width=ya.shape[1], cols=[col_a, col_b, 0, 0])[0]


def _merge_bwd(dm, proj, col_a, col_b, ya, yb, *, name):
    def fn(dmv, a, b, p, q):
        sa, sb = _sigmoid(a), _sigmoid(b)
        return dmv * p * sa * (1.0 - sa), dmv * q * sb * (1.0 - sb), dmv * sa, dmv * sb

    return _elementwise(fn, [dm, proj, proj, ya, yb], [BF16, BF16, BF16, BF16], name=name, width=ya.shape[1],
                        cols=[0, col_a, col_b, 0, 0])


def _loss_head(xo, target, *, name):
    t, d = xo.shape
    tr = _pick(t, 256)

    def body(x_ref, t_ref, dx_ref, l_ref):
        @pl.when(pl.program_id(0) == 0)
        def _():
            l_ref[...] = jnp.zeros_like(l_ref)

        err = x_ref[...] - t_ref[...]
        dx_ref[...] = err * (1.0 / d)
        l_ref[...] += 0.5 * jnp.sum(jnp.mean(err * err, axis=-1, keepdims=True), axis=0, keepdims=True)

    row = pl.BlockSpec((tr, d), lambda i: (i, 0))
    dx, l = pl.pallas_call(
        body, name=name, grid=(t // tr,), in_specs=[row, row],
        out_specs=(row, pl.BlockSpec((1, 1), lambda i: (0, 0))),
        out_shape=(jax.ShapeDtypeStruct((t, d), F32), jax.ShapeDtypeStruct((1, 1), F32)),
        compiler_params=_params("arbitrary"),
    )(xo, target)
    return dx, l[0, 0]


def _rope(xin, tabs, *, name, group, backward, out_dtype, col=0, ngroup=None):
    t = xin.shape[0]
    ngroup = xin.shape[1] // group if ngroup is None else ngroup
    wdt = ngroup * group
    tr = _pick(t, 256)
    assert col % group == 0
    cos_t, nsin_t, sin_t = tabs

    def body(x_ref, c_ref, n_ref, s_ref, o_ref):
        xv = x_ref[...].astype(F32)
        rot = xv[:, group - LANE:]
        if backward:
            out = rot * c_ref[...] + pltpu.roll(rot * n_ref[...], 32, 1) + pltpu.roll(rot * s_ref[...], LANE - 32, 1)
        else:
            out = rot * c_ref[...] + pltpu.roll(rot, LANE - 32, 1) * n_ref[...] + pltpu.roll(rot, 32, 1) * s_ref[...]
        if group > LANE:
            out = jnp.concatenate([xv[:, :group - LANE], out], axis=1)
        o_ref[...] = out.astype(out_dtype)

    xspec = pl.BlockSpec((tr, group), lambda i, g: (i, g))
    tspec = pl.BlockSpec((tr, LANE), lambda i, g: (i, 0))
    return pl.pallas_call(
        body, name=name, grid=(t // tr, ngroup),
        in_specs=[pl.BlockSpec((tr, group), lambda i, g: (i, g + col // group)), tspec, tspec, tspec], out_specs=xspec,
        out_shape=jax.ShapeDtypeStruct((t, wdt), out_dtype), compiler_params=_params("parallel", "parallel"),
    )(xin, cos_t, nsin_t, sin_t)


def _scores(q, kv, kr, qi, tq, scale):
    kcat = jnp.concatenate([kv[:, :HEAD], kr], axis=1)
    s = lax.dot_general(q, kcat, (((1,), (1,)), ((), ())), preferred_element_type=F32) * scale
    row = qi * tq + lax.broadcasted_iota(jnp.int32, s.shape, 0)
    col = lax.broadcasted_iota(jnp.int32, s.shape, 1)
    s = jnp.where(col <= row, s, -jnp.inf)
    p = jnp.exp(s - jnp.max(s, axis=-1, keepdims=True))
    return p / jnp.sum(p, axis=-1, keepdims=True), kcat


def _attn_fwd(qcat, kv, kr, *, name, scale):
    t = qcat.shape[0]
    nh = qcat.shape[1] // QGROUP
    tq = _pick(t, 256)

    def body(q_ref, kv_ref, kr_ref, o_ref):
        kvv = kv_ref[...]
        p, _ = _scores(q_ref[...], kvv, kr_ref[...], pl.program_id(1), tq, scale)
        o_ref[...] = jnp.dot(p.astype(BF16), kvv[:, HEAD:], preferred_element_type=F32).astype(BF16)

    return pl.pallas_call(
        body, name=name, grid=(nh, t // tq),
        in_specs=[pl.BlockSpec((tq, QGROUP), lambda h, i: (i, h)), pl.BlockSpec((t, QGROUP), lambda h, i: (0, h)),
                  pl.BlockSpec((t, LANE), lambda h, i: (0, 0))],
        out_specs=pl.BlockSpec((tq, HEAD), lambda h, i: (i, h)),
        out_shape=jax.ShapeDtypeStruct((t, nh * HEAD), BF16), compiler_params=_params("parallel", "parallel"),
    )(qcat, kv, kr)


def _attn_bwd(qcat, kv, kr, do, *, name, scale):
    t = qcat.shape[0]
    nh = qcat.shape[1] // QGROUP
    tq = _pick(t, 256)
    nq = t // tq

    def body(q_ref, kv_ref, kr_ref, do_ref, dq_ref, dkv_ref, dkr_ref, dk_acc, dv_acc):
        h, i = pl.program_id(0), pl.program_id(1)

        @pl.when(i == 0)
        def _():
            dk_acc[...] = jnp.zeros_like(dk_acc)
            dv_acc[...] = jnp.zeros_like(dv_acc)

        @pl.when((i == 0) & (h == 0))
        def _():
            dkr_ref[...] = jnp.zeros_like(dkr_ref)

        q = q_ref[...]
        kvv = kv_ref[...]
        dov = do_ref[...].astype(BF16)
        p, kcat = _scores(q, kvv, kr_ref[...], i, tq, scale)
        dp = lax.dot_general(dov, kvv[:, HEAD:], (((1,), (1,)), ((), ())), preferred_element_type=F32)
        ds = (p * (dp - jnp.sum(p * dp, axis=-1, keepdims=True)) * scale).astype(BF16)
        dq_ref[...] = jnp.dot(ds, kcat, preferred_element_type=F32)
        dk_acc[...] += lax.dot_general(ds, q, (((0,), (0,)), ((), ())), preferred_element_type=F32)
        dv_acc[...] += lax.dot_general(p.astype(BF16), dov, (((0,), (0,)), ((), ())), preferred_element_type=F32)

        @pl.when(i == nq - 1)
        def _():
            dk = dk_acc[...]
            dkv_ref[...] = jnp.concatenate([dk[:, :HEAD], dv_acc[...]], axis=1)
            dkr_ref[...] += dk[:, HEAD:]

    return pl.pallas_call(
        body, name=name, grid=(nh, nq),
        in_specs=[pl.BlockSpec((tq, QGROUP), lambda h, i: (i, h)), pl.BlockSpec((t, QGROUP), lambda h, i: (0, h)),
                  pl.BlockSpec((t, LANE), lambda h, i: (0, 0)), pl.BlockSpec((tq, HEAD), lambda h, i: (i, h))],
        out_specs=(pl.BlockSpec((tq, QGROUP), lambda h, i: (i, h)), pl.BlockSpec((t, QGROUP), lambda h, i: (0, h)),
                   pl.BlockSpec((t, LANE), lambda h, i: (0, 0))),
        out_shape=(jax.ShapeDtypeStruct((t, nh * QGROUP), F32), jax.ShapeDtypeStruct((t, nh * QGROUP), F32),
                   jax.ShapeDtypeStruct((t, LANE), F32)),
        scratch_shapes=[pltpu.VMEM((t, QGROUP), F32), pltpu.VMEM((t, HEAD), F32)],
        compiler_params=_params("arbitrary", "arbitrary"),
    )(qcat, kv, kr, do)


def _split3(x):
    hi = x.astype(BF16)
    r1 = x - hi.astype(F32)
    mid = r1.astype(BF16)
    lo = (r1 - mid.astype(F32)).astype(BF16)
    return hi, mid, lo


def _tri_matmul(mask, x):
    m = mask.astype(BF16)
    return sum(jnp.dot(m, part, preferred_element_type=F32) for part in _split3(x))


def _sub_cumsum(g, tb):
    row = lax.broadcasted_iota(jnp.int32, (tb, tb), 0)
    col = lax.broadcasted_iota(jnp.int32, (tb, tb), 1)
    return _tri_matmul(jnp.where((col <= row) & (col // SUB == row // SUB), 1.0, 0.0), g)


def _sub_suffix_prefix(after, before, tb):
    row = lax.broadcasted_iota(jnp.int32, (tb, tb), 0)
    col = lax.broadcasted_iota(jnp.int32, (tb, tb), 1)
    same = col // SUB == row // SUB
    return (_tri_matmul(jnp.where((col >= row) & same, 1.0, 0.0), after)
            + _tri_matmul(jnp.where((col < row) & same, 1.0, 0.0), before))


def _lower_bound(logits):
    mx = jnp.max(logits, axis=0, keepdims=True)
    e = jnp.exp(logits - mx)
    return e[0:1, :] / jnp.sum(e, axis=0, keepdims=True)


def _hgrn_fwd(proj, cols, wdt, logits, out_norm, *, name):
    t = proj.shape[0]
    nh = wdt // HEAD
    tb = _pick(t, 128)
    ns = tb // SUB

    def body(hq_ref, hf_ref, hi_ref, hg_ref, lg_ref, w_ref, o_ref, yb_ref, st_ref, s_ref, q_s, k_s, b_s):
        @pl.when(pl.program_id(1) == 0)
        def _():
            s_ref[...] = jnp.zeros_like(s_ref)

        lb = _lower_bound(lg_ref[...])
        f = lb + (1.0 - lb) * _sigmoid(hf_ref[...])
        q_s[...] = _silu(hq_ref[...])
        k_s[...] = 1.0 - f
        b_s[...] = _sub_cumsum(jnp.log(f), tb)
        rowid = lax.broadcasted_iota(jnp.int32, (SUB, HEAD), 0)

        def sub(c, carry):
            rows = pl.ds(pl.multiple_of(c * SUB, SUB), SUB)
            qc, kc, bc, vc = q_s[rows, :], k_s[rows, :], b_s[rows, :], hi_ref[rows, :]
            st = s_ref[...]
            st_ref[0, c] = st
            bl = bc[SUB - 1:SUB, :]
            oc = lax.dot_general((qc * jnp.exp(bc)).astype(BF16), st.astype(BF16), (((1,), (1,)), ((), ())),
                                 preferred_element_type=F32)
            for s in range(SUB):
                e = jnp.where(rowid >= s, jnp.exp(bc - bc[s:s + 1, :]), 0.0)
                a = jnp.sum(qc * e * kc[s:s + 1, :], axis=1, keepdims=True)
                oc = oc + a * vc[s:s + 1, :]
            o_ref[rows, :] = oc
            kd = kc * jnp.exp(bl - bc)
            s_ref[...] = jnp.exp(bl) * st + lax.dot_general(vc.astype(BF16), kd.astype(BF16), (((0,), (0,)), ((), ())),
                                                             preferred_element_type=F32)
            return carry

        lax.fori_loop(0, ns, sub, 0)
        o = o_ref[...]
        r = lax.rsqrt(jnp.mean(o * o, axis=-1, keepdims=True) + NORM_EPS)
        yb_ref[...] = (o * r * w_ref[...] * _silu(hg_ref[...])).astype(BF16)

    blk = pl.BlockSpec((tb, HEAD), lambda h, j: (j, h))
    return pl.pallas_call(
        body, name=name, grid=(nh, t // tb),
        in_specs=[pl.BlockSpec((tb, HEAD), lambda h, j, o=c // HEAD: (j, h + o)) for c in cols]
        + [pl.BlockSpec((2, HEAD), lambda h, j: (0, h)), pl.BlockSpec((1, HEAD), lambda h, j: (0, 0))],
        out_specs=(blk, blk, pl.BlockSpec((1, ns, HEAD, HEAD), lambda h, j: (h, j, 0, 0))),
        out_shape=(jax.ShapeDtypeStruct((t, wdt), F32), jax.ShapeDtypeStruct((t, wdt), BF16),
                   jax.ShapeDtypeStruct((nh, t // SUB, HEAD, HEAD), F32)),
        scratch_shapes=[pltpu.VMEM((HEAD, HEAD), F32)] + [pltpu.VMEM((tb, HEAD), F32)] * 3,
        compiler_params=_params("parallel", "arbitrary"),
    )(proj, proj, proj, proj, logits, out_norm)


def _hgrn_bwd(proj, cols, wdt, o_raw, dyb, states, logits, out_norm, *, name):
    t = proj.shape[0]
    nh = wdt // HEAD
    tb = _pick(t, 128)
    ns = tb // SUB
    nb = t // tb

    def body(hq_ref, hf_ref, hi_ref, hg_ref, o_ref, dy_ref, st_ref, lg_ref, w_ref,
             dhq_ref, dhf_ref, dhi_ref, dhg_ref, dlb_ref, dw_ref,
             ds_ref, q_s, k_s, b_s, do_s, dq_s, dk_s, dv_s, after_s, before_s, thru_s):
        @pl.when(pl.program_id(1) == 0)
        def _():
            ds_ref[...] = jnp.zeros_like(ds_ref)
            dlb_ref[...] = jnp.zeros_like(dlb_ref)
            dw_ref[...] = jnp.zeros_like(dw_ref)

        lb = _lower_bound(lg_ref[...])
        hqv, hgv = hq_ref[...], hg_ref[...]
        sig = _sigmoid(hf_ref[...])
        f = lb + (1.0 - lb) * sig
        q_s[...] = _silu(hqv)
        k_s[...] = 1.0 - f
        b_s[...] = _sub_cumsum(jnp.log(f), tb)

        o = o_ref[...]
        r = lax.rsqrt(jnp.mean(o * o, axis=-1, keepdims=True) + NORM_EPS)
        nrm = o * r
        w = w_ref[...]
        dy = dy_ref[...].astype(F32)
        dhg_ref[...] = (dy * nrm * w * _dsilu(hgv)).astype(BF16)
        dnw = dy * _silu(hgv)
        dw_ref[0] += jnp.sum(dnw * nrm, axis=0, keepdims=True)
        tt = dnw * w
        do_s[...] = r * (tt - nrm * jnp.mean(tt * nrm, axis=-1, keepdims=True))
        rowid = lax.broadcasted_iota(jnp.int32, (SUB, HEAD), 0)

        def sub(cc, carry):
            c = ns - 1 - cc
            rows = pl.ds(pl.multiple_of(c * SUB, SUB), SUB)
            qc, kc, bc, vc, doc = q_s[rows, :], k_s[rows, :], b_s[rows, :], hi_ref[rows, :], do_s[rows, :]
            st = st_ref[0, c]
            dst = ds_ref[...]
            bl = bc[SUB - 1:SUB, :]
            eb = jnp.exp(bc)
            ekd = jnp.exp(bl - bc)
            qe, kd = qc * eb, kc * ekd
            dob, vcb = doc.astype(BF16), vc.astype(BF16)
            dq_st = jnp.dot(dob, st.astype(BF16), preferred_element_type=F32) * eb
            dk_st = jnp.dot(vcb, dst.astype(BF16), preferred_element_type=F32) * ekd
            dv = lax.dot_general(kd.astype(BF16), dst.astype(BF16), (((1,), (1,)), ((), ())), preferred_element_type=F32)
            dq_in = jnp.zeros_like(qc)
            dk_in = jnp.zeros_like(qc)
            for s in range(SUB):
                e = jnp.where(rowid >= s, jnp.exp(bc - bc[s:s + 1, :]), 0.0)
                ek = e * kc[s:s + 1, :]
                a = jnp.sum(qc * ek, axis=1, keepdims=True)
                da = jnp.sum(doc * vc[s:s + 1, :], axis=1, keepdims=True)
                dq_in = dq_in + da * ek
                dk_in = dk_in + jnp.where(rowid == s, jnp.sum(da * e * qc, axis=0, keepdims=True), 0.0)
                dv = dv + jnp.where(rowid == s, jnp.sum(a * doc, axis=0, keepdims=True), 0.0)
            ebl = jnp.exp(bl)
            ds_ref[...] = ebl * dst + lax.dot_general(dob, qe.astype(BF16), (((0,), (0,)), ((), ())),
                                                      preferred_element_type=F32)
            dq_s[rows, :] = dq_st + dq_in
            dk_s[rows, :] = dk_st + dk_in
            dv_s[rows, :] = dv
            after_s[rows, :] = qc * (dq_st + dq_in) - kc * dk_in
            before_s[rows, :] = kc * dk_st
            thru_s[rows, :] = jnp.broadcast_to(ebl * jnp.sum(st * dst, axis=0, keepdims=True), (SUB, HEAD))
            return carry

        lax.fori_loop(0, ns, sub, 0)
        dg = _sub_suffix_prefix(after_s[...], before_s[...], tb) + thru_s[...]
        dhq_ref[...] = (dq_s[...] * _dsilu(hqv)).astype(BF16)
        dft = dg / f - dk_s[...]
        dhf_ref[...] = (dft * (1.0 - lb) * sig * (1.0 - sig)).astype(BF16)
        dlb_ref[0] += jnp.sum(dft * (1.0 - sig), axis=0, keepdims=True)
        dhi_ref[...] = dv_s[...].astype(BF16)

    blk = pl.BlockSpec((tb, HEAD), lambda h, j: (nb - 1 - j, h))
    vec = pl.BlockSpec((1, 1, HEAD), lambda h, j: (h, 0, 0))
    tok = jax.ShapeDtypeStruct((t, wdt), BF16)
    per_head = jax.ShapeDtypeStruct((nh, 1, HEAD), F32)
    return pl.pallas_call(
        body, name=name, grid=(nh, nb),
        in_specs=[pl.BlockSpec((tb, HEAD), lambda h, j, o=c // HEAD: (nb - 1 - j, h + o)) for c in cols]
        + [blk, blk] + [pl.BlockSpec((1, ns, HEAD, HEAD), lambda h, j: (h, nb - 1 - j, 0, 0)),
                              pl.BlockSpec((2, HEAD), lambda h, j: (0, h)), pl.BlockSpec((1, HEAD), lambda h, j: (0, 0))],
        out_specs=(blk, blk, blk, blk, vec, vec),
        out_shape=(tok, tok, tok, tok, per_head, per_head),
        scratch_shapes=[pltpu.VMEM((HEAD, HEAD), F32)] + [pltpu.VMEM((tb, HEAD), F32)] * 10,
        compiler_params=_params("arbitrary", "arbitrary"),
    )(proj, proj, proj, proj, o_raw, dyb, states, logits, out_norm)


def _lb_logits_grad(logits, dlb, *, name):
    def body(lg_ref, d_ref, o_ref):
        lg = lg_ref[...]
        e = jnp.exp(lg - jnp.max(lg, axis=0, keepdims=True))
        p = e / jnp.sum(e, axis=0, keepdims=True)
        d = d_ref[...]
        rowid = lax.broadcasted_iota(jnp.int32, lg.shape, 0)
        dp = jnp.where(rowid == 0, d, 0.0)
        o_ref[...] = p * (dp - jnp.sum(p * dp, axis=0, keepdims=True))

    return pl.pallas_call(body, name=name, out_shape=jax.ShapeDtypeStruct(logits.shape, F32))(logits, dlb)


def _adamw(w, g, m, v, *, name):
    r, c = w.shape
    tc = _pick(c, 2048) if c % LANE == 0 else c
    tr = _row_tile(r, tc * 4)

    def body(w_ref, g_ref, m_ref, v_ref, d_ref, nm_ref, nv_ref):
        gv = g_ref[...]
        nm = ADAM_B1 * m_ref[...] + (1.0 - ADAM_B1) * gv
        nv = ADAM_B2 * v_ref[...] + (1.0 - ADAM_B2) * (gv * gv)
        m_hat = nm / (1.0 - ADAM_B1 ** ADAM_STEP)
        v_hat = nv / (1.0 - ADAM_B2 ** ADAM_STEP)
        d_ref[...] = -ADAM_LR * (m_hat / (jnp.sqrt(v_hat) + ADAM_EPS) + ADAM_WD * w_ref[...])
        nm_ref[...] = nm
        nv_ref[...] = nv

    spec = pl.BlockSpec((tr, tc), lambda i, j: (i, j))
    shp = jax.ShapeDtypeStruct((r, c), F32)
    return pl.pallas_call(
        body, name=name, grid=(r // tr, c // tc), in_specs=[spec] * 4, out_specs=[spec] * 3,
        out_shape=[shp, shp, shp], compiler_params=_params("parallel", "parallel"),
    )(w, g, m, v)


def _coords():
    return lax.axis_index("x"), lax.axis_index("y"), lax.axis_index("c")


def _other_chips(x, y):
    return [(1 - x, y), (x, 1 - y), (1 - x, 1 - y)]


ANY = pl.BlockSpec(memory_space=pl.ANY)


class _Layout:
    def __init__(self, d, dff, in_cols, q_lora, kv_lora, nh):
        assert q_lora == kv_lora and nh % 4 == 0 and dff % 4 == 0 and in_cols % 4 == 0 and d % 4 == 0
        self.d, self.dff, self.q_lora, self.nh = d, dff, q_lora, nh
        self.head = q_lora + kv_lora + ROPE
        self.pad = d - self.head
        self.dffp = -(-dff // LANE) * LANE
        self.ffn_pad = self.dffp - dff
        dffp = self.dffp
        nff, ncol, r_o, hps = dff // 4, in_cols // 4, d // 4, nh // 4
        assert self.head <= ncol
        names = ("ffn1_w_gate", "ffn1_w_up", "ffn1_w_down", "ffn2_w_gate", "ffn2_w_up", "ffn2_w_down")
        self.ffn_names = names
        self.off = {n: i * dffp for i, n in enumerate(names)}
        self.off.update(w_in=6 * dffp, mla_w_o=6 * dffp + 7 * d, hgrn_w_o=6 * dffp + 8 * d, w_out=6 * dffp + 9 * d)
        self.rows_wide = 6 * dffp + 10 * d
        self.off_q, self.off_kv, self.rows_narrow = 0, nh * QGROUP, 2 * nh * QGROUP
        self.loff = {n: i * nff for i, n in enumerate(names)}
        self.loff.update(w_in=6 * nff, mla_w_o=6 * nff + ncol, hgrn_w_o=6 * nff + ncol + r_o, w_out=6 * nff + ncol + 2 * r_o)
        self.lrows = {n: nff for n in names}
        self.lrows.update(w_in=ncol, mla_w_o=r_o, hgrn_w_o=r_o, w_out=r_o)
        self.lrows_wide = 6 * nff + ncol + 3 * r_o
        self.lrows_narrow = hps * (HEAD + ROPE) + hps * QGROUP
        self.loff_q, self.loff_kv = 0, hps * (HEAD + ROPE)
        self.nff, self.ncol, self.r_o, self.hps = nff, ncol, r_o, hps

    def segments(self, k):
        first = lambda a, b: jnp.where(k == 0, a, b) if not isinstance(k, int) else (a if k == 0 else b)
        segs = []
        for n in ("ffn1_w_gate", "ffn1_w_up", "ffn1_w_down"):
            segs.append((0, self.loff[n], self.nff, self.off[n] + self.nff * k, 0))
        w_in = self.off["w_in"]
        segs.append((0, self.loff["w_in"], self.head, w_in + first(0, self.ncol * k + self.pad), 1))
        segs.append((0, self.loff["w_in"] + self.head, self.ncol - self.head, w_in + self.ncol * k + self.d, 1))
        for n in ("mla_w_o", "hgrn_w_o", "w_out"):
            segs.append((0, self.loff[n], self.r_o, self.off[n] + self.r_o * k, 1))
        for hh in range(self.hps):
            segs.append((1, (HEAD + ROPE) * hh, HEAD + ROPE, QGROUP * (self.hps * k + hh), 2))
        segs.append((1, self.loff_kv, self.hps * QGROUP, self.off_kv + self.hps * QGROUP * k, 2))
        for n in ("ffn2_w_gate", "ffn2_w_up", "ffn2_w_down"):
            segs.append((0, self.loff[n], self.nff, self.off[n] + self.nff * k, 3))
        return segs

    def stream_rows(self, stream):
        return sum(s[2] for s in self.segments(0) if s[4] == stream)


N_STREAM = 4


def _half(ref, row, rows, half):
    return ref.at[half, pl.ds(row, rows)]


def _both(ref, row, rows):
    return ref.at[:, pl.ds(row, rows)]


def _wait_bytes_of(ref_like, send_sem, recv_sem, me, *, send):
    cp = pltpu.make_async_remote_copy(src_ref=ref_like, dst_ref=ref_like, send_sem=send_sem, recv_sem=recv_sem,
                                      device_id=me, device_id_type=MESH)
    if send:
        cp.wait_send()
    else:
        cp.wait_recv()


def _gather_weights(lwide, lnarrow, zwide, znarrow, lay, *, name):
    d, ql = lay.d, lay.q_lora
    qpad = QGROUP - HEAD - ROPE

    def body(l0, l1, z0, z1, w0, w1, send, recv, fsend, frecv, osend, orecv):
        x, y, c = _coords()
        me_chip = 2 * x + y
        chips = _other_chips(x, y)
        src, dst = (l0, l1), (w0, w1)

        def to_sibling(a, src_ref, drow, rows):
            pltpu.make_async_remote_copy(src_ref=src_ref, dst_ref=_both(dst[a], drow, rows), send_sem=osend.at[a],
                                         recv_sem=orecv.at[a], device_id=(x, y, 1 - c), device_id_type=MESH).start()

        for a, lrow, rows, drow, _ in lay.segments(me_chip):
            to_sibling(a, _both(src[a], lrow, rows), drow, rows)
        to_sibling(0, _both(z0, 0, lay.pad), lay.off["w_in"] + lay.head, lay.pad)
        if lay.ffn_pad:
            for n in lay.ffn_names:
                to_sibling(0, _both(z0, 0, lay.ffn_pad), lay.off[n] + lay.dff, lay.ffn_pad)
        for g in range(lay.nh):
            to_sibling(1, z1, QGROUP * g + HEAD + ROPE, qpad)

        for j, (px, py) in enumerate(chips):
            for a, lrow, rows, drow, st in lay.segments(me_chip):
                pltpu.make_async_remote_copy(
                    src_ref=_half(src[a], lrow, rows, c), dst_ref=_half(dst[a], drow, rows, c),
                    send_sem=send.at[j, st], recv_sem=recv.at[j, st], device_id=(px, py, c), device_id_type=MESH).start()

        def total(st):
            return _half(dst[1 if st == 2 else 0], 0, lay.stream_rows(st), 0)

        for st in range(N_STREAM):
            for j, (px, py) in enumerate(chips):
                _wait_bytes_of(total(st), send.at[j, st], recv.at[j, st], (x, y, c), send=False)
                for a, lrow, rows, drow, s2 in lay.segments(2 * px + py):
                    if s2 == st:
                        blk = _half(dst[a], drow, rows, c)
                        pltpu.make_async_remote_copy(
                            src_ref=blk, dst_ref=blk, send_sem=fsend.at[j, st], recv_sem=frecv.at[j, st],
                            device_id=(x, y, 1 - c), device_id_type=MESH).start()
        for st in range(N_STREAM):
            for j in range(3):
                _wait_bytes_of(total(st), fsend.at[j, st], frecv.at[j, st], (x, y, c), send=False)
        for st in range(N_STREAM):
            for j in range(3):
                _wait_bytes_of(total(st), fsend.at[j, st], frecv.at[j, st], (x, y, c), send=True)
                _wait_bytes_of(total(st), send.at[j, st], recv.at[j, st], (x, y, c), send=True)
        own = (_both(w0, 0, lay.lrows_wide + lay.pad + 6 * lay.ffn_pad), _both(w1, 0, lay.lrows_narrow + lay.nh * qpad))
        for a in range(2):
            _wait_bytes_of(own[a], osend.at[a], orecv.at[a], (x, y, c), send=False)
            _wait_bytes_of(own[a], osend.at[a], orecv.at[a], (x, y, c), send=True)

    sem = pltpu.SemaphoreType.DMA((3, N_STREAM))
    return pl.pallas_call(
        body, name=name, in_specs=[ANY] * 4, out_specs=[ANY] * 2,
        out_shape=[jax.ShapeDtypeStruct((2, lay.rows_wide, d // 2), BF16),
                   jax.ShapeDtypeStruct((2, lay.rows_narrow, ql // 2), BF16)],
        scratch_shapes=[sem, sem, sem, sem, pltpu.SemaphoreType.DMA((2,)), pltpu.SemaphoreType.DMA((2,))],
    )(lwide, lnarrow, zwide, znarrow)


def _swap_halves(gwide, gnarrow, *, name):
    def body(g0, g1, r0, r1, send_sems, recv_sems):
        x, y, c = _coords()
        copies = []
        for a, (g, r) in enumerate(((g0, r0), (g1, r1))):
            copies.append(pltpu.make_async_remote_copy(
                src_ref=g.at[1 - c], dst_ref=r, send_sem=send_sems.at[a], recv_sem=recv_sems.at[a],
                device_id=(x, y, 1 - c), device_id_type=MESH))
        for cp in copies:
            cp.start()
        for cp in copies:
            cp.wait()

    return pl.pallas_call(
        body, name=name, in_specs=[ANY] * 2, out_specs=[ANY] * 2,
        out_shape=[jax.ShapeDtypeStruct(g.shape[1:], g.dtype) for g in (gwide, gnarrow)],
        scratch_shapes=[pltpu.SemaphoreType.DMA((2,)), pltpu.SemaphoreType.DMA((2,))],
    )(gwide, gnarrow)


def _add_sibling(g, recv, sel, *, name):
    rows, hw = recv.shape
    tr = _row_tile(rows, hw * 4)

    def body(sel_ref, g_ref, r_ref, o_ref):
        o_ref[...] = (g_ref[...] + r_ref[...]).astype(BF16)

    return pl.pallas_call(
        body, name=name, out_shape=jax.ShapeDtypeStruct((rows, hw), BF16),
        grid_spec=pltpu.PrefetchScalarGridSpec(
            num_scalar_prefetch=1, grid=(rows // tr,),
            in_specs=[pl.BlockSpec((None, tr, hw), lambda i, s: (s[0], i, 0)), pl.BlockSpec((tr, hw), lambda i, s: (i, 0))],
            out_specs=pl.BlockSpec((tr, hw), lambda i, s: (i, 0))),
        compiler_params=_params("parallel"),
    )(sel, g, recv)


def _exchange_chips(swide, snarrow, lay, *, name):
    def body(s0, s1, r0, r1, send, recv, lsem):
        x, y, c = _coords()
        me_chip = 2 * x + y
        src, dst = (s0, s1), (r0, r1)
        for a, lrow, rows, drow, _ in lay.segments(me_chip):
            pltpu.make_async_copy(src[a].at[pl.ds(drow, rows)], dst[a].at[me_chip, pl.ds(lrow, rows)], lsem.at[a]).start()
        for j, (px, py) in enumerate(_other_chips(x, y)):
            for a, lrow, rows, drow, _ in lay.segments(2 * px + py):
                pltpu.make_async_remote_copy(
                    src_ref=src[a].at[pl.ds(drow, rows)], dst_ref=dst[a].at[me_chip, pl.ds(lrow, rows)],
                    send_sem=send.at[j, a], recv_sem=recv.at[j, a], device_id=(px, py, c), device_id_type=MESH).start()
        for a in range(2):
            for j in range(3):
                _wait_bytes_of(dst[a].at[0], send.at[j, a], recv.at[j, a], (x, y, c), send=False)
        for a in range(2):
            for j in range(3):
                _wait_bytes_of(dst[a].at[0], send.at[j, a], recv.at[j, a], (x, y, c), send=True)
            pltpu.make_async_copy(dst[a].at[0], dst[a].at[0], lsem.at[a]).wait()

    sem = pltpu.SemaphoreType.DMA((3, 2))
    return pl.pallas_call(
        body, name=name, in_specs=[ANY] * 2, out_specs=[ANY] * 2,
        out_shape=[jax.ShapeDtypeStruct((4, lay.lrows_wide, swide.shape[1]), BF16),
                   jax.ShapeDtypeStruct((4, lay.lrows_narrow, snarrow.shape[1]), BF16)],
        scratch_shapes=[sem, sem, pltpu.SemaphoreType.DMA((2,))],
    )(swide, snarrow)


def _add_chips(parts, sel, *, name):
    _, rows, wdt = parts.shape
    tr = _row_tile(rows, wdt * 4)

    def body(sel_ref, p_ref, o_ref):
        o_ref[...] = ((p_ref[0].astype(F32) + p_ref[1].astype(F32)) + p_ref[2].astype(F32)) + p_ref[3].astype(F32)

    return pl.pallas_call(
        body, name=name, out_shape=jax.ShapeDtypeStruct((2, rows, wdt), F32),
        grid_spec=pltpu.PrefetchScalarGridSpec(
            num_scalar_prefetch=1, grid=(rows // tr,),
            in_specs=[pl.BlockSpec((4, tr, wdt), lambda i, s: (0, i, 0))],
            out_specs=pl.BlockSpec((None, tr, wdt), lambda i, s: (s[0], i, 0))),
        compiler_params=_params("parallel"),
    )(sel, parts)


def _join_halves(fwide, fnarrow, *, name):
    def body(i0, i1, f0, f1, send_sems, recv_sems):
        x, y, c = _coords()
        copies = [pltpu.make_async_remote_copy(
            src_ref=f.at[c], dst_ref=f.at[c], send_sem=send_sems.at[a], recv_sem=recv_sems.at[a],
            device_id=(x, y, 1 - c), device_id_type=MESH) for a, f in enumerate((f0, f1))]
        for cp in copies:
            cp.start()
        for cp in copies:
            cp.wait()

    sem = pltpu.SemaphoreType.DMA((2,))
    return pl.pallas_call(
        body, name=name, in_specs=[ANY] * 2, out_specs=[ANY] * 2, input_output_aliases={0: 0, 1: 1},
        out_shape=[jax.ShapeDtypeStruct(f.shape, f.dtype) for f in (fwide, fnarrow)],
        scratch_shapes=[sem, sem],
    )(fwide, fnarrow)


def _all_reduce_small(vec, *, name):
    n = vec.shape[1]

    def body(v_ref, o_ref, buf, send_sems, recv_sems):
        x, y, c = _coords()
        me = 4 * x + 2 * y + c
        buf[me] = v_ref[...]
        copies = []
        for m in range(1, 8):
            peer = (x ^ ((m >> 2) & 1), y ^ ((m >> 1) & 1), c ^ (m & 1))
            copies.append(pltpu.make_async_remote_copy(
                src_ref=v_ref, dst_ref=buf.at[me], send_sem=send_sems.at[m - 1], recv_sem=recv_sems.at[m - 1],
                device_id=peer, device_id_type=MESH))
        for cp in copies:
            cp.start()
        for cp in copies:
            cp.wait()
        acc = buf[0]
        for d in range(1, 8):
            acc = acc + buf[d]
        o_ref[...] = acc

    return pl.pallas_call(
        body, name=name, out_shape=jax.ShapeDtypeStruct((1, n), F32),
        in_specs=[pl.BlockSpec(memory_space=pltpu.VMEM)], out_specs=pl.BlockSpec(memory_space=pltpu.VMEM),
        scratch_shapes=[pltpu.VMEM((8, 1, n), F32), pltpu.SemaphoreType.DMA((7,)), pltpu.SemaphoreType.DMA((7,))],
    )(vec)


def _ffn_fwd(x, n_pre, n_post, wide, lay, tag):
    wg, wu, wd = ((wide, lay.off[f"{tag}_w_{p}"], lay.dffp) for p in ("gate", "up", "down"))
    h = _norm_fwd(x, n_pre, name=f"{tag}_norm_pre", out_dtype=BF16)
    g = _mm([(h, wg)], name=f"{tag}_gate", mode="nt")
    u = _mm([(h, wu)], name=f"{tag}_up", mode="nt")
    a = _swiglu_fwd(g, u, name=f"{tag}_swiglu")
    yv = _mm([(a, wd)], name=f"{tag}_down", mode="nn")
    out = _norm_fwd(yv, n_post, name=f"{tag}_norm_post", resid=x, scale=MACARON_SCALE)
    return out, (x, h, g, u, a, yv)


def _ffn_bwd(dout, saved, n_pre, n_post, wide, gwide, lay, tag):
    x, h, g, u, a, yv = saved
    og, ou, od = (lay.off[f"{tag}_w_{p}"] for p in ("gate", "up", "down"))
    dy, dn_post = _norm_bwd(yv, n_post, dout, name=f"{tag}_norm_post_bwd", scale=MACARON_SCALE)
    da = _mm([(dy, (wide, od, lay.dffp))], name=f"{tag}_down_dx", mode="nt")
    gwide = _mm([(a, dy)], name=f"{tag}_down_dw", mode="tn", into=(gwide, od))
    dg, du = _swiglu_bwd(da, g, u, name=f"{tag}_swiglu_bwd")
    dh = _mm([(dg, (wide, og, lay.dffp)), (du, (wide, ou, lay.dffp))], name=f"{tag}_up_dx", mode="nn")
    gwide = _mm([(dg, h)], name=f"{tag}_gate_dw", mode="tn", into=(gwide, og))
    gwide = _mm([(du, h)], name=f"{tag}_up_dw", mode="tn", into=(gwide, ou))
    dx, dn_pre = _norm_bwd(x, n_pre, dh, name=f"{tag}_norm_pre_bwd", dres=dout)
    return dx, dn_pre, dn_post, gwide


def _rope_tables(positions):
    half = ROPE // 2
    inv_freq = ROPE_THETA ** (-jnp.arange(half, dtype=F32) / half)
    ang = positions.astype(F32)[:, None] * inv_freq
    cos, sin = jnp.cos(ang), jnp.sin(ang)
    z = jnp.zeros_like(cos)
    z2 = jnp.zeros((positions.shape[0], LANE - ROPE), F32)
    return (jnp.concatenate([cos, cos, z2], axis=1), jnp.concatenate([-sin, z, z2], axis=1),
            jnp.concatenate([z, sin, z2], axis=1))


def kernel(x, positions, ffn1_norm_pre, ffn1_w_gate, ffn1_w_up, ffn1_w_down, ffn1_norm_post, mix_norm_pre, w_in, mla_q_norm, mla_w_q_up, mla_kv_norm, mla_w_kv_up, mla_w_o, hgrn_lb_logits, hgrn_out_norm, hgrn_w_o, w_out, mix_norm_post, ffn2_norm_pre, ffn2_w_gate, ffn2_w_up, ffn2_w_down, ffn2_norm_post, loss_target, m_ffn1_norm_pre, m_ffn1_w_gate, m_ffn1_w_up, m_ffn1_w_down, m_ffn1_norm_post, m_mix_norm_pre, m_w_in, m_mla_q_norm, m_mla_w_q_up, m_mla_kv_norm, m_mla_w_kv_up, m_mla_w_o, m_hgrn_lb_logits, m_hgrn_out_norm, m_hgrn_w_o, m_w_out, m_mix_norm_post, m_ffn2_norm_pre, m_ffn2_w_gate, m_ffn2_w_up, m_ffn2_w_down, m_ffn2_norm_post, v_ffn1_norm_pre, v_ffn1_w_gate, v_ffn1_w_up, v_ffn1_w_down, v_ffn1_norm_post, v_mix_norm_pre, v_w_in, v_mla_q_norm, v_mla_w_q_up, v_mla_kv_norm, v_mla_w_kv_up, v_mla_w_o, v_hgrn_lb_logits, v_hgrn_out_norm, v_hgrn_w_o, v_w_out, v_mix_norm_post, v_ffn2_norm_pre, v_ffn2_w_gate, v_ffn2_w_up, v_ffn2_w_down, v_ffn2_norm_post):
    given = dict(locals())
    wts = {n: given[n] for n in ALL_WEIGHTS}
    mom = {n: given["m_" + n] for n in ALL_WEIGHTS}
    var = {n: given["v_" + n] for n in ALL_WEIGHTS}
    xin = x[0]
    target = loss_target[0]
    t, d = xin.shape
    cx, cy, cc = _coords()

    q_lora, kv_lora = mla_q_norm.shape[1], mla_kv_norm.shape[1]
    nh_mla = 4 * mla_w_kv_up.shape[2] // QGROUP
    lay = _Layout(d, 4 * ffn1_w_gate.shape[2], 4 * w_in.shape[2], q_lora, kv_lora, nh_mla)
    col_sharded = lambda n: wts[n][0].T.astype(BF16)
    row_sharded = lambda n: wts[n][0].astype(BF16)
    lwide = jnp.concatenate([col_sharded("ffn1_w_gate"), col_sharded("ffn1_w_up"), row_sharded("ffn1_w_down"),
                             col_sharded("ffn2_w_gate"), col_sharded("ffn2_w_up"), row_sharded("ffn2_w_down"),
                             col_sharded("w_in"), row_sharded("mla_w_o"), row_sharded("hgrn_w_o"), row_sharded("w_out")])
    lnarrow = jnp.concatenate([col_sharded("mla_w_q_up"), col_sharded("mla_w_kv_up")])
    halves = lambda a: a.reshape(a.shape[0], 2, a.shape[1] // 2).transpose(1, 0, 2)
    wide, narrow = _gather_weights(halves(lwide), halves(lnarrow), jnp.zeros((2, max(lay.pad, lay.ffn_pad), d // 2), BF16),
                                   jnp.zeros((2, QGROUP - HEAD - ROPE, q_lora // 2), BF16), lay, name="gather_weights")
    w_in_v = (wide, lay.off["w_in"], 7 * d)
    w_q_v = (narrow, lay.off_q, nh_mla * QGROUP)
    w_kv_v = (narrow, lay.off_kv, nh_mla * QGROUP)
    w_o_v = {n: (wide, lay.off[n], d) for n in ("mla_w_o", "hgrn_w_o", "w_out")}
    col_kr = q_lora + kv_lora
    hgrn_cols = [d, 2 * d, 3 * d, 4 * d]
    col_ga, col_gb = 5 * d, 6 * d
    tabs = _rope_tables(positions[0])
    scale = (HEAD + ROPE) ** -0.5

    x1, saved1 = _ffn_fwd(xin, ffn1_norm_pre, ffn1_norm_post, wide, lay, "ffn1")

    h2 = _norm_fwd(x1, mix_norm_pre, name="mix_norm_pre", out_dtype=BF16)
    proj = _mm([(h2, w_in_v)], name="mix_in", mode="nt")
    cqn = _norm_fwd(proj, mla_q_norm, name="mla_q_norm", out_dtype=BF16, col=0)
    ckvn = _norm_fwd(proj, mla_kv_norm, name="mla_kv_norm", out_dtype=BF16, col=q_lora)
    qp = _mm([(cqn, w_q_v)], name="mla_q_up", mode="nt")
    kvb = _mm([(ckvn, w_kv_v)], name="mla_kv_up", mode="nt", out_dtype=BF16)
    qcat = _rope(qp, tabs, name="rope_q", group=QGROUP, backward=False, out_dtype=BF16)
    krot = _rope(proj, tabs, name="rope_k", group=LANE, backward=False, out_dtype=BF16, col=col_kr, ngroup=1)
    o_mla = _attn_fwd(qcat, kvb, krot, name="mla_attention", scale=scale)
    y_a = _mm([(o_mla, w_o_v["mla_w_o"])], name="mla_out", mode="nn")

    o_raw, yb, states = _hgrn_fwd(proj, hgrn_cols, d, hgrn_lb_logits, hgrn_out_norm, name="hgrn_scan")
    y_b = _mm([(yb, w_o_v["hgrn_w_o"])], name="hgrn_out", mode="nn")

    merged = _merge_fwd(proj, col_ga, col_gb, y_a, y_b, name="mix_merge")
    y_mix = _mm([(merged, w_o_v["w_out"])], name="mix_out", mode="nn")
    x2 = _norm_fwd(y_mix, mix_norm_post, name="mix_norm_post", resid=x1, scale=1.0)

    x3, saved2 = _ffn_fwd(x2, ffn2_norm_pre, ffn2_norm_post, wide, lay, "ffn2")
    dx3, loss_local = _loss_head(x3, target, name="loss_head")

    grads = {}
    gwide = lax.empty((2, lay.rows_wide, d // 2), F32)
    gnarrow = lax.empty((2, lay.rows_narrow, q_lora // 2), F32)
    dx2, grads["ffn2_norm_pre"], grads["ffn2_norm_post"], gwide = _ffn_bwd(
        dx3, saved2, ffn2_norm_pre, ffn2_norm_post, wide, gwide, lay, "ffn2")

    dy_mix, grads["mix_norm_post"] = _norm_bwd(y_mix, mix_norm_post, dx2, name="mix_norm_post_bwd")
    dmerged = _mm([(dy_mix, w_o_v["w_out"])], name="mix_out_dx", mode="nt")
    gwide = _mm([(merged, dy_mix)], name="mix_out_dw", mode="tn", into=(gwide, lay.off["w_out"]))
    dga, dgb, dy_a, dy_b = _merge_bwd(dmerged, proj, col_ga, col_gb, y_a, y_b, name="mix_merge_bwd")

    do_mla = _mm([(dy_a, w_o_v["mla_w_o"])], name="mla_out_dx", mode="nt")
    gwide = _mm([(o_mla, dy_a)], name="mla_out_dw", mode="tn", into=(gwide, lay.off["mla_w_o"]))
    dqcat, dkv, dkr = _attn_bwd(qcat, kvb, krot, do_mla, name="mla_attention_bwd", scale=scale)
    dqp = _rope(dqcat, tabs, name="rope_q_bwd", group=QGROUP, backward=True, out_dtype=BF16)
    dk_r = _rope(dkr, tabs, name="rope_k_bwd", group=LANE, backward=True, out_dtype=BF16)
    dcqn = _mm([(dqp, w_q_v)], name="mla_q_up_dx", mode="nn")
    gnarrow = _mm([(dqp, cqn)], name="mla_q_up_dw", mode="tn", into=(gnarrow, lay.off_q))
    dkvb = dkv.astype(BF16)
    dckvn = _mm([(dkvb, w_kv_v)], name="mla_kv_up_dx", mode="nn")
    gnarrow = _mm([(dkvb, ckvn)], name="mla_kv_up_dw", mode="tn", into=(gnarrow, lay.off_kv))
    dc_q, grads["mla_q_norm"] = _norm_bwd(proj, mla_q_norm, dcqn, name="mla_q_norm_bwd", col=0, dx_dtype=BF16)
    dc_kv, grads["mla_kv_norm"] = _norm_bwd(proj, mla_kv_norm, dckvn, name="mla_kv_norm_bwd", col=q_lora, dx_dtype=BF16)

    dyb = _mm([(dy_b, w_o_v["hgrn_w_o"])], name="hgrn_out_dx", mode="nt")
    gwide = _mm([(yb, dy_b)], name="hgrn_out_dw", mode="tn", into=(gwide, lay.off["hgrn_w_o"]))
    dhq, dhf, dhi, dhg, dlb_h, dnorm_h = _hgrn_bwd(proj, hgrn_cols, d, o_raw, dyb, states, hgrn_lb_logits, hgrn_out_norm,
                                                   name="hgrn_scan_bwd")

    dproj = jnp.concatenate([dc_q, dc_kv, dk_r, jnp.zeros((t, d - col_kr - LANE), BF16), dhq, dhf, dhi, dhg, dga, dgb], axis=1)
    dh2 = _mm([(dproj, w_in_v)], name="mix_in_dx", mode="nn")
    gwide = _mm([(dproj, h2)], name="mix_in_dw", mode="tn", into=(gwide, lay.off["w_in"]))
    dx1, grads["mix_norm_pre"] = _norm_bwd(x1, mix_norm_pre, dh2, name="mix_norm_pre_bwd", dres=dx2)

    dx0, grads["ffn1_norm_pre"], grads["ffn1_norm_post"], gwide = _ffn_bwd(
        dx1, saved1, ffn1_norm_pre, ffn1_norm_post, wide, gwide, lay, "ffn1")

    sel = jnp.reshape(cc, (1,)).astype(jnp.int32)
    rwide, rnarrow = _swap_halves(gwide, gnarrow, name="grad_swap_sibling")
    swide = _add_sibling(gwide, rwide, sel, name="grad_add_sibling_wide")
    snarrow = _add_sibling(gnarrow, rnarrow, sel, name="grad_add_sibling_narrow")
    pwide, pnarrow = _exchange_chips(swide, snarrow, lay, name="grad_exchange_chips")
    fwide, fnarrow = _join_halves(_add_chips(pwide, sel, name="grad_add_chips_wide"),
                                  _add_chips(pnarrow, sel, name="grad_add_chips_narrow"), name="grad_join_halves")
    for n in BIG_WEIGHTS:
        if n == "mla_w_q_up":
            g_n = fnarrow[:, lay.loff_q:lay.loff_kv]
        elif n == "mla_w_kv_up":
            g_n = fnarrow[:, lay.loff_kv:]
        else:
            g_n = fwide[:, lay.loff[n]:lay.loff[n] + lay.lrows[n]]
        rows_n, hw_n = g_n.shape[1:]
        grads[n] = (g_n.transpose(0, 2, 1).reshape(2 * hw_n, rows_n) if n in COL_SHARDED
                    else g_n.transpose(1, 0, 2).reshape(rows_n, 2 * hw_n))

    dlb = dlb_h.reshape(1, -1)
    dnorm = jnp.sum(dnorm_h, axis=0)
    small = {**{n: grads[n] for n in SMALL_WEIGHTS if n not in ("hgrn_lb_logits", "hgrn_out_norm")},
             "hgrn_lb_logits": dlb, "hgrn_out_norm": dnorm}
    vec = jnp.concatenate([small[n] for n in SMALL_WEIGHTS], axis=1)
    vec = _all_reduce_small(vec, name="grad_all_reduce_small")
    off = 0
    for n in SMALL_WEIGHTS:
        w_n = small[n].shape[1]
        grads[n] = vec[:, off:off + w_n]
        off += w_n
    grads["hgrn_lb_logits"] = _lb_logits_grad(hgrn_lb_logits, grads["hgrn_lb_logits"], name="lb_logits_grad")

    deltas, new_m, new_v = {}, {}, {}
    for n in ALL_WEIGHTS:
        w_n = wts[n]
        shp = w_n.shape
        two_d = (lambda a: a[0]) if n in BIG_WEIGHTS else (lambda a: a)
        dl, nm, nv = _adamw(two_d(w_n), grads[n], two_d(mom[n]), two_d(var[n]), name=f"adamw_{n}")
        grads[n] = grads[n].reshape(shp)
        deltas[n], new_m[n], new_v[n] = dl.reshape(shp), nm.reshape(shp), nv.reshape(shp)

    loss = lax.psum(loss_local, ("x", "y", "c"))
    dx_out = dx0.reshape(x.shape)
    return (loss, dx_out, *[grads[n] for n in ALL_WEIGHTS], *[deltas[n] for n in ALL_WEIGHTS],
            *[new_m[n] for n in ALL_WEIGHTS], *[new_v[n] for n in ALL_WEIGHTS])
```

```python
import functools

import jax
import jax.numpy as jnp
from jax import lax
from jax.experimental import pallas as pl
from jax.experimental.pallas import tpu as pltpu

F32 = jnp.float32
BF16 = jnp.bfloat16
MESH = pl.DeviceIdType.MESH

NORM_EPS = 1e-6
MACARON_SCALE = 0.5
ROPE_THETA = 10000.0
HEAD = 128
ROPE = 64
QGROUP = 2 * HEAD
SUB = 16
ADAM_LR, ADAM_B1, ADAM_B2, ADAM_EPS, ADAM_WD, ADAM_STEP = 0.001, 0.9, 0.999, 1e-08, 0.01, 10

LANE = 128
VMEM_LIMIT = 48 * 1024 * 1024
MM_TILE = 1024
MM_TILE_WIDE = 1536

BIG_WEIGHTS = ("ffn1_w_gate", "ffn1_w_up", "ffn1_w_down", "w_in", "mla_w_q_up", "mla_w_kv_up",
               "mla_w_o", "hgrn_w_o", "w_out", "ffn2_w_gate", "ffn2_w_up", "ffn2_w_down")
COL_SHARDED = ("ffn1_w_gate", "ffn1_w_up", "w_in", "mla_w_q_up", "mla_w_kv_up", "ffn2_w_gate", "ffn2_w_up")
SMALL_WEIGHTS = ("ffn1_norm_pre", "ffn1_norm_post", "mix_norm_pre", "mla_q_norm", "mla_kv_norm",
                 "hgrn_lb_logits", "hgrn_out_norm", "mix_norm_post", "ffn2_norm_pre", "ffn2_norm_post")
ALL_WEIGHTS = ("ffn1_norm_pre", "ffn1_w_gate", "ffn1_w_up", "ffn1_w_down", "ffn1_norm_post", "mix_norm_pre",
               "w_in", "mla_q_norm", "mla_w_q_up", "mla_kv_norm", "mla_w_kv_up", "mla_w_o", "hgrn_lb_logits",
               "hgrn_out_norm", "hgrn_w_o", "w_out", "mix_norm_post", "ffn2_norm_pre", "ffn2_w_gate",
               "ffn2_w_up", "ffn2_w_down", "ffn2_norm_post")


def _params(*sem):
    return pltpu.CompilerParams(dimension_semantics=sem or None, vmem_limit_bytes=VMEM_LIMIT)


def _pick(n, cap, offset=0):
    if n <= cap and offset % n == 0:
        return n
    best = None
    for t in range(LANE, min(n, cap) + 1, LANE):
        if n % t == 0 and offset % t == 0:
            best = t
    assert best is not None, (n, cap, offset)
    return best


def _row_tile(n, row_bytes, budget=1 << 20):
    best = None
    for t in range(8, n + 1, 8):
        if n % t == 0 and t * row_bytes <= budget:
            best = t
    return n if best is None else best


def _sigmoid(x):
    return 1.0 / (1.0 + jnp.exp(-x))


def _silu(x):
    return x * _sigmoid(x)


def _dsilu(x):
    s = _sigmoid(x)
    return s * (1.0 + x * (1.0 - s))


def _mm(pairs, *, name, mode="nn", out_dtype=F32, into=None, deps=()):
    halves = isinstance(pairs[0][1], tuple)
    assert halves or mode == "tn"
    pairs = [(a, b if halves else (b, 0, b.shape[0])) for a, b in pairs]
    a0, (b0, b_off, b_rows) = pairs[0]
    hw = b0.shape[2] if halves else (into[0].shape[2] if into is not None else None)
    if mode == "nn":
        (m, kdim), n = a0.shape, 2 * hw
    elif mode == "nt":
        (m, kdim), n = a0.shape, b_rows
        assert kdim == 2 * hw
    else:
        (kdim, m), n = a0.shape, b0.shape[1]
    out_off = 0 if into is None else into[1]
    tm = _pick(m, MM_TILE_WIDE if mode == "tn" else MM_TILE, out_off)
    tn = hw if (mode == "nn" or into is not None) else _pick(n, MM_TILE_WIDE, b_off if mode == "nt" else 0)
    tk = hw if mode == "nt" else _pick(kdim, MM_TILE, b_off if mode == "nn" else 0)
    assert n % tn == 0 and kdim % tk == 0
    nk = kdim // tk
    npair = len(pairs)
    dims = {"nn": (((1,), (0,)), ((), ())), "nt": (((1,), (1,)), ((), ())), "tn": (((0,), (0,)), ((), ()))}[mode]

    def body(*refs):
        ins, o_ref, acc_ref = refs[:2 * npair], refs[-2], refs[-1]
        k = pl.program_id(2)

        @pl.when(k == 0)
        def _():
            acc_ref[...] = jnp.zeros_like(acc_ref)

        for p in range(npair):
            a = ins[2 * p][...].astype(BF16)
            b = ins[2 * p + 1][...].astype(BF16)
            acc_ref[...] += lax.dot_general(a, b, dims, preferred_element_type=F32)

        @pl.when(k == nk - 1)
        def _():
            o_ref[...] = acc_ref[...].astype(o_ref.dtype)

    a_spec = pl.BlockSpec((tk, tm), lambda i, j, k: (k, i)) if mode == "tn" else pl.BlockSpec((tm, tk), lambda i, j, k: (i, k))
    in_specs, flat = [], []
    for a, (b, off, _) in pairs:
        if mode == "nt":
            b_spec = pl.BlockSpec((None, tn, tk), lambda i, j, k, o=off // tn: (k, j + o, 0))
        elif mode == "nn":
            b_spec = pl.BlockSpec((None, tk, tn), lambda i, j, k, o=off // tk: (j, k + o, 0))
        else:
            b_spec = pl.BlockSpec((tk, tn), lambda i, j, k: (k, j))
        in_specs += [a_spec, b_spec]
        flat += [a, b]
    for dep in deps:
        in_specs.append(pl.BlockSpec(memory_space=pl.ANY))
        flat.append(dep)
    if into is None:
        out_shape, aliases = jax.ShapeDtypeStruct((m, n), out_dtype), {}
        out_spec = pl.BlockSpec((tm, tn), lambda i, j, k: (i, j))
    else:
        out_shape, aliases = jax.ShapeDtypeStruct(into[0].shape, into[0].dtype), {len(flat): 0}
        out_spec = pl.BlockSpec((None, tm, tn), lambda i, j, k, o=out_off // tm: (j, i + o, 0))
        in_specs.append(pl.BlockSpec(memory_space=pl.ANY))
        flat.append(into[0])
    return pl.pallas_call(
        body, name=name, grid=(m // tm, n // tn, nk),
        in_specs=in_specs,
        out_specs=out_spec,
        out_shape=out_shape, input_output_aliases=aliases,
        scratch_shapes=[pltpu.VMEM((tm, tn), F32)],
        compiler_params=_params("parallel", "parallel", "arbitrary"),
    )(*flat)


def _norm_fwd(y, w, *, name, resid=None, scale=1.0, out_dtype=F32, col=0):
    t, d = y.shape[0], w.shape[1]
    tr = _pick(t, 256)
    assert col % d == 0

    def body(*refs):
        if resid is None:
            y_ref, w_ref, o_ref = refs
        else:
            y_ref, w_ref, r_ref, o_ref = refs
        yv = y_ref[...]
        out = yv * lax.rsqrt(jnp.mean(yv * yv, axis=-1, keepdims=True) + NORM_EPS) * w_ref[...]
        if resid is not None:
            out = r_ref[...] + scale * out
        o_ref[...] = out.astype(out_dtype)

    row = pl.BlockSpec((tr, d), lambda i: (i, 0))
    wspec = pl.BlockSpec((1, d), lambda i: (0, 0))
    ins, specs = [y, w], [pl.BlockSpec((tr, d), lambda i: (i, col // d)), wspec]
    if resid is not None:
        ins.append(resid)
        specs.append(row)
    return pl.pallas_call(
        body, name=name, grid=(t // tr,), in_specs=specs, out_specs=row,
        out_shape=jax.ShapeDtypeStruct((t, d), out_dtype), compiler_params=_params("parallel"),
    )(*ins)


def _norm_bwd(x, w, dy, *, name, scale=1.0, dres=None, col=0, dx_dtype=F32):
    t, d = x.shape[0], w.shape[1]
    tr = _pick(t, 256)
    assert col % d == 0

    def body(*refs):
        if dres is None:
            x_ref, w_ref, dy_ref, dx_ref, dw_ref = refs
        else:
            x_ref, w_ref, dy_ref, dr_ref, dx_ref, dw_ref = refs

        @pl.when(pl.program_id(0) == 0)
        def _():
            dw_ref[...] = jnp.zeros_like(dw_ref)

        xv = x_ref[...]
        r = lax.rsqrt(jnp.mean(xv * xv, axis=-1, keepdims=True) + NORM_EPS)
        xhat = xv * r
        dyv = dy_ref[...].astype(F32) * scale
        dw_ref[...] += jnp.sum(dyv * xhat, axis=0, keepdims=True)
        t_ = dyv * w_ref[...]
        dx = r * (t_ - xhat * jnp.mean(t_ * xhat, axis=-1, keepdims=True))
        if dres is not None:
            dx = dx + dr_ref[...]
        dx_ref[...] = dx.astype(dx_dtype)

    row = pl.BlockSpec((tr, d), lambda i: (i, 0))
    wspec = pl.BlockSpec((1, d), lambda i: (0, 0))
    ins, specs = [x, w, dy], [pl.BlockSpec((tr, d), lambda i: (i, col // d)), wspec, row]
    if dres is not None:
        ins.append(dres)
        specs.append(row)
    return pl.pallas_call(
        body, name=name, grid=(t // tr,), in_specs=specs, out_specs=(row, wspec),
        out_shape=(jax.ShapeDtypeStruct((t, d), dx_dtype), jax.ShapeDtypeStruct((1, d), F32)),
        compiler_params=_params("arbitrary"),
    )(*ins)


def _elementwise(fn, ins, out_dtypes, *, name, width=None, cols=None):
    t = ins[0].shape[0]
    d = ins[0].shape[1] if width is None else width
    cols = [0] * len(ins) if cols is None else cols
    tc = _pick(d, 2048)
    for c in cols:
        tc = _pick(d, tc, c)
    tr = _row_tile(t, tc * 4)
    nout = len(out_dtypes)

    def body(*refs):
        outs = fn(*[r[...].astype(F32) for r in refs[:len(ins)]])
        for o_ref, o in zip(refs[len(ins):], outs):
            o_ref[...] = o.astype(o_ref.dtype)

    spec = pl.BlockSpec((tr, tc), lambda i, j: (i, j))
    in_specs = [pl.BlockSpec((tr, tc), lambda i, j, o=c // tc: (i, j + o)) for c in cols]
    return pl.pallas_call(
        body, name=name, grid=(t // tr, d // tc), in_specs=in_specs, out_specs=[spec] * nout,
        out_shape=[jax.ShapeDtypeStruct((t, d), dt) for dt in out_dtypes],
        compiler_params=_params("parallel", "parallel"),
    )(*ins)


def _swiglu_fwd(g, u, *, name):
    return _elementwise(lambda gv, uv: (_silu(gv) * uv,), [g, u], [BF16], name=name)[0]


def _swiglu_bwd(da, g, u, *, name):
    return _elementwise(lambda dav, gv, uv: (dav * uv * _dsilu(gv), dav * _silu(gv)), [da, g, u], [BF16, BF16], name=name)


def _merge_fwd(proj, col_a, col_b, ya, yb, *, name):
    return _elementwise(lambda a, b, p, q: (_sigmoid(a) * p + _sigmoid(b) * q,), [proj, proj, ya, yb], [BF16],
                        name=name, width=ya.shape[1], cols=[col_a, col_b, 0, 0])[0]


def _merge_bwd(dm, proj, col_a, col_b, ya, yb, *, name):
    def fn(dmv, a, b, p, q):
        sa, sb = _sigmoid(a), _sigmoid(b)
        return dmv * p * sa * (1.0 - sa), dmv * q * sb * (1.0 - sb), dmv * sa, dmv * sb

    return _elementwise(fn, [dm, proj, proj, ya, yb], [BF16, BF16, BF16, BF16], name=name, width=ya.shape[1],
                        cols=[0, col_a, col_b, 0, 0])


def _loss_head(xo, target, *, name):
    t, d = xo.shape
    tr = _pick(t, 256)

    def body(x_ref, t_ref, dx_ref, l_ref):
        @pl.when(pl.program_id(0) == 0)
        def _():
            l_ref[...] = jnp.zeros_like(l_ref)

        err = x_ref[...] - t_ref[...]
        dx_ref[...] = err * (1.0 / d)
        l_ref[...] += 0.5 * jnp.sum(jnp.mean(err * err, axis=-1, keepdims=True), axis=0, keepdims=True)

    row = pl.BlockSpec((tr, d), lambda i: (i, 0))
    dx, l = pl.pallas_call(
        body, name=name, grid=(t // tr,), in_specs=[row, row],
        out_specs=(row, pl.BlockSpec((1, 1), lambda i: (0, 0))),
        out_shape=(jax.ShapeDtypeStruct((t, d), F32), jax.ShapeDtypeStruct((1, 1), F32)),
        compiler_params=_params("arbitrary"),
    )(xo, target)
    return dx, l[0, 0]


def _rope(xin, tabs, *, name, group, backward, out_dtype, col=0, ngroup=None):
    t = xin.shape[0]
    ngroup = xin.shape[1] // group if ngroup is None else ngroup
    wdt = ngroup * group
    tr = _pick(t, 256)
    assert col % group == 0
    cos_t, nsin_t, sin_t = tabs

    def body(x_ref, c_ref, n_ref, s_ref, o_ref):
        xv = x_ref[...].astype(F32)
        rot = xv[:, group - LANE:]
        if backward:
            out = rot * c_ref[...] + pltpu.roll(rot * n_ref[...], 32, 1) + pltpu.roll(rot * s_ref[...], LANE - 32, 1)
        else:
            out = rot * c_ref[...] + pltpu.roll(rot, LANE - 32, 1) * n_ref[...] + pltpu.roll(rot, 32, 1) * s_ref[...]
        if group > LANE:
            out = jnp.concatenate([xv[:, :group - LANE], out], axis=1)
        o_ref[...] = out.astype(out_dtype)

    xspec = pl.BlockSpec((tr, group), lambda i, g: (i, g))
    tspec = pl.BlockSpec((tr, LANE), lambda i, g: (i, 0))
    return pl.pallas_call(
        body, name=name, grid=(t // tr, ngroup),
        in_specs=[pl.BlockSpec((tr, group), lambda i, g: (i, g + col // group)), tspec, tspec, tspec], out_specs=xspec,
        out_shape=jax.ShapeDtypeStruct((t, wdt), out_dtype), compiler_params=_params("parallel", "parallel"),
    )(xin, cos_t, nsin_t, sin_t)


def _scores(q, kv, kr, qi, tq, scale):
    kcat = jnp.concatenate([kv[:, :HEAD], kr], axis=1)
    s = lax.dot_general(q, kcat, (((1,), (1,)), ((), ())), preferred_element_type=F32) * scale
    row = qi * tq + lax.broadcasted_iota(jnp.int32, s.shape, 0)
    col = lax.broadcasted_iota(jnp.int32, s.shape, 1)
    s = jnp.where(col <= row, s, -jnp.inf)
    p = jnp.exp(s - jnp.max(s, axis=-1, keepdims=True))
    return p / jnp.sum(p, axis=-1, keepdims=True), kcat


def _attn_fwd(qcat, kv, kr, *, name, scale):
    t = qcat.shape[0]
    nh = qcat.shape[1] // QGROUP
    tq = _pick(t, 256)

    def body(q_ref, kv_ref, kr_ref, o_ref):
        kvv = kv_ref[...]
        p, _ = _scores(q_ref[...], kvv, kr_ref[...], pl.program_id(1), tq, scale)
        o_ref[...] = jnp.dot(p.astype(BF16), kvv[:, HEAD:], preferred_element_type=F32).astype(BF16)

    return pl.pallas_call(
        body, name=name, grid=(nh, t // tq),
        in_specs=[pl.BlockSpec((tq, QGROUP), lambda h, i: (i, h)), pl.BlockSpec((t, QGROUP), lambda h, i: (0, h)),
                  pl.BlockSpec((t, LANE), lambda h, i: (0, 0))],
        out_specs=pl.BlockSpec((tq, HEAD), lambda h, i: (i, h)),
        out_shape=jax.ShapeDtypeStruct((t, nh * HEAD), BF16), compiler_params=_params("parallel", "parallel"),
    )(qcat, kv, kr)


def _attn_bwd(qcat, kv, kr, do, *, name, scale):
    t = qcat.shape[0]
    nh = qcat.shape[1] // QGROUP
    tq = _pick(t, 256)
    nq = t // tq

    def body(q_ref, kv_ref, kr_ref, do_ref, dq_ref, dkv_ref, dkr_ref, dk_acc, dv_acc):
        h, i = pl.program_id(0), pl.program_id(1)

        @pl.when(i == 0)
        def _():
            dk_acc[...] = jnp.zeros_like(dk_acc)
            dv_acc[...] = jnp.zeros_like(dv_acc)

        @pl.when((i == 0) & (h == 0))
        def _():
            dkr_ref[...] = jnp.zeros_like(dkr_ref)

        q = q_ref[...]
        kvv = kv_ref[...]
        dov = do_ref[...].astype(BF16)
        p, kcat = _scores(q, kvv, kr_ref[...], i, tq, scale)
        dp = lax.dot_general(dov, kvv[:, HEAD:], (((1,), (1,)), ((), ())), preferred_element_type=F32)
        ds = (p * (dp - jnp.sum(p * dp, axis=-1, keepdims=True)) * scale).astype(BF16)
        dq_ref[...] = jnp.dot(ds, kcat, preferred_element_type=F32)
        dk_acc[...] += lax.dot_general(ds, q, (((0,), (0,)), ((), ())), preferred_element_type=F32)
        dv_acc[...] += lax.dot_general(p.astype(BF16), dov, (((0,), (0,)), ((), ())), preferred_element_type=F32)

        @pl.when(i == nq - 1)
        def _():
            dk = dk_acc[...]
            dkv_ref[...] = jnp.concatenate([dk[:, :HEAD], dv_acc[...]], axis=1)
            dkr_ref[...] += dk[:, HEAD:]

    return pl.pallas_call(
        body, name=name, grid=(nh, nq),
        in_specs=[pl.BlockSpec((tq, QGROUP), lambda h, i: (i, h)), pl.BlockSpec((t, QGROUP), lambda h, i: (0, h)),
                  pl.BlockSpec((t, LANE), lambda h, i: (0, 0)), pl.BlockSpec((tq, HEAD), lambda h, i: (i, h))],
        out_specs=(pl.BlockSpec((tq, QGROUP), lambda h, i: (i, h)), pl.BlockSpec((t, QGROUP), lambda h, i: (0, h)),
                   pl.BlockSpec((t, LANE), lambda h, i: (0, 0))),
        out_shape=(jax.ShapeDtypeStruct((t, nh * QGROUP), F32), jax.ShapeDtypeStruct((t, nh * QGROUP), F32),
                   jax.ShapeDtypeStruct((t, LANE), F32)),
        scratch_shapes=[pltpu.VMEM((t, QGROUP), F32), pltpu.VMEM((t, HEAD), F32)],
        compiler_params=_params("arbitrary", "arbitrary"),
    )(qcat, kv, kr, do)


def _split3(x):
    hi = x.astype(BF16)
    r1 = x - hi.astype(F32)
    mid = r1.astype(BF16)
    lo = (r1 - mid.astype(F32)).astype(BF16)
    return hi, mid, lo


def _tri_matmul(mask, x):
    m = mask.astype(BF16)
    return sum(jnp.dot(m, part, preferred_element_type=F32) for part in _split3(x))


def _sub_cumsum(g, tb):
    row = lax.broadcasted_iota(jnp.int32, (tb, tb), 0)
    col = lax.broadcasted_iota(jnp.int32, (tb, tb), 1)
    return _tri_matmul(jnp.where((col <= row) & (col // SUB == row // SUB), 1.0, 0.0), g)


def _sub_suffix_prefix(after, before, tb):
    row = lax.broadcasted_iota(jnp.int32, (tb, tb), 0)
    col = lax.broadcasted_iota(jnp.int32, (tb, tb), 1)
    same = col // SUB == row // SUB
    return (_tri_matmul(jnp.where((col >= row) & same, 1.0, 0.0), after)
            + _tri_matmul(jnp.where((col < row) & same, 1.0, 0.0), before))


def _lower_bound(logits):
    mx = jnp.max(logits, axis=0, keepdims=True)
    e = jnp.exp(logits - mx)
    return e[0:1, :] / jnp.sum(e, axis=0, keepdims=True)


def _hgrn_fwd(proj, cols, wdt, logits, out_norm, *, name):
    t = proj.shape[0]
    nh = wdt // HEAD
    tb = _pick(t, 128)
    ns = tb // SUB

    def body(hq_ref, hf_ref, hi_ref, hg_ref, lg_ref, w_ref, o_ref, yb_ref, st_ref, s_ref, q_s, k_s, b_s):
        @pl.when(pl.program_id(1) == 0)
        def _():
            s_ref[...] = jnp.zeros_like(s_ref)

        lb = _lower_bound(lg_ref[...])
        f = lb + (1.0 - lb) * _sigmoid(hf_ref[...])
        q_s[...] = _silu(hq_ref[...])
        k_s[...] = 1.0 - f
        b_s[...] = _sub_cumsum(jnp.log(f), tb)
        rowid = lax.broadcasted_iota(jnp.int32, (SUB, HEAD), 0)

        def sub(c, carry):
            rows = pl.ds(pl.multiple_of(c * SUB, SUB), SUB)
            qc, kc, bc, vc = q_s[rows, :], k_s[rows, :], b_s[rows, :], hi_ref[rows, :]
            st = s_ref[...]
            st_ref[0, c] = st
            bl = bc[SUB - 1:SUB, :]
            oc = lax.dot_general((qc * jnp.exp(bc)).astype(BF16), st.astype(BF16), (((1,), (1,)), ((), ())),
                                 preferred_element_type=F32)
            for s in range(SUB):
                e = jnp.where(rowid >= s, jnp.exp(bc - bc[s:s + 1, :]), 0.0)
                a = jnp.sum(qc * e * kc[s:s + 1, :], axis=1, keepdims=True)
                oc = oc + a * vc[s:s + 1, :]
            o_ref[rows, :] = oc
            kd = kc * jnp.exp(bl - bc)
            s_ref[...] = jnp.exp(bl) * st + lax.dot_general(vc.astype(BF16), kd.astype(BF16), (((0,), (0,)), ((), ())),
                                                             preferred_element_type=F32)
            return carry

        lax.fori_loop(0, ns, sub, 0)
        o = o_ref[...]
        r = lax.rsqrt(jnp.mean(o * o, axis=-1, keepdims=True) + NORM_EPS)
        yb_ref[...] = (o * r * w_ref[...] * _silu(hg_ref[...])).astype(BF16)

    blk = pl.BlockSpec((tb, HEAD), lambda h, j: (j, h))
    return pl.pallas_call(
        body, name=name, grid=(nh, t // tb),
        in_specs=[pl.BlockSpec((tb, HEAD), lambda h, j, o=c // HEAD: (j, h + o)) for c in cols]
        + [pl.BlockSpec((2, HEAD), lambda h, j: (0, h)), pl.BlockSpec((1, HEAD), lambda h, j: (0, 0))],
        out_specs=(blk, blk, pl.BlockSpec((1, ns, HEAD, HEAD), lambda h, j: (h, j, 0, 0))),
        out_shape=(jax.ShapeDtypeStruct((t, wdt), F32), jax.ShapeDtypeStruct((t, wdt), BF16),
                   jax.ShapeDtypeStruct((nh, t // SUB, HEAD, HEAD), F32)),
        scratch_shapes=[pltpu.VMEM((HEAD, HEAD), F32)] + [pltpu.VMEM((tb, HEAD), F32)] * 3,
        compiler_params=_params("parallel", "arbitrary"),
    )(proj, proj, proj, proj, logits, out_norm)


def _hgrn_bwd(proj, cols, wdt, o_raw, dyb, states, logits, out_norm, *, name):
    t = proj.shape[0]
    nh = wdt // HEAD
    tb = _pick(t, 128)
    ns = tb // SUB
    nb = t // tb

    def body(hq_ref, hf_ref, hi_ref, hg_ref, o_ref, dy_ref, st_ref, lg_ref, w_ref,
             dhq_ref, dhf_ref, dhi_ref, dhg_ref, dlb_ref, dw_ref,
             ds_ref, q_s, k_s, b_s, do_s, dq_s, dk_s, dv_s, after_s, before_s, thru_s):
        @pl.when(pl.program_id(1) == 0)
        def _():
            ds_ref[...] = jnp.zeros_like(ds_ref)
            dlb_ref[...] = jnp.zeros_like(dlb_ref)
            dw_ref[...] = jnp.zeros_like(dw_ref)

        lb = _lower_bound(lg_ref[...])
        hqv, hgv = hq_ref[...], hg_ref[...]
        sig = _sigmoid(hf_ref[...])
        f = lb + (1.0 - lb) * sig
        q_s[...] = _silu(hqv)
        k_s[...] = 1.0 - f
        b_s[...] = _sub_cumsum(jnp.log(f), tb)

        o = o_ref[...]
        r = lax.rsqrt(jnp.mean(o * o, axis=-1, keepdims=True) + NORM_EPS)
        nrm = o * r
        w = w_ref[...]
        dy = dy_ref[...].astype(F32)
        dhg_ref[...] = (dy * nrm * w * _dsilu(hgv)).astype(BF16)
        dnw = dy * _silu(hgv)
        dw_ref[0] += jnp.sum(dnw * nrm, axis=0, keepdims=True)
        tt = dnw * w
        do_s[...] = r * (tt - nrm * jnp.mean(tt * nrm, axis=-1, keepdims=True))
        rowid = lax.broadcasted_iota(jnp.int32, (SUB, HEAD), 0)

        def sub(cc, carry):
            c = ns - 1 - cc
            rows = pl.ds(pl.multiple_of(c * SUB, SUB), SUB)
            qc, kc, bc, vc, doc = q_s[rows, :], k_s[rows, :], b_s[rows, :], hi_ref[rows, :], do_s[rows, :]
            st = st_ref[0, c]
            dst = ds_ref[...]
            bl = bc[SUB - 1:SUB, :]
            eb = jnp.exp(bc)
            ekd = jnp.exp(bl - bc)
            qe, kd = qc * eb, kc * ekd
            dob, vcb = doc.astype(BF16), vc.astype(BF16)
            dq_st = jnp.dot(dob, st.astype(BF16), preferred_element_type=F32) * eb
            dk_st = jnp.dot(vcb, dst.astype(BF16), preferred_element_type=F32) * ekd
            dv = lax.dot_general(kd.astype(BF16), dst.astype(BF16), (((1,), (1,)), ((), ())), preferred_element_type=F32)
            dq_in = jnp.zeros_like(qc)
            dk_in = jnp.zeros_like(qc)
            for s in range(SUB):
                e = jnp.where(rowid >= s, jnp.exp(bc - bc[s:s + 1, :]), 0.0)
                ek = e * kc[s:s + 1, :]
                a = jnp.sum(qc * ek, axis=1, keepdims=True)
                da = jnp.sum(doc * vc[s:s + 1, :], axis=1, keepdims=True)
                dq_in = dq_in + da * ek
                dk_in = dk_in + jnp.where(rowid == s, jnp.sum(da * e * qc, axis=0, keepdims=True), 0.0)
                dv = dv + jnp.where(rowid == s, jnp.sum(a * doc, axis=0, keepdims=True), 0.0)
            ebl = jnp.exp(bl)
            ds_ref[...] = ebl * dst + lax.dot_general(dob, qe.astype(BF16), (((0,), (0,)), ((), ())),
                                                      preferred_element_type=F32)
            dq_s[rows, :] = dq_st + dq_in
            dk_s[rows, :] = dk_st + dk_in
            dv_s[rows, :] = dv
            after_s[rows, :] = qc * (dq_st + dq_in) - kc * dk_in
            before_s[rows, :] = kc * dk_st
            thru_s[rows, :] = jnp.broadcast_to(ebl * jnp.sum(st * dst, axis=0, keepdims=True), (SUB, HEAD))
            return carry

        lax.fori_loop(0, ns, sub, 0)
        dg = _sub_suffix_prefix(after_s[...], before_s[...], tb) + thru_s[...]
        dhq_ref[...] = (dq_s[...] * _dsilu(hqv)).astype(BF16)
        dft = dg / f - dk_s[...]
        dhf_ref[...] = (dft * (1.0 - lb) * sig * (1.0 - sig)).astype(BF16)
        dlb_ref[0] += jnp.sum(dft * (1.0 - sig), axis=0, keepdims=True)
        dhi_ref[...] = dv_s[...].astype(BF16)

    blk = pl.BlockSpec((tb, HEAD), lambda h, j: (nb - 1 - j, h))
    vec = pl.BlockSpec((1, 1, HEAD), lambda h, j: (h, 0, 0))
    tok = jax.ShapeDtypeStruct((t, wdt), BF16)
    per_head = jax.ShapeDtypeStruct((nh, 1, HEAD), F32)
    return pl.pallas_call(
        body, name=name, grid=(nh, nb),
        in_specs=[pl.BlockSpec((tb, HEAD), lambda h, j, o=c // HEAD: (nb - 1 - j, h + o)) for c in cols]
        + [blk, blk] + [pl.BlockSpec((1, ns, HEAD, HEAD), lambda h, j: (h, nb - 1 - j, 0, 0)),
                              pl.BlockSpec((2, HEAD), lambda h, j: (0, h)), pl.BlockSpec((1, HEAD), lambda h, j: (0, 0))],
        out_specs=(blk, blk, blk, blk, vec, vec),
        out_shape=(tok, tok, tok, tok, per_head, per_head),
        scratch_shapes=[pltpu.VMEM((HEAD, HEAD), F32)] + [pltpu.VMEM((tb, HEAD), F32)] * 10,
        compiler_params=_params("arbitrary", "arbitrary"),
    )(proj, proj, proj, proj, o_raw, dyb, states, logits, out_norm)


def _lb_logits_grad(logits, dlb, *, name):
    def body(lg_ref, d_ref, o_ref):
        lg = lg_ref[...]
        e = jnp.exp(lg - jnp.max(lg, axis=0, keepdims=True))
        p = e / jnp.sum(e, axis=0, keepdims=True)
        d = d_ref[...]
        rowid = lax.broadcasted_iota(jnp.int32, lg.shape, 0)
        dp = jnp.where(rowid == 0, d, 0.0)
        o_ref[...] = p * (dp - jnp.sum(p * dp, axis=0, keepdims=True))

    return pl.pallas_call(body, name=name, out_shape=jax.ShapeDtypeStruct(logits.shape, F32))(logits, dlb)


def _adamw(w, g, m, v, *, name, deps=()):
    r, c = w.shape
    tc = _pick(c, 2048) if c % LANE == 0 else c
    tr = _row_tile(r, tc * 4)

    def body(w_ref, g_ref, m_ref, v_ref, *rest):
        d_ref, nm_ref, nv_ref = rest[-3:]
        gv = g_ref[...]
        nm = ADAM_B1 * m_ref[...] + (1.0 - ADAM_B1) * gv
        nv = ADAM_B2 * v_ref[...] + (1.0 - ADAM_B2) * (gv * gv)
        m_hat = nm / (1.0 - ADAM_B1 ** ADAM_STEP)
        v_hat = nv / (1.0 - ADAM_B2 ** ADAM_STEP)
        d_ref[...] = -ADAM_LR * (m_hat / (jnp.sqrt(v_hat) + ADAM_EPS) + ADAM_WD * w_ref[...])
        nm_ref[...] = nm
        nv_ref[...] = nv

    spec = pl.BlockSpec((tr, tc), lambda i, j: (i, j))
    shp = jax.ShapeDtypeStruct((r, c), F32)
    return pl.pallas_call(
        body, name=name, grid=(r // tr, c // tc), in_specs=[spec] * 4 + [ANY] * len(deps), out_specs=[spec] * 3,
        out_shape=[shp, shp, shp], compiler_params=_params("parallel", "parallel"),
    )(w, g, m, v, *deps)


def _coords():
    return lax.axis_index("x"), lax.axis_index("y"), lax.axis_index("c")


def _other_chips(x, y):
    return [(1 - x, y), (x, 1 - y), (1 - x, 1 - y)]


ANY = pl.BlockSpec(memory_space=pl.ANY)


class _Layout:
    def __init__(self, d, dff, in_cols, q_lora, kv_lora, nh):
        assert q_lora == kv_lora and nh % 4 == 0 and dff % 4 == 0 and in_cols % 4 == 0 and d % 4 == 0
        self.d, self.dff, self.q_lora, self.nh = d, dff, q_lora, nh
        self.head = q_lora + kv_lora + ROPE
        self.pad = d - self.head
        self.dffp = -(-dff // LANE) * LANE
        self.ffn_pad = self.dffp - dff
        dffp = self.dffp
        nff, ncol, r_o, hps = dff // 4, in_cols // 4, d // 4, nh // 4
        assert self.head <= ncol
        names = ("ffn1_w_gate", "ffn1_w_up", "ffn1_w_down", "ffn2_w_gate", "ffn2_w_up", "ffn2_w_down")
        self.ffn_names = names
        self.off = {n: i * dffp for i, n in enumerate(names)}
        self.off.update(w_in=6 * dffp, mla_w_o=6 * dffp + 7 * d, hgrn_w_o=6 * dffp + 8 * d, w_out=6 * dffp + 9 * d)
        self.rows_wide = 6 * dffp + 10 * d
        self.off_q, self.off_kv, self.rows_narrow = 0, nh * QGROUP, 2 * nh * QGROUP
        self.loff = {n: i * nff for i, n in enumerate(names)}
        self.loff.update(w_in=6 * nff, mla_w_o=6 * nff + ncol, hgrn_w_o=6 * nff + ncol + r_o, w_out=6 * nff + ncol + 2 * r_o)
        self.lrows = {n: nff for n in names}
        self.lrows.update(w_in=ncol, mla_w_o=r_o, hgrn_w_o=r_o, w_out=r_o)
        self.lrows_wide = 6 * nff + ncol + 3 * r_o
        self.lrows_narrow = hps * (HEAD + ROPE) + hps * QGROUP
        self.loff_q, self.loff_kv = 0, hps * (HEAD + ROPE)
        self.nff, self.ncol, self.r_o, self.hps = nff, ncol, r_o, hps

    def segments(self, k):
        first = lambda a, b: jnp.where(k == 0, a, b) if not isinstance(k, int) else (a if k == 0 else b)
        segs = []
        for n in ("ffn1_w_gate", "ffn1_w_up", "ffn1_w_down"):
            segs.append((0, self.loff[n], self.nff, self.off[n] + self.nff * k, 0))
        w_in = self.off["w_in"]
        segs.append((0, self.loff["w_in"], self.head, w_in + first(0, self.ncol * k + self.pad), 1))
        segs.append((0, self.loff["w_in"] + self.head, self.ncol - self.head, w_in + self.ncol * k + self.d, 1))
        for n in ("mla_w_o", "hgrn_w_o", "w_out"):
            segs.append((0, self.loff[n], self.r_o, self.off[n] + self.r_o * k, 1))
        for hh in range(self.hps):
            segs.append((1, (HEAD + ROPE) * hh, HEAD + ROPE, QGROUP * (self.hps * k + hh), 2))
        segs.append((1, self.loff_kv, self.hps * QGROUP, self.off_kv + self.hps * QGROUP * k, 2))
        for n in ("ffn2_w_gate", "ffn2_w_up", "ffn2_w_down"):
            segs.append((0, self.loff[n], self.nff, self.off[n] + self.nff * k, 3))
        return segs

    def stream_rows(self, stream):
        return sum(s[2] for s in self.segments(0) if s[4] == stream)


N_STREAM = 4


def _half(ref, row, rows, half):
    return ref.at[half, pl.ds(row, rows)]


def _both(ref, row, rows):
    return ref.at[:, pl.ds(row, rows)]


def _wait_bytes_of(ref_like, send_sem, recv_sem, me, *, send):
    cp = pltpu.make_async_remote_copy(src_ref=ref_like, dst_ref=ref_like, send_sem=send_sem, recv_sem=recv_sem,
                                      device_id=me, device_id_type=MESH)
    if send:
        cp.wait_send()
    else:
        cp.wait_recv()


def _gather_weights(lwide, lnarrow, zwide, znarrow, lay, *, name):
    d, ql = lay.d, lay.q_lora
    qpad = QGROUP - HEAD - ROPE

    def body(l0, l1, z0, z1, w0, w1, send, recv, fsend, frecv, osend, orecv):
        x, y, c = _coords()
        me_chip = 2 * x + y
        chips = _other_chips(x, y)
        src, dst = (l0, l1), (w0, w1)

        def to_sibling(a, src_ref, drow, rows):
            pltpu.make_async_remote_copy(src_ref=src_ref, dst_ref=_both(dst[a], drow, rows), send_sem=osend.at[a],
                                         recv_sem=orecv.at[a], device_id=(x, y, 1 - c), device_id_type=MESH).start()

        for a, lrow, rows, drow, _ in lay.segments(me_chip):
            to_sibling(a, _both(src[a], lrow, rows), drow, rows)
        to_sibling(0, _both(z0, 0, lay.pad), lay.off["w_in"] + lay.head, lay.pad)
        if lay.ffn_pad:
            for n in lay.ffn_names:
                to_sibling(0, _both(z0, 0, lay.ffn_pad), lay.off[n] + lay.dff, lay.ffn_pad)
        for g in range(lay.nh):
            to_sibling(1, z1, QGROUP * g + HEAD + ROPE, qpad)

        for j, (px, py) in enumerate(chips):
            for a, lrow, rows, drow, st in lay.segments(me_chip):
                pltpu.make_async_remote_copy(
                    src_ref=_half(src[a], lrow, rows, c), dst_ref=_half(dst[a], drow, rows, c),
                    send_sem=send.at[j, st], recv_sem=recv.at[j, st], device_id=(px, py, c), device_id_type=MESH).start()

        def total(st):
            return _half(dst[1 if st == 2 else 0], 0, lay.stream_rows(st), 0)

        for st in range(N_STREAM):
            for j, (px, py) in enumerate(chips):
                _wait_bytes_of(total(st), send.at[j, st], recv.at[j, st], (x, y, c), send=False)
                for a, lrow, rows, drow, s2 in lay.segments(2 * px + py):
                    if s2 == st:
                        blk = _half(dst[a], drow, rows, c)
                        pltpu.make_async_remote_copy(
                            src_ref=blk, dst_ref=blk, send_sem=fsend.at[j, st], recv_sem=frecv.at[j, st],
                            device_id=(x, y, 1 - c), device_id_type=MESH).start()
        for st in range(N_STREAM):
            for j in range(3):
                _wait_bytes_of(total(st), fsend.at[j, st], frecv.at[j, st], (x, y, c), send=False)
        for st in range(N_STREAM):
            for j in range(3):
                _wait_bytes_of(total(st), fsend.at[j, st], frecv.at[j, st], (x, y, c), send=True)
                _wait_bytes_of(total(st), send.at[j, st], recv.at[j, st], (x, y, c), send=True)
        own = (_both(w0, 0, lay.lrows_wide + lay.pad + 6 * lay.ffn_pad), _both(w1, 0, lay.lrows_narrow + lay.nh * qpad))
        for a in range(2):
            _wait_bytes_of(own[a], osend.at[a], orecv.at[a], (x, y, c), send=False)
            _wait_bytes_of(own[a], osend.at[a], orecv.at[a], (x, y, c), send=True)

    sem = pltpu.SemaphoreType.DMA((3, N_STREAM))
    return pl.pallas_call(
        body, name=name, in_specs=[ANY] * 4, out_specs=[ANY] * 2,
        out_shape=[jax.ShapeDtypeStruct((2, lay.rows_wide, d // 2), BF16),
                   jax.ShapeDtypeStruct((2, lay.rows_narrow, ql // 2), BF16)],
        scratch_shapes=[sem, sem, sem, sem, pltpu.SemaphoreType.DMA((2,)), pltpu.SemaphoreType.DMA((2,))],
    )(lwide, lnarrow, zwide, znarrow)


def _swap_halves(gwide, gnarrow, *, name):
    def body(g0, g1, r0, r1, send_sems, recv_sems):
        x, y, c = _coords()
        copies = []
        for a, (g, r) in enumerate(((g0, r0), (g1, r1))):
            copies.append(pltpu.make_async_remote_copy(
                src_ref=g.at[1 - c], dst_ref=r, send_sem=send_sems.at[a], recv_sem=recv_sems.at[a],
                device_id=(x, y, 1 - c), device_id_type=MESH))
        for cp in copies:
            cp.start()
        for cp in copies:
            cp.wait()

    return pl.pallas_call(
        body, name=name, in_specs=[ANY] * 2, out_specs=[ANY] * 2,
        out_shape=[jax.ShapeDtypeStruct(g.shape[1:], g.dtype) for g in (gwide, gnarrow)],
        scratch_shapes=[pltpu.SemaphoreType.DMA((2,)), pltpu.SemaphoreType.DMA((2,))],
    )(gwide, gnarrow)


def _add_sibling(g, recv, sel, *, name):
    rows, hw = recv.shape
    tr = _row_tile(rows, hw * 4)

    def body(sel_ref, g_ref, r_ref, o_ref):
        o_ref[...] = (g_ref[...] + r_ref[...]).astype(BF16)

    return pl.pallas_call(
        body, name=name, out_shape=jax.ShapeDtypeStruct((rows, hw), BF16),
        grid_spec=pltpu.PrefetchScalarGridSpec(
            num_scalar_prefetch=1, grid=(rows // tr,),
            in_specs=[pl.BlockSpec((None, tr, hw), lambda i, s: (s[0], i, 0)), pl.BlockSpec((tr, hw), lambda i, s: (i, 0))],
            out_specs=pl.BlockSpec((tr, hw), lambda i, s: (i, 0))),
        compiler_params=_params("parallel"),
    )(sel, g, recv)


def _exchange_chips(swide, snarrow, lay, *, name):
    def body(s0, s1, r0, r1, send, recv, lsem):
        x, y, c = _coords()
        me_chip = 2 * x + y
        src, dst = (s0, s1), (r0, r1)
        for a, lrow, rows, drow, _ in lay.segments(me_chip):
            pltpu.make_async_copy(src[a].at[pl.ds(drow, rows)], dst[a].at[me_chip, pl.ds(lrow, rows)], lsem.at[a]).start()
        for j, (px, py) in enumerate(_other_chips(x, y)):
            for a, lrow, rows, drow, _ in lay.segments(2 * px + py):
                pltpu.make_async_remote_copy(
                    src_ref=src[a].at[pl.ds(drow, rows)], dst_ref=dst[a].at[me_chip, pl.ds(lrow, rows)],
                    send_sem=send.at[j, a], recv_sem=recv.at[j, a], device_id=(px, py, c), device_id_type=MESH).start()
        for a in range(2):
            for j in range(3):
                _wait_bytes_of(dst[a].at[0], send.at[j, a], recv.at[j, a], (x, y, c), send=False)
        for a in range(2):
            for j in range(3):
                _wait_bytes_of(dst[a].at[0], send.at[j, a], recv.at[j, a], (x, y, c), send=True)
            pltpu.make_async_copy(dst[a].at[0], dst[a].at[0], lsem.at[a]).wait()

    sem = pltpu.SemaphoreType.DMA((3, 2))
    return pl.pallas_call(
        body, name=name, in_specs=[ANY] * 2, out_specs=[ANY] * 2,
        out_shape=[jax.ShapeDtypeStruct((4, lay.lrows_wide, swide.shape[1]), BF16),
                   jax.ShapeDtypeStruct((4, lay.lrows_narrow, snarrow.shape[1]), BF16)],
        scratch_shapes=[sem, sem, pltpu.SemaphoreType.DMA((2,))],
    )(swide, snarrow)


def _add_chips(parts, sel, *, name):
    _, rows, wdt = parts.shape
    tr = _row_tile(rows, wdt * 4)

    def body(sel_ref, p_ref, o_ref):
        o_ref[...] = ((p_ref[0].astype(F32) + p_ref[1].astype(F32)) + p_ref[2].astype(F32)) + p_ref[3].astype(F32)

    return pl.pallas_call(
        body, name=name, out_shape=jax.ShapeDtypeStruct((2, rows, wdt), F32),
        grid_spec=pltpu.PrefetchScalarGridSpec(
            num_scalar_prefetch=1, grid=(rows // tr,),
            in_specs=[pl.BlockSpec((4, tr, wdt), lambda i, s: (0, i, 0))],
            out_specs=pl.BlockSpec((None, tr, wdt), lambda i, s: (s[0], i, 0))),
        compiler_params=_params("parallel"),
    )(sel, parts)


def _join_halves(fwide, fnarrow, *, name):
    def body(i0, i1, f0, f1, send_sems, recv_sems):
        x, y, c = _coords()
        copies = [pltpu.make_async_remote_copy(
            src_ref=f.at[c], dst_ref=f.at[c], send_sem=send_sems.at[a], recv_sem=recv_sems.at[a],
            device_id=(x, y, 1 - c), device_id_type=MESH) for a, f in enumerate((f0, f1))]
        for cp in copies:
            cp.start()
        for cp in copies:
            cp.wait()

    sem = pltpu.SemaphoreType.DMA((2,))
    return pl.pallas_call(
        body, name=name, in_specs=[ANY] * 2, out_specs=[ANY] * 2, input_output_aliases={0: 0, 1: 1},
        out_shape=[jax.ShapeDtypeStruct(f.shape, f.dtype) for f in (fwide, fnarrow)],
        scratch_shapes=[sem, sem],
    )(fwide, fnarrow)


class _Job:
    def __init__(self, a, blk, n_outer, n_inner, stride, start):
        self.a, self.blk, self.n_outer, self.n_inner, self.stride, self.start = a, blk, n_outer, n_inner, stride, start
        self.rows_out = n_outer * n_inner * blk

    def pieces(self, k):
        return [(self.start(k) + o * self.stride * self.blk, self.n_inner * self.blk) for o in range(self.n_outer)]


def _block_rows(rows, cap, *also):
    best = None
    for b in range(16, min(rows, cap) + 1, 16):
        if rows % b == 0 and all(v % b == 0 for v in also):
            best = b
    assert best is not None, (rows, also)
    return best


def _ffn_jobs(lay):
    b = _block_rows(lay.nff, 704, lay.dffp)
    return [_Job(0, b, 3, lay.nff // b, lay.dffp // b, lambda k: lay.nff * k)]


def _mix_jobs(lay):
    d, ncol, head, pad = lay.d, lay.ncol, lay.head, lay.pad
    first = lambda k, a, b: jnp.where(k == 0, a, b) if not isinstance(k, int) else (a if k == 0 else b)
    ba = _block_rows(head, 704, *[ncol * k + pad for k in (1, 2, 3)])
    bb = _block_rows(ncol - head, 704, *[ncol * k + d for k in (0, 1, 2, 3)])
    bo = _block_rows(lay.r_o, 704, d)
    bq = _block_rows(HEAD + ROPE, 704, QGROUP)
    bk = _block_rows(lay.hps * QGROUP, 704, lay.off_kv)
    return [_Job(0, ba, 1, head // ba, 0, lambda k: first(k, 0, ncol * k + pad)),
            _Job(0, bb, 1, (ncol - head) // bb, 0, lambda k: ncol * k + d),
            _Job(0, bo, 3, lay.r_o // bo, d // bo, lambda k: 7 * d + lay.r_o * k),
            _Job(1, bq, lay.hps, (HEAD + ROPE) // bq, QGROUP // bq, lambda k: QGROUP * lay.hps * k),
            _Job(1, bk, 1, lay.hps * QGROUP // bk, 0, lambda k: lay.off_kv + lay.hps * QGROUP * k)]


HBM = pl.BlockSpec(memory_space=pltpu.HBM)
SEMS = pl.BlockSpec(memory_space=pltpu.SEMAPHORE)
SPLIT = dict(has_side_effects=pltpu.SideEffectType.DATAFLOW_SIDE_EFFECTING)


def _in_hbm(a):
    return pltpu.with_memory_space_constraint(a, pltpu.HBM)


def _swap_start(gs, *, name):
    n = len(gs)
    lands = [lax.empty(g.shape[1:], g.dtype) for g in gs]

    def body(*refs):
        g_refs, land_refs, send, recv, token = refs[:n], refs[n:2 * n], refs[2 * n], refs[2 * n + 1], refs[-1]
        x, y, c = _coords()
        for a in range(n):
            pltpu.make_async_remote_copy(src_ref=g_refs[a].at[1 - c], dst_ref=land_refs[a], send_sem=send.at[a],
                                         recv_sem=recv.at[a], device_id=(x, y, 1 - c), device_id_type=MESH).start()
        token[...] = jnp.zeros_like(token)

    thru = [pltpu.HBM(a.shape, a.dtype) for a in gs + lands]
    out = pl.pallas_call(
        body, name=name, in_specs=[HBM] * (2 * n),
        out_shape=(pltpu.SemaphoreType.DMA((n,)), pltpu.SemaphoreType.DMA((n,)), *thru, jax.ShapeDtypeStruct((8, LANE), F32)),
        out_specs=(SEMS, SEMS, *[HBM] * (2 * n), pl.BlockSpec(memory_space=pltpu.VMEM)),
        input_output_aliases={i: 2 + i for i in range(2 * n)}, compiler_params=pltpu.CompilerParams(**SPLIT),
    )(*[_in_hbm(a) for a in gs + lands])
    return dict(send=out[0], recv=out[1], bufs=list(out[2:2 + 2 * n]), n=n), out[-1]


def _swap_wait(handle, after, *, name):
    n = handle["n"]

    def body(*refs):
        g_refs, land_refs, send, recv = refs[:n], refs[n:2 * n], refs[2 * n], refs[2 * n + 1]
        x, y, c = _coords()
        for a in range(n):
            cp = pltpu.make_async_remote_copy(src_ref=g_refs[a].at[1 - c], dst_ref=land_refs[a], send_sem=send.at[a],
                                              recv_sem=recv.at[a], device_id=(x, y, 1 - c), device_id_type=MESH)
            cp.wait_send()
            cp.wait_recv()

    out = pl.pallas_call(
        body, name=name, in_specs=[HBM] * (2 * n) + [SEMS, SEMS, ANY],
        out_shape=[pltpu.HBM(a.shape, a.dtype) for a in handle["bufs"]], out_specs=[HBM] * (2 * n),
        input_output_aliases={i: i for i in range(2 * n)}, compiler_params=pltpu.CompilerParams(**SPLIT),
    )(*handle["bufs"], handle["send"], handle["recv"], after)
    return list(out[:n]), list(out[n:])


def _exchange_start(ss, jobs, *, name):
    n = len(ss)
    lands = [lax.empty((3,) + s.shape, s.dtype) for s in ss]

    def body(*refs):
        s_refs, land_refs, send, recv, token = refs[:n], refs[n:2 * n], refs[2 * n], refs[2 * n + 1], refs[-1]
        x, y, c = _coords()
        for j, (px, py) in enumerate(_other_chips(x, y)):
            for job in jobs:
                for row, rows in job.pieces(2 * px + py):
                    pltpu.make_async_remote_copy(
                        src_ref=s_refs[job.a].at[pl.ds(row, rows)], dst_ref=land_refs[job.a].at[j, pl.ds(row, rows)],
                        send_sem=send.at[n * j + job.a], recv_sem=recv.at[n * j + job.a], device_id=(px, py, c),
                        device_id_type=MESH).start()
        token[...] = jnp.zeros_like(token)

    thru = [pltpu.HBM(a.shape, a.dtype) for a in ss + lands]
    out = pl.pallas_call(
        body, name=name, in_specs=[HBM] * (2 * n),
        out_shape=(pltpu.SemaphoreType.DMA((3 * n,)), pltpu.SemaphoreType.DMA((3 * n,)), *thru, jax.ShapeDtypeStruct((8, LANE), F32)),
        out_specs=(SEMS, SEMS, *[HBM] * (2 * n), pl.BlockSpec(memory_space=pltpu.VMEM)),
        input_output_aliases={i: 2 + i for i in range(2 * n)}, compiler_params=pltpu.CompilerParams(**SPLIT),
    )(*[_in_hbm(a) for a in ss + lands])
    return dict(send=out[0], recv=out[1], bufs=list(out[2:2 + 2 * n]), n=n, jobs=jobs), out[-1]


def _exchange_wait(handle, after, *, name):
    n, jobs = handle["n"], handle["jobs"]
    total = [sum(rows for job in jobs if job.a == a for _, rows in job.pieces(0)) for a in range(n)]

    def body(*refs):
        s_refs, land_refs, send, recv = refs[:n], refs[n:2 * n], refs[2 * n], refs[2 * n + 1]
        x, y, c = _coords()
        for a in range(n):
            for j in range(3):
                all_rows = land_refs[a].at[0, pl.ds(0, total[a])]
                cp = pltpu.make_async_remote_copy(src_ref=all_rows, dst_ref=all_rows, send_sem=send.at[n * j + a],
                                                  recv_sem=recv.at[n * j + a], device_id=(x, y, c), device_id_type=MESH)
                cp.wait_send()
                cp.wait_recv()

    out = pl.pallas_call(
        body, name=name, in_specs=[HBM] * (2 * n) + [SEMS, SEMS, ANY],
        out_shape=[pltpu.HBM(a.shape, a.dtype) for a in handle["bufs"]], out_specs=[HBM] * (2 * n),
        input_output_aliases={i: i for i in range(2 * n)}, compiler_params=pltpu.CompilerParams(**SPLIT),
    )(*handle["bufs"], handle["send"], handle["recv"], after)
    return list(out[:n]), list(out[n:])


def _add_shard(s, land, job, sel, k, *, name):
    hw = s.shape[1]
    blk, no, ni, stride = job.blk, job.n_outer, job.n_inner, job.stride
    scal = jnp.stack([sel, job.start(k) // blk]).astype(jnp.int32)

    def body(sc_ref, own_ref, r_ref, o_ref):
        o_ref[...] = ((own_ref[...].astype(F32) + r_ref[0].astype(F32)) + r_ref[1].astype(F32)) + r_ref[2].astype(F32)

    return pl.pallas_call(
        body, name=name, out_shape=jax.ShapeDtypeStruct((2, job.rows_out, hw), F32),
        grid_spec=pltpu.PrefetchScalarGridSpec(
            num_scalar_prefetch=1, grid=(no, ni),
            in_specs=[pl.BlockSpec((blk, hw), lambda o, b, sc: (sc[1] + o * stride + b, 0)),
                      pl.BlockSpec((3, blk, hw), lambda o, b, sc: (0, sc[1] + o * stride + b, 0))],
            out_specs=pl.BlockSpec((None, blk, hw), lambda o, b, sc: (sc[0], o * ni + b, 0))),
        compiler_params=_params("parallel", "parallel"),
    )(scal, s, land)


def _join_list(fs, *, name):
    n = len(fs)

    def body(*refs):
        f_refs, send_sems, recv_sems = refs[n:2 * n], refs[2 * n], refs[2 * n + 1]
        x, y, c = _coords()
        copies = [pltpu.make_async_remote_copy(
            src_ref=f.at[c], dst_ref=f.at[c], send_sem=send_sems.at[a], recv_sem=recv_sems.at[a],
            device_id=(x, y, 1 - c), device_id_type=MESH) for a, f in enumerate(f_refs)]
        for cp in copies:
            cp.start()
        for cp in copies:
            cp.wait()

    sem = pltpu.SemaphoreType.DMA((n,))
    return pl.pallas_call(
        body, name=name, in_specs=[ANY] * n, out_specs=[ANY] * n, input_output_aliases={i: i for i in range(n)},
        out_shape=[jax.ShapeDtypeStruct(f.shape, f.dtype) for f in fs], scratch_shapes=[sem, sem],
    )(*fs)


def _all_reduce_small(vec, *, name):
    n = vec.shape[1]

    def body(v_ref, o_ref, buf, send_sems, recv_sems):
        x, y, c = _coords()
        me = 4 * x + 2 * y + c
        buf[me] = v_ref[...]
        copies = []
        for m in range(1, 8):
            peer = (x ^ ((m >> 2) & 1), y ^ ((m >> 1) & 1), c ^ (m & 1))
            copies.append(pltpu.make_async_remote_copy(
                src_ref=v_ref, dst_ref=buf.at[me], send_sem=send_sems.at[m - 1], recv_sem=recv_sems.at[m - 1],
                device_id=peer, device_id_type=MESH))
        for cp in copies:
            cp.start()
        for cp in copies:
            cp.wait()
        acc = buf[0]
        for d in range(1, 8):
            acc = acc + buf[d]
        o_ref[...] = acc

    return pl.pallas_call(
        body, name=name, out_shape=jax.ShapeDtypeStruct((1, n), F32),
        in_specs=[pl.BlockSpec(memory_space=pltpu.VMEM)], out_specs=pl.BlockSpec(memory_space=pltpu.VMEM),
        scratch_shapes=[pltpu.VMEM((8, 1, n), F32), pltpu.SemaphoreType.DMA((7,)), pltpu.SemaphoreType.DMA((7,))],
    )(vec)


def _ffn_fwd(x, n_pre, n_post, wide, lay, tag):
    wg, wu, wd = ((wide, lay.off[f"{tag}_w_{p}"], lay.dffp) for p in ("gate", "up", "down"))
    h = _norm_fwd(x, n_pre, name=f"{tag}_norm_pre", out_dtype=BF16)
    g = _mm([(h, wg)], name=f"{tag}_gate", mode="nt")
    u = _mm([(h, wu)], name=f"{tag}_up", mode="nt")
    a = _swiglu_fwd(g, u, name=f"{tag}_swiglu")
    yv = _mm([(a, wd)], name=f"{tag}_down", mode="nn")
    out = _norm_fwd(yv, n_post, name=f"{tag}_norm_post", resid=x, scale=MACARON_SCALE)
    return out, (x, h, g, u, a, yv)


def _ffn_bwd(dout, saved, n_pre, n_post, wide, lay, tag, deps=(), mid=None):
    x, h, g, u, a, yv = saved
    og, ou, od = (lay.off[f"{tag}_w_{p}"] for p in ("gate", "up", "down"))
    gbuf = lax.empty((2, 3 * lay.dffp, lay.d // 2), F32)
    dy, dn_post = _norm_bwd(yv, n_post, dout, name=f"{tag}_norm_post_bwd", scale=MACARON_SCALE)
    da = _mm([(dy, (wide, od, lay.dffp))], name=f"{tag}_down_dx", mode="nt", deps=deps)
    gbuf = _mm([(a, dy)], name=f"{tag}_down_dw", mode="tn", into=(gbuf, 2 * lay.dffp))
    dg, du = _swiglu_bwd(da, g, u, name=f"{tag}_swiglu_bwd")
    deps = mid(du) if mid is not None else ()
    dh = _mm([(dg, (wide, og, lay.dffp)), (du, (wide, ou, lay.dffp))], name=f"{tag}_up_dx", mode="nn", deps=deps)
    gbuf = _mm([(dg, h)], name=f"{tag}_gate_dw", mode="tn", into=(gbuf, 0))
    gbuf = _mm([(du, h)], name=f"{tag}_up_dw", mode="tn", into=(gbuf, lay.dffp))
    dx, dn_pre = _norm_bwd(x, n_pre, dh, name=f"{tag}_norm_pre_bwd", dres=dout)
    return dx, dn_pre, dn_post, gbuf


def _rope_tables(positions):
    half = ROPE // 2
    inv_freq = ROPE_THETA ** (-jnp.arange(half, dtype=F32) / half)
    ang = positions.astype(F32)[:, None] * inv_freq
    cos, sin = jnp.cos(ang), jnp.sin(ang)
    z = jnp.zeros_like(cos)
    z2 = jnp.zeros((positions.shape[0], LANE - ROPE), F32)
    return (jnp.concatenate([cos, cos, z2], axis=1), jnp.concatenate([-sin, z, z2], axis=1),
            jnp.concatenate([z, sin, z2], axis=1))


def kernel(x, positions, ffn1_norm_pre, ffn1_w_gate, ffn1_w_up, ffn1_w_down, ffn1_norm_post, mix_norm_pre, w_in, mla_q_norm, mla_w_q_up, mla_kv_norm, mla_w_kv_up, mla_w_o, hgrn_lb_logits, hgrn_out_norm, hgrn_w_o, w_out, mix_norm_post, ffn2_norm_pre, ffn2_w_gate, ffn2_w_up, ffn2_w_down, ffn2_norm_post, loss_target, m_ffn1_norm_pre, m_ffn1_w_gate, m_ffn1_w_up, m_ffn1_w_down, m_ffn1_norm_post, m_mix_norm_pre, m_w_in, m_mla_q_norm, m_mla_w_q_up, m_mla_kv_norm, m_mla_w_kv_up, m_mla_w_o, m_hgrn_lb_logits, m_hgrn_out_norm, m_hgrn_w_o, m_w_out, m_mix_norm_post, m_ffn2_norm_pre, m_ffn2_w_gate, m_ffn2_w_up, m_ffn2_w_down, m_ffn2_norm_post, v_ffn1_norm_pre, v_ffn1_w_gate, v_ffn1_w_up, v_ffn1_w_down, v_ffn1_norm_post, v_mix_norm_pre, v_w_in, v_mla_q_norm, v_mla_w_q_up, v_mla_kv_norm, v_mla_w_kv_up, v_mla_w_o, v_hgrn_lb_logits, v_hgrn_out_norm, v_hgrn_w_o, v_w_out, v_mix_norm_post, v_ffn2_norm_pre, v_ffn2_w_gate, v_ffn2_w_up, v_ffn2_w_down, v_ffn2_norm_post):
    given = dict(locals())
    wts = {n: given[n] for n in ALL_WEIGHTS}
    mom = {n: given["m_" + n] for n in ALL_WEIGHTS}
    var = {n: given["v_" + n] for n in ALL_WEIGHTS}
    xin = x[0]
    target = loss_target[0]
    t, d = xin.shape
    cx, cy, cc = _coords()

    q_lora, kv_lora = mla_q_norm.shape[1], mla_kv_norm.shape[1]
    nh_mla = 4 * mla_w_kv_up.shape[2] // QGROUP
    lay = _Layout(d, 4 * ffn1_w_gate.shape[2], 4 * w_in.shape[2], q_lora, kv_lora, nh_mla)
    col_sharded = lambda n: wts[n][0].T.astype(BF16)
    row_sharded = lambda n: wts[n][0].astype(BF16)
    lwide = jnp.concatenate([col_sharded("ffn1_w_gate"), col_sharded("ffn1_w_up"), row_sharded("ffn1_w_down"),
                             col_sharded("ffn2_w_gate"), col_sharded("ffn2_w_up"), row_sharded("ffn2_w_down"),
                             col_sharded("w_in"), row_sharded("mla_w_o"), row_sharded("hgrn_w_o"), row_sharded("w_out")])
    lnarrow = jnp.concatenate([col_sharded("mla_w_q_up"), col_sharded("mla_w_kv_up")])
    halves = lambda a: a.reshape(a.shape[0], 2, a.shape[1] // 2).transpose(1, 0, 2)
    wide, narrow = _gather_weights(halves(lwide), halves(lnarrow), jnp.zeros((2, max(lay.pad, lay.ffn_pad), d // 2), BF16),
                                   jnp.zeros((2, QGROUP - HEAD - ROPE, q_lora // 2), BF16), lay, name="gather_weights")
    w_in_v = (wide, lay.off["w_in"], 7 * d)
    w_q_v = (narrow, lay.off_q, nh_mla * QGROUP)
    w_kv_v = (narrow, lay.off_kv, nh_mla * QGROUP)
    w_o_v = {n: (wide, lay.off[n], d) for n in ("mla_w_o", "hgrn_w_o", "w_out")}
    col_kr = q_lora + kv_lora
    hgrn_cols = [d, 2 * d, 3 * d, 4 * d]
    col_ga, col_gb = 5 * d, 6 * d
    tabs = _rope_tables(positions[0])
    scale = (HEAD + ROPE) ** -0.5

    x1, saved1 = _ffn_fwd(xin, ffn1_norm_pre, ffn1_norm_post, wide, lay, "ffn1")

    h2 = _norm_fwd(x1, mix_norm_pre, name="mix_norm_pre", out_dtype=BF16)
    proj = _mm([(h2, w_in_v)], name="mix_in", mode="nt")
    cqn = _norm_fwd(proj, mla_q_norm, name="mla_q_norm", out_dtype=BF16, col=0)
    ckvn = _norm_fwd(proj, mla_kv_norm, name="mla_kv_norm", out_dtype=BF16, col=q_lora)
    qp = _mm([(cqn, w_q_v)], name="mla_q_up", mode="nt")
    kvb = _mm([(ckvn, w_kv_v)], name="mla_kv_up", mode="nt", out_dtype=BF16)
    qcat = _rope(qp, tabs, name="rope_q", group=QGROUP, backward=False, out_dtype=BF16)
    krot = _rope(proj, tabs, name="rope_k", group=LANE, backward=False, out_dtype=BF16, col=col_kr, ngroup=1)
    o_mla = _attn_fwd(qcat, kvb, krot, name="mla_attention", scale=scale)
    y_a = _mm([(o_mla, w_o_v["mla_w_o"])], name="mla_out", mode="nn")

    o_raw, yb, states = _hgrn_fwd(proj, hgrn_cols, d, hgrn_lb_logits, hgrn_out_norm, name="hgrn_scan")
    y_b = _mm([(yb, w_o_v["hgrn_w_o"])], name="hgrn_out", mode="nn")

    merged = _merge_fwd(proj, col_ga, col_gb, y_a, y_b, name="mix_merge")
    y_mix = _mm([(merged, w_o_v["w_out"])], name="mix_out", mode="nn")
    x2 = _norm_fwd(y_mix, mix_norm_post, name="mix_norm_post", resid=x1, scale=1.0)

    x3, saved2 = _ffn_fwd(x2, ffn2_norm_pre, ffn2_norm_post, wide, lay, "ffn2")
    dx3, loss_local = _loss_head(x3, target, name="loss_head")

    grads, deltas, new_m, new_v = {}, {}, {}, {}
    sel = cc.astype(jnp.int32)
    sel1 = jnp.reshape(sel, (1,))
    me_chip = (2 * cx + cy).astype(jnp.int32)

    def reduce_mid(handle, after, jobs, tag):
        bufs, recvd = _swap_wait(handle, after, name=f"grad_swap_{tag}_wait")
        sums = [_add_sibling(b, r, sel1, name=f"grad_add_sibling_{tag}_{i}") for i, (b, r) in enumerate(zip(bufs, recvd))]
        return _exchange_start(sums, jobs, name=f"grad_exchange_{tag}")

    def reduce_end(handle, after, tag):
        sums, lands = _exchange_wait(handle, after, name=f"grad_exchange_{tag}_wait")
        parts = [_add_shard(sums[job.a], lands[job.a], job, sel, me_chip, name=f"grad_add_chips_{tag}_{i}")
                 for i, job in enumerate(handle["jobs"])]
        return _join_list(parts, name=f"grad_join_{tag}")

    def natural(part, lo, rows, transposed):
        g_n = part[:, lo:lo + rows]
        hw_n = g_n.shape[2]
        return g_n.transpose(0, 2, 1).reshape(2 * hw_n, rows) if transposed else g_n.transpose(1, 0, 2).reshape(rows, 2 * hw_n)

    def adam(names, deps=()):
        for i, n in enumerate(names):
            shp = wts[n].shape
            two_d = (lambda a: a[0]) if n in BIG_WEIGHTS else (lambda a: a)
            dl, nm, nv = _adamw(two_d(wts[n]), grads[n], two_d(mom[n]), two_d(var[n]), name=f"adamw_{n}",
                                deps=deps if i == 0 else ())
            grads[n] = grads[n].reshape(shp)
            deltas[n], new_m[n], new_v[n] = dl.reshape(shp), nm.reshape(shp), nv.reshape(shp)
        return deltas[names[-1]]

    def ffn_grads(joined, tag, deps=()):
        nff = lay.nff
        grads[f"{tag}_w_gate"] = natural(joined[0], 0, nff, True)
        grads[f"{tag}_w_up"] = natural(joined[0], nff, nff, True)
        grads[f"{tag}_w_down"] = natural(joined[0], 2 * nff, nff, False)
        return adam([f"{tag}_w_gate", f"{tag}_w_up", f"{tag}_w_down"], deps)

    dx2, grads["ffn2_norm_pre"], grads["ffn2_norm_post"], g_ffn2 = _ffn_bwd(
        dx3, saved2, ffn2_norm_pre, ffn2_norm_post, wide, lay, "ffn2")
    swap2, tok = _swap_start([g_ffn2], name="grad_swap_ffn2")

    gwide = lax.empty((2, 10 * d, d // 2), F32)
    gnarrow = lax.empty((2, lay.rows_narrow, q_lora // 2), F32)
    dy_mix, grads["mix_norm_post"] = _norm_bwd(y_mix, mix_norm_post, dx2, name="mix_norm_post_bwd")
    dmerged = _mm([(dy_mix, w_o_v["w_out"])], name="mix_out_dx", mode="nt", deps=[tok])
    gwide = _mm([(merged, dy_mix)], name="mix_out_dw", mode="tn", into=(gwide, 9 * d))
    dga, dgb, dy_a, dy_b = _merge_bwd(dmerged, proj, col_ga, col_gb, y_a, y_b, name="mix_merge_bwd")

    do_mla = _mm([(dy_a, w_o_v["mla_w_o"])], name="mla_out_dx", mode="nt")
    gwide = _mm([(o_mla, dy_a)], name="mla_out_dw", mode="tn", into=(gwide, 7 * d))
    dqcat, dkv, dkr = _attn_bwd(qcat, kvb, krot, do_mla, name="mla_attention_bwd", scale=scale)
    exch2, tok = reduce_mid(swap2, dkr, _ffn_jobs(lay), "ffn2")

    dqp = _rope(dqcat, tabs, name="rope_q_bwd", group=QGROUP, backward=True, out_dtype=BF16)
    dk_r = _rope(dkr, tabs, name="rope_k_bwd", group=LANE, backward=True, out_dtype=BF16)
    dcqn = _mm([(dqp, w_q_v)], name="mla_q_up_dx", mode="nn", deps=[tok])
    gnarrow = _mm([(dqp, cqn)], name="mla_q_up_dw", mode="tn", into=(gnarrow, lay.off_q))
    dkvb = dkv.astype(BF16)
    dckvn = _mm([(dkvb, w_kv_v)], name="mla_kv_up_dx", mode="nn")
    gnarrow = _mm([(dkvb, ckvn)], name="mla_kv_up_dw", mode="tn", into=(gnarrow, lay.off_kv))
    dc_q, grads["mla_q_norm"] = _norm_bwd(proj, mla_q_norm, dcqn, name="mla_q_norm_bwd", col=0, dx_dtype=BF16)
    dc_kv, grads["mla_kv_norm"] = _norm_bwd(proj, mla_kv_norm, dckvn, name="mla_kv_norm_bwd", col=q_lora, dx_dtype=BF16)

    dyb = _mm([(dy_b, w_o_v["hgrn_w_o"])], name="hgrn_out_dx", mode="nt")
    gwide = _mm([(yb, dy_b)], name="hgrn_out_dw", mode="tn", into=(gwide, 8 * d))
    dhq, dhf, dhi, dhg, dlb_h, dnorm_h = _hgrn_bwd(proj, hgrn_cols, d, o_raw, dyb, states, hgrn_lb_logits, hgrn_out_norm,
                                                   name="hgrn_scan_bwd")

    dproj = jnp.concatenate([dc_q, dc_kv, dk_r, jnp.zeros((t, d - col_kr - LANE), BF16), dhq, dhf, dhi, dhg, dga, dgb], axis=1)
    dh2 = _mm([(dproj, w_in_v)], name="mix_in_dx", mode="nn")
    gwide = _mm([(dproj, h2)], name="mix_in_dw", mode="tn", into=(gwide, 0))
    dx1, grads["mix_norm_pre"] = _norm_bwd(x1, mix_norm_pre, dh2, name="mix_norm_pre_bwd", dres=dx2)
    swap_m, tok = _swap_start([gwide, gnarrow], name="grad_swap_mix")
    joined2 = reduce_end(exch2, dx1, "ffn2")

    held = {}

    def ffn1_mid(after):
        held["exch"], tok_m = reduce_mid(swap_m, after, _mix_jobs(lay), "mix")
        return [tok_m]

    dx0, grads["ffn1_norm_pre"], grads["ffn1_norm_post"], g_ffn1 = _ffn_bwd(
        dx1, saved1, ffn1_norm_pre, ffn1_norm_post, wide, lay, "ffn1", deps=[tok], mid=ffn1_mid)
    swap1, tok = _swap_start([g_ffn1], name="grad_swap_ffn1")

    done = ffn_grads(joined2, "ffn2", deps=[tok])
    joined_m = reduce_end(held["exch"], done, "mix")
    exch1, tok = reduce_mid(swap1, joined_m[0], _ffn_jobs(lay), "ffn1")
    grads["w_in"] = natural(jnp.concatenate([joined_m[0], joined_m[1]], axis=1), 0, lay.ncol, True)
    for i, n in enumerate(("mla_w_o", "hgrn_w_o", "w_out")):
        grads[n] = natural(joined_m[2], i * lay.r_o, lay.r_o, False)
    grads["mla_w_q_up"] = natural(joined_m[3], 0, lay.hps * (HEAD + ROPE), True)
    grads["mla_w_kv_up"] = natural(joined_m[4], 0, lay.hps * QGROUP, True)
    done = adam(["w_in", "mla_w_q_up", "mla_w_kv_up", "mla_w_o", "hgrn_w_o", "w_out"], deps=[tok])
    ffn_grads(reduce_end(exch1, done, "ffn1"), "ffn1")

    dlb = dlb_h.reshape(1, -1)
    dnorm = jnp.sum(dnorm_h, axis=0)
    small = {**{n: grads[n] for n in SMALL_WEIGHTS if n not in ("hgrn_lb_logits", "hgrn_out_norm")},
             "hgrn_lb_logits": dlb, "hgrn_out_norm": dnorm}
    vec = jnp.concatenate([small[n] for n in SMALL_WEIGHTS], axis=1)
    vec = _all_reduce_small(vec, name="grad_all_reduce_small")
    off = 0
    for n in SMALL_WEIGHTS:
        w_n = small[n].shape[1]
        grads[n] = vec[:, off:off + w_n]
        off += w_n
    grads["hgrn_lb_logits"] = _lb_logits_grad(hgrn_lb_logits, grads["hgrn_lb_logits"], name="lb_logits_grad")

    adam(list(SMALL_WEIGHTS))

    loss = lax.psum(loss_local, ("x", "y", "c"))
    dx_out = dx0.reshape(x.shape)
    return (loss, dx_out, *[grads[n] for n in ALL_WEIGHTS], *[deltas[n] for n in ALL_WEIGHTS],
            *[new_m[n] for n in ALL_WEIGHTS], *[new_v[n] for n in ALL_WEIGHTS])
```

```python
import functools

import jax
import jax.numpy as jnp
from jax import lax
from jax.experimental import pallas as pl
from jax.experimental.pallas import tpu as pltpu

F32 = jnp.float32
BF16 = jnp.bfloat16
MESH = pl.DeviceIdType.MESH

NORM_EPS = 1e-6
MACARON_SCALE = 0.5
ROPE_THETA = 10000.0
HEAD = 128
ROPE = 64
QGROUP = 2 * HEAD
SUB = 16
ADAM_LR, ADAM_B1, ADAM_B2, ADAM_EPS, ADAM_WD, ADAM_STEP = 0.001, 0.9, 0.999, 1e-08, 0.01, 10

LANE = 128
VMEM_LIMIT = 48 * 1024 * 1024
MM_TILE = 1024
MM_TILE_WIDE = 1536

BIG_WEIGHTS = ("ffn1_w_gate", "ffn1_w_up", "ffn1_w_down", "w_in", "mla_w_q_up", "mla_w_kv_up",
               "mla_w_o", "hgrn_w_o", "w_out", "ffn2_w_gate", "ffn2_w_up", "ffn2_w_down")
COL_SHARDED = ("ffn1_w_gate", "ffn1_w_up", "w_in", "mla_w_q_up", "mla_w_kv_up", "ffn2_w_gate", "ffn2_w_up")
SMALL_WEIGHTS = ("ffn1_norm_pre", "ffn1_norm_post", "mix_norm_pre", "mla_q_norm", "mla_kv_norm",
                 "hgrn_lb_logits", "hgrn_out_norm", "mix_norm_post", "ffn2_norm_pre", "ffn2_norm_post")
ALL_WEIGHTS = ("ffn1_norm_pre", "ffn1_w_gate", "ffn1_w_up", "ffn1_w_down", "ffn1_norm_post", "mix_norm_pre",
               "w_in", "mla_q_norm", "mla_w_q_up", "mla_kv_norm", "mla_w_kv_up", "mla_w_o", "hgrn_lb_logits",
               "hgrn_out_norm", "hgrn_w_o", "w_out", "mix_norm_post", "ffn2_norm_pre", "ffn2_w_gate",
               "ffn2_w_up", "ffn2_w_down", "ffn2_norm_post")


def _params(*sem):
    return pltpu.CompilerParams(dimension_semantics=sem or None, vmem_limit_bytes=VMEM_LIMIT)


def _pick(n, cap, offset=0):
    if n <= cap and offset % n == 0:
        return n
    best = None
    for t in range(LANE, min(n, cap) + 1, LANE):
        if n % t == 0 and offset % t == 0:
            best = t
    assert best is not None, (n, cap, offset)
    return best


def _row_tile(n, row_bytes, budget=1 << 20):
    best = None
    for t in range(8, n + 1, 8):
        if n % t == 0 and t * row_bytes <= budget:
            best = t
    return n if best is None else best


def _sigmoid(x):
    return 1.0 / (1.0 + jnp.exp(-x))


def _silu(x):
    return x * _sigmoid(x)


def _dsilu(x):
    s = _sigmoid(x)
    return s * (1.0 + x * (1.0 - s))


def _mm(pairs, *, name, mode="nn", out_dtype=F32, into=None, deps=()):
    halves = isinstance(pairs[0][1], tuple)
    assert halves or mode == "tn"
    pairs = [(a, b if halves else (b, 0, b.shape[0])) for a, b in pairs]
    a0, (b0, b_off, b_rows) = pairs[0]
    hw = b0.shape[2] if halves else (into[0].shape[2] if into is not None else None)
    if mode == "nn":
        (m, kdim), n = a0.shape, 2 * hw
    elif mode == "nt":
        (m, kdim), n = a0.shape, b_rows
        assert kdim == 2 * hw
    else:
        (kdim, m), n = a0.shape, b0.shape[1]
    out_off = 0 if into is None else into[1]
    tm = _pick(m, MM_TILE_WIDE if mode == "tn" else MM_TILE, out_off)
    tn = hw if (mode == "nn" or into is not None) else _pick(n, MM_TILE_WIDE, b_off if mode == "nt" else 0)
    tk = hw if mode == "nt" else _pick(kdim, MM_TILE, b_off if mode == "nn" else 0)
    assert n % tn == 0 and kdim % tk == 0
    nk = kdim // tk
    npair = len(pairs)
    dims = {"nn": (((1,), (0,)), ((), ())), "nt": (((1,), (1,)), ((), ())), "tn": (((0,), (0,)), ((), ()))}[mode]

    def body(*refs):
        ins, o_ref, acc_ref = refs[:2 * npair], refs[-2], refs[-1]
        k = pl.program_id(2)

        @pl.when(k == 0)
        def _():
            acc_ref[...] = jnp.zeros_like(acc_ref)

        for p in range(npair):
            a = ins[2 * p][...].astype(BF16)
            b = ins[2 * p + 1][...].astype(BF16)
            acc_ref[...] += lax.dot_general(a, b, dims, preferred_element_type=F32)

        @pl.when(k == nk - 1)
        def _():
            o_ref[...] = acc_ref[...].astype(o_ref.dtype)

    a_spec = pl.BlockSpec((tk, tm), lambda i, j, k: (k, i)) if mode == "tn" else pl.BlockSpec((tm, tk), lambda i, j, k: (i, k))
    in_specs, flat = [], []
    for a, (b, off, _) in pairs:
        if mode == "nt":
            b_spec = pl.BlockSpec((None, tn, tk), lambda i, j, k, o=off // tn: (k, j + o, 0))
        elif mode == "nn":
            b_spec = pl.BlockSpec((None, tk, tn), lambda i, j, k, o=off // tk: (j, k + o, 0))
        else:
            b_spec = pl.BlockSpec((tk, tn), lambda i, j, k: (k, j))
        in_specs += [a_spec, b_spec]
        flat += [a, b]
    for dep in deps:
        in_specs.append(pl.BlockSpec(memory_space=pl.ANY))
        flat.append(dep)
    if into is None:
        out_shape, aliases = jax.ShapeDtypeStruct((m, n), out_dtype), {}
        out_spec = pl.BlockSpec((tm, tn), lambda i, j, k: (i, j))
    else:
        out_shape, aliases = jax.ShapeDtypeStruct(into[0].shape, into[0].dtype), {len(flat): 0}
        out_spec = pl.BlockSpec((None, tm, tn), lambda i, j, k, o=out_off // tm: (j, i + o, 0))
        in_specs.append(pl.BlockSpec(memory_space=pl.ANY))
        flat.append(into[0])
    return pl.pallas_call(
        body, name=name, grid=(m // tm, n // tn, nk),
        in_specs=in_specs,
        out_specs=out_spec,
        out_shape=out_shape, input_output_aliases=aliases,
        scratch_shapes=[pltpu.VMEM((tm, tn), F32)],
        compiler_params=_params("parallel", "parallel", "arbitrary"),
    )(*flat)


def _norm_fwd(y, w, *, name, resid=None, scale=1.0, out_dtype=F32, col=0):
    t, d = y.shape[0], w.shape[1]
    tr = _pick(t, 256)
    assert col % d == 0

    def body(*refs):
        if resid is None:
            y_ref, w_ref, o_ref = refs
        else:
            y_ref, w_ref, r_ref, o_ref = refs
        yv = y_ref[...]
        out = yv * lax.rsqrt(jnp.mean(yv * yv, axis=-1, keepdims=True) + NORM_EPS) * w_ref[...]
        if resid is not None:
            out = r_ref[...] + scale * out
        o_ref[...] = out.astype(out_dtype)

    row = pl.BlockSpec((tr, d), lambda i: (i, 0))
    wspec = pl.BlockSpec((1, d), lambda i: (0, 0))
    ins, specs = [y, w], [pl.BlockSpec((tr, d), lambda i: (i, col // d)), wspec]
    if resid is not None:
        ins.append(resid)
        specs.append(row)
    return pl.pallas_call(
        body, name=name, grid=(t // tr,), in_specs=specs, out_specs=row,
        out_shape=jax.ShapeDtypeStruct((t, d), out_dtype), compiler_params=_params("parallel"),
    )(*ins)


def _norm_bwd(x, w, dy, *, name, scale=1.0, dres=None, col=0, dx_dtype=F32):
    t, d = x.shape[0], w.shape[1]
    tr = _pick(t, 256)
    assert col % d == 0

    def body(*refs):
        if dres is None:
            x_ref, w_ref, dy_ref, dx_ref, dw_ref = refs
        else:
            x_ref, w_ref, dy_ref, dr_ref, dx_ref, dw_ref = refs

        @pl.when(pl.program_id(0) == 0)
        def _():
            dw_ref[...] = jnp.zeros_like(dw_ref)

        xv = x_ref[...]
        r = lax.rsqrt(jnp.mean(xv * xv, axis=-1, keepdims=True) + NORM_EPS)
        xhat = xv * r
        dyv = dy_ref[...].astype(F32) * scale
        dw_ref[...] += jnp.sum(dyv * xhat, axis=0, keepdims=True)
        t_ = dyv * w_ref[...]
        dx = r * (t_ - xhat * jnp.mean(t_ * xhat, axis=-1, keepdims=True))
        if dres is not None:
            dx = dx + dr_ref[...]
        dx_ref[...] = dx.astype(dx_dtype)

    row = pl.BlockSpec((tr, d), lambda i: (i, 0))
    wspec = pl.BlockSpec((1, d), lambda i: (0, 0))
    ins, specs = [x, w, dy], [pl.BlockSpec((tr, d), lambda i: (i, col // d)), wspec, row]
    if dres is not None:
        ins.append(dres)
        specs.append(row)
    return pl.pallas_call(
        body, name=name, grid=(t // tr,), in_specs=specs, out_specs=(row, wspec),
        out_shape=(jax.ShapeDtypeStruct((t, d), dx_dtype), jax.ShapeDtypeStruct((1, d), F32)),
        compiler_params=_params("arbitrary"),
    )(*ins)


def _elementwise(fn, ins, out_dtypes, *, name, width=None, cols=None):
    t = ins[0].shape[0]
    d = ins[0].shape[1] if width is None else width
    cols = [0] * len(ins) if cols is None else cols
    tc = _pick(d, 2048)
    for c in cols:
        tc = _pick(d, tc, c)
    tr = _row_tile(t, tc * 4)
    nout = len(out_dtypes)

    def body(*refs):
        outs = fn(*[r[...].astype(F32) for r in refs[:len(ins)]])
        for o_ref, o in zip(refs[len(ins):], outs):
            o_ref[...] = o.astype(o_ref.dtype)

    spec = pl.BlockSpec((tr, tc), lambda i, j: (i, j))
    in_specs = [pl.BlockSpec((tr, tc), lambda i, j, o=c // tc: (i, j + o)) for c in cols]
    return pl.pallas_call(
        body, name=name, grid=(t // tr, d // tc), in_specs=in_specs, out_specs=[spec] * nout,
        out_shape=[jax.ShapeDtypeStruct((t, d), dt) for dt in out_dtypes],
        compiler_params=_params("parallel", "parallel"),
    )(*ins)


def _swiglu_fwd(g, u, *, name):
    return _elementwise(lambda gv, uv: (_silu(gv) * uv,), [g, u], [BF16], name=name)[0]


def _swiglu_bwd(da, g, u, *, name):
    return _elementwise(lambda dav, gv, uv: (dav * uv * _dsilu(gv), dav * _silu(gv)), [da, g, u], [BF16, BF16], name=name)


def _merge_fwd(proj, col_a, col_b, ya, yb, *, name):
    return _elementwise(lambda a, b, p, q: (_sigmoid(a) * p + _sigmoid(b) * q,), [proj, proj, ya, yb], [BF16],
                        name=name, width=ya.shape[1], cols=[col_a, col_b, 0, 0])[0]


def _merge_bwd(dm, proj, col_a, col_b, ya, yb, *, name):
    def fn(dmv, a, b, p, q):
        sa, sb = _sigmoid(a), _sigmoid(b)
        return dmv * p * sa * (1.0 - sa), dmv * q * sb * (1.0 - sb), dmv * sa, dmv * sb

    return _elementwise(fn, [dm, proj, proj, ya, yb], [BF16, BF16, BF16, BF16], name=name, width=ya.shape[1],
                        cols=[0, col_a, col_b, 0, 0])


def _loss_head(xo, target, *, name):
    t, d = xo.shape
    tr = _pick(t, 256)

    def body(x_ref, t_ref, dx_ref, l_ref):
        @pl.when(pl.program_id(0) == 0)
        def _():
            l_ref[...] = jnp.zeros_like(l_ref)

        err = x_ref[...] - t_ref[...]
        dx_ref[...] = err * (1.0 / d)
        l_ref[...] += 0.5 * jnp.sum(jnp.mean(err * err, axis=-1, keepdims=True), axis=0, keepdims=True)

    row = pl.BlockSpec((tr, d), lambda i: (i, 0))
    dx, l = pl.pallas_call(
        body, name=name, grid=(t // tr,), in_specs=[row, row],
        out_specs=(row, pl.BlockSpec((1, 1), lambda i: (0, 0))),
        out_shape=(jax.ShapeDtypeStruct((t, d), F32), jax.ShapeDtypeStruct((1, 1), F32)),
        compiler_params=_params("arbitrary"),
    )(xo, target)
    return dx, l[0, 0]


def _rope(xin, tabs, *, name, group, backward, out_dtype, col=0, ngroup=None):
    t = xin.shape[0]
    ngroup = xin.shape[1] // group if ngroup is None else ngroup
    wdt = ngroup * group
    tr = _pick(t, 256)
    assert col % group == 0
    cos_t, nsin_t, sin_t = tabs

    def body(x_ref, c_ref, n_ref, s_ref, o_ref):
        xv = x_ref[...].astype(F32)
        rot = xv[:, group - LANE:]
        if backward:
            out = rot * c_ref[...] + pltpu.roll(rot * n_ref[...], 32, 1) + pltpu.roll(rot * s_ref[...], LANE - 32, 1)
        else:
            out = rot * c_ref[...] + pltpu.roll(rot, LANE - 32, 1) * n_ref[...] + pltpu.roll(rot, 32, 1) * s_ref[...]
        if group > LANE:
            out = jnp.concatenate([xv[:, :group - LANE], out], axis=1)
        o_ref[...] = out.astype(out_dtype)

    xspec = pl.BlockSpec((tr, group), lambda i, g: (i, g))
    tspec = pl.BlockSpec((tr, LANE), lambda i, g: (i, 0))
    return pl.pallas_call(
        body, name=name, grid=(t // tr, ngroup),
        in_specs=[pl.BlockSpec((tr, group), lambda i, g: (i, g + col // group)), tspec, tspec, tspec], out_specs=xspec,
        out_shape=jax.ShapeDtypeStruct((t, wdt), out_dtype), compiler_params=_params("parallel", "parallel"),
    )(xin, cos_t, nsin_t, sin_t)


def _scores(q, kv, kr, qi, tq, scale):
    kcat = jnp.concatenate([kv[:, :HEAD], kr], axis=1)
    s = lax.dot_general(q, kcat, (((1,), (1,)), ((), ())), preferred_element_type=F32) * scale
    row = qi * tq + lax.broadcasted_iota(jnp.int32, s.shape, 0)
    col = lax.broadcasted_iota(jnp.int32, s.shape, 1)
    s = jnp.where(col <= row, s, -jnp.inf)
    p = jnp.exp(s - jnp.max(s, axis=-1, keepdims=True))
    return p / jnp.sum(p, axis=-1, keepdims=True), kcat


def _attn_fwd(qcat, kv, kr, *, name, scale):
    t = qcat.shape[0]
    nh = qcat.shape[1] // QGROUP
    tq = _pick(t, 256)

    def body(q_ref, kv_ref, kr_ref, o_ref):
        kvv = kv_ref[...]
        p, _ = _scores(q_ref[...], kvv, kr_ref[...], pl.program_id(1), tq, scale)
        o_ref[...] = jnp.dot(p.astype(BF16), kvv[:, HEAD:], preferred_element_type=F32).astype(BF16)

    return pl.pallas_call(
        body, name=name, grid=(nh, t // tq),
        in_specs=[pl.BlockSpec((tq, QGROUP), lambda h, i: (i, h)), pl.BlockSpec((t, QGROUP), lambda h, i: (0, h)),
                  pl.BlockSpec((t, LANE), lambda h, i: (0, 0))],
        out_specs=pl.BlockSpec((tq, HEAD), lambda h, i: (i, h)),
        out_shape=jax.ShapeDtypeStruct((t, nh * HEAD), BF16), compiler_params=_params("parallel", "parallel"),
    )(qcat, kv, kr)


def _attn_bwd(qcat, kv, kr, do, *, name, scale):
    t = qcat.shape[0]
    nh = qcat.shape[1] // QGROUP
    tq = _pick(t, 256)
    nq = t // tq

    def body(q_ref, kv_ref, kr_ref, do_ref, dq_ref, dkv_ref, dkr_ref, dk_acc, dv_acc):
        h, i = pl.program_id(0), pl.program_id(1)

        @pl.when(i == 0)
        def _():
            dk_acc[...] = jnp.zeros_like(dk_acc)
            dv_acc[...] = jnp.zeros_like(dv_acc)

        @pl.when((i == 0) & (h == 0))
        def _():
            dkr_ref[...] = jnp.zeros_like(dkr_ref)

        q = q_ref[...]
        kvv = kv_ref[...]
        dov = do_ref[...].astype(BF16)
        p, kcat = _scores(q, kvv, kr_ref[...], i, tq, scale)
        dp = lax.dot_general(dov, kvv[:, HEAD:], (((1,), (1,)), ((), ())), preferred_element_type=F32)
        ds = (p * (dp - jnp.sum(p * dp, axis=-1, keepdims=True)) * scale).astype(BF16)
        dq_ref[...] = jnp.dot(ds, kcat, preferred_element_type=F32)
        dk_acc[...] += lax.dot_general(ds, q, (((0,), (0,)), ((), ())), preferred_element_type=F32)
        dv_acc[...] += lax.dot_general(p.astype(BF16), dov, (((0,), (0,)), ((), ())), preferred_element_type=F32)

        @pl.when(i == nq - 1)
        def _():
            dk = dk_acc[...]
            dkv_ref[...] = jnp.concatenate([dk[:, :HEAD], dv_acc[...]], axis=1)
            dkr_ref[...] += dk[:, HEAD:]

    return pl.pallas_call(
        body, name=name, grid=(nh, nq),
        in_specs=[pl.BlockSpec((tq, QGROUP), lambda h, i: (i, h)), pl.BlockSpec((t, QGROUP), lambda h, i: (0, h)),
                  pl.BlockSpec((t, LANE), lambda h, i: (0, 0)), pl.BlockSpec((tq, HEAD), lambda h, i: (i, h))],
        out_specs=(pl.BlockSpec((tq, QGROUP), lambda h, i: (i, h)), pl.BlockSpec((t, QGROUP), lambda h, i: (0, h)),
                   pl.BlockSpec((t, LANE), lambda h, i: (0, 0))),
        out_shape=(jax.ShapeDtypeStruct((t, nh * QGROUP), F32), jax.ShapeDtypeStruct((t, nh * QGROUP), F32),
                   jax.ShapeDtypeStruct((t, LANE), F32)),
        scratch_shapes=[pltpu.VMEM((t, QGROUP), F32), pltpu.VMEM((t, HEAD), F32)],
        compiler_params=_params("arbitrary", "arbitrary"),
    )(qcat, kv, kr, do)


def _split3(x):
    hi = x.astype(BF16)
    r1 = x - hi.astype(F32)
    mid = r1.astype(BF16)
    lo = (r1 - mid.astype(F32)).astype(BF16)
    return hi, mid, lo


def _tri_matmul(mask, x):
    m = mask.astype(BF16)
    return sum(jnp.dot(m, part, preferred_element_type=F32) for part in _split3(x))


def _sub_cumsum(g, tb):
    row = lax.broadcasted_iota(jnp.int32, (tb, tb), 0)
    col = lax.broadcasted_iota(jnp.int32, (tb, tb), 1)
    return _tri_matmul(jnp.where((col <= row) & (col // SUB == row // SUB), 1.0, 0.0), g)


def _sub_suffix_prefix(after, before, tb):
    row = lax.broadcasted_iota(jnp.int32, (tb, tb), 0)
    col = lax.broadcasted_iota(jnp.int32, (tb, tb), 1)
    same = col // SUB == row // SUB
    return (_tri_matmul(jnp.where((col >= row) & same, 1.0, 0.0), after)
            + _tri_matmul(jnp.where((col < row) & same, 1.0, 0.0), before))


def _lower_bound(logits):
    mx = jnp.max(logits, axis=0, keepdims=True)
    e = jnp.exp(logits - mx)
    return e[0:1, :] / jnp.sum(e, axis=0, keepdims=True)


def _hgrn_fwd(proj, cols, wdt, logits, out_norm, *, name):
    t = proj.shape[0]
    nh = wdt // HEAD
    tb = _pick(t, 128)
    ns = tb // SUB

    def body(hq_ref, hf_ref, hi_ref, hg_ref, lg_ref, w_ref, o_ref, yb_ref, st_ref, s_ref, q_s, k_s, b_s):
        @pl.when(pl.program_id(1) == 0)
        def _():
            s_ref[...] = jnp.zeros_like(s_ref)

        lb = _lower_bound(lg_ref[...])
        f = lb + (1.0 - lb) * _sigmoid(hf_ref[...])
        q_s[...] = _silu(hq_ref[...])
        k_s[...] = 1.0 - f
        b_s[...] = _sub_cumsum(jnp.log(f), tb)
        rowid = lax.broadcasted_iota(jnp.int32, (SUB, HEAD), 0)

        def sub(c, carry):
            rows = pl.ds(pl.multiple_of(c * SUB, SUB), SUB)
            qc, kc, bc, vc = q_s[rows, :], k_s[rows, :], b_s[rows, :], hi_ref[rows, :]
            st = s_ref[...]
            st_ref[0, c] = st
            bl = bc[SUB - 1:SUB, :]
            oc = lax.dot_general((qc * jnp.exp(bc)).astype(BF16), st.astype(BF16), (((1,), (1,)), ((), ())),
                                 preferred_element_type=F32)
            for s in range(SUB):
                e = jnp.where(rowid >= s, jnp.exp(bc - bc[s:s + 1, :]), 0.0)
                a = jnp.sum(qc * e * kc[s:s + 1, :], axis=1, keepdims=True)
                oc = oc + a * vc[s:s + 1, :]
            o_ref[rows, :] = oc
            kd = kc * jnp.exp(bl - bc)
            s_ref[...] = jnp.exp(bl) * st + lax.dot_general(vc.astype(BF16), kd.astype(BF16), (((0,), (0,)), ((), ())),
                                                             preferred_element_type=F32)
            return carry

        lax.fori_loop(0, ns, sub, 0)
        o = o_ref[...]
        r = lax.rsqrt(jnp.mean(o * o, axis=-1, keepdims=True) + NORM_EPS)
        yb_ref[...] = (o * r * w_ref[...] * _silu(hg_ref[...])).astype(BF16)

    blk = pl.BlockSpec((tb, HEAD), lambda h, j: (j, h))
    return pl.pallas_call(
        body, name=name, grid=(nh, t // tb),
        in_specs=[pl.BlockSpec((tb, HEAD), lambda h, j, o=c // HEAD: (j, h + o)) for c in cols]
        + [pl.BlockSpec((2, HEAD), lambda h, j: (0, h)), pl.BlockSpec((1, HEAD), lambda h, j: (0, 0))],
        out_specs=(blk, blk, pl.BlockSpec((1, ns, HEAD, HEAD), lambda h, j: (h, j, 0, 0))),
        out_shape=(jax.ShapeDtypeStruct((t, wdt), F32), jax.ShapeDtypeStruct((t, wdt), BF16),
                   jax.ShapeDtypeStruct((nh, t // SUB, HEAD, HEAD), F32)),
        scratch_shapes=[pltpu.VMEM((HEAD, HEAD), F32)] + [pltpu.VMEM((tb, HEAD), F32)] * 3,
        compiler_params=_params("parallel", "arbitrary"),
    )(proj, proj, proj, proj, logits, out_norm)


def _hgrn_bwd(proj, cols, wdt, o_raw, dyb, states, logits, out_norm, *, name):
    t = proj.shape[0]
    nh = wdt // HEAD
    tb = _pick(t, 128)
    ns = tb // SUB
    nb = t // tb

    def body(hq_ref, hf_ref, hi_ref, hg_ref, o_ref, dy_ref, st_ref, lg_ref, w_ref,
             dhq_ref, dhf_ref, dhi_ref, dhg_ref, dlb_ref, dw_ref,
             ds_ref, q_s, k_s, b_s, do_s, dq_s, dk_s, dv_s, after_s, before_s, thru_s):
        @pl.when(pl.program_id(1) == 0)
        def _():
            ds_ref[...] = jnp.zeros_like(ds_ref)
            dlb_ref[...] = jnp.zeros_like(dlb_ref)
            dw_ref[...] = jnp.zeros_like(dw_ref)

        lb = _lower_bound(lg_ref[...])
        hqv, hgv = hq_ref[...], hg_ref[...]
        sig = _sigmoid(hf_ref[...])
        f = lb + (1.0 - lb) * sig
        q_s[...] = _silu(hqv)
        k_s[...] = 1.0 - f
        b_s[...] = _sub_cumsum(jnp.log(f), tb)

        o = o_ref[...]
        r = lax.rsqrt(jnp.mean(o * o, axis=-1, keepdims=True) + NORM_EPS)
        nrm = o * r
        w = w_ref[...]
        dy = dy_ref[...].astype(F32)
        dhg_ref[...] = (dy * nrm * w * _dsilu(hgv)).astype(BF16)
        dnw = dy * _silu(hgv)
        dw_ref[0] += jnp.sum(dnw * nrm, axis=0, keepdims=True)
        tt = dnw * w
        do_s[...] = r * (tt - nrm * jnp.mean(tt * nrm, axis=-1, keepdims=True))
        rowid = lax.broadcasted_iota(jnp.int32, (SUB, HEAD), 0)

        def sub(cc, carry):
            c = ns - 1 - cc
            rows = pl.ds(pl.multiple_of(c * SUB, SUB), SUB)
            qc, kc, bc, vc, doc = q_s[rows, :], k_s[rows, :], b_s[rows, :], hi_ref[rows, :], do_s[rows, :]
            st = st_ref[0, c]
            dst = ds_ref[...]
            bl = bc[SUB - 1:SUB, :]
            eb = jnp.exp(bc)
            ekd = jnp.exp(bl - bc)
            qe, kd = qc * eb, kc * ekd
            dob, vcb = doc.astype(BF16), vc.astype(BF16)
            dq_st = jnp.dot(dob, st.astype(BF16), preferred_element_type=F32) * eb
            dk_st = jnp.dot(vcb, dst.astype(BF16), preferred_element_type=F32) * ekd
            dv = lax.dot_general(kd.astype(BF16), dst.astype(BF16), (((1,), (1,)), ((), ())), preferred_element_type=F32)
            dq_in = jnp.zeros_like(qc)
            dk_in = jnp.zeros_like(qc)
            for s in range(SUB):
                e = jnp.where(rowid >= s, jnp.exp(bc - bc[s:s + 1, :]), 0.0)
                ek = e * kc[s:s + 1, :]
                a = jnp.sum(qc * ek, axis=1, keepdims=True)
                da = jnp.sum(doc * vc[s:s + 1, :], axis=1, keepdims=True)
                dq_in = dq_in + da * ek
                dk_in = dk_in + jnp.where(rowid == s, jnp.sum(da * e * qc, axis=0, keepdims=True), 0.0)
                dv = dv + jnp.where(rowid == s, jnp.sum(a * doc, axis=0, keepdims=True), 0.0)
            ebl = jnp.exp(bl)
            ds_ref[...] = ebl * dst + lax.dot_general(dob, qe.astype(BF16), (((0,), (0,)), ((), ())),
                                                      preferred_element_type=F32)
            dq_s[rows, :] = dq_st + dq_in
            dk_s[rows, :] = dk_st + dk_in
            dv_s[rows, :] = dv
            after_s[rows, :] = qc * (dq_st + dq_in) - kc * dk_in
            before_s[rows, :] = kc * dk_st
            thru_s[rows, :] = jnp.broadcast_to(ebl * jnp.sum(st * dst, axis=0, keepdims=True), (SUB, HEAD))
            return carry

        lax.fori_loop(0, ns, sub, 0)
        dg = _sub_suffix_prefix(after_s[...], before_s[...], tb) + thru_s[...]
        dhq_ref[...] = (dq_s[...] * _dsilu(hqv)).astype(BF16)
        dft = dg / f - dk_s[...]
        dhf_ref[...] = (dft * (1.0 - lb) * sig * (1.0 - sig)).astype(BF16)
        dlb_ref[0] += jnp.sum(dft * (1.0 - sig), axis=0, keepdims=True)
        dhi_ref[...] = dv_s[...].astype(BF16)

    blk = pl.BlockSpec((tb, HEAD), lambda h, j: (nb - 1 - j, h))
    vec = pl.BlockSpec((1, 1, HEAD), lambda h, j: (h, 0, 0))
    tok = jax.ShapeDtypeStruct((t, wdt), BF16)
    per_head = jax.ShapeDtypeStruct((nh, 1, HEAD), F32)
    return pl.pallas_call(
        body, name=name, grid=(nh, nb),
        in_specs=[pl.BlockSpec((tb, HEAD), lambda h, j, o=c // HEAD: (nb - 1 - j, h + o)) for c in cols]
        + [blk, blk] + [pl.BlockSpec((1, ns, HEAD, HEAD), lambda h, j: (h, nb - 1 - j, 0, 0)),
                              pl.BlockSpec((2, HEAD), lambda h, j: (0, h)), pl.BlockSpec((1, HEAD), lambda h, j: (0, 0))],
        out_specs=(blk, blk, blk, blk, vec, vec),
        out_shape=(tok, tok, tok, tok, per_head, per_head),
        scratch_shapes=[pltpu.VMEM((HEAD, HEAD), F32)] + [pltpu.VMEM((tb, HEAD), F32)] * 10,
        compiler_params=_params("arbitrary", "arbitrary"),
    )(proj, proj, proj, proj, o_raw, dyb, states, logits, out_norm)


def _lb_logits_grad(logits, dlb, *, name):
    def body(lg_ref, d_ref, o_ref):
        lg = lg_ref[...]
        e = jnp.exp(lg - jnp.max(lg, axis=0, keepdims=True))
        p = e / jnp.sum(e, axis=0, keepdims=True)
        d = d_ref[...]
        rowid = lax.broadcasted_iota(jnp.int32, lg.shape, 0)
        dp = jnp.where(rowid == 0, d, 0.0)
        o_ref[...] = p * (dp - jnp.sum(p * dp, axis=0, keepdims=True))

    return pl.pallas_call(body, name=name, out_shape=jax.ShapeDtypeStruct(logits.shape, F32))(logits, dlb)


def _adamw(w, g, m, v, *, name, deps=()):
    r, c = w.shape
    tc = _pick(c, 2048) if c % LANE == 0 else c
    tr = _row_tile(r, tc * 4)

    def body(w_ref, g_ref, m_ref, v_ref, *rest):
        d_ref, nm_ref, nv_ref = rest[-3:]
        gv = g_ref[...]
        nm = ADAM_B1 * m_ref[...] + (1.0 - ADAM_B1) * gv
        nv = ADAM_B2 * v_ref[...] + (1.0 - ADAM_B2) * (gv * gv)
        m_hat = nm / (1.0 - ADAM_B1 ** ADAM_STEP)
        v_hat = nv / (1.0 - ADAM_B2 ** ADAM_STEP)
        d_ref[...] = -ADAM_LR * (m_hat / (jnp.sqrt(v_hat) + ADAM_EPS) + ADAM_WD * w_ref[...])
        nm_ref[...] = nm
        nv_ref[...] = nv

    spec = pl.BlockSpec((tr, tc), lambda i, j: (i, j))
    shp = jax.ShapeDtypeStruct((r, c), F32)
    return pl.pallas_call(
        body, name=name, grid=(r // tr, c // tc), in_specs=[spec] * 4 + [ANY] * len(deps), out_specs=[spec] * 3,
        out_shape=[shp, shp, shp], compiler_params=_params("parallel", "parallel"),
    )(w, g, m, v, *deps)


def _coords():
    return lax.axis_index("x"), lax.axis_index("y"), lax.axis_index("c")


def _other_chips(x, y):
    return [(1 - x, y), (x, 1 - y), (1 - x, 1 - y)]


ANY = pl.BlockSpec(memory_space=pl.ANY)


class _Layout:
    def __init__(self, d, dff, in_cols, q_lora, kv_lora, nh):
        assert q_lora == kv_lora and nh % 4 == 0 and dff % 4 == 0 and in_cols % 4 == 0 and d % 4 == 0
        self.d, self.dff, self.q_lora, self.nh = d, dff, q_lora, nh
        self.head = q_lora + kv_lora + ROPE
        self.pad = d - self.head
        self.dffp = -(-dff // LANE) * LANE
        self.ffn_pad = self.dffp - dff
        dffp = self.dffp
        nff, ncol, r_o, hps = dff // 4, in_cols // 4, d // 4, nh // 4
        assert self.head <= ncol
        names = ("ffn1_w_gate", "ffn1_w_up", "ffn1_w_down", "ffn2_w_gate", "ffn2_w_up", "ffn2_w_down")
        self.ffn_names = names
        self.off = {n: i * dffp for i, n in enumerate(names)}
        self.off.update(w_in=6 * dffp, mla_w_o=6 * dffp + 7 * d, hgrn_w_o=6 * dffp + 8 * d, w_out=6 * dffp + 9 * d)
        self.rows_wide = 6 * dffp + 10 * d
        self.off_q, self.off_kv, self.rows_narrow = 0, nh * QGROUP, 2 * nh * QGROUP
        self.loff = {n: i * nff for i, n in enumerate(names)}
        self.loff.update(w_in=6 * nff, mla_w_o=6 * nff + ncol, hgrn_w_o=6 * nff + ncol + r_o, w_out=6 * nff + ncol + 2 * r_o)
        self.lrows = {n: nff for n in names}
        self.lrows.update(w_in=ncol, mla_w_o=r_o, hgrn_w_o=r_o, w_out=r_o)
        self.lrows_wide = 6 * nff + ncol + 3 * r_o
        self.lrows_narrow = hps * (HEAD + ROPE) + hps * QGROUP
        self.loff_q, self.loff_kv = 0, hps * (HEAD + ROPE)
        self.nff, self.ncol, self.r_o, self.hps = nff, ncol, r_o, hps

    def segments(self, k):
        first = lambda a, b: jnp.where(k == 0, a, b) if not isinstance(k, int) else (a if k == 0 else b)
        segs = []
        for n in ("ffn1_w_gate", "ffn1_w_up", "ffn1_w_down"):
            segs.append((0, self.loff[n], self.nff, self.off[n] + self.nff * k, 0))
        w_in = self.off["w_in"]
        segs.append((0, self.loff["w_in"], self.head, w_in + first(0, self.ncol * k + self.pad), 1))
        segs.append((0, self.loff["w_in"] + self.head, self.ncol - self.head, w_in + self.ncol * k + self.d, 1))
        for n in ("mla_w_o", "hgrn_w_o", "w_out"):
            segs.append((0, self.loff[n], self.r_o, self.off[n] + self.r_o * k, 1))
        for hh in range(self.hps):
            segs.append((1, (HEAD + ROPE) * hh, HEAD + ROPE, QGROUP * (self.hps * k + hh), 2))
        segs.append((1, self.loff_kv, self.hps * QGROUP, self.off_kv + self.hps * QGROUP * k, 2))
        for n in ("ffn2_w_gate", "ffn2_w_up", "ffn2_w_down"):
            segs.append((0, self.loff[n], self.nff, self.off[n] + self.nff * k, 3))
        return segs

    def stream_rows(self, stream):
        return sum(s[2] for s in self.segments(0) if s[4] == stream)


N_STREAM = 4


def _half(ref, row, rows, half):
    return ref.at[half, pl.ds(row, rows)]


def _both(ref, row, rows):
    return ref.at[:, pl.ds(row, rows)]


def _wait_bytes_of(ref_like, send_sem, recv_sem, me, *, send):
    cp = pltpu.make_async_remote_copy(src_ref=ref_like, dst_ref=ref_like, send_sem=send_sem, recv_sem=recv_sem,
                                      device_id=me, device_id_type=MESH)
    if send:
        cp.wait_send()
    else:
        cp.wait_recv()


def _gather_weights(lwide, lnarrow, zwide, znarrow, lay, *, name):
    d, ql = lay.d, lay.q_lora
    qpad = QGROUP - HEAD - ROPE

    def body(l0, l1, z0, z1, w0, w1, send, recv, fsend, frecv, osend, orecv):
        x, y, c = _coords()
        me_chip = 2 * x + y
        chips = _other_chips(x, y)
        src, dst = (l0, l1), (w0, w1)

        def to_sibling(a, src_ref, drow, rows):
            pltpu.make_async_remote_copy(src_ref=src_ref, dst_ref=_both(dst[a], drow, rows), send_sem=osend.at[a],
                                         recv_sem=orecv.at[a], device_id=(x, y, 1 - c), device_id_type=MESH).start()

        for a, lrow, rows, drow, _ in lay.segments(me_chip):
            to_sibling(a, _both(src[a], lrow, rows), drow, rows)
        to_sibling(0, _both(z0, 0, lay.pad), lay.off["w_in"] + lay.head, lay.pad)
        if lay.ffn_pad:
            for n in lay.ffn_names:
                to_sibling(0, _both(z0, 0, lay.ffn_pad), lay.off[n] + lay.dff, lay.ffn_pad)
        for g in range(lay.nh):
            to_sibling(1, z1, QGROUP * g + HEAD + ROPE, qpad)

        for j, (px, py) in enumerate(chips):
            for a, lrow, rows, drow, st in lay.segments(me_chip):
                pltpu.make_async_remote_copy(
                    src_ref=_half(src[a], lrow, rows, c), dst_ref=_half(dst[a], drow, rows, c),
                    send_sem=send.at[j, st], recv_sem=recv.at[j, st], device_id=(px, py, c), device_id_type=MESH).start()

        def total(st):
            return _half(dst[1 if st == 2 else 0], 0, lay.stream_rows(st), 0)

        for st in range(N_STREAM):
            for j, (px, py) in enumerate(chips):
                _wait_bytes_of(total(st), send.at[j, st], recv.at[j, st], (x, y, c), send=False)
                for a, lrow, rows, drow, s2 in lay.segments(2 * px + py):
                    if s2 == st:
                        blk = _half(dst[a], drow, rows, c)
                        pltpu.make_async_remote_copy(
                            src_ref=blk, dst_ref=blk, send_sem=fsend.at[j, st], recv_sem=frecv.at[j, st],
                            device_id=(x, y, 1 - c), device_id_type=MESH).start()
        for st in range(N_STREAM):
            for j in range(3):
                _wait_bytes_of(total(st), fsend.at[j, st], frecv.at[j, st], (x, y, c), send=False)
        for st in range(N_STREAM):
            for j in range(3):
                _wait_bytes_of(total(st), fsend.at[j, st], frecv.at[j, st], (x, y, c), send=True)
                _wait_bytes_of(total(st), send.at[j, st], recv.at[j, st], (x, y, c), send=True)
        own = (_both(w0, 0, lay.lrows_wide + lay.pad + 6 * lay.ffn_pad), _both(w1, 0, lay.lrows_narrow + lay.nh * qpad))
        for a in range(2):
            _wait_bytes_of(own[a], osend.at[a], orecv.at[a], (x, y, c), send=False)
            _wait_bytes_of(own[a], osend.at[a], orecv.at[a], (x, y, c), send=True)

    sem = pltpu.SemaphoreType.DMA((3, N_STREAM))
    return pl.pallas_call(
        body, name=name, in_specs=[ANY] * 4, out_specs=[ANY] * 2,
        out_shape=[jax.ShapeDtypeStruct((2, lay.rows_wide, d // 2), BF16),
                   jax.ShapeDtypeStruct((2, lay.rows_narrow, ql // 2), BF16)],
        scratch_shapes=[sem, sem, sem, sem, pltpu.SemaphoreType.DMA((2,)), pltpu.SemaphoreType.DMA((2,))],
    )(lwide, lnarrow, zwide, znarrow)


def _add_sibling(g, recv, sel, *, name):
    rows, hw = recv.shape
    tr = _row_tile(rows, hw * 4)

    def body(sel_ref, g_ref, r_ref, o_ref):
        o_ref[...] = (g_ref[...] + r_ref[...]).astype(BF16)

    return pl.pallas_call(
        body, name=name, out_shape=jax.ShapeDtypeStruct((rows, hw), BF16),
        grid_spec=pltpu.PrefetchScalarGridSpec(
            num_scalar_prefetch=1, grid=(rows // tr,),
            in_specs=[pl.BlockSpec((None, tr, hw), lambda i, s: (s[0], i, 0)), pl.BlockSpec((tr, hw), lambda i, s: (i, 0))],
            out_specs=pl.BlockSpec((tr, hw), lambda i, s: (i, 0))),
        compiler_params=_params("parallel"),
    )(sel, g, recv)


class _Job:
    def __init__(self, a, blk, n_outer, n_inner, stride, start):
        self.a, self.blk, self.n_outer, self.n_inner, self.stride, self.start = a, blk, n_outer, n_inner, stride, start
        self.rows_out = n_outer * n_inner * blk

    def pieces(self, k):
        return [(self.start(k) + o * self.stride * self.blk, self.n_inner * self.blk) for o in range(self.n_outer)]


def _block_rows(rows, cap, *also):
    best = None
    for b in range(16, min(rows, cap) + 1, 16):
        if rows % b == 0 and all(v % b == 0 for v in also):
            best = b
    assert best is not None, (rows, also)
    return best


def _ffn_jobs(lay):
    b = _block_rows(lay.nff, 704, lay.dffp)
    return [_Job(0, b, 3, lay.nff // b, lay.dffp // b, lambda k: lay.nff * k)]


def _mix_jobs(lay):
    d, ncol, head, pad = lay.d, lay.ncol, lay.head, lay.pad
    first = lambda k, a, b: jnp.where(k == 0, a, b) if not isinstance(k, int) else (a if k == 0 else b)
    ba = _block_rows(head, 704, *[ncol * k + pad for k in (1, 2, 3)])
    bb = _block_rows(ncol - head, 704, *[ncol * k + d for k in (0, 1, 2, 3)])
    bo = _block_rows(lay.r_o, 704, d)
    bq = _block_rows(HEAD + ROPE, 704, QGROUP)
    bk = _block_rows(lay.hps * QGROUP, 704, lay.off_kv)
    return [_Job(0, ba, 1, head // ba, 0, lambda k: first(k, 0, ncol * k + pad)),
            _Job(0, bb, 1, (ncol - head) // bb, 0, lambda k: ncol * k + d),
            _Job(0, bo, 3, lay.r_o // bo, d // bo, lambda k: 7 * d + lay.r_o * k),
            _Job(1, bq, lay.hps, (HEAD + ROPE) // bq, QGROUP // bq, lambda k: QGROUP * lay.hps * k),
            _Job(1, bk, 1, lay.hps * QGROUP // bk, 0, lambda k: lay.off_kv + lay.hps * QGROUP * k)]


HBM = pl.BlockSpec(memory_space=pltpu.HBM)
SEMS = pl.BlockSpec(memory_space=pltpu.SEMAPHORE)
SPLIT = dict(has_side_effects=pltpu.SideEffectType.DATAFLOW_SIDE_EFFECTING)


def _in_hbm(a):
    return pltpu.with_memory_space_constraint(a, pltpu.HBM)


def _swap_start(gs, *, name):
    n = len(gs)
    lands = [lax.empty(g.shape[1:], g.dtype) for g in gs]

    def body(*refs):
        g_refs, land_refs, send, recv, token = refs[:n], refs[n:2 * n], refs[2 * n], refs[2 * n + 1], refs[-1]
        x, y, c = _coords()
        for a in range(n):
            pltpu.make_async_remote_copy(src_ref=g_refs[a].at[1 - c], dst_ref=land_refs[a], send_sem=send.at[a],
                                         recv_sem=recv.at[a], device_id=(x, y, 1 - c), device_id_type=MESH).start()
        token[...] = jnp.zeros_like(token)

    thru = [pltpu.HBM(a.shape, a.dtype) for a in gs + lands]
    out = pl.pallas_call(
        body, name=name, in_specs=[HBM] * (2 * n),
        out_shape=(pltpu.SemaphoreType.DMA((n,)), pltpu.SemaphoreType.DMA((n,)), *thru, jax.ShapeDtypeStruct((8, LANE), F32)),
        out_specs=(SEMS, SEMS, *[HBM] * (2 * n), pl.BlockSpec(memory_space=pltpu.VMEM)),
        input_output_aliases={i: 2 + i for i in range(2 * n)}, compiler_params=pltpu.CompilerParams(**SPLIT),
    )(*[_in_hbm(a) for a in gs + lands])
    return dict(send=out[0], recv=out[1], bufs=list(out[2:2 + 2 * n]), n=n), out[-1]


def _swap_wait(handle, after, *, name):
    n = handle["n"]

    def body(*refs):
        g_refs, land_refs, send, recv = refs[:n], refs[n:2 * n], refs[2 * n], refs[2 * n + 1]
        x, y, c = _coords()
        for a in range(n):
            cp = pltpu.make_async_remote_copy(src_ref=g_refs[a].at[1 - c], dst_ref=land_refs[a], send_sem=send.at[a],
                                              recv_sem=recv.at[a], device_id=(x, y, 1 - c), device_id_type=MESH)
            cp.wait_send()
            cp.wait_recv()

    out = pl.pallas_call(
        body, name=name, in_specs=[HBM] * (2 * n) + [SEMS, SEMS] + [ANY] * len(after),
        out_shape=[pltpu.HBM(a.shape, a.dtype) for a in handle["bufs"]], out_specs=[HBM] * (2 * n),
        input_output_aliases={i: i for i in range(2 * n)}, compiler_params=pltpu.CompilerParams(**SPLIT),
    )(*handle["bufs"], handle["send"], handle["recv"], *after)
    return list(out[:n]), list(out[n:])


def _exchange_start(ss, jobs, *, name):
    n = len(ss)
    lands = [lax.empty((3,) + s.shape, s.dtype) for s in ss]

    def body(*refs):
        s_refs, land_refs, send, recv, token = refs[:n], refs[n:2 * n], refs[2 * n], refs[2 * n + 1], refs[-1]
        x, y, c = _coords()
        for j, (px, py) in enumerate(_other_chips(x, y)):
            for job in jobs:
                for row, rows in job.pieces(2 * px + py):
                    pltpu.make_async_remote_copy(
                        src_ref=s_refs[job.a].at[pl.ds(row, rows)], dst_ref=land_refs[job.a].at[j, pl.ds(row, rows)],
                        send_sem=send.at[n * j + job.a], recv_sem=recv.at[n * j + job.a], device_id=(px, py, c),
                        device_id_type=MESH).start()
        token[...] = jnp.zeros_like(token)

    thru = [pltpu.HBM(a.shape, a.dtype) for a in ss + lands]
    out = pl.pallas_call(
        body, name=name, in_specs=[HBM] * (2 * n),
        out_shape=(pltpu.SemaphoreType.DMA((3 * n,)), pltpu.SemaphoreType.DMA((3 * n,)), *thru, jax.ShapeDtypeStruct((8, LANE), F32)),
        out_specs=(SEMS, SEMS, *[HBM] * (2 * n), pl.BlockSpec(memory_space=pltpu.VMEM)),
        input_output_aliases={i: 2 + i for i in range(2 * n)}, compiler_params=pltpu.CompilerParams(**SPLIT),
    )(*[_in_hbm(a) for a in ss + lands])
    return dict(send=out[0], recv=out[1], bufs=list(out[2:2 + 2 * n]), n=n, jobs=jobs), out[-1]


def _exchange_wait(handle, after, *, name):
    n, jobs = handle["n"], handle["jobs"]
    total = [sum(rows for job in jobs if job.a == a for _, rows in job.pieces(0)) for a in range(n)]

    def body(*refs):
        s_refs, land_refs, send, recv = refs[:n], refs[n:2 * n], refs[2 * n], refs[2 * n + 1]
        x, y, c = _coords()
        for a in range(n):
            for j in range(3):
                all_rows = land_refs[a].at[0, pl.ds(0, total[a])]
                cp = pltpu.make_async_remote_copy(src_ref=all_rows, dst_ref=all_rows, send_sem=send.at[n * j + a],
                                                  recv_sem=recv.at[n * j + a], device_id=(x, y, c), device_id_type=MESH)
                cp.wait_send()
                cp.wait_recv()

    out = pl.pallas_call(
        body, name=name, in_specs=[HBM] * (2 * n) + [SEMS, SEMS] + [ANY] * len(after),
        out_shape=[pltpu.HBM(a.shape, a.dtype) for a in handle["bufs"]], out_specs=[HBM] * (2 * n),
        input_output_aliases={i: i for i in range(2 * n)}, compiler_params=pltpu.CompilerParams(**SPLIT),
    )(*handle["bufs"], handle["send"], handle["recv"], *after)
    return list(out[:n]), list(out[n:])


def _add_shard(s, land, job, sel, k, *, name):
    hw = s.shape[1]
    blk, no, ni, stride = job.blk, job.n_outer, job.n_inner, job.stride
    scal = jnp.stack([sel, job.start(k) // blk]).astype(jnp.int32)

    def body(sc_ref, own_ref, r_ref, o_ref):
        o_ref[...] = ((own_ref[...].astype(F32) + r_ref[0].astype(F32)) + r_ref[1].astype(F32)) + r_ref[2].astype(F32)

    return pl.pallas_call(
        body, name=name, out_shape=jax.ShapeDtypeStruct((2, job.rows_out, hw), F32),
        grid_spec=pltpu.PrefetchScalarGridSpec(
            num_scalar_prefetch=1, grid=(no, ni),
            in_specs=[pl.BlockSpec((blk, hw), lambda o, b, sc: (sc[1] + o * stride + b, 0)),
                      pl.BlockSpec((3, blk, hw), lambda o, b, sc: (0, sc[1] + o * stride + b, 0))],
            out_specs=pl.BlockSpec((None, blk, hw), lambda o, b, sc: (sc[0], o * ni + b, 0))),
        compiler_params=_params("parallel", "parallel"),
    )(scal, s, land)


def _join_list(fs, *, name):
    n = len(fs)

    def body(*refs):
        f_refs, send_sems, recv_sems = refs[n:2 * n], refs[2 * n], refs[2 * n + 1]
        x, y, c = _coords()
        copies = [pltpu.make_async_remote_copy(
            src_ref=f.at[c], dst_ref=f.at[c], send_sem=send_sems.at[a], recv_sem=recv_sems.at[a],
            device_id=(x, y, 1 - c), device_id_type=MESH) for a, f in enumerate(f_refs)]
        for cp in copies:
            cp.start()
        for cp in copies:
            cp.wait()

    sem = pltpu.SemaphoreType.DMA((n,))
    return pl.pallas_call(
        body, name=name, in_specs=[ANY] * n, out_specs=[ANY] * n, input_output_aliases={i: i for i in range(n)},
        out_shape=[jax.ShapeDtypeStruct(f.shape, f.dtype) for f in fs], scratch_shapes=[sem, sem],
    )(*fs)


def _all_reduce_small(vec, *, name):
    n = vec.shape[1]

    def body(v_ref, o_ref, buf, send_sems, recv_sems):
        x, y, c = _coords()
        me = 4 * x + 2 * y + c
        buf[me] = v_ref[...]
        copies = []
        for m in range(1, 8):
            peer = (x ^ ((m >> 2) & 1), y ^ ((m >> 1) & 1), c ^ (m & 1))
            copies.append(pltpu.make_async_remote_copy(
                src_ref=v_ref, dst_ref=buf.at[me], send_sem=send_sems.at[m - 1], recv_sem=recv_sems.at[m - 1],
                device_id=peer, device_id_type=MESH))
        for cp in copies:
            cp.start()
        for cp in copies:
            cp.wait()
        acc = buf[0]
        for d in range(1, 8):
            acc = acc + buf[d]
        o_ref[...] = acc

    return pl.pallas_call(
        body, name=name, out_shape=jax.ShapeDtypeStruct((1, n), F32),
        in_specs=[pl.BlockSpec(memory_space=pltpu.VMEM)], out_specs=pl.BlockSpec(memory_space=pltpu.VMEM),
        scratch_shapes=[pltpu.VMEM((8, 1, n), F32), pltpu.SemaphoreType.DMA((7,)), pltpu.SemaphoreType.DMA((7,))],
    )(vec)


def _ffn_fwd(x, n_pre, n_post, wide, lay, tag):
    wg, wu, wd = ((wide, lay.off[f"{tag}_w_{p}"], lay.dffp) for p in ("gate", "up", "down"))
    h = _norm_fwd(x, n_pre, name=f"{tag}_norm_pre", out_dtype=BF16)
    g = _mm([(h, wg)], name=f"{tag}_gate", mode="nt")
    u = _mm([(h, wu)], name=f"{tag}_up", mode="nt")
    a = _swiglu_fwd(g, u, name=f"{tag}_swiglu")
    yv = _mm([(a, wd)], name=f"{tag}_down", mode="nn")
    out = _norm_fwd(yv, n_post, name=f"{tag}_norm_post", resid=x, scale=MACARON_SCALE)
    return out, (x, h, g, u, a, yv)


def _ffn_bwd(dout, saved, n_pre, n_post, wide, lay, tag, deps=(), after_act=None, after_dw=None):
    x, h, g, u, a, yv = saved
    og, ou, od = (lay.off[f"{tag}_w_{p}"] for p in ("gate", "up", "down"))
    gbuf = lax.empty((2, 3 * lay.dffp, lay.d // 2), F32)
    dy, dn_post = _norm_bwd(yv, n_post, dout, name=f"{tag}_norm_post_bwd", scale=MACARON_SCALE)
    da = _mm([(dy, (wide, od, lay.dffp))], name=f"{tag}_down_dx", mode="nt", deps=deps)
    dg, du = _swiglu_bwd(da, g, u, name=f"{tag}_swiglu_bwd")
    deps = after_act(du) if after_act is not None else ()
    gbuf = _mm([(a, dy)], name=f"{tag}_down_dw", mode="tn", into=(gbuf, 2 * lay.dffp), deps=deps)
    gbuf = _mm([(dg, h)], name=f"{tag}_gate_dw", mode="tn", into=(gbuf, 0))
    gbuf = _mm([(du, h)], name=f"{tag}_up_dw", mode="tn", into=(gbuf, lay.dffp))
    deps = after_dw(gbuf)
    dh = _mm([(dg, (wide, og, lay.dffp)), (du, (wide, ou, lay.dffp))], name=f"{tag}_up_dx", mode="nn", deps=deps)
    dx, dn_pre = _norm_bwd(x, n_pre, dh, name=f"{tag}_norm_pre_bwd", dres=dout)
    return dx, dn_pre, dn_post


def _rope_tables(positions):
    half = ROPE // 2
    inv_freq = ROPE_THETA ** (-jnp.arange(half, dtype=F32) / half)
    ang = positions.astype(F32)[:, None] * inv_freq
    cos, sin = jnp.cos(ang), jnp.sin(ang)
    z = jnp.zeros_like(cos)
    z2 = jnp.zeros((positions.shape[0], LANE - ROPE), F32)
    return (jnp.concatenate([cos, cos, z2], axis=1), jnp.concatenate([-sin, z, z2], axis=1),
            jnp.concatenate([z, sin, z2], axis=1))


def kernel(x, positions, ffn1_norm_pre, ffn1_w_gate, ffn1_w_up, ffn1_w_down, ffn1_norm_post, mix_norm_pre, w_in, mla_q_norm, mla_w_q_up, mla_kv_norm, mla_w_kv_up, mla_w_o, hgrn_lb_logits, hgrn_out_norm, hgrn_w_o, w_out, mix_norm_post, ffn2_norm_pre, ffn2_w_gate, ffn2_w_up, ffn2_w_down, ffn2_norm_post, loss_target, m_ffn1_norm_pre, m_ffn1_w_gate, m_ffn1_w_up, m_ffn1_w_down, m_ffn1_norm_post, m_mix_norm_pre, m_w_in, m_mla_q_norm, m_mla_w_q_up, m_mla_kv_norm, m_mla_w_kv_up, m_mla_w_o, m_hgrn_lb_logits, m_hgrn_out_norm, m_hgrn_w_o, m_w_out, m_mix_norm_post, m_ffn2_norm_pre, m_ffn2_w_gate, m_ffn2_w_up, m_ffn2_w_down, m_ffn2_norm_post, v_ffn1_norm_pre, v_ffn1_w_gate, v_ffn1_w_up, v_ffn1_w_down, v_ffn1_norm_post, v_mix_norm_pre, v_w_in, v_mla_q_norm, v_mla_w_q_up, v_mla_kv_norm, v_mla_w_kv_up, v_mla_w_o, v_hgrn_lb_logits, v_hgrn_out_norm, v_hgrn_w_o, v_w_out, v_mix_norm_post, v_ffn2_norm_pre, v_ffn2_w_gate, v_ffn2_w_up, v_ffn2_w_down, v_ffn2_norm_post):
    given = dict(locals())
    wts = {n: given[n] for n in ALL_WEIGHTS}
    mom = {n: given["m_" + n] for n in ALL_WEIGHTS}
    var = {n: given["v_" + n] for n in ALL_WEIGHTS}
    xin = x[0]
    target = loss_target[0]
    t, d = xin.shape
    cx, cy, cc = _coords()

    q_lora, kv_lora = mla_q_norm.shape[1], mla_kv_norm.shape[1]
    nh_mla = 4 * mla_w_kv_up.shape[2] // QGROUP
    lay = _Layout(d, 4 * ffn1_w_gate.shape[2], 4 * w_in.shape[2], q_lora, kv_lora, nh_mla)
    col_sharded = lambda n: wts[n][0].T.astype(BF16)
    row_sharded = lambda n: wts[n][0].astype(BF16)
    lwide = jnp.concatenate([col_sharded("ffn1_w_gate"), col_sharded("ffn1_w_up"), row_sharded("ffn1_w_down"),
                             col_sharded("ffn2_w_gate"), col_sharded("ffn2_w_up"), row_sharded("ffn2_w_down"),
                             col_sharded("w_in"), row_sharded("mla_w_o"), row_sharded("hgrn_w_o"), row_sharded("w_out")])
    lnarrow = jnp.concatenate([col_sharded("mla_w_q_up"), col_sharded("mla_w_kv_up")])
    halves = lambda a: a.reshape(a.shape[0], 2, a.shape[1] // 2).transpose(1, 0, 2)
    wide, narrow = _gather_weights(halves(lwide), halves(lnarrow), jnp.zeros((2, max(lay.pad, lay.ffn_pad), d // 2), BF16),
                                   jnp.zeros((2, QGROUP - HEAD - ROPE, q_lora // 2), BF16), lay, name="gather_weights")
    w_in_v = (wide, lay.off["w_in"], 7 * d)
    w_q_v = (narrow, lay.off_q, nh_mla * QGROUP)
    w_kv_v = (narrow, lay.off_kv, nh_mla * QGROUP)
    w_o_v = {n: (wide, lay.off[n], d) for n in ("mla_w_o", "hgrn_w_o", "w_out")}
    col_kr = q_lora + kv_lora
    hgrn_cols = [d, 2 * d, 3 * d, 4 * d]
    col_ga, col_gb = 5 * d, 6 * d
    tabs = _rope_tables(positions[0])
    scale = (HEAD + ROPE) ** -0.5

    x1, saved1 = _ffn_fwd(xin, ffn1_norm_pre, ffn1_norm_post, wide, lay, "ffn1")

    h2 = _norm_fwd(x1, mix_norm_pre, name="mix_norm_pre", out_dtype=BF16)
    proj = _mm([(h2, w_in_v)], name="mix_in", mode="nt")
    cqn = _norm_fwd(proj, mla_q_norm, name="mla_q_norm", out_dtype=BF16, col=0)
    ckvn = _norm_fwd(proj, mla_kv_norm, name="mla_kv_norm", out_dtype=BF16, col=q_lora)
    qp = _mm([(cqn, w_q_v)], name="mla_q_up", mode="nt")
    kvb = _mm([(ckvn, w_kv_v)], name="mla_kv_up", mode="nt", out_dtype=BF16)
    qcat = _rope(qp, tabs, name="rope_q", group=QGROUP, backward=False, out_dtype=BF16)
    krot = _rope(proj, tabs, name="rope_k", group=LANE, backward=False, out_dtype=BF16, col=col_kr, ngroup=1)
    o_mla = _attn_fwd(qcat, kvb, krot, name="mla_attention", scale=scale)
    y_a = _mm([(o_mla, w_o_v["mla_w_o"])], name="mla_out", mode="nn")

    o_raw, yb, states = _hgrn_fwd(proj, hgrn_cols, d, hgrn_lb_logits, hgrn_out_norm, name="hgrn_scan")
    y_b = _mm([(yb, w_o_v["hgrn_w_o"])], name="hgrn_out", mode="nn")

    merged = _merge_fwd(proj, col_ga, col_gb, y_a, y_b, name="mix_merge")
    y_mix = _mm([(merged, w_o_v["w_out"])], name="mix_out", mode="nn")
    x2 = _norm_fwd(y_mix, mix_norm_post, name="mix_norm_post", resid=x1, scale=1.0)

    x3, saved2 = _ffn_fwd(x2, ffn2_norm_pre, ffn2_norm_post, wide, lay, "ffn2")
    dx3, loss_local = _loss_head(x3, target, name="loss_head")

    grads, deltas, new_m, new_v = {}, {}, {}, {}
    sel = cc.astype(jnp.int32)
    sel1 = jnp.reshape(sel, (1,))
    me_chip = (2 * cx + cy).astype(jnp.int32)

    def reduce_mid(handle, after, jobs, tag):
        bufs, recvd = _swap_wait(handle, after, name=f"grad_swap_{tag}_wait")
        sums = [_add_sibling(b, r, sel1, name=f"grad_add_sibling_{tag}_{i}") for i, (b, r) in enumerate(zip(bufs, recvd))]
        return _exchange_start(sums, jobs, name=f"grad_exchange_{tag}")

    def reduce_end(handle, after, tag):
        sums, lands = _exchange_wait(handle, after, name=f"grad_exchange_{tag}_wait")
        parts = [_add_shard(sums[job.a], lands[job.a], job, sel, me_chip, name=f"grad_add_chips_{tag}_{i}")
                 for i, job in enumerate(handle["jobs"])]
        return _join_list(parts, name=f"grad_join_{tag}")

    def natural(part, lo, rows, transposed):
        g_n = part[:, lo:lo + rows]
        hw_n = g_n.shape[2]
        return g_n.transpose(0, 2, 1).reshape(2 * hw_n, rows) if transposed else g_n.transpose(1, 0, 2).reshape(rows, 2 * hw_n)

    def adam(names, deps=()):
        for i, n in enumerate(names):
            shp = wts[n].shape
            two_d = (lambda a: a[0]) if n in BIG_WEIGHTS else (lambda a: a)
            dl, nm, nv = _adamw(two_d(wts[n]), grads[n], two_d(mom[n]), two_d(var[n]), name=f"adamw_{n}",
                                deps=deps if i == 0 else ())
            grads[n] = grads[n].reshape(shp)
            deltas[n], new_m[n], new_v[n] = dl.reshape(shp), nm.reshape(shp), nv.reshape(shp)
        return [deltas[n] for n in names]

    def ffn_grads(joined, tag, deps=()):
        nff = lay.nff
        grads[f"{tag}_w_gate"] = natural(joined[0], 0, nff, True)
        grads[f"{tag}_w_up"] = natural(joined[0], nff, nff, True)
        grads[f"{tag}_w_down"] = natural(joined[0], 2 * nff, nff, False)
        return adam([f"{tag}_w_gate", f"{tag}_w_up", f"{tag}_w_down"], deps)

    swaps = {}

    def start_swap(tag):
        def hook(gbuf):
            swaps[tag], started = _swap_start([gbuf], name=f"grad_swap_{tag}")
            return [started]
        return hook

    dx2, grads["ffn2_norm_pre"], grads["ffn2_norm_post"] = _ffn_bwd(
        dx3, saved2, ffn2_norm_pre, ffn2_norm_post, wide, lay, "ffn2", after_dw=start_swap("ffn2"))

    gwide = lax.empty((2, 10 * d, d // 2), F32)
    gnarrow = lax.empty((2, lay.rows_narrow, q_lora // 2), F32)
    dy_mix, grads["mix_norm_post"] = _norm_bwd(y_mix, mix_norm_post, dx2, name="mix_norm_post_bwd")
    dmerged = _mm([(dy_mix, w_o_v["w_out"])], name="mix_out_dx", mode="nt")
    gwide = _mm([(merged, dy_mix)], name="mix_out_dw", mode="tn", into=(gwide, 9 * d))
    dga, dgb, dy_a, dy_b = _merge_bwd(dmerged, proj, col_ga, col_gb, y_a, y_b, name="mix_merge_bwd")

    do_mla = _mm([(dy_a, w_o_v["mla_w_o"])], name="mla_out_dx", mode="nt")
    gwide = _mm([(o_mla, dy_a)], name="mla_out_dw", mode="tn", into=(gwide, 7 * d))
    dqcat, dkv, dkr = _attn_bwd(qcat, kvb, krot, do_mla, name="mla_attention_bwd", scale=scale)
    exch2, tok = reduce_mid(swaps["ffn2"], [dkr], _ffn_jobs(lay), "ffn2")

    dqp = _rope(dqcat, tabs, name="rope_q_bwd", group=QGROUP, backward=True, out_dtype=BF16)
    dk_r = _rope(dkr, tabs, name="rope_k_bwd", group=LANE, backward=True, out_dtype=BF16)
    dcqn = _mm([(dqp, w_q_v)], name="mla_q_up_dx", mode="nn", deps=[tok])
    gnarrow = _mm([(dqp, cqn)], name="mla_q_up_dw", mode="tn", into=(gnarrow, lay.off_q))
    dkvb = dkv.astype(BF16)
    dckvn = _mm([(dkvb, w_kv_v)], name="mla_kv_up_dx", mode="nn")
    gnarrow = _mm([(dkvb, ckvn)], name="mla_kv_up_dw", mode="tn", into=(gnarrow, lay.off_kv))
    dc_q, grads["mla_q_norm"] = _norm_bwd(proj, mla_q_norm, dcqn, name="mla_q_norm_bwd", col=0, dx_dtype=BF16)
    dc_kv, grads["mla_kv_norm"] = _norm_bwd(proj, mla_kv_norm, dckvn, name="mla_kv_norm_bwd", col=q_lora, dx_dtype=BF16)

    dyb = _mm([(dy_b, w_o_v["hgrn_w_o"])], name="hgrn_out_dx", mode="nt")
    gwide = _mm([(yb, dy_b)], name="hgrn_out_dw", mode="tn", into=(gwide, 8 * d))
    dhq, dhf, dhi, dhg, dlb_h, dnorm_h = _hgrn_bwd(proj, hgrn_cols, d, o_raw, dyb, states, hgrn_lb_logits, hgrn_out_norm,
                                                   name="hgrn_scan_bwd")

    dproj = jnp.concatenate([dc_q, dc_kv, dk_r, jnp.zeros((t, d - col_kr - LANE), BF16), dhq, dhf, dhi, dhg, dga, dgb], axis=1)
    dh2 = _mm([(dproj, w_in_v)], name="mix_in_dx", mode="nn")
    gwide = _mm([(dproj, h2)], name="mix_in_dw", mode="tn", into=(gwide, 0))
    dx1, grads["mix_norm_pre"] = _norm_bwd(x1, mix_norm_pre, dh2, name="mix_norm_pre_bwd", dres=dx2)
    swap_m, tok = _swap_start([gwide, gnarrow], name="grad_swap_mix")
    joined2 = reduce_end(exch2, [dx1], "ffn2")

    exchanges = {}

    def mix_exchange(after):
        exchanges["mix"], started = reduce_mid(swap_m, [after], _mix_jobs(lay), "mix")
        return [started]

    dx0, grads["ffn1_norm_pre"], grads["ffn1_norm_post"] = _ffn_bwd(
        dx1, saved1, ffn1_norm_pre, ffn1_norm_post, wide, lay, "ffn1", deps=[tok], after_act=mix_exchange,
        after_dw=start_swap("ffn1"))
    exch1, tok = reduce_mid(swaps["ffn1"], [dx0], _ffn_jobs(lay), "ffn1")

    joined_m = reduce_end(exchanges["mix"], [dx0, tok], "mix")
    done = ffn_grads(joined2, "ffn2")
    grads["w_in"] = natural(jnp.concatenate([joined_m[0], joined_m[1]], axis=1), 0, lay.ncol, True)
    for i, n in enumerate(("mla_w_o", "hgrn_w_o", "w_out")):
        grads[n] = natural(joined_m[2], i * lay.r_o, lay.r_o, False)
    grads["mla_w_q_up"] = natural(joined_m[3], 0, lay.hps * (HEAD + ROPE), True)
    grads["mla_w_kv_up"] = natural(joined_m[4], 0, lay.hps * QGROUP, True)
    done += adam(["w_in", "mla_w_q_up", "mla_w_kv_up", "mla_w_o", "hgrn_w_o", "w_out"])

    dlb = dlb_h.reshape(1, -1)
    dnorm = jnp.sum(dnorm_h, axis=0)
    small = {**{n: grads[n] for n in SMALL_WEIGHTS if n not in ("hgrn_lb_logits", "hgrn_out_norm")},
             "hgrn_lb_logits": dlb, "hgrn_out_norm": dnorm}
    vec = jnp.concatenate([small[n] for n in SMALL_WEIGHTS], axis=1)
    vec = _all_reduce_small(vec, name="grad_all_reduce_small")
    off = 0
    for n in SMALL_WEIGHTS:
        w_n = small[n].shape[1]
        grads[n] = vec[:, off:off + w_n]
        off += w_n
    grads["hgrn_lb_logits"] = _lb_logits_grad(hgrn_lb_logits, grads["hgrn_lb_logits"], name="lb_logits_grad")

    done += adam(list(SMALL_WEIGHTS))
    ffn_grads(reduce_end(exch1, done, "ffn1"), "ffn1")

    loss = lax.psum(loss_local, ("x", "y", "c"))
    dx_out = dx0.reshape(x.shape)
    return (loss, dx_out, *[grads[n] for n in ALL_WEIGHTS], *[deltas[n] for n in ALL_WEIGHTS],
            *[new_m[n] for n in ALL_WEIGHTS], *[new_v[n] for n in ALL_WEIGHTS])
```

```python
import functools

import jax
import jax.numpy as jnp
from jax import lax
from jax.experimental import pallas as pl
from jax.experimental.pallas import tpu as pltpu

F32 = jnp.float32
BF16 = jnp.bfloat16
MESH = pl.DeviceIdType.MESH

NORM_EPS = 1e-6
MACARON_SCALE = 0.5
ROPE_THETA = 10000.0
HEAD = 128
ROPE = 64
QGROUP = 2 * HEAD
SUB = 16
ADAM_LR, ADAM_B1, ADAM_B2, ADAM_EPS, ADAM_WD, ADAM_STEP = 0.001, 0.9, 0.999, 1e-08, 0.01, 10

LANE = 128
VMEM_LIMIT = 48 * 1024 * 1024
MM_TILE = 1024
MM_TILE_WIDE = 1536

BIG_WEIGHTS = ("ffn1_w_gate", "ffn1_w_up", "ffn1_w_down", "w_in", "mla_w_q_up", "mla_w_kv_up",
               "mla_w_o", "hgrn_w_o", "w_out", "ffn2_w_gate", "ffn2_w_up", "ffn2_w_down")
COL_SHARDED = ("ffn1_w_gate", "ffn1_w_up", "w_in", "mla_w_q_up", "mla_w_kv_up", "ffn2_w_gate", "ffn2_w_up")
SMALL_WEIGHTS = ("ffn1_norm_pre", "ffn1_norm_post", "mix_norm_pre", "mla_q_norm", "mla_kv_norm",
                 "hgrn_lb_logits", "hgrn_out_norm", "mix_norm_post", "ffn2_norm_pre", "ffn2_norm_post")
ALL_WEIGHTS = ("ffn1_norm_pre", "ffn1_w_gate", "ffn1_w_up", "ffn1_w_down", "ffn1_norm_post", "mix_norm_pre",
               "w_in", "mla_q_norm", "mla_w_q_up", "mla_kv_norm", "mla_w_kv_up", "mla_w_o", "hgrn_lb_logits",
               "hgrn_out_norm", "hgrn_w_o", "w_out", "mix_norm_post", "ffn2_norm_pre", "ffn2_w_gate",
               "ffn2_w_up", "ffn2_w_down", "ffn2_norm_post")


def _params(*sem):
    return pltpu.CompilerParams(dimension_semantics=sem or None, vmem_limit_bytes=VMEM_LIMIT)


def _pick(n, cap, offset=0):
    if n <= cap and offset % n == 0:
        return n
    best = None
    for t in range(LANE, min(n, cap) + 1, LANE):
        if n % t == 0 and offset % t == 0:
            best = t
    assert best is not None, (n, cap, offset)
    return best


def _row_tile(n, row_bytes, budget=1 << 20):
    best = None
    for t in range(8, n + 1, 8):
        if n % t == 0 and t * row_bytes <= budget:
            best = t
    return n if best is None else best


def _sigmoid(x):
    return 1.0 / (1.0 + jnp.exp(-x))


def _silu(x):
    return x * _sigmoid(x)


def _dsilu(x):
    s = _sigmoid(x)
    return s * (1.0 + x * (1.0 - s))


def _mm(pairs, *, name, mode="nn", out_dtype=F32, into=None, deps=()):
    halves = isinstance(pairs[0][1], tuple)
    assert halves or mode == "tn"
    pairs = [(a, b if halves else (b, 0, b.shape[0])) for a, b in pairs]
    a0, (b0, b_off, b_rows) = pairs[0]
    hw = b0.shape[2] if halves else (into[0].shape[2] if into is not None else None)
    if mode == "nn":
        (m, kdim), n = a0.shape, 2 * hw
    elif mode == "nt":
        (m, kdim), n = a0.shape, b_rows
        assert kdim == 2 * hw
    else:
        (kdim, m), n = a0.shape, b0.shape[1]
    out_off = 0 if into is None else into[1]
    tm = _pick(m, MM_TILE_WIDE if mode == "tn" else MM_TILE, out_off)
    tn = hw if (mode == "nn" or into is not None) else _pick(n, MM_TILE_WIDE, b_off if mode == "nt" else 0)
    tk = hw if mode == "nt" else _pick(kdim, MM_TILE, b_off if mode == "nn" else 0)
    assert n % tn == 0 and kdim % tk == 0
    nk = kdim // tk
    npair = len(pairs)
    dims = {"nn": (((1,), (0,)), ((), ())), "nt": (((1,), (1,)), ((), ())), "tn": (((0,), (0,)), ((), ()))}[mode]

    def body(*refs):
        ins, o_ref, acc_ref = refs[:2 * npair], refs[-2], refs[-1]
        k = pl.program_id(2)

        @pl.when(k == 0)
        def _():
            acc_ref[...] = jnp.zeros_like(acc_ref)

        for p in range(npair):
            a = ins[2 * p][...].astype(BF16)
            b = ins[2 * p + 1][...].astype(BF16)
            acc_ref[...] += lax.dot_general(a, b, dims, preferred_element_type=F32)

        @pl.when(k == nk - 1)
        def _():
            o_ref[...] = acc_ref[...].astype(o_ref.dtype)

    a_spec = pl.BlockSpec((tk, tm), lambda i, j, k: (k, i)) if mode == "tn" else pl.BlockSpec((tm, tk), lambda i, j, k: (i, k))
    in_specs, flat = [], []
    for a, (b, off, _) in pairs:
        if mode == "nt":
            b_spec = pl.BlockSpec((None, tn, tk), lambda i, j, k, o=off // tn: (k, j + o, 0))
        elif mode == "nn":
            b_spec = pl.BlockSpec((None, tk, tn), lambda i, j, k, o=off // tk: (j, k + o, 0))
        else:
            b_spec = pl.BlockSpec((tk, tn), lambda i, j, k: (k, j))
        in_specs += [a_spec, b_spec]
        flat += [a, b]
    for dep in deps:
        in_specs.append(pl.BlockSpec(memory_space=pl.ANY))
        flat.append(dep)
    if into is None:
        out_shape, aliases = jax.ShapeDtypeStruct((m, n), out_dtype), {}
        out_spec = pl.BlockSpec((tm, tn), lambda i, j, k: (i, j))
    else:
        out_shape, aliases = jax.ShapeDtypeStruct(into[0].shape, into[0].dtype), {len(flat): 0}
        out_spec = pl.BlockSpec((None, tm, tn), lambda i, j, k, o=out_off // tm: (j, i + o, 0))
        in_specs.append(pl.BlockSpec(memory_space=pl.ANY))
        flat.append(into[0])
    return pl.pallas_call(
        body, name=name, grid=(m // tm, n // tn, nk),
        in_specs=in_specs,
        out_specs=out_spec,
        out_shape=out_shape, input_output_aliases=aliases,
        scratch_shapes=[pltpu.VMEM((tm, tn), F32)],
        compiler_params=_params("parallel", "parallel", "arbitrary"),
    )(*flat)


def _norm_fwd(y, w, *, name, resid=None, scale=1.0, out_dtype=F32, col=0):
    t, d = y.shape[0], w.shape[1]
    tr = _pick(t, 256)
    assert col % d == 0

    def body(*refs):
        if resid is None:
            y_ref, w_ref, o_ref = refs
        else:
            y_ref, w_ref, r_ref, o_ref = refs
        yv = y_ref[...]
        out = yv * lax.rsqrt(jnp.mean(yv * yv, axis=-1, keepdims=True) + NORM_EPS) * w_ref[...]
        if resid is not None:
            out = r_ref[...] + scale * out
        o_ref[...] = out.astype(out_dtype)

    row = pl.BlockSpec((tr, d), lambda i: (i, 0))
    wspec = pl.BlockSpec((1, d), lambda i: (0, 0))
    ins, specs = [y, w], [pl.BlockSpec((tr, d), lambda i: (i, col // d)), wspec]
    if resid is not None:
        ins.append(resid)
        specs.append(row)
    return pl.pallas_call(
        body, name=name, grid=(t // tr,), in_specs=specs, out_specs=row,
        out_shape=jax.ShapeDtypeStruct((t, d), out_dtype), compiler_params=_params("parallel"),
    )(*ins)


def _norm_bwd(x, w, dy, *, name, scale=1.0, dres=None, col=0, dx_dtype=F32):
    t, d = x.shape[0], w.shape[1]
    tr = _pick(t, 256)
    assert col % d == 0

    def body(*refs):
        if dres is None:
            x_ref, w_ref, dy_ref, dx_ref, dw_ref = refs
        else:
            x_ref, w_ref, dy_ref, dr_ref, dx_ref, dw_ref = refs

        @pl.when(pl.program_id(0) == 0)
        def _():
            dw_ref[...] = jnp.zeros_like(dw_ref)

        xv = x_ref[...]
        r = lax.rsqrt(jnp.mean(xv * xv, axis=-1, keepdims=True) + NORM_EPS)
        xhat = xv * r
        dyv = dy_ref[...].astype(F32) * scale
        dw_ref[...] += jnp.sum(dyv * xhat, axis=0, keepdims=True)
        t_ = dyv * w_ref[...]
        dx = r * (t_ - xhat * jnp.mean(t_ * xhat, axis=-1, keepdims=True))
        if dres is not None:
            dx = dx + dr_ref[...]
        dx_ref[...] = dx.astype(dx_dtype)

    row = pl.BlockSpec((tr, d), lambda i: (i, 0))
    wspec = pl.BlockSpec((1, d), lambda i: (0, 0))
    ins, specs = [x, w, dy], [pl.BlockSpec((tr, d), lambda i: (i, col // d)), wspec, row]
    if dres is not None:
        ins.append(dres)
        specs.append(row)
    return pl.pallas_call(
        body, name=name, grid=(t // tr,), in_specs=specs, out_specs=(row, wspec),
        out_shape=(jax.ShapeDtypeStruct((t, d), dx_dtype), jax.ShapeDtypeStruct((1, d), F32)),
        compiler_params=_params("arbitrary"),
    )(*ins)


def _elementwise(fn, ins, out_dtypes, *, name, width=None, cols=None):
    t = ins[0].shape[0]
    d = ins[0].shape[1] if width is None else width
    cols = [0] * len(ins) if cols is None else cols
    tc = _pick(d, 2048)
    for c in cols:
        tc = _pick(d, tc, c)
    tr = _row_tile(t, tc * 4)
    nout = len(out_dtypes)

    def body(*refs):
        outs = fn(*[r[...].astype(F32) for r in refs[:len(ins)]])
        for o_ref, o in zip(refs[len(ins):], outs):
            o_ref[...] = o.astype(o_ref.dtype)

    spec = pl.BlockSpec((tr, tc), lambda i, j: (i, j))
    in_specs = [pl.BlockSpec((tr, tc), lambda i, j, o=c // tc: (i, j + o)) for c in cols]
    return pl.pallas_call(
        body, name=name, grid=(t // tr, d // tc), in_specs=in_specs, out_specs=[spec] * nout,
        out_shape=[jax.ShapeDtypeStruct((t, d), dt) for dt in out_dtypes],
        compiler_params=_params("parallel", "parallel"),
    )(*ins)


def _swiglu_fwd(g, u, *, name):
    return _elementwise(lambda gv, uv: (_silu(gv) * uv,), [g, u], [BF16], name=name)[0]


def _swiglu_bwd(da, g, u, *, name):
    return _elementwise(lambda dav, gv, uv: (dav * uv * _dsilu(gv), dav * _silu(gv)), [da, g, u], [BF16, BF16], name=name)


def _merge_fwd(proj, col_a, col_b, ya, yb, *, name):
    return _elementwise(lambda a, b, p, q: (_sigmoid(a) * p + _sigmoid(b) * q,), [proj, proj, ya, yb], [BF16],
                        name=name, width=ya.shape[1], cols=[col_a, col_b, 0, 0])[0]


def _merge_bwd(dm, proj, col_a, col_b, ya, yb, *, name):
    def fn(dmv, a, b, p, q):
        sa, sb = _sigmoid(a), _sigmoid(b)
        return dmv * p * sa * (1.0 - sa), dmv * q * sb * (1.0 - sb), dmv * sa, dmv * sb

    return _elementwise(fn, [dm, proj, proj, ya, yb], [BF16, BF16, BF16, BF16], name=name, width=ya.shape[1],
                        cols=[0, col_a, col_b, 0, 0])


def _loss_head(xo, target, *, name):
    t, d = xo.shape
    tr = _pick(t, 256)

    def body(x_ref, t_ref, dx_ref, l_ref):
        @pl.when(pl.program_id(0) == 0)
        def _():
            l_ref[...] = jnp.zeros_like(l_ref)

        err = x_ref[...] - t_ref[...]
        dx_ref[...] = err * (1.0 / d)
        l_ref[...] += 0.5 * jnp.sum(jnp.mean(err * err, axis=-1, keepdims=True), axis=0, keepdims=True)

    row = pl.BlockSpec((tr, d), lambda i: (i, 0))
    dx, l = pl.pallas_call(
        body, name=name, grid=(t // tr,), in_specs=[row, row],
        out_specs=(row, pl.BlockSpec((1, 1), lambda i: (0, 0))),
        out_shape=(jax.ShapeDtypeStruct((t, d), F32), jax.ShapeDtypeStruct((1, 1), F32)),
        compiler_params=_params("arbitrary"),
    )(xo, target)
    return dx, l[0, 0]


def _rope(xin, tabs, *, name, group, backward, out_dtype, col=0, ngroup=None):
    t = xin.shape[0]
    ngroup = xin.shape[1] // group if ngroup is None else ngroup
    wdt = ngroup * group
    tr = _pick(t, 256)
    assert col % group == 0
    cos_t, nsin_t, sin_t = tabs

    def body(x_ref, c_ref, n_ref, s_ref, o_ref):
        xv = x_ref[...].astype(F32)
        rot = xv[:, group - LANE:]
        if backward:
            out = rot * c_ref[...] + pltpu.roll(rot * n_ref[...], 32, 1) + pltpu.roll(rot * s_ref[...], LANE - 32, 1)
        else:
            out = rot * c_ref[...] + pltpu.roll(rot, LANE - 32, 1) * n_ref[...] + pltpu.roll(rot, 32, 1) * s_ref[...]
        if group > LANE:
            out = jnp.concatenate([xv[:, :group - LANE], out], axis=1)
        o_ref[...] = out.astype(out_dtype)

    xspec = pl.BlockSpec((tr, group), lambda i, g: (i, g))
    tspec = pl.BlockSpec((tr, LANE), lambda i, g: (i, 0))
    return pl.pallas_call(
        body, name=name, grid=(t // tr, ngroup),
        in_specs=[pl.BlockSpec((tr, group), lambda i, g: (i, g + col // group)), tspec, tspec, tspec], out_specs=xspec,
        out_shape=jax.ShapeDtypeStruct((t, wdt), out_dtype), compiler_params=_params("parallel", "parallel"),
    )(xin, cos_t, nsin_t, sin_t)


def _scores(q, kv, kr, qi, tq, scale):
    kcat = jnp.concatenate([kv[:, :HEAD], kr], axis=1)
    s = lax.dot_general(q, kcat, (((1,), (1,)), ((), ())), preferred_element_type=F32) * scale
    row = qi * tq + lax.broadcasted_iota(jnp.int32, s.shape, 0)
    col = lax.broadcasted_iota(jnp.int32, s.shape, 1)
    s = jnp.where(col <= row, s, -jnp.inf)
    p = jnp.exp(s - jnp.max(s, axis=-1, keepdims=True))
    return p / jnp.sum(p, axis=-1, keepdims=True), kcat


def _attn_fwd(qcat, kv, kr, *, name, scale):
    t = qcat.shape[0]
    nh = qcat.shape[1] // QGROUP
    tq = _pick(t, 256)

    def body(q_ref, kv_ref, kr_ref, o_ref):
        kvv = kv_ref[...]
        p, _ = _scores(q_ref[...], kvv, kr_ref[...], pl.program_id(1), tq, scale)
        o_ref[...] = jnp.dot(p.astype(BF16), kvv[:, HEAD:], preferred_element_type=F32).astype(BF16)

    return pl.pallas_call(
        body, name=name, grid=(nh, t // tq),
        in_specs=[pl.BlockSpec((tq, QGROUP), lambda h, i: (i, h)), pl.BlockSpec((t, QGROUP), lambda h, i: (0, h)),
                  pl.BlockSpec((t, LANE), lambda h, i: (0, 0))],
        out_specs=pl.BlockSpec((tq, HEAD), lambda h, i: (i, h)),
        out_shape=jax.ShapeDtypeStruct((t, nh * HEAD), BF16), compiler_params=_params("parallel", "parallel"),
    )(qcat, kv, kr)


def _attn_bwd(qcat, kv, kr, do, *, name, scale):
    t = qcat.shape[0]
    nh = qcat.shape[1] // QGROUP
    tq = _pick(t, 256)
    nq = t // tq

    def body(q_ref, kv_ref, kr_ref, do_ref, dq_ref, dkv_ref, dkr_ref, dk_acc, dv_acc):
        h, i = pl.program_id(0), pl.program_id(1)

        @pl.when(i == 0)
        def _():
            dk_acc[...] = jnp.zeros_like(dk_acc)
            dv_acc[...] = jnp.zeros_like(dv_acc)

        @pl.when((i == 0) & (h == 0))
        def _():
            dkr_ref[...] = jnp.zeros_like(dkr_ref)

        q = q_ref[...]
        kvv = kv_ref[...]
        dov = do_ref[...].astype(BF16)
        p, kcat = _scores(q, kvv, kr_ref[...], i, tq, scale)
        dp = lax.dot_general(dov, kvv[:, HEAD:], (((1,), (1,)), ((), ())), preferred_element_type=F32)
        ds = (p * (dp - jnp.sum(p * dp, axis=-1, keepdims=True)) * scale).astype(BF16)
        dq_ref[...] = jnp.dot(ds, kcat, preferred_element_type=F32)
        dk_acc[...] += lax.dot_general(ds, q, (((0,), (0,)), ((), ())), preferred_element_type=F32)
        dv_acc[...] += lax.dot_general(p.astype(BF16), dov, (((0,), (0,)), ((), ())), preferred_element_type=F32)

        @pl.when(i == nq - 1)
        def _():
            dk = dk_acc[...]
            dkv_ref[...] = jnp.concatenate([dk[:, :HEAD], dv_acc[...]], axis=1)
            dkr_ref[...] += dk[:, HEAD:]

    return pl.pallas_call(
        body, name=name, grid=(nh, nq),
        in_specs=[pl.BlockSpec((tq, QGROUP), lambda h, i: (i, h)), pl.BlockSpec((t, QGROUP), lambda h, i: (0, h)),
                  pl.BlockSpec((t, LANE), lambda h, i: (0, 0)), pl.BlockSpec((tq, HEAD), lambda h, i: (i, h))],
        out_specs=(pl.BlockSpec((tq, QGROUP), lambda h, i: (i, h)), pl.BlockSpec((t, QGROUP), lambda h, i: (0, h)),
                   pl.BlockSpec((t, LANE), lambda h, i: (0, 0))),
        out_shape=(jax.ShapeDtypeStruct((t, nh * QGROUP), F32), jax.ShapeDtypeStruct((t, nh * QGROUP), F32),
                   jax.ShapeDtypeStruct((t, LANE), F32)),
        scratch_shapes=[pltpu.VMEM((t, QGROUP), F32), pltpu.VMEM((t, HEAD), F32)],
        compiler_params=_params("arbitrary", "arbitrary"),
    )(qcat, kv, kr, do)


def _split3(x):
    hi = x.astype(BF16)
    r1 = x - hi.astype(F32)
    mid = r1.astype(BF16)
    lo = (r1 - mid.astype(F32)).astype(BF16)
    return hi, mid, lo


def _tri_matmul(mask, x):
    m = mask.astype(BF16)
    return sum(jnp.dot(m, part, preferred_element_type=F32) for part in _split3(x))


def _sub_cumsum(g, tb):
    row = lax.broadcasted_iota(jnp.int32, (tb, tb), 0)
    col = lax.broadcasted_iota(jnp.int32, (tb, tb), 1)
    return _tri_matmul(jnp.where((col <= row) & (col // SUB == row // SUB), 1.0, 0.0), g)


def _sub_suffix_prefix(after, before, tb):
    row = lax.broadcasted_iota(jnp.int32, (tb, tb), 0)
    col = lax.broadcasted_iota(jnp.int32, (tb, tb), 1)
    same = col // SUB == row // SUB
    return (_tri_matmul(jnp.where((col >= row) & same, 1.0, 0.0), after)
            + _tri_matmul(jnp.where((col < row) & same, 1.0, 0.0), before))


def _lower_bound(logits):
    mx = jnp.max(logits, axis=0, keepdims=True)
    e = jnp.exp(logits - mx)
    return e[0:1, :] / jnp.sum(e, axis=0, keepdims=True)


def _hgrn_fwd(proj, cols, wdt, logits, out_norm, *, name):
    t = proj.shape[0]
    nh = wdt // HEAD
    tb = _pick(t, 128)
    ns = tb // SUB

    def body(hq_ref, hf_ref, hi_ref, hg_ref, lg_ref, w_ref, o_ref, yb_ref, st_ref, s_ref, q_s, k_s, b_s):
        @pl.when(pl.program_id(1) == 0)
        def _():
            s_ref[...] = jnp.zeros_like(s_ref)

        lb = _lower_bound(lg_ref[...])
        f = lb + (1.0 - lb) * _sigmoid(hf_ref[...])
        q_s[...] = _silu(hq_ref[...])
        k_s[...] = 1.0 - f
        b_s[...] = _sub_cumsum(jnp.log(f), tb)
        rowid = lax.broadcasted_iota(jnp.int32, (SUB, HEAD), 0)

        def sub(c, carry):
            rows = pl.ds(pl.multiple_of(c * SUB, SUB), SUB)
            qc, kc, bc, vc = q_s[rows, :], k_s[rows, :], b_s[rows, :], hi_ref[rows, :]
            st = s_ref[...]
            st_ref[0, c] = st
            bl = bc[SUB - 1:SUB, :]
            oc = lax.dot_general((qc * jnp.exp(bc)).astype(BF16), st.astype(BF16), (((1,), (1,)), ((), ())),
                                 preferred_element_type=F32)
            for s in range(SUB):
                e = jnp.where(rowid >= s, jnp.exp(bc - bc[s:s + 1, :]), 0.0)
                a = jnp.sum(qc * e * kc[s:s + 1, :], axis=1, keepdims=True)
                oc = oc + a * vc[s:s + 1, :]
            o_ref[rows, :] = oc
            kd = kc * jnp.exp(bl - bc)
            s_ref[...] = jnp.exp(bl) * st + lax.dot_general(vc.astype(BF16), kd.astype(BF16), (((0,), (0,)), ((), ())),
                                                             preferred_element_type=F32)
            return carry

        lax.fori_loop(0, ns, sub, 0)
        o = o_ref[...]
        r = lax.rsqrt(jnp.mean(o * o, axis=-1, keepdims=True) + NORM_EPS)
        yb_ref[...] = (o * r * w_ref[...] * _silu(hg_ref[...])).astype(BF16)

    blk = pl.BlockSpec((tb, HEAD), lambda h, j: (j, h))
    return pl.pallas_call(
        body, name=name, grid=(nh, t // tb),
        in_specs=[pl.BlockSpec((tb, HEAD), lambda h, j, o=c // HEAD: (j, h + o)) for c in cols]
        + [pl.BlockSpec((2, HEAD), lambda h, j: (0, h)), pl.BlockSpec((1, HEAD), lambda h, j: (0, 0))],
        out_specs=(blk, blk, pl.BlockSpec((1, ns, HEAD, HEAD), lambda h, j: (h, j, 0, 0))),
        out_shape=(jax.ShapeDtypeStruct((t, wdt), F32), jax.ShapeDtypeStruct((t, wdt), BF16),
                   jax.ShapeDtypeStruct((nh, t // SUB, HEAD, HEAD), F32)),
        scratch_shapes=[pltpu.VMEM((HEAD, HEAD), F32)] + [pltpu.VMEM((tb, HEAD), F32)] * 3,
        compiler_params=_params("parallel", "arbitrary"),
    )(proj, proj, proj, proj, logits, out_norm)


def _hgrn_bwd(proj, cols, wdt, o_raw, dyb, states, logits, out_norm, *, name):
    t = proj.shape[0]
    nh = wdt // HEAD
    tb = _pick(t, 128)
    ns = tb // SUB
    nb = t // tb

    def body(hq_ref, hf_ref, hi_ref, hg_ref, o_ref, dy_ref, st_ref, lg_ref, w_ref,
             dhq_ref, dhf_ref, dhi_ref, dhg_ref, dlb_ref, dw_ref,
             ds_ref, q_s, k_s, b_s, do_s, dq_s, dk_s, dv_s, after_s, before_s, thru_s):
        @pl.when(pl.program_id(1) == 0)
        def _():
            ds_ref[...] = jnp.zeros_like(ds_ref)
            dlb_ref[...] = jnp.zeros_like(dlb_ref)
            dw_ref[...] = jnp.zeros_like(dw_ref)

        lb = _lower_bound(lg_ref[...])
        hqv, hgv = hq_ref[...], hg_ref[...]
        sig = _sigmoid(hf_ref[...])
        f = lb + (1.0 - lb) * sig
        q_s[...] = _silu(hqv)
        k_s[...] = 1.0 - f
        b_s[...] = _sub_cumsum(jnp.log(f), tb)

        o = o_ref[...]
        r = lax.rsqrt(jnp.mean(o * o, axis=-1, keepdims=True) + NORM_EPS)
        nrm = o * r
        w = w_ref[...]
        dy = dy_ref[...].astype(F32)
        dhg_ref[...] = (dy * nrm * w * _dsilu(hgv)).astype(BF16)
        dnw = dy * _silu(hgv)
        dw_ref[0] += jnp.sum(dnw * nrm, axis=0, keepdims=True)
        tt = dnw * w
        do_s[...] = r * (tt - nrm * jnp.mean(tt * nrm, axis=-1, keepdims=True))
        rowid = lax.broadcasted_iota(jnp.int32, (SUB, HEAD), 0)

        def sub(cc, carry):
            c = ns - 1 - cc
            rows = pl.ds(pl.multiple_of(c * SUB, SUB), SUB)
            qc, kc, bc, vc, doc = q_s[rows, :], k_s[rows, :], b_s[rows, :], hi_ref[rows, :], do_s[rows, :]
            st = st_ref[0, c]
            dst = ds_ref[...]
            bl = bc[SUB - 1:SUB, :]
            eb = jnp.exp(bc)
            ekd = jnp.exp(bl - bc)
            qe, kd = qc * eb, kc * ekd
            dob, vcb = doc.astype(BF16), vc.astype(BF16)
            dq_st = jnp.dot(dob, st.astype(BF16), preferred_element_type=F32) * eb
            dk_st = jnp.dot(vcb, dst.astype(BF16), preferred_element_type=F32) * ekd
            dv = lax.dot_general(kd.astype(BF16), dst.astype(BF16), (((1,), (1,)), ((), ())), preferred_element_type=F32)
            dq_in = jnp.zeros_like(qc)
            dk_in = jnp.zeros_like(qc)
            for s in range(SUB):
                e = jnp.where(rowid >= s, jnp.exp(bc - bc[s:s + 1, :]), 0.0)
                ek = e * kc[s:s + 1, :]
                a = jnp.sum(qc * ek, axis=1, keepdims=True)
                da = jnp.sum(doc * vc[s:s + 1, :], axis=1, keepdims=True)
                dq_in = dq_in + da * ek
                dk_in = dk_in + jnp.where(rowid == s, jnp.sum(da * e * qc, axis=0, keepdims=True), 0.0)
                dv = dv + jnp.where(rowid == s, jnp.sum(a * doc, axis=0, keepdims=True), 0.0)
            ebl = jnp.exp(bl)
            ds_ref[...] = ebl * dst + lax.dot_general(dob, qe.astype(BF16), (((0,), (0,)), ((), ())),
                                                      preferred_element_type=F32)
            dq_s[rows, :] = dq_st + dq_in
            dk_s[rows, :] = dk_st + dk_in
            dv_s[rows, :] = dv
            after_s[rows, :] = qc * (dq_st + dq_in) - kc * dk_in
            before_s[rows, :] = kc * dk_st
            thru_s[rows, :] = jnp.broadcast_to(ebl * jnp.sum(st * dst, axis=0, keepdims=True), (SUB, HEAD))
            return carry

        lax.fori_loop(0, ns, sub, 0)
        dg = _sub_suffix_prefix(after_s[...], before_s[...], tb) + thru_s[...]
        dhq_ref[...] = (dq_s[...] * _dsilu(hqv)).astype(BF16)
        dft = dg / f - dk_s[...]
        dhf_ref[...] = (dft * (1.0 - lb) * sig * (1.0 - sig)).astype(BF16)
        dlb_ref[0] += jnp.sum(dft * (1.0 - sig), axis=0, keepdims=True)
        dhi_ref[...] = dv_s[...].astype(BF16)

    blk = pl.BlockSpec((tb, HEAD), lambda h, j: (nb - 1 - j, h))
    vec = pl.BlockSpec((1, 1, HEAD), lambda h, j: (h, 0, 0))
    tok = jax.ShapeDtypeStruct((t, wdt), BF16)
    per_head = jax.ShapeDtypeStruct((nh, 1, HEAD), F32)
    return pl.pallas_call(
        body, name=name, grid=(nh, nb),
        in_specs=[pl.BlockSpec((tb, HEAD), lambda h, j, o=c // HEAD: (nb - 1 - j, h + o)) for c in cols]
        + [blk, blk] + [pl.BlockSpec((1, ns, HEAD, HEAD), lambda h, j: (h, nb - 1 - j, 0, 0)),
                              pl.BlockSpec((2, HEAD), lambda h, j: (0, h)), pl.BlockSpec((1, HEAD), lambda h, j: (0, 0))],
        out_specs=(blk, blk, blk, blk, vec, vec),
        out_shape=(tok, tok, tok, tok, per_head, per_head),
        scratch_shapes=[pltpu.VMEM((HEAD, HEAD), F32)] + [pltpu.VMEM((tb, HEAD), F32)] * 10,
        compiler_params=_params("arbitrary", "arbitrary"),
    )(proj, proj, proj, proj, o_raw, dyb, states, logits, out_norm)


def _lb_logits_grad(logits, dlb, *, name):
    def body(lg_ref, d_ref, o_ref):
        lg = lg_ref[...]
        e = jnp.exp(lg - jnp.max(lg, axis=0, keepdims=True))
        p = e / jnp.sum(e, axis=0, keepdims=True)
        d = d_ref[...]
        rowid = lax.broadcasted_iota(jnp.int32, lg.shape, 0)
        dp = jnp.where(rowid == 0, d, 0.0)
        o_ref[...] = p * (dp - jnp.sum(p * dp, axis=0, keepdims=True))

    return pl.pallas_call(body, name=name, out_shape=jax.ShapeDtypeStruct(logits.shape, F32))(logits, dlb)


def _adamw(w, g, m, v, *, name, deps=()):
    r, c = w.shape
    tc = _pick(c, 2048) if c % LANE == 0 else c
    tr = _row_tile(r, tc * 4)

    def body(w_ref, g_ref, m_ref, v_ref, *rest):
        d_ref, nm_ref, nv_ref = rest[-3:]
        gv = g_ref[...]
        nm = ADAM_B1 * m_ref[...] + (1.0 - ADAM_B1) * gv
        nv = ADAM_B2 * v_ref[...] + (1.0 - ADAM_B2) * (gv * gv)
        m_hat = nm / (1.0 - ADAM_B1 ** ADAM_STEP)
        v_hat = nv / (1.0 - ADAM_B2 ** ADAM_STEP)
        d_ref[...] = -ADAM_LR * (m_hat / (jnp.sqrt(v_hat) + ADAM_EPS) + ADAM_WD * w_ref[...])
        nm_ref[...] = nm
        nv_ref[...] = nv

    spec = pl.BlockSpec((tr, tc), lambda i, j: (i, j))
    shp = jax.ShapeDtypeStruct((r, c), F32)
    return pl.pallas_call(
        body, name=name, grid=(r // tr, c // tc), in_specs=[spec] * 4 + [ANY] * len(deps), out_specs=[spec] * 3,
        out_shape=[shp, shp, shp], compiler_params=_params("parallel", "parallel"),
    )(w, g, m, v, *deps)


def _coords():
    return lax.axis_index("x"), lax.axis_index("y"), lax.axis_index("c")


def _other_chips(x, y):
    return [(1 - x, y), (x, 1 - y), (1 - x, 1 - y)]


ANY = pl.BlockSpec(memory_space=pl.ANY)


class _Layout:
    def __init__(self, d, dff, in_cols, q_lora, kv_lora, nh):
        assert q_lora == kv_lora and nh % 4 == 0 and dff % (4 * LANE) == 0 and in_cols % 4 == 0 and d % 4 == 0
        self.d, self.dff, self.q_lora, self.nh = d, dff, q_lora, nh
        self.head = q_lora + kv_lora + ROPE
        self.pad = d - self.head
        self.nff, self.ncol, self.r_o, self.hps = dff // 4, in_cols // 4, d // 4, nh // 4
        assert self.head <= self.ncol
        self.off_q, self.off_kv, self.rows_narrow = 0, nh * QGROUP, 2 * nh * QGROUP


HBM = pl.BlockSpec(memory_space=pltpu.HBM)
SEMS = pl.BlockSpec(memory_space=pltpu.SEMAPHORE)
SPLIT = dict(has_side_effects=pltpu.SideEffectType.DATAFLOW_SIDE_EFFECTING)


def _in_hbm(a):
    return pltpu.with_memory_space_constraint(a, pltpu.HBM)


def _shard_rows(jobs, k):
    out, lrow = [], [0, 0]
    for job in jobs:
        for row, rows in job.pieces(k):
            out.append((job.a, lrow[job.a], row, rows))
            lrow[job.a] += rows
    return out


def _shard_total(jobs, a):
    return sum(rows for b, _, _, rows in _shard_rows(jobs, 0) if b == a)


def _gather_start(packs, lands, jobs, *, name, deps=()):
    n = len(packs)

    def body(*refs):
        p_refs, l_refs, send, recv, token = refs[:n], refs[n:2 * n], refs[-2 * n - 3], refs[-2 * n - 2], refs[-1]
        x, y, c = _coords()
        for a, lrow, row, rows in _shard_rows(jobs, 2 * x + y):
            pltpu.make_async_remote_copy(
                src_ref=p_refs[a].at[:, pl.ds(lrow, rows)], dst_ref=l_refs[a].at[:, pl.ds(row, rows)],
                send_sem=send.at[4 * a + 3], recv_sem=recv.at[4 * a + 3], device_id=(x, y, 1 - c), device_id_type=MESH).start()
            for j, (px, py) in enumerate(_other_chips(x, y)):
                pltpu.make_async_remote_copy(
                    src_ref=p_refs[a].at[c, pl.ds(lrow, rows)], dst_ref=l_refs[a].at[c, pl.ds(row, rows)],
                    send_sem=send.at[4 * a + j], recv_sem=recv.at[4 * a + j], device_id=(px, py, c), device_id_type=MESH).start()
        token[...] = jnp.zeros_like(token)

    thru = [pltpu.HBM(a.shape, a.dtype) for a in packs + lands]
    out = pl.pallas_call(
        body, name=name, in_specs=[HBM] * (2 * n) + [ANY] * len(deps),
        out_shape=(pltpu.SemaphoreType.DMA((4 * n,)), pltpu.SemaphoreType.DMA((4 * n,)), *thru, jax.ShapeDtypeStruct((8, LANE), F32)),
        out_specs=(SEMS, SEMS, *[HBM] * (2 * n), pl.BlockSpec(memory_space=pltpu.VMEM)),
        input_output_aliases={i: 2 + i for i in range(2 * n)}, compiler_params=pltpu.CompilerParams(**SPLIT),
    )(*[_in_hbm(a) for a in packs + lands], *deps)
    return dict(send=out[0], recv=out[1], bufs=list(out[2:2 + 2 * n]), n=n, jobs=jobs), out[-1]


def _gather_wait(handle, after, *, name):
    n, jobs = handle["n"], handle["jobs"]

    def body(*refs):
        l_refs, send, recv = refs[n:2 * n], refs[2 * n], refs[2 * n + 1]
        x, y, c = _coords()
        for a in range(n):
            total = _shard_total(jobs, a)
            for j, like in enumerate([l_refs[a].at[0, pl.ds(0, total)]] * 3 + [l_refs[a].at[:, pl.ds(0, total)]]):
                cp = pltpu.make_async_remote_copy(src_ref=like, dst_ref=like, send_sem=send.at[4 * a + j],
                                                  recv_sem=recv.at[4 * a + j], device_id=(x, y, c), device_id_type=MESH)
                cp.wait_send()
                cp.wait_recv()

    out = pl.pallas_call(
        body, name=name, in_specs=[HBM] * (2 * n) + [SEMS, SEMS] + [ANY] * len(after),
        out_shape=[pltpu.HBM(a.shape, a.dtype) for a in handle["bufs"]], out_specs=[HBM] * (2 * n),
        input_output_aliases={i: i for i in range(2 * n)}, compiler_params=pltpu.CompilerParams(**SPLIT),
    )(*handle["bufs"], handle["send"], handle["recv"], *after)
    return list(out[n:])


def _gather_forward(lands, jobs, *, name):
    n = len(lands)

    def body(*refs):
        l_refs, send, recv = refs[n:2 * n], refs[2 * n], refs[2 * n + 1]
        x, y, c = _coords()
        for j, (px, py) in enumerate(_other_chips(x, y)):
            for a, _, row, rows in _shard_rows(jobs, 2 * px + py):
                blk = l_refs[a].at[c, pl.ds(row, rows)]
                pltpu.make_async_remote_copy(src_ref=blk, dst_ref=blk, send_sem=send.at[3 * a + j], recv_sem=recv.at[3 * a + j],
                                             device_id=(x, y, 1 - c), device_id_type=MESH).start()
        for a in range(n):
            like = l_refs[a].at[0, pl.ds(0, _shard_total(jobs, a))]
            for j in range(3):
                cp = pltpu.make_async_remote_copy(src_ref=like, dst_ref=like, send_sem=send.at[3 * a + j],
                                                  recv_sem=recv.at[3 * a + j], device_id=(x, y, c), device_id_type=MESH)
                cp.wait_send()
                cp.wait_recv()

    sem = pltpu.SemaphoreType.DMA((3 * n,))
    return pl.pallas_call(
        body, name=name, in_specs=[ANY] * n, out_specs=[ANY] * n, input_output_aliases={i: i for i in range(n)},
        out_shape=[jax.ShapeDtypeStruct(a.shape, a.dtype) for a in lands], scratch_shapes=[sem, sem],
    )(*lands)


def _add_sibling(g, recv, sel, *, name):
    rows, hw = recv.shape
    tr = _row_tile(rows, hw * 4)

    def body(sel_ref, g_ref, r_ref, o_ref):
        o_ref[...] = (g_ref[...] + r_ref[...]).astype(BF16)

    return pl.pallas_call(
        body, name=name, out_shape=jax.ShapeDtypeStruct((rows, hw), BF16),
        grid_spec=pltpu.PrefetchScalarGridSpec(
            num_scalar_prefetch=1, grid=(rows // tr,),
            in_specs=[pl.BlockSpec((None, tr, hw), lambda i, s: (s[0], i, 0)), pl.BlockSpec((tr, hw), lambda i, s: (i, 0))],
            out_specs=pl.BlockSpec((tr, hw), lambda i, s: (i, 0))),
        compiler_params=_params("parallel"),
    )(sel, g, recv)


class _Job:
    def __init__(self, a, blk, n_outer, n_inner, stride, start):
        self.a, self.blk, self.n_outer, self.n_inner, self.stride, self.start = a, blk, n_outer, n_inner, stride, start
        self.rows_out = n_outer * n_inner * blk

    def pieces(self, k):
        return [(self.start(k) + o * self.stride * self.blk, self.n_inner * self.blk) for o in range(self.n_outer)]


def _block_rows(rows, cap, *also):
    best = None
    for b in range(16, min(rows, cap) + 1, 16):
        if rows % b == 0 and all(v % b == 0 for v in also):
            best = b
    assert best is not None, (rows, also)
    return best


def _ffn_jobs(lay):
    b = _block_rows(lay.nff, 704, lay.dff)
    return [_Job(0, b, 3, lay.nff // b, lay.dff // b, lambda k: lay.nff * k)]


def _mix_jobs(lay):
    d, ncol, head, pad = lay.d, lay.ncol, lay.head, lay.pad
    first = lambda k, a, b: jnp.where(k == 0, a, b) if not isinstance(k, int) else (a if k == 0 else b)
    ba = _block_rows(head, 704, *[ncol * k + pad for k in (1, 2, 3)])
    bb = _block_rows(ncol - head, 704, *[ncol * k + d for k in (0, 1, 2, 3)])
    bo = _block_rows(lay.r_o, 704, d)
    bq = _block_rows(HEAD + ROPE, 704, QGROUP)
    bk = _block_rows(lay.hps * QGROUP, 704, lay.off_kv)
    return [_Job(0, ba, 1, head // ba, 0, lambda k: first(k, 0, ncol * k + pad)),
            _Job(0, bb, 1, (ncol - head) // bb, 0, lambda k: ncol * k + d),
            _Job(0, bo, 3, lay.r_o // bo, d // bo, lambda k: 7 * d + lay.r_o * k),
            _Job(1, bq, lay.hps, (HEAD + ROPE) // bq, QGROUP // bq, lambda k: QGROUP * lay.hps * k),
            _Job(1, bk, 1, lay.hps * QGROUP // bk, 0, lambda k: lay.off_kv + lay.hps * QGROUP * k)]


def _swap_start(gs, *, name):
    n = len(gs)
    lands = [lax.empty(g.shape[1:], g.dtype) for g in gs]

    def body(*refs):
        g_refs, land_refs, send, recv, token = refs[:n], refs[n:2 * n], refs[2 * n], refs[2 * n + 1], refs[-1]
        x, y, c = _coords()
        for a in range(n):
            pltpu.make_async_remote_copy(src_ref=g_refs[a].at[1 - c], dst_ref=land_refs[a], send_sem=send.at[a],
                                         recv_sem=recv.at[a], device_id=(x, y, 1 - c), device_id_type=MESH).start()
        token[...] = jnp.zeros_like(token)

    thru = [pltpu.HBM(a.shape, a.dtype) for a in gs + lands]
    out = pl.pallas_call(
        body, name=name, in_specs=[HBM] * (2 * n),
        out_shape=(pltpu.SemaphoreType.DMA((n,)), pltpu.SemaphoreType.DMA((n,)), *thru, jax.ShapeDtypeStruct((8, LANE), F32)),
        out_specs=(SEMS, SEMS, *[HBM] * (2 * n), pl.BlockSpec(memory_space=pltpu.VMEM)),
        input_output_aliases={i: 2 + i for i in range(2 * n)}, compiler_params=pltpu.CompilerParams(**SPLIT),
    )(*[_in_hbm(a) for a in gs + lands])
    return dict(send=out[0], recv=out[1], bufs=list(out[2:2 + 2 * n]), n=n), out[-1]


def _swap_wait(handle, after, *, name):
    n = handle["n"]

    def body(*refs):
        g_refs, land_refs, send, recv = refs[:n], refs[n:2 * n], refs[2 * n], refs[2 * n + 1]
        x, y, c = _coords()
        for a in range(n):
            cp = pltpu.make_async_remote_copy(src_ref=g_refs[a].at[1 - c], dst_ref=land_refs[a], send_sem=send.at[a],
                                              recv_sem=recv.at[a], device_id=(x, y, 1 - c), device_id_type=MESH)
            cp.wait_send()
            cp.wait_recv()

    out = pl.pallas_call(
        body, name=name, in_specs=[HBM] * (2 * n) + [SEMS, SEMS] + [ANY] * len(after),
        out_shape=[pltpu.HBM(a.shape, a.dtype) for a in handle["bufs"]], out_specs=[HBM] * (2 * n),
        input_output_aliases={i: i for i in range(2 * n)}, compiler_params=pltpu.CompilerParams(**SPLIT),
    )(*handle["bufs"], handle["send"], handle["recv"], *after)
    return list(out[:n]), list(out[n:])


def _exchange_start(ss, jobs, *, name):
    n = len(ss)
    lands = [lax.empty((3,) + s.shape, s.dtype) for s in ss]

    def body(*refs):
        s_refs, land_refs, send, recv, token = refs[:n], refs[n:2 * n], refs[2 * n], refs[2 * n + 1], refs[-1]
        x, y, c = _coords()
        for j, (px, py) in enumerate(_other_chips(x, y)):
            for job in jobs:
                for row, rows in job.pieces(2 * px + py):
                    pltpu.make_async_remote_copy(
                        src_ref=s_refs[job.a].at[pl.ds(row, rows)], dst_ref=land_refs[job.a].at[j, pl.ds(row, rows)],
                        send_sem=send.at[n * j + job.a], recv_sem=recv.at[n * j + job.a], device_id=(px, py, c),
                        device_id_type=MESH).start()
        token[...] = jnp.zeros_like(token)

    thru = [pltpu.HBM(a.shape, a.dtype) for a in ss + lands]
    out = pl.pallas_call(
        body, name=name, in_specs=[HBM] * (2 * n),
        out_shape=(pltpu.SemaphoreType.DMA((3 * n,)), pltpu.SemaphoreType.DMA((3 * n,)), *thru, jax.ShapeDtypeStruct((8, LANE), F32)),
        out_specs=(SEMS, SEMS, *[HBM] * (2 * n), pl.BlockSpec(memory_space=pltpu.VMEM)),
        input_output_aliases={i: 2 + i for i in range(2 * n)}, compiler_params=pltpu.CompilerParams(**SPLIT),
    )(*[_in_hbm(a) for a in ss + lands])
    return dict(send=out[0], recv=out[1], bufs=list(out[2:2 + 2 * n]), n=n, jobs=jobs), out[-1]


def _exchange_wait(handle, after, *, name):
    n, jobs = handle["n"], handle["jobs"]
    total = [sum(rows for job in jobs if job.a == a for _, rows in job.pieces(0)) for a in range(n)]

    def body(*refs):
        s_refs, land_refs, send, recv = refs[:n], refs[n:2 * n], refs[2 * n], refs[2 * n + 1]
        x, y, c = _coords()
        for a in range(n):
            for j in range(3):
                all_rows = land_refs[a].at[0, pl.ds(0, total[a])]
                cp = pltpu.make_async_remote_copy(src_ref=all_rows, dst_ref=all_rows, send_sem=send.at[n * j + a],
                                                  recv_sem=recv.at[n * j + a], device_id=(x, y, c), device_id_type=MESH)
                cp.wait_send()
                cp.wait_recv()

    out = pl.pallas_call(
        body, name=name, in_specs=[HBM] * (2 * n) + [SEMS, SEMS] + [ANY] * len(after),
        out_shape=[pltpu.HBM(a.shape, a.dtype) for a in handle["bufs"]], out_specs=[HBM] * (2 * n),
        input_output_aliases={i: i for i in range(2 * n)}, compiler_params=pltpu.CompilerParams(**SPLIT),
    )(*handle["bufs"], handle["send"], handle["recv"], *after)
    return list(out[:n]), list(out[n:])


def _add_shard(s, land, job, sel, k, *, name):
    hw = s.shape[1]
    blk, no, ni, stride = job.blk, job.n_outer, job.n_inner, job.stride
    scal = jnp.stack([sel, job.start(k) // blk]).astype(jnp.int32)

    def body(sc_ref, own_ref, r_ref, o_ref):
        o_ref[...] = ((own_ref[...].astype(F32) + r_ref[0].astype(F32)) + r_ref[1].astype(F32)) + r_ref[2].astype(F32)

    return pl.pallas_call(
        body, name=name, out_shape=jax.ShapeDtypeStruct((2, job.rows_out, hw), F32),
        grid_spec=pltpu.PrefetchScalarGridSpec(
            num_scalar_prefetch=1, grid=(no, ni),
            in_specs=[pl.BlockSpec((blk, hw), lambda o, b, sc: (sc[1] + o * stride + b, 0)),
                      pl.BlockSpec((3, blk, hw), lambda o, b, sc: (0, sc[1] + o * stride + b, 0))],
            out_specs=pl.BlockSpec((None, blk, hw), lambda o, b, sc: (sc[0], o * ni + b, 0))),
        compiler_params=_params("parallel", "parallel"),
    )(scal, s, land)


def _join_list(fs, *, name):
    n = len(fs)

    def body(*refs):
        f_refs, send_sems, recv_sems = refs[n:2 * n], refs[2 * n], refs[2 * n + 1]
        x, y, c = _coords()
        copies = [pltpu.make_async_remote_copy(
            src_ref=f.at[c], dst_ref=f.at[c], send_sem=send_sems.at[a], recv_sem=recv_sems.at[a],
            device_id=(x, y, 1 - c), device_id_type=MESH) for a, f in enumerate(f_refs)]
        for cp in copies:
            cp.start()
        for cp in copies:
            cp.wait()

    sem = pltpu.SemaphoreType.DMA((n,))
    return pl.pallas_call(
        body, name=name, in_specs=[ANY] * n, out_specs=[ANY] * n, input_output_aliases={i: i for i in range(n)},
        out_shape=[jax.ShapeDtypeStruct(f.shape, f.dtype) for f in fs], scratch_shapes=[sem, sem],
    )(*fs)


def _all_reduce_small(vec, *, name):
    n = vec.shape[1]

    def body(v_ref, o_ref, buf, send_sems, recv_sems):
        x, y, c = _coords()
        me = 4 * x + 2 * y + c
        buf[me] = v_ref[...]
        copies = []
        for m in range(1, 8):
            peer = (x ^ ((m >> 2) & 1), y ^ ((m >> 1) & 1), c ^ (m & 1))
            copies.append(pltpu.make_async_remote_copy(
                src_ref=v_ref, dst_ref=buf.at[me], send_sem=send_sems.at[m - 1], recv_sem=recv_sems.at[m - 1],
                device_id=peer, device_id_type=MESH))
        for cp in copies:
            cp.start()
        for cp in copies:
            cp.wait()
        acc = buf[0]
        for d in range(1, 8):
            acc = acc + buf[d]
        o_ref[...] = acc

    return pl.pallas_call(
        body, name=name, out_shape=jax.ShapeDtypeStruct((1, n), F32),
        in_specs=[pl.BlockSpec(memory_space=pltpu.VMEM)], out_specs=pl.BlockSpec(memory_space=pltpu.VMEM),
        scratch_shapes=[pltpu.VMEM((8, 1, n), F32), pltpu.SemaphoreType.DMA((7,)), pltpu.SemaphoreType.DMA((7,))],
    )(vec)


def _ffn_fwd(x, n_pre, n_post, wbuf, lay, tag, deps=()):
    wg, wu, wd = ((wbuf, i * lay.dff, lay.dff) for i in range(3))
    h = _norm_fwd(x, n_pre, name=f"{tag}_norm_pre", out_dtype=BF16)
    g = _mm([(h, wg)], name=f"{tag}_gate", mode="nt", deps=deps)
    u = _mm([(h, wu)], name=f"{tag}_up", mode="nt")
    a = _swiglu_fwd(g, u, name=f"{tag}_swiglu")
    yv = _mm([(a, wd)], name=f"{tag}_down", mode="nn")
    out = _norm_fwd(yv, n_post, name=f"{tag}_norm_post", resid=x, scale=MACARON_SCALE)
    return out, (x, h, g, u, a, yv)


def _ffn_bwd(dout, saved, n_pre, n_post, wbuf, lay, tag, deps=(), after_act=None, after_dw=None):
    x, h, g, u, a, yv = saved
    dff = lay.dff
    gbuf = lax.empty((2, 3 * dff, lay.d // 2), F32)
    dy, dn_post = _norm_bwd(yv, n_post, dout, name=f"{tag}_norm_post_bwd", scale=MACARON_SCALE)
    da = _mm([(dy, (wbuf, 2 * dff, dff))], name=f"{tag}_down_dx", mode="nt", deps=deps)
    dg, du = _swiglu_bwd(da, g, u, name=f"{tag}_swiglu_bwd")
    deps = after_act(du) if after_act is not None else ()
    gbuf = _mm([(a, dy)], name=f"{tag}_down_dw", mode="tn", into=(gbuf, 2 * dff), deps=deps)
    gbuf = _mm([(dg, h)], name=f"{tag}_gate_dw", mode="tn", into=(gbuf, 0))
    gbuf = _mm([(du, h)], name=f"{tag}_up_dw", mode="tn", into=(gbuf, dff))
    deps = after_dw(gbuf)
    dh = _mm([(dg, (wbuf, 0, dff)), (du, (wbuf, dff, dff))], name=f"{tag}_up_dx", mode="nn", deps=deps)
    dx, dn_pre = _norm_bwd(x, n_pre, dh, name=f"{tag}_norm_pre_bwd", dres=dout)
    return dx, dn_pre, dn_post


def _rope_tables(positions):
    half = ROPE // 2
    inv_freq = ROPE_THETA ** (-jnp.arange(half, dtype=F32) / half)
    ang = positions.astype(F32)[:, None] * inv_freq
    cos, sin = jnp.cos(ang), jnp.sin(ang)
    z = jnp.zeros_like(cos)
    z2 = jnp.zeros((positions.shape[0], LANE - ROPE), F32)
    return (jnp.concatenate([cos, cos, z2], axis=1), jnp.concatenate([-sin, z, z2], axis=1),
            jnp.concatenate([z, sin, z2], axis=1))


def kernel(x, positions, ffn1_norm_pre, ffn1_w_gate, ffn1_w_up, ffn1_w_down, ffn1_norm_post, mix_norm_pre, w_in, mla_q_norm, mla_w_q_up, mla_kv_norm, mla_w_kv_up, mla_w_o, hgrn_lb_logits, hgrn_out_norm, hgrn_w_o, w_out, mix_norm_post, ffn2_norm_pre, ffn2_w_gate, ffn2_w_up, ffn2_w_down, ffn2_norm_post, loss_target, m_ffn1_norm_pre, m_ffn1_w_gate, m_ffn1_w_up, m_ffn1_w_down, m_ffn1_norm_post, m_mix_norm_pre, m_w_in, m_mla_q_norm, m_mla_w_q_up, m_mla_kv_norm, m_mla_w_kv_up, m_mla_w_o, m_hgrn_lb_logits, m_hgrn_out_norm, m_hgrn_w_o, m_w_out, m_mix_norm_post, m_ffn2_norm_pre, m_ffn2_w_gate, m_ffn2_w_up, m_ffn2_w_down, m_ffn2_norm_post, v_ffn1_norm_pre, v_ffn1_w_gate, v_ffn1_w_up, v_ffn1_w_down, v_ffn1_norm_post, v_mix_norm_pre, v_w_in, v_mla_q_norm, v_mla_w_q_up, v_mla_kv_norm, v_mla_w_kv_up, v_mla_w_o, v_hgrn_lb_logits, v_hgrn_out_norm, v_hgrn_w_o, v_w_out, v_mix_norm_post, v_ffn2_norm_pre, v_ffn2_w_gate, v_ffn2_w_up, v_ffn2_w_down, v_ffn2_norm_post):
    given = dict(locals())
    wts = {n: given[n] for n in ALL_WEIGHTS}
    mom = {n: given["m_" + n] for n in ALL_WEIGHTS}
    var = {n: given["v_" + n] for n in ALL_WEIGHTS}
    xin = x[0]
    target = loss_target[0]
    t, d = xin.shape
    cx, cy, cc = _coords()

    q_lora, kv_lora = mla_q_norm.shape[1], mla_kv_norm.shape[1]
    nh_mla = 4 * mla_w_kv_up.shape[2] // QGROUP
    lay = _Layout(d, 4 * ffn1_w_gate.shape[2], 4 * w_in.shape[2], q_lora, kv_lora, nh_mla)
    jobs_ffn, jobs_mix = _ffn_jobs(lay), _mix_jobs(lay)
    col_sharded = lambda n: wts[n][0].T.astype(BF16)
    row_sharded = lambda n: wts[n][0].astype(BF16)

    def pack(parts):
        a = jnp.concatenate(parts)
        return a.reshape(a.shape[0], 2, a.shape[1] // 2).transpose(1, 0, 2)

    def ffn_pack(tag):
        return [pack([col_sharded(f"{tag}_w_gate"), col_sharded(f"{tag}_w_up"), row_sharded(f"{tag}_w_down")])]

    ffn_land = lambda: [lax.empty((2, 3 * lay.dff, d // 2), BF16)]
    got1, tok = _gather_start(ffn_pack("ffn1"), ffn_land(), jobs_ffn, name="gather_ffn1")
    (w_ffn1,) = _gather_forward(_gather_wait(got1, [], name="gather_ffn1_wait"), jobs_ffn, name="gather_ffn1_forward")
    got_m, tok = _gather_start(
        [pack([col_sharded("w_in"), row_sharded("mla_w_o"), row_sharded("hgrn_w_o"), row_sharded("w_out")]),
         pack([col_sharded("mla_w_q_up"), col_sharded("mla_w_kv_up")])],
        [jnp.zeros((2, 10 * d, d // 2), BF16), jnp.zeros((2, lay.rows_narrow, q_lora // 2), BF16)], jobs_mix,
        name="gather_mix", deps=[w_ffn1])
    col_kr = q_lora + kv_lora
    hgrn_cols = [d, 2 * d, 3 * d, 4 * d]
    col_ga, col_gb = 5 * d, 6 * d
    tabs = _rope_tables(positions[0])
    scale = (HEAD + ROPE) ** -0.5

    x1, saved1 = _ffn_fwd(xin, ffn1_norm_pre, ffn1_norm_post, w_ffn1, lay, "ffn1", deps=[tok])

    wide, narrow = _gather_forward(_gather_wait(got_m, [x1], name="gather_mix_wait"), jobs_mix, name="gather_mix_forward")
    got2, tok = _gather_start(ffn_pack("ffn2"), ffn_land(), jobs_ffn, name="gather_ffn2", deps=[wide])
    w_in_v = (wide, 0, 7 * d)
    w_o_v = {n: (wide, (7 + i) * d, d) for i, n in enumerate(("mla_w_o", "hgrn_w_o", "w_out"))}
    w_q_v = (narrow, lay.off_q, nh_mla * QGROUP)
    w_kv_v = (narrow, lay.off_kv, nh_mla * QGROUP)

    h2 = _norm_fwd(x1, mix_norm_pre, name="mix_norm_pre", out_dtype=BF16)
    proj = _mm([(h2, w_in_v)], name="mix_in", mode="nt", deps=[tok])
    cqn = _norm_fwd(proj, mla_q_norm, name="mla_q_norm", out_dtype=BF16, col=0)
    ckvn = _norm_fwd(proj, mla_kv_norm, name="mla_kv_norm", out_dtype=BF16, col=q_lora)
    qp = _mm([(cqn, w_q_v)], name="mla_q_up", mode="nt")
    kvb = _mm([(ckvn, w_kv_v)], name="mla_kv_up", mode="nt", out_dtype=BF16)
    qcat = _rope(qp, tabs, name="rope_q", group=QGROUP, backward=False, out_dtype=BF16)
    krot = _rope(proj, tabs, name="rope_k", group=LANE, backward=False, out_dtype=BF16, col=col_kr, ngroup=1)
    o_mla = _attn_fwd(qcat, kvb, krot, name="mla_attention", scale=scale)
    y_a = _mm([(o_mla, w_o_v["mla_w_o"])], name="mla_out", mode="nn")

    o_raw, yb, states = _hgrn_fwd(proj, hgrn_cols, d, hgrn_lb_logits, hgrn_out_norm, name="hgrn_scan")
    y_b = _mm([(yb, w_o_v["hgrn_w_o"])], name="hgrn_out", mode="nn")

    merged = _merge_fwd(proj, col_ga, col_gb, y_a, y_b, name="mix_merge")
    y_mix = _mm([(merged, w_o_v["w_out"])], name="mix_out", mode="nn")
    x2 = _norm_fwd(y_mix, mix_norm_post, name="mix_norm_post", resid=x1, scale=1.0)

    (w_ffn2,) = _gather_forward(_gather_wait(got2, [x2], name="gather_ffn2_wait"), jobs_ffn, name="gather_ffn2_forward")
    x3, saved2 = _ffn_fwd(x2, ffn2_norm_pre, ffn2_norm_post, w_ffn2, lay, "ffn2")
    dx3, loss_local = _loss_head(x3, target, name="loss_head")

    grads, deltas, new_m, new_v = {}, {}, {}, {}
    sel = cc.astype(jnp.int32)
    sel1 = jnp.reshape(sel, (1,))
    me_chip = (2 * cx + cy).astype(jnp.int32)

    def reduce_mid(handle, after, jobs, tag):
        bufs, recvd = _swap_wait(handle, after, name=f"grad_swap_{tag}_wait")
        sums = [_add_sibling(b, r, sel1, name=f"grad_add_sibling_{tag}_{i}") for i, (b, r) in enumerate(zip(bufs, recvd))]
        return _exchange_start(sums, jobs, name=f"grad_exchange_{tag}")

    def reduce_end(handle, after, tag):
        sums, lands = _exchange_wait(handle, after, name=f"grad_exchange_{tag}_wait")
        parts = [_add_shard(sums[job.a], lands[job.a], job, sel, me_chip, name=f"grad_add_chips_{tag}_{i}")
                 for i, job in enumerate(handle["jobs"])]
        return _join_list(parts, name=f"grad_join_{tag}")

    def natural(part, lo, rows, transposed):
        g_n = part[:, lo:lo + rows]
        hw_n = g_n.shape[2]
        return g_n.transpose(0, 2, 1).reshape(2 * hw_n, rows) if transposed else g_n.transpose(1, 0, 2).reshape(rows, 2 * hw_n)

    def adam(names, deps=()):
        for i, n in enumerate(names):
            shp = wts[n].shape
            two_d = (lambda a: a[0]) if n in BIG_WEIGHTS else (lambda a: a)
            dl, nm, nv = _adamw(two_d(wts[n]), grads[n], two_d(mom[n]), two_d(var[n]), name=f"adamw_{n}",
                                deps=deps if i == 0 else ())
            grads[n] = grads[n].reshape(shp)
            deltas[n], new_m[n], new_v[n] = dl.reshape(shp), nm.reshape(shp), nv.reshape(shp)
        return [deltas[n] for n in names]

    def ffn_grads(joined, tag, deps=()):
        nff = lay.nff
        grads[f"{tag}_w_gate"] = natural(joined[0], 0, nff, True)
        grads[f"{tag}_w_up"] = natural(joined[0], nff, nff, True)
        grads[f"{tag}_w_down"] = natural(joined[0], 2 * nff, nff, False)
        return adam([f"{tag}_w_gate", f"{tag}_w_up", f"{tag}_w_down"], deps)

    swaps = {}

    def start_swap(tag):
        def hook(gbuf):
            swaps[tag], started = _swap_start([gbuf], name=f"grad_swap_{tag}")
            return [started]
        return hook

    dx2, grads["ffn2_norm_pre"], grads["ffn2_norm_post"] = _ffn_bwd(
        dx3, saved2, ffn2_norm_pre, ffn2_norm_post, w_ffn2, lay, "ffn2", after_dw=start_swap("ffn2"))

    gwide = lax.empty((2, 10 * d, d // 2), F32)
    gnarrow = lax.empty((2, lay.rows_narrow, q_lora // 2), F32)
    dy_mix, grads["mix_norm_post"] = _norm_bwd(y_mix, mix_norm_post, dx2, name="mix_norm_post_bwd")
    dmerged = _mm([(dy_mix, w_o_v["w_out"])], name="mix_out_dx", mode="nt")
    gwide = _mm([(merged, dy_mix)], name="mix_out_dw", mode="tn", into=(gwide, 9 * d))
    dga, dgb, dy_a, dy_b = _merge_bwd(dmerged, proj, col_ga, col_gb, y_a, y_b, name="mix_merge_bwd")

    do_mla = _mm([(dy_a, w_o_v["mla_w_o"])], name="mla_out_dx", mode="nt")
    gwide = _mm([(o_mla, dy_a)], name="mla_out_dw", mode="tn", into=(gwide, 7 * d))
    dqcat, dkv, dkr = _attn_bwd(qcat, kvb, krot, do_mla, name="mla_attention_bwd", scale=scale)
    exch2, tok = reduce_mid(swaps["ffn2"], [dkr], _ffn_jobs(lay), "ffn2")

    dqp = _rope(dqcat, tabs, name="rope_q_bwd", group=QGROUP, backward=True, out_dtype=BF16)
    dk_r = _rope(dkr, tabs, name="rope_k_bwd", group=LANE, backward=True, out_dtype=BF16)
    dcqn = _mm([(dqp, w_q_v)], name="mla_q_up_dx", mode="nn", deps=[tok])
    gnarrow = _mm([(dqp, cqn)], name="mla_q_up_dw", mode="tn", into=(gnarrow, lay.off_q))
    dkvb = dkv.astype(BF16)
    dckvn = _mm([(dkvb, w_kv_v)], name="mla_kv_up_dx", mode="nn")
    gnarrow = _mm([(dkvb, ckvn)], name="mla_kv_up_dw", mode="tn", into=(gnarrow, lay.off_kv))
    dc_q, grads["mla_q_norm"] = _norm_bwd(proj, mla_q_norm, dcqn, name="mla_q_norm_bwd", col=0, dx_dtype=BF16)
    dc_kv, grads["mla_kv_norm"] = _norm_bwd(proj, mla_kv_norm, dckvn, name="mla_kv_norm_bwd", col=q_lora, dx_dtype=BF16)

    dyb = _mm([(dy_b, w_o_v["hgrn_w_o"])], name="hgrn_out_dx", mode="nt")
    gwide = _mm([(yb, dy_b)], name="hgrn_out_dw", mode="tn", into=(gwide, 8 * d))
    dhq, dhf, dhi, dhg, dlb_h, dnorm_h = _hgrn_bwd(proj, hgrn_cols, d, o_raw, dyb, states, hgrn_lb_logits, hgrn_out_norm,
                                                   name="hgrn_scan_bwd")

    dproj = jnp.concatenate([dc_q, dc_kv, dk_r, jnp.zeros((t, d - col_kr - LANE), BF16), dhq, dhf, dhi, dhg, dga, dgb], axis=1)
    dh2 = _mm([(dproj, w_in_v)], name="mix_in_dx", mode="nn")
    gwide = _mm([(dproj, h2)], name="mix_in_dw", mode="tn", into=(gwide, 0))
    dx1, grads["mix_norm_pre"] = _norm_bwd(x1, mix_norm_pre, dh2, name="mix_norm_pre_bwd", dres=dx2)
    swap_m, tok = _swap_start([gwide, gnarrow], name="grad_swap_mix")
    joined2 = reduce_end(exch2, [dx1], "ffn2")

    exchanges = {}

    def mix_exchange(after):
        exchanges["mix"], started = reduce_mid(swap_m, [after], _mix_jobs(lay), "mix")
        return [started]

    dx0, grads["ffn1_norm_pre"], grads["ffn1_norm_post"] = _ffn_bwd(
        dx1, saved1, ffn1_norm_pre, ffn1_norm_post, w_ffn1, lay, "ffn1", deps=[tok], after_act=mix_exchange,
        after_dw=start_swap("ffn1"))
    exch1, tok = reduce_mid(swaps["ffn1"], [dx0], _ffn_jobs(lay), "ffn1")

    joined_m = reduce_end(exchanges["mix"], [dx0, tok], "mix")
    done = ffn_grads(joined2, "ffn2")
    grads["w_in"] = natural(jnp.concatenate([joined_m[0], joined_m[1]], axis=1), 0, lay.ncol, True)
    for i, n in enumerate(("mla_w_o", "hgrn_w_o", "w_out")):
        grads[n] = natural(joined_m[2], i * lay.r_o, lay.r_o, False)
    grads["mla_w_q_up"] = natural(joined_m[3], 0, lay.hps * (HEAD + ROPE), True)
    grads["mla_w_kv_up"] = natural(joined_m[4], 0, lay.hps * QGROUP, True)
    done += adam(["w_in", "mla_w_q_up", "mla_w_kv_up", "mla_w_o", "hgrn_w_o", "w_out"])

    dlb = dlb_h.reshape(1, -1)
    dnorm = jnp.sum(dnorm_h, axis=0)
    small = {**{n: grads[n] for n in SMALL_WEIGHTS if n not in ("hgrn_lb_logits", "hgrn_out_norm")},
             "hgrn_lb_logits": dlb, "hgrn_out_norm": dnorm}
    vec = jnp.concatenate([small[n] for n in SMALL_WEIGHTS], axis=1)
    vec = _all_reduce_small(vec, name="grad_all_reduce_small")
    off = 0
    for n in SMALL_WEIGHTS:
        w_n = small[n].shape[1]
        grads[n] = vec[:, off:off + w_n]
        off += w_n
    grads["hgrn_lb_logits"] = _lb_logits_grad(hgrn_lb_logits, grads["hgrn_lb_logits"], name="lb_logits_grad")

    done += adam(list(SMALL_WEIGHTS))
    ffn_grads(reduce_end(exch1, done, "ffn1"), "ffn1")

    loss = lax.psum(loss_local, ("x", "y", "c"))
    dx_out = dx0.reshape(x.shape)
    return (loss, dx_out, *[grads[n] for n in ALL_WEIGHTS], *[deltas[n] for n in ALL_WEIGHTS],
            *[new_m[n] for n in ALL_WEIGHTS], *[new_v[n] for n in ALL_WEIGHTS])
```

```python
import functools

import jax
import jax.numpy as jnp
from jax import lax
from jax.experimental import pallas as pl
from jax.experimental.pallas import tpu as pltpu

F32 = jnp.float32
BF16 = jnp.bfloat16
MESH = pl.DeviceIdType.MESH

NORM_EPS = 1e-6
MACARON_SCALE = 0.5
ROPE_THETA = 10000.0
HEAD = 128
ROPE = 64
QGROUP = 2 * HEAD
SUB = 16
ADAM_LR, ADAM_B1, ADAM_B2, ADAM_EPS, ADAM_WD, ADAM_STEP = 0.001, 0.9, 0.999, 1e-08, 0.01, 10

LANE = 128
VMEM_LIMIT = 48 * 1024 * 1024
MM_TILE = 1024
MM_TILE_WIDE = 1536

BIG_WEIGHTS = ("ffn1_w_gate", "ffn1_w_up", "ffn1_w_down", "w_in", "mla_w_q_up", "mla_w_kv_up",
               "mla_w_o", "hgrn_w_o", "w_out", "ffn2_w_gate", "ffn2_w_up", "ffn2_w_down")
COL_SHARDED = ("ffn1_w_gate", "ffn1_w_up", "w_in", "mla_w_q_up", "mla_w_kv_up", "ffn2_w_gate", "ffn2_w_up")
SMALL_WEIGHTS = ("ffn1_norm_pre", "ffn1_norm_post", "mix_norm_pre", "mla_q_norm", "mla_kv_norm",
                 "hgrn_lb_logits", "hgrn_out_norm", "mix_norm_post", "ffn2_norm_pre", "ffn2_norm_post")
ALL_WEIGHTS = ("ffn1_norm_pre", "ffn1_w_gate", "ffn1_w_up", "ffn1_w_down", "ffn1_norm_post", "mix_norm_pre",
               "w_in", "mla_q_norm", "mla_w_q_up", "mla_kv_norm", "mla_w_kv_up", "mla_w_o", "hgrn_lb_logits",
               "hgrn_out_norm", "hgrn_w_o", "w_out", "mix_norm_post", "ffn2_norm_pre", "ffn2_w_gate",
               "ffn2_w_up", "ffn2_w_down", "ffn2_norm_post")


def _params(*sem):
    return pltpu.CompilerParams(dimension_semantics=sem or None, vmem_limit_bytes=VMEM_LIMIT)


def _pick(n, cap, offset=0):
    if n <= cap and offset % n == 0:
        return n
    best = None
    for t in range(LANE, min(n, cap) + 1, LANE):
        if n % t == 0 and offset % t == 0:
            best = t
    assert best is not None, (n, cap, offset)
    return best


def _row_tile(n, row_bytes, budget=1 << 20):
    best = None
    for t in range(8, n + 1, 8):
        if n % t == 0 and t * row_bytes <= budget:
            best = t
    return n if best is None else best


def _sigmoid(x):
    return 1.0 / (1.0 + jnp.exp(-x))


def _silu(x):
    return x * _sigmoid(x)


def _dsilu(x):
    s = _sigmoid(x)
    return s * (1.0 + x * (1.0 - s))


def _mm(pairs, *, name, mode="nn", out_dtype=F32, into=None, deps=()):
    halves = isinstance(pairs[0][1], tuple)
    assert halves or mode == "tn"
    pairs = [(a, b if halves else (b, 0, b.shape[0])) for a, b in pairs]
    a0, (b0, b_off, b_rows) = pairs[0]
    hw = b0.shape[2] if halves else (into[0].shape[2] if into is not None else None)
    if mode == "nn":
        (m, kdim), n = a0.shape, 2 * hw
    elif mode == "nt":
        (m, kdim), n = a0.shape, b_rows
        assert kdim == 2 * hw
    else:
        (kdim, m), n = a0.shape, b0.shape[1]
    out_off = 0 if into is None else into[1]
    tm = _pick(m, MM_TILE_WIDE if mode == "tn" else MM_TILE, out_off)
    tn = hw if (mode == "nn" or into is not None) else _pick(n, MM_TILE_WIDE, b_off if mode == "nt" else 0)
    tk = hw if mode == "nt" else _pick(kdim, MM_TILE, b_off if mode == "nn" else 0)
    assert n % tn == 0 and kdim % tk == 0
    nk = kdim // tk
    npair = len(pairs)
    dims = {"nn": (((1,), (0,)), ((), ())), "nt": (((1,), (1,)), ((), ())), "tn": (((0,), (0,)), ((), ()))}[mode]

    def body(*refs):
        ins, o_ref, acc_ref = refs[:2 * npair], refs[-2], refs[-1]
        k = pl.program_id(2)

        @pl.when(k == 0)
        def _():
            acc_ref[...] = jnp.zeros_like(acc_ref)

        for p in range(npair):
            a = ins[2 * p][...].astype(BF16)
            b = ins[2 * p + 1][...].astype(BF16)
            acc_ref[...] += lax.dot_general(a, b, dims, preferred_element_type=F32)

        @pl.when(k == nk - 1)
        def _():
            o_ref[...] = acc_ref[...].astype(o_ref.dtype)

    a_spec = pl.BlockSpec((tk, tm), lambda i, j, k: (k, i)) if mode == "tn" else pl.BlockSpec((tm, tk), lambda i, j, k: (i, k))
    in_specs, flat = [], []
    for a, (b, off, _) in pairs:
        if mode == "nt":
            b_spec = pl.BlockSpec((None, tn, tk), lambda i, j, k, o=off // tn: (k, j + o, 0))
        elif mode == "nn":
            b_spec = pl.BlockSpec((None, tk, tn), lambda i, j, k, o=off // tk: (j, k + o, 0))
        else:
            b_spec = pl.BlockSpec((tk, tn), lambda i, j, k: (k, j))
        in_specs += [a_spec, b_spec]
        flat += [a, b]
    for dep in deps:
        in_specs.append(pl.BlockSpec(memory_space=pl.ANY))
        flat.append(dep)
    if into is None:
        out_shape, aliases = jax.ShapeDtypeStruct((m, n), out_dtype), {}
        out_spec = pl.BlockSpec((tm, tn), lambda i, j, k: (i, j))
    else:
        out_shape, aliases = jax.ShapeDtypeStruct(into[0].shape, into[0].dtype), {len(flat): 0}
        out_spec = pl.BlockSpec((None, tm, tn), lambda i, j, k, o=out_off // tm: (j, i + o, 0))
        in_specs.append(pl.BlockSpec(memory_space=pl.ANY))
        flat.append(into[0])
    return pl.pallas_call(
        body, name=name, grid=(m // tm, n // tn, nk),
        in_specs=in_specs,
        out_specs=out_spec,
        out_shape=out_shape, input_output_aliases=aliases,
        scratch_shapes=[pltpu.VMEM((tm, tn), F32)],
        compiler_params=_params("parallel", "parallel", "arbitrary"),
    )(*flat)


def _norm_fwd(y, w, *, name, resid=None, scale=1.0, out_dtype=F32, col=0):
    t, d = y.shape[0], w.shape[1]
    tr = _pick(t, 256)
    assert col % d == 0

    def body(*refs):
        if resid is None:
            y_ref, w_ref, o_ref = refs
        else:
            y_ref, w_ref, r_ref, o_ref = refs
        yv = y_ref[...]
        out = yv * lax.rsqrt(jnp.mean(yv * yv, axis=-1, keepdims=True) + NORM_EPS) * w_ref[...]
        if resid is not None:
            out = r_ref[...] + scale * out
        o_ref[...] = out.astype(out_dtype)

    row = pl.BlockSpec((tr, d), lambda i: (i, 0))
    wspec = pl.BlockSpec((1, d), lambda i: (0, 0))
    ins, specs = [y, w], [pl.BlockSpec((tr, d), lambda i: (i, col // d)), wspec]
    if resid is not None:
        ins.append(resid)
        specs.append(row)
    return pl.pallas_call(
        body, name=name, grid=(t // tr,), in_specs=specs, out_specs=row,
        out_shape=jax.ShapeDtypeStruct((t, d), out_dtype), compiler_params=_params("parallel"),
    )(*ins)


def _norm_bwd(x, w, dy, *, name, scale=1.0, dres=None, col=0, dx_dtype=F32):
    t, d = x.shape[0], w.shape[1]
    tr = _pick(t, 256)
    assert col % d == 0

    def body(*refs):
        if dres is None:
            x_ref, w_ref, dy_ref, dx_ref, dw_ref = refs
        else:
            x_ref, w_ref, dy_ref, dr_ref, dx_ref, dw_ref = refs

        @pl.when(pl.program_id(0) == 0)
        def _():
            dw_ref[...] = jnp.zeros_like(dw_ref)

        xv = x_ref[...]
        r = lax.rsqrt(jnp.mean(xv * xv, axis=-1, keepdims=True) + NORM_EPS)
        xhat = xv * r
        dyv = dy_ref[...].astype(F32) * scale
        dw_ref[...] += jnp.sum(dyv * xhat, axis=0, keepdims=True)
        t_ = dyv * w_ref[...]
        dx = r * (t_ - xhat * jnp.mean(t_ * xhat, axis=-1, keepdims=True))
        if dres is not None:
            dx = dx + dr_ref[...]
        dx_ref[...] = dx.astype(dx_dtype)

    row = pl.BlockSpec((tr, d), lambda i: (i, 0))
    wspec = pl.BlockSpec((1, d), lambda i: (0, 0))
    ins, specs = [x, w, dy], [pl.BlockSpec((tr, d), lambda i: (i, col // d)), wspec, row]
    if dres is not None:
        ins.append(dres)
        specs.append(row)
    return pl.pallas_call(
        body, name=name, grid=(t // tr,), in_specs=specs, out_specs=(row, wspec),
        out_shape=(jax.ShapeDtypeStruct((t, d), dx_dtype), jax.ShapeDtypeStruct((1, d), F32)),
        compiler_params=_params("arbitrary"),
    )(*ins)


def _elementwise(fn, ins, out_dtypes, *, name, width=None, cols=None):
    t = ins[0].shape[0]
    d = ins[0].shape[1] if width is None else width
    cols = [0] * len(ins) if cols is None else cols
    tc = _pick(d, 2048)
    for c in cols:
        tc = _pick(d, tc, c)
    tr = _row_tile(t, tc * 4)
    nout = len(out_dtypes)

    def body(*refs):
        outs = fn(*[r[...].astype(F32) for r in refs[:len(ins)]])
        for o_ref, o in zip(refs[len(ins):], outs):
            o_ref[...] = o.astype(o_ref.dtype)

    spec = pl.BlockSpec((tr, tc), lambda i, j: (i, j))
    in_specs = [pl.BlockSpec((tr, tc), lambda i, j, o=c // tc: (i, j + o)) for c in cols]
    return pl.pallas_call(
        body, name=name, grid=(t // tr, d // tc), in_specs=in_specs, out_specs=[spec] * nout,
        out_shape=[jax.ShapeDtypeStruct((t, d), dt) for dt in out_dtypes],
        compiler_params=_params("parallel", "parallel"),
    )(*ins)


def _swiglu_fwd(g, u, *, name):
    return _elementwise(lambda gv, uv: (_silu(gv) * uv,), [g, u], [BF16], name=name)[0]


def _swiglu_bwd(da, g, u, *, name):
    return _elementwise(lambda dav, gv, uv: (dav * uv * _dsilu(gv), dav * _silu(gv)), [da, g, u], [BF16, BF16], name=name)


def _merge_fwd(proj, col_a, col_b, ya, yb, *, name):
    return _elementwise(lambda a, b, p, q: (_sigmoid(a) * p + _sigmoid(b) * q,), [proj, proj, ya, yb], [BF16],
                        name=name, width=ya.shape[1], cols=[col_a, col_b, 0, 0])[0]


def _merge_bwd(dm, proj, col_a, col_b, ya, yb, *, name):
    def fn(dmv, a, b, p, q):
        sa, sb = _sigmoid(a), _sigmoid(b)
        return dmv * p * sa * (1.0 - sa), dmv * q * sb * (1.0 - sb), dmv * sa, dmv * sb

    return _elementwise(fn, [dm, proj, proj, ya, yb], [BF16, BF16, BF16, BF16], name=name, width=ya.shape[1],
                        cols=[0, col_a, col_b, 0, 0])


def _loss_head(xo, target, *, name):
    t, d = xo.shape
    tr = _pick(t, 256)

    def body(x_ref, t_ref, dx_ref, l_ref):
        @pl.when(pl.program_id(0) == 0)
        def _():
            l_ref[...] = jnp.zeros_like(l_ref)

        err = x_ref[...] - t_ref[...]
        dx_ref[...] = err * (1.0 / d)
        l_ref[...] += 0.5 * jnp.sum(jnp.mean(err * err, axis=-1, keepdims=True), axis=0, keepdims=True)

    row = pl.BlockSpec((tr, d), lambda i: (i, 0))
    dx, l = pl.pallas_call(
        body, name=name, grid=(t // tr,), in_specs=[row, row],
        out_specs=(row, pl.BlockSpec((1, 1), lambda i: (0, 0))),
        out_shape=(jax.ShapeDtypeStruct((t, d), F32), jax.ShapeDtypeStruct((1, 1), F32)),
        compiler_params=_params("arbitrary"),
    )(xo, target)
    return dx, l[0, 0]


def _rope(xin, tabs, *, name, group, backward, out_dtype, col=0, ngroup=None):
    t = xin.shape[0]
    ngroup = xin.shape[1] // group if ngroup is None else ngroup
    wdt = ngroup * group
    tr = _pick(t, 256)
    assert col % group == 0
    cos_t, nsin_t, sin_t = tabs

    def body(x_ref, c_ref, n_ref, s_ref, o_ref):
        xv = x_ref[...].astype(F32)
        rot = xv[:, group - LANE:]
        if backward:
            out = rot * c_ref[...] + pltpu.roll(rot * n_ref[...], 32, 1) + pltpu.roll(rot * s_ref[...], LANE - 32, 1)
        else:
            out = rot * c_ref[...] + pltpu.roll(rot, LANE - 32, 1) * n_ref[...] + pltpu.roll(rot, 32, 1) * s_ref[...]
        if group > LANE:
            out = jnp.concatenate([xv[:, :group - LANE], out], axis=1)
        o_ref[...] = out.astype(out_dtype)

    xspec = pl.BlockSpec((tr, group), lambda i, g: (i, g))
    tspec = pl.BlockSpec((tr, LANE), lambda i, g: (i, 0))
    return pl.pallas_call(
        body, name=name, grid=(t // tr, ngroup),
        in_specs=[pl.BlockSpec((tr, group), lambda i, g: (i, g + col // group)), tspec, tspec, tspec], out_specs=xspec,
        out_shape=jax.ShapeDtypeStruct((t, wdt), out_dtype), compiler_params=_params("parallel", "parallel"),
    )(xin, cos_t, nsin_t, sin_t)


def _scores(q, kv, kr, qi, tq, scale):
    kcat = jnp.concatenate([kv[:, :HEAD], kr], axis=1)
    s = lax.dot_general(q, kcat, (((1,), (1,)), ((), ())), preferred_element_type=F32) * scale
    row = qi * tq + lax.broadcasted_iota(jnp.int32, s.shape, 0)
    col = lax.broadcasted_iota(jnp.int32, s.shape, 1)
    s = jnp.where(col <= row, s, -jnp.inf)
    p = jnp.exp(s - jnp.max(s, axis=-1, keepdims=True))
    return p / jnp.sum(p, axis=-1, keepdims=True), kcat


def _attn_fwd(qcat, kv, kr, *, name, scale):
    t = qcat.shape[0]
    nh = qcat.shape[1] // QGROUP
    tq = _pick(t, 256)

    def body(q_ref, kv_ref, kr_ref, o_ref):
        for qi in range(t // tq):
            @pl.when(pl.program_id(1) == qi)
            def _(qi=qi):
                kvv = kv_ref[0:(qi + 1) * tq, :]
                p, _ = _scores(q_ref[...], kvv, kr_ref[0:(qi + 1) * tq, :], qi, tq, scale)
                o_ref[...] = jnp.dot(p.astype(BF16), kvv[:, HEAD:], preferred_element_type=F32).astype(BF16)

    return pl.pallas_call(
        body, name=name, grid=(nh, t // tq),
        in_specs=[pl.BlockSpec((tq, QGROUP), lambda h, i: (i, h)), pl.BlockSpec((t, QGROUP), lambda h, i: (0, h)),
                  pl.BlockSpec((t, LANE), lambda h, i: (0, 0))],
        out_specs=pl.BlockSpec((tq, HEAD), lambda h, i: (i, h)),
        out_shape=jax.ShapeDtypeStruct((t, nh * HEAD), BF16), compiler_params=_params("parallel", "parallel"),
    )(qcat, kv, kr)


def _attn_bwd(qcat, kv, kr, do, *, name, scale):
    t = qcat.shape[0]
    nh = qcat.shape[1] // QGROUP
    tq = _pick(t, 256)
    nq = t // tq

    def body(q_ref, kv_ref, kr_ref, do_ref, dq_ref, dkv_ref, dkr_ref, dk_acc, dv_acc):
        h, i = pl.program_id(0), pl.program_id(1)

        @pl.when(i == 0)
        def _():
            dk_acc[...] = jnp.zeros_like(dk_acc)
            dv_acc[...] = jnp.zeros_like(dv_acc)

        @pl.when((i == 0) & (h == 0))
        def _():
            dkr_ref[...] = jnp.zeros_like(dkr_ref)

        for qi in range(nq):
            @pl.when(i == qi)
            def _(qi=qi):
                keys = slice(0, (qi + 1) * tq)
                q = q_ref[...]
                kvv = kv_ref[keys, :]
                dov = do_ref[...].astype(BF16)
                p, kcat = _scores(q, kvv, kr_ref[keys, :], qi, tq, scale)
                dp = lax.dot_general(dov, kvv[:, HEAD:], (((1,), (1,)), ((), ())), preferred_element_type=F32)
                ds = (p * (dp - jnp.sum(p * dp, axis=-1, keepdims=True)) * scale).astype(BF16)
                dq_ref[...] = jnp.dot(ds, kcat, preferred_element_type=F32)
                dk_acc[keys, :] += lax.dot_general(ds, q, (((0,), (0,)), ((), ())), preferred_element_type=F32)
                dv_acc[keys, :] += lax.dot_general(p.astype(BF16), dov, (((0,), (0,)), ((), ())), preferred_element_type=F32)

        @pl.when(i == nq - 1)
        def _():
            dk = dk_acc[...]
            dkv_ref[...] = jnp.concatenate([dk[:, :HEAD], dv_acc[...]], axis=1)
            dkr_ref[...] += dk[:, HEAD:]

    return pl.pallas_call(
        body, name=name, grid=(nh, nq),
        in_specs=[pl.BlockSpec((tq, QGROUP), lambda h, i: (i, h)), pl.BlockSpec((t, QGROUP), lambda h, i: (0, h)),
                  pl.BlockSpec((t, LANE), lambda h, i: (0, 0)), pl.BlockSpec((tq, HEAD), lambda h, i: (i, h))],
        out_specs=(pl.BlockSpec((tq, QGROUP), lambda h, i: (i, h)), pl.BlockSpec((t, QGROUP), lambda h, i: (0, h)),
                   pl.BlockSpec((t, LANE), lambda h, i: (0, 0))),
        out_shape=(jax.ShapeDtypeStruct((t, nh * QGROUP), F32), jax.ShapeDtypeStruct((t, nh * QGROUP), F32),
                   jax.ShapeDtypeStruct((t, LANE), F32)),
        scratch_shapes=[pltpu.VMEM((t, QGROUP), F32), pltpu.VMEM((t, HEAD), F32)],
        compiler_params=_params("arbitrary", "arbitrary"),
    )(qcat, kv, kr, do)


def _split3(x):
    hi = x.astype(BF16)
    r1 = x - hi.astype(F32)
    mid = r1.astype(BF16)
    lo = (r1 - mid.astype(F32)).astype(BF16)
    return hi, mid, lo


def _tri_matmul(mask, x):
    m = mask.astype(BF16)
    return sum(jnp.dot(m, part, preferred_element_type=F32) for part in _split3(x))


def _sub_cumsum(g, tb):
    row = lax.broadcasted_iota(jnp.int32, (tb, tb), 0)
    col = lax.broadcasted_iota(jnp.int32, (tb, tb), 1)
    return _tri_matmul(jnp.where((col <= row) & (col // SUB == row // SUB), 1.0, 0.0), g)


def _sub_suffix_prefix(after, before, tb):
    row = lax.broadcasted_iota(jnp.int32, (tb, tb), 0)
    col = lax.broadcasted_iota(jnp.int32, (tb, tb), 1)
    same = col // SUB == row // SUB
    return (_tri_matmul(jnp.where((col >= row) & same, 1.0, 0.0), after)
            + _tri_matmul(jnp.where((col < row) & same, 1.0, 0.0), before))


def _lower_bound(logits):
    mx = jnp.max(logits, axis=0, keepdims=True)
    e = jnp.exp(logits - mx)
    return e[0:1, :] / jnp.sum(e, axis=0, keepdims=True)


def _hgrn_fwd(proj, cols, wdt, logits, out_norm, *, name):
    t = proj.shape[0]
    nh = wdt // HEAD
    tb = _pick(t, 128)
    ns = tb // SUB

    def body(hq_ref, hf_ref, hi_ref, hg_ref, lg_ref, w_ref, o_ref, yb_ref, st_ref, s_ref, q_s, k_s, b_s):
        @pl.when(pl.program_id(1) == 0)
        def _():
            s_ref[...] = jnp.zeros_like(s_ref)

        lb = _lower_bound(lg_ref[...])
        f = lb + (1.0 - lb) * _sigmoid(hf_ref[...])
        q_s[...] = _silu(hq_ref[...])
        k_s[...] = 1.0 - f
        b_s[...] = _sub_cumsum(jnp.log(f), tb)
        rowid = lax.broadcasted_iota(jnp.int32, (SUB, HEAD), 0)

        def sub(c, st):
            rows = pl.ds(pl.multiple_of(c * SUB, SUB), SUB)
            qc, kc, bc, vc = q_s[rows, :], k_s[rows, :], b_s[rows, :], hi_ref[rows, :]
            st_ref[0, c] = st
            bl = bc[SUB - 1:SUB, :]
            oc = lax.dot_general((qc * jnp.exp(bc)).astype(BF16), st.astype(BF16), (((1,), (1,)), ((), ())),
                                 preferred_element_type=F32)
            for s in range(SUB):
                e = jnp.where(rowid >= s, jnp.exp(bc - bc[s:s + 1, :]), 0.0)
                a = jnp.sum(qc * e * kc[s:s + 1, :], axis=1, keepdims=True)
                oc = oc + a * vc[s:s + 1, :]
            o_ref[rows, :] = oc
            kd = kc * jnp.exp(bl - bc)
            return jnp.exp(bl) * st + lax.dot_general(vc.astype(BF16), kd.astype(BF16), (((0,), (0,)), ((), ())),
                                                      preferred_element_type=F32)

        s_ref[...] = lax.fori_loop(0, ns, sub, s_ref[...], unroll=True)
        o = o_ref[...]
        r = lax.rsqrt(jnp.mean(o * o, axis=-1, keepdims=True) + NORM_EPS)
        yb_ref[...] = (o * r * w_ref[...] * _silu(hg_ref[...])).astype(BF16)

    blk = pl.BlockSpec((tb, HEAD), lambda h, j: (j, h))
    return pl.pallas_call(
        body, name=name, grid=(nh, t // tb),
        in_specs=[pl.BlockSpec((tb, HEAD), lambda h, j, o=c // HEAD: (j, h + o)) for c in cols]
        + [pl.BlockSpec((2, HEAD), lambda h, j: (0, h)), pl.BlockSpec((1, HEAD), lambda h, j: (0, 0))],
        out_specs=(blk, blk, pl.BlockSpec((1, ns, HEAD, HEAD), lambda h, j: (h, j, 0, 0))),
        out_shape=(jax.ShapeDtypeStruct((t, wdt), F32), jax.ShapeDtypeStruct((t, wdt), BF16),
                   jax.ShapeDtypeStruct((nh, t // SUB, HEAD, HEAD), F32)),
        scratch_shapes=[pltpu.VMEM((HEAD, HEAD), F32)] + [pltpu.VMEM((tb, HEAD), F32)] * 3,
        compiler_params=_params("parallel", "arbitrary"),
    )(proj, proj, proj, proj, logits, out_norm)


def _hgrn_bwd(proj, cols, wdt, o_raw, dyb, states, logits, out_norm, *, name):
    t = proj.shape[0]
    nh = wdt // HEAD
    tb = _pick(t, 128)
    ns = tb // SUB
    nb = t // tb

    def body(hq_ref, hf_ref, hi_ref, hg_ref, o_ref, dy_ref, st_ref, lg_ref, w_ref,
             dhq_ref, dhf_ref, dhi_ref, dhg_ref, dlb_ref, dw_ref,
             ds_ref, q_s, k_s, b_s, do_s, dq_s, dk_s, dv_s, after_s, before_s, thru_s):
        @pl.when(pl.program_id(1) == 0)
        def _():
            ds_ref[...] = jnp.zeros_like(ds_ref)
            dlb_ref[...] = jnp.zeros_like(dlb_ref)
            dw_ref[...] = jnp.zeros_like(dw_ref)

        lb = _lower_bound(lg_ref[...])
        hqv, hgv = hq_ref[...], hg_ref[...]
        sig = _sigmoid(hf_ref[...])
        f = lb + (1.0 - lb) * sig
        q_s[...] = _silu(hqv)
        k_s[...] = 1.0 - f
        b_s[...] = _sub_cumsum(jnp.log(f), tb)

        o = o_ref[...]
        r = lax.rsqrt(jnp.mean(o * o, axis=-1, keepdims=True) + NORM_EPS)
        nrm = o * r
        w = w_ref[...]
        dy = dy_ref[...].astype(F32)
        dhg_ref[...] = (dy * nrm * w * _dsilu(hgv)).astype(BF16)
        dnw = dy * _silu(hgv)
        dw_ref[0] += jnp.sum(dnw * nrm, axis=0, keepdims=True)
        tt = dnw * w
        do_s[...] = r * (tt - nrm * jnp.mean(tt * nrm, axis=-1, keepdims=True))
        rowid = lax.broadcasted_iota(jnp.int32, (SUB, HEAD), 0)

        def sub(cc, dst):
            c = ns - 1 - cc
            rows = pl.ds(pl.multiple_of(c * SUB, SUB), SUB)
            qc, kc, bc, vc, doc = q_s[rows, :], k_s[rows, :], b_s[rows, :], hi_ref[rows, :], do_s[rows, :]
            st = st_ref[0, c]
            bl = bc[SUB - 1:SUB, :]
            eb = jnp.exp(bc)
            ekd = jnp.exp(bl - bc)
            qe, kd = qc * eb, kc * ekd
            dob, vcb = doc.astype(BF16), vc.astype(BF16)
            dq_st = jnp.dot(dob, st.astype(BF16), preferred_element_type=F32) * eb
            dk_st = jnp.dot(vcb, dst.astype(BF16), preferred_element_type=F32) * ekd
            dv = lax.dot_general(kd.astype(BF16), dst.astype(BF16), (((1,), (1,)), ((), ())), preferred_element_type=F32)
            dq_in = jnp.zeros_like(qc)
            dk_in = jnp.zeros_like(qc)
            for s in range(SUB):
                e = jnp.where(rowid >= s, jnp.exp(bc - bc[s:s + 1, :]), 0.0)
                ek = e * kc[s:s + 1, :]
                a = jnp.sum(qc * ek, axis=1, keepdims=True)
                da = jnp.sum(doc * vc[s:s + 1, :], axis=1, keepdims=True)
                dq_in = dq_in + da * ek
                dk_in = dk_in + jnp.where(rowid == s, jnp.sum(da * e * qc, axis=0, keepdims=True), 0.0)
                dv = dv + jnp.where(rowid == s, jnp.sum(a * doc, axis=0, keepdims=True), 0.0)
            ebl = jnp.exp(bl)
            dq_s[rows, :] = dq_st + dq_in
            dk_s[rows, :] = dk_st + dk_in
            dv_s[rows, :] = dv
            after_s[rows, :] = qc * (dq_st + dq_in) - kc * dk_in
            before_s[rows, :] = kc * dk_st
            thru_s[rows, :] = jnp.broadcast_to(ebl * jnp.sum(st * dst, axis=0, keepdims=True), (SUB, HEAD))
            return ebl * dst + lax.dot_general(dob, qe.astype(BF16), (((0,), (0,)), ((), ())), preferred_element_type=F32)

        ds_ref[...] = lax.fori_loop(0, ns, sub, ds_ref[...], unroll=True)
        dg = _sub_suffix_prefix(after_s[...], before_s[...], tb) + thru_s[...]
        dhq_ref[...] = (dq_s[...] * _dsilu(hqv)).astype(BF16)
        dft = dg / f - dk_s[...]
        dhf_ref[...] = (dft * (1.0 - lb) * sig * (1.0 - sig)).astype(BF16)
        dlb_ref[0] += jnp.sum(dft * (1.0 - sig), axis=0, keepdims=True)
        dhi_ref[...] = dv_s[...].astype(BF16)

    blk = pl.BlockSpec((tb, HEAD), lambda h, j: (nb - 1 - j, h))
    vec = pl.BlockSpec((1, 1, HEAD), lambda h, j: (h, 0, 0))
    tok = jax.ShapeDtypeStruct((t, wdt), BF16)
    per_head = jax.ShapeDtypeStruct((nh, 1, HEAD), F32)
    return pl.pallas_call(
        body, name=name, grid=(nh, nb),
        in_specs=[pl.BlockSpec((tb, HEAD), lambda h, j, o=c // HEAD: (nb - 1 - j, h + o)) for c in cols]
        + [blk, blk] + [pl.BlockSpec((1, ns, HEAD, HEAD), lambda h, j: (h, nb - 1 - j, 0, 0)),
                              pl.BlockSpec((2, HEAD), lambda h, j: (0, h)), pl.BlockSpec((1, HEAD), lambda h, j: (0, 0))],
        out_specs=(blk, blk, blk, blk, vec, vec),
        out_shape=(tok, tok, tok, tok, per_head, per_head),
        scratch_shapes=[pltpu.VMEM((HEAD, HEAD), F32)] + [pltpu.VMEM((tb, HEAD), F32)] * 10,
        compiler_params=_params("arbitrary", "arbitrary"),
    )(proj, proj, proj, proj, o_raw, dyb, states, logits, out_norm)


def _lb_logits_grad(logits, dlb, *, name):
    def body(lg_ref, d_ref, o_ref):
        lg = lg_ref[...]
        e = jnp.exp(lg - jnp.max(lg, axis=0, keepdims=True))
        p = e / jnp.sum(e, axis=0, keepdims=True)
        d = d_ref[...]
        rowid = lax.broadcasted_iota(jnp.int32, lg.shape, 0)
        dp = jnp.where(rowid == 0, d, 0.0)
        o_ref[...] = p * (dp - jnp.sum(p * dp, axis=0, keepdims=True))

    return pl.pallas_call(body, name=name, out_shape=jax.ShapeDtypeStruct(logits.shape, F32))(logits, dlb)


def _adamw(w, g, m, v, *, name, deps=()):
    r, c = w.shape
    tc = _pick(c, 2048) if c % LANE == 0 else c
    tr = _row_tile(r, tc * 4)

    def body(w_ref, g_ref, m_ref, v_ref, *rest):
        d_ref, nm_ref, nv_ref = rest[-3:]
        gv = g_ref[...]
        nm = ADAM_B1 * m_ref[...] + (1.0 - ADAM_B1) * gv
        nv = ADAM_B2 * v_ref[...] + (1.0 - ADAM_B2) * (gv * gv)
        m_hat = nm / (1.0 - ADAM_B1 ** ADAM_STEP)
        v_hat = nv / (1.0 - ADAM_B2 ** ADAM_STEP)
        d_ref[...] = -ADAM_LR * (m_hat / (jnp.sqrt(v_hat) + ADAM_EPS) + ADAM_WD * w_ref[...])
        nm_ref[...] = nm
        nv_ref[...] = nv

    spec = pl.BlockSpec((tr, tc), lambda i, j: (i, j))
    shp = jax.ShapeDtypeStruct((r, c), F32)
    return pl.pallas_call(
        body, name=name, grid=(r // tr, c // tc), in_specs=[spec] * 4 + [ANY] * len(deps), out_specs=[spec] * 3,
        out_shape=[shp, shp, shp], compiler_params=_params("parallel", "parallel"),
    )(w, g, m, v, *deps)


def _coords():
    return lax.axis_index("x"), lax.axis_index("y"), lax.axis_index("c")


def _other_chips(x, y):
    return [(1 - x, y), (x, 1 - y), (1 - x, 1 - y)]


ANY = pl.BlockSpec(memory_space=pl.ANY)


class _Layout:
    def __init__(self, d, dff, in_cols, q_lora, kv_lora, nh):
        assert q_lora == kv_lora and nh % 4 == 0 and dff % (4 * LANE) == 0 and in_cols % 4 == 0 and d % 4 == 0
        self.d, self.dff, self.q_lora, self.nh = d, dff, q_lora, nh
        self.head = q_lora + kv_lora + ROPE
        self.pad = d - self.head
        self.nff, self.ncol, self.r_o, self.hps = dff // 4, in_cols // 4, d // 4, nh // 4
        assert self.head <= self.ncol
        self.off_q, self.off_kv, self.rows_narrow = 0, nh * QGROUP, 2 * nh * QGROUP


HBM = pl.BlockSpec(memory_space=pltpu.HBM)
SEMS = pl.BlockSpec(memory_space=pltpu.SEMAPHORE)
SPLIT = dict(has_side_effects=pltpu.SideEffectType.DATAFLOW_SIDE_EFFECTING)


def _in_hbm(a):
    return pltpu.with_memory_space_constraint(a, pltpu.HBM)


def _shard_rows(jobs, k):
    out, lrow = [], [0, 0]
    for job in jobs:
        for row, rows in job.pieces(k):
            out.append((job.a, lrow[job.a], row, rows))
            lrow[job.a] += rows
    return out


def _shard_total(jobs, a):
    return sum(rows for b, _, _, rows in _shard_rows(jobs, 0) if b == a)


def _gather_start(packs, lands, jobs, *, name, deps=()):
    n = len(packs)

    def body(*refs):
        p_refs, l_refs, send, recv, token = refs[:n], refs[n:2 * n], refs[-2 * n - 3], refs[-2 * n - 2], refs[-1]
        x, y, c = _coords()
        for a, lrow, row, rows in _shard_rows(jobs, 2 * x + y):
            pltpu.make_async_remote_copy(
                src_ref=p_refs[a].at[:, pl.ds(lrow, rows)], dst_ref=l_refs[a].at[:, pl.ds(row, rows)],
                send_sem=send.at[4 * a + 3], recv_sem=recv.at[4 * a + 3], device_id=(x, y, 1 - c), device_id_type=MESH).start()
            for j, (px, py) in enumerate(_other_chips(x, y)):
                pltpu.make_async_remote_copy(
                    src_ref=p_refs[a].at[c, pl.ds(lrow, rows)], dst_ref=l_refs[a].at[c, pl.ds(row, rows)],
                    send_sem=send.at[4 * a + j], recv_sem=recv.at[4 * a + j], device_id=(px, py, c), device_id_type=MESH).start()
        token[...] = jnp.zeros_like(token)

    thru = [pltpu.HBM(a.shape, a.dtype) for a in packs + lands]
    out = pl.pallas_call(
        body, name=name, in_specs=[HBM] * (2 * n) + [ANY] * len(deps),
        out_shape=(pltpu.SemaphoreType.DMA((4 * n,)), pltpu.SemaphoreType.DMA((4 * n,)), *thru, jax.ShapeDtypeStruct((8, LANE), F32)),
        out_specs=(SEMS, SEMS, *[HBM] * (2 * n), pl.BlockSpec(memory_space=pltpu.VMEM)),
        input_output_aliases={i: 2 + i for i in range(2 * n)}, compiler_params=pltpu.CompilerParams(**SPLIT),
    )(*[_in_hbm(a) for a in packs + lands], *deps)
    return dict(send=out[0], recv=out[1], bufs=list(out[2:2 + 2 * n]), n=n, jobs=jobs), out[-1]


def _gather_wait(handle, after, *, name):
    n, jobs = handle["n"], handle["jobs"]

    def body(*refs):
        l_refs, send, recv = refs[n:2 * n], refs[2 * n], refs[2 * n + 1]
        x, y, c = _coords()
        for a in range(n):
            total = _shard_total(jobs, a)
            for j, like in enumerate([l_refs[a].at[0, pl.ds(0, total)]] * 3 + [l_refs[a].at[:, pl.ds(0, total)]]):
                cp = pltpu.make_async_remote_copy(src_ref=like, dst_ref=like, send_sem=send.at[4 * a + j],
                                                  recv_sem=recv.at[4 * a + j], device_id=(x, y, c), device_id_type=MESH)
                cp.wait_send()
                cp.wait_recv()

    out = pl.pallas_call(
        body, name=name, in_specs=[HBM] * (2 * n) + [SEMS, SEMS] + [ANY] * len(after),
        out_shape=[pltpu.HBM(a.shape, a.dtype) for a in handle["bufs"]], out_specs=[HBM] * (2 * n),
        input_output_aliases={i: i for i in range(2 * n)}, compiler_params=pltpu.CompilerParams(**SPLIT),
    )(*handle["bufs"], handle["send"], handle["recv"], *after)
    return list(out[n:])


def _gather_forward(lands, jobs, *, name):
    n = len(lands)

    def body(*refs):
        l_refs, send, recv = refs[n:2 * n], refs[2 * n], refs[2 * n + 1]
        x, y, c = _coords()
        for j, (px, py) in enumerate(_other_chips(x, y)):
            for a, _, row, rows in _shard_rows(jobs, 2 * px + py):
                blk = l_refs[a].at[c, pl.ds(row, rows)]
                pltpu.make_async_remote_copy(src_ref=blk, dst_ref=blk, send_sem=send.at[3 * a + j], recv_sem=recv.at[3 * a + j],
                                             device_id=(x, y, 1 - c), device_id_type=MESH).start()
        for a in range(n):
            like = l_refs[a].at[0, pl.ds(0, _shard_total(jobs, a))]
            for j in range(3):
                cp = pltpu.make_async_remote_copy(src_ref=like, dst_ref=like, send_sem=send.at[3 * a + j],
                                                  recv_sem=recv.at[3 * a + j], device_id=(x, y, c), device_id_type=MESH)
                cp.wait_send()
                cp.wait_recv()

    sem = pltpu.SemaphoreType.DMA((3 * n,))
    return pl.pallas_call(
        body, name=name, in_specs=[ANY] * n, out_specs=[ANY] * n, input_output_aliases={i: i for i in range(n)},
        out_shape=[jax.ShapeDtypeStruct(a.shape, a.dtype) for a in lands], scratch_shapes=[sem, sem],
    )(*lands)


def _add_sibling(g, recv, sel, *, name):
    rows, hw = recv.shape
    tr = _row_tile(rows, hw * 4)

    def body(sel_ref, g_ref, r_ref, o_ref):
        o_ref[...] = (g_ref[...] + r_ref[...]).astype(BF16)

    return pl.pallas_call(
        body, name=name, out_shape=jax.ShapeDtypeStruct((rows, hw), BF16),
        grid_spec=pltpu.PrefetchScalarGridSpec(
            num_scalar_prefetch=1, grid=(rows // tr,),
            in_specs=[pl.BlockSpec((None, tr, hw), lambda i, s: (s[0], i, 0)), pl.BlockSpec((tr, hw), lambda i, s: (i, 0))],
            out_specs=pl.BlockSpec((tr, hw), lambda i, s: (i, 0))),
        compiler_params=_params("parallel"),
    )(sel, g, recv)


class _Job:
    def __init__(self, a, blk, n_outer, n_inner, stride, start):
        self.a, self.blk, self.n_outer, self.n_inner, self.stride, self.start = a, blk, n_outer, n_inner, stride, start
        self.rows_out = n_outer * n_inner * blk

    def pieces(self, k):
        return [(self.start(k) + o * self.stride * self.blk, self.n_inner * self.blk) for o in range(self.n_outer)]


def _block_rows(rows, cap, *also):
    best = None
    for b in range(16, min(rows, cap) + 1, 16):
        if rows % b == 0 and all(v % b == 0 for v in also):
            best = b
    assert best is not None, (rows, also)
    return best


def _ffn_jobs(lay):
    b = _block_rows(lay.nff, 704, lay.dff)
    return [_Job(0, b, 3, lay.nff // b, lay.dff // b, lambda k: lay.nff * k)]


def _mix_jobs(lay):
    d, ncol, head, pad = lay.d, lay.ncol, lay.head, lay.pad
    first = lambda k, a, b: jnp.where(k == 0, a, b) if not isinstance(k, int) else (a if k == 0 else b)
    ba = _block_rows(head, 704, *[ncol * k + pad for k in (1, 2, 3)])
    bb = _block_rows(ncol - head, 704, *[ncol * k + d for k in (0, 1, 2, 3)])
    bo = _block_rows(lay.r_o, 704, d)
    bq = _block_rows(HEAD + ROPE, 704, QGROUP)
    bk = _block_rows(lay.hps * QGROUP, 704, lay.off_kv)
    return [_Job(0, ba, 1, head // ba, 0, lambda k: first(k, 0, ncol * k + pad)),
            _Job(0, bb, 1, (ncol - head) // bb, 0, lambda k: ncol * k + d),
            _Job(0, bo, 3, lay.r_o // bo, d // bo, lambda k: 7 * d + lay.r_o * k),
            _Job(1, bq, lay.hps, (HEAD + ROPE) // bq, QGROUP // bq, lambda k: QGROUP * lay.hps * k),
            _Job(1, bk, 1, lay.hps * QGROUP // bk, 0, lambda k: lay.off_kv + lay.hps * QGROUP * k)]


def _swap_start(gs, *, name):
    n = len(gs)
    lands = [lax.empty(g.shape[1:], g.dtype) for g in gs]

    def body(*refs):
        g_refs, land_refs, send, recv, token = refs[:n], refs[n:2 * n], refs[2 * n], refs[2 * n + 1], refs[-1]
        x, y, c = _coords()
        for a in range(n):
            pltpu.make_async_remote_copy(src_ref=g_refs[a].at[1 - c], dst_ref=land_refs[a], send_sem=send.at[a],
                                         recv_sem=recv.at[a], device_id=(x, y, 1 - c), device_id_type=MESH).start()
        token[...] = jnp.zeros_like(token)

    thru = [pltpu.HBM(a.shape, a.dtype) for a in gs + lands]
    out = pl.pallas_call(
        body, name=name, in_specs=[HBM] * (2 * n),
        out_shape=(pltpu.SemaphoreType.DMA((n,)), pltpu.SemaphoreType.DMA((n,)), *thru, jax.ShapeDtypeStruct((8, LANE), F32)),
        out_specs=(SEMS, SEMS, *[HBM] * (2 * n), pl.BlockSpec(memory_space=pltpu.VMEM)),
        input_output_aliases={i: 2 + i for i in range(2 * n)}, compiler_params=pltpu.CompilerParams(**SPLIT),
    )(*[_in_hbm(a) for a in gs + lands])
    return dict(send=out[0], recv=out[1], bufs=list(out[2:2 + 2 * n]), n=n), out[-1]


def _swap_wait(handle, after, *, name):
    n = handle["n"]

    def body(*refs):
        g_refs, land_refs, send, recv = refs[:n], refs[n:2 * n], refs[2 * n], refs[2 * n + 1]
        x, y, c = _coords()
        for a in range(n):
            cp = pltpu.make_async_remote_copy(src_ref=g_refs[a].at[1 - c], dst_ref=land_refs[a], send_sem=send.at[a],
                                              recv_sem=recv.at[a], device_id=(x, y, 1 - c), device_id_type=MESH)
            cp.wait_send()
            cp.wait_recv()

    out = pl.pallas_call(
        body, name=name, in_specs=[HBM] * (2 * n) + [SEMS, SEMS] + [ANY] * len(after),
        out_shape=[pltpu.HBM(a.shape, a.dtype) for a in handle["bufs"]], out_specs=[HBM] * (2 * n),
        input_output_aliases={i: i for i in range(2 * n)}, compiler_params=pltpu.CompilerParams(**SPLIT),
    )(*handle["bufs"], handle["send"], handle["recv"], *after)
    return list(out[:n]), list(out[n:])


def _exchange_start(ss, jobs, *, name):
    n = len(ss)
    lands = [lax.empty((3,) + s.shape, s.dtype) for s in ss]

    def body(*refs):
        s_refs, land_refs, send, recv, token = refs[:n], refs[n:2 * n], refs[2 * n], refs[2 * n + 1], refs[-1]
        x, y, c = _coords()
        for j, (px, py) in enumerate(_other_chips(x, y)):
            for job in jobs:
                for row, rows in job.pieces(2 * px + py):
                    pltpu.make_async_remote_copy(
                        src_ref=s_refs[job.a].at[pl.ds(row, rows)], dst_ref=land_refs[job.a].at[j, pl.ds(row, rows)],
                        send_sem=send.at[n * j + job.a], recv_sem=recv.at[n * j + job.a], device_id=(px, py, c),
                        device_id_type=MESH).start()
        token[...] = jnp.zeros_like(token)

    thru = [pltpu.HBM(a.shape, a.dtype) for a in ss + lands]
    out = pl.pallas_call(
        body, name=name, in_specs=[HBM] * (2 * n),
        out_shape=(pltpu.SemaphoreType.DMA((3 * n,)), pltpu.SemaphoreType.DMA((3 * n,)), *thru, jax.ShapeDtypeStruct((8, LANE), F32)),
        out_specs=(SEMS, SEMS, *[HBM] * (2 * n), pl.BlockSpec(memory_space=pltpu.VMEM)),
        input_output_aliases={i: 2 + i for i in range(2 * n)}, compiler_params=pltpu.CompilerParams(**SPLIT),
    )(*[_in_hbm(a) for a in ss + lands])
    return dict(send=out[0], recv=out[1], bufs=list(out[2:2 + 2 * n]), n=n, jobs=jobs), out[-1]


def _exchange_wait(handle, after, *, name):
    n, jobs = handle["n"], handle["jobs"]
    total = [sum(rows for job in jobs if job.a == a for _, rows in job.pieces(0)) for a in range(n)]

    def body(*refs):
        s_refs, land_refs, send, recv = refs[:n], refs[n:2 * n], refs[2 * n], refs[2 * n + 1]
        x, y, c = _coords()
        for a in range(n):
            for j in range(3):
                all_rows = land_refs[a].at[0, pl.ds(0, total[a])]
                cp = pltpu.make_async_remote_copy(src_ref=all_rows, dst_ref=all_rows, send_sem=send.at[n * j + a],
                                                  recv_sem=recv.at[n * j + a], device_id=(x, y, c), device_id_type=MESH)
                cp.wait_send()
                cp.wait_recv()

    out = pl.pallas_call(
        body, name=name, in_specs=[HBM] * (2 * n) + [SEMS, SEMS] + [ANY] * len(after),
        out_shape=[pltpu.HBM(a.shape, a.dtype) for a in handle["bufs"]], out_specs=[HBM] * (2 * n),
        input_output_aliases={i: i for i in range(2 * n)}, compiler_params=pltpu.CompilerParams(**SPLIT),
    )(*handle["bufs"], handle["send"], handle["recv"], *after)
    return list(out[:n]), list(out[n:])


def _add_shard(s, land, job, sel, k, *, name):
    hw = s.shape[1]
    blk, no, ni, stride = job.blk, job.n_outer, job.n_inner, job.stride
    scal = jnp.stack([sel, job.start(k) // blk]).astype(jnp.int32)

    def body(sc_ref, own_ref, r_ref, o_ref):
        o_ref[...] = ((own_ref[...].astype(F32) + r_ref[0].astype(F32)) + r_ref[1].astype(F32)) + r_ref[2].astype(F32)

    return pl.pallas_call(
        body, name=name, out_shape=jax.ShapeDtypeStruct((2, job.rows_out, hw), F32),
        grid_spec=pltpu.PrefetchScalarGridSpec(
            num_scalar_prefetch=1, grid=(no, ni),
            in_specs=[pl.BlockSpec((blk, hw), lambda o, b, sc: (sc[1] + o * stride + b, 0)),
                      pl.BlockSpec((3, blk, hw), lambda o, b, sc: (0, sc[1] + o * stride + b, 0))],
            out_specs=pl.BlockSpec((None, blk, hw), lambda o, b, sc: (sc[0], o * ni + b, 0))),
        compiler_params=_params("parallel", "parallel"),
    )(scal, s, land)


def _join_list(fs, *, name):
    n = len(fs)

    def body(*refs):
        f_refs, send_sems, recv_sems = refs[n:2 * n], refs[2 * n], refs[2 * n + 1]
        x, y, c = _coords()
        copies = [pltpu.make_async_remote_copy(
            src_ref=f.at[c], dst_ref=f.at[c], send_sem=send_sems.at[a], recv_sem=recv_sems.at[a],
            device_id=(x, y, 1 - c), device_id_type=MESH) for a, f in enumerate(f_refs)]
        for cp in copies:
            cp.start()
        for cp in copies:
            cp.wait()

    sem = pltpu.SemaphoreType.DMA((n,))
    return pl.pallas_call(
        body, name=name, in_specs=[ANY] * n, out_specs=[ANY] * n, input_output_aliases={i: i for i in range(n)},
        out_shape=[jax.ShapeDtypeStruct(f.shape, f.dtype) for f in fs], scratch_shapes=[sem, sem],
    )(*fs)


def _all_reduce_small(vec, *, name):
    n = vec.shape[1]

    def body(v_ref, o_ref, buf, send_sems, recv_sems):
        x, y, c = _coords()
        me = 4 * x + 2 * y + c
        buf[me] = v_ref[...]
        copies = []
        for m in range(1, 8):
            peer = (x ^ ((m >> 2) & 1), y ^ ((m >> 1) & 1), c ^ (m & 1))
            copies.append(pltpu.make_async_remote_copy(
                src_ref=v_ref, dst_ref=buf.at[me], send_sem=send_sems.at[m - 1], recv_sem=recv_sems.at[m - 1],
                device_id=peer, device_id_type=MESH))
        for cp in copies:
            cp.start()
        for cp in copies:
            cp.wait()
        acc = buf[0]
        for d in range(1, 8):
            acc = acc + buf[d]
        o_ref[...] = acc

    return pl.pallas_call(
        body, name=name, out_shape=jax.ShapeDtypeStruct((1, n), F32),
        in_specs=[pl.BlockSpec(memory_space=pltpu.VMEM)], out_specs=pl.BlockSpec(memory_space=pltpu.VMEM),
        scratch_shapes=[pltpu.VMEM((8, 1, n), F32), pltpu.SemaphoreType.DMA((7,)), pltpu.SemaphoreType.DMA((7,))],
    )(vec)


def _ffn_fwd(x, n_pre, n_post, wbuf, lay, tag, deps=()):
    wg, wu, wd = ((wbuf, i * lay.dff, lay.dff) for i in range(3))
    h = _norm_fwd(x, n_pre, name=f"{tag}_norm_pre", out_dtype=BF16)
    g = _mm([(h, wg)], name=f"{tag}_gate", mode="nt", deps=deps)
    u = _mm([(h, wu)], name=f"{tag}_up", mode="nt")
    a = _swiglu_fwd(g, u, name=f"{tag}_swiglu")
    yv = _mm([(a, wd)], name=f"{tag}_down", mode="nn")
    out = _norm_fwd(yv, n_post, name=f"{tag}_norm_post", resid=x, scale=MACARON_SCALE)
    return out, (x, h, g, u, a, yv)


def _ffn_bwd(dout, saved, n_pre, n_post, wbuf, lay, tag, deps=(), after_act=None, after_dw=None):
    x, h, g, u, a, yv = saved
    dff = lay.dff
    gbuf = lax.empty((2, 3 * dff, lay.d // 2), F32)
    dy, dn_post = _norm_bwd(yv, n_post, dout, name=f"{tag}_norm_post_bwd", scale=MACARON_SCALE)
    da = _mm([(dy, (wbuf, 2 * dff, dff))], name=f"{tag}_down_dx", mode="nt", deps=deps)
    dg, du = _swiglu_bwd(da, g, u, name=f"{tag}_swiglu_bwd")
    deps = after_act(du) if after_act is not None else ()
    gbuf = _mm([(a, dy)], name=f"{tag}_down_dw", mode="tn", into=(gbuf, 2 * dff), deps=deps)
    gbuf = _mm([(dg, h)], name=f"{tag}_gate_dw", mode="tn", into=(gbuf, 0))
    gbuf = _mm([(du, h)], name=f"{tag}_up_dw", mode="tn", into=(gbuf, dff))
    deps = after_dw(gbuf)
    dh = _mm([(dg, (wbuf, 0, dff)), (du, (wbuf, dff, dff))], name=f"{tag}_up_dx", mode="nn", deps=deps)
    dx, dn_pre = _norm_bwd(x, n_pre, dh, name=f"{tag}_norm_pre_bwd", dres=dout)
    return dx, dn_pre, dn_post


def _rope_tables(positions):
    half = ROPE // 2
    inv_freq = ROPE_THETA ** (-jnp.arange(half, dtype=F32) / half)
    ang = positions.astype(F32)[:, None] * inv_freq
    cos, sin = jnp.cos(ang), jnp.sin(ang)
    z = jnp.zeros_like(cos)
    z2 = jnp.zeros((positions.shape[0], LANE - ROPE), F32)
    return (jnp.concatenate([cos, cos, z2], axis=1), jnp.concatenate([-sin, z, z2], axis=1),
            jnp.concatenate([z, sin, z2], axis=1))


def kernel(x, positions, ffn1_norm_pre, ffn1_w_gate, ffn1_w_up, ffn1_w_down, ffn1_norm_post, mix_norm_pre, w_in, mla_q_norm, mla_w_q_up, mla_kv_norm, mla_w_kv_up, mla_w_o, hgrn_lb_logits, hgrn_out_norm, hgrn_w_o, w_out, mix_norm_post, ffn2_norm_pre, ffn2_w_gate, ffn2_w_up, ffn2_w_down, ffn2_norm_post, loss_target, m_ffn1_norm_pre, m_ffn1_w_gate, m_ffn1_w_up, m_ffn1_w_down, m_ffn1_norm_post, m_mix_norm_pre, m_w_in, m_mla_q_norm, m_mla_w_q_up, m_mla_kv_norm, m_mla_w_kv_up, m_mla_w_o, m_hgrn_lb_logits, m_hgrn_out_norm, m_hgrn_w_o, m_w_out, m_mix_norm_post, m_ffn2_norm_pre, m_ffn2_w_gate, m_ffn2_w_up, m_ffn2_w_down, m_ffn2_norm_post, v_ffn1_norm_pre, v_ffn1_w_gate, v_ffn1_w_up, v_ffn1_w_down, v_ffn1_norm_post, v_mix_norm_pre, v_w_in, v_mla_q_norm, v_mla_w_q_up, v_mla_kv_norm, v_mla_w_kv_up, v_mla_w_o, v_hgrn_lb_logits, v_hgrn_out_norm, v_hgrn_w_o, v_w_out, v_mix_norm_post, v_ffn2_norm_pre, v_ffn2_w_gate, v_ffn2_w_up, v_ffn2_w_down, v_ffn2_norm_post):
    given = dict(locals())
    wts = {n: given[n] for n in ALL_WEIGHTS}
    mom = {n: given["m_" + n] for n in ALL_WEIGHTS}
    var = {n: given["v_" + n] for n in ALL_WEIGHTS}
    xin = x[0]
    target = loss_target[0]
    t, d = xin.shape
    cx, cy, cc = _coords()

    q_lora, kv_lora = mla_q_norm.shape[1], mla_kv_norm.shape[1]
    nh_mla = 4 * mla_w_kv_up.shape[2] // QGROUP
    lay = _Layout(d, 4 * ffn1_w_gate.shape[2], 4 * w_in.shape[2], q_lora, kv_lora, nh_mla)
    jobs_ffn, jobs_mix = _ffn_jobs(lay), _mix_jobs(lay)
    col_sharded = lambda n: wts[n][0].T.astype(BF16)
    row_sharded = lambda n: wts[n][0].astype(BF16)

    def pack(parts):
        a = jnp.concatenate(parts)
        return a.reshape(a.shape[0], 2, a.shape[1] // 2).transpose(1, 0, 2)

    def ffn_pack(tag):
        return [pack([col_sharded(f"{tag}_w_gate"), col_sharded(f"{tag}_w_up"), row_sharded(f"{tag}_w_down")])]

    ffn_land = lambda: [lax.empty((2, 3 * lay.dff, d // 2), BF16)]
    got1, tok = _gather_start(ffn_pack("ffn1"), ffn_land(), jobs_ffn, name="gather_ffn1")
    (w_ffn1,) = _gather_forward(_gather_wait(got1, [], name="gather_ffn1_wait"), jobs_ffn, name="gather_ffn1_forward")
    got_m, tok = _gather_start(
        [pack([col_sharded("w_in"), row_sharded("mla_w_o"), row_sharded("hgrn_w_o"), row_sharded("w_out")]),
         pack([col_sharded("mla_w_q_up"), col_sharded("mla_w_kv_up")])],
        [jnp.zeros((2, 10 * d, d // 2), BF16), jnp.zeros((2, lay.rows_narrow, q_lora // 2), BF16)], jobs_mix,
        name="gather_mix", deps=[w_ffn1])
    col_kr = q_lora + kv_lora
    hgrn_cols = [d, 2 * d, 3 * d, 4 * d]
    col_ga, col_gb = 5 * d, 6 * d
    tabs = _rope_tables(positions[0])
    scale = (HEAD + ROPE) ** -0.5

    x1, saved1 = _ffn_fwd(xin, ffn1_norm_pre, ffn1_norm_post, w_ffn1, lay, "ffn1", deps=[tok])

    wide, narrow = _gather_forward(_gather_wait(got_m, [x1], name="gather_mix_wait"), jobs_mix, name="gather_mix_forward")
    got2, tok = _gather_start(ffn_pack("ffn2"), ffn_land(), jobs_ffn, name="gather_ffn2", deps=[wide])
    w_in_v = (wide, 0, 7 * d)
    w_o_v = {n: (wide, (7 + i) * d, d) for i, n in enumerate(("mla_w_o", "hgrn_w_o", "w_out"))}
    w_q_v = (narrow, lay.off_q, nh_mla * QGROUP)
    w_kv_v = (narrow, lay.off_kv, nh_mla * QGROUP)

    h2 = _norm_fwd(x1, mix_norm_pre, name="mix_norm_pre", out_dtype=BF16)
    proj = _mm([(h2, w_in_v)], name="mix_in", mode="nt", deps=[tok])
    cqn = _norm_fwd(proj, mla_q_norm, name="mla_q_norm", out_dtype=BF16, col=0)
    ckvn = _norm_fwd(proj, mla_kv_norm, name="mla_kv_norm", out_dtype=BF16, col=q_lora)
    qp = _mm([(cqn, w_q_v)], name="mla_q_up", mode="nt")
    kvb = _mm([(ckvn, w_kv_v)], name="mla_kv_up", mode="nt", out_dtype=BF16)
    qcat = _rope(qp, tabs, name="rope_q", group=QGROUP, backward=False, out_dtype=BF16)
    krot = _rope(proj, tabs, name="rope_k", group=LANE, backward=False, out_dtype=BF16, col=col_kr, ngroup=1)
    o_mla = _attn_fwd(qcat, kvb, krot, name="mla_attention", scale=scale)
    y_a = _mm([(o_mla, w_o_v["mla_w_o"])], name="mla_out", mode="nn")

    o_raw, yb, states = _hgrn_fwd(proj, hgrn_cols, d, hgrn_lb_logits, hgrn_out_norm, name="hgrn_scan")
    y_b = _mm([(yb, w_o_v["hgrn_w_o"])], name="hgrn_out", mode="nn")

    merged = _merge_fwd(proj, col_ga, col_gb, y_a, y_b, name="mix_merge")
    y_mix = _mm([(merged, w_o_v["w_out"])], name="mix_out", mode="nn")
    x2 = _norm_fwd(y_mix, mix_norm_post, name="mix_norm_post", resid=x1, scale=1.0)

    (w_ffn2,) = _gather_forward(_gather_wait(got2, [x2], name="gather_ffn2_wait"), jobs_ffn, name="gather_ffn2_forward")
    x3, saved2 = _ffn_fwd(x2, ffn2_norm_pre, ffn2_norm_post, w_ffn2, lay, "ffn2")
    dx3, loss_local = _loss_head(x3, target, name="loss_head")

    grads, deltas, new_m, new_v = {}, {}, {}, {}
    sel = cc.astype(jnp.int32)
    sel1 = jnp.reshape(sel, (1,))
    me_chip = (2 * cx + cy).astype(jnp.int32)

    def reduce_mid(handle, after, jobs, tag):
        bufs, recvd = _swap_wait(handle, after, name=f"grad_swap_{tag}_wait")
        sums = [_add_sibling(b, r, sel1, name=f"grad_add_sibling_{tag}_{i}") for i, (b, r) in enumerate(zip(bufs, recvd))]
        return _exchange_start(sums, jobs, name=f"grad_exchange_{tag}")

    def reduce_end(handle, after, tag):
        sums, lands = _exchange_wait(handle, after, name=f"grad_exchange_{tag}_wait")
        parts = [_add_shard(sums[job.a], lands[job.a], job, sel, me_chip, name=f"grad_add_chips_{tag}_{i}")
                 for i, job in enumerate(handle["jobs"])]
        return _join_list(parts, name=f"grad_join_{tag}")

    def natural(part, lo, rows, transposed):
        g_n = part[:, lo:lo + rows]
        hw_n = g_n.shape[2]
        return g_n.transpose(0, 2, 1).reshape(2 * hw_n, rows) if transposed else g_n.transpose(1, 0, 2).reshape(rows, 2 * hw_n)

    def adam(names, deps=()):
        for i, n in enumerate(names):
            shp = wts[n].shape
            two_d = (lambda a: a[0]) if n in BIG_WEIGHTS else (lambda a: a)
            dl, nm, nv = _adamw(two_d(wts[n]), grads[n], two_d(mom[n]), two_d(var[n]), name=f"adamw_{n}",
                                deps=deps if i == 0 else ())
            grads[n] = grads[n].reshape(shp)
            deltas[n], new_m[n], new_v[n] = dl.reshape(shp), nm.reshape(shp), nv.reshape(shp)
        return [deltas[n] for n in names]

    def ffn_grads(joined, tag, deps=()):
        nff = lay.nff
        grads[f"{tag}_w_gate"] = natural(joined[0], 0, nff, True)
        grads[f"{tag}_w_up"] = natural(joined[0], nff, nff, True)
        grads[f"{tag}_w_down"] = natural(joined[0], 2 * nff, nff, False)
        return adam([f"{tag}_w_gate", f"{tag}_w_up", f"{tag}_w_down"], deps)

    swaps = {}

    def start_swap(tag):
        def hook(gbuf):
            swaps[tag], started = _swap_start([gbuf], name=f"grad_swap_{tag}")
            return [started]
        return hook

    dx2, grads["ffn2_norm_pre"], grads["ffn2_norm_post"] = _ffn_bwd(
        dx3, saved2, ffn2_norm_pre, ffn2_norm_post, w_ffn2, lay, "ffn2", after_dw=start_swap("ffn2"))

    gwide = lax.empty((2, 10 * d, d // 2), F32)
    gnarrow = lax.empty((2, lay.rows_narrow, q_lora // 2), F32)
    dy_mix, grads["mix_norm_post"] = _norm_bwd(y_mix, mix_norm_post, dx2, name="mix_norm_post_bwd")
    dmerged = _mm([(dy_mix, w_o_v["w_out"])], name="mix_out_dx", mode="nt")
    gwide = _mm([(merged, dy_mix)], name="mix_out_dw", mode="tn", into=(gwide, 9 * d))
    dga, dgb, dy_a, dy_b = _merge_bwd(dmerged, proj, col_ga, col_gb, y_a, y_b, name="mix_merge_bwd")

    do_mla = _mm([(dy_a, w_o_v["mla_w_o"])], name="mla_out_dx", mode="nt")
    gwide = _mm([(o_mla, dy_a)], name="mla_out_dw", mode="tn", into=(gwide, 7 * d))
    dqcat, dkv, dkr = _attn_bwd(qcat, kvb, krot, do_mla, name="mla_attention_bwd", scale=scale)
    exch2, tok = reduce_mid(swaps["ffn2"], [dkr], _ffn_jobs(lay), "ffn2")

    dqp = _rope(dqcat, tabs, name="rope_q_bwd", group=QGROUP, backward=True, out_dtype=BF16)
    dk_r = _rope(dkr, tabs, name="rope_k_bwd", group=LANE, backward=True, out_dtype=BF16)
    dcqn = _mm([(dqp, w_q_v)], name="mla_q_up_dx", mode="nn", deps=[tok])
    gnarrow = _mm([(dqp, cqn)], name="mla_q_up_dw", mode="tn", into=(gnarrow, lay.off_q))
    dkvb = dkv.astype(BF16)
    dckvn = _mm([(dkvb, w_kv_v)], name="mla_kv_up_dx", mode="nn")
    gnarrow = _mm([(dkvb, ckvn)], name="mla_kv_up_dw", mode="tn", into=(gnarrow, lay.off_kv))
    dc_q, grads["mla_q_norm"] = _norm_bwd(proj, mla_q_norm, dcqn, name="mla_q_norm_bwd", col=0, dx_dtype=BF16)
    dc_kv, grads["mla_kv_norm"] = _norm_bwd(proj, mla_kv_norm, dckvn, name="mla_kv_norm_bwd", col=q_lora, dx_dtype=BF16)

    dyb = _mm([(dy_b, w_o_v["hgrn_w_o"])], name="hgrn_out_dx", mode="nt")
    gwide = _mm([(yb, dy_b)], name="hgrn_out_dw", mode="tn", into=(gwide, 8 * d))
    dhq, dhf, dhi, dhg, dlb_h, dnorm_h = _hgrn_bwd(proj, hgrn_cols, d, o_raw, dyb, states, hgrn_lb_logits, hgrn_out_norm,
                                                   name="hgrn_scan_bwd")

    dproj = jnp.concatenate([dc_q, dc_kv, dk_r, jnp.zeros((t, d - col_kr - LANE), BF16), dhq, dhf, dhi, dhg, dga, dgb], axis=1)
    dh2 = _mm([(dproj, w_in_v)], name="mix_in_dx", mode="nn")
    gwide = _mm([(dproj, h2)], name="mix_in_dw", mode="tn", into=(gwide, 0))
    dx1, grads["mix_norm_pre"] = _norm_bwd(x1, mix_norm_pre, dh2, name="mix_norm_pre_bwd", dres=dx2)
    swap_m, tok = _swap_start([gwide, gnarrow], name="grad_swap_mix")
    joined2 = reduce_end(exch2, [dx1], "ffn2")

    exchanges = {}

    def mix_exchange(after):
        exchanges["mix"], started = reduce_mid(swap_m, [after], _mix_jobs(lay), "mix")
        return [started]

    dx0, grads["ffn1_norm_pre"], grads["ffn1_norm_post"] = _ffn_bwd(
        dx1, saved1, ffn1_norm_pre, ffn1_norm_post, w_ffn1, lay, "ffn1", deps=[tok], after_act=mix_exchange,
        after_dw=start_swap("ffn1"))
    exch1, tok = reduce_mid(swaps["ffn1"], [dx0], _ffn_jobs(lay), "ffn1")

    joined_m = reduce_end(exchanges["mix"], [dx0, tok], "mix")
    done = ffn_grads(joined2, "ffn2")
    grads["w_in"] = natural(jnp.concatenate([joined_m[0], joined_m[1]], axis=1), 0, lay.ncol, True)
    for i, n in enumerate(("mla_w_o", "hgrn_w_o", "w_out")):
        grads[n] = natural(joined_m[2], i * lay.r_o, lay.r_o, False)
    grads["mla_w_q_up"] = natural(joined_m[3], 0, lay.hps * (HEAD + ROPE), True)
    grads["mla_w_kv_up"] = natural(joined_m[4], 0, lay.hps * QGROUP, True)
    done += adam(["w_in", "mla_w_q_up", "mla_w_kv_up", "mla_w_o", "hgrn_w_o", "w_out"])

    dlb = dlb_h.reshape(1, -1)
    dnorm = jnp.sum(dnorm_h, axis=0)
    small = {**{n: grads[n] for n in SMALL_WEIGHTS if n not in ("hgrn_lb_logits", "hgrn_out_norm")},
             "hgrn_lb_logits": dlb, "hgrn_out_norm": dnorm}
    vec = jnp.concatenate([small[n] for n in SMALL_WEIGHTS], axis=1)
    vec = _all_reduce_small(vec, name="grad_all_reduce_small")
    off = 0
    for n in SMALL_WEIGHTS:
        w_n = small[n].shape[1]
        grads[n] = vec[:, off:off + w_n]
        off += w_n
    grads["hgrn_lb_logits"] = _lb_logits_grad(hgrn_lb_logits, grads["hgrn_lb_logits"], name="lb_logits_grad")

    done += adam(list(SMALL_WEIGHTS))
    ffn_grads(reduce_end(exch1, done, "ffn1"), "ffn1")

    loss = lax.psum(loss_local, ("x", "y", "c"))
    dx_out = dx0.reshape(x.shape)
    return (loss, dx_out, *[grads[n] for n in ALL_WEIGHTS], *[deltas[n] for n in ALL_WEIGHTS],
            *[new_m[n] for n in ALL_WEIGHTS], *[new_v[n] for n in ALL_WEIGHTS])
```

```python
import functools

import jax
import jax.numpy as jnp
from jax import lax
from jax.experimental import pallas as pl
from jax.experimental.pallas import tpu as pltpu

F32 = jnp.float32
BF16 = jnp.bfloat16
MESH = pl.DeviceIdType.MESH

NORM_EPS = 1e-6
MACARON_SCALE = 0.5
ROPE_THETA = 10000.0
HEAD = 128
ROPE = 64
QGROUP = 2 * HEAD
SUB = 16
ADAM_LR, ADAM_B1, ADAM_B2, ADAM_EPS, ADAM_WD, ADAM_STEP = 0.001, 0.9, 0.999, 1e-08, 0.01, 10

LANE = 128
VMEM_LIMIT = 48 * 1024 * 1024
MM_TILE = 1024
MM_TILE_WIDE = 1536

BIG_WEIGHTS = ("ffn1_w_gate", "ffn1_w_up", "ffn1_w_down", "w_in", "mla_w_q_up", "mla_w_kv_up",
               "mla_w_o", "hgrn_w_o", "w_out", "ffn2_w_gate", "ffn2_w_up", "ffn2_w_down")
COL_SHARDED = ("ffn1_w_gate", "ffn1_w_up", "w_in", "mla_w_q_up", "mla_w_kv_up", "ffn2_w_gate", "ffn2_w_up")
SMALL_WEIGHTS = ("ffn1_norm_pre", "ffn1_norm_post", "mix_norm_pre", "mla_q_norm", "mla_kv_norm",
                 "hgrn_lb_logits", "hgrn_out_norm", "mix_norm_post", "ffn2_norm_pre", "ffn2_norm_post")
ALL_WEIGHTS = ("ffn1_norm_pre", "ffn1_w_gate", "ffn1_w_up", "ffn1_w_down", "ffn1_norm_post", "mix_norm_pre",
               "w_in", "mla_q_norm", "mla_w_q_up", "mla_kv_norm", "mla_w_kv_up", "mla_w_o", "hgrn_lb_logits",
               "hgrn_out_norm", "hgrn_w_o", "w_out", "mix_norm_post", "ffn2_norm_pre", "ffn2_w_gate",
               "ffn2_w_up", "ffn2_w_down", "ffn2_norm_post")


def _params(*sem):
    return pltpu.CompilerParams(dimension_semantics=sem or None, vmem_limit_bytes=VMEM_LIMIT)


def _pick(n, cap, offset=0):
    if n <= cap and offset % n == 0:
        return n
    best = None
    for t in range(LANE, min(n, cap) + 1, LANE):
        if n % t == 0 and offset % t == 0:
            best = t
    assert best is not None, (n, cap, offset)
    return best


def _row_tile(n, row_bytes, budget=1 << 20):
    best = None
    for t in range(8, n + 1, 8):
        if n % t == 0 and t * row_bytes <= budget:
            best = t
    return n if best is None else best


def _sigmoid(x):
    return 1.0 / (1.0 + jnp.exp(-x))


def _silu(x):
    return x * _sigmoid(x)


def _dsilu(x):
    s = _sigmoid(x)
    return s * (1.0 + x * (1.0 - s))


def _mm(pairs, *, name, mode="nn", out_dtype=F32, into=None, deps=()):
    halves = isinstance(pairs[0][1], tuple)
    assert halves or mode == "tn"
    pairs = [(a, b if halves else (b, 0, b.shape[0])) for a, b in pairs]
    a0, (b0, b_off, b_rows) = pairs[0]
    hw = b0.shape[2] if halves else (into[0].shape[2] if into is not None else None)
    if mode == "nn":
        (m, kdim), n = a0.shape, 2 * hw
    elif mode == "nt":
        (m, kdim), n = a0.shape, b_rows
        assert kdim == 2 * hw
    else:
        (kdim, m), n = a0.shape, b0.shape[1]
    out_off = 0 if into is None else into[1]
    tm = _pick(m, MM_TILE_WIDE if mode == "tn" else MM_TILE, out_off)
    tn = hw if (mode == "nn" or into is not None) else _pick(n, MM_TILE_WIDE, b_off if mode == "nt" else 0)
    tk = hw if mode == "nt" else _pick(kdim, MM_TILE, b_off if mode == "nn" else 0)
    assert n % tn == 0 and kdim % tk == 0
    nk = kdim // tk
    npair = len(pairs)
    dims = {"nn": (((1,), (0,)), ((), ())), "nt": (((1,), (1,)), ((), ())), "tn": (((0,), (0,)), ((), ()))}[mode]

    def body(*refs):
        ins, o_ref, acc_ref = refs[:2 * npair], refs[-2], refs[-1]
        k = pl.program_id(2)

        @pl.when(k == 0)
        def _():
            acc_ref[...] = jnp.zeros_like(acc_ref)

        for p in range(npair):
            a = ins[2 * p][...].astype(BF16)
            b = ins[2 * p + 1][...].astype(BF16)
            acc_ref[...] += lax.dot_general(a, b, dims, preferred_element_type=F32)

        @pl.when(k == nk - 1)
        def _():
            o_ref[...] = acc_ref[...].astype(o_ref.dtype)

    a_spec = pl.BlockSpec((tk, tm), lambda i, j, k: (k, i)) if mode == "tn" else pl.BlockSpec((tm, tk), lambda i, j, k: (i, k))
    in_specs, flat = [], []
    for a, (b, off, _) in pairs:
        if mode == "nt":
            b_spec = pl.BlockSpec((None, tn, tk), lambda i, j, k, o=off // tn: (k, j + o, 0))
        elif mode == "nn":
            b_spec = pl.BlockSpec((None, tk, tn), lambda i, j, k, o=off // tk: (j, k + o, 0))
        else:
            b_spec = pl.BlockSpec((tk, tn), lambda i, j, k: (k, j))
        in_specs += [a_spec, b_spec]
        flat += [a, b]
    for dep in deps:
        in_specs.append(pl.BlockSpec(memory_space=pl.ANY))
        flat.append(dep)
    if into is None:
        out_shape, aliases = jax.ShapeDtypeStruct((m, n), out_dtype), {}
        out_spec = pl.BlockSpec((tm, tn), lambda i, j, k: (i, j))
    else:
        out_shape, aliases = jax.ShapeDtypeStruct(into[0].shape, into[0].dtype), {len(flat): 0}
        out_spec = pl.BlockSpec((None, tm, tn), lambda i, j, k, o=out_off // tm: (j, i + o, 0))
        in_specs.append(pl.BlockSpec(memory_space=pl.ANY))
        flat.append(into[0])
    return pl.pallas_call(
        body, name=name, grid=(m // tm, n // tn, nk),
        in_specs=in_specs,
        out_specs=out_spec,
        out_shape=out_shape, input_output_aliases=aliases,
        scratch_shapes=[pltpu.VMEM((tm, tn), F32)],
        compiler_params=_params("parallel", "parallel", "arbitrary"),
    )(*flat)


def _norm_fwd(y, w, *, name, resid=None, scale=1.0, out_dtype=F32, col=0):
    t, d = y.shape[0], w.shape[1]
    tr = _pick(t, 256)
    assert col % d == 0

    def body(*refs):
        if resid is None:
            y_ref, w_ref, o_ref = refs
        else:
            y_ref, w_ref, r_ref, o_ref = refs
        yv = y_ref[...]
        out = yv * lax.rsqrt(jnp.mean(yv * yv, axis=-1, keepdims=True) + NORM_EPS) * w_ref[...]
        if resid is not None:
            out = r_ref[...] + scale * out
        o_ref[...] = out.astype(out_dtype)

    row = pl.BlockSpec((tr, d), lambda i: (i, 0))
    wspec = pl.BlockSpec((1, d), lambda i: (0, 0))
    ins, specs = [y, w], [pl.BlockSpec((tr, d), lambda i: (i, col // d)), wspec]
    if resid is not None:
        ins.append(resid)
        specs.append(row)
    return pl.pallas_call(
        body, name=name, grid=(t // tr,), in_specs=specs, out_specs=row,
        out_shape=jax.ShapeDtypeStruct((t, d), out_dtype), compiler_params=_params("parallel"),
    )(*ins)


def _norm_bwd(x, w, dy, *, name, scale=1.0, dres=None, col=0, dx_dtype=F32):
    t, d = x.shape[0], w.shape[1]
    tr = _pick(t, 256)
    assert col % d == 0

    def body(*refs):
        if dres is None:
            x_ref, w_ref, dy_ref, dx_ref, dw_ref = refs
        else:
            x_ref, w_ref, dy_ref, dr_ref, dx_ref, dw_ref = refs

        @pl.when(pl.program_id(0) == 0)
        def _():
            dw_ref[...] = jnp.zeros_like(dw_ref)

        xv = x_ref[...]
        r = lax.rsqrt(jnp.mean(xv * xv, axis=-1, keepdims=True) + NORM_EPS)
        xhat = xv * r
        dyv = dy_ref[...].astype(F32) * scale
        dw_ref[...] += jnp.sum(dyv * xhat, axis=0, keepdims=True)
        t_ = dyv * w_ref[...]
        dx = r * (t_ - xhat * jnp.mean(t_ * xhat, axis=-1, keepdims=True))
        if dres is not None:
            dx = dx + dr_ref[...]
        dx_ref[...] = dx.astype(dx_dtype)

    row = pl.BlockSpec((tr, d), lambda i: (i, 0))
    wspec = pl.BlockSpec((1, d), lambda i: (0, 0))
    ins, specs = [x, w, dy], [pl.BlockSpec((tr, d), lambda i: (i, col // d)), wspec, row]
    if dres is not None:
        ins.append(dres)
        specs.append(row)
    return pl.pallas_call(
        body, name=name, grid=(t // tr,), in_specs=specs, out_specs=(row, wspec),
        out_shape=(jax.ShapeDtypeStruct((t, d), dx_dtype), jax.ShapeDtypeStruct((1, d), F32)),
        compiler_params=_params("arbitrary"),
    )(*ins)


def _elementwise(fn, ins, out_dtypes, *, name, width=None, cols=None):
    t = ins[0].shape[0]
    d = ins[0].shape[1] if width is None else width
    cols = [0] * len(ins) if cols is None else cols
    tc = _pick(d, 2048)
    for c in cols:
        tc = _pick(d, tc, c)
    tr = _row_tile(t, tc * 4)
    nout = len(out_dtypes)

    def body(*refs):
        outs = fn(*[r[...].astype(F32) for r in refs[:len(ins)]])
        for o_ref, o in zip(refs[len(ins):], outs):
            o_ref[...] = o.astype(o_ref.dtype)

    spec = pl.BlockSpec((tr, tc), lambda i, j: (i, j))
    in_specs = [pl.BlockSpec((tr, tc), lambda i, j, o=c // tc: (i, j + o)) for c in cols]
    return pl.pallas_call(
        body, name=name, grid=(t // tr, d // tc), in_specs=in_specs, out_specs=[spec] * nout,
        out_shape=[jax.ShapeDtypeStruct((t, d), dt) for dt in out_dtypes],
        compiler_params=_params("parallel", "parallel"),
    )(*ins)


def _swiglu_fwd(g, u, *, name):
    return _elementwise(lambda gv, uv: (_silu(gv) * uv,), [g, u], [BF16], name=name)[0]


def _swiglu_bwd(da, g, u, *, name):
    return _elementwise(lambda dav, gv, uv: (dav * uv * _dsilu(gv), dav * _silu(gv)), [da, g, u], [BF16, BF16], name=name)


def _merge_fwd(proj, col_a, col_b, ya, yb, *, name):
    return _elementwise(lambda a, b, p, q: (_sigmoid(a) * p + _sigmoid(b) * q,), [proj, proj, ya, yb], [BF16],
                        name=name, width=ya.shape[1], cols=[col_a, col_b, 0, 0])[0]


def _merge_bwd(dm, proj, col_a, col_b, ya, yb, *, name):
    def fn(dmv, a, b, p, q):
        sa, sb = _sigmoid(a), _sigmoid(b)
        return dmv * p * sa * (1.0 - sa), dmv * q * sb * (1.0 - sb), dmv * sa, dmv * sb

    return _elementwise(fn, [dm, proj, proj, ya, yb], [BF16, BF16, BF16, BF16], name=name, width=ya.shape[1],
                        cols=[0, col_a, col_b, 0, 0])


def _loss_head(xo, target, *, name):
    t, d = xo.shape
    tr = _pick(t, 256)

    def body(x_ref, t_ref, dx_ref, l_ref):
        @pl.when(pl.program_id(0) == 0)
        def _():
            l_ref[...] = jnp.zeros_like(l_ref)

        err = x_ref[...] - t_ref[...]
        dx_ref[...] = err * (1.0 / d)
        l_ref[...] += 0.5 * jnp.sum(jnp.mean(err * err, axis=-1, keepdims=True), axis=0, keepdims=True)

    row = pl.BlockSpec((tr, d), lambda i: (i, 0))
    dx, l = pl.pallas_call(
        body, name=name, grid=(t // tr,), in_specs=[row, row],
        out_specs=(row, pl.BlockSpec((1, 1), lambda i: (0, 0))),
        out_shape=(jax.ShapeDtypeStruct((t, d), F32), jax.ShapeDtypeStruct((1, 1), F32)),
        compiler_params=_params("arbitrary"),
    )(xo, target)
    return dx, l[0, 0]


def _rope(xin, tabs, *, name, group, backward, out_dtype, col=0, ngroup=None):
    t = xin.shape[0]
    ngroup = xin.shape[1] // group if ngroup is None else ngroup
    wdt = ngroup * group
    tr = _pick(t, 256)
    assert col % wdt == 0
    cos_t, nsin_t, sin_t = tabs

    def body(x_ref, c_ref, n_ref, s_ref, o_ref):
        cv, nv, sv = c_ref[...], n_ref[...], s_ref[...]
        for g in range(ngroup):
            lo, hi = g * group, (g + 1) * group
            rot = x_ref[:, hi - LANE:hi].astype(F32)
            if backward:
                out = rot * cv + pltpu.roll(rot * nv, 32, 1) + pltpu.roll(rot * sv, LANE - 32, 1)
            else:
                out = rot * cv + pltpu.roll(rot, LANE - 32, 1) * nv + pltpu.roll(rot, 32, 1) * sv
            if group > LANE:
                o_ref[:, lo:hi - LANE] = x_ref[:, lo:hi - LANE].astype(out_dtype)
            o_ref[:, hi - LANE:hi] = out.astype(out_dtype)

    xspec = pl.BlockSpec((tr, wdt), lambda i: (i, 0))
    tspec = pl.BlockSpec((tr, LANE), lambda i: (i, 0))
    return pl.pallas_call(
        body, name=name, grid=(t // tr,),
        in_specs=[pl.BlockSpec((tr, wdt), lambda i: (i, col // wdt)), tspec, tspec, tspec], out_specs=xspec,
        out_shape=jax.ShapeDtypeStruct((t, wdt), out_dtype), compiler_params=_params("parallel"),
    )(xin, cos_t, nsin_t, sin_t)


def _scores(q, kv, kr, qi, tq, scale):
    kcat = jnp.concatenate([kv[:, :HEAD], kr], axis=1)
    s = lax.dot_general(q, kcat, (((1,), (1,)), ((), ())), preferred_element_type=F32) * scale
    row = qi * tq + lax.broadcasted_iota(jnp.int32, s.shape, 0)
    col = lax.broadcasted_iota(jnp.int32, s.shape, 1)
    s = jnp.where(col <= row, s, -jnp.inf)
    p = jnp.exp(s - jnp.max(s, axis=-1, keepdims=True))
    return p / jnp.sum(p, axis=-1, keepdims=True), kcat


def _attn_fwd(qcat, kv, kr, *, name, scale):
    t = qcat.shape[0]
    nh = qcat.shape[1] // QGROUP
    tq = _pick(t, 256)

    def body(q_ref, kv_ref, kr_ref, o_ref):
        for qi in range(t // tq):
            @pl.when(pl.program_id(1) == qi)
            def _(qi=qi):
                kvv = kv_ref[0:(qi + 1) * tq, :]
                p, _ = _scores(q_ref[...], kvv, kr_ref[0:(qi + 1) * tq, :], qi, tq, scale)
                o_ref[...] = jnp.dot(p.astype(BF16), kvv[:, HEAD:], preferred_element_type=F32).astype(BF16)

    return pl.pallas_call(
        body, name=name, grid=(nh, t // tq),
        in_specs=[pl.BlockSpec((tq, QGROUP), lambda h, i: (i, h)), pl.BlockSpec((t, QGROUP), lambda h, i: (0, h)),
                  pl.BlockSpec((t, LANE), lambda h, i: (0, 0))],
        out_specs=pl.BlockSpec((tq, HEAD), lambda h, i: (i, h)),
        out_shape=jax.ShapeDtypeStruct((t, nh * HEAD), BF16), compiler_params=_params("parallel", "parallel"),
    )(qcat, kv, kr)


def _attn_bwd(qcat, kv, kr, do, *, name, scale):
    t = qcat.shape[0]
    nh = qcat.shape[1] // QGROUP
    tq = _pick(t, 256)
    nq = t // tq

    def body(q_ref, kv_ref, kr_ref, do_ref, dq_ref, dkv_ref, dkr_ref, dk_acc, dv_acc):
        h, i = pl.program_id(0), pl.program_id(1)

        @pl.when(i == 0)
        def _():
            dk_acc[...] = jnp.zeros_like(dk_acc)
            dv_acc[...] = jnp.zeros_like(dv_acc)

        @pl.when((i == 0) & (h == 0))
        def _():
            dkr_ref[...] = jnp.zeros_like(dkr_ref)

        for qi in range(nq):
            @pl.when(i == qi)
            def _(qi=qi):
                keys = slice(0, (qi + 1) * tq)
                q = q_ref[...]
                kvv = kv_ref[keys, :]
                dov = do_ref[...].astype(BF16)
                p, kcat = _scores(q, kvv, kr_ref[keys, :], qi, tq, scale)
                dp = lax.dot_general(dov, kvv[:, HEAD:], (((1,), (1,)), ((), ())), preferred_element_type=F32)
                ds = (p * (dp - jnp.sum(p * dp, axis=-1, keepdims=True)) * scale).astype(BF16)
                dq_ref[...] = jnp.dot(ds, kcat, preferred_element_type=F32)
                dk_acc[keys, :] += lax.dot_general(ds, q, (((0,), (0,)), ((), ())), preferred_element_type=F32)
                dv_acc[keys, :] += lax.dot_general(p.astype(BF16), dov, (((0,), (0,)), ((), ())), preferred_element_type=F32)

        @pl.when(i == nq - 1)
        def _():
            dk = dk_acc[...]
            dkv_ref[...] = jnp.concatenate([dk[:, :HEAD], dv_acc[...]], axis=1)
            dkr_ref[...] += dk[:, HEAD:]

    return pl.pallas_call(
        body, name=name, grid=(nh, nq),
        in_specs=[pl.BlockSpec((tq, QGROUP), lambda h, i: (i, h)), pl.BlockSpec((t, QGROUP), lambda h, i: (0, h)),
                  pl.BlockSpec((t, LANE), lambda h, i: (0, 0)), pl.BlockSpec((tq, HEAD), lambda h, i: (i, h))],
        out_specs=(pl.BlockSpec((tq, QGROUP), lambda h, i: (i, h)), pl.BlockSpec((t, QGROUP), lambda h, i: (0, h)),
                   pl.BlockSpec((t, LANE), lambda h, i: (0, 0))),
        out_shape=(jax.ShapeDtypeStruct((t, nh * QGROUP), F32), jax.ShapeDtypeStruct((t, nh * QGROUP), F32),
                   jax.ShapeDtypeStruct((t, LANE), F32)),
        scratch_shapes=[pltpu.VMEM((t, QGROUP), F32), pltpu.VMEM((t, HEAD), F32)],
        compiler_params=_params("arbitrary", "arbitrary"),
    )(qcat, kv, kr, do)


def _split3(x):
    hi = x.astype(BF16)
    r1 = x - hi.astype(F32)
    mid = r1.astype(BF16)
    lo = (r1 - mid.astype(F32)).astype(BF16)
    return hi, mid, lo


def _tri_matmul(mask, x):
    m = mask.astype(BF16)
    return sum(jnp.dot(m, part, preferred_element_type=F32) for part in _split3(x))


def _sub_cumsum(g, tb):
    row = lax.broadcasted_iota(jnp.int32, (tb, tb), 0)
    col = lax.broadcasted_iota(jnp.int32, (tb, tb), 1)
    return _tri_matmul(jnp.where((col <= row) & (col // SUB == row // SUB), 1.0, 0.0), g)


def _sub_suffix_prefix(after, before, tb):
    row = lax.broadcasted_iota(jnp.int32, (tb, tb), 0)
    col = lax.broadcasted_iota(jnp.int32, (tb, tb), 1)
    same = col // SUB == row // SUB
    return (_tri_matmul(jnp.where((col >= row) & same, 1.0, 0.0), after)
            + _tri_matmul(jnp.where((col < row) & same, 1.0, 0.0), before))


def _lower_bound(logits):
    mx = jnp.max(logits, axis=0, keepdims=True)
    e = jnp.exp(logits - mx)
    return e[0:1, :] / jnp.sum(e, axis=0, keepdims=True)


def _hgrn_fwd(proj, cols, wdt, logits, out_norm, *, name):
    t = proj.shape[0]
    nh = wdt // HEAD
    tb = _pick(t, 128)
    ns = tb // SUB

    def body(hq_ref, hf_ref, hi_ref, hg_ref, lg_ref, w_ref, o_ref, yb_ref, st_ref, s_ref, q_s, k_s, b_s):
        @pl.when(pl.program_id(1) == 0)
        def _():
            s_ref[...] = jnp.zeros_like(s_ref)

        lb = _lower_bound(lg_ref[...])
        f = lb + (1.0 - lb) * _sigmoid(hf_ref[...])
        q_s[...] = _silu(hq_ref[...])
        k_s[...] = 1.0 - f
        b_s[...] = _sub_cumsum(jnp.log(f), tb)
        rowid = lax.broadcasted_iota(jnp.int32, (SUB, HEAD), 0)

        def sub(c, st):
            rows = pl.ds(pl.multiple_of(c * SUB, SUB), SUB)
            qc, kc, bc, vc = q_s[rows, :], k_s[rows, :], b_s[rows, :], hi_ref[rows, :]
            st_ref[0, c] = st
            bl = bc[SUB - 1:SUB, :]
            oc = lax.dot_general((qc * jnp.exp(bc)).astype(BF16), st.astype(BF16), (((1,), (1,)), ((), ())),
                                 preferred_element_type=F32)
            for s in range(SUB):
                e = jnp.where(rowid >= s, jnp.exp(bc - bc[s:s + 1, :]), 0.0)
                a = jnp.sum(qc * e * kc[s:s + 1, :], axis=1, keepdims=True)
                oc = oc + a * vc[s:s + 1, :]
            o_ref[rows, :] = oc
            kd = kc * jnp.exp(bl - bc)
            return jnp.exp(bl) * st + lax.dot_general(vc.astype(BF16), kd.astype(BF16), (((0,), (0,)), ((), ())),
                                                      preferred_element_type=F32)

        s_ref[...] = lax.fori_loop(0, ns, sub, s_ref[...], unroll=True)
        o = o_ref[...]
        r = lax.rsqrt(jnp.mean(o * o, axis=-1, keepdims=True) + NORM_EPS)
        yb_ref[...] = (o * r * w_ref[...] * _silu(hg_ref[...])).astype(BF16)

    blk = pl.BlockSpec((tb, HEAD), lambda h, j: (j, h))
    return pl.pallas_call(
        body, name=name, grid=(nh, t // tb),
        in_specs=[pl.BlockSpec((tb, HEAD), lambda h, j, o=c // HEAD: (j, h + o)) for c in cols]
        + [pl.BlockSpec((2, HEAD), lambda h, j: (0, h)), pl.BlockSpec((1, HEAD), lambda h, j: (0, 0))],
        out_specs=(blk, blk, pl.BlockSpec((1, ns, HEAD, HEAD), lambda h, j: (h, j, 0, 0))),
        out_shape=(jax.ShapeDtypeStruct((t, wdt), F32), jax.ShapeDtypeStruct((t, wdt), BF16),
                   jax.ShapeDtypeStruct((nh, t // SUB, HEAD, HEAD), F32)),
        scratch_shapes=[pltpu.VMEM((HEAD, HEAD), F32)] + [pltpu.VMEM((tb, HEAD), F32)] * 3,
        compiler_params=_params("parallel", "arbitrary"),
    )(proj, proj, proj, proj, logits, out_norm)


def _hgrn_bwd(proj, cols, wdt, o_raw, dyb, states, logits, out_norm, *, name):
    t = proj.shape[0]
    nh = wdt // HEAD
    tb = _pick(t, 128)
    ns = tb // SUB
    nb = t // tb

    def body(hq_ref, hf_ref, hi_ref, hg_ref, o_ref, dy_ref, st_ref, lg_ref, w_ref,
             dhq_ref, dhf_ref, dhi_ref, dhg_ref, dlb_ref, dw_ref,
             ds_ref, q_s, k_s, b_s, do_s, dq_s, dk_s, dv_s, after_s, before_s, thru_s):
        @pl.when(pl.program_id(1) == 0)
        def _():
            ds_ref[...] = jnp.zeros_like(ds_ref)
            dlb_ref[...] = jnp.zeros_like(dlb_ref)
            dw_ref[...] = jnp.zeros_like(dw_ref)

        lb = _lower_bound(lg_ref[...])
        hqv, hgv = hq_ref[...], hg_ref[...]
        sig = _sigmoid(hf_ref[...])
        f = lb + (1.0 - lb) * sig
        q_s[...] = _silu(hqv)
        k_s[...] = 1.0 - f
        b_s[...] = _sub_cumsum(jnp.log(f), tb)

        o = o_ref[...]
        r = lax.rsqrt(jnp.mean(o * o, axis=-1, keepdims=True) + NORM_EPS)
        nrm = o * r
        w = w_ref[...]
        dy = dy_ref[...].astype(F32)
        dhg_ref[...] = (dy * nrm * w * _dsilu(hgv)).astype(BF16)
        dnw = dy * _silu(hgv)
        dw_ref[0] += jnp.sum(dnw * nrm, axis=0, keepdims=True)
        tt = dnw * w
        do_s[...] = r * (tt - nrm * jnp.mean(tt * nrm, axis=-1, keepdims=True))
        rowid = lax.broadcasted_iota(jnp.int32, (SUB, HEAD), 0)

        def sub(cc, dst):
            c = ns - 1 - cc
            rows = pl.ds(pl.multiple_of(c * SUB, SUB), SUB)
            qc, kc, bc, vc, doc = q_s[rows, :], k_s[rows, :], b_s[rows, :], hi_ref[rows, :], do_s[rows, :]
            st = st_ref[0, c]
            bl = bc[SUB - 1:SUB, :]
            eb = jnp.exp(bc)
            ekd = jnp.exp(bl - bc)
            qe, kd = qc * eb, kc * ekd
            dob, vcb = doc.astype(BF16), vc.astype(BF16)
            dq_st = jnp.dot(dob, st.astype(BF16), preferred_element_type=F32) * eb
            dk_st = jnp.dot(vcb, dst.astype(BF16), preferred_element_type=F32) * ekd
            dv = lax.dot_general(kd.astype(BF16), dst.astype(BF16), (((1,), (1,)), ((), ())), preferred_element_type=F32)
            dq_in = jnp.zeros_like(qc)
            dk_in = jnp.zeros_like(qc)
            for s in range(SUB):
                e = jnp.where(rowid >= s, jnp.exp(bc - bc[s:s + 1, :]), 0.0)
                ek = e * kc[s:s + 1, :]
                a = jnp.sum(qc * ek, axis=1, keepdims=True)
                da = jnp.sum(doc * vc[s:s + 1, :], axis=1, keepdims=True)
                dq_in = dq_in + da * ek
                dk_in = dk_in + jnp.where(rowid == s, jnp.sum(da * e * qc, axis=0, keepdims=True), 0.0)
                dv = dv + jnp.where(rowid == s, jnp.sum(a * doc, axis=0, keepdims=True), 0.0)
            ebl = jnp.exp(bl)
            dq_s[rows, :] = dq_st + dq_in
            dk_s[rows, :] = dk_st + dk_in
            dv_s[rows, :] = dv
            after_s[rows, :] = qc * (dq_st + dq_in) - kc * dk_in
            before_s[rows, :] = kc * dk_st
            thru_s[rows, :] = jnp.broadcast_to(ebl * jnp.sum(st * dst, axis=0, keepdims=True), (SUB, HEAD))
            return ebl * dst + lax.dot_general(dob, qe.astype(BF16), (((0,), (0,)), ((), ())), preferred_element_type=F32)

        ds_ref[...] = lax.fori_loop(0, ns, sub, ds_ref[...], unroll=True)
        dg = _sub_suffix_prefix(after_s[...], before_s[...], tb) + thru_s[...]
        dhq_ref[...] = (dq_s[...] * _dsilu(hqv)).astype(BF16)
        dft = dg / f - dk_s[...]
        dhf_ref[...] = (dft * (1.0 - lb) * sig * (1.0 - sig)).astype(BF16)
        dlb_ref[0] += jnp.sum(dft * (1.0 - sig), axis=0, keepdims=True)
        dhi_ref[...] = dv_s[...].astype(BF16)

    blk = pl.BlockSpec((tb, HEAD), lambda h, j: (nb - 1 - j, h))
    vec = pl.BlockSpec((1, 1, HEAD), lambda h, j: (h, 0, 0))
    tok = jax.ShapeDtypeStruct((t, wdt), BF16)
    per_head = jax.ShapeDtypeStruct((nh, 1, HEAD), F32)
    return pl.pallas_call(
        body, name=name, grid=(nh, nb),
        in_specs=[pl.BlockSpec((tb, HEAD), lambda h, j, o=c // HEAD: (nb - 1 - j, h + o)) for c in cols]
        + [blk, blk] + [pl.BlockSpec((1, ns, HEAD, HEAD), lambda h, j: (h, nb - 1 - j, 0, 0)),
                              pl.BlockSpec((2, HEAD), lambda h, j: (0, h)), pl.BlockSpec((1, HEAD), lambda h, j: (0, 0))],
        out_specs=(blk, blk, blk, blk, vec, vec),
        out_shape=(tok, tok, tok, tok, per_head, per_head),
        scratch_shapes=[pltpu.VMEM((HEAD, HEAD), F32)] + [pltpu.VMEM((tb, HEAD), F32)] * 10,
        compiler_params=_params("arbitrary", "arbitrary"),
    )(proj, proj, proj, proj, o_raw, dyb, states, logits, out_norm)


def _lb_logits_grad(logits, dlb, *, name):
    def body(lg_ref, d_ref, o_ref):
        lg = lg_ref[...]
        e = jnp.exp(lg - jnp.max(lg, axis=0, keepdims=True))
        p = e / jnp.sum(e, axis=0, keepdims=True)
        d = d_ref[...]
        rowid = lax.broadcasted_iota(jnp.int32, lg.shape, 0)
        dp = jnp.where(rowid == 0, d, 0.0)
        o_ref[...] = p * (dp - jnp.sum(p * dp, axis=0, keepdims=True))

    return pl.pallas_call(body, name=name, out_shape=jax.ShapeDtypeStruct(logits.shape, F32))(logits, dlb)


def _adamw(w, g, m, v, *, name, deps=()):
    r, c = w.shape
    tc = _pick(c, 2048) if c % LANE == 0 else c
    tr = _row_tile(r, tc * 4)

    def body(w_ref, g_ref, m_ref, v_ref, *rest):
        d_ref, nm_ref, nv_ref = rest[-3:]
        gv = g_ref[...]
        nm = ADAM_B1 * m_ref[...] + (1.0 - ADAM_B1) * gv
        nv = ADAM_B2 * v_ref[...] + (1.0 - ADAM_B2) * (gv * gv)
        m_hat = nm / (1.0 - ADAM_B1 ** ADAM_STEP)
        v_hat = nv / (1.0 - ADAM_B2 ** ADAM_STEP)
        d_ref[...] = -ADAM_LR * (m_hat / (jnp.sqrt(v_hat) + ADAM_EPS) + ADAM_WD * w_ref[...])
        nm_ref[...] = nm
        nv_ref[...] = nv

    spec = pl.BlockSpec((tr, tc), lambda i, j: (i, j))
    shp = jax.ShapeDtypeStruct((r, c), F32)
    return pl.pallas_call(
        body, name=name, grid=(r // tr, c // tc), in_specs=[spec] * 4 + [ANY] * len(deps), out_specs=[spec] * 3,
        out_shape=[shp, shp, shp], compiler_params=_params("parallel", "parallel"),
    )(w, g, m, v, *deps)


def _coords():
    return lax.axis_index("x"), lax.axis_index("y"), lax.axis_index("c")


def _other_chips(x, y):
    return [(1 - x, y), (x, 1 - y), (1 - x, 1 - y)]


ANY = pl.BlockSpec(memory_space=pl.ANY)


class _Layout:
    def __init__(self, d, dff, in_cols, q_lora, kv_lora, nh):
        assert q_lora == kv_lora and nh % 4 == 0 and dff % (4 * LANE) == 0 and in_cols % 4 == 0 and d % 4 == 0
        self.d, self.dff, self.q_lora, self.nh = d, dff, q_lora, nh
        self.head = q_lora + kv_lora + ROPE
        self.pad = d - self.head
        self.nff, self.ncol, self.r_o, self.hps = dff // 4, in_cols // 4, d // 4, nh // 4
        assert self.head <= self.ncol
        self.off_q, self.off_kv, self.rows_narrow = 0, nh * QGROUP, 2 * nh * QGROUP


HBM = pl.BlockSpec(memory_space=pltpu.HBM)
SEMS = pl.BlockSpec(memory_space=pltpu.SEMAPHORE)
SPLIT = dict(has_side_effects=pltpu.SideEffectType.DATAFLOW_SIDE_EFFECTING)


def _in_hbm(a):
    return pltpu.with_memory_space_constraint(a, pltpu.HBM)


def _shard_rows(jobs, k):
    out, lrow = [], [0, 0]
    for job in jobs:
        for row, rows in job.pieces(k):
            out.append((job.a, lrow[job.a], row, rows))
            lrow[job.a] += rows
    return out


def _shard_total(jobs, a):
    return sum(rows for b, _, _, rows in _shard_rows(jobs, 0) if b == a)


def _gather_start(packs, lands, jobs, *, name, deps=()):
    n = len(packs)

    def body(*refs):
        p_refs, l_refs, send, recv, token = refs[:n], refs[n:2 * n], refs[-2 * n - 3], refs[-2 * n - 2], refs[-1]
        x, y, c = _coords()
        for a, lrow, row, rows in _shard_rows(jobs, 2 * x + y):
            pltpu.make_async_remote_copy(
                src_ref=p_refs[a].at[:, pl.ds(lrow, rows)], dst_ref=l_refs[a].at[:, pl.ds(row, rows)],
                send_sem=send.at[4 * a + 3], recv_sem=recv.at[4 * a + 3], device_id=(x, y, 1 - c), device_id_type=MESH).start()
            for j, (px, py) in enumerate(_other_chips(x, y)):
                pltpu.make_async_remote_copy(
                    src_ref=p_refs[a].at[c, pl.ds(lrow, rows)], dst_ref=l_refs[a].at[c, pl.ds(row, rows)],
                    send_sem=send.at[4 * a + j], recv_sem=recv.at[4 * a + j], device_id=(px, py, c), device_id_type=MESH).start()
        token[...] = jnp.zeros_like(token)

    thru = [pltpu.HBM(a.shape, a.dtype) for a in packs + lands]
    out = pl.pallas_call(
        body, name=name, in_specs=[HBM] * (2 * n) + [ANY] * len(deps),
        out_shape=(pltpu.SemaphoreType.DMA((4 * n,)), pltpu.SemaphoreType.DMA((4 * n,)), *thru, jax.ShapeDtypeStruct((8, LANE), F32)),
        out_specs=(SEMS, SEMS, *[HBM] * (2 * n), pl.BlockSpec(memory_space=pltpu.VMEM)),
        input_output_aliases={i: 2 + i for i in range(2 * n)}, compiler_params=pltpu.CompilerParams(**SPLIT),
    )(*[_in_hbm(a) for a in packs + lands], *deps)
    return dict(send=out[0], recv=out[1], bufs=list(out[2:2 + 2 * n]), n=n, jobs=jobs), out[-1]


def _gather_wait(handle, after, *, name):
    n, jobs = handle["n"], handle["jobs"]

    def body(*refs):
        l_refs, send, recv, token = refs[n:2 * n], refs[2 * n], refs[2 * n + 1], refs[-1]
        token[...] = jnp.zeros_like(token)
        x, y, c = _coords()
        for a in range(n):
            total = _shard_total(jobs, a)
            for j, like in enumerate([l_refs[a].at[0, pl.ds(0, total)]] * 3 + [l_refs[a].at[:, pl.ds(0, total)]]):
                cp = pltpu.make_async_remote_copy(src_ref=like, dst_ref=like, send_sem=send.at[4 * a + j],
                                                  recv_sem=recv.at[4 * a + j], device_id=(x, y, c), device_id_type=MESH)
                cp.wait_send()
                cp.wait_recv()

    out = pl.pallas_call(
        body, name=name, in_specs=[HBM] * (2 * n) + [SEMS, SEMS] + [ANY] * len(after),
        out_shape=[pltpu.HBM(a.shape, a.dtype) for a in handle["bufs"]] + [jax.ShapeDtypeStruct((8, LANE), F32)],
        out_specs=[HBM] * (2 * n) + [pl.BlockSpec(memory_space=pltpu.VMEM)],
        input_output_aliases={i: i for i in range(2 * n)}, compiler_params=pltpu.CompilerParams(**SPLIT),
    )(*handle["bufs"], handle["send"], handle["recv"], *after)
    return list(out[n:2 * n]), out[-1]


def _gather_forward(lands, jobs, *, name, deps=()):
    n = len(lands)

    def body(*refs):
        l_refs, send, recv = refs[n + len(deps):2 * n + len(deps)], refs[-2], refs[-1]
        x, y, c = _coords()
        for j, (px, py) in enumerate(_other_chips(x, y)):
            for a, _, row, rows in _shard_rows(jobs, 2 * px + py):
                blk = l_refs[a].at[c, pl.ds(row, rows)]
                pltpu.make_async_remote_copy(src_ref=blk, dst_ref=blk, send_sem=send.at[3 * a + j], recv_sem=recv.at[3 * a + j],
                                             device_id=(x, y, 1 - c), device_id_type=MESH).start()
        for a in range(n):
            like = l_refs[a].at[0, pl.ds(0, _shard_total(jobs, a))]
            for j in range(3):
                cp = pltpu.make_async_remote_copy(src_ref=like, dst_ref=like, send_sem=send.at[3 * a + j],
                                                  recv_sem=recv.at[3 * a + j], device_id=(x, y, c), device_id_type=MESH)
                cp.wait_send()
                cp.wait_recv()

    sem = pltpu.SemaphoreType.DMA((3 * n,))
    return pl.pallas_call(
        body, name=name, in_specs=[ANY] * (n + len(deps)), out_specs=[ANY] * n, input_output_aliases={i: i for i in range(n)},
        out_shape=[jax.ShapeDtypeStruct(a.shape, a.dtype) for a in lands], scratch_shapes=[sem, sem],
    )(*lands, *deps)


def _add_sibling(g, recv, sel, *, name):
    rows, hw = recv.shape
    tr = _row_tile(rows, hw * 4)

    def body(sel_ref, g_ref, r_ref, o_ref):
        o_ref[...] = (g_ref[...] + r_ref[...]).astype(BF16)

    return pl.pallas_call(
        body, name=name, out_shape=jax.ShapeDtypeStruct((rows, hw), BF16),
        grid_spec=pltpu.PrefetchScalarGridSpec(
            num_scalar_prefetch=1, grid=(rows // tr,),
            in_specs=[pl.BlockSpec((None, tr, hw), lambda i, s: (s[0], i, 0)), pl.BlockSpec((tr, hw), lambda i, s: (i, 0))],
            out_specs=pl.BlockSpec((tr, hw), lambda i, s: (i, 0))),
        compiler_params=_params("parallel"),
    )(sel, g, recv)


class _Job:
    def __init__(self, a, blk, n_outer, n_inner, stride, start):
        self.a, self.blk, self.n_outer, self.n_inner, self.stride, self.start = a, blk, n_outer, n_inner, stride, start
        self.rows_out = n_outer * n_inner * blk

    def pieces(self, k):
        return [(self.start(k) + o * self.stride * self.blk, self.n_inner * self.blk) for o in range(self.n_outer)]


def _block_rows(rows, cap, *also):
    best = None
    for b in range(16, min(rows, cap) + 1, 16):
        if rows % b == 0 and all(v % b == 0 for v in also):
            best = b
    assert best is not None, (rows, also)
    return best


def _ffn_jobs(lay):
    b = _block_rows(lay.nff, 704, lay.dff)
    return [_Job(0, b, 3, lay.nff // b, lay.dff // b, lambda k: lay.nff * k)]


def _mix_jobs(lay):
    d, ncol, head, pad = lay.d, lay.ncol, lay.head, lay.pad
    first = lambda k, a, b: jnp.where(k == 0, a, b) if not isinstance(k, int) else (a if k == 0 else b)
    ba = _block_rows(head, 704, *[ncol * k + pad for k in (1, 2, 3)])
    bb = _block_rows(ncol - head, 704, *[ncol * k + d for k in (0, 1, 2, 3)])
    bo = _block_rows(lay.r_o, 704, d)
    bq = _block_rows(HEAD + ROPE, 704, QGROUP)
    bk = _block_rows(lay.hps * QGROUP, 704, lay.off_kv)
    return [_Job(0, ba, 1, head // ba, 0, lambda k: first(k, 0, ncol * k + pad)),
            _Job(0, bb, 1, (ncol - head) // bb, 0, lambda k: ncol * k + d),
            _Job(0, bo, 3, lay.r_o // bo, d // bo, lambda k: 7 * d + lay.r_o * k),
            _Job(1, bq, lay.hps, (HEAD + ROPE) // bq, QGROUP // bq, lambda k: QGROUP * lay.hps * k),
            _Job(1, bk, 1, lay.hps * QGROUP // bk, 0, lambda k: lay.off_kv + lay.hps * QGROUP * k)]


def _swap_start(gs, *, name):
    n = len(gs)
    lands = [lax.empty(g.shape[1:], g.dtype) for g in gs]

    def body(*refs):
        g_refs, land_refs, send, recv, token = refs[:n], refs[n:2 * n], refs[2 * n], refs[2 * n + 1], refs[-1]
        x, y, c = _coords()
        for a in range(n):
            pltpu.make_async_remote_copy(src_ref=g_refs[a].at[1 - c], dst_ref=land_refs[a], send_sem=send.at[a],
                                         recv_sem=recv.at[a], device_id=(x, y, 1 - c), device_id_type=MESH).start()
        token[...] = jnp.zeros_like(token)

    thru = [pltpu.HBM(a.shape, a.dtype) for a in gs + lands]
    out = pl.pallas_call(
        body, name=name, in_specs=[HBM] * (2 * n),
        out_shape=(pltpu.SemaphoreType.DMA((n,)), pltpu.SemaphoreType.DMA((n,)), *thru, jax.ShapeDtypeStruct((8, LANE), F32)),
        out_specs=(SEMS, SEMS, *[HBM] * (2 * n), pl.BlockSpec(memory_space=pltpu.VMEM)),
        input_output_aliases={i: 2 + i for i in range(2 * n)}, compiler_params=pltpu.CompilerParams(**SPLIT),
    )(*[_in_hbm(a) for a in gs + lands])
    return dict(send=out[0], recv=out[1], bufs=list(out[2:2 + 2 * n]), n=n), out[-1]


def _swap_wait(handle, after, *, name):
    n = handle["n"]

    def body(*refs):
        g_refs, land_refs, send, recv = refs[:n], refs[n:2 * n], refs[2 * n], refs[2 * n + 1]
        x, y, c = _coords()
        for a in range(n):
            cp = pltpu.make_async_remote_copy(src_ref=g_refs[a].at[1 - c], dst_ref=land_refs[a], send_sem=send.at[a],
                                              recv_sem=recv.at[a], device_id=(x, y, 1 - c), device_id_type=MESH)
            cp.wait_send()
            cp.wait_recv()

    out = pl.pallas_call(
        body, name=name, in_specs=[HBM] * (2 * n) + [SEMS, SEMS] + [ANY] * len(after),
        out_shape=[pltpu.HBM(a.shape, a.dtype) for a in handle["bufs"]], out_specs=[HBM] * (2 * n),
        input_output_aliases={i: i for i in range(2 * n)}, compiler_params=pltpu.CompilerParams(**SPLIT),
    )(*handle["bufs"], handle["send"], handle["recv"], *after)
    return list(out[:n]), list(out[n:])


def _exchange_start(ss, jobs, *, name):
    n = len(ss)
    lands = [lax.empty((3,) + s.shape, s.dtype) for s in ss]

    def body(*refs):
        s_refs, land_refs, send, recv, token = refs[:n], refs[n:2 * n], refs[2 * n], refs[2 * n + 1], refs[-1]
        x, y, c = _coords()
        for j, (px, py) in enumerate(_other_chips(x, y)):
            for job in jobs:
                for row, rows in job.pieces(2 * px + py):
                    pltpu.make_async_remote_copy(
                        src_ref=s_refs[job.a].at[pl.ds(row, rows)], dst_ref=land_refs[job.a].at[j, pl.ds(row, rows)],
                        send_sem=send.at[n * j + job.a], recv_sem=recv.at[n * j + job.a], device_id=(px, py, c),
                        device_id_type=MESH).start()
        token[...] = jnp.zeros_like(token)

    thru = [pltpu.HBM(a.shape, a.dtype) for a in ss + lands]
    out = pl.pallas_call(
        body, name=name, in_specs=[HBM] * (2 * n),
        out_shape=(pltpu.SemaphoreType.DMA((3 * n,)), pltpu.SemaphoreType.DMA((3 * n,)), *thru, jax.ShapeDtypeStruct((8, LANE), F32)),
        out_specs=(SEMS, SEMS, *[HBM] * (2 * n), pl.BlockSpec(memory_space=pltpu.VMEM)),
        input_output_aliases={i: 2 + i for i in range(2 * n)}, compiler_params=pltpu.CompilerParams(**SPLIT),
    )(*[_in_hbm(a) for a in ss + lands])
    return dict(send=out[0], recv=out[1], bufs=list(out[2:2 + 2 * n]), n=n, jobs=jobs), out[-1]


def _exchange_wait(handle, after, *, name):
    n, jobs = handle["n"], handle["jobs"]
    total = [sum(rows for job in jobs if job.a == a for _, rows in job.pieces(0)) for a in range(n)]

    def body(*refs):
        s_refs, land_refs, send, recv = refs[:n], refs[n:2 * n], refs[2 * n], refs[2 * n + 1]
        x, y, c = _coords()
        for a in range(n):
            for j in range(3):
                all_rows = land_refs[a].at[0, pl.ds(0, total[a])]
                cp = pltpu.make_async_remote_copy(src_ref=all_rows, dst_ref=all_rows, send_sem=send.at[n * j + a],
                                                  recv_sem=recv.at[n * j + a], device_id=(x, y, c), device_id_type=MESH)
                cp.wait_send()
                cp.wait_recv()

    out = pl.pallas_call(
        body, name=name, in_specs=[HBM] * (2 * n) + [SEMS, SEMS] + [ANY] * len(after),
        out_shape=[pltpu.HBM(a.shape, a.dtype) for a in handle["bufs"]], out_specs=[HBM] * (2 * n),
        input_output_aliases={i: i for i in range(2 * n)}, compiler_params=pltpu.CompilerParams(**SPLIT),
    )(*handle["bufs"], handle["send"], handle["recv"], *after)
    return list(out[:n]), list(out[n:])


def _add_shard(s, land, job, sel, k, *, name):
    hw = s.shape[1]
    blk, no, ni, stride = job.blk, job.n_outer, job.n_inner, job.stride
    scal = jnp.stack([sel, job.start(k) // blk]).astype(jnp.int32)

    def body(sc_ref, own_ref, r_ref, o_ref):
        o_ref[...] = ((own_ref[...].astype(F32) + r_ref[0].astype(F32)) + r_ref[1].astype(F32)) + r_ref[2].astype(F32)

    return pl.pallas_call(
        body, name=name, out_shape=jax.ShapeDtypeStruct((2, job.rows_out, hw), F32),
        grid_spec=pltpu.PrefetchScalarGridSpec(
            num_scalar_prefetch=1, grid=(no, ni),
            in_specs=[pl.BlockSpec((blk, hw), lambda o, b, sc: (sc[1] + o * stride + b, 0)),
                      pl.BlockSpec((3, blk, hw), lambda o, b, sc: (0, sc[1] + o * stride + b, 0))],
            out_specs=pl.BlockSpec((None, blk, hw), lambda o, b, sc: (sc[0], o * ni + b, 0))),
        compiler_params=_params("parallel", "parallel"),
    )(scal, s, land)


def _join_list(fs, *, name):
    n = len(fs)

    def body(*refs):
        f_refs, send_sems, recv_sems = refs[n:2 * n], refs[2 * n], refs[2 * n + 1]
        x, y, c = _coords()
        copies = [pltpu.make_async_remote_copy(
            src_ref=f.at[c], dst_ref=f.at[c], send_sem=send_sems.at[a], recv_sem=recv_sems.at[a],
            device_id=(x, y, 1 - c), device_id_type=MESH) for a, f in enumerate(f_refs)]
        for cp in copies:
            cp.start()
        for cp in copies:
            cp.wait()

    sem = pltpu.SemaphoreType.DMA((n,))
    return pl.pallas_call(
        body, name=name, in_specs=[ANY] * n, out_specs=[ANY] * n, input_output_aliases={i: i for i in range(n)},
        out_shape=[jax.ShapeDtypeStruct(f.shape, f.dtype) for f in fs], scratch_shapes=[sem, sem],
    )(*fs)


def _all_reduce_small(vec, *, name):
    n = vec.shape[1]

    def body(v_ref, o_ref, buf, send_sems, recv_sems):
        x, y, c = _coords()
        me = 4 * x + 2 * y + c
        buf[me] = v_ref[...]
        copies = []
        for m in range(1, 8):
            peer = (x ^ ((m >> 2) & 1), y ^ ((m >> 1) & 1), c ^ (m & 1))
            copies.append(pltpu.make_async_remote_copy(
                src_ref=v_ref, dst_ref=buf.at[me], send_sem=send_sems.at[m - 1], recv_sem=recv_sems.at[m - 1],
                device_id=peer, device_id_type=MESH))
        for cp in copies:
            cp.start()
        for cp in copies:
            cp.wait()
        acc = buf[0]
        for d in range(1, 8):
            acc = acc + buf[d]
        o_ref[...] = acc

    return pl.pallas_call(
        body, name=name, out_shape=jax.ShapeDtypeStruct((1, n), F32),
        in_specs=[pl.BlockSpec(memory_space=pltpu.VMEM)], out_specs=pl.BlockSpec(memory_space=pltpu.VMEM),
        scratch_shapes=[pltpu.VMEM((8, 1, n), F32), pltpu.SemaphoreType.DMA((7,)), pltpu.SemaphoreType.DMA((7,))],
    )(vec)


def _ffn_fwd(x, n_pre, n_post, wbuf, lay, tag, deps=()):
    wg, wu, wd = ((wbuf, i * lay.dff, lay.dff) for i in range(3))
    h = _norm_fwd(x, n_pre, name=f"{tag}_norm_pre", out_dtype=BF16)
    g = _mm([(h, wg)], name=f"{tag}_gate", mode="nt", deps=deps)
    u = _mm([(h, wu)], name=f"{tag}_up", mode="nt")
    a = _swiglu_fwd(g, u, name=f"{tag}_swiglu")
    yv = _mm([(a, wd)], name=f"{tag}_down", mode="nn")
    out = _norm_fwd(yv, n_post, name=f"{tag}_norm_post", resid=x, scale=MACARON_SCALE)
    return out, (x, h, g, u, a, yv)


def _ffn_bwd(dout, saved, n_pre, n_post, wbuf, lay, tag, deps=(), after_act=None, after_dw=None):
    x, h, g, u, a, yv = saved
    dff = lay.dff
    gbuf = lax.empty((2, 3 * dff, lay.d // 2), F32)
    dy, dn_post = _norm_bwd(yv, n_post, dout, name=f"{tag}_norm_post_bwd", scale=MACARON_SCALE)
    da = _mm([(dy, (wbuf, 2 * dff, dff))], name=f"{tag}_down_dx", mode="nt", deps=deps)
    dg, du = _swiglu_bwd(da, g, u, name=f"{tag}_swiglu_bwd")
    deps = after_act(du) if after_act is not None else ()
    gbuf = _mm([(a, dy)], name=f"{tag}_down_dw", mode="tn", into=(gbuf, 2 * dff), deps=deps)
    gbuf = _mm([(dg, h)], name=f"{tag}_gate_dw", mode="tn", into=(gbuf, 0))
    gbuf = _mm([(du, h)], name=f"{tag}_up_dw", mode="tn", into=(gbuf, dff))
    deps = after_dw(gbuf)
    dh = _mm([(dg, (wbuf, 0, dff)), (du, (wbuf, dff, dff))], name=f"{tag}_up_dx", mode="nn", deps=deps)
    dx, dn_pre = _norm_bwd(x, n_pre, dh, name=f"{tag}_norm_pre_bwd", dres=dout)
    return dx, dn_pre, dn_post


def _rope_tables(positions):
    half = ROPE // 2
    inv_freq = ROPE_THETA ** (-jnp.arange(half, dtype=F32) / half)
    ang = positions.astype(F32)[:, None] * inv_freq
    cos, sin = jnp.cos(ang), jnp.sin(ang)
    z = jnp.zeros_like(cos)
    z2 = jnp.zeros((positions.shape[0], LANE - ROPE), F32)
    return (jnp.concatenate([cos, cos, z2], axis=1), jnp.concatenate([-sin, z, z2], axis=1),
            jnp.concatenate([z, sin, z2], axis=1))


def kernel(x, positions, ffn1_norm_pre, ffn1_w_gate, ffn1_w_up, ffn1_w_down, ffn1_norm_post, mix_norm_pre, w_in, mla_q_norm, mla_w_q_up, mla_kv_norm, mla_w_kv_up, mla_w_o, hgrn_lb_logits, hgrn_out_norm, hgrn_w_o, w_out, mix_norm_post, ffn2_norm_pre, ffn2_w_gate, ffn2_w_up, ffn2_w_down, ffn2_norm_post, loss_target, m_ffn1_norm_pre, m_ffn1_w_gate, m_ffn1_w_up, m_ffn1_w_down, m_ffn1_norm_post, m_mix_norm_pre, m_w_in, m_mla_q_norm, m_mla_w_q_up, m_mla_kv_norm, m_mla_w_kv_up, m_mla_w_o, m_hgrn_lb_logits, m_hgrn_out_norm, m_hgrn_w_o, m_w_out, m_mix_norm_post, m_ffn2_norm_pre, m_ffn2_w_gate, m_ffn2_w_up, m_ffn2_w_down, m_ffn2_norm_post, v_ffn1_norm_pre, v_ffn1_w_gate, v_ffn1_w_up, v_ffn1_w_down, v_ffn1_norm_post, v_mix_norm_pre, v_w_in, v_mla_q_norm, v_mla_w_q_up, v_mla_kv_norm, v_mla_w_kv_up, v_mla_w_o, v_hgrn_lb_logits, v_hgrn_out_norm, v_hgrn_w_o, v_w_out, v_mix_norm_post, v_ffn2_norm_pre, v_ffn2_w_gate, v_ffn2_w_up, v_ffn2_w_down, v_ffn2_norm_post):
    given = dict(locals())
    wts = {n: given[n] for n in ALL_WEIGHTS}
    mom = {n: given["m_" + n] for n in ALL_WEIGHTS}
    var = {n: given["v_" + n] for n in ALL_WEIGHTS}
    xin = x[0]
    target = loss_target[0]
    t, d = xin.shape
    cx, cy, cc = _coords()

    q_lora, kv_lora = mla_q_norm.shape[1], mla_kv_norm.shape[1]
    nh_mla = 4 * mla_w_kv_up.shape[2] // QGROUP
    lay = _Layout(d, 4 * ffn1_w_gate.shape[2], 4 * w_in.shape[2], q_lora, kv_lora, nh_mla)
    jobs_ffn, jobs_mix = _ffn_jobs(lay), _mix_jobs(lay)
    col_sharded = lambda n: wts[n][0].T.astype(BF16)
    row_sharded = lambda n: wts[n][0].astype(BF16)

    def pack(parts):
        a = jnp.concatenate(parts)
        return a.reshape(a.shape[0], 2, a.shape[1] // 2).transpose(1, 0, 2)

    def ffn_pack(tag):
        return [pack([col_sharded(f"{tag}_w_gate"), col_sharded(f"{tag}_w_up"), row_sharded(f"{tag}_w_down")])]

    ffn_land = lambda: [lax.empty((2, 3 * lay.dff, d // 2), BF16)]
    got1, _ = _gather_start(ffn_pack("ffn1"), ffn_land(), jobs_ffn, name="gather_ffn1")
    arrived, tok = _gather_wait(got1, [], name="gather_ffn1_wait")
    got_m, tok = _gather_start(
        [pack([col_sharded("w_in"), row_sharded("mla_w_o"), row_sharded("hgrn_w_o"), row_sharded("w_out")]),
         pack([col_sharded("mla_w_q_up"), col_sharded("mla_w_kv_up")])],
        [jnp.zeros((2, 10 * d, d // 2), BF16), jnp.zeros((2, lay.rows_narrow, q_lora // 2), BF16)], jobs_mix,
        name="gather_mix", deps=[tok])
    (w_ffn1,) = _gather_forward(arrived, jobs_ffn, name="gather_ffn1_forward", deps=[tok])
    col_kr = q_lora + kv_lora
    hgrn_cols = [d, 2 * d, 3 * d, 4 * d]
    col_ga, col_gb = 5 * d, 6 * d
    tabs = _rope_tables(positions[0])
    scale = (HEAD + ROPE) ** -0.5

    x1, saved1 = _ffn_fwd(xin, ffn1_norm_pre, ffn1_norm_post, w_ffn1, lay, "ffn1")

    arrived, tok = _gather_wait(got_m, [x1], name="gather_mix_wait")
    got2, tok = _gather_start(ffn_pack("ffn2"), ffn_land(), jobs_ffn, name="gather_ffn2", deps=[tok])
    wide, narrow = _gather_forward(arrived, jobs_mix, name="gather_mix_forward", deps=[tok])
    w_in_v = (wide, 0, 7 * d)
    w_o_v = {n: (wide, (7 + i) * d, d) for i, n in enumerate(("mla_w_o", "hgrn_w_o", "w_out"))}
    w_q_v = (narrow, lay.off_q, nh_mla * QGROUP)
    w_kv_v = (narrow, lay.off_kv, nh_mla * QGROUP)

    h2 = _norm_fwd(x1, mix_norm_pre, name="mix_norm_pre", out_dtype=BF16)
    proj = _mm([(h2, w_in_v)], name="mix_in", mode="nt", deps=[tok])
    cqn = _norm_fwd(proj, mla_q_norm, name="mla_q_norm", out_dtype=BF16, col=0)
    ckvn = _norm_fwd(proj, mla_kv_norm, name="mla_kv_norm", out_dtype=BF16, col=q_lora)
    qp = _mm([(cqn, w_q_v)], name="mla_q_up", mode="nt")
    kvb = _mm([(ckvn, w_kv_v)], name="mla_kv_up", mode="nt", out_dtype=BF16)
    qcat = _rope(qp, tabs, name="rope_q", group=QGROUP, backward=False, out_dtype=BF16)
    krot = _rope(proj, tabs, name="rope_k", group=LANE, backward=False, out_dtype=BF16, col=col_kr, ngroup=1)
    o_mla = _attn_fwd(qcat, kvb, krot, name="mla_attention", scale=scale)
    y_a = _mm([(o_mla, w_o_v["mla_w_o"])], name="mla_out", mode="nn")

    o_raw, yb, states = _hgrn_fwd(proj, hgrn_cols, d, hgrn_lb_logits, hgrn_out_norm, name="hgrn_scan")
    y_b = _mm([(yb, w_o_v["hgrn_w_o"])], name="hgrn_out", mode="nn")

    merged = _merge_fwd(proj, col_ga, col_gb, y_a, y_b, name="mix_merge")
    y_mix = _mm([(merged, w_o_v["w_out"])], name="mix_out", mode="nn")
    x2 = _norm_fwd(y_mix, mix_norm_post, name="mix_norm_post", resid=x1, scale=1.0)

    (w_ffn2,) = _gather_forward(_gather_wait(got2, [x2], name="gather_ffn2_wait")[0], jobs_ffn, name="gather_ffn2_forward")
    x3, saved2 = _ffn_fwd(x2, ffn2_norm_pre, ffn2_norm_post, w_ffn2, lay, "ffn2")
    dx3, loss_local = _loss_head(x3, target, name="loss_head")

    grads, deltas, new_m, new_v = {}, {}, {}, {}
    sel = cc.astype(jnp.int32)
    sel1 = jnp.reshape(sel, (1,))
    me_chip = (2 * cx + cy).astype(jnp.int32)

    def reduce_mid(handle, after, jobs, tag):
        bufs, recvd = _swap_wait(handle, after, name=f"grad_swap_{tag}_wait")
        sums = [_add_sibling(b, r, sel1, name=f"grad_add_sibling_{tag}_{i}") for i, (b, r) in enumerate(zip(bufs, recvd))]
        return _exchange_start(sums, jobs, name=f"grad_exchange_{tag}")

    def reduce_end(handle, after, tag):
        sums, lands = _exchange_wait(handle, after, name=f"grad_exchange_{tag}_wait")
        parts = [_add_shard(sums[job.a], lands[job.a], job, sel, me_chip, name=f"grad_add_chips_{tag}_{i}")
                 for i, job in enumerate(handle["jobs"])]
        return _join_list(parts, name=f"grad_join_{tag}")

    def natural(part, lo, rows, transposed):
        g_n = part[:, lo:lo + rows]
        hw_n = g_n.shape[2]
        return g_n.transpose(0, 2, 1).reshape(2 * hw_n, rows) if transposed else g_n.transpose(1, 0, 2).reshape(rows, 2 * hw_n)

    def adam(names, deps=()):
        for i, n in enumerate(names):
            shp = wts[n].shape
            two_d = (lambda a: a[0]) if n in BIG_WEIGHTS else (lambda a: a)
            dl, nm, nv = _adamw(two_d(wts[n]), grads[n], two_d(mom[n]), two_d(var[n]), name=f"adamw_{n}",
                                deps=deps if i == 0 else ())
            grads[n] = grads[n].reshape(shp)
            deltas[n], new_m[n], new_v[n] = dl.reshape(shp), nm.reshape(shp), nv.reshape(shp)
        return [deltas[n] for n in names]

    def ffn_grads(joined, tag, deps=()):
        nff = lay.nff
        grads[f"{tag}_w_gate"] = natural(joined[0], 0, nff, True)
        grads[f"{tag}_w_up"] = natural(joined[0], nff, nff, True)
        grads[f"{tag}_w_down"] = natural(joined[0], 2 * nff, nff, False)
        return adam([f"{tag}_w_gate", f"{tag}_w_up", f"{tag}_w_down"], deps)

    swaps = {}

    def start_swap(tag):
        def hook(gbuf):
            swaps[tag], started = _swap_start([gbuf], name=f"grad_swap_{tag}")
            return [started]
        return hook

    dx2, grads["ffn2_norm_pre"], grads["ffn2_norm_post"] = _ffn_bwd(
        dx3, saved2, ffn2_norm_pre, ffn2_norm_post, w_ffn2, lay, "ffn2", after_dw=start_swap("ffn2"))

    gwide = lax.empty((2, 10 * d, d // 2), F32)
    gnarrow = lax.empty((2, lay.rows_narrow, q_lora // 2), F32)
    dy_mix, grads["mix_norm_post"] = _norm_bwd(y_mix, mix_norm_post, dx2, name="mix_norm_post_bwd")
    dmerged = _mm([(dy_mix, w_o_v["w_out"])], name="mix_out_dx", mode="nt")
    gwide = _mm([(merged, dy_mix)], name="mix_out_dw", mode="tn", into=(gwide, 9 * d))
    dga, dgb, dy_a, dy_b = _merge_bwd(dmerged, proj, col_ga, col_gb, y_a, y_b, name="mix_merge_bwd")

    do_mla = _mm([(dy_a, w_o_v["mla_w_o"])], name="mla_out_dx", mode="nt")
    gwide = _mm([(o_mla, dy_a)], name="mla_out_dw", mode="tn", into=(gwide, 7 * d))
    dqcat, dkv, dkr = _attn_bwd(qcat, kvb, krot, do_mla, name="mla_attention_bwd", scale=scale)
    exch2, tok = reduce_mid(swaps["ffn2"], [dkr], _ffn_jobs(lay), "ffn2")

    dqp = _rope(dqcat, tabs, name="rope_q_bwd", group=QGROUP, backward=True, out_dtype=BF16)
    dk_r = _rope(dkr, tabs, name="rope_k_bwd", group=LANE, backward=True, out_dtype=BF16)
    dcqn = _mm([(dqp, w_q_v)], name="mla_q_up_dx", mode="nn", deps=[tok])
    gnarrow = _mm([(dqp, cqn)], name="mla_q_up_dw", mode="tn", into=(gnarrow, lay.off_q))
    dkvb = dkv.astype(BF16)
    dckvn = _mm([(dkvb, w_kv_v)], name="mla_kv_up_dx", mode="nn")
    gnarrow = _mm([(dkvb, ckvn)], name="mla_kv_up_dw", mode="tn", into=(gnarrow, lay.off_kv))
    dc_q, grads["mla_q_norm"] = _norm_bwd(proj, mla_q_norm, dcqn, name="mla_q_norm_bwd", col=0, dx_dtype=BF16)
    dc_kv, grads["mla_kv_norm"] = _norm_bwd(proj, mla_kv_norm, dckvn, name="mla_kv_norm_bwd", col=q_lora, dx_dtype=BF16)

    dyb = _mm([(dy_b, w_o_v["hgrn_w_o"])], name="hgrn_out_dx", mode="nt")
    gwide = _mm([(yb, dy_b)], name="hgrn_out_dw", mode="tn", into=(gwide, 8 * d))
    dhq, dhf, dhi, dhg, dlb_h, dnorm_h = _hgrn_bwd(proj, hgrn_cols, d, o_raw, dyb, states, hgrn_lb_logits, hgrn_out_norm,
                                                   name="hgrn_scan_bwd")

    dproj = jnp.concatenate([dc_q, dc_kv, dk_r, jnp.zeros((t, d - col_kr - LANE), BF16), dhq, dhf, dhi, dhg, dga, dgb], axis=1)
    dh2 = _mm([(dproj, w_in_v)], name="mix_in_dx", mode="nn")
    gwide = _mm([(dproj, h2)], name="mix_in_dw", mode="tn", into=(gwide, 0))
    dx1, grads["mix_norm_pre"] = _norm_bwd(x1, mix_norm_pre, dh2, name="mix_norm_pre_bwd", dres=dx2)
    swap_m, tok = _swap_start([gwide, gnarrow], name="grad_swap_mix")
    joined2 = reduce_end(exch2, [dx1], "ffn2")

    exchanges = {}

    def mix_exchange(after):
        exchanges["mix"], started = reduce_mid(swap_m, [after], _mix_jobs(lay), "mix")
        return [started]

    dx0, grads["ffn1_norm_pre"], grads["ffn1_norm_post"] = _ffn_bwd(
        dx1, saved1, ffn1_norm_pre, ffn1_norm_post, w_ffn1, lay, "ffn1", deps=[tok], after_act=mix_exchange,
        after_dw=start_swap("ffn1"))
    exch1, tok = reduce_mid(swaps["ffn1"], [dx0], _ffn_jobs(lay), "ffn1")

    joined_m = reduce_end(exchanges["mix"], [dx0, tok], "mix")
    done = ffn_grads(joined2, "ffn2")
    grads["w_in"] = natural(jnp.concatenate([joined_m[0], joined_m[1]], axis=1), 0, lay.ncol, True)
    for i, n in enumerate(("mla_w_o", "hgrn_w_o", "w_out")):
        grads[n] = natural(joined_m[2], i * lay.r_o, lay.r_o, False)
    grads["mla_w_q_up"] = natural(joined_m[3], 0, lay.hps * (HEAD + ROPE), True)
    grads["mla_w_kv_up"] = natural(joined_m[4], 0, lay.hps * QGROUP, True)
    done += adam(["w_in", "mla_w_q_up", "mla_w_kv_up", "mla_w_o", "hgrn_w_o", "w_out"])

    dlb = dlb_h.reshape(1, -1)
    dnorm = jnp.sum(dnorm_h, axis=0)
    small = {**{n: grads[n] for n in SMALL_WEIGHTS if n not in ("hgrn_lb_logits", "hgrn_out_norm")},
             "hgrn_lb_logits": dlb, "hgrn_out_norm": dnorm}
    vec = jnp.concatenate([small[n] for n in SMALL_WEIGHTS], axis=1)
    vec = _all_reduce_small(vec, name="grad_all_reduce_small")
    off = 0
    for n in SMALL_WEIGHTS:
        w_n = small[n].shape[1]
        grads[n] = vec[:, off:off + w_n]
        off += w_n
    grads["hgrn_lb_logits"] = _lb_logits_grad(hgrn_lb_logits, grads["hgrn_lb_logits"], name="lb_logits_grad")

    done += adam(list(SMALL_WEIGHTS))
    ffn_grads(reduce_end(exch1, done, "ffn1"), "ffn1")

    loss = lax.psum(loss_local, ("x", "y", "c"))
    dx_out = dx0.reshape(x.shape)
    return (loss, dx_out, *[grads[n] for n in ALL_WEIGHTS], *[deltas[n] for n in ALL_WEIGHTS],
            *[new_m[n] for n in ALL_WEIGHTS], *[new_v[n] for n in ALL_WEIGHTS])
```

```python
import functools

import jax
import jax.numpy as jnp
from jax import lax
from jax.experimental import pallas as pl
from jax.experimental.pallas import tpu as pltpu

F32 = jnp.float32
BF16 = jnp.bfloat16
MESH = pl.DeviceIdType.MESH

NORM_EPS = 1e-6
MACARON_SCALE = 0.5
ROPE_THETA = 10000.0
HEAD = 128
ROPE = 64
QGROUP = 2 * HEAD
SUB = 16
ADAM_LR, ADAM_B1, ADAM_B2, ADAM_EPS, ADAM_WD, ADAM_STEP = 0.001, 0.9, 0.999, 1e-08, 0.01, 10

LANE = 128
VMEM_LIMIT = 48 * 1024 * 1024
MM_TILE = 1024
MM_TILE_WIDE = 1536

BIG_WEIGHTS = ("ffn1_w_gate", "ffn1_w_up", "ffn1_w_down", "w_in", "mla_w_q_up", "mla_w_kv_up",
               "mla_w_o", "hgrn_w_o", "w_out", "ffn2_w_gate", "ffn2_w_up", "ffn2_w_down")
COL_SHARDED = ("ffn1_w_gate", "ffn1_w_up", "w_in", "mla_w_q_up", "mla_w_kv_up", "ffn2_w_gate", "ffn2_w_up")
SMALL_WEIGHTS = ("ffn1_norm_pre", "ffn1_norm_post", "mix_norm_pre", "mla_q_norm", "mla_kv_norm",
                 "hgrn_lb_logits", "hgrn_out_norm", "mix_norm_post", "ffn2_norm_pre", "ffn2_norm_post")
ALL_WEIGHTS = ("ffn1_norm_pre", "ffn1_w_gate", "ffn1_w_up", "ffn1_w_down", "ffn1_norm_post", "mix_norm_pre",
               "w_in", "mla_q_norm", "mla_w_q_up", "mla_kv_norm", "mla_w_kv_up", "mla_w_o", "hgrn_lb_logits",
               "hgrn_out_norm", "hgrn_w_o", "w_out", "mix_norm_post", "ffn2_norm_pre", "ffn2_w_gate",
               "ffn2_w_up", "ffn2_w_down", "ffn2_norm_post")


def _params(*sem):
    return pltpu.CompilerParams(dimension_semantics=sem or None, vmem_limit_bytes=VMEM_LIMIT)


def _pick(n, cap, offset=0):
    if n <= cap and offset % n == 0:
        return n
    best = None
    for t in range(LANE, min(n, cap) + 1, LANE):
        if n % t == 0 and offset % t == 0:
            best = t
    assert best is not None, (n, cap, offset)
    return best


def _row_tile(n, row_bytes, budget=1 << 20):
    best = None
    for t in range(8, n + 1, 8):
        if n % t == 0 and t * row_bytes <= budget:
            best = t
    return n if best is None else best


def _sigmoid(x):
    return 1.0 / (1.0 + jnp.exp(-x))


def _silu(x):
    return x * _sigmoid(x)


def _dsilu(x):
    s = _sigmoid(x)
    return s * (1.0 + x * (1.0 - s))


def _mm(pairs, *, name, mode="nn", out_dtype=F32, into=None, deps=(), extras=(), epilogue=None, out_dtypes=None, tm_cap=None):
    halves = isinstance(pairs[0][1], tuple)
    assert halves or mode == "tn"
    pairs = [(a, b if halves else (b, 0, b.shape[0])) for a, b in pairs]
    a0, (b0, b_off, b_rows) = pairs[0]
    hw = b0.shape[2] if halves else (into[0].shape[2] if into is not None else None)
    if mode == "nn":
        (m, kdim), n = a0.shape, 2 * hw
    elif mode == "nt":
        (m, kdim), n = a0.shape, b_rows
        assert kdim == 2 * hw
    else:
        (kdim, m), n = a0.shape, b0.shape[1]
    out_off = 0 if into is None else into[1]
    tm = _pick(m, tm_cap or (MM_TILE_WIDE if mode == "tn" else MM_TILE), out_off)
    tn = hw if (mode == "nn" or into is not None) else _pick(n, MM_TILE_WIDE, b_off if mode == "nt" else 0)
    tk = hw if mode == "nt" else _pick(kdim, MM_TILE, b_off if mode == "nn" else 0)
    assert n % tn == 0 and kdim % tk == 0
    nk = kdim // tk
    npair = len(pairs)
    dims = {"nn": (((1,), (0,)), ((), ())), "nt": (((1,), (1,)), ((), ())), "tn": (((0,), (0,)), ((), ()))}[mode]

    nout = 1 if epilogue is None else len(out_dtypes)

    def body(*refs):
        ins, x_refs = refs[:2 * npair], refs[2 * npair:2 * npair + len(extras)]
        o_refs, acc_ref = refs[-1 - nout:-1], refs[-1]
        k = pl.program_id(2)

        @pl.when(k == 0)
        def _():
            acc_ref[...] = jnp.zeros_like(acc_ref)

        for p in range(npair):
            a = ins[2 * p][...].astype(BF16)
            b = ins[2 * p + 1][...].astype(BF16)
            acc_ref[...] += lax.dot_general(a, b, dims, preferred_element_type=F32)

        @pl.when(k == nk - 1)
        def _():
            outs = (acc_ref[...],) if epilogue is None else epilogue(acc_ref[...], *[x[...] for x in x_refs])
            for o_ref, o in zip(o_refs, outs):
                o_ref[...] = o.astype(o_ref.dtype)

    a_spec = pl.BlockSpec((tk, tm), lambda i, j, k: (k, i)) if mode == "tn" else pl.BlockSpec((tm, tk), lambda i, j, k: (i, k))
    in_specs, flat = [], []
    for a, (b, off, _) in pairs:
        if mode == "nt":
            b_spec = pl.BlockSpec((None, tn, tk), lambda i, j, k, o=off // tn: (k, j + o, 0))
        elif mode == "nn":
            b_spec = pl.BlockSpec((None, tk, tn), lambda i, j, k, o=off // tk: (j, k + o, 0))
        else:
            b_spec = pl.BlockSpec((tk, tn), lambda i, j, k: (k, j))
        in_specs += [a_spec, b_spec]
        flat += [a, b]
    for extra in extras:
        in_specs.append(pl.BlockSpec((tm, tn), lambda i, j, k: (i, j)))
        flat.append(extra)
    for dep in deps:
        in_specs.append(pl.BlockSpec(memory_space=pl.ANY))
        flat.append(dep)
    if epilogue is not None:
        assert into is None
        out_shape, aliases = [jax.ShapeDtypeStruct((m, n), dt) for dt in out_dtypes], {}
        out_spec = [pl.BlockSpec((tm, tn), lambda i, j, k: (i, j))] * nout
    elif into is None:
        out_shape, aliases = jax.ShapeDtypeStruct((m, n), out_dtype), {}
        out_spec = pl.BlockSpec((tm, tn), lambda i, j, k: (i, j))
    else:
        out_shape, aliases = jax.ShapeDtypeStruct(into[0].shape, into[0].dtype), {len(flat): 0}
        out_spec = pl.BlockSpec((None, tm, tn), lambda i, j, k, o=out_off // tm: (j, i + o, 0))
        in_specs.append(pl.BlockSpec(memory_space=pl.ANY))
        flat.append(into[0])
    return pl.pallas_call(
        body, name=name, grid=(m // tm, n // tn, nk),
        in_specs=in_specs,
        out_specs=out_spec,
        out_shape=out_shape, input_output_aliases=aliases,
        scratch_shapes=[pltpu.VMEM((tm, tn), F32)],
        compiler_params=_params("parallel", "parallel", "arbitrary"),
    )(*flat)


def _norm_fwd(y, w, *, name, resid=None, scale=1.0, out_dtype=F32, col=0):
    t, d = y.shape[0], w.shape[1]
    tr = _pick(t, 256)
    assert col % d == 0

    def body(*refs):
        if resid is None:
            y_ref, w_ref, o_ref = refs
        else:
            y_ref, w_ref, r_ref, o_ref = refs
        yv = y_ref[...]
        out = yv * lax.rsqrt(jnp.mean(yv * yv, axis=-1, keepdims=True) + NORM_EPS) * w_ref[...]
        if resid is not None:
            out = r_ref[...] + scale * out
        o_ref[...] = out.astype(out_dtype)

    row = pl.BlockSpec((tr, d), lambda i: (i, 0))
    wspec = pl.BlockSpec((1, d), lambda i: (0, 0))
    ins, specs = [y, w], [pl.BlockSpec((tr, d), lambda i: (i, col // d)), wspec]
    if resid is not None:
        ins.append(resid)
        specs.append(row)
    return pl.pallas_call(
        body, name=name, grid=(t // tr,), in_specs=specs, out_specs=row,
        out_shape=jax.ShapeDtypeStruct((t, d), out_dtype), compiler_params=_params("parallel"),
    )(*ins)


def _norm_bwd(x, w, dy, *, name, scale=1.0, dres=None, col=0, dx_dtype=F32):
    t, d = x.shape[0], w.shape[1]
    tr = _pick(t, 256)
    assert col % d == 0

    def body(*refs):
        if dres is None:
            x_ref, w_ref, dy_ref, dx_ref, dw_ref = refs
        else:
            x_ref, w_ref, dy_ref, dr_ref, dx_ref, dw_ref = refs

        @pl.when(pl.program_id(0) == 0)
        def _():
            dw_ref[...] = jnp.zeros_like(dw_ref)

        xv = x_ref[...]
        r = lax.rsqrt(jnp.mean(xv * xv, axis=-1, keepdims=True) + NORM_EPS)
        xhat = xv * r
        dyv = dy_ref[...].astype(F32) * scale
        dw_ref[...] += jnp.sum(dyv * xhat, axis=0, keepdims=True)
        t_ = dyv * w_ref[...]
        dx = r * (t_ - xhat * jnp.mean(t_ * xhat, axis=-1, keepdims=True))
        if dres is not None:
            dx = dx + dr_ref[...]
        dx_ref[...] = dx.astype(dx_dtype)

    row = pl.BlockSpec((tr, d), lambda i: (i, 0))
    wspec = pl.BlockSpec((1, d), lambda i: (0, 0))
    ins, specs = [x, w, dy], [pl.BlockSpec((tr, d), lambda i: (i, col // d)), wspec, row]
    if dres is not None:
        ins.append(dres)
        specs.append(row)
    return pl.pallas_call(
        body, name=name, grid=(t // tr,), in_specs=specs, out_specs=(row, wspec),
        out_shape=(jax.ShapeDtypeStruct((t, d), dx_dtype), jax.ShapeDtypeStruct((1, d), F32)),
        compiler_params=_params("arbitrary"),
    )(*ins)


def _elementwise(fn, ins, out_dtypes, *, name, width=None, cols=None):
    t = ins[0].shape[0]
    d = ins[0].shape[1] if width is None else width
    cols = [0] * len(ins) if cols is None else cols
    tc = _pick(d, 2048)
    for c in cols:
        tc = _pick(d, tc, c)
    tr = _row_tile(t, tc * 4)
    nout = len(out_dtypes)

    def body(*refs):
        outs = fn(*[r[...].astype(F32) for r in refs[:len(ins)]])
        for o_ref, o in zip(refs[len(ins):], outs):
            o_ref[...] = o.astype(o_ref.dtype)

    spec = pl.BlockSpec((tr, tc), lambda i, j: (i, j))
    in_specs = [pl.BlockSpec((tr, tc), lambda i, j, o=c // tc: (i, j + o)) for c in cols]
    return pl.pallas_call(
        body, name=name, grid=(t // tr, d // tc), in_specs=in_specs, out_specs=[spec] * nout,
        out_shape=[jax.ShapeDtypeStruct((t, d), dt) for dt in out_dtypes],
        compiler_params=_params("parallel", "parallel"),
    )(*ins)


def _merge_fwd(proj, col_a, col_b, ya, yb, *, name):
    return _elementwise(lambda a, b, p, q: (_sigmoid(a) * p + _sigmoid(b) * q,), [proj, proj, ya, yb], [BF16],
                        name=name, width=ya.shape[1], cols=[col_a, col_b, 0, 0])[0]


def _merge_bwd(dm, proj, col_a, col_b, ya, yb, *, name):
    def fn(dmv, a, b, p, q):
        sa, sb = _sigmoid(a), _sigmoid(b)
        return dmv * p * sa * (1.0 - sa), dmv * q * sb * (1.0 - sb), dmv * sa, dmv * sb

    return _elementwise(fn, [dm, proj, proj, ya, yb], [BF16, BF16, BF16, BF16], name=name, width=ya.shape[1],
                        cols=[0, col_a, col_b, 0, 0])


def _loss_head(xo, target, *, name):
    t, d = xo.shape
    tr = _pick(t, 256)

    def body(x_ref, t_ref, dx_ref, l_ref):
        @pl.when(pl.program_id(0) == 0)
        def _():
            l_ref[...] = jnp.zeros_like(l_ref)

        err = x_ref[...] - t_ref[...]
        dx_ref[...] = err * (1.0 / d)
        l_ref[...] += 0.5 * jnp.sum(jnp.mean(err * err, axis=-1, keepdims=True), axis=0, keepdims=True)

    row = pl.BlockSpec((tr, d), lambda i: (i, 0))
    dx, l = pl.pallas_call(
        body, name=name, grid=(t // tr,), in_specs=[row, row],
        out_specs=(row, pl.BlockSpec((1, 1), lambda i: (0, 0))),
        out_shape=(jax.ShapeDtypeStruct((t, d), F32), jax.ShapeDtypeStruct((1, 1), F32)),
        compiler_params=_params("arbitrary"),
    )(xo, target)
    return dx, l[0, 0]


def _rope(xin, tabs, *, name, group, backward, out_dtype, col=0, ngroup=None):
    t = xin.shape[0]
    ngroup = xin.shape[1] // group if ngroup is None else ngroup
    wdt = ngroup * group
    tr = _pick(t, 256)
    assert col % wdt == 0
    cos_t, nsin_t, sin_t = tabs

    def body(x_ref, c_ref, n_ref, s_ref, o_ref):
        cv, nv, sv = c_ref[...], n_ref[...], s_ref[...]
        for g in range(ngroup):
            lo, hi = g * group, (g + 1) * group
            rot = x_ref[:, hi - LANE:hi].astype(F32)
            if backward:
                out = rot * cv + pltpu.roll(rot * nv, 32, 1) + pltpu.roll(rot * sv, LANE - 32, 1)
            else:
                out = rot * cv + pltpu.roll(rot, LANE - 32, 1) * nv + pltpu.roll(rot, 32, 1) * sv
            if group > LANE:
                o_ref[:, lo:hi - LANE] = x_ref[:, lo:hi - LANE].astype(out_dtype)
            o_ref[:, hi - LANE:hi] = out.astype(out_dtype)

    xspec = pl.BlockSpec((tr, wdt), lambda i: (i, 0))
    tspec = pl.BlockSpec((tr, LANE), lambda i: (i, 0))
    return pl.pallas_call(
        body, name=name, grid=(t // tr,),
        in_specs=[pl.BlockSpec((tr, wdt), lambda i: (i, col // wdt)), tspec, tspec, tspec], out_specs=xspec,
        out_shape=jax.ShapeDtypeStruct((t, wdt), out_dtype), compiler_params=_params("parallel"),
    )(xin, cos_t, nsin_t, sin_t)


def _scores(q, kv, kr, qi, tq, scale):
    kcat = jnp.concatenate([kv[:, :HEAD], kr], axis=1)
    s = lax.dot_general(q, kcat, (((1,), (1,)), ((), ())), preferred_element_type=F32) * scale
    row = qi * tq + lax.broadcasted_iota(jnp.int32, s.shape, 0)
    col = lax.broadcasted_iota(jnp.int32, s.shape, 1)
    s = jnp.where(col <= row, s, -jnp.inf)
    p = jnp.exp(s - jnp.max(s, axis=-1, keepdims=True))
    return p / jnp.sum(p, axis=-1, keepdims=True), kcat


def _attn_fwd(qcat, kv, kr, *, name, scale):
    t = qcat.shape[0]
    nh = qcat.shape[1] // QGROUP
    tq = _pick(t, 256)

    def body(q_ref, kv_ref, kr_ref, o_ref):
        for qi in range(t // tq):
            @pl.when(pl.program_id(1) == qi)
            def _(qi=qi):
                kvv = kv_ref[0:(qi + 1) * tq, :]
                p, _ = _scores(q_ref[...], kvv, kr_ref[0:(qi + 1) * tq, :], qi, tq, scale)
                o_ref[...] = jnp.dot(p.astype(BF16), kvv[:, HEAD:], preferred_element_type=F32).astype(BF16)

    return pl.pallas_call(
        body, name=name, grid=(nh, t // tq),
        in_specs=[pl.BlockSpec((tq, QGROUP), lambda h, i: (i, h)), pl.BlockSpec((t, QGROUP), lambda h, i: (0, h)),
                  pl.BlockSpec((t, LANE), lambda h, i: (0, 0))],
        out_specs=pl.BlockSpec((tq, HEAD), lambda h, i: (i, h)),
        out_shape=jax.ShapeDtypeStruct((t, nh * HEAD), BF16), compiler_params=_params("parallel", "parallel"),
    )(qcat, kv, kr)


def _attn_bwd(qcat, kv, kr, do, *, name, scale):
    t = qcat.shape[0]
    nh = qcat.shape[1] // QGROUP
    tq = _pick(t, 256)
    nq = t // tq

    def body(q_ref, kv_ref, kr_ref, do_ref, dq_ref, dkv_ref, dkr_ref, dk_acc, dv_acc):
        h, i = pl.program_id(0), pl.program_id(1)

        @pl.when(i == 0)
        def _():
            dk_acc[...] = jnp.zeros_like(dk_acc)
            dv_acc[...] = jnp.zeros_like(dv_acc)

        @pl.when((i == 0) & (h == 0))
        def _():
            dkr_ref[...] = jnp.zeros_like(dkr_ref)

        for qi in range(nq):
            @pl.when(i == qi)
            def _(qi=qi):
                keys = slice(0, (qi + 1) * tq)
                q = q_ref[...]
                kvv = kv_ref[keys, :]
                dov = do_ref[...].astype(BF16)
                p, kcat = _scores(q, kvv, kr_ref[keys, :], qi, tq, scale)
                dp = lax.dot_general(dov, kvv[:, HEAD:], (((1,), (1,)), ((), ())), preferred_element_type=F32)
                ds = (p * (dp - jnp.sum(p * dp, axis=-1, keepdims=True)) * scale).astype(BF16)
                dq_ref[...] = jnp.dot(ds, kcat, preferred_element_type=F32)
                dk_acc[keys, :] += lax.dot_general(ds, q, (((0,), (0,)), ((), ())), preferred_element_type=F32)
                dv_acc[keys, :] += lax.dot_general(p.astype(BF16), dov, (((0,), (0,)), ((), ())), preferred_element_type=F32)

        @pl.when(i == nq - 1)
        def _():
            dk = dk_acc[...]
            dkv_ref[...] = jnp.concatenate([dk[:, :HEAD], dv_acc[...]], axis=1)
            dkr_ref[...] += dk[:, HEAD:]

    return pl.pallas_call(
        body, name=name, grid=(nh, nq),
        in_specs=[pl.BlockSpec((tq, QGROUP), lambda h, i: (i, h)), pl.BlockSpec((t, QGROUP), lambda h, i: (0, h)),
                  pl.BlockSpec((t, LANE), lambda h, i: (0, 0)), pl.BlockSpec((tq, HEAD), lambda h, i: (i, h))],
        out_specs=(pl.BlockSpec((tq, QGROUP), lambda h, i: (i, h)), pl.BlockSpec((t, QGROUP), lambda h, i: (0, h)),
                   pl.BlockSpec((t, LANE), lambda h, i: (0, 0))),
        out_shape=(jax.ShapeDtypeStruct((t, nh * QGROUP), F32), jax.ShapeDtypeStruct((t, nh * QGROUP), F32),
                   jax.ShapeDtypeStruct((t, LANE), F32)),
        scratch_shapes=[pltpu.VMEM((t, QGROUP), F32), pltpu.VMEM((t, HEAD), F32)],
        compiler_params=_params("arbitrary", "arbitrary"),
    )(qcat, kv, kr, do)


def _split3(x):
    hi = x.astype(BF16)
    r1 = x - hi.astype(F32)
    mid = r1.astype(BF16)
    lo = (r1 - mid.astype(F32)).astype(BF16)
    return hi, mid, lo


def _tri_matmul(mask, x):
    m = mask.astype(BF16)
    return sum(jnp.dot(m, part, preferred_element_type=F32) for part in _split3(x))


def _sub_cumsum(g, tb):
    row = lax.broadcasted_iota(jnp.int32, (tb, tb), 0)
    col = lax.broadcasted_iota(jnp.int32, (tb, tb), 1)
    return _tri_matmul(jnp.where((col <= row) & (col // SUB == row // SUB), 1.0, 0.0), g)


def _sub_suffix_prefix(after, before, tb):
    row = lax.broadcasted_iota(jnp.int32, (tb, tb), 0)
    col = lax.broadcasted_iota(jnp.int32, (tb, tb), 1)
    same = col // SUB == row // SUB
    return (_tri_matmul(jnp.where((col >= row) & same, 1.0, 0.0), after)
            + _tri_matmul(jnp.where((col < row) & same, 1.0, 0.0), before))


def _lower_bound(logits):
    mx = jnp.max(logits, axis=0, keepdims=True)
    e = jnp.exp(logits - mx)
    return e[0:1, :] / jnp.sum(e, axis=0, keepdims=True)


def _hgrn_fwd(proj, cols, wdt, logits, out_norm, *, name):
    t = proj.shape[0]
    nh = wdt // HEAD
    tb = _pick(t, 128)
    ns = tb // SUB

    def body(hq_ref, hf_ref, hi_ref, hg_ref, lg_ref, w_ref, o_ref, yb_ref, st_ref, s_ref, q_s, k_s, b_s):
        @pl.when(pl.program_id(1) == 0)
        def _():
            s_ref[...] = jnp.zeros_like(s_ref)

        lb = _lower_bound(lg_ref[...])
        f = lb + (1.0 - lb) * _sigmoid(hf_ref[...])
        q_s[...] = _silu(hq_ref[...])
        k_s[...] = 1.0 - f
        b_s[...] = _sub_cumsum(jnp.log(f), tb)
        rowid = lax.broadcasted_iota(jnp.int32, (SUB, HEAD), 0)

        def sub(c, st):
            rows = pl.ds(pl.multiple_of(c * SUB, SUB), SUB)
            qc, kc, bc, vc = q_s[rows, :], k_s[rows, :], b_s[rows, :], hi_ref[rows, :]
            st_ref[0, c] = st
            bl = bc[SUB - 1:SUB, :]
            oc = lax.dot_general((qc * jnp.exp(bc)).astype(BF16), st.astype(BF16), (((1,), (1,)), ((), ())),
                                 preferred_element_type=F32)
            for s in range(SUB):
                e = jnp.where(rowid >= s, jnp.exp(bc - bc[s:s + 1, :]), 0.0)
                a = jnp.sum(qc * e * kc[s:s + 1, :], axis=1, keepdims=True)
                oc = oc + a * vc[s:s + 1, :]
            o_ref[rows, :] = oc
            kd = kc * jnp.exp(bl - bc)
            return jnp.exp(bl) * st + lax.dot_general(vc.astype(BF16), kd.astype(BF16), (((0,), (0,)), ((), ())),
                                                      preferred_element_type=F32)

        s_ref[...] = lax.fori_loop(0, ns, sub, s_ref[...], unroll=True)
        o = o_ref[...]
        r = lax.rsqrt(jnp.mean(o * o, axis=-1, keepdims=True) + NORM_EPS)
        yb_ref[...] = (o * r * w_ref[...] * _silu(hg_ref[...])).astype(BF16)

    blk = pl.BlockSpec((tb, HEAD), lambda h, j: (j, h))
    return pl.pallas_call(
        body, name=name, grid=(nh, t // tb),
        in_specs=[pl.BlockSpec((tb, HEAD), lambda h, j, o=c // HEAD: (j, h + o)) for c in cols]
        + [pl.BlockSpec((2, HEAD), lambda h, j: (0, h)), pl.BlockSpec((1, HEAD), lambda h, j: (0, 0))],
        out_specs=(blk, blk, pl.BlockSpec((1, ns, HEAD, HEAD), lambda h, j: (h, j, 0, 0))),
        out_shape=(jax.ShapeDtypeStruct((t, wdt), F32), jax.ShapeDtypeStruct((t, wdt), BF16),
                   jax.ShapeDtypeStruct((nh, t // SUB, HEAD, HEAD), F32)),
        scratch_shapes=[pltpu.VMEM((HEAD, HEAD), F32)] + [pltpu.VMEM((tb, HEAD), F32)] * 3,
        compiler_params=_params("parallel", "arbitrary"),
    )(proj, proj, proj, proj, logits, out_norm)


def _hgrn_bwd(proj, cols, wdt, o_raw, dyb, states, logits, out_norm, *, name):
    t = proj.shape[0]
    nh = wdt // HEAD
    tb = _pick(t, 128)
    ns = tb // SUB
    nb = t // tb

    def body(hq_ref, hf_ref, hi_ref, hg_ref, o_ref, dy_ref, st_ref, lg_ref, w_ref,
             dhq_ref, dhf_ref, dhi_ref, dhg_ref, dlb_ref, dw_ref,
             ds_ref, q_s, k_s, b_s, do_s, dq_s, dk_s, dv_s, after_s, before_s, thru_s):
        @pl.when(pl.program_id(1) == 0)
        def _():
            ds_ref[...] = jnp.zeros_like(ds_ref)
            dlb_ref[...] = jnp.zeros_like(dlb_ref)
            dw_ref[...] = jnp.zeros_like(dw_ref)

        lb = _lower_bound(lg_ref[...])
        hqv, hgv = hq_ref[...], hg_ref[...]
        sig = _sigmoid(hf_ref[...])
        f = lb + (1.0 - lb) * sig
        q_s[...] = _silu(hqv)
        k_s[...] = 1.0 - f
        b_s[...] = _sub_cumsum(jnp.log(f), tb)

        o = o_ref[...]
        r = lax.rsqrt(jnp.mean(o * o, axis=-1, keepdims=True) + NORM_EPS)
        nrm = o * r
        w = w_ref[...]
        dy = dy_ref[...].astype(F32)
        dhg_ref[...] = (dy * nrm * w * _dsilu(hgv)).astype(BF16)
        dnw = dy * _silu(hgv)
        dw_ref[0] += jnp.sum(dnw * nrm, axis=0, keepdims=True)
        tt = dnw * w
        do_s[...] = r * (tt - nrm * jnp.mean(tt * nrm, axis=-1, keepdims=True))
        rowid = lax.broadcasted_iota(jnp.int32, (SUB, HEAD), 0)

        def sub(cc, dst):
            c = ns - 1 - cc
            rows = pl.ds(pl.multiple_of(c * SUB, SUB), SUB)
            qc, kc, bc, vc, doc = q_s[rows, :], k_s[rows, :], b_s[rows, :], hi_ref[rows, :], do_s[rows, :]
            st = st_ref[0, c]
            bl = bc[SUB - 1:SUB, :]
            eb = jnp.exp(bc)
            ekd = jnp.exp(bl - bc)
            qe, kd = qc * eb, kc * ekd
            dob, vcb = doc.astype(BF16), vc.astype(BF16)
            dq_st = jnp.dot(dob, st.astype(BF16), preferred_element_type=F32) * eb
            dk_st = jnp.dot(vcb, dst.astype(BF16), preferred_element_type=F32) * ekd
            dv = lax.dot_general(kd.astype(BF16), dst.astype(BF16), (((1,), (1,)), ((), ())), preferred_element_type=F32)
            dq_in = jnp.zeros_like(qc)
            dk_in = jnp.zeros_like(qc)
            for s in range(SUB):
                e = jnp.where(rowid >= s, jnp.exp(bc - bc[s:s + 1, :]), 0.0)
                ek = e * kc[s:s + 1, :]
                a = jnp.sum(qc * ek, axis=1, keepdims=True)
                da = jnp.sum(doc * vc[s:s + 1, :], axis=1, keepdims=True)
                dq_in = dq_in + da * ek
                dk_in = dk_in + jnp.where(rowid == s, jnp.sum(da * e * qc, axis=0, keepdims=True), 0.0)
                dv = dv + jnp.where(rowid == s, jnp.sum(a * doc, axis=0, keepdims=True), 0.0)
            ebl = jnp.exp(bl)
            dq_s[rows, :] = dq_st + dq_in
            dk_s[rows, :] = dk_st + dk_in
            dv_s[rows, :] = dv
            after_s[rows, :] = qc * (dq_st + dq_in) - kc * dk_in
            before_s[rows, :] = kc * dk_st
            thru_s[rows, :] = jnp.broadcast_to(ebl * jnp.sum(st * dst, axis=0, keepdims=True), (SUB, HEAD))
            return ebl * dst + lax.dot_general(dob, qe.astype(BF16), (((0,), (0,)), ((), ())), preferred_element_type=F32)

        ds_ref[...] = lax.fori_loop(0, ns, sub, ds_ref[...], unroll=True)
        dg = _sub_suffix_prefix(after_s[...], before_s[...], tb) + thru_s[...]
        dhq_ref[...] = (dq_s[...] * _dsilu(hqv)).astype(BF16)
        dft = dg / f - dk_s[...]
        dhf_ref[...] = (dft * (1.0 - lb) * sig * (1.0 - sig)).astype(BF16)
        dlb_ref[0] += jnp.sum(dft * (1.0 - sig), axis=0, keepdims=True)
        dhi_ref[...] = dv_s[...].astype(BF16)

    blk = pl.BlockSpec((tb, HEAD), lambda h, j: (nb - 1 - j, h))
    vec = pl.BlockSpec((1, 1, HEAD), lambda h, j: (h, 0, 0))
    tok = jax.ShapeDtypeStruct((t, wdt), BF16)
    per_head = jax.ShapeDtypeStruct((nh, 1, HEAD), F32)
    return pl.pallas_call(
        body, name=name, grid=(nh, nb),
        in_specs=[pl.BlockSpec((tb, HEAD), lambda h, j, o=c // HEAD: (nb - 1 - j, h + o)) for c in cols]
        + [blk, blk] + [pl.BlockSpec((1, ns, HEAD, HEAD), lambda h, j: (h, nb - 1 - j, 0, 0)),
                              pl.BlockSpec((2, HEAD), lambda h, j: (0, h)), pl.BlockSpec((1, HEAD), lambda h, j: (0, 0))],
        out_specs=(blk, blk, blk, blk, vec, vec),
        out_shape=(tok, tok, tok, tok, per_head, per_head),
        scratch_shapes=[pltpu.VMEM((HEAD, HEAD), F32)] + [pltpu.VMEM((tb, HEAD), F32)] * 10,
        compiler_params=_params("arbitrary", "arbitrary"),
    )(proj, proj, proj, proj, o_raw, dyb, states, logits, out_norm)


def _lb_logits_grad(logits, dlb, *, name):
    def body(lg_ref, d_ref, o_ref):
        lg = lg_ref[...]
        e = jnp.exp(lg - jnp.max(lg, axis=0, keepdims=True))
        p = e / jnp.sum(e, axis=0, keepdims=True)
        d = d_ref[...]
        rowid = lax.broadcasted_iota(jnp.int32, lg.shape, 0)
        dp = jnp.where(rowid == 0, d, 0.0)
        o_ref[...] = p * (dp - jnp.sum(p * dp, axis=0, keepdims=True))

    return pl.pallas_call(body, name=name, out_shape=jax.ShapeDtypeStruct(logits.shape, F32))(logits, dlb)


def _adamw(w, g, m, v, *, name, deps=()):
    r, c = w.shape
    tc = _pick(c, 2048) if c % LANE == 0 else c
    tr = _row_tile(r, tc * 4)

    def body(w_ref, g_ref, m_ref, v_ref, *rest):
        d_ref, nm_ref, nv_ref = rest[-3:]
        gv = g_ref[...]
        nm = ADAM_B1 * m_ref[...] + (1.0 - ADAM_B1) * gv
        nv = ADAM_B2 * v_ref[...] + (1.0 - ADAM_B2) * (gv * gv)
        m_hat = nm / (1.0 - ADAM_B1 ** ADAM_STEP)
        v_hat = nv / (1.0 - ADAM_B2 ** ADAM_STEP)
        d_ref[...] = -ADAM_LR * (m_hat / (jnp.sqrt(v_hat) + ADAM_EPS) + ADAM_WD * w_ref[...])
        nm_ref[...] = nm
        nv_ref[...] = nv

    spec = pl.BlockSpec((tr, tc), lambda i, j: (i, j))
    shp = jax.ShapeDtypeStruct((r, c), F32)
    return pl.pallas_call(
        body, name=name, grid=(r // tr, c // tc), in_specs=[spec] * 4 + [ANY] * len(deps), out_specs=[spec] * 3,
        out_shape=[shp, shp, shp], compiler_params=_params("parallel", "parallel"),
    )(w, g, m, v, *deps)


def _coords():
    return lax.axis_index("x"), lax.axis_index("y"), lax.axis_index("c")


def _other_chips(x, y):
    return [(1 - x, y), (x, 1 - y), (1 - x, 1 - y)]


ANY = pl.BlockSpec(memory_space=pl.ANY)


class _Layout:
    def __init__(self, d, dff, in_cols, q_lora, kv_lora, nh):
        assert q_lora == kv_lora and nh % 4 == 0 and dff % (4 * LANE) == 0 and in_cols % 4 == 0 and d % 4 == 0
        self.d, self.dff, self.q_lora, self.nh = d, dff, q_lora, nh
        self.head = q_lora + kv_lora + ROPE
        self.pad = d - self.head
        self.nff, self.ncol, self.r_o, self.hps = dff // 4, in_cols // 4, d // 4, nh // 4
        assert self.head <= self.ncol
        self.off_q, self.off_kv, self.rows_narrow = 0, nh * QGROUP, 2 * nh * QGROUP


HBM = pl.BlockSpec(memory_space=pltpu.HBM)
SEMS = pl.BlockSpec(memory_space=pltpu.SEMAPHORE)
SPLIT = dict(has_side_effects=pltpu.SideEffectType.DATAFLOW_SIDE_EFFECTING)


def _in_hbm(a):
    return pltpu.with_memory_space_constraint(a, pltpu.HBM)


def _shard_rows(jobs, k):
    out, lrow = [], [0, 0]
    for job in jobs:
        for row, rows in job.pieces(k):
            out.append((job.a, lrow[job.a], row, rows))
            lrow[job.a] += rows
    return out


def _shard_total(jobs, a):
    return sum(rows for b, _, _, rows in _shard_rows(jobs, 0) if b == a)


def _gather_start(packs, lands, jobs, *, name, deps=()):
    n = len(packs)

    def body(*refs):
        p_refs, l_refs, send, recv, token = refs[:n], refs[n:2 * n], refs[-2 * n - 3], refs[-2 * n - 2], refs[-1]
        x, y, c = _coords()
        for a, lrow, row, rows in _shard_rows(jobs, 2 * x + y):
            pltpu.make_async_remote_copy(
                src_ref=p_refs[a].at[:, pl.ds(lrow, rows)], dst_ref=l_refs[a].at[:, pl.ds(row, rows)],
                send_sem=send.at[4 * a + 3], recv_sem=recv.at[4 * a + 3], device_id=(x, y, 1 - c), device_id_type=MESH).start()
            for j, (px, py) in enumerate(_other_chips(x, y)):
                pltpu.make_async_remote_copy(
                    src_ref=p_refs[a].at[c, pl.ds(lrow, rows)], dst_ref=l_refs[a].at[c, pl.ds(row, rows)],
                    send_sem=send.at[4 * a + j], recv_sem=recv.at[4 * a + j], device_id=(px, py, c), device_id_type=MESH).start()
        token[...] = jnp.zeros_like(token)

    thru = [pltpu.HBM(a.shape, a.dtype) for a in packs + lands]
    out = pl.pallas_call(
        body, name=name, in_specs=[HBM] * (2 * n) + [ANY] * len(deps),
        out_shape=(pltpu.SemaphoreType.DMA((4 * n,)), pltpu.SemaphoreType.DMA((4 * n,)), *thru, jax.ShapeDtypeStruct((8, LANE), F32)),
        out_specs=(SEMS, SEMS, *[HBM] * (2 * n), pl.BlockSpec(memory_space=pltpu.VMEM)),
        input_output_aliases={i: 2 + i for i in range(2 * n)}, compiler_params=pltpu.CompilerParams(**SPLIT),
    )(*[_in_hbm(a) for a in packs + lands], *deps)
    return dict(send=out[0], recv=out[1], bufs=list(out[2:2 + 2 * n]), n=n, jobs=jobs), out[-1]


def _gather_wait(handle, after, *, name):
    n, jobs = handle["n"], handle["jobs"]

    def body(*refs):
        l_refs, send, recv, token = refs[n:2 * n], refs[2 * n], refs[2 * n + 1], refs[-1]
        token[...] = jnp.zeros_like(token)
        x, y, c = _coords()
        for a in range(n):
            total = _shard_total(jobs, a)
            for j, like in enumerate([l_refs[a].at[0, pl.ds(0, total)]] * 3 + [l_refs[a].at[:, pl.ds(0, total)]]):
                cp = pltpu.make_async_remote_copy(src_ref=like, dst_ref=like, send_sem=send.at[4 * a + j],
                                                  recv_sem=recv.at[4 * a + j], device_id=(x, y, c), device_id_type=MESH)
                cp.wait_send()
                cp.wait_recv()

    out = pl.pallas_call(
        body, name=name, in_specs=[HBM] * (2 * n) + [SEMS, SEMS] + [ANY] * len(after),
        out_shape=[pltpu.HBM(a.shape, a.dtype) for a in handle["bufs"]] + [jax.ShapeDtypeStruct((8, LANE), F32)],
        out_specs=[HBM] * (2 * n) + [pl.BlockSpec(memory_space=pltpu.VMEM)],
        input_output_aliases={i: i for i in range(2 * n)}, compiler_params=pltpu.CompilerParams(**SPLIT),
    )(*handle["bufs"], handle["send"], handle["recv"], *after)
    return list(out[n:2 * n]), out[-1]


def _gather_forward(lands, jobs, *, name, deps=()):
    n = len(lands)

    def body(*refs):
        l_refs, send, recv = refs[n + len(deps):2 * n + len(deps)], refs[-2], refs[-1]
        x, y, c = _coords()
        for j, (px, py) in enumerate(_other_chips(x, y)):
            for a, _, row, rows in _shard_rows(jobs, 2 * px + py):
                blk = l_refs[a].at[c, pl.ds(row, rows)]
                pltpu.make_async_remote_copy(src_ref=blk, dst_ref=blk, send_sem=send.at[3 * a + j], recv_sem=recv.at[3 * a + j],
                                             device_id=(x, y, 1 - c), device_id_type=MESH).start()
        for a in range(n):
            like = l_refs[a].at[0, pl.ds(0, _shard_total(jobs, a))]
            for j in range(3):
                cp = pltpu.make_async_remote_copy(src_ref=like, dst_ref=like, send_sem=send.at[3 * a + j],
                                                  recv_sem=recv.at[3 * a + j], device_id=(x, y, c), device_id_type=MESH)
                cp.wait_send()
                cp.wait_recv()

    sem = pltpu.SemaphoreType.DMA((3 * n,))
    return pl.pallas_call(
        body, name=name, in_specs=[ANY] * (n + len(deps)), out_specs=[ANY] * n, input_output_aliases={i: i for i in range(n)},
        out_shape=[jax.ShapeDtypeStruct(a.shape, a.dtype) for a in lands], scratch_shapes=[sem, sem],
    )(*lands, *deps)


def _add_sibling(g, recv, sel, *, name):
    rows, hw = recv.shape
    tr = _row_tile(rows, hw * 4)

    def body(sel_ref, g_ref, r_ref, o_ref):
        o_ref[...] = (g_ref[...] + r_ref[...]).astype(BF16)

    return pl.pallas_call(
        body, name=name, out_shape=jax.ShapeDtypeStruct((rows, hw), BF16),
        grid_spec=pltpu.PrefetchScalarGridSpec(
            num_scalar_prefetch=1, grid=(rows // tr,),
            in_specs=[pl.BlockSpec((None, tr, hw), lambda i, s: (s[0], i, 0)), pl.BlockSpec((tr, hw), lambda i, s: (i, 0))],
            out_specs=pl.BlockSpec((tr, hw), lambda i, s: (i, 0))),
        compiler_params=_params("parallel"),
    )(sel, g, recv)


class _Job:
    def __init__(self, a, blk, n_outer, n_inner, stride, start):
        self.a, self.blk, self.n_outer, self.n_inner, self.stride, self.start = a, blk, n_outer, n_inner, stride, start
        self.rows_out = n_outer * n_inner * blk

    def pieces(self, k):
        return [(self.start(k) + o * self.stride * self.blk, self.n_inner * self.blk) for o in range(self.n_outer)]


def _block_rows(rows, cap, *also):
    best = None
    for b in range(16, min(rows, cap) + 1, 16):
        if rows % b == 0 and all(v % b == 0 for v in also):
            best = b
    assert best is not None, (rows, also)
    return best


def _ffn_jobs(lay):
    b = _block_rows(lay.nff, 704, lay.dff)
    return [_Job(0, b, 3, lay.nff // b, lay.dff // b, lambda k: lay.nff * k)]


def _mix_jobs(lay):
    d, ncol, head, pad = lay.d, lay.ncol, lay.head, lay.pad
    first = lambda k, a, b: jnp.where(k == 0, a, b) if not isinstance(k, int) else (a if k == 0 else b)
    ba = _block_rows(head, 704, *[ncol * k + pad for k in (1, 2, 3)])
    bb = _block_rows(ncol - head, 704, *[ncol * k + d for k in (0, 1, 2, 3)])
    bo = _block_rows(lay.r_o, 704, d)
    bq = _block_rows(HEAD + ROPE, 704, QGROUP)
    bk = _block_rows(lay.hps * QGROUP, 704, lay.off_kv)
    return [_Job(0, ba, 1, head // ba, 0, lambda k: first(k, 0, ncol * k + pad)),
            _Job(0, bb, 1, (ncol - head) // bb, 0, lambda k: ncol * k + d),
            _Job(0, bo, 3, lay.r_o // bo, d // bo, lambda k: 7 * d + lay.r_o * k),
            _Job(1, bq, lay.hps, (HEAD + ROPE) // bq, QGROUP // bq, lambda k: QGROUP * lay.hps * k),
            _Job(1, bk, 1, lay.hps * QGROUP // bk, 0, lambda k: lay.off_kv + lay.hps * QGROUP * k)]


def _swap_start(gs, *, name):
    n = len(gs)
    lands = [lax.empty(g.shape[1:], g.dtype) for g in gs]

    def body(*refs):
        g_refs, land_refs, send, recv, token = refs[:n], refs[n:2 * n], refs[2 * n], refs[2 * n + 1], refs[-1]
        x, y, c = _coords()
        for a in range(n):
            pltpu.make_async_remote_copy(src_ref=g_refs[a].at[1 - c], dst_ref=land_refs[a], send_sem=send.at[a],
                                         recv_sem=recv.at[a], device_id=(x, y, 1 - c), device_id_type=MESH).start()
        token[...] = jnp.zeros_like(token)

    thru = [pltpu.HBM(a.shape, a.dtype) for a in gs + lands]
    out = pl.pallas_call(
        body, name=name, in_specs=[HBM] * (2 * n),
        out_shape=(pltpu.SemaphoreType.DMA((n,)), pltpu.SemaphoreType.DMA((n,)), *thru, jax.ShapeDtypeStruct((8, LANE), F32)),
        out_specs=(SEMS, SEMS, *[HBM] * (2 * n), pl.BlockSpec(memory_space=pltpu.VMEM)),
        input_output_aliases={i: 2 + i for i in range(2 * n)}, compiler_params=pltpu.CompilerParams(**SPLIT),
    )(*[_in_hbm(a) for a in gs + lands])
    return dict(send=out[0], recv=out[1], bufs=list(out[2:2 + 2 * n]), n=n), out[-1]


def _swap_wait(handle, after, *, name):
    n = handle["n"]

    def body(*refs):
        g_refs, land_refs, send, recv = refs[:n], refs[n:2 * n], refs[2 * n], refs[2 * n + 1]
        x, y, c = _coords()
        for a in range(n):
            cp = pltpu.make_async_remote_copy(src_ref=g_refs[a].at[1 - c], dst_ref=land_refs[a], send_sem=send.at[a],
                                              recv_sem=recv.at[a], device_id=(x, y, 1 - c), device_id_type=MESH)
            cp.wait_send()
            cp.wait_recv()

    out = pl.pallas_call(
        body, name=name, in_specs=[HBM] * (2 * n) + [SEMS, SEMS] + [ANY] * len(after),
        out_shape=[pltpu.HBM(a.shape, a.dtype) for a in handle["bufs"]], out_specs=[HBM] * (2 * n),
        input_output_aliases={i: i for i in range(2 * n)}, compiler_params=pltpu.CompilerParams(**SPLIT),
    )(*handle["bufs"], handle["send"], handle["recv"], *after)
    return list(out[:n]), list(out[n:])


def _exchange_start(ss, jobs, *, name):
    n = len(ss)
    lands = [lax.empty((3,) + s.shape, s.dtype) for s in ss]

    def body(*refs):
        s_refs, land_refs, send, recv, token = refs[:n], refs[n:2 * n], refs[2 * n], refs[2 * n + 1], refs[-1]
        x, y, c = _coords()
        for j, (px, py) in enumerate(_other_chips(x, y)):
            for job in jobs:
                for row, rows in job.pieces(2 * px + py):
                    pltpu.make_async_remote_copy(
                        src_ref=s_refs[job.a].at[pl.ds(row, rows)], dst_ref=land_refs[job.a].at[j, pl.ds(row, rows)],
                        send_sem=send.at[n * j + job.a], recv_sem=recv.at[n * j + job.a], device_id=(px, py, c),
                        device_id_type=MESH).start()
        token[...] = jnp.zeros_like(token)

    thru = [pltpu.HBM(a.shape, a.dtype) for a in ss + lands]
    out = pl.pallas_call(
        body, name=name, in_specs=[HBM] * (2 * n),
        out_shape=(pltpu.SemaphoreType.DMA((3 * n,)), pltpu.SemaphoreType.DMA((3 * n,)), *thru, jax.ShapeDtypeStruct((8, LANE), F32)),
        out_specs=(SEMS, SEMS, *[HBM] * (2 * n), pl.BlockSpec(memory_space=pltpu.VMEM)),
        input_output_aliases={i: 2 + i for i in range(2 * n)}, compiler_params=pltpu.CompilerParams(**SPLIT),
    )(*[_in_hbm(a) for a in ss + lands])
    return dict(send=out[0], recv=out[1], bufs=list(out[2:2 + 2 * n]), n=n, jobs=jobs), out[-1]


def _exchange_wait(handle, after, *, name):
    n, jobs = handle["n"], handle["jobs"]
    total = [sum(rows for job in jobs if job.a == a for _, rows in job.pieces(0)) for a in range(n)]

    def body(*refs):
        s_refs, land_refs, send, recv = refs[:n], refs[n:2 * n], refs[2 * n], refs[2 * n + 1]
        x, y, c = _coords()
        for a in range(n):
            for j in range(3):
                all_rows = land_refs[a].at[0, pl.ds(0, total[a])]
                cp = pltpu.make_async_remote_copy(src_ref=all_rows, dst_ref=all_rows, send_sem=send.at[n * j + a],
                                                  recv_sem=recv.at[n * j + a], device_id=(x, y, c), device_id_type=MESH)
                cp.wait_send()
                cp.wait_recv()

    out = pl.pallas_call(
        body, name=name, in_specs=[HBM] * (2 * n) + [SEMS, SEMS] + [ANY] * len(after),
        out_shape=[pltpu.HBM(a.shape, a.dtype) for a in handle["bufs"]], out_specs=[HBM] * (2 * n),
        input_output_aliases={i: i for i in range(2 * n)}, compiler_params=pltpu.CompilerParams(**SPLIT),
    )(*handle["bufs"], handle["send"], handle["recv"], *after)
    return list(out[:n]), list(out[n:])


def _add_shard(s, land, job, sel, k, *, name):
    hw = s.shape[1]
    blk, no, ni, stride = job.blk, job.n_outer, job.n_inner, job.stride
    scal = jnp.stack([sel, job.start(k) // blk]).astype(jnp.int32)

    def body(sc_ref, own_ref, r_ref, o_ref):
        o_ref[...] = ((own_ref[...].astype(F32) + r_ref[0].astype(F32)) + r_ref[1].astype(F32)) + r_ref[2].astype(F32)

    return pl.pallas_call(
        body, name=name, out_shape=jax.ShapeDtypeStruct((2, job.rows_out, hw), F32),
        grid_spec=pltpu.PrefetchScalarGridSpec(
            num_scalar_prefetch=1, grid=(no, ni),
            in_specs=[pl.BlockSpec((blk, hw), lambda o, b, sc: (sc[1] + o * stride + b, 0)),
                      pl.BlockSpec((3, blk, hw), lambda o, b, sc: (0, sc[1] + o * stride + b, 0))],
            out_specs=pl.BlockSpec((None, blk, hw), lambda o, b, sc: (sc[0], o * ni + b, 0))),
        compiler_params=_params("parallel", "parallel"),
    )(scal, s, land)


def _join_list(fs, *, name):
    n = len(fs)

    def body(*refs):
        f_refs, send_sems, recv_sems = refs[n:2 * n], refs[2 * n], refs[2 * n + 1]
        x, y, c = _coords()
        copies = [pltpu.make_async_remote_copy(
            src_ref=f.at[c], dst_ref=f.at[c], send_sem=send_sems.at[a], recv_sem=recv_sems.at[a],
            device_id=(x, y, 1 - c), device_id_type=MESH) for a, f in enumerate(f_refs)]
        for cp in copies:
            cp.start()
        for cp in copies:
            cp.wait()

    sem = pltpu.SemaphoreType.DMA((n,))
    return pl.pallas_call(
        body, name=name, in_specs=[ANY] * n, out_specs=[ANY] * n, input_output_aliases={i: i for i in range(n)},
        out_shape=[jax.ShapeDtypeStruct(f.shape, f.dtype) for f in fs], scratch_shapes=[sem, sem],
    )(*fs)


def _all_reduce_small(vec, *, name):
    n = vec.shape[1]

    def body(v_ref, o_ref, buf, send_sems, recv_sems):
        x, y, c = _coords()
        me = 4 * x + 2 * y + c
        buf[me] = v_ref[...]
        copies = []
        for m in range(1, 8):
            peer = (x ^ ((m >> 2) & 1), y ^ ((m >> 1) & 1), c ^ (m & 1))
            copies.append(pltpu.make_async_remote_copy(
                src_ref=v_ref, dst_ref=buf.at[me], send_sem=send_sems.at[m - 1], recv_sem=recv_sems.at[m - 1],
                device_id=peer, device_id_type=MESH))
        for cp in copies:
            cp.start()
        for cp in copies:
            cp.wait()
        acc = buf[0]
        for d in range(1, 8):
            acc = acc + buf[d]
        o_ref[...] = acc

    return pl.pallas_call(
        body, name=name, out_shape=jax.ShapeDtypeStruct((1, n), F32),
        in_specs=[pl.BlockSpec(memory_space=pltpu.VMEM)], out_specs=pl.BlockSpec(memory_space=pltpu.VMEM),
        scratch_shapes=[pltpu.VMEM((8, 1, n), F32), pltpu.SemaphoreType.DMA((7,)), pltpu.SemaphoreType.DMA((7,))],
    )(vec)


def _ffn_fwd(x, n_pre, n_post, wbuf, lay, tag, deps=()):
    wg, wu, wd = ((wbuf, i * lay.dff, lay.dff) for i in range(3))
    h = _norm_fwd(x, n_pre, name=f"{tag}_norm_pre", out_dtype=BF16)
    g = _mm([(h, wg)], name=f"{tag}_gate", mode="nt", deps=deps)
    u, a = _mm([(h, wu)], name=f"{tag}_up", mode="nt", extras=[g], out_dtypes=[F32, BF16], tm_cap=MM_TILE // 2,
               epilogue=lambda up, gate: (up, _silu(gate) * up))
    yv = _mm([(a, wd)], name=f"{tag}_down", mode="nn")
    out = _norm_fwd(yv, n_post, name=f"{tag}_norm_post", resid=x, scale=MACARON_SCALE)
    return out, (x, h, g, u, a, yv)


def _ffn_bwd(dout, saved, n_pre, n_post, wbuf, lay, tag, deps=(), after_act=None, after_dw=None):
    x, h, g, u, a, yv = saved
    dff = lay.dff
    gbuf = lax.empty((2, 3 * dff, lay.d // 2), F32)
    dy, dn_post = _norm_bwd(yv, n_post, dout, name=f"{tag}_norm_post_bwd", scale=MACARON_SCALE)
    dg, du = _mm([(dy, (wbuf, 2 * dff, dff))], name=f"{tag}_down_dx", mode="nt", deps=deps, extras=[g, u],
                 out_dtypes=[BF16, BF16], tm_cap=MM_TILE // 2,
                 epilogue=lambda da, gate, up: (da * up * _dsilu(gate), da * _silu(gate)))
    deps = after_act(du) if after_act is not None else ()
    gbuf = _mm([(a, dy)], name=f"{tag}_down_dw", mode="tn", into=(gbuf, 2 * dff), deps=deps)
    gbuf = _mm([(dg, h)], name=f"{tag}_gate_dw", mode="tn", into=(gbuf, 0))
    gbuf = _mm([(du, h)], name=f"{tag}_up_dw", mode="tn", into=(gbuf, dff))
    deps = after_dw(gbuf)
    dh = _mm([(dg, (wbuf, 0, dff)), (du, (wbuf, dff, dff))], name=f"{tag}_up_dx", mode="nn", deps=deps)
    dx, dn_pre = _norm_bwd(x, n_pre, dh, name=f"{tag}_norm_pre_bwd", dres=dout)
    return dx, dn_pre, dn_post


def _rope_tables(positions):
    half = ROPE // 2
    inv_freq = ROPE_THETA ** (-jnp.arange(half, dtype=F32) / half)
    ang = positions.astype(F32)[:, None] * inv_freq
    cos, sin = jnp.cos(ang), jnp.sin(ang)
    z = jnp.zeros_like(cos)
    z2 = jnp.zeros((positions.shape[0], LANE - ROPE), F32)
    return (jnp.concatenate([cos, cos, z2], axis=1), jnp.concatenate([-sin, z, z2], axis=1),
            jnp.concatenate([z, sin, z2], axis=1))


def kernel(x, positions, ffn1_norm_pre, ffn1_w_gate, ffn1_w_up, ffn1_w_down, ffn1_norm_post, mix_norm_pre, w_in, mla_q_norm, mla_w_q_up, mla_kv_norm, mla_w_kv_up, mla_w_o, hgrn_lb_logits, hgrn_out_norm, hgrn_w_o, w_out, mix_norm_post, ffn2_norm_pre, ffn2_w_gate, ffn2_w_up, ffn2_w_down, ffn2_norm_post, loss_target, m_ffn1_norm_pre, m_ffn1_w_gate, m_ffn1_w_up, m_ffn1_w_down, m_ffn1_norm_post, m_mix_norm_pre, m_w_in, m_mla_q_norm, m_mla_w_q_up, m_mla_kv_norm, m_mla_w_kv_up, m_mla_w_o, m_hgrn_lb_logits, m_hgrn_out_norm, m_hgrn_w_o, m_w_out, m_mix_norm_post, m_ffn2_norm_pre, m_ffn2_w_gate, m_ffn2_w_up, m_ffn2_w_down, m_ffn2_norm_post, v_ffn1_norm_pre, v_ffn1_w_gate, v_ffn1_w_up, v_ffn1_w_down, v_ffn1_norm_post, v_mix_norm_pre, v_w_in, v_mla_q_norm, v_mla_w_q_up, v_mla_kv_norm, v_mla_w_kv_up, v_mla_w_o, v_hgrn_lb_logits, v_hgrn_out_norm, v_hgrn_w_o, v_w_out, v_mix_norm_post, v_ffn2_norm_pre, v_ffn2_w_gate, v_ffn2_w_up, v_ffn2_w_down, v_ffn2_norm_post):
    given = dict(locals())
    wts = {n: given[n] for n in ALL_WEIGHTS}
    mom = {n: given["m_" + n] for n in ALL_WEIGHTS}
    var = {n: given["v_" + n] for n in ALL_WEIGHTS}
    xin = x[0]
    target = loss_target[0]
    t, d = xin.shape
    cx, cy, cc = _coords()

    q_lora, kv_lora = mla_q_norm.shape[1], mla_kv_norm.shape[1]
    nh_mla = 4 * mla_w_kv_up.shape[2] // QGROUP
    lay = _Layout(d, 4 * ffn1_w_gate.shape[2], 4 * w_in.shape[2], q_lora, kv_lora, nh_mla)
    jobs_ffn, jobs_mix = _ffn_jobs(lay), _mix_jobs(lay)
    def pack(src, col_sharded, row_sharded=()):
        a = jnp.concatenate([src[n][0].T.astype(BF16) for n in col_sharded] + [src[n][0].astype(BF16) for n in row_sharded])
        return a.reshape(a.shape[0], 2, a.shape[1] // 2).transpose(1, 0, 2)

    ffn_land = lambda: [lax.empty((2, 3 * lay.dff, d // 2), BF16)]
    got1, tok = _gather_start([pack(wts, ["ffn1_w_gate", "ffn1_w_up"], ["ffn1_w_down"])], ffn_land(), jobs_ffn, name="gather_ffn1")
    later, _ = lax.optimization_barrier(({n: wts[n] for n in BIG_WEIGHTS if not n.startswith("ffn1")}, tok))
    packs_mix = [pack(later, ["w_in"], ["mla_w_o", "hgrn_w_o", "w_out"]), pack(later, ["mla_w_q_up", "mla_w_kv_up"])]
    packs_ffn2 = [pack(later, ["ffn2_w_gate", "ffn2_w_up"], ["ffn2_w_down"])]
    lands_mix = [jnp.zeros((2, 10 * d, d // 2), BF16), jnp.zeros((2, lay.rows_narrow, q_lora // 2), BF16)]
    arrived, tok = _gather_wait(got1, packs_mix + packs_ffn2 + lands_mix, name="gather_ffn1_wait")
    got_m, tok = _gather_start(packs_mix, lands_mix, jobs_mix, name="gather_mix", deps=[tok])
    (w_ffn1,) = _gather_forward(arrived, jobs_ffn, name="gather_ffn1_forward", deps=[tok])
    col_kr = q_lora + kv_lora
    hgrn_cols = [d, 2 * d, 3 * d, 4 * d]
    col_ga, col_gb = 5 * d, 6 * d
    tabs = _rope_tables(positions[0])
    scale = (HEAD + ROPE) ** -0.5

    x1, saved1 = _ffn_fwd(xin, ffn1_norm_pre, ffn1_norm_post, w_ffn1, lay, "ffn1")

    arrived, tok = _gather_wait(got_m, [x1], name="gather_mix_wait")
    got2, tok = _gather_start(packs_ffn2, ffn_land(), jobs_ffn, name="gather_ffn2", deps=[tok])
    wide, narrow = _gather_forward(arrived, jobs_mix, name="gather_mix_forward", deps=[tok])
    w_in_v = (wide, 0, 7 * d)
    w_o_v = {n: (wide, (7 + i) * d, d) for i, n in enumerate(("mla_w_o", "hgrn_w_o", "w_out"))}
    w_q_v = (narrow, lay.off_q, nh_mla * QGROUP)
    w_kv_v = (narrow, lay.off_kv, nh_mla * QGROUP)

    h2 = _norm_fwd(x1, mix_norm_pre, name="mix_norm_pre", out_dtype=BF16)
    proj = _mm([(h2, w_in_v)], name="mix_in", mode="nt", deps=[tok])
    cqn = _norm_fwd(proj, mla_q_norm, name="mla_q_norm", out_dtype=BF16, col=0)
    ckvn = _norm_fwd(proj, mla_kv_norm, name="mla_kv_norm", out_dtype=BF16, col=q_lora)
    qp = _mm([(cqn, w_q_v)], name="mla_q_up", mode="nt")
    kvb = _mm([(ckvn, w_kv_v)], name="mla_kv_up", mode="nt", out_dtype=BF16)
    qcat = _rope(qp, tabs, name="rope_q", group=QGROUP, backward=False, out_dtype=BF16)
    krot = _rope(proj, tabs, name="rope_k", group=LANE, backward=False, out_dtype=BF16, col=col_kr, ngroup=1)
    o_mla = _attn_fwd(qcat, kvb, krot, name="mla_attention", scale=scale)
    y_a = _mm([(o_mla, w_o_v["mla_w_o"])], name="mla_out", mode="nn")

    o_raw, yb, states = _hgrn_fwd(proj, hgrn_cols, d, hgrn_lb_logits, hgrn_out_norm, name="hgrn_scan")
    y_b = _mm([(yb, w_o_v["hgrn_w_o"])], name="hgrn_out", mode="nn")

    merged = _merge_fwd(proj, col_ga, col_gb, y_a, y_b, name="mix_merge")
    y_mix = _mm([(merged, w_o_v["w_out"])], name="mix_out", mode="nn")
    x2 = _norm_fwd(y_mix, mix_norm_post, name="mix_norm_post", resid=x1, scale=1.0)

    (w_ffn2,) = _gather_forward(_gather_wait(got2, [x2], name="gather_ffn2_wait")[0], jobs_ffn, name="gather_ffn2_forward")
    x3, saved2 = _ffn_fwd(x2, ffn2_norm_pre, ffn2_norm_post, w_ffn2, lay, "ffn2")
    dx3, loss_local = _loss_head(x3, target, name="loss_head")

    grads, deltas, new_m, new_v = {}, {}, {}, {}
    sel = cc.astype(jnp.int32)
    sel1 = jnp.reshape(sel, (1,))
    me_chip = (2 * cx + cy).astype(jnp.int32)

    def reduce_mid(handle, after, jobs, tag):
        bufs, recvd = _swap_wait(handle, after, name=f"grad_swap_{tag}_wait")
        sums = [_add_sibling(b, r, sel1, name=f"grad_add_sibling_{tag}_{i}") for i, (b, r) in enumerate(zip(bufs, recvd))]
        return _exchange_start(sums, jobs, name=f"grad_exchange_{tag}")

    def reduce_end(handle, after, tag):
        sums, lands = _exchange_wait(handle, after, name=f"grad_exchange_{tag}_wait")
        parts = [_add_shard(sums[job.a], lands[job.a], job, sel, me_chip, name=f"grad_add_chips_{tag}_{i}")
                 for i, job in enumerate(handle["jobs"])]
        return _join_list(parts, name=f"grad_join_{tag}")

    def natural(part, lo, rows, transposed):
        g_n = part[:, lo:lo + rows]
        hw_n = g_n.shape[2]
        return g_n.transpose(0, 2, 1).reshape(2 * hw_n, rows) if transposed else g_n.transpose(1, 0, 2).reshape(rows, 2 * hw_n)

    def adam(names, deps=()):
        for i, n in enumerate(names):
            shp = wts[n].shape
            two_d = (lambda a: a[0]) if n in BIG_WEIGHTS else (lambda a: a)
            dl, nm, nv = _adamw(two_d(wts[n]), grads[n], two_d(mom[n]), two_d(var[n]), name=f"adamw_{n}",
                                deps=deps if i == 0 else ())
            grads[n] = grads[n].reshape(shp)
            deltas[n], new_m[n], new_v[n] = dl.reshape(shp), nm.reshape(shp), nv.reshape(shp)
        return [deltas[n] for n in names]

    def ffn_grads(joined, tag, deps=()):
        nff = lay.nff
        grads[f"{tag}_w_gate"] = natural(joined[0], 0, nff, True)
        grads[f"{tag}_w_up"] = natural(joined[0], nff, nff, True)
        grads[f"{tag}_w_down"] = natural(joined[0], 2 * nff, nff, False)
        return adam([f"{tag}_w_gate", f"{tag}_w_up", f"{tag}_w_down"], deps)

    swaps = {}

    def start_swap(tag):
        def hook(gbuf):
            swaps[tag], started = _swap_start([gbuf], name=f"grad_swap_{tag}")
            return [started]
        return hook

    dx2, grads["ffn2_norm_pre"], grads["ffn2_norm_post"] = _ffn_bwd(
        dx3, saved2, ffn2_norm_pre, ffn2_norm_post, w_ffn2, lay, "ffn2", after_dw=start_swap("ffn2"))

    gwide = lax.empty((2, 10 * d, d // 2), F32)
    gnarrow = lax.empty((2, lay.rows_narrow, q_lora // 2), F32)
    dy_mix, grads["mix_norm_post"] = _norm_bwd(y_mix, mix_norm_post, dx2, name="mix_norm_post_bwd")
    dmerged = _mm([(dy_mix, w_o_v["w_out"])], name="mix_out_dx", mode="nt")
    gwide = _mm([(merged, dy_mix)], name="mix_out_dw", mode="tn", into=(gwide, 9 * d))
    dga, dgb, dy_a, dy_b = _merge_bwd(dmerged, proj, col_ga, col_gb, y_a, y_b, name="mix_merge_bwd")

    do_mla = _mm([(dy_a, w_o_v["mla_w_o"])], name="mla_out_dx", mode="nt")
    gwide = _mm([(o_mla, dy_a)], name="mla_out_dw", mode="tn", into=(gwide, 7 * d))
    dqcat, dkv, dkr = _attn_bwd(qcat, kvb, krot, do_mla, name="mla_attention_bwd", scale=scale)
    exch2, tok = reduce_mid(swaps["ffn2"], [dkr], _ffn_jobs(lay), "ffn2")

    dqp = _rope(dqcat, tabs, name="rope_q_bwd", group=QGROUP, backward=True, out_dtype=BF16)
    dk_r = _rope(dkr, tabs, name="rope_k_bwd", group=LANE, backward=True, out_dtype=BF16)
    dcqn = _mm([(dqp, w_q_v)], name="mla_q_up_dx", mode="nn", deps=[tok])
    gnarrow = _mm([(dqp, cqn)], name="mla_q_up_dw", mode="tn", into=(gnarrow, lay.off_q))
    dkvb = dkv.astype(BF16)
    dckvn = _mm([(dkvb, w_kv_v)], name="mla_kv_up_dx", mode="nn")
    gnarrow = _mm([(dkvb, ckvn)], name="mla_kv_up_dw", mode="tn", into=(gnarrow, lay.off_kv))
    dc_q, grads["mla_q_norm"] = _norm_bwd(proj, mla_q_norm, dcqn, name="mla_q_norm_bwd", col=0, dx_dtype=BF16)
    dc_kv, grads["mla_kv_norm"] = _norm_bwd(proj, mla_kv_norm, dckvn, name="mla_kv_norm_bwd", col=q_lora, dx_dtype=BF16)

    dyb = _mm([(dy_b, w_o_v["hgrn_w_o"])], name="hgrn_out_dx", mode="nt")
    gwide = _mm([(yb, dy_b)], name="hgrn_out_dw", mode="tn", into=(gwide, 8 * d))
    dhq, dhf, dhi, dhg, dlb_h, dnorm_h = _hgrn_bwd(proj, hgrn_cols, d, o_raw, dyb, states, hgrn_lb_logits, hgrn_out_norm,
                                                   name="hgrn_scan_bwd")

    dproj = jnp.concatenate([dc_q, dc_kv, dk_r, jnp.zeros((t, d - col_kr - LANE), BF16), dhq, dhf, dhi, dhg, dga, dgb], axis=1)
    dh2 = _mm([(dproj, w_in_v)], name="mix_in_dx", mode="nn")
    gwide = _mm([(dproj, h2)], name="mix_in_dw", mode="tn", into=(gwide, 0))
    dx1, grads["mix_norm_pre"] = _norm_bwd(x1, mix_norm_pre, dh2, name="mix_norm_pre_bwd", dres=dx2)
    swap_m, tok = _swap_start([gwide, gnarrow], name="grad_swap_mix")
    joined2 = reduce_end(exch2, [dx1], "ffn2")

    exchanges = {}

    def mix_exchange(after):
        exchanges["mix"], started = reduce_mid(swap_m, [after], _mix_jobs(lay), "mix")
        return [started]

    dx0, grads["ffn1_norm_pre"], grads["ffn1_norm_post"] = _ffn_bwd(
        dx1, saved1, ffn1_norm_pre, ffn1_norm_post, w_ffn1, lay, "ffn1", deps=[tok], after_act=mix_exchange,
        after_dw=start_swap("ffn1"))
    exch1, tok = reduce_mid(swaps["ffn1"], [dx0], _ffn_jobs(lay), "ffn1")

    joined_m = reduce_end(exchanges["mix"], [dx0, tok], "mix")
    done = ffn_grads(joined2, "ffn2")
    grads["w_in"] = natural(jnp.concatenate([joined_m[0], joined_m[1]], axis=1), 0, lay.ncol, True)
    for i, n in enumerate(("mla_w_o", "hgrn_w_o", "w_out")):
        grads[n] = natural(joined_m[2], i * lay.r_o, lay.r_o, False)
    grads["mla_w_q_up"] = natural(joined_m[3], 0, lay.hps * (HEAD + ROPE), True)
    grads["mla_w_kv_up"] = natural(joined_m[4], 0, lay.hps * QGROUP, True)
    done += adam(["w_in", "mla_w_q_up", "mla_w_kv_up", "mla_w_o", "hgrn_w_o", "w_out"])

    dlb = dlb_h.reshape(1, -1)
    dnorm = jnp.sum(dnorm_h, axis=0)
    small = {**{n: grads[n] for n in SMALL_WEIGHTS if n not in ("hgrn_lb_logits", "hgrn_out_norm")},
             "hgrn_lb_logits": dlb, "hgrn_out_norm": dnorm}
    vec = jnp.concatenate([small[n] for n in SMALL_WEIGHTS], axis=1)
    vec = _all_reduce_small(vec, name="grad_all_reduce_small")
    off = 0
    for n in SMALL_WEIGHTS:
        w_n = small[n].shape[1]
        grads[n] = vec[:, off:off + w_n]
        off += w_n
    grads["hgrn_lb_logits"] = _lb_logits_grad(hgrn_lb_logits, grads["hgrn_lb_logits"], name="lb_logits_grad")

    done += adam(list(SMALL_WEIGHTS))
    ffn_grads(reduce_end(exch1, done, "ffn1"), "ffn1")

    loss = lax.psum(loss_local, ("x", "y", "c"))
    dx_out = dx0.reshape(x.shape)
    return (loss, dx_out, *[grads[n] for n in ALL_WEIGHTS], *[deltas[n] for n in ALL_WEIGHTS],
            *[new_m[n] for n in ALL_WEIGHTS], *[new_v[n] for n in ALL_WEIGHTS])
```

```python
import functools

import jax
import jax.numpy as jnp
from jax import lax
from jax.experimental import pallas as pl
from jax.experimental.pallas import tpu as pltpu

F32 = jnp.float32
BF16 = jnp.bfloat16
MESH = pl.DeviceIdType.MESH

NORM_EPS = 1e-6
MACARON_SCALE = 0.5
ROPE_THETA = 10000.0
HEAD = 128
ROPE = 64
QGROUP = 2 * HEAD
SUB = 16
ADAM_LR, ADAM_B1, ADAM_B2, ADAM_EPS, ADAM_WD, ADAM_STEP = 0.001, 0.9, 0.999, 1e-08, 0.01, 10

LANE = 128
VMEM_LIMIT = 48 * 1024 * 1024
MM_TILE = 1024
MM_TILE_WIDE = 1536

BIG_WEIGHTS = ("ffn1_w_gate", "ffn1_w_up", "ffn1_w_down", "w_in", "mla_w_q_up", "mla_w_kv_up",
               "mla_w_o", "hgrn_w_o", "w_out", "ffn2_w_gate", "ffn2_w_up", "ffn2_w_down")
COL_SHARDED = ("ffn1_w_gate", "ffn1_w_up", "w_in", "mla_w_q_up", "mla_w_kv_up", "ffn2_w_gate", "ffn2_w_up")
SMALL_WEIGHTS = ("ffn1_norm_pre", "ffn1_norm_post", "mix_norm_pre", "mla_q_norm", "mla_kv_norm",
                 "hgrn_lb_logits", "hgrn_out_norm", "mix_norm_post", "ffn2_norm_pre", "ffn2_norm_post")
ALL_WEIGHTS = ("ffn1_norm_pre", "ffn1_w_gate", "ffn1_w_up", "ffn1_w_down", "ffn1_norm_post", "mix_norm_pre",
               "w_in", "mla_q_norm", "mla_w_q_up", "mla_kv_norm", "mla_w_kv_up", "mla_w_o", "hgrn_lb_logits",
               "hgrn_out_norm", "hgrn_w_o", "w_out", "mix_norm_post", "ffn2_norm_pre", "ffn2_w_gate",
               "ffn2_w_up", "ffn2_w_down", "ffn2_norm_post")


def _params(*sem):
    return pltpu.CompilerParams(dimension_semantics=sem or None, vmem_limit_bytes=VMEM_LIMIT)


def _pick(n, cap, offset=0):
    if n <= cap and offset % n == 0:
        return n
    best = None
    for t in range(LANE, min(n, cap) + 1, LANE):
        if n % t == 0 and offset % t == 0:
            best = t
    assert best is not None, (n, cap, offset)
    return best


def _row_tile(n, row_bytes, budget=1 << 20):
    best = None
    for t in range(8, n + 1, 8):
        if n % t == 0 and t * row_bytes <= budget:
            best = t
    return n if best is None else best


def _sigmoid(x):
    return 1.0 / (1.0 + jnp.exp(-x))


def _silu(x):
    return x * _sigmoid(x)


def _dsilu(x):
    s = _sigmoid(x)
    return s * (1.0 + x * (1.0 - s))


def _mm(pairs, *, name, mode="nn", out_dtype=F32, into=None, deps=(), extras=(), epilogue=None, out_dtypes=None, tm_cap=None):
    halves = isinstance(pairs[0][1], tuple)
    assert halves or mode == "tn"
    pairs = [(a, b if halves else (b, 0, b.shape[0])) for a, b in pairs]
    a0, (b0, b_off, b_rows) = pairs[0]
    hw = b0.shape[2] if halves else (into[0].shape[2] if into is not None else None)
    if mode == "nn":
        (m, kdim), n = a0.shape, 2 * hw
    elif mode == "nt":
        (m, kdim), n = a0.shape, b_rows
        assert kdim == 2 * hw
    else:
        (kdim, m), n = a0.shape, b0.shape[1]
    out_off = 0 if into is None else into[1]
    tm = _pick(m, tm_cap or (MM_TILE_WIDE if mode == "tn" else MM_TILE), out_off)
    tn = hw if (mode == "nn" or into is not None) else _pick(n, MM_TILE_WIDE, b_off if mode == "nt" else 0)
    tk = hw if mode == "nt" else _pick(kdim, MM_TILE if len(pairs) <= 2 else MM_TILE // 2, b_off if mode == "nn" else 0)
    assert n % tn == 0 and kdim % tk == 0
    nk = kdim // tk
    npair = len(pairs)
    dims = {"nn": (((1,), (0,)), ((), ())), "nt": (((1,), (1,)), ((), ())), "tn": (((0,), (0,)), ((), ()))}[mode]

    nout = 1 if epilogue is None else len(out_dtypes)

    def body(*refs):
        ins, x_refs = refs[:2 * npair], refs[2 * npair:2 * npair + len(extras)]
        o_refs, acc_ref = refs[-1 - nout:-1], refs[-1]
        k = pl.program_id(2)

        @pl.when(k == 0)
        def _():
            acc_ref[...] = jnp.zeros_like(acc_ref)

        for p in range(npair):
            a = ins[2 * p][...].astype(BF16)
            b = ins[2 * p + 1][...].astype(BF16)
            acc_ref[...] += lax.dot_general(a, b, dims, preferred_element_type=F32)

        @pl.when(k == nk - 1)
        def _():
            outs = (acc_ref[...],) if epilogue is None else epilogue(acc_ref[...], *[x[...] for x in x_refs])
            for o_ref, o in zip(o_refs, outs):
                o_ref[...] = o.astype(o_ref.dtype)

    a_spec = pl.BlockSpec((tk, tm), lambda i, j, k: (k, i)) if mode == "tn" else pl.BlockSpec((tm, tk), lambda i, j, k: (i, k))
    in_specs, flat = [], []
    for a, (b, off, _) in pairs:
        if mode == "nt":
            b_spec = pl.BlockSpec((None, tn, tk), lambda i, j, k, o=off // tn: (k, j + o, 0))
        elif mode == "nn":
            b_spec = pl.BlockSpec((None, tk, tn), lambda i, j, k, o=off // tk: (j, k + o, 0))
        else:
            b_spec = pl.BlockSpec((tk, tn), lambda i, j, k: (k, j))
        in_specs += [a_spec, b_spec]
        flat += [a, b]
    for extra in extras:
        in_specs.append(pl.BlockSpec((tm, tn), lambda i, j, k: (i, j)))
        flat.append(extra)
    for dep in deps:
        in_specs.append(pl.BlockSpec(memory_space=pl.ANY))
        flat.append(dep)
    if epilogue is not None:
        assert into is None
        out_shape, aliases = [jax.ShapeDtypeStruct((m, n), dt) for dt in out_dtypes], {}
        out_spec = [pl.BlockSpec((tm, tn), lambda i, j, k: (i, j))] * nout
    elif into is None:
        out_shape, aliases = jax.ShapeDtypeStruct((m, n), out_dtype), {}
        out_spec = pl.BlockSpec((tm, tn), lambda i, j, k: (i, j))
    else:
        out_shape, aliases = jax.ShapeDtypeStruct(into[0].shape, into[0].dtype), {len(flat): 0}
        out_spec = pl.BlockSpec((None, tm, tn), lambda i, j, k, o=out_off // tm: (j, i + o, 0))
        in_specs.append(pl.BlockSpec(memory_space=pl.ANY))
        flat.append(into[0])
    return pl.pallas_call(
        body, name=name, grid=(m // tm, n // tn, nk),
        in_specs=in_specs,
        out_specs=out_spec,
        out_shape=out_shape, input_output_aliases=aliases,
        scratch_shapes=[pltpu.VMEM((tm, tn), F32)],
        compiler_params=_params("parallel", "parallel", "arbitrary"),
    )(*flat)


def _norm_fwd(y, w, *, name, resid=None, scale=1.0, out_dtype=F32, col=0):
    t, d = y.shape[0], w.shape[1]
    tr = _pick(t, 256)
    assert col % d == 0

    def body(*refs):
        if resid is None:
            y_ref, w_ref, o_ref = refs
        else:
            y_ref, w_ref, r_ref, o_ref = refs
        yv = y_ref[...]
        out = yv * lax.rsqrt(jnp.mean(yv * yv, axis=-1, keepdims=True) + NORM_EPS) * w_ref[...]
        if resid is not None:
            out = r_ref[...] + scale * out
        o_ref[...] = out.astype(out_dtype)

    row = pl.BlockSpec((tr, d), lambda i: (i, 0))
    wspec = pl.BlockSpec((1, d), lambda i: (0, 0))
    ins, specs = [y, w], [pl.BlockSpec((tr, d), lambda i: (i, col // d)), wspec]
    if resid is not None:
        ins.append(resid)
        specs.append(row)
    return pl.pallas_call(
        body, name=name, grid=(t // tr,), in_specs=specs, out_specs=row,
        out_shape=jax.ShapeDtypeStruct((t, d), out_dtype), compiler_params=_params("parallel"),
    )(*ins)


def _norm_bwd(x, w, dy, *, name, scale=1.0, dres=None, col=0, dx_dtype=F32):
    t, d = x.shape[0], w.shape[1]
    tr = _pick(t, 256)
    assert col % d == 0

    def body(*refs):
        if dres is None:
            x_ref, w_ref, dy_ref, dx_ref, dw_ref = refs
        else:
            x_ref, w_ref, dy_ref, dr_ref, dx_ref, dw_ref = refs

        @pl.when(pl.program_id(0) == 0)
        def _():
            dw_ref[...] = jnp.zeros_like(dw_ref)

        xv = x_ref[...]
        r = lax.rsqrt(jnp.mean(xv * xv, axis=-1, keepdims=True) + NORM_EPS)
        xhat = xv * r
        dyv = dy_ref[...].astype(F32) * scale
        dw_ref[...] += jnp.sum(dyv * xhat, axis=0, keepdims=True)
        t_ = dyv * w_ref[...]
        dx = r * (t_ - xhat * jnp.mean(t_ * xhat, axis=-1, keepdims=True))
        if dres is not None:
            dx = dx + dr_ref[...]
        dx_ref[...] = dx.astype(dx_dtype)

    row = pl.BlockSpec((tr, d), lambda i: (i, 0))
    wspec = pl.BlockSpec((1, d), lambda i: (0, 0))
    ins, specs = [x, w, dy], [pl.BlockSpec((tr, d), lambda i: (i, col // d)), wspec, row]
    if dres is not None:
        ins.append(dres)
        specs.append(row)
    return pl.pallas_call(
        body, name=name, grid=(t // tr,), in_specs=specs, out_specs=(row, wspec),
        out_shape=(jax.ShapeDtypeStruct((t, d), dx_dtype), jax.ShapeDtypeStruct((1, d), F32)),
        compiler_params=_params("arbitrary"),
    )(*ins)


def _elementwise(fn, ins, out_dtypes, *, name, width=None, cols=None):
    t = ins[0].shape[0]
    d = ins[0].shape[1] if width is None else width
    cols = [0] * len(ins) if cols is None else cols
    tc = _pick(d, 2048)
    for c in cols:
        tc = _pick(d, tc, c)
    tr = _row_tile(t, tc * 4)
    nout = len(out_dtypes)

    def body(*refs):
        outs = fn(*[r[...].astype(F32) for r in refs[:len(ins)]])
        for o_ref, o in zip(refs[len(ins):], outs):
            o_ref[...] = o.astype(o_ref.dtype)

    spec = pl.BlockSpec((tr, tc), lambda i, j: (i, j))
    in_specs = [pl.BlockSpec((tr, tc), lambda i, j, o=c // tc: (i, j + o)) for c in cols]
    return pl.pallas_call(
        body, name=name, grid=(t // tr, d // tc), in_specs=in_specs, out_specs=[spec] * nout,
        out_shape=[jax.ShapeDtypeStruct((t, d), dt) for dt in out_dtypes],
        compiler_params=_params("parallel", "parallel"),
    )(*ins)


def _merge_fwd(proj, col_a, col_b, ya, yb, *, name):
    return _elementwise(lambda a, b, p, q: (_sigmoid(a) * p + _sigmoid(b) * q,), [proj, proj, ya, yb], [BF16],
                        name=name, width=ya.shape[1], cols=[col_a, col_b, 0, 0])[0]


def _merge_bwd(dm, proj, col_a, col_b, ya, yb, *, name):
    def fn(dmv, a, b, p, q):
        sa, sb = _sigmoid(a), _sigmoid(b)
        return dmv * p * sa * (1.0 - sa), dmv * q * sb * (1.0 - sb), dmv * sa, dmv * sb

    return _elementwise(fn, [dm, proj, proj, ya, yb], [BF16, BF16, BF16, BF16], name=name, width=ya.shape[1],
                        cols=[0, col_a, col_b, 0, 0])


def _loss_head(xo, target, *, name):
    t, d = xo.shape
    tr = _pick(t, 256)

    def body(x_ref, t_ref, dx_ref, l_ref):
        @pl.when(pl.program_id(0) == 0)
        def _():
            l_ref[...] = jnp.zeros_like(l_ref)

        err = x_ref[...] - t_ref[...]
        dx_ref[...] = err * (1.0 / d)
        l_ref[...] += 0.5 * jnp.sum(jnp.mean(err * err, axis=-1, keepdims=True), axis=0, keepdims=True)

    row = pl.BlockSpec((tr, d), lambda i: (i, 0))
    dx, l = pl.pallas_call(
        body, name=name, grid=(t // tr,), in_specs=[row, row],
        out_specs=(row, pl.BlockSpec((1, 1), lambda i: (0, 0))),
        out_shape=(jax.ShapeDtypeStruct((t, d), F32), jax.ShapeDtypeStruct((1, 1), F32)),
        compiler_params=_params("arbitrary"),
    )(xo, target)
    return dx, l[0, 0]


def _rope(xin, tabs, *, name, group, backward, out_dtype, col=0, ngroup=None):
    t = xin.shape[0]
    ngroup = xin.shape[1] // group if ngroup is None else ngroup
    wdt = ngroup * group
    tr = _pick(t, 256)
    assert col % wdt == 0
    cos_t, nsin_t, sin_t = tabs

    def body(x_ref, c_ref, n_ref, s_ref, o_ref):
        cv, nv, sv = c_ref[...], n_ref[...], s_ref[...]
        for g in range(ngroup):
            lo, hi = g * group, (g + 1) * group
            rot = x_ref[:, hi - LANE:hi].astype(F32)
            if backward:
                out = rot * cv + pltpu.roll(rot * nv, 32, 1) + pltpu.roll(rot * sv, LANE - 32, 1)
            else:
                out = rot * cv + pltpu.roll(rot, LANE - 32, 1) * nv + pltpu.roll(rot, 32, 1) * sv
            if group > LANE:
                o_ref[:, lo:hi - LANE] = x_ref[:, lo:hi - LANE].astype(out_dtype)
            o_ref[:, hi - LANE:hi] = out.astype(out_dtype)

    xspec = pl.BlockSpec((tr, wdt), lambda i: (i, 0))
    tspec = pl.BlockSpec((tr, LANE), lambda i: (i, 0))
    return pl.pallas_call(
        body, name=name, grid=(t // tr,),
        in_specs=[pl.BlockSpec((tr, wdt), lambda i: (i, col // wdt)), tspec, tspec, tspec], out_specs=xspec,
        out_shape=jax.ShapeDtypeStruct((t, wdt), out_dtype), compiler_params=_params("parallel"),
    )(xin, cos_t, nsin_t, sin_t)


def _scores(q, kv, kr, qi, tq, scale):
    kcat = jnp.concatenate([kv[:, :HEAD], kr], axis=1)
    s = lax.dot_general(q, kcat, (((1,), (1,)), ((), ())), preferred_element_type=F32) * scale
    row = qi * tq + lax.broadcasted_iota(jnp.int32, s.shape, 0)
    col = lax.broadcasted_iota(jnp.int32, s.shape, 1)
    s = jnp.where(col <= row, s, -jnp.inf)
    p = jnp.exp(s - jnp.max(s, axis=-1, keepdims=True))
    return p / jnp.sum(p, axis=-1, keepdims=True), kcat


def _attn_fwd(qcat, kv, kr, *, name, scale):
    t = qcat.shape[0]
    nh = qcat.shape[1] // QGROUP
    tq = _pick(t, 256)

    def body(q_ref, kv_ref, kr_ref, o_ref):
        for qi in range(t // tq):
            @pl.when(pl.program_id(1) == qi)
            def _(qi=qi):
                kvv = kv_ref[0:(qi + 1) * tq, :]
                p, _ = _scores(q_ref[...], kvv, kr_ref[0:(qi + 1) * tq, :], qi, tq, scale)
                o_ref[...] = jnp.dot(p.astype(BF16), kvv[:, HEAD:], preferred_element_type=F32).astype(BF16)

    return pl.pallas_call(
        body, name=name, grid=(nh, t // tq),
        in_specs=[pl.BlockSpec((tq, QGROUP), lambda h, i: (i, h)), pl.BlockSpec((t, QGROUP), lambda h, i: (0, h)),
                  pl.BlockSpec((t, LANE), lambda h, i: (0, 0))],
        out_specs=pl.BlockSpec((tq, HEAD), lambda h, i: (i, h)),
        out_shape=jax.ShapeDtypeStruct((t, nh * HEAD), BF16), compiler_params=_params("parallel", "parallel"),
    )(qcat, kv, kr)


def _attn_bwd(qcat, kv, kr, do, *, name, scale):
    t = qcat.shape[0]
    nh = qcat.shape[1] // QGROUP
    tq = _pick(t, 256)
    nq = t // tq

    def body(q_ref, kv_ref, kr_ref, do_ref, dq_ref, dkv_ref, dkr_ref, dk_acc, dv_acc):
        h, i = pl.program_id(0), pl.program_id(1)

        @pl.when(i == 0)
        def _():
            dk_acc[...] = jnp.zeros_like(dk_acc)
            dv_acc[...] = jnp.zeros_like(dv_acc)

        @pl.when((i == 0) & (h == 0))
        def _():
            dkr_ref[...] = jnp.zeros_like(dkr_ref)

        for qi in range(nq):
            @pl.when(i == qi)
            def _(qi=qi):
                keys = slice(0, (qi + 1) * tq)
                q = q_ref[...]
                kvv = kv_ref[keys, :]
                dov = do_ref[...].astype(BF16)
                p, kcat = _scores(q, kvv, kr_ref[keys, :], qi, tq, scale)
                dp = lax.dot_general(dov, kvv[:, HEAD:], (((1,), (1,)), ((), ())), preferred_element_type=F32)
                ds = (p * (dp - jnp.sum(p * dp, axis=-1, keepdims=True)) * scale).astype(BF16)
                dq_ref[...] = jnp.dot(ds, kcat, preferred_element_type=F32)
                dk_acc[keys, :] += lax.dot_general(ds, q, (((0,), (0,)), ((), ())), preferred_element_type=F32)
                dv_acc[keys, :] += lax.dot_general(p.astype(BF16), dov, (((0,), (0,)), ((), ())), preferred_element_type=F32)

        @pl.when(i == nq - 1)
        def _():
            dk = dk_acc[...]
            dkv_ref[...] = jnp.concatenate([dk[:, :HEAD], dv_acc[...]], axis=1)
            dkr_ref[...] += dk[:, HEAD:]

    return pl.pallas_call(
        body, name=name, grid=(nh, nq),
        in_specs=[pl.BlockSpec((tq, QGROUP), lambda h, i: (i, h)), pl.BlockSpec((t, QGROUP), lambda h, i: (0, h)),
                  pl.BlockSpec((t, LANE), lambda h, i: (0, 0)), pl.BlockSpec((tq, HEAD), lambda h, i: (i, h))],
        out_specs=(pl.BlockSpec((tq, QGROUP), lambda h, i: (i, h)), pl.BlockSpec((t, QGROUP), lambda h, i: (0, h)),
                   pl.BlockSpec((t, LANE), lambda h, i: (0, 0))),
        out_shape=(jax.ShapeDtypeStruct((t, nh * QGROUP), F32), jax.ShapeDtypeStruct((t, nh * QGROUP), F32),
                   jax.ShapeDtypeStruct((t, LANE), F32)),
        scratch_shapes=[pltpu.VMEM((t, QGROUP), F32), pltpu.VMEM((t, HEAD), F32)],
        compiler_params=_params("arbitrary", "arbitrary"),
    )(qcat, kv, kr, do)


def _split3(x):
    hi = x.astype(BF16)
    r1 = x - hi.astype(F32)
    mid = r1.astype(BF16)
    lo = (r1 - mid.astype(F32)).astype(BF16)
    return hi, mid, lo


def _tri_matmul(mask, x):
    m = mask.astype(BF16)
    return sum(jnp.dot(m, part, preferred_element_type=F32) for part in _split3(x))


def _sub_cumsum(g, tb):
    row = lax.broadcasted_iota(jnp.int32, (tb, tb), 0)
    col = lax.broadcasted_iota(jnp.int32, (tb, tb), 1)
    return _tri_matmul(jnp.where((col <= row) & (col // SUB == row // SUB), 1.0, 0.0), g)


def _sub_suffix_prefix(after, before, tb):
    row = lax.broadcasted_iota(jnp.int32, (tb, tb), 0)
    col = lax.broadcasted_iota(jnp.int32, (tb, tb), 1)
    same = col // SUB == row // SUB
    return (_tri_matmul(jnp.where((col >= row) & same, 1.0, 0.0), after)
            + _tri_matmul(jnp.where((col < row) & same, 1.0, 0.0), before))


def _lower_bound(logits):
    mx = jnp.max(logits, axis=0, keepdims=True)
    e = jnp.exp(logits - mx)
    return e[0:1, :] / jnp.sum(e, axis=0, keepdims=True)


def _hgrn_fwd(proj, cols, wdt, logits, out_norm, *, name):
    t = proj.shape[0]
    nh = wdt // HEAD
    tb = _pick(t, 128)
    ns = tb // SUB

    def body(hq_ref, hf_ref, hi_ref, hg_ref, lg_ref, w_ref, o_ref, yb_ref, st_ref, s_ref, q_s, k_s, b_s):
        @pl.when(pl.program_id(1) == 0)
        def _():
            s_ref[...] = jnp.zeros_like(s_ref)

        lb = _lower_bound(lg_ref[...])
        f = lb + (1.0 - lb) * _sigmoid(hf_ref[...])
        q_s[...] = _silu(hq_ref[...])
        k_s[...] = 1.0 - f
        b_s[...] = _sub_cumsum(jnp.log(f), tb)
        rowid = lax.broadcasted_iota(jnp.int32, (SUB, HEAD), 0)

        def sub(c, st):
            rows = pl.ds(pl.multiple_of(c * SUB, SUB), SUB)
            qc, kc, bc, vc = q_s[rows, :], k_s[rows, :], b_s[rows, :], hi_ref[rows, :]
            st_ref[0, c] = st
            bl = bc[SUB - 1:SUB, :]
            oc = lax.dot_general((qc * jnp.exp(bc)).astype(BF16), st.astype(BF16), (((1,), (1,)), ((), ())),
                                 preferred_element_type=F32)
            for s in range(SUB):
                e = jnp.where(rowid >= s, jnp.exp(bc - bc[s:s + 1, :]), 0.0)
                a = jnp.sum(qc * e * kc[s:s + 1, :], axis=1, keepdims=True)
                oc = oc + a * vc[s:s + 1, :]
            o_ref[rows, :] = oc
            kd = kc * jnp.exp(bl - bc)
            return jnp.exp(bl) * st + lax.dot_general(vc.astype(BF16), kd.astype(BF16), (((0,), (0,)), ((), ())),
                                                      preferred_element_type=F32)

        s_ref[...] = lax.fori_loop(0, ns, sub, s_ref[...], unroll=True)
        o = o_ref[...]
        r = lax.rsqrt(jnp.mean(o * o, axis=-1, keepdims=True) + NORM_EPS)
        yb_ref[...] = (o * r * w_ref[...] * _silu(hg_ref[...])).astype(BF16)

    blk = pl.BlockSpec((tb, HEAD), lambda h, j: (j, h))
    return pl.pallas_call(
        body, name=name, grid=(nh, t // tb),
        in_specs=[pl.BlockSpec((tb, HEAD), lambda h, j, o=c // HEAD: (j, h + o)) for c in cols]
        + [pl.BlockSpec((2, HEAD), lambda h, j: (0, h)), pl.BlockSpec((1, HEAD), lambda h, j: (0, 0))],
        out_specs=(blk, blk, pl.BlockSpec((1, ns, HEAD, HEAD), lambda h, j: (h, j, 0, 0))),
        out_shape=(jax.ShapeDtypeStruct((t, wdt), F32), jax.ShapeDtypeStruct((t, wdt), BF16),
                   jax.ShapeDtypeStruct((nh, t // SUB, HEAD, HEAD), F32)),
        scratch_shapes=[pltpu.VMEM((HEAD, HEAD), F32)] + [pltpu.VMEM((tb, HEAD), F32)] * 3,
        compiler_params=_params("parallel", "arbitrary"),
    )(proj, proj, proj, proj, logits, out_norm)


def _hgrn_bwd(proj, cols, wdt, o_raw, dyb, states, logits, out_norm, *, name):
    t = proj.shape[0]
    nh = wdt // HEAD
    tb = _pick(t, 128)
    ns = tb // SUB
    nb = t // tb

    def body(hq_ref, hf_ref, hi_ref, hg_ref, o_ref, dy_ref, st_ref, lg_ref, w_ref,
             dhq_ref, dhf_ref, dhi_ref, dhg_ref, dlb_ref, dw_ref,
             ds_ref, q_s, k_s, b_s, do_s, dq_s, dk_s, dv_s, after_s, before_s, thru_s):
        @pl.when(pl.program_id(1) == 0)
        def _():
            ds_ref[...] = jnp.zeros_like(ds_ref)
            dlb_ref[...] = jnp.zeros_like(dlb_ref)
            dw_ref[...] = jnp.zeros_like(dw_ref)

        lb = _lower_bound(lg_ref[...])
        hqv, hgv = hq_ref[...], hg_ref[...]
        sig = _sigmoid(hf_ref[...])
        f = lb + (1.0 - lb) * sig
        q_s[...] = _silu(hqv)
        k_s[...] = 1.0 - f
        b_s[...] = _sub_cumsum(jnp.log(f), tb)

        o = o_ref[...]
        r = lax.rsqrt(jnp.mean(o * o, axis=-1, keepdims=True) + NORM_EPS)
        nrm = o * r
        w = w_ref[...]
        dy = dy_ref[...].astype(F32)
        dhg_ref[...] = (dy * nrm * w * _dsilu(hgv)).astype(BF16)
        dnw = dy * _silu(hgv)
        dw_ref[0] += jnp.sum(dnw * nrm, axis=0, keepdims=True)
        tt = dnw * w
        do_s[...] = r * (tt - nrm * jnp.mean(tt * nrm, axis=-1, keepdims=True))
        rowid = lax.broadcasted_iota(jnp.int32, (SUB, HEAD), 0)

        def sub(cc, dst):
            c = ns - 1 - cc
            rows = pl.ds(pl.multiple_of(c * SUB, SUB), SUB)
            qc, kc, bc, vc, doc = q_s[rows, :], k_s[rows, :], b_s[rows, :], hi_ref[rows, :], do_s[rows, :]
            st = st_ref[0, c]
            bl = bc[SUB - 1:SUB, :]
            eb = jnp.exp(bc)
            ekd = jnp.exp(bl - bc)
            qe, kd = qc * eb, kc * ekd
            dob, vcb = doc.astype(BF16), vc.astype(BF16)
            dq_st = jnp.dot(dob, st.astype(BF16), preferred_element_type=F32) * eb
            dk_st = jnp.dot(vcb, dst.astype(BF16), preferred_element_type=F32) * ekd
            dv = lax.dot_general(kd.astype(BF16), dst.astype(BF16), (((1,), (1,)), ((), ())), preferred_element_type=F32)
            dq_in = jnp.zeros_like(qc)
            dk_in = jnp.zeros_like(qc)
            for s in range(SUB):
                e = jnp.where(rowid >= s, jnp.exp(bc - bc[s:s + 1, :]), 0.0)
                ek = e * kc[s:s + 1, :]
                a = jnp.sum(qc * ek, axis=1, keepdims=True)
                da = jnp.sum(doc * vc[s:s + 1, :], axis=1, keepdims=True)
                dq_in = dq_in + da * ek
                dk_in = dk_in + jnp.where(rowid == s, jnp.sum(da * e * qc, axis=0, keepdims=True), 0.0)
                dv = dv + jnp.where(rowid == s, jnp.sum(a * doc, axis=0, keepdims=True), 0.0)
            ebl = jnp.exp(bl)
            dq_s[rows, :] = dq_st + dq_in
            dk_s[rows, :] = dk_st + dk_in
            dv_s[rows, :] = dv
            after_s[rows, :] = qc * (dq_st + dq_in) - kc * dk_in
            before_s[rows, :] = kc * dk_st
            thru_s[rows, :] = jnp.broadcast_to(ebl * jnp.sum(st * dst, axis=0, keepdims=True), (SUB, HEAD))
            return ebl * dst + lax.dot_general(dob, qe.astype(BF16), (((0,), (0,)), ((), ())), preferred_element_type=F32)

        ds_ref[...] = lax.fori_loop(0, ns, sub, ds_ref[...], unroll=True)
        dg = _sub_suffix_prefix(after_s[...], before_s[...], tb) + thru_s[...]
        dhq_ref[...] = (dq_s[...] * _dsilu(hqv)).astype(BF16)
        dft = dg / f - dk_s[...]
        dhf_ref[...] = (dft * (1.0 - lb) * sig * (1.0 - sig)).astype(BF16)
        dlb_ref[0] += jnp.sum(dft * (1.0 - sig), axis=0, keepdims=True)
        dhi_ref[...] = dv_s[...].astype(BF16)

    blk = pl.BlockSpec((tb, HEAD), lambda h, j: (nb - 1 - j, h))
    vec = pl.BlockSpec((1, 1, HEAD), lambda h, j: (h, 0, 0))
    tok = jax.ShapeDtypeStruct((t, wdt), BF16)
    per_head = jax.ShapeDtypeStruct((nh, 1, HEAD), F32)
    return pl.pallas_call(
        body, name=name, grid=(nh, nb),
        in_specs=[pl.BlockSpec((tb, HEAD), lambda h, j, o=c // HEAD: (nb - 1 - j, h + o)) for c in cols]
        + [blk, blk] + [pl.BlockSpec((1, ns, HEAD, HEAD), lambda h, j: (h, nb - 1 - j, 0, 0)),
                              pl.BlockSpec((2, HEAD), lambda h, j: (0, h)), pl.BlockSpec((1, HEAD), lambda h, j: (0, 0))],
        out_specs=(blk, blk, blk, blk, vec, vec),
        out_shape=(tok, tok, tok, tok, per_head, per_head),
        scratch_shapes=[pltpu.VMEM((HEAD, HEAD), F32)] + [pltpu.VMEM((tb, HEAD), F32)] * 10,
        compiler_params=_params("arbitrary", "arbitrary"),
    )(proj, proj, proj, proj, o_raw, dyb, states, logits, out_norm)


def _lb_logits_grad(logits, dlb, *, name):
    def body(lg_ref, d_ref, o_ref):
        lg = lg_ref[...]
        e = jnp.exp(lg - jnp.max(lg, axis=0, keepdims=True))
        p = e / jnp.sum(e, axis=0, keepdims=True)
        d = d_ref[...]
        rowid = lax.broadcasted_iota(jnp.int32, lg.shape, 0)
        dp = jnp.where(rowid == 0, d, 0.0)
        o_ref[...] = p * (dp - jnp.sum(p * dp, axis=0, keepdims=True))

    return pl.pallas_call(body, name=name, out_shape=jax.ShapeDtypeStruct(logits.shape, F32))(logits, dlb)


def _adamw(w, g, m, v, *, name, deps=()):
    r, c = w.shape
    tc = _pick(c, 2048) if c % LANE == 0 else c
    tr = _row_tile(r, tc * 4)

    def body(w_ref, g_ref, m_ref, v_ref, *rest):
        d_ref, nm_ref, nv_ref = rest[-3:]
        gv = g_ref[...]
        nm = ADAM_B1 * m_ref[...] + (1.0 - ADAM_B1) * gv
        nv = ADAM_B2 * v_ref[...] + (1.0 - ADAM_B2) * (gv * gv)
        m_hat = nm / (1.0 - ADAM_B1 ** ADAM_STEP)
        v_hat = nv / (1.0 - ADAM_B2 ** ADAM_STEP)
        d_ref[...] = -ADAM_LR * (m_hat / (jnp.sqrt(v_hat) + ADAM_EPS) + ADAM_WD * w_ref[...])
        nm_ref[...] = nm
        nv_ref[...] = nv

    spec = pl.BlockSpec((tr, tc), lambda i, j: (i, j))
    shp = jax.ShapeDtypeStruct((r, c), F32)
    return pl.pallas_call(
        body, name=name, grid=(r // tr, c // tc), in_specs=[spec] * 4 + [ANY] * len(deps), out_specs=[spec] * 3,
        out_shape=[shp, shp, shp], compiler_params=_params("parallel", "parallel"),
    )(w, g, m, v, *deps)


def _coords():
    return lax.axis_index("x"), lax.axis_index("y"), lax.axis_index("c")


def _other_chips(x, y):
    return [(1 - x, y), (x, 1 - y), (1 - x, 1 - y)]


ANY = pl.BlockSpec(memory_space=pl.ANY)


class _Layout:
    def __init__(self, d, dff, in_cols, q_lora, kv_lora, nh):
        assert q_lora == kv_lora and nh % 4 == 0 and dff % (4 * LANE) == 0 and in_cols % 4 == 0 and d % 4 == 0
        self.d, self.dff, self.q_lora, self.nh = d, dff, q_lora, nh
        self.head = q_lora + kv_lora + ROPE
        self.pad = d - self.head
        self.nff, self.ncol, self.r_o, self.hps = dff // 4, in_cols // 4, d // 4, nh // 4
        assert self.head <= self.ncol
        self.off_q, self.off_kv, self.rows_narrow = 0, nh * QGROUP, 2 * nh * QGROUP


HBM = pl.BlockSpec(memory_space=pltpu.HBM)
SEMS = pl.BlockSpec(memory_space=pltpu.SEMAPHORE)
SPLIT = dict(has_side_effects=pltpu.SideEffectType.DATAFLOW_SIDE_EFFECTING)


def _in_hbm(a):
    return pltpu.with_memory_space_constraint(a, pltpu.HBM)


def _shard_rows(jobs, k):
    out, lrow = [], [0, 0]
    for job in jobs:
        for row, rows in job.pieces(k):
            out.append((job.a, lrow[job.a], row, rows))
            lrow[job.a] += rows
    return out


def _shard_total(jobs, a):
    return sum(rows for b, _, _, rows in _shard_rows(jobs, 0) if b == a)


def _gather_start(packs, lands, jobs, *, name, deps=()):
    n = len(packs)

    def body(*refs):
        p_refs, l_refs, send, recv, token = refs[:n], refs[n:2 * n], refs[-2 * n - 3], refs[-2 * n - 2], refs[-1]
        x, y, c = _coords()
        for a, lrow, row, rows in _shard_rows(jobs, 2 * x + y):
            pltpu.make_async_remote_copy(
                src_ref=p_refs[a].at[:, pl.ds(lrow, rows)], dst_ref=l_refs[a].at[:, pl.ds(row, rows)],
                send_sem=send.at[4 * a + 3], recv_sem=recv.at[4 * a + 3], device_id=(x, y, 1 - c), device_id_type=MESH).start()
            for j, (px, py) in enumerate(_other_chips(x, y)):
                pltpu.make_async_remote_copy(
                    src_ref=p_refs[a].at[c, pl.ds(lrow, rows)], dst_ref=l_refs[a].at[c, pl.ds(row, rows)],
                    send_sem=send.at[4 * a + j], recv_sem=recv.at[4 * a + j], device_id=(px, py, c), device_id_type=MESH).start()
        token[...] = jnp.zeros_like(token)

    thru = [pltpu.HBM(a.shape, a.dtype) for a in packs + lands]
    out = pl.pallas_call(
        body, name=name, in_specs=[HBM] * (2 * n) + [ANY] * len(deps),
        out_shape=(pltpu.SemaphoreType.DMA((4 * n,)), pltpu.SemaphoreType.DMA((4 * n,)), *thru, jax.ShapeDtypeStruct((8, LANE), F32)),
        out_specs=(SEMS, SEMS, *[HBM] * (2 * n), pl.BlockSpec(memory_space=pltpu.VMEM)),
        input_output_aliases={i: 2 + i for i in range(2 * n)}, compiler_params=pltpu.CompilerParams(**SPLIT),
    )(*[_in_hbm(a) for a in packs + lands], *deps)
    return dict(send=out[0], recv=out[1], bufs=list(out[2:2 + 2 * n]), n=n, jobs=jobs), out[-1]


def _gather_wait(handle, after, *, name):
    n, jobs = handle["n"], handle["jobs"]

    def body(*refs):
        l_refs, send, recv, token = refs[n:2 * n], refs[2 * n], refs[2 * n + 1], refs[-1]
        token[...] = jnp.zeros_like(token)
        x, y, c = _coords()
        for a in range(n):
            total = _shard_total(jobs, a)
            for j, like in enumerate([l_refs[a].at[0, pl.ds(0, total)]] * 3 + [l_refs[a].at[:, pl.ds(0, total)]]):
                cp = pltpu.make_async_remote_copy(src_ref=like, dst_ref=like, send_sem=send.at[4 * a + j],
                                                  recv_sem=recv.at[4 * a + j], device_id=(x, y, c), device_id_type=MESH)
                cp.wait_send()
                cp.wait_recv()

    out = pl.pallas_call(
        body, name=name, in_specs=[HBM] * (2 * n) + [SEMS, SEMS] + [ANY] * len(after),
        out_shape=[pltpu.HBM(a.shape, a.dtype) for a in handle["bufs"]] + [jax.ShapeDtypeStruct((8, LANE), F32)],
        out_specs=[HBM] * (2 * n) + [pl.BlockSpec(memory_space=pltpu.VMEM)],
        input_output_aliases={i: i for i in range(2 * n)}, compiler_params=pltpu.CompilerParams(**SPLIT),
    )(*handle["bufs"], handle["send"], handle["recv"], *after)
    return list(out[n:2 * n]), out[-1]


def _gather_forward(lands, jobs, *, name, deps=()):
    n = len(lands)

    def body(*refs):
        l_refs, send, recv = refs[n + len(deps):2 * n + len(deps)], refs[-2], refs[-1]
        x, y, c = _coords()
        for j, (px, py) in enumerate(_other_chips(x, y)):
            for a, _, row, rows in _shard_rows(jobs, 2 * px + py):
                blk = l_refs[a].at[c, pl.ds(row, rows)]
                pltpu.make_async_remote_copy(src_ref=blk, dst_ref=blk, send_sem=send.at[3 * a + j], recv_sem=recv.at[3 * a + j],
                                             device_id=(x, y, 1 - c), device_id_type=MESH).start()
        for a in range(n):
            like = l_refs[a].at[0, pl.ds(0, _shard_total(jobs, a))]
            for j in range(3):
                cp = pltpu.make_async_remote_copy(src_ref=like, dst_ref=like, send_sem=send.at[3 * a + j],
                                                  recv_sem=recv.at[3 * a + j], device_id=(x, y, c), device_id_type=MESH)
                cp.wait_send()
                cp.wait_recv()

    sem = pltpu.SemaphoreType.DMA((3 * n,))
    return pl.pallas_call(
        body, name=name, in_specs=[ANY] * (n + len(deps)), out_specs=[ANY] * n, input_output_aliases={i: i for i in range(n)},
        out_shape=[jax.ShapeDtypeStruct(a.shape, a.dtype) for a in lands], scratch_shapes=[sem, sem],
    )(*lands, *deps)


def _add_sibling(g, recv, sel, *, name):
    rows, hw = recv.shape
    tr = _row_tile(rows, hw * 4)

    def body(sel_ref, g_ref, r_ref, o_ref):
        o_ref[...] = (g_ref[...] + r_ref[...]).astype(BF16)

    return pl.pallas_call(
        body, name=name, out_shape=jax.ShapeDtypeStruct((rows, hw), BF16),
        grid_spec=pltpu.PrefetchScalarGridSpec(
            num_scalar_prefetch=1, grid=(rows // tr,),
            in_specs=[pl.BlockSpec((None, tr, hw), lambda i, s: (s[0], i, 0)), pl.BlockSpec((tr, hw), lambda i, s: (i, 0))],
            out_specs=pl.BlockSpec((tr, hw), lambda i, s: (i, 0))),
        compiler_params=_params("parallel"),
    )(sel, g, recv)


class _Job:
    def __init__(self, a, blk, n_outer, n_inner, stride, start):
        self.a, self.blk, self.n_outer, self.n_inner, self.stride, self.start = a, blk, n_outer, n_inner, stride, start
        self.rows_out = n_outer * n_inner * blk

    def pieces(self, k):
        return [(self.start(k) + o * self.stride * self.blk, self.n_inner * self.blk) for o in range(self.n_outer)]


def _block_rows(rows, cap, *also):
    best = None
    for b in range(16, min(rows, cap) + 1, 16):
        if rows % b == 0 and all(v % b == 0 for v in also):
            best = b
    assert best is not None, (rows, also)
    return best


def _ffn_jobs(lay):
    b = _block_rows(lay.nff, 704, lay.dff)
    return [_Job(0, b, 3, lay.nff // b, lay.dff // b, lambda k: lay.nff * k)]


def _mix_jobs(lay):
    d, ncol, head, pad = lay.d, lay.ncol, lay.head, lay.pad
    first = lambda k, a, b: jnp.where(k == 0, a, b) if not isinstance(k, int) else (a if k == 0 else b)
    ba = _block_rows(head, 704, *[ncol * k + pad for k in (1, 2, 3)])
    bb = _block_rows(ncol - head, 704, *[ncol * k + d for k in (0, 1, 2, 3)])
    bo = _block_rows(lay.r_o, 704, d)
    bq = _block_rows(HEAD + ROPE, 704, QGROUP)
    bk = _block_rows(lay.hps * QGROUP, 704, lay.off_kv)
    return [_Job(0, ba, 1, head // ba, 0, lambda k: first(k, 0, ncol * k + pad)),
            _Job(0, bb, 1, (ncol - head) // bb, 0, lambda k: ncol * k + d),
            _Job(0, bo, 3, lay.r_o // bo, d // bo, lambda k: 7 * d + lay.r_o * k),
            _Job(1, bq, lay.hps, (HEAD + ROPE) // bq, QGROUP // bq, lambda k: QGROUP * lay.hps * k),
            _Job(1, bk, 1, lay.hps * QGROUP // bk, 0, lambda k: lay.off_kv + lay.hps * QGROUP * k)]


def _swap_start(gs, *, name):
    n = len(gs)
    lands = [lax.empty(g.shape[1:], g.dtype) for g in gs]

    def body(*refs):
        g_refs, land_refs, send, recv, token = refs[:n], refs[n:2 * n], refs[2 * n], refs[2 * n + 1], refs[-1]
        x, y, c = _coords()
        for a in range(n):
            pltpu.make_async_remote_copy(src_ref=g_refs[a].at[1 - c], dst_ref=land_refs[a], send_sem=send.at[a],
                                         recv_sem=recv.at[a], device_id=(x, y, 1 - c), device_id_type=MESH).start()
        token[...] = jnp.zeros_like(token)

    thru = [pltpu.HBM(a.shape, a.dtype) for a in gs + lands]
    out = pl.pallas_call(
        body, name=name, in_specs=[HBM] * (2 * n),
        out_shape=(pltpu.SemaphoreType.DMA((n,)), pltpu.SemaphoreType.DMA((n,)), *thru, jax.ShapeDtypeStruct((8, LANE), F32)),
        out_specs=(SEMS, SEMS, *[HBM] * (2 * n), pl.BlockSpec(memory_space=pltpu.VMEM)),
        input_output_aliases={i: 2 + i for i in range(2 * n)}, compiler_params=pltpu.CompilerParams(**SPLIT),
    )(*[_in_hbm(a) for a in gs + lands])
    return dict(send=out[0], recv=out[1], bufs=list(out[2:2 + 2 * n]), n=n), out[-1]


def _swap_wait(handle, after, *, name):
    n = handle["n"]

    def body(*refs):
        g_refs, land_refs, send, recv = refs[:n], refs[n:2 * n], refs[2 * n], refs[2 * n + 1]
        x, y, c = _coords()
        for a in range(n):
            cp = pltpu.make_async_remote_copy(src_ref=g_refs[a].at[1 - c], dst_ref=land_refs[a], send_sem=send.at[a],
                                              recv_sem=recv.at[a], device_id=(x, y, 1 - c), device_id_type=MESH)
            cp.wait_send()
            cp.wait_recv()

    out = pl.pallas_call(
        body, name=name, in_specs=[HBM] * (2 * n) + [SEMS, SEMS] + [ANY] * len(after),
        out_shape=[pltpu.HBM(a.shape, a.dtype) for a in handle["bufs"]], out_specs=[HBM] * (2 * n),
        input_output_aliases={i: i for i in range(2 * n)}, compiler_params=pltpu.CompilerParams(**SPLIT),
    )(*handle["bufs"], handle["send"], handle["recv"], *after)
    return list(out[:n]), list(out[n:])


def _exchange_start(ss, jobs, *, name):
    n = len(ss)
    lands = [lax.empty((3,) + s.shape, s.dtype) for s in ss]

    def body(*refs):
        s_refs, land_refs, send, recv, token = refs[:n], refs[n:2 * n], refs[2 * n], refs[2 * n + 1], refs[-1]
        x, y, c = _coords()
        for j, (px, py) in enumerate(_other_chips(x, y)):
            for job in jobs:
                for row, rows in job.pieces(2 * px + py):
                    pltpu.make_async_remote_copy(
                        src_ref=s_refs[job.a].at[pl.ds(row, rows)], dst_ref=land_refs[job.a].at[j, pl.ds(row, rows)],
                        send_sem=send.at[n * j + job.a], recv_sem=recv.at[n * j + job.a], device_id=(px, py, c),
                        device_id_type=MESH).start()
        token[...] = jnp.zeros_like(token)

    thru = [pltpu.HBM(a.shape, a.dtype) for a in ss + lands]
    out = pl.pallas_call(
        body, name=name, in_specs=[HBM] * (2 * n),
        out_shape=(pltpu.SemaphoreType.DMA((3 * n,)), pltpu.SemaphoreType.DMA((3 * n,)), *thru, jax.ShapeDtypeStruct((8, LANE), F32)),
        out_specs=(SEMS, SEMS, *[HBM] * (2 * n), pl.BlockSpec(memory_space=pltpu.VMEM)),
        input_output_aliases={i: 2 + i for i in range(2 * n)}, compiler_params=pltpu.CompilerParams(**SPLIT),
    )(*[_in_hbm(a) for a in ss + lands])
    return dict(send=out[0], recv=out[1], bufs=list(out[2:2 + 2 * n]), n=n, jobs=jobs), out[-1]


def _exchange_wait(handle, after, *, name):
    n, jobs = handle["n"], handle["jobs"]
    total = [sum(rows for job in jobs if job.a == a for _, rows in job.pieces(0)) for a in range(n)]

    def body(*refs):
        s_refs, land_refs, send, recv = refs[:n], refs[n:2 * n], refs[2 * n], refs[2 * n + 1]
        x, y, c = _coords()
        for a in range(n):
            for j in range(3):
                all_rows = land_refs[a].at[0, pl.ds(0, total[a])]
                cp = pltpu.make_async_remote_copy(src_ref=all_rows, dst_ref=all_rows, send_sem=send.at[n * j + a],
                                                  recv_sem=recv.at[n * j + a], device_id=(x, y, c), device_id_type=MESH)
                cp.wait_send()
                cp.wait_recv()

    out = pl.pallas_call(
        body, name=name, in_specs=[HBM] * (2 * n) + [SEMS, SEMS] + [ANY] * len(after),
        out_shape=[pltpu.HBM(a.shape, a.dtype) for a in handle["bufs"]], out_specs=[HBM] * (2 * n),
        input_output_aliases={i: i for i in range(2 * n)}, compiler_params=pltpu.CompilerParams(**SPLIT),
    )(*handle["bufs"], handle["send"], handle["recv"], *after)
    return list(out[:n]), list(out[n:])


def _add_shard(s, land, job, sel, k, *, name):
    hw = s.shape[1]
    blk, no, ni, stride = job.blk, job.n_outer, job.n_inner, job.stride
    scal = jnp.stack([sel, job.start(k) // blk]).astype(jnp.int32)

    def body(sc_ref, own_ref, r_ref, o_ref):
        o_ref[...] = ((own_ref[...].astype(F32) + r_ref[0].astype(F32)) + r_ref[1].astype(F32)) + r_ref[2].astype(F32)

    return pl.pallas_call(
        body, name=name, out_shape=jax.ShapeDtypeStruct((2, job.rows_out, hw), F32),
        grid_spec=pltpu.PrefetchScalarGridSpec(
            num_scalar_prefetch=1, grid=(no, ni),
            in_specs=[pl.BlockSpec((blk, hw), lambda o, b, sc: (sc[1] + o * stride + b, 0)),
                      pl.BlockSpec((3, blk, hw), lambda o, b, sc: (0, sc[1] + o * stride + b, 0))],
            out_specs=pl.BlockSpec((None, blk, hw), lambda o, b, sc: (sc[0], o * ni + b, 0))),
        compiler_params=_params("parallel", "parallel"),
    )(scal, s, land)


def _join_list(fs, *, name):
    n = len(fs)

    def body(*refs):
        f_refs, send_sems, recv_sems = refs[n:2 * n], refs[2 * n], refs[2 * n + 1]
        x, y, c = _coords()
        copies = [pltpu.make_async_remote_copy(
            src_ref=f.at[c], dst_ref=f.at[c], send_sem=send_sems.at[a], recv_sem=recv_sems.at[a],
            device_id=(x, y, 1 - c), device_id_type=MESH) for a, f in enumerate(f_refs)]
        for cp in copies:
            cp.start()
        for cp in copies:
            cp.wait()

    sem = pltpu.SemaphoreType.DMA((n,))
    return pl.pallas_call(
        body, name=name, in_specs=[ANY] * n, out_specs=[ANY] * n, input_output_aliases={i: i for i in range(n)},
        out_shape=[jax.ShapeDtypeStruct(f.shape, f.dtype) for f in fs], scratch_shapes=[sem, sem],
    )(*fs)


def _all_reduce_small(vec, *, name):
    n = vec.shape[1]

    def body(v_ref, o_ref, buf, send_sems, recv_sems):
        x, y, c = _coords()
        me = 4 * x + 2 * y + c
        buf[me] = v_ref[...]
        copies = []
        for m in range(1, 8):
            peer = (x ^ ((m >> 2) & 1), y ^ ((m >> 1) & 1), c ^ (m & 1))
            copies.append(pltpu.make_async_remote_copy(
                src_ref=v_ref, dst_ref=buf.at[me], send_sem=send_sems.at[m - 1], recv_sem=recv_sems.at[m - 1],
                device_id=peer, device_id_type=MESH))
        for cp in copies:
            cp.start()
        for cp in copies:
            cp.wait()
        acc = buf[0]
        for d in range(1, 8):
            acc = acc + buf[d]
        o_ref[...] = acc

    return pl.pallas_call(
        body, name=name, out_shape=jax.ShapeDtypeStruct((1, n), F32),
        in_specs=[pl.BlockSpec(memory_space=pltpu.VMEM)], out_specs=pl.BlockSpec(memory_space=pltpu.VMEM),
        scratch_shapes=[pltpu.VMEM((8, 1, n), F32), pltpu.SemaphoreType.DMA((7,)), pltpu.SemaphoreType.DMA((7,))],
    )(vec)


def _ffn_fwd(x, n_pre, n_post, wbuf, lay, tag, deps=()):
    wg, wu, wd = ((wbuf, i * lay.dff, lay.dff) for i in range(3))
    h = _norm_fwd(x, n_pre, name=f"{tag}_norm_pre", out_dtype=BF16)
    g = _mm([(h, wg)], name=f"{tag}_gate", mode="nt", deps=deps)
    u, a = _mm([(h, wu)], name=f"{tag}_up", mode="nt", extras=[g], out_dtypes=[F32, BF16], tm_cap=MM_TILE // 2,
               epilogue=lambda up, gate: (up, _silu(gate) * up))
    yv = _mm([(a, wd)], name=f"{tag}_down", mode="nn")
    out = _norm_fwd(yv, n_post, name=f"{tag}_norm_post", resid=x, scale=MACARON_SCALE)
    return out, (x, h, g, u, a, yv)


def _ffn_bwd(dout, saved, n_pre, n_post, wbuf, lay, tag, deps=(), after_act=None, after_dw=None):
    x, h, g, u, a, yv = saved
    dff = lay.dff
    gbuf = lax.empty((2, 3 * dff, lay.d // 2), F32)
    dy, dn_post = _norm_bwd(yv, n_post, dout, name=f"{tag}_norm_post_bwd", scale=MACARON_SCALE)
    dg, du = _mm([(dy, (wbuf, 2 * dff, dff))], name=f"{tag}_down_dx", mode="nt", deps=deps, extras=[g, u],
                 out_dtypes=[BF16, BF16], tm_cap=MM_TILE // 2,
                 epilogue=lambda da, gate, up: (da * up * _dsilu(gate), da * _silu(gate)))
    deps = after_act(du) if after_act is not None else ()
    gbuf = _mm([(a, dy)], name=f"{tag}_down_dw", mode="tn", into=(gbuf, 2 * dff), deps=deps)
    gbuf = _mm([(dg, h)], name=f"{tag}_gate_dw", mode="tn", into=(gbuf, 0))
    gbuf = _mm([(du, h)], name=f"{tag}_up_dw", mode="tn", into=(gbuf, dff))
    deps = after_dw(gbuf)
    dh = _mm([(dg, (wbuf, 0, dff)), (du, (wbuf, dff, dff))], name=f"{tag}_up_dx", mode="nn", deps=deps)
    dx, dn_pre = _norm_bwd(x, n_pre, dh, name=f"{tag}_norm_pre_bwd", dres=dout)
    return dx, dn_pre, dn_post


def _rope_tables(positions):
    half = ROPE // 2
    inv_freq = ROPE_THETA ** (-jnp.arange(half, dtype=F32) / half)
    ang = positions.astype(F32)[:, None] * inv_freq
    cos, sin = jnp.cos(ang), jnp.sin(ang)
    z = jnp.zeros_like(cos)
    z2 = jnp.zeros((positions.shape[0], LANE - ROPE), F32)
    return (jnp.concatenate([cos, cos, z2], axis=1), jnp.concatenate([-sin, z, z2], axis=1),
            jnp.concatenate([z, sin, z2], axis=1))


def kernel(x, positions, ffn1_norm_pre, ffn1_w_gate, ffn1_w_up, ffn1_w_down, ffn1_norm_post, mix_norm_pre, w_in, mla_q_norm, mla_w_q_up, mla_kv_norm, mla_w_kv_up, mla_w_o, hgrn_lb_logits, hgrn_out_norm, hgrn_w_o, w_out, mix_norm_post, ffn2_norm_pre, ffn2_w_gate, ffn2_w_up, ffn2_w_down, ffn2_norm_post, loss_target, m_ffn1_norm_pre, m_ffn1_w_gate, m_ffn1_w_up, m_ffn1_w_down, m_ffn1_norm_post, m_mix_norm_pre, m_w_in, m_mla_q_norm, m_mla_w_q_up, m_mla_kv_norm, m_mla_w_kv_up, m_mla_w_o, m_hgrn_lb_logits, m_hgrn_out_norm, m_hgrn_w_o, m_w_out, m_mix_norm_post, m_ffn2_norm_pre, m_ffn2_w_gate, m_ffn2_w_up, m_ffn2_w_down, m_ffn2_norm_post, v_ffn1_norm_pre, v_ffn1_w_gate, v_ffn1_w_up, v_ffn1_w_down, v_ffn1_norm_post, v_mix_norm_pre, v_w_in, v_mla_q_norm, v_mla_w_q_up, v_mla_kv_norm, v_mla_w_kv_up, v_mla_w_o, v_hgrn_lb_logits, v_hgrn_out_norm, v_hgrn_w_o, v_w_out, v_mix_norm_post, v_ffn2_norm_pre, v_ffn2_w_gate, v_ffn2_w_up, v_ffn2_w_down, v_ffn2_norm_post):
    given = dict(locals())
    wts = {n: given[n] for n in ALL_WEIGHTS}
    mom = {n: given["m_" + n] for n in ALL_WEIGHTS}
    var = {n: given["v_" + n] for n in ALL_WEIGHTS}
    xin = x[0]
    target = loss_target[0]
    t, d = xin.shape
    cx, cy, cc = _coords()

    q_lora, kv_lora = mla_q_norm.shape[1], mla_kv_norm.shape[1]
    nh_mla = 4 * mla_w_kv_up.shape[2] // QGROUP
    lay = _Layout(d, 4 * ffn1_w_gate.shape[2], 4 * w_in.shape[2], q_lora, kv_lora, nh_mla)
    jobs_ffn, jobs_mix = _ffn_jobs(lay), _mix_jobs(lay)
    def pack(src, col_sharded, row_sharded=()):
        a = jnp.concatenate([src[n][0].T.astype(BF16) for n in col_sharded] + [src[n][0].astype(BF16) for n in row_sharded])
        return a.reshape(a.shape[0], 2, a.shape[1] // 2).transpose(1, 0, 2)

    ffn_land = lambda: [lax.empty((2, 3 * lay.dff, d // 2), BF16)]
    got1, tok = _gather_start([pack(wts, ["ffn1_w_gate", "ffn1_w_up"], ["ffn1_w_down"])], ffn_land(), jobs_ffn, name="gather_ffn1")
    later, _ = lax.optimization_barrier(({n: wts[n] for n in BIG_WEIGHTS if not n.startswith("ffn1")}, tok))
    packs_mix = [pack(later, ["w_in"], ["mla_w_o", "hgrn_w_o", "w_out"]), pack(later, ["mla_w_q_up", "mla_w_kv_up"])]
    packs_ffn2 = [pack(later, ["ffn2_w_gate", "ffn2_w_up"], ["ffn2_w_down"])]
    lands_mix = [jnp.zeros((2, 10 * d, d // 2), BF16), jnp.zeros((2, lay.rows_narrow, q_lora // 2), BF16)]
    arrived, tok = _gather_wait(got1, packs_mix + packs_ffn2 + lands_mix, name="gather_ffn1_wait")
    got_m, tok = _gather_start(packs_mix, lands_mix, jobs_mix, name="gather_mix", deps=[tok])
    (w_ffn1,) = _gather_forward(arrived, jobs_ffn, name="gather_ffn1_forward", deps=[tok])
    col_kr = q_lora + kv_lora
    hgrn_cols = [d, 2 * d, 3 * d, 4 * d]
    col_ga, col_gb = 5 * d, 6 * d
    tabs = _rope_tables(positions[0])
    scale = (HEAD + ROPE) ** -0.5

    x1, saved1 = _ffn_fwd(xin, ffn1_norm_pre, ffn1_norm_post, w_ffn1, lay, "ffn1")

    arrived, tok = _gather_wait(got_m, [x1], name="gather_mix_wait")
    got2, tok = _gather_start(packs_ffn2, ffn_land(), jobs_ffn, name="gather_ffn2", deps=[tok])
    wide, narrow = _gather_forward(arrived, jobs_mix, name="gather_mix_forward", deps=[tok])
    w_in_v = (wide, 0, 7 * d)
    w_o_v = {n: (wide, (7 + i) * d, d) for i, n in enumerate(("mla_w_o", "hgrn_w_o", "w_out"))}
    w_q_v = (narrow, lay.off_q, nh_mla * QGROUP)
    w_kv_v = (narrow, lay.off_kv, nh_mla * QGROUP)

    h2 = _norm_fwd(x1, mix_norm_pre, name="mix_norm_pre", out_dtype=BF16)
    proj = _mm([(h2, w_in_v)], name="mix_in", mode="nt", deps=[tok])
    cqn = _norm_fwd(proj, mla_q_norm, name="mla_q_norm", out_dtype=BF16, col=0)
    ckvn = _norm_fwd(proj, mla_kv_norm, name="mla_kv_norm", out_dtype=BF16, col=q_lora)
    qp = _mm([(cqn, w_q_v)], name="mla_q_up", mode="nt")
    kvb = _mm([(ckvn, w_kv_v)], name="mla_kv_up", mode="nt", out_dtype=BF16)
    qcat = _rope(qp, tabs, name="rope_q", group=QGROUP, backward=False, out_dtype=BF16)
    krot = _rope(proj, tabs, name="rope_k", group=LANE, backward=False, out_dtype=BF16, col=col_kr, ngroup=1)
    o_mla = _attn_fwd(qcat, kvb, krot, name="mla_attention", scale=scale)
    y_a = _mm([(o_mla, w_o_v["mla_w_o"])], name="mla_out", mode="nn")

    o_raw, yb, states = _hgrn_fwd(proj, hgrn_cols, d, hgrn_lb_logits, hgrn_out_norm, name="hgrn_scan")
    y_b = _mm([(yb, w_o_v["hgrn_w_o"])], name="hgrn_out", mode="nn")

    merged = _merge_fwd(proj, col_ga, col_gb, y_a, y_b, name="mix_merge")
    y_mix = _mm([(merged, w_o_v["w_out"])], name="mix_out", mode="nn")
    x2 = _norm_fwd(y_mix, mix_norm_post, name="mix_norm_post", resid=x1, scale=1.0)

    (w_ffn2,) = _gather_forward(_gather_wait(got2, [x2], name="gather_ffn2_wait")[0], jobs_ffn, name="gather_ffn2_forward")
    x3, saved2 = _ffn_fwd(x2, ffn2_norm_pre, ffn2_norm_post, w_ffn2, lay, "ffn2")
    dx3, loss_local = _loss_head(x3, target, name="loss_head")

    grads, deltas, new_m, new_v = {}, {}, {}, {}
    sel = cc.astype(jnp.int32)
    sel1 = jnp.reshape(sel, (1,))
    me_chip = (2 * cx + cy).astype(jnp.int32)

    def reduce_mid(handle, after, jobs, tag):
        bufs, recvd = _swap_wait(handle, after, name=f"grad_swap_{tag}_wait")
        sums = [_add_sibling(b, r, sel1, name=f"grad_add_sibling_{tag}_{i}") for i, (b, r) in enumerate(zip(bufs, recvd))]
        return _exchange_start(sums, jobs, name=f"grad_exchange_{tag}")

    def reduce_end(handle, after, tag):
        sums, lands = _exchange_wait(handle, after, name=f"grad_exchange_{tag}_wait")
        parts = [_add_shard(sums[job.a], lands[job.a], job, sel, me_chip, name=f"grad_add_chips_{tag}_{i}")
                 for i, job in enumerate(handle["jobs"])]
        return _join_list(parts, name=f"grad_join_{tag}")

    def natural(part, lo, rows, transposed):
        g_n = part[:, lo:lo + rows]
        hw_n = g_n.shape[2]
        return g_n.transpose(0, 2, 1).reshape(2 * hw_n, rows) if transposed else g_n.transpose(1, 0, 2).reshape(rows, 2 * hw_n)

    def adam(names, deps=()):
        for i, n in enumerate(names):
            shp = wts[n].shape
            two_d = (lambda a: a[0]) if n in BIG_WEIGHTS else (lambda a: a)
            dl, nm, nv = _adamw(two_d(wts[n]), grads[n], two_d(mom[n]), two_d(var[n]), name=f"adamw_{n}",
                                deps=deps if i == 0 else ())
            grads[n] = grads[n].reshape(shp)
            deltas[n], new_m[n], new_v[n] = dl.reshape(shp), nm.reshape(shp), nv.reshape(shp)
        return [deltas[n] for n in names]

    def ffn_grads(joined, tag, deps=()):
        nff = lay.nff
        grads[f"{tag}_w_gate"] = natural(joined[0], 0, nff, True)
        grads[f"{tag}_w_up"] = natural(joined[0], nff, nff, True)
        grads[f"{tag}_w_down"] = natural(joined[0], 2 * nff, nff, False)
        return adam([f"{tag}_w_gate", f"{tag}_w_up", f"{tag}_w_down"], deps)

    swaps = {}

    def start_swap(tag):
        def hook(gbuf):
            swaps[tag], started = _swap_start([gbuf], name=f"grad_swap_{tag}")
            return [started]
        return hook

    dx2, grads["ffn2_norm_pre"], grads["ffn2_norm_post"] = _ffn_bwd(
        dx3, saved2, ffn2_norm_pre, ffn2_norm_post, w_ffn2, lay, "ffn2", after_dw=start_swap("ffn2"))

    gwide = lax.empty((2, 10 * d, d // 2), F32)
    gnarrow = lax.empty((2, lay.rows_narrow, q_lora // 2), F32)
    dy_mix, grads["mix_norm_post"] = _norm_bwd(y_mix, mix_norm_post, dx2, name="mix_norm_post_bwd")
    dmerged = _mm([(dy_mix, w_o_v["w_out"])], name="mix_out_dx", mode="nt")
    gwide = _mm([(merged, dy_mix)], name="mix_out_dw", mode="tn", into=(gwide, 9 * d))
    dga, dgb, dy_a, dy_b = _merge_bwd(dmerged, proj, col_ga, col_gb, y_a, y_b, name="mix_merge_bwd")

    do_mla = _mm([(dy_a, w_o_v["mla_w_o"])], name="mla_out_dx", mode="nt")
    gwide = _mm([(o_mla, dy_a)], name="mla_out_dw", mode="tn", into=(gwide, 7 * d))
    dqcat, dkv, dkr = _attn_bwd(qcat, kvb, krot, do_mla, name="mla_attention_bwd", scale=scale)
    exch2, tok = reduce_mid(swaps["ffn2"], [dkr], _ffn_jobs(lay), "ffn2")

    dqp = _rope(dqcat, tabs, name="rope_q_bwd", group=QGROUP, backward=True, out_dtype=BF16)
    dk_r = _rope(dkr, tabs, name="rope_k_bwd", group=LANE, backward=True, out_dtype=BF16)
    dcqn = _mm([(dqp, w_q_v)], name="mla_q_up_dx", mode="nn", deps=[tok])
    gnarrow = _mm([(dqp, cqn)], name="mla_q_up_dw", mode="tn", into=(gnarrow, lay.off_q))
    dkvb = dkv.astype(BF16)
    dckvn = _mm([(dkvb, w_kv_v)], name="mla_kv_up_dx", mode="nn")
    gnarrow = _mm([(dkvb, ckvn)], name="mla_kv_up_dw", mode="tn", into=(gnarrow, lay.off_kv))
    dc_q, grads["mla_q_norm"] = _norm_bwd(proj, mla_q_norm, dcqn, name="mla_q_norm_bwd", col=0, dx_dtype=BF16)
    dc_kv, grads["mla_kv_norm"] = _norm_bwd(proj, mla_kv_norm, dckvn, name="mla_kv_norm_bwd", col=q_lora, dx_dtype=BF16)

    dyb = _mm([(dy_b, w_o_v["hgrn_w_o"])], name="hgrn_out_dx", mode="nt")
    gwide = _mm([(yb, dy_b)], name="hgrn_out_dw", mode="tn", into=(gwide, 8 * d))
    dhq, dhf, dhi, dhg, dlb_h, dnorm_h = _hgrn_bwd(proj, hgrn_cols, d, o_raw, dyb, states, hgrn_lb_logits, hgrn_out_norm,
                                                   name="hgrn_scan_bwd")

    dhead = jnp.concatenate([dc_q, dc_kv, dk_r, jnp.zeros((t, d - col_kr - LANE), BF16)], axis=1)
    dparts = [dhead, dhq, dhf, dhi, dhg, dga, dgb]
    dh2 = _mm([(p, (wide, i * d, d)) for i, p in enumerate(dparts)], name="mix_in_dx", mode="nn")
    for i, p in enumerate(dparts):
        gwide = _mm([(p, h2)], name=f"mix_in_dw_{i}", mode="tn", into=(gwide, i * d))
    dx1, grads["mix_norm_pre"] = _norm_bwd(x1, mix_norm_pre, dh2, name="mix_norm_pre_bwd", dres=dx2)
    swap_m, tok = _swap_start([gwide, gnarrow], name="grad_swap_mix")
    joined2 = reduce_end(exch2, [dx1], "ffn2")

    exchanges = {}

    def mix_exchange(after):
        exchanges["mix"], started = reduce_mid(swap_m, [after], _mix_jobs(lay), "mix")
        return [started]

    dx0, grads["ffn1_norm_pre"], grads["ffn1_norm_post"] = _ffn_bwd(
        dx1, saved1, ffn1_norm_pre, ffn1_norm_post, w_ffn1, lay, "ffn1", deps=[tok], after_act=mix_exchange,
        after_dw=start_swap("ffn1"))
    exch1, tok = reduce_mid(swaps["ffn1"], [dx0], _ffn_jobs(lay), "ffn1")

    joined_m = reduce_end(exchanges["mix"], [dx0, tok], "mix")
    done = ffn_grads(joined2, "ffn2")
    grads["w_in"] = natural(jnp.concatenate([joined_m[0], joined_m[1]], axis=1), 0, lay.ncol, True)
    for i, n in enumerate(("mla_w_o", "hgrn_w_o", "w_out")):
        grads[n] = natural(joined_m[2], i * lay.r_o, lay.r_o, False)
    grads["mla_w_q_up"] = natural(joined_m[3], 0, lay.hps * (HEAD + ROPE), True)
    grads["mla_w_kv_up"] = natural(joined_m[4], 0, lay.hps * QGROUP, True)
    done += adam(["w_in", "mla_w_q_up", "mla_w_kv_up", "mla_w_o", "hgrn_w_o", "w_out"])

    dlb = dlb_h.reshape(1, -1)
    dnorm = jnp.sum(dnorm_h, axis=0)
    small = {**{n: grads[n] for n in SMALL_WEIGHTS if n not in ("hgrn_lb_logits", "hgrn_out_norm")},
             "hgrn_lb_logits": dlb, "hgrn_out_norm": dnorm}
    vec = jnp.concatenate([small[n] for n in SMALL_WEIGHTS], axis=1)
    vec = _all_reduce_small(vec, name="grad_all_reduce_small")
    off = 0
    for n in SMALL_WEIGHTS:
        w_n = small[n].shape[1]
        grads[n] = vec[:, off:off + w_n]
        off += w_n
    grads["hgrn_lb_logits"] = _lb_logits_grad(hgrn_lb_logits, grads["hgrn_lb_logits"], name="lb_logits_grad")

    done += adam(list(SMALL_WEIGHTS))
    ffn_grads(reduce_end(exch1, done, "ffn1"), "ffn1")

    loss = lax.psum(loss_local, ("x", "y", "c"))
    dx_out = dx0.reshape(x.shape)
    return (loss, dx_out, *[grads[n] for n in ALL_WEIGHTS], *[deltas[n] for n in ALL_WEIGHTS],
            *[new_m[n] for n in ALL_WEIGHTS], *[new_v[n] for n in ALL_WEIGHTS])
```

```python
import functools

import jax
import jax.numpy as jnp
from jax import lax
from jax.experimental import pallas as pl
from jax.experimental.pallas import tpu as pltpu

F32 = jnp.float32
BF16 = jnp.bfloat16
MESH = pl.DeviceIdType.MESH

NORM_EPS = 1e-6
MACARON_SCALE = 0.5
ROPE_THETA = 10000.0
HEAD = 128
ROPE = 64
QGROUP = 2 * HEAD
SUB = 16
ADAM_LR, ADAM_B1, ADAM_B2, ADAM_EPS, ADAM_WD, ADAM_STEP = 0.001, 0.9, 0.999, 1e-08, 0.01, 10

LANE = 128
VMEM_LIMIT = 48 * 1024 * 1024
MM_TILE = 1024
MM_TILE_WIDE = 1536

BIG_WEIGHTS = ("ffn1_w_gate", "ffn1_w_up", "ffn1_w_down", "w_in", "mla_w_q_up", "mla_w_kv_up",
               "mla_w_o", "hgrn_w_o", "w_out", "ffn2_w_gate", "ffn2_w_up", "ffn2_w_down")
COL_SHARDED = ("ffn1_w_gate", "ffn1_w_up", "w_in", "mla_w_q_up", "mla_w_kv_up", "ffn2_w_gate", "ffn2_w_up")
SMALL_WEIGHTS = ("ffn1_norm_pre", "ffn1_norm_post", "mix_norm_pre", "mla_q_norm", "mla_kv_norm",
                 "hgrn_lb_logits", "hgrn_out_norm", "mix_norm_post", "ffn2_norm_pre", "ffn2_norm_post")
ALL_WEIGHTS = ("ffn1_norm_pre", "ffn1_w_gate", "ffn1_w_up", "ffn1_w_down", "ffn1_norm_post", "mix_norm_pre",
               "w_in", "mla_q_norm", "mla_w_q_up", "mla_kv_norm", "mla_w_kv_up", "mla_w_o", "hgrn_lb_logits",
               "hgrn_out_norm", "hgrn_w_o", "w_out", "mix_norm_post", "ffn2_norm_pre", "ffn2_w_gate",
               "ffn2_w_up", "ffn2_w_down", "ffn2_norm_post")


def _params(*sem):
    return pltpu.CompilerParams(dimension_semantics=sem or None, vmem_limit_bytes=VMEM_LIMIT)


def _pick(n, cap, offset=0):
    if n <= cap and offset % n == 0:
        return n
    best = None
    for t in range(LANE, min(n, cap) + 1, LANE):
        if n % t == 0 and offset % t == 0:
            best = t
    assert best is not None, (n, cap, offset)
    return best


def _row_tile(n, row_bytes, budget=1 << 20):
    best = None
    for t in range(8, n + 1, 8):
        if n % t == 0 and t * row_bytes <= budget:
            best = t
    return n if best is None else best


def _sigmoid(x):
    return 1.0 / (1.0 + jnp.exp(-x))


def _silu(x):
    return x * _sigmoid(x)


def _dsilu(x):
    s = _sigmoid(x)
    return s * (1.0 + x * (1.0 - s))


def _mm(pairs, *, name, mode="nn", out_dtype=F32, into=None, deps=(), extras=(), epilogue=None, out_dtypes=None, tm_cap=None):
    halves = isinstance(pairs[0][1], tuple)
    assert halves or mode == "tn"
    pairs = [(a, b if halves else (b, 0, b.shape[0])) for a, b in pairs]
    a0, (b0, b_off, b_rows) = pairs[0]
    hw = b0.shape[2] if halves else (into[0].shape[2] if into is not None else None)
    if mode == "nn":
        (m, kdim), n = a0.shape, 2 * hw
    elif mode == "nt":
        (m, kdim), n = a0.shape, b_rows
        assert kdim == 2 * hw
    else:
        (kdim, m), n = a0.shape, b0.shape[1]
    out_off = 0 if into is None else into[1]
    tm = _pick(m, tm_cap or (MM_TILE_WIDE if mode == "tn" else MM_TILE), out_off)
    tn = hw if (mode == "nn" or into is not None) else _pick(n, MM_TILE_WIDE, b_off if mode == "nt" else 0)
    tk = hw if mode == "nt" else _pick(kdim, MM_TILE if len(pairs) <= 2 else MM_TILE // 2, b_off if mode == "nn" else 0)
    assert n % tn == 0 and kdim % tk == 0
    nk = kdim // tk
    npair = len(pairs)
    dims = {"nn": (((1,), (0,)), ((), ())), "nt": (((1,), (1,)), ((), ())), "tn": (((0,), (0,)), ((), ()))}[mode]

    nout = 1 if epilogue is None else len(out_dtypes)

    def body(*refs):
        ins, x_refs = refs[:2 * npair], refs[2 * npair:2 * npair + len(extras)]
        o_refs, acc_ref = refs[-1 - nout:-1], refs[-1]
        k = pl.program_id(2)

        @pl.when(k == 0)
        def _():
            acc_ref[...] = jnp.zeros_like(acc_ref)

        for p in range(npair):
            a = ins[2 * p][...].astype(BF16)
            b = ins[2 * p + 1][...].astype(BF16)
            acc_ref[...] += lax.dot_general(a, b, dims, preferred_element_type=F32)

        @pl.when(k == nk - 1)
        def _():
            outs = (acc_ref[...],) if epilogue is None else epilogue(acc_ref[...], *[x[...] for x in x_refs])
            for o_ref, o in zip(o_refs, outs):
                o_ref[...] = o.astype(o_ref.dtype)

    a_spec = pl.BlockSpec((tk, tm), lambda i, j, k: (k, i)) if mode == "tn" else pl.BlockSpec((tm, tk), lambda i, j, k: (i, k))
    in_specs, flat = [], []
    for a, (b, off, _) in pairs:
        if mode == "nt":
            b_spec = pl.BlockSpec((None, tn, tk), lambda i, j, k, o=off // tn: (k, j + o, 0))
        elif mode == "nn":
            b_spec = pl.BlockSpec((None, tk, tn), lambda i, j, k, o=off // tk: (j, k + o, 0))
        else:
            b_spec = pl.BlockSpec((tk, tn), lambda i, j, k: (k, j))
        in_specs += [a_spec, b_spec]
        flat += [a, b]
    for extra in extras:
        in_specs.append(pl.BlockSpec((tm, tn), lambda i, j, k: (i, j)))
        flat.append(extra)
    for dep in deps:
        in_specs.append(pl.BlockSpec(memory_space=pl.ANY))
        flat.append(dep)
    if epilogue is not None:
        assert into is None
        out_shape, aliases = [jax.ShapeDtypeStruct((m, n), dt) for dt in out_dtypes], {}
        out_spec = [pl.BlockSpec((tm, tn), lambda i, j, k: (i, j))] * nout
    elif into is None:
        out_shape, aliases = jax.ShapeDtypeStruct((m, n), out_dtype), {}
        out_spec = pl.BlockSpec((tm, tn), lambda i, j, k: (i, j))
    else:
        out_shape, aliases = jax.ShapeDtypeStruct(into[0].shape, into[0].dtype), {len(flat): 0}
        out_spec = pl.BlockSpec((None, tm, tn), lambda i, j, k, o=out_off // tm: (j, i + o, 0))
        in_specs.append(pl.BlockSpec(memory_space=pl.ANY))
        flat.append(into[0])
    return pl.pallas_call(
        body, name=name, grid=(m // tm, n // tn, nk),
        in_specs=in_specs,
        out_specs=out_spec,
        out_shape=out_shape, input_output_aliases=aliases,
        scratch_shapes=[pltpu.VMEM((tm, tn), F32)],
        compiler_params=_params("parallel", "parallel", "arbitrary"),
    )(*flat)


def _norm_fwd(y, w, *, name, resid=None, scale=1.0, out_dtype=F32, col=0):
    t, d = y.shape[0], w.shape[1]
    tr = _pick(t, 256)
    assert col % d == 0

    def body(*refs):
        if resid is None:
            y_ref, w_ref, o_ref = refs
        else:
            y_ref, w_ref, r_ref, o_ref = refs
        yv = y_ref[...]
        out = yv * lax.rsqrt(jnp.mean(yv * yv, axis=-1, keepdims=True) + NORM_EPS) * w_ref[...]
        if resid is not None:
            out = r_ref[...] + scale * out
        o_ref[...] = out.astype(out_dtype)

    row = pl.BlockSpec((tr, d), lambda i: (i, 0))
    wspec = pl.BlockSpec((1, d), lambda i: (0, 0))
    ins, specs = [y, w], [pl.BlockSpec((tr, d), lambda i: (i, col // d)), wspec]
    if resid is not None:
        ins.append(resid)
        specs.append(row)
    return pl.pallas_call(
        body, name=name, grid=(t // tr,), in_specs=specs, out_specs=row,
        out_shape=jax.ShapeDtypeStruct((t, d), out_dtype), compiler_params=_params("parallel"),
    )(*ins)


def _norm_bwd(x, w, dy, *, name, scale=1.0, dres=None, col=0, dx_dtype=F32):
    t, d = x.shape[0], w.shape[1]
    tr = _pick(t, 256)
    assert col % d == 0

    def body(*refs):
        if dres is None:
            x_ref, w_ref, dy_ref, dx_ref, dw_ref = refs
        else:
            x_ref, w_ref, dy_ref, dr_ref, dx_ref, dw_ref = refs

        @pl.when(pl.program_id(0) == 0)
        def _():
            dw_ref[...] = jnp.zeros_like(dw_ref)

        xv = x_ref[...]
        r = lax.rsqrt(jnp.mean(xv * xv, axis=-1, keepdims=True) + NORM_EPS)
        xhat = xv * r
        dyv = dy_ref[...].astype(F32) * scale
        dw_ref[...] += jnp.sum(dyv * xhat, axis=0, keepdims=True)
        t_ = dyv * w_ref[...]
        dx = r * (t_ - xhat * jnp.mean(t_ * xhat, axis=-1, keepdims=True))
        if dres is not None:
            dx = dx + dr_ref[...]
        dx_ref[...] = dx.astype(dx_dtype)

    row = pl.BlockSpec((tr, d), lambda i: (i, 0))
    wspec = pl.BlockSpec((1, d), lambda i: (0, 0))
    ins, specs = [x, w, dy], [pl.BlockSpec((tr, d), lambda i: (i, col // d)), wspec, row]
    if dres is not None:
        ins.append(dres)
        specs.append(row)
    return pl.pallas_call(
        body, name=name, grid=(t // tr,), in_specs=specs, out_specs=(row, wspec),
        out_shape=(jax.ShapeDtypeStruct((t, d), dx_dtype), jax.ShapeDtypeStruct((1, d), F32)),
        compiler_params=_params("arbitrary"),
    )(*ins)


def _elementwise(fn, ins, out_dtypes, *, name, width=None, cols=None):
    t = ins[0].shape[0]
    d = ins[0].shape[1] if width is None else width
    cols = [0] * len(ins) if cols is None else cols
    tc = _pick(d, 2048)
    for c in cols:
        tc = _pick(d, tc, c)
    tr = _row_tile(t, tc * 4)
    nout = len(out_dtypes)

    def body(*refs):
        outs = fn(*[r[...].astype(F32) for r in refs[:len(ins)]])
        for o_ref, o in zip(refs[len(ins):], outs):
            o_ref[...] = o.astype(o_ref.dtype)

    spec = pl.BlockSpec((tr, tc), lambda i, j: (i, j))
    in_specs = [pl.BlockSpec((tr, tc), lambda i, j, o=c // tc: (i, j + o)) for c in cols]
    return pl.pallas_call(
        body, name=name, grid=(t // tr, d // tc), in_specs=in_specs, out_specs=[spec] * nout,
        out_shape=[jax.ShapeDtypeStruct((t, d), dt) for dt in out_dtypes],
        compiler_params=_params("parallel", "parallel"),
    )(*ins)


def _merge_fwd(proj, col_a, col_b, ya, yb, *, name):
    return _elementwise(lambda a, b, p, q: (_sigmoid(a) * p + _sigmoid(b) * q,), [proj, proj, ya, yb], [BF16],
                        name=name, width=ya.shape[1], cols=[col_a, col_b, 0, 0])[0]


def _merge_bwd(dm, proj, col_a, col_b, ya, yb, *, name):
    def fn(dmv, a, b, p, q):
        sa, sb = _sigmoid(a), _sigmoid(b)
        return dmv * p * sa * (1.0 - sa), dmv * q * sb * (1.0 - sb), dmv * sa, dmv * sb

    return _elementwise(fn, [dm, proj, proj, ya, yb], [BF16, BF16, BF16, BF16], name=name, width=ya.shape[1],
                        cols=[0, col_a, col_b, 0, 0])


def _loss_head(xo, target, *, name):
    t, d = xo.shape
    tr = _pick(t, 256)

    def body(x_ref, t_ref, dx_ref, l_ref):
        @pl.when(pl.program_id(0) == 0)
        def _():
            l_ref[...] = jnp.zeros_like(l_ref)

        err = x_ref[...] - t_ref[...]
        dx_ref[...] = err * (1.0 / d)
        l_ref[...] += 0.5 * jnp.sum(jnp.mean(err * err, axis=-1, keepdims=True), axis=0, keepdims=True)

    row = pl.BlockSpec((tr, d), lambda i: (i, 0))
    dx, l = pl.pallas_call(
        body, name=name, grid=(t // tr,), in_specs=[row, row],
        out_specs=(row, pl.BlockSpec((1, 1), lambda i: (0, 0))),
        out_shape=(jax.ShapeDtypeStruct((t, d), F32), jax.ShapeDtypeStruct((1, 1), F32)),
        compiler_params=_params("arbitrary"),
    )(xo, target)
    return dx, l[0, 0]


def _rope(xin, tabs, *, name, group, backward, out_dtype, col=0, ngroup=None):
    t = xin.shape[0]
    ngroup = xin.shape[1] // group if ngroup is None else ngroup
    wdt = ngroup * group
    tr = _pick(t, 256)
    assert col % wdt == 0
    cos_t, nsin_t, sin_t = tabs

    def body(x_ref, c_ref, n_ref, s_ref, o_ref):
        cv, nv, sv = c_ref[...], n_ref[...], s_ref[...]
        for g in range(ngroup):
            lo, hi = g * group, (g + 1) * group
            rot = x_ref[:, hi - LANE:hi].astype(F32)
            if backward:
                out = rot * cv + pltpu.roll(rot * nv, 32, 1) + pltpu.roll(rot * sv, LANE - 32, 1)
            else:
                out = rot * cv + pltpu.roll(rot, LANE - 32, 1) * nv + pltpu.roll(rot, 32, 1) * sv
            if group > LANE:
                o_ref[:, lo:hi - LANE] = x_ref[:, lo:hi - LANE].astype(out_dtype)
            o_ref[:, hi - LANE:hi] = out.astype(out_dtype)

    xspec = pl.BlockSpec((tr, wdt), lambda i: (i, 0))
    tspec = pl.BlockSpec((tr, LANE), lambda i: (i, 0))
    return pl.pallas_call(
        body, name=name, grid=(t // tr,),
        in_specs=[pl.BlockSpec((tr, wdt), lambda i: (i, col // wdt)), tspec, tspec, tspec], out_specs=xspec,
        out_shape=jax.ShapeDtypeStruct((t, wdt), out_dtype), compiler_params=_params("parallel"),
    )(xin, cos_t, nsin_t, sin_t)


def _scores(q, kv, kr, qi, tq, scale):
    kcat = jnp.concatenate([kv[:, :HEAD], kr], axis=1)
    s = lax.dot_general(q, kcat, (((1,), (1,)), ((), ())), preferred_element_type=F32) * scale
    row = qi * tq + lax.broadcasted_iota(jnp.int32, s.shape, 0)
    col = lax.broadcasted_iota(jnp.int32, s.shape, 1)
    s = jnp.where(col <= row, s, -jnp.inf)
    p = jnp.exp(s - jnp.max(s, axis=-1, keepdims=True))
    return p / jnp.sum(p, axis=-1, keepdims=True), kcat


def _attn_fwd(qcat, kv, kr, *, name, scale):
    t = qcat.shape[0]
    nh = qcat.shape[1] // QGROUP
    tq = _pick(t, 256)

    def body(q_ref, kv_ref, kr_ref, o_ref):
        for qi in range(t // tq):
            @pl.when(pl.program_id(1) == qi)
            def _(qi=qi):
                kvv = kv_ref[0:(qi + 1) * tq, :]
                p, _ = _scores(q_ref[...], kvv, kr_ref[0:(qi + 1) * tq, :], qi, tq, scale)
                o_ref[...] = jnp.dot(p.astype(BF16), kvv[:, HEAD:], preferred_element_type=F32).astype(BF16)

    return pl.pallas_call(
        body, name=name, grid=(nh, t // tq),
        in_specs=[pl.BlockSpec((tq, QGROUP), lambda h, i: (i, h)), pl.BlockSpec((t, QGROUP), lambda h, i: (0, h)),
                  pl.BlockSpec((t, LANE), lambda h, i: (0, 0))],
        out_specs=pl.BlockSpec((tq, HEAD), lambda h, i: (i, h)),
        out_shape=jax.ShapeDtypeStruct((t, nh * HEAD), BF16), compiler_params=_params("parallel", "parallel"),
    )(qcat, kv, kr)


def _attn_bwd(qcat, kv, kr, do, *, name, scale):
    t = qcat.shape[0]
    nh = qcat.shape[1] // QGROUP
    tq = _pick(t, 256)
    nq = t // tq

    def body(q_ref, kv_ref, kr_ref, do_ref, dq_ref, dkv_ref, dkr_ref, dk_acc, dv_acc):
        h, i = pl.program_id(0), pl.program_id(1)

        @pl.when(i == 0)
        def _():
            dk_acc[...] = jnp.zeros_like(dk_acc)
            dv_acc[...] = jnp.zeros_like(dv_acc)

        @pl.when((i == 0) & (h == 0))
        def _():
            dkr_ref[...] = jnp.zeros_like(dkr_ref)

        for qi in range(nq):
            @pl.when(i == qi)
            def _(qi=qi):
                keys = slice(0, (qi + 1) * tq)
                q = q_ref[...]
                kvv = kv_ref[keys, :]
                dov = do_ref[...].astype(BF16)
                p, kcat = _scores(q, kvv, kr_ref[keys, :], qi, tq, scale)
                dp = lax.dot_general(dov, kvv[:, HEAD:], (((1,), (1,)), ((), ())), preferred_element_type=F32)
                ds = (p * (dp - jnp.sum(p * dp, axis=-1, keepdims=True)) * scale).astype(BF16)
                dq_ref[...] = jnp.dot(ds, kcat, preferred_element_type=F32)
                dk_acc[keys, :] += lax.dot_general(ds, q, (((0,), (0,)), ((), ())), preferred_element_type=F32)
                dv_acc[keys, :] += lax.dot_general(p.astype(BF16), dov, (((0,), (0,)), ((), ())), preferred_element_type=F32)

        @pl.when(i == nq - 1)
        def _():
            dk = dk_acc[...]
            dkv_ref[...] = jnp.concatenate([dk[:, :HEAD], dv_acc[...]], axis=1)
            dkr_ref[...] += dk[:, HEAD:]

    return pl.pallas_call(
        body, name=name, grid=(nh, nq),
        in_specs=[pl.BlockSpec((tq, QGROUP), lambda h, i: (i, h)), pl.BlockSpec((t, QGROUP), lambda h, i: (0, h)),
                  pl.BlockSpec((t, LANE), lambda h, i: (0, 0)), pl.BlockSpec((tq, HEAD), lambda h, i: (i, h))],
        out_specs=(pl.BlockSpec((tq, QGROUP), lambda h, i: (i, h)), pl.BlockSpec((t, QGROUP), lambda h, i: (0, h)),
                   pl.BlockSpec((t, LANE), lambda h, i: (0, 0))),
        out_shape=(jax.ShapeDtypeStruct((t, nh * QGROUP), F32), jax.ShapeDtypeStruct((t, nh * QGROUP), F32),
                   jax.ShapeDtypeStruct((t, LANE), F32)),
        scratch_shapes=[pltpu.VMEM((t, QGROUP), F32), pltpu.VMEM((t, HEAD), F32)],
        compiler_params=_params("arbitrary", "arbitrary"),
    )(qcat, kv, kr, do)


def _split3(x):
    hi = x.astype(BF16)
    r1 = x - hi.astype(F32)
    mid = r1.astype(BF16)
    lo = (r1 - mid.astype(F32)).astype(BF16)
    return hi, mid, lo


def _tri_matmul(mask, x):
    m = mask.astype(BF16)
    return sum(jnp.dot(m, part, preferred_element_type=F32) for part in _split3(x))


def _sub_cumsum(g, tb):
    row = lax.broadcasted_iota(jnp.int32, (tb, tb), 0)
    col = lax.broadcasted_iota(jnp.int32, (tb, tb), 1)
    return _tri_matmul(jnp.where((col <= row) & (col // SUB == row // SUB), 1.0, 0.0), g)


def _sub_suffix_prefix(after, before, tb):
    row = lax.broadcasted_iota(jnp.int32, (tb, tb), 0)
    col = lax.broadcasted_iota(jnp.int32, (tb, tb), 1)
    same = col // SUB == row // SUB
    return (_tri_matmul(jnp.where((col >= row) & same, 1.0, 0.0), after)
            + _tri_matmul(jnp.where((col < row) & same, 1.0, 0.0), before))


def _lower_bound(logits):
    mx = jnp.max(logits, axis=0, keepdims=True)
    e = jnp.exp(logits - mx)
    return e[0:1, :] / jnp.sum(e, axis=0, keepdims=True)


def _hgrn_fwd(proj, cols, wdt, logits, out_norm, *, name):
    t = proj.shape[0]
    nh = wdt // HEAD
    tb = _pick(t, 128)
    ns = tb // SUB

    def body(hq_ref, hf_ref, hi_ref, hg_ref, lg_ref, w_ref, o_ref, yb_ref, st_ref, s_ref, q_s, k_s, b_s):
        @pl.when(pl.program_id(1) == 0)
        def _():
            s_ref[...] = jnp.zeros_like(s_ref)

        lb = _lower_bound(lg_ref[...])
        f = lb + (1.0 - lb) * _sigmoid(hf_ref[...])
        q_s[...] = _silu(hq_ref[...])
        k_s[...] = 1.0 - f
        b_s[...] = _sub_cumsum(jnp.log(f), tb)
        rowid = lax.broadcasted_iota(jnp.int32, (SUB, HEAD), 0)

        def sub(c, st):
            rows = pl.ds(pl.multiple_of(c * SUB, SUB), SUB)
            qc, kc, bc, vc = q_s[rows, :], k_s[rows, :], b_s[rows, :], hi_ref[rows, :]
            st_ref[0, c] = st
            bl = bc[SUB - 1:SUB, :]
            oc = lax.dot_general((qc * jnp.exp(bc)).astype(BF16), st.astype(BF16), (((1,), (1,)), ((), ())),
                                 preferred_element_type=F32)
            for s in range(SUB):
                e = jnp.where(rowid >= s, jnp.exp(bc - bc[s:s + 1, :]), 0.0)
                a = jnp.sum(qc * e * kc[s:s + 1, :], axis=1, keepdims=True)
                oc = oc + a * vc[s:s + 1, :]
            o_ref[rows, :] = oc
            kd = kc * jnp.exp(bl - bc)
            return jnp.exp(bl) * st + lax.dot_general(vc.astype(BF16), kd.astype(BF16), (((0,), (0,)), ((), ())),
                                                      preferred_element_type=F32)

        s_ref[...] = lax.fori_loop(0, ns, sub, s_ref[...], unroll=True)
        o = o_ref[...]
        r = lax.rsqrt(jnp.mean(o * o, axis=-1, keepdims=True) + NORM_EPS)
        yb_ref[...] = (o * r * w_ref[...] * _silu(hg_ref[...])).astype(BF16)

    blk = pl.BlockSpec((tb, HEAD), lambda h, j: (j, h))
    return pl.pallas_call(
        body, name=name, grid=(nh, t // tb),
        in_specs=[pl.BlockSpec((tb, HEAD), lambda h, j, o=c // HEAD: (j, h + o)) for c in cols]
        + [pl.BlockSpec((2, HEAD), lambda h, j: (0, h)), pl.BlockSpec((1, HEAD), lambda h, j: (0, 0))],
        out_specs=(blk, blk, pl.BlockSpec((1, ns, HEAD, HEAD), lambda h, j: (h, j, 0, 0))),
        out_shape=(jax.ShapeDtypeStruct((t, wdt), F32), jax.ShapeDtypeStruct((t, wdt), BF16),
                   jax.ShapeDtypeStruct((nh, t // SUB, HEAD, HEAD), F32)),
        scratch_shapes=[pltpu.VMEM((HEAD, HEAD), F32)] + [pltpu.VMEM((tb, HEAD), F32)] * 3,
        compiler_params=_params("parallel", "arbitrary"),
    )(proj, proj, proj, proj, logits, out_norm)


def _hgrn_bwd(proj, cols, wdt, o_raw, dyb, states, logits, out_norm, *, name):
    t = proj.shape[0]
    nh = wdt // HEAD
    tb = _pick(t, 128)
    ns = tb // SUB
    nb = t // tb

    def body(hq_ref, hf_ref, hi_ref, hg_ref, o_ref, dy_ref, st_ref, lg_ref, w_ref,
             dhq_ref, dhf_ref, dhi_ref, dhg_ref, dlb_ref, dw_ref,
             ds_ref, q_s, k_s, b_s, do_s, dq_s, dk_s, dv_s, after_s, before_s, thru_s):
        @pl.when(pl.program_id(1) == 0)
        def _():
            ds_ref[...] = jnp.zeros_like(ds_ref)
            dlb_ref[...] = jnp.zeros_like(dlb_ref)
            dw_ref[...] = jnp.zeros_like(dw_ref)

        lb = _lower_bound(lg_ref[...])
        hqv, hgv = hq_ref[...], hg_ref[...]
        sig = _sigmoid(hf_ref[...])
        f = lb + (1.0 - lb) * sig
        q_s[...] = _silu(hqv)
        k_s[...] = 1.0 - f
        b_s[...] = _sub_cumsum(jnp.log(f), tb)

        o = o_ref[...]
        r = lax.rsqrt(jnp.mean(o * o, axis=-1, keepdims=True) + NORM_EPS)
        nrm = o * r
        w = w_ref[...]
        dy = dy_ref[...].astype(F32)
        dhg_ref[...] = (dy * nrm * w * _dsilu(hgv)).astype(BF16)
        dnw = dy * _silu(hgv)
        dw_ref[0] += jnp.sum(dnw * nrm, axis=0, keepdims=True)
        tt = dnw * w
        do_s[...] = r * (tt - nrm * jnp.mean(tt * nrm, axis=-1, keepdims=True))
        rowid = lax.broadcasted_iota(jnp.int32, (SUB, HEAD), 0)

        def sub(cc, dst):
            c = ns - 1 - cc
            rows = pl.ds(pl.multiple_of(c * SUB, SUB), SUB)
            qc, kc, bc, vc, doc = q_s[rows, :], k_s[rows, :], b_s[rows, :], hi_ref[rows, :], do_s[rows, :]
            st = st_ref[0, c]
            bl = bc[SUB - 1:SUB, :]
            eb = jnp.exp(bc)
            ekd = jnp.exp(bl - bc)
            qe, kd = qc * eb, kc * ekd
            dob, vcb = doc.astype(BF16), vc.astype(BF16)
            dq_st = jnp.dot(dob, st.astype(BF16), preferred_element_type=F32) * eb
            dk_st = jnp.dot(vcb, dst.astype(BF16), preferred_element_type=F32) * ekd
            dv = lax.dot_general(kd.astype(BF16), dst.astype(BF16), (((1,), (1,)), ((), ())), preferred_element_type=F32)
            dq_in = jnp.zeros_like(qc)
            dk_in = jnp.zeros_like(qc)
            for s in range(SUB):
                e = jnp.where(rowid >= s, jnp.exp(bc - bc[s:s + 1, :]), 0.0)
                ek = e * kc[s:s + 1, :]
                a = jnp.sum(qc * ek, axis=1, keepdims=True)
                da = jnp.sum(doc * vc[s:s + 1, :], axis=1, keepdims=True)
                dq_in = dq_in + da * ek
                dk_in = dk_in + jnp.where(rowid == s, jnp.sum(da * e * qc, axis=0, keepdims=True), 0.0)
                dv = dv + jnp.where(rowid == s, jnp.sum(a * doc, axis=0, keepdims=True), 0.0)
            ebl = jnp.exp(bl)
            dq_s[rows, :] = dq_st + dq_in
            dk_s[rows, :] = dk_st + dk_in
            dv_s[rows, :] = dv
            after_s[rows, :] = qc * (dq_st + dq_in) - kc * dk_in
            before_s[rows, :] = kc * dk_st
            thru_s[rows, :] = jnp.broadcast_to(ebl * jnp.sum(st * dst, axis=0, keepdims=True), (SUB, HEAD))
            return ebl * dst + lax.dot_general(dob, qe.astype(BF16), (((0,), (0,)), ((), ())), preferred_element_type=F32)

        ds_ref[...] = lax.fori_loop(0, ns, sub, ds_ref[...], unroll=True)
        dg = _sub_suffix_prefix(after_s[...], before_s[...], tb) + thru_s[...]
        dhq_ref[...] = (dq_s[...] * _dsilu(hqv)).astype(BF16)
        dft = dg / f - dk_s[...]
        dhf_ref[...] = (dft * (1.0 - lb) * sig * (1.0 - sig)).astype(BF16)
        dlb_ref[0] += jnp.sum(dft * (1.0 - sig), axis=0, keepdims=True)
        dhi_ref[...] = dv_s[...].astype(BF16)

    blk = pl.BlockSpec((tb, HEAD), lambda h, j: (nb - 1 - j, h))
    vec = pl.BlockSpec((1, 1, HEAD), lambda h, j: (h, 0, 0))
    tok = jax.ShapeDtypeStruct((t, wdt), BF16)
    per_head = jax.ShapeDtypeStruct((nh, 1, HEAD), F32)
    return pl.pallas_call(
        body, name=name, grid=(nh, nb),
        in_specs=[pl.BlockSpec((tb, HEAD), lambda h, j, o=c // HEAD: (nb - 1 - j, h + o)) for c in cols]
        + [blk, blk] + [pl.BlockSpec((1, ns, HEAD, HEAD), lambda h, j: (h, nb - 1 - j, 0, 0)),
                              pl.BlockSpec((2, HEAD), lambda h, j: (0, h)), pl.BlockSpec((1, HEAD), lambda h, j: (0, 0))],
        out_specs=(blk, blk, blk, blk, vec, vec),
        out_shape=(tok, tok, tok, tok, per_head, per_head),
        scratch_shapes=[pltpu.VMEM((HEAD, HEAD), F32)] + [pltpu.VMEM((tb, HEAD), F32)] * 10,
        compiler_params=_params("arbitrary", "arbitrary"),
    )(proj, proj, proj, proj, o_raw, dyb, states, logits, out_norm)


def _lb_logits_grad(logits, dlb, *, name):
    def body(lg_ref, d_ref, o_ref):
        lg = lg_ref[...]
        e = jnp.exp(lg - jnp.max(lg, axis=0, keepdims=True))
        p = e / jnp.sum(e, axis=0, keepdims=True)
        d = d_ref[...]
        rowid = lax.broadcasted_iota(jnp.int32, lg.shape, 0)
        dp = jnp.where(rowid == 0, d, 0.0)
        o_ref[...] = p * (dp - jnp.sum(p * dp, axis=0, keepdims=True))

    return pl.pallas_call(body, name=name, out_shape=jax.ShapeDtypeStruct(logits.shape, F32))(logits, dlb)


def _adamw(w, g, m, v, *, name, deps=()):
    r, c = w.shape
    tc = _pick(c, 2048) if c % LANE == 0 else c
    tr = _row_tile(r, tc * 4)

    def body(w_ref, g_ref, m_ref, v_ref, *rest):
        d_ref, nm_ref, nv_ref = rest[-3:]
        gv = g_ref[...]
        nm = ADAM_B1 * m_ref[...] + (1.0 - ADAM_B1) * gv
        nv = ADAM_B2 * v_ref[...] + (1.0 - ADAM_B2) * (gv * gv)
        m_hat = nm / (1.0 - ADAM_B1 ** ADAM_STEP)
        v_hat = nv / (1.0 - ADAM_B2 ** ADAM_STEP)
        d_ref[...] = -ADAM_LR * (m_hat / (jnp.sqrt(v_hat) + ADAM_EPS) + ADAM_WD * w_ref[...])
        nm_ref[...] = nm
        nv_ref[...] = nv

    spec = pl.BlockSpec((tr, tc), lambda i, j: (i, j))
    shp = jax.ShapeDtypeStruct((r, c), F32)
    return pl.pallas_call(
        body, name=name, grid=(r // tr, c // tc), in_specs=[spec] * 4 + [ANY] * len(deps), out_specs=[spec] * 3,
        out_shape=[shp, shp, shp], compiler_params=_params("parallel", "parallel"),
    )(w, g, m, v, *deps)


def _coords():
    return lax.axis_index("x"), lax.axis_index("y"), lax.axis_index("c")


def _other_chips(x, y):
    return [(1 - x, y), (x, 1 - y), (1 - x, 1 - y)]


ANY = pl.BlockSpec(memory_space=pl.ANY)


class _Layout:
    def __init__(self, d, dff, in_cols, q_lora, kv_lora, nh):
        assert q_lora == kv_lora and nh % 4 == 0 and dff % (4 * LANE) == 0 and in_cols % 4 == 0 and d % 4 == 0
        self.d, self.dff, self.q_lora, self.nh = d, dff, q_lora, nh
        self.head = q_lora + kv_lora + ROPE
        self.pad = d - self.head
        self.nff, self.ncol, self.r_o, self.hps = dff // 4, in_cols // 4, d // 4, nh // 4
        assert self.head <= self.ncol
        self.off_q, self.off_kv, self.rows_narrow = 0, nh * QGROUP, 2 * nh * QGROUP


HBM = pl.BlockSpec(memory_space=pltpu.HBM)
SEMS = pl.BlockSpec(memory_space=pltpu.SEMAPHORE)
SPLIT = dict(has_side_effects=pltpu.SideEffectType.DATAFLOW_SIDE_EFFECTING)


def _in_hbm(a):
    return pltpu.with_memory_space_constraint(a, pltpu.HBM)


def _shard_rows(jobs, k):
    out, lrow = [], [0, 0]
    for job in jobs:
        for row, rows in job.pieces(k):
            out.append((job.a, lrow[job.a], row, rows))
            lrow[job.a] += rows
    return out


def _shard_total(jobs, a):
    return sum(rows for b, _, _, rows in _shard_rows(jobs, 0) if b == a)


def _gather_start(packs, lands, jobs, *, name, deps=()):
    n = len(packs)

    def body(*refs):
        p_refs, l_refs, send, recv, token = refs[:n], refs[n:2 * n], refs[-2 * n - 3], refs[-2 * n - 2], refs[-1]
        x, y, c = _coords()
        for a, lrow, row, rows in _shard_rows(jobs, 2 * x + y):
            pltpu.make_async_remote_copy(
                src_ref=p_refs[a].at[:, pl.ds(lrow, rows)], dst_ref=l_refs[a].at[:, pl.ds(row, rows)],
                send_sem=send.at[4 * a + 3], recv_sem=recv.at[4 * a + 3], device_id=(x, y, 1 - c), device_id_type=MESH).start()
            for j, (px, py) in enumerate(_other_chips(x, y)):
                pltpu.make_async_remote_copy(
                    src_ref=p_refs[a].at[c, pl.ds(lrow, rows)], dst_ref=l_refs[a].at[c, pl.ds(row, rows)],
                    send_sem=send.at[4 * a + j], recv_sem=recv.at[4 * a + j], device_id=(px, py, c), device_id_type=MESH).start()
        token[...] = jnp.zeros_like(token)

    thru = [pltpu.HBM(a.shape, a.dtype) for a in packs + lands]
    out = pl.pallas_call(
        body, name=name, in_specs=[HBM] * (2 * n) + [ANY] * len(deps),
        out_shape=(pltpu.SemaphoreType.DMA((4 * n,)), pltpu.SemaphoreType.DMA((4 * n,)), *thru, jax.ShapeDtypeStruct((8, LANE), F32)),
        out_specs=(SEMS, SEMS, *[HBM] * (2 * n), pl.BlockSpec(memory_space=pltpu.VMEM)),
        input_output_aliases={i: 2 + i for i in range(2 * n)}, compiler_params=pltpu.CompilerParams(**SPLIT),
    )(*[_in_hbm(a) for a in packs + lands], *deps)
    return dict(send=out[0], recv=out[1], bufs=list(out[2:2 + 2 * n]), n=n, jobs=jobs), out[-1]


def _gather_wait(handle, after, *, name):
    n, jobs = handle["n"], handle["jobs"]

    def body(*refs):
        l_refs, send, recv, token = refs[n:2 * n], refs[2 * n], refs[2 * n + 1], refs[-1]
        token[...] = jnp.zeros_like(token)
        x, y, c = _coords()
        for a in range(n):
            total = _shard_total(jobs, a)
            for j, like in enumerate([l_refs[a].at[0, pl.ds(0, total)]] * 3 + [l_refs[a].at[:, pl.ds(0, total)]]):
                cp = pltpu.make_async_remote_copy(src_ref=like, dst_ref=like, send_sem=send.at[4 * a + j],
                                                  recv_sem=recv.at[4 * a + j], device_id=(x, y, c), device_id_type=MESH)
                cp.wait_send()
                cp.wait_recv()

    out = pl.pallas_call(
        body, name=name, in_specs=[HBM] * (2 * n) + [SEMS, SEMS] + [ANY] * len(after),
        out_shape=[pltpu.HBM(a.shape, a.dtype) for a in handle["bufs"]] + [jax.ShapeDtypeStruct((8, LANE), F32)],
        out_specs=[HBM] * (2 * n) + [pl.BlockSpec(memory_space=pltpu.VMEM)],
        input_output_aliases={i: i for i in range(2 * n)}, compiler_params=pltpu.CompilerParams(**SPLIT),
    )(*handle["bufs"], handle["send"], handle["recv"], *after)
    return list(out[n:2 * n]), out[-1]


def _gather_forward(lands, jobs, *, name, deps=()):
    n = len(lands)

    def body(*refs):
        l_refs, send, recv = refs[n + len(deps):2 * n + len(deps)], refs[-2], refs[-1]
        x, y, c = _coords()
        for j, (px, py) in enumerate(_other_chips(x, y)):
            for a, _, row, rows in _shard_rows(jobs, 2 * px + py):
                blk = l_refs[a].at[c, pl.ds(row, rows)]
                pltpu.make_async_remote_copy(src_ref=blk, dst_ref=blk, send_sem=send.at[3 * a + j], recv_sem=recv.at[3 * a + j],
                                             device_id=(x, y, 1 - c), device_id_type=MESH).start()
        for a in range(n):
            like = l_refs[a].at[0, pl.ds(0, _shard_total(jobs, a))]
            for j in range(3):
                cp = pltpu.make_async_remote_copy(src_ref=like, dst_ref=like, send_sem=send.at[3 * a + j],
                                                  recv_sem=recv.at[3 * a + j], device_id=(x, y, c), device_id_type=MESH)
                cp.wait_send()
                cp.wait_recv()

    sem = pltpu.SemaphoreType.DMA((3 * n,))
    return pl.pallas_call(
        body, name=name, in_specs=[ANY] * (n + len(deps)), out_specs=[ANY] * n, input_output_aliases={i: i for i in range(n)},
        out_shape=[jax.ShapeDtypeStruct(a.shape, a.dtype) for a in lands], scratch_shapes=[sem, sem],
    )(*lands, *deps)


def _add_sibling(g, recv, sel, *, name):
    rows, hw = recv.shape
    tr = _row_tile(rows, hw * 4)

    def body(sel_ref, g_ref, r_ref, o_ref):
        o_ref[...] = (g_ref[...] + r_ref[...]).astype(BF16)

    return pl.pallas_call(
        body, name=name, out_shape=jax.ShapeDtypeStruct((rows, hw), BF16),
        grid_spec=pltpu.PrefetchScalarGridSpec(
            num_scalar_prefetch=1, grid=(rows // tr,),
            in_specs=[pl.BlockSpec((None, tr, hw), lambda i, s: (s[0], i, 0)), pl.BlockSpec((tr, hw), lambda i, s: (i, 0))],
            out_specs=pl.BlockSpec((tr, hw), lambda i, s: (i, 0))),
        compiler_params=_params("parallel"),
    )(sel, g, recv)


class _Job:
    def __init__(self, a, blk, n_outer, n_inner, stride, start):
        self.a, self.blk, self.n_outer, self.n_inner, self.stride, self.start = a, blk, n_outer, n_inner, stride, start
        self.rows_out = n_outer * n_inner * blk

    def pieces(self, k):
        return [(self.start(k) + o * self.stride * self.blk, self.n_inner * self.blk) for o in range(self.n_outer)]


def _block_rows(rows, cap, *also):
    best = None
    for b in range(16, min(rows, cap) + 1, 16):
        if rows % b == 0 and all(v % b == 0 for v in also):
            best = b
    assert best is not None, (rows, also)
    return best


def _ffn_jobs(lay):
    b = _block_rows(lay.nff, 704, lay.dff)
    return [_Job(0, b, 3, lay.nff // b, lay.dff // b, lambda k: lay.nff * k)]


def _mix_jobs(lay):
    d, ncol, head, pad = lay.d, lay.ncol, lay.head, lay.pad
    first = lambda k, a, b: jnp.where(k == 0, a, b) if not isinstance(k, int) else (a if k == 0 else b)
    ba = _block_rows(head, 704, *[ncol * k + pad for k in (1, 2, 3)])
    bb = _block_rows(ncol - head, 704, *[ncol * k + d for k in (0, 1, 2, 3)])
    bo = _block_rows(lay.r_o, 704, d)
    bq = _block_rows(HEAD + ROPE, 704, QGROUP)
    bk = _block_rows(lay.hps * QGROUP, 704, lay.off_kv)
    return [_Job(0, ba, 1, head // ba, 0, lambda k: first(k, 0, ncol * k + pad)),
            _Job(0, bb, 1, (ncol - head) // bb, 0, lambda k: ncol * k + d),
            _Job(0, bo, 3, lay.r_o // bo, d // bo, lambda k: 7 * d + lay.r_o * k),
            _Job(1, bq, lay.hps, (HEAD + ROPE) // bq, QGROUP // bq, lambda k: QGROUP * lay.hps * k),
            _Job(1, bk, 1, lay.hps * QGROUP // bk, 0, lambda k: lay.off_kv + lay.hps * QGROUP * k)]


def _swap_start(gs, *, name):
    n = len(gs)
    lands = [lax.empty(g.shape[1:], g.dtype) for g in gs]

    def body(*refs):
        g_refs, land_refs, send, recv, token = refs[:n], refs[n:2 * n], refs[2 * n], refs[2 * n + 1], refs[-1]
        x, y, c = _coords()
        for a in range(n):
            pltpu.make_async_remote_copy(src_ref=g_refs[a].at[1 - c], dst_ref=land_refs[a], send_sem=send.at[a],
                                         recv_sem=recv.at[a], device_id=(x, y, 1 - c), device_id_type=MESH).start()
        token[...] = jnp.zeros_like(token)

    thru = [pltpu.HBM(a.shape, a.dtype) for a in gs + lands]
    out = pl.pallas_call(
        body, name=name, in_specs=[HBM] * (2 * n),
        out_shape=(pltpu.SemaphoreType.DMA((n,)), pltpu.SemaphoreType.DMA((n,)), *thru, jax.ShapeDtypeStruct((8, LANE), F32)),
        out_specs=(SEMS, SEMS, *[HBM] * (2 * n), pl.BlockSpec(memory_space=pltpu.VMEM)),
        input_output_aliases={i: 2 + i for i in range(2 * n)}, compiler_params=pltpu.CompilerParams(**SPLIT),
    )(*[_in_hbm(a) for a in gs + lands])
    return dict(send=out[0], recv=out[1], bufs=list(out[2:2 + 2 * n]), n=n), out[-1]


def _swap_wait(handle, after, *, name):
    n = handle["n"]

    def body(*refs):
        g_refs, land_refs, send, recv = refs[:n], refs[n:2 * n], refs[2 * n], refs[2 * n + 1]
        x, y, c = _coords()
        for a in range(n):
            cp = pltpu.make_async_remote_copy(src_ref=g_refs[a].at[1 - c], dst_ref=land_refs[a], send_sem=send.at[a],
                                              recv_sem=recv.at[a], device_id=(x, y, 1 - c), device_id_type=MESH)
            cp.wait_send()
            cp.wait_recv()

    out = pl.pallas_call(
        body, name=name, in_specs=[HBM] * (2 * n) + [SEMS, SEMS] + [ANY] * len(after),
        out_shape=[pltpu.HBM(a.shape, a.dtype) for a in handle["bufs"]], out_specs=[HBM] * (2 * n),
        input_output_aliases={i: i for i in range(2 * n)}, compiler_params=pltpu.CompilerParams(**SPLIT),
    )(*handle["bufs"], handle["send"], handle["recv"], *after)
    return list(out[:n]), list(out[n:])


def _exchange_start(ss, jobs, *, name):
    n = len(ss)
    lands = [lax.empty((3,) + s.shape, s.dtype) for s in ss]

    def body(*refs):
        s_refs, land_refs, send, recv, token = refs[:n], refs[n:2 * n], refs[2 * n], refs[2 * n + 1], refs[-1]
        x, y, c = _coords()
        for j, (px, py) in enumerate(_other_chips(x, y)):
            for job in jobs:
                for row, rows in job.pieces(2 * px + py):
                    pltpu.make_async_remote_copy(
                        src_ref=s_refs[job.a].at[pl.ds(row, rows)], dst_ref=land_refs[job.a].at[j, pl.ds(row, rows)],
                        send_sem=send.at[n * j + job.a], recv_sem=recv.at[n * j + job.a], device_id=(px, py, c),
                        device_id_type=MESH).start()
        token[...] = jnp.zeros_like(token)

    thru = [pltpu.HBM(a.shape, a.dtype) for a in ss + lands]
    out = pl.pallas_call(
        body, name=name, in_specs=[HBM] * (2 * n),
        out_shape=(pltpu.SemaphoreType.DMA((3 * n,)), pltpu.SemaphoreType.DMA((3 * n,)), *thru, jax.ShapeDtypeStruct((8, LANE), F32)),
        out_specs=(SEMS, SEMS, *[HBM] * (2 * n), pl.BlockSpec(memory_space=pltpu.VMEM)),
        input_output_aliases={i: 2 + i for i in range(2 * n)}, compiler_params=pltpu.CompilerParams(**SPLIT),
    )(*[_in_hbm(a) for a in ss + lands])
    return dict(send=out[0], recv=out[1], bufs=list(out[2:2 + 2 * n]), n=n, jobs=jobs), out[-1]


def _exchange_wait(handle, after, *, name):
    n, jobs = handle["n"], handle["jobs"]
    total = [sum(rows for job in jobs if job.a == a for _, rows in job.pieces(0)) for a in range(n)]

    def body(*refs):
        s_refs, land_refs, send, recv = refs[:n], refs[n:2 * n], refs[2 * n], refs[2 * n + 1]
        x, y, c = _coords()
        for a in range(n):
            for j in range(3):
                all_rows = land_refs[a].at[0, pl.ds(0, total[a])]
                cp = pltpu.make_async_remote_copy(src_ref=all_rows, dst_ref=all_rows, send_sem=send.at[n * j + a],
                                                  recv_sem=recv.at[n * j + a], device_id=(x, y, c), device_id_type=MESH)
                cp.wait_send()
                cp.wait_recv()

    out = pl.pallas_call(
        body, name=name, in_specs=[HBM] * (2 * n) + [SEMS, SEMS] + [ANY] * len(after),
        out_shape=[pltpu.HBM(a.shape, a.dtype) for a in handle["bufs"]], out_specs=[HBM] * (2 * n),
        input_output_aliases={i: i for i in range(2 * n)}, compiler_params=pltpu.CompilerParams(**SPLIT),
    )(*handle["bufs"], handle["send"], handle["recv"], *after)
    return list(out[:n]), list(out[n:])


def _add_shard(s, land, job, sel, k, *, name):
    hw = s.shape[1]
    blk, no, ni, stride = job.blk, job.n_outer, job.n_inner, job.stride
    scal = jnp.stack([sel, job.start(k) // blk]).astype(jnp.int32)

    def body(sc_ref, own_ref, r_ref, o_ref):
        o_ref[...] = ((own_ref[...].astype(F32) + r_ref[0].astype(F32)) + r_ref[1].astype(F32)) + r_ref[2].astype(F32)

    return pl.pallas_call(
        body, name=name, out_shape=jax.ShapeDtypeStruct((2, job.rows_out, hw), F32),
        grid_spec=pltpu.PrefetchScalarGridSpec(
            num_scalar_prefetch=1, grid=(no, ni),
            in_specs=[pl.BlockSpec((blk, hw), lambda o, b, sc: (sc[1] + o * stride + b, 0)),
                      pl.BlockSpec((3, blk, hw), lambda o, b, sc: (0, sc[1] + o * stride + b, 0))],
            out_specs=pl.BlockSpec((None, blk, hw), lambda o, b, sc: (sc[0], o * ni + b, 0))),
        compiler_params=_params("parallel", "parallel"),
    )(scal, s, land)


def _join_list(fs, *, name):
    n = len(fs)

    def body(*refs):
        f_refs, send_sems, recv_sems = refs[n:2 * n], refs[2 * n], refs[2 * n + 1]
        x, y, c = _coords()
        copies = [pltpu.make_async_remote_copy(
            src_ref=f.at[c], dst_ref=f.at[c], send_sem=send_sems.at[a], recv_sem=recv_sems.at[a],
            device_id=(x, y, 1 - c), device_id_type=MESH) for a, f in enumerate(f_refs)]
        for cp in copies:
            cp.start()
        for cp in copies:
            cp.wait()

    sem = pltpu.SemaphoreType.DMA((n,))
    return pl.pallas_call(
        body, name=name, in_specs=[ANY] * n, out_specs=[ANY] * n, input_output_aliases={i: i for i in range(n)},
        out_shape=[jax.ShapeDtypeStruct(f.shape, f.dtype) for f in fs], scratch_shapes=[sem, sem],
    )(*fs)


def _all_reduce_small(vec, *, name):
    n = vec.shape[1]

    def body(v_ref, o_ref, buf, send_sems, recv_sems):
        x, y, c = _coords()
        me = 4 * x + 2 * y + c
        buf[me] = v_ref[...]
        copies = []
        for m in range(1, 8):
            peer = (x ^ ((m >> 2) & 1), y ^ ((m >> 1) & 1), c ^ (m & 1))
            copies.append(pltpu.make_async_remote_copy(
                src_ref=v_ref, dst_ref=buf.at[me], send_sem=send_sems.at[m - 1], recv_sem=recv_sems.at[m - 1],
                device_id=peer, device_id_type=MESH))
        for cp in copies:
            cp.start()
        for cp in copies:
            cp.wait()
        acc = buf[0]
        for d in range(1, 8):
            acc = acc + buf[d]
        o_ref[...] = acc

    return pl.pallas_call(
        body, name=name, out_shape=jax.ShapeDtypeStruct((1, n), F32),
        in_specs=[pl.BlockSpec(memory_space=pltpu.VMEM)], out_specs=pl.BlockSpec(memory_space=pltpu.VMEM),
        scratch_shapes=[pltpu.VMEM((8, 1, n), F32), pltpu.SemaphoreType.DMA((7,)), pltpu.SemaphoreType.DMA((7,))],
    )(vec)


def _ffn_fwd(x, n_pre, n_post, wbuf, lay, tag, deps=()):
    wg, wu, wd = ((wbuf, i * lay.dff, lay.dff) for i in range(3))
    h = _norm_fwd(x, n_pre, name=f"{tag}_norm_pre", out_dtype=BF16)
    g = _mm([(h, wg)], name=f"{tag}_gate", mode="nt", deps=deps)
    u, a = _mm([(h, wu)], name=f"{tag}_up", mode="nt", extras=[g], out_dtypes=[F32, BF16], tm_cap=MM_TILE // 2,
               epilogue=lambda up, gate: (up, _silu(gate) * up))
    yv = _mm([(a, wd)], name=f"{tag}_down", mode="nn")
    out = _norm_fwd(yv, n_post, name=f"{tag}_norm_post", resid=x, scale=MACARON_SCALE)
    return out, (x, h, g, u, a, yv)


def _ffn_bwd(dout, saved, n_pre, n_post, wbuf, lay, tag, deps=(), after_act=None, after_dw=None):
    x, h, g, u, a, yv = saved
    dff = lay.dff
    gbuf = lax.empty((2, 3 * dff, lay.d // 2), F32)
    dy, dn_post = _norm_bwd(yv, n_post, dout, name=f"{tag}_norm_post_bwd", scale=MACARON_SCALE)
    dg, du = _mm([(dy, (wbuf, 2 * dff, dff))], name=f"{tag}_down_dx", mode="nt", deps=deps, extras=[g, u],
                 out_dtypes=[BF16, BF16], tm_cap=MM_TILE // 2,
                 epilogue=lambda da, gate, up: (da * up * _dsilu(gate), da * _silu(gate)))
    deps = after_act(du) if after_act is not None else ()
    gbuf = _mm([(a, dy)], name=f"{tag}_down_dw", mode="tn", into=(gbuf, 2 * dff), deps=deps)
    gbuf = _mm([(dg, h)], name=f"{tag}_gate_dw", mode="tn", into=(gbuf, 0))
    gbuf = _mm([(du, h)], name=f"{tag}_up_dw", mode="tn", into=(gbuf, dff))
    deps = after_dw(gbuf)
    dh = _mm([(dg, (wbuf, 0, dff)), (du, (wbuf, dff, dff))], name=f"{tag}_up_dx", mode="nn", deps=deps)
    dx, dn_pre = _norm_bwd(x, n_pre, dh, name=f"{tag}_norm_pre_bwd", dres=dout)
    return dx, dn_pre, dn_post


def _rope_tables(positions):
    half = ROPE // 2
    inv_freq = ROPE_THETA ** (-jnp.arange(half, dtype=F32) / half)
    ang = positions.astype(F32)[:, None] * inv_freq
    cos, sin = jnp.cos(ang), jnp.sin(ang)
    z = jnp.zeros_like(cos)
    z2 = jnp.zeros((positions.shape[0], LANE - ROPE), F32)
    return (jnp.concatenate([cos, cos, z2], axis=1), jnp.concatenate([-sin, z, z2], axis=1),
            jnp.concatenate([z, sin, z2], axis=1))


def kernel(x, positions, ffn1_norm_pre, ffn1_w_gate, ffn1_w_up, ffn1_w_down, ffn1_norm_post, mix_norm_pre, w_in, mla_q_norm, mla_w_q_up, mla_kv_norm, mla_w_kv_up, mla_w_o, hgrn_lb_logits, hgrn_out_norm, hgrn_w_o, w_out, mix_norm_post, ffn2_norm_pre, ffn2_w_gate, ffn2_w_up, ffn2_w_down, ffn2_norm_post, loss_target, m_ffn1_norm_pre, m_ffn1_w_gate, m_ffn1_w_up, m_ffn1_w_down, m_ffn1_norm_post, m_mix_norm_pre, m_w_in, m_mla_q_norm, m_mla_w_q_up, m_mla_kv_norm, m_mla_w_kv_up, m_mla_w_o, m_hgrn_lb_logits, m_hgrn_out_norm, m_hgrn_w_o, m_w_out, m_mix_norm_post, m_ffn2_norm_pre, m_ffn2_w_gate, m_ffn2_w_up, m_ffn2_w_down, m_ffn2_norm_post, v_ffn1_norm_pre, v_ffn1_w_gate, v_ffn1_w_up, v_ffn1_w_down, v_ffn1_norm_post, v_mix_norm_pre, v_w_in, v_mla_q_norm, v_mla_w_q_up, v_mla_kv_norm, v_mla_w_kv_up, v_mla_w_o, v_hgrn_lb_logits, v_hgrn_out_norm, v_hgrn_w_o, v_w_out, v_mix_norm_post, v_ffn2_norm_pre, v_ffn2_w_gate, v_ffn2_w_up, v_ffn2_w_down, v_ffn2_norm_post):
    given = dict(locals())
    wts = {n: given[n] for n in ALL_WEIGHTS}
    mom = {n: given["m_" + n] for n in ALL_WEIGHTS}
    var = {n: given["v_" + n] for n in ALL_WEIGHTS}
    xin = x[0]
    target = loss_target[0]
    t, d = xin.shape
    cx, cy, cc = _coords()

    q_lora, kv_lora = mla_q_norm.shape[1], mla_kv_norm.shape[1]
    nh_mla = 4 * mla_w_kv_up.shape[2] // QGROUP
    lay = _Layout(d, 4 * ffn1_w_gate.shape[2], 4 * w_in.shape[2], q_lora, kv_lora, nh_mla)
    jobs_ffn, jobs_mix = _ffn_jobs(lay), _mix_jobs(lay)
    def pack(src, col_sharded, row_sharded=()):
        a = jnp.concatenate([src[n][0].T.astype(BF16) for n in col_sharded] + [src[n][0].astype(BF16) for n in row_sharded])
        return a.reshape(a.shape[0], 2, a.shape[1] // 2).transpose(1, 0, 2)

    ffn_land = lambda: [lax.empty((2, 3 * lay.dff, d // 2), BF16)]
    got1, tok = _gather_start([pack(wts, ["ffn1_w_gate", "ffn1_w_up"], ["ffn1_w_down"])], ffn_land(), jobs_ffn, name="gather_ffn1")
    later, _ = lax.optimization_barrier(({n: wts[n] for n in BIG_WEIGHTS if not n.startswith("ffn1")}, tok))
    packs_mix = [pack(later, ["w_in"], ["mla_w_o", "hgrn_w_o", "w_out"]), pack(later, ["mla_w_q_up", "mla_w_kv_up"])]
    packs_ffn2 = [pack(later, ["ffn2_w_gate", "ffn2_w_up"], ["ffn2_w_down"])]
    lands_mix = [jnp.zeros((2, 10 * d, d // 2), BF16), jnp.zeros((2, lay.rows_narrow, q_lora // 2), BF16)]
    arrived, tok = _gather_wait(got1, packs_mix + packs_ffn2 + lands_mix, name="gather_ffn1_wait")
    got_m, tok = _gather_start(packs_mix, lands_mix, jobs_mix, name="gather_mix", deps=[tok])
    (w_ffn1,) = _gather_forward(arrived, jobs_ffn, name="gather_ffn1_forward", deps=[tok])
    col_kr = q_lora + kv_lora
    hgrn_cols = [d, 2 * d, 3 * d, 4 * d]
    col_ga, col_gb = 5 * d, 6 * d
    tabs = _rope_tables(positions[0])
    scale = (HEAD + ROPE) ** -0.5

    x1, saved1 = _ffn_fwd(xin, ffn1_norm_pre, ffn1_norm_post, w_ffn1, lay, "ffn1")

    arrived, tok = _gather_wait(got_m, [x1], name="gather_mix_wait")
    got2, tok = _gather_start(packs_ffn2, ffn_land(), jobs_ffn, name="gather_ffn2", deps=[tok])
    wide, narrow = _gather_forward(arrived, jobs_mix, name="gather_mix_forward", deps=[tok])
    w_in_v = (wide, 0, 7 * d)
    w_o_v = {n: (wide, (7 + i) * d, d) for i, n in enumerate(("mla_w_o", "hgrn_w_o", "w_out"))}
    w_q_v = (narrow, lay.off_q, nh_mla * QGROUP)
    w_kv_v = (narrow, lay.off_kv, nh_mla * QGROUP)

    h2 = _norm_fwd(x1, mix_norm_pre, name="mix_norm_pre", out_dtype=BF16)
    proj = _mm([(h2, w_in_v)], name="mix_in", mode="nt", deps=[tok])
    cqn = _norm_fwd(proj, mla_q_norm, name="mla_q_norm", out_dtype=BF16, col=0)
    ckvn = _norm_fwd(proj, mla_kv_norm, name="mla_kv_norm", out_dtype=BF16, col=q_lora)
    qp = _mm([(cqn, w_q_v)], name="mla_q_up", mode="nt")
    kvb = _mm([(ckvn, w_kv_v)], name="mla_kv_up", mode="nt", out_dtype=BF16)
    qcat = _rope(qp, tabs, name="rope_q", group=QGROUP, backward=False, out_dtype=BF16)
    krot = _rope(proj, tabs, name="rope_k", group=LANE, backward=False, out_dtype=BF16, col=col_kr, ngroup=1)
    o_mla = _attn_fwd(qcat, kvb, krot, name="mla_attention", scale=scale)
    y_a = _mm([(o_mla, w_o_v["mla_w_o"])], name="mla_out", mode="nn")

    o_raw, yb, states = _hgrn_fwd(proj, hgrn_cols, d, hgrn_lb_logits, hgrn_out_norm, name="hgrn_scan")
    y_b = _mm([(yb, w_o_v["hgrn_w_o"])], name="hgrn_out", mode="nn")

    merged = _merge_fwd(proj, col_ga, col_gb, y_a, y_b, name="mix_merge")
    y_mix = _mm([(merged, w_o_v["w_out"])], name="mix_out", mode="nn")
    x2 = _norm_fwd(y_mix, mix_norm_post, name="mix_norm_post", resid=x1, scale=1.0)

    (w_ffn2,) = _gather_forward(_gather_wait(got2, [x2], name="gather_ffn2_wait")[0], jobs_ffn, name="gather_ffn2_forward")
    x3, saved2 = _ffn_fwd(x2, ffn2_norm_pre, ffn2_norm_post, w_ffn2, lay, "ffn2")
    dx3, loss_local = _loss_head(x3, target, name="loss_head")

    grads, deltas, new_m, new_v = {}, {}, {}, {}
    sel = cc.astype(jnp.int32)
    sel1 = jnp.reshape(sel, (1,))
    me_chip = (2 * cx + cy).astype(jnp.int32)

    def reduce_mid(handle, after, jobs, tag):
        bufs, recvd = _swap_wait(handle, after, name=f"grad_swap_{tag}_wait")
        sums = [_add_sibling(b, r, sel1, name=f"grad_add_sibling_{tag}_{i}") for i, (b, r) in enumerate(zip(bufs, recvd))]
        return _exchange_start(sums, jobs, name=f"grad_exchange_{tag}")

    def reduce_end(handle, after, tag):
        sums, lands = _exchange_wait(handle, after, name=f"grad_exchange_{tag}_wait")
        parts = [_add_shard(sums[job.a], lands[job.a], job, sel, me_chip, name=f"grad_add_chips_{tag}_{i}")
                 for i, job in enumerate(handle["jobs"])]
        return _join_list(parts, name=f"grad_join_{tag}")

    def natural(part, lo, rows, transposed):
        g_n = part[:, lo:lo + rows]
        hw_n = g_n.shape[2]
        return g_n.transpose(0, 2, 1).reshape(2 * hw_n, rows) if transposed else g_n.transpose(1, 0, 2).reshape(rows, 2 * hw_n)

    def adam(names, deps=()):
        for i, n in enumerate(names):
            shp = wts[n].shape
            two_d = (lambda a: a[0]) if n in BIG_WEIGHTS else (lambda a: a)
            dl, nm, nv = _adamw(two_d(wts[n]), grads[n], two_d(mom[n]), two_d(var[n]), name=f"adamw_{n}",
                                deps=deps if i == 0 else ())
            grads[n] = grads[n].reshape(shp)
            deltas[n], new_m[n], new_v[n] = dl.reshape(shp), nm.reshape(shp), nv.reshape(shp)
        return [deltas[n] for n in names]

    def ffn_grads(joined, tag, deps=()):
        nff = lay.nff
        grads[f"{tag}_w_gate"] = natural(joined[0], 0, nff, True)
        grads[f"{tag}_w_up"] = natural(joined[0], nff, nff, True)
        grads[f"{tag}_w_down"] = natural(joined[0], 2 * nff, nff, False)
        return adam([f"{tag}_w_gate", f"{tag}_w_up", f"{tag}_w_down"], deps)

    swaps = {}

    def start_swap(tag):
        def hook(gbuf):
            swaps[tag], started = _swap_start([gbuf], name=f"grad_swap_{tag}")
            return [started]
        return hook

    dx2, grads["ffn2_norm_pre"], grads["ffn2_norm_post"] = _ffn_bwd(
        dx3, saved2, ffn2_norm_pre, ffn2_norm_post, w_ffn2, lay, "ffn2", after_dw=start_swap("ffn2"))

    gwide = lax.empty((2, 10 * d, d // 2), F32)
    gnarrow = lax.empty((2, lay.rows_narrow, q_lora // 2), F32)
    dy_mix, grads["mix_norm_post"] = _norm_bwd(y_mix, mix_norm_post, dx2, name="mix_norm_post_bwd")
    dmerged = _mm([(dy_mix, w_o_v["w_out"])], name="mix_out_dx", mode="nt")
    gwide = _mm([(merged, dy_mix)], name="mix_out_dw", mode="tn", into=(gwide, 9 * d))
    dga, dgb, dy_a, dy_b = _merge_bwd(dmerged, proj, col_ga, col_gb, y_a, y_b, name="mix_merge_bwd")

    do_mla = _mm([(dy_a, w_o_v["mla_w_o"])], name="mla_out_dx", mode="nt")
    gwide = _mm([(o_mla, dy_a)], name="mla_out_dw", mode="tn", into=(gwide, 7 * d))
    dqcat, dkv, dkr = _attn_bwd(qcat, kvb, krot, do_mla, name="mla_attention_bwd", scale=scale)
    exch2, tok = reduce_mid(swaps["ffn2"], [dkr], _ffn_jobs(lay), "ffn2")

    dqp = _rope(dqcat, tabs, name="rope_q_bwd", group=QGROUP, backward=True, out_dtype=BF16)
    dk_r = _rope(dkr, tabs, name="rope_k_bwd", group=LANE, backward=True, out_dtype=BF16)
    dcqn = _mm([(dqp, w_q_v)], name="mla_q_up_dx", mode="nn", deps=[tok])
    gnarrow = _mm([(dqp, cqn)], name="mla_q_up_dw", mode="tn", into=(gnarrow, lay.off_q))
    dkvb = dkv.astype(BF16)
    dckvn = _mm([(dkvb, w_kv_v)], name="mla_kv_up_dx", mode="nn")
    gnarrow = _mm([(dkvb, ckvn)], name="mla_kv_up_dw", mode="tn", into=(gnarrow, lay.off_kv))
    dc_q, grads["mla_q_norm"] = _norm_bwd(proj, mla_q_norm, dcqn, name="mla_q_norm_bwd", col=0, dx_dtype=BF16)
    dc_kv, grads["mla_kv_norm"] = _norm_bwd(proj, mla_kv_norm, dckvn, name="mla_kv_norm_bwd", col=q_lora, dx_dtype=BF16)

    dyb = _mm([(dy_b, w_o_v["hgrn_w_o"])], name="hgrn_out_dx", mode="nt")
    gwide = _mm([(yb, dy_b)], name="hgrn_out_dw", mode="tn", into=(gwide, 8 * d))
    dhq, dhf, dhi, dhg, dlb_h, dnorm_h = _hgrn_bwd(proj, hgrn_cols, d, o_raw, dyb, states, hgrn_lb_logits, hgrn_out_norm,
                                                   name="hgrn_scan_bwd")

    dhead = jnp.concatenate([dc_q, dc_kv, dk_r, jnp.zeros((t, d - col_kr - LANE), BF16)], axis=1)
    dparts = [dhead, dhq, dhf, dhi, dhg, dga, dgb]
    dh2 = _mm([(p, (wide, i * d, d)) for i, p in enumerate(dparts)], name="mix_in_dx", mode="nn")
    for i, p in enumerate(dparts):
        gwide = _mm([(p, h2)], name=f"mix_in_dw_{i}", mode="tn", into=(gwide, i * d))
    dx1, grads["mix_norm_pre"] = _norm_bwd(x1, mix_norm_pre, dh2, name="mix_norm_pre_bwd", dres=dx2)
    swap_m, tok = _swap_start([gwide, gnarrow], name="grad_swap_mix")
    joined2 = reduce_end(exch2, [dx1], "ffn2")

    exchanges = {}

    def mix_exchange(after):
        exchanges["mix"], started = reduce_mid(swap_m, [after], _mix_jobs(lay), "mix")
        return [started]

    dx0, grads["ffn1_norm_pre"], grads["ffn1_norm_post"] = _ffn_bwd(
        dx1, saved1, ffn1_norm_pre, ffn1_norm_post, w_ffn1, lay, "ffn1", deps=[tok], after_act=mix_exchange,
        after_dw=start_swap("ffn1"))
    exch1, tok = reduce_mid(swaps["ffn1"], [dx0], _ffn_jobs(lay), "ffn1")

    joined_m = reduce_end(exchanges["mix"], [dx0, tok], "mix")
    done = ffn_grads(joined2, "ffn2")
    grads["w_in"] = natural(jnp.concatenate([joined_m[0], joined_m[1]], axis=1), 0, lay.ncol, True)
    for i, n in enumerate(("mla_w_o", "hgrn_w_o", "w_out")):
        grads[n] = natural(joined_m[2], i * lay.r_o, lay.r_o, False)
    grads["mla_w_q_up"] = natural(joined_m[3], 0, lay.hps * (HEAD + ROPE), True)
    grads["mla_w_kv_up"] = natural(joined_m[4], 0, lay.hps * QGROUP, True)
    done += adam(["w_in", "mla_w_q_up", "mla_w_kv_up", "mla_w_o", "hgrn_w_o", "w_out"])

    joined1 = reduce_end(exch1, done, "ffn1")

    dlb = dlb_h.reshape(1, -1)
    dnorm = jnp.sum(dnorm_h, axis=0)
    small = {**{n: grads[n] for n in SMALL_WEIGHTS if n not in ("hgrn_lb_logits", "hgrn_out_norm")},
             "hgrn_lb_logits": dlb, "hgrn_out_norm": dnorm}
    vec, _ = lax.optimization_barrier((jnp.concatenate([small[n] for n in SMALL_WEIGHTS], axis=1), joined1[0]))
    vec = _all_reduce_small(vec, name="grad_all_reduce_small")
    off = 0
    for n in SMALL_WEIGHTS:
        w_n = small[n].shape[1]
        grads[n] = vec[:, off:off + w_n]
        off += w_n
    grads["hgrn_lb_logits"] = _lb_logits_grad(hgrn_lb_logits, grads["hgrn_lb_logits"], name="lb_logits_grad")

    adam(list(SMALL_WEIGHTS))
    ffn_grads(joined1, "ffn1")

    loss = lax.psum(loss_local, ("x", "y", "c"))
    dx_out = dx0.reshape(x.shape)
    return (loss, dx_out, *[grads[n] for n in ALL_WEIGHTS], *[deltas[n] for n in ALL_WEIGHTS],
            *[new_m[n] for n in ALL_WEIGHTS], *[new_v[n] for n in ALL_WEIGHTS])
```

```python
import functools

import jax
import jax.numpy as jnp
from jax import lax
from jax.experimental import pallas as pl
from jax.experimental.pallas import tpu as pltpu

F32 = jnp.float32
BF16 = jnp.bfloat16
MESH = pl.DeviceIdType.MESH

NORM_EPS = 1e-6
MACARON_SCALE = 0.5
ROPE_THETA = 10000.0
HEAD = 128
ROPE = 64
QGROUP = 2 * HEAD
SUB = 16
ADAM_LR, ADAM_B1, ADAM_B2, ADAM_EPS, ADAM_WD, ADAM_STEP = 0.001, 0.9, 0.999, 1e-08, 0.01, 10

LANE = 128
VMEM_LIMIT = 48 * 1024 * 1024
MM_TILE = 1024
MM_TILE_WIDE = 1536

BIG_WEIGHTS = ("ffn1_w_gate", "ffn1_w_up", "ffn1_w_down", "w_in", "mla_w_q_up", "mla_w_kv_up",
               "mla_w_o", "hgrn_w_o", "w_out", "ffn2_w_gate", "ffn2_w_up", "ffn2_w_down")
COL_SHARDED = ("ffn1_w_gate", "ffn1_w_up", "w_in", "mla_w_q_up", "mla_w_kv_up", "ffn2_w_gate", "ffn2_w_up")
SMALL_WEIGHTS = ("ffn1_norm_pre", "ffn1_norm_post", "mix_norm_pre", "mla_q_norm", "mla_kv_norm",
                 "hgrn_lb_logits", "hgrn_out_norm", "mix_norm_post", "ffn2_norm_pre", "ffn2_norm_post")
ALL_WEIGHTS = ("ffn1_norm_pre", "ffn1_w_gate", "ffn1_w_up", "ffn1_w_down", "ffn1_norm_post", "mix_norm_pre",
               "w_in", "mla_q_norm", "mla_w_q_up", "mla_kv_norm", "mla_w_kv_up", "mla_w_o", "hgrn_lb_logits",
               "hgrn_out_norm", "hgrn_w_o", "w_out", "mix_norm_post", "ffn2_norm_pre", "ffn2_w_gate",
               "ffn2_w_up", "ffn2_w_down", "ffn2_norm_post")


def _params(*sem):
    return pltpu.CompilerParams(dimension_semantics=sem or None, vmem_limit_bytes=VMEM_LIMIT)


def _pick(n, cap, offset=0):
    if n <= cap and offset % n == 0:
        return n
    best = None
    for t in range(LANE, min(n, cap) + 1, LANE):
        if n % t == 0 and offset % t == 0:
            best = t
    assert best is not None, (n, cap, offset)
    return best


def _row_tile(n, row_bytes, budget=1 << 20):
    best = None
    for t in range(8, n + 1, 8):
        if n % t == 0 and t * row_bytes <= budget:
            best = t
    return n if best is None else best


def _sigmoid(x):
    return 1.0 / (1.0 + jnp.exp(-x))


def _silu(x):
    return x * _sigmoid(x)


def _dsilu(x):
    s = _sigmoid(x)
    return s * (1.0 + x * (1.0 - s))


def _mm(pairs, *, name, mode="nn", out_dtype=F32, into=None, deps=(), extras=(), epilogue=None, out_dtypes=None, tm_cap=None):
    halves = isinstance(pairs[0][1], tuple)
    assert halves or mode == "tn"
    pairs = [(a, b if halves else (b, 0, b.shape[0])) for a, b in pairs]
    a0, (b0, b_off, b_rows) = pairs[0]
    hw = b0.shape[2] if halves else (into[0].shape[2] if into is not None else None)
    if mode == "nn":
        (m, kdim), n = a0.shape, 2 * hw
    elif mode == "nt":
        (m, kdim), n = a0.shape, b_rows
        assert kdim == 2 * hw
    else:
        (kdim, m), n = a0.shape, b0.shape[1]
    out_off = 0 if into is None else into[1]
    tm = _pick(m, tm_cap or (MM_TILE_WIDE if mode == "tn" else MM_TILE), out_off)
    tn = hw if (mode == "nn" or into is not None) else _pick(n, MM_TILE_WIDE, b_off if mode == "nt" else 0)
    tk = hw if mode == "nt" else _pick(kdim, MM_TILE if len(pairs) <= 2 else MM_TILE // 2, b_off if mode == "nn" else 0)
    assert n % tn == 0 and kdim % tk == 0
    nk = kdim // tk
    npair = len(pairs)
    dims = {"nn": (((1,), (0,)), ((), ())), "nt": (((1,), (1,)), ((), ())), "tn": (((0,), (0,)), ((), ()))}[mode]

    nout = 1 if epilogue is None else len(out_dtypes)

    def body(*refs):
        ins, x_refs = refs[:2 * npair], refs[2 * npair:2 * npair + len(extras)]
        o_refs, acc_ref = refs[-1 - nout:-1], refs[-1]
        k = pl.program_id(2)

        @pl.when(k == 0)
        def _():
            acc_ref[...] = jnp.zeros_like(acc_ref)

        for p in range(npair):
            a = ins[2 * p][...].astype(BF16)
            b = ins[2 * p + 1][...].astype(BF16)
            acc_ref[...] += lax.dot_general(a, b, dims, preferred_element_type=F32)

        @pl.when(k == nk - 1)
        def _():
            outs = (acc_ref[...],) if epilogue is None else epilogue(acc_ref[...], *[x[...] for x in x_refs])
            for o_ref, o in zip(o_refs, outs):
                o_ref[...] = o.astype(o_ref.dtype)

    a_spec = pl.BlockSpec((tk, tm), lambda i, j, k: (k, i)) if mode == "tn" else pl.BlockSpec((tm, tk), lambda i, j, k: (i, k))
    in_specs, flat = [], []
    for a, (b, off, _) in pairs:
        if mode == "nt":
            b_spec = pl.BlockSpec((None, tn, tk), lambda i, j, k, o=off // tn: (k, j + o, 0))
        elif mode == "nn":
            b_spec = pl.BlockSpec((None, tk, tn), lambda i, j, k, o=off // tk: (j, k + o, 0))
        else:
            b_spec = pl.BlockSpec((tk, tn), lambda i, j, k: (k, j))
        in_specs += [a_spec, b_spec]
        flat += [a, b]
    for extra in extras:
        in_specs.append(pl.BlockSpec((tm, tn), lambda i, j, k: (i, j)))
        flat.append(extra)
    for dep in deps:
        in_specs.append(pl.BlockSpec(memory_space=pl.ANY))
        flat.append(dep)
    if epilogue is not None:
        assert into is None
        out_shape, aliases = [jax.ShapeDtypeStruct((m, n), dt) for dt in out_dtypes], {}
        out_spec = [pl.BlockSpec((tm, tn), lambda i, j, k: (i, j))] * nout
    elif into is None:
        out_shape, aliases = jax.ShapeDtypeStruct((m, n), out_dtype), {}
        out_spec = pl.BlockSpec((tm, tn), lambda i, j, k: (i, j))
    else:
        out_shape, aliases = jax.ShapeDtypeStruct(into[0].shape, into[0].dtype), {len(flat): 0}
        out_spec = pl.BlockSpec((None, tm, tn), lambda i, j, k, o=out_off // tm: (j, i + o, 0))
        in_specs.append(pl.BlockSpec(memory_space=pl.ANY))
        flat.append(into[0])
    return pl.pallas_call(
        body, name=name, grid=(m // tm, n // tn, nk),
        in_specs=in_specs,
        out_specs=out_spec,
        out_shape=out_shape, input_output_aliases=aliases,
        scratch_shapes=[pltpu.VMEM((tm, tn), F32)],
        compiler_params=_params("parallel", "parallel", "arbitrary"),
    )(*flat)


def _norm_fwd(y, w, *, name, resid=None, scale=1.0, out_dtype=F32, col=0):
    t, d = y.shape[0], w.shape[1]
    tr = _pick(t, 256)
    assert col % d == 0

    def body(*refs):
        if resid is None:
            y_ref, w_ref, o_ref = refs
        else:
            y_ref, w_ref, r_ref, o_ref = refs
        yv = y_ref[...]
        out = yv * lax.rsqrt(jnp.mean(yv * yv, axis=-1, keepdims=True) + NORM_EPS) * w_ref[...]
        if resid is not None:
            out = r_ref[...] + scale * out
        o_ref[...] = out.astype(out_dtype)

    row = pl.BlockSpec((tr, d), lambda i: (i, 0))
    wspec = pl.BlockSpec((1, d), lambda i: (0, 0))
    ins, specs = [y, w], [pl.BlockSpec((tr, d), lambda i: (i, col // d)), wspec]
    if resid is not None:
        ins.append(resid)
        specs.append(row)
    return pl.pallas_call(
        body, name=name, grid=(t // tr,), in_specs=specs, out_specs=row,
        out_shape=jax.ShapeDtypeStruct((t, d), out_dtype), compiler_params=_params("parallel"),
    )(*ins)


def _norm_bwd(x, w, dy, *, name, scale=1.0, dres=None, col=0, dx_dtype=F32):
    t, d = x.shape[0], w.shape[1]
    tr = _pick(t, 256)
    assert col % d == 0

    def body(*refs):
        if dres is None:
            x_ref, w_ref, dy_ref, dx_ref, dw_ref = refs
        else:
            x_ref, w_ref, dy_ref, dr_ref, dx_ref, dw_ref = refs

        @pl.when(pl.program_id(0) == 0)
        def _():
            dw_ref[...] = jnp.zeros_like(dw_ref)

        xv = x_ref[...]
        r = lax.rsqrt(jnp.mean(xv * xv, axis=-1, keepdims=True) + NORM_EPS)
        xhat = xv * r
        dyv = dy_ref[...].astype(F32) * scale
        dw_ref[...] += jnp.sum(dyv * xhat, axis=0, keepdims=True)
        t_ = dyv * w_ref[...]
        dx = r * (t_ - xhat * jnp.mean(t_ * xhat, axis=-1, keepdims=True))
        if dres is not None:
            dx = dx + dr_ref[...]
        dx_ref[...] = dx.astype(dx_dtype)

    row = pl.BlockSpec((tr, d), lambda i: (i, 0))
    wspec = pl.BlockSpec((1, d), lambda i: (0, 0))
    ins, specs = [x, w, dy], [pl.BlockSpec((tr, d), lambda i: (i, col // d)), wspec, row]
    if dres is not None:
        ins.append(dres)
        specs.append(row)
    return pl.pallas_call(
        body, name=name, grid=(t // tr,), in_specs=specs, out_specs=(row, wspec),
        out_shape=(jax.ShapeDtypeStruct((t, d), dx_dtype), jax.ShapeDtypeStruct((1, d), F32)),
        compiler_params=_params("arbitrary"),
    )(*ins)


def _elementwise(fn, ins, out_dtypes, *, name, width=None, cols=None):
    t = ins[0].shape[0]
    d = ins[0].shape[1] if width is None else width
    cols = [0] * len(ins) if cols is None else cols
    tc = _pick(d, 2048)
    for c in cols:
        tc = _pick(d, tc, c)
    tr = _row_tile(t, tc * 4)
    nout = len(out_dtypes)

    def body(*refs):
        outs = fn(*[r[...].astype(F32) for r in refs[:len(ins)]])
        for o_ref, o in zip(refs[len(ins):], outs):
            o_ref[...] = o.astype(o_ref.dtype)

    spec = pl.BlockSpec((tr, tc), lambda i, j: (i, j))
    in_specs = [pl.BlockSpec((tr, tc), lambda i, j, o=c // tc: (i, j + o)) for c in cols]
    return pl.pallas_call(
        body, name=name, grid=(t // tr, d // tc), in_specs=in_specs, out_specs=[spec] * nout,
        out_shape=[jax.ShapeDtypeStruct((t, d), dt) for dt in out_dtypes],
        compiler_params=_params("parallel", "parallel"),
    )(*ins)


def _merge_fwd(proj, col_a, col_b, ya, yb, *, name):
    return _elementwise(lambda a, b, p, q: (_sigmoid(a) * p + _sigmoid(b) * q,), [proj, proj, ya, yb], [BF16],
                        name=name, width=ya.shape[1], cols=[col_a, col_b, 0, 0])[0]


def _merge_bwd(dm, proj, col_a, col_b, ya, yb, *, name):
    def fn(dmv, a, b, p, q):
        sa, sb = _sigmoid(a), _sigmoid(b)
        return dmv * p * sa * (1.0 - sa), dmv * q * sb * (1.0 - sb), dmv * sa, dmv * sb

    return _elementwise(fn, [dm, proj, proj, ya, yb], [BF16, BF16, BF16, BF16], name=name, width=ya.shape[1],
                        cols=[0, col_a, col_b, 0, 0])


def _loss_head(xo, target, *, name):
    t, d = xo.shape
    tr = _pick(t, 256)

    def body(x_ref, t_ref, dx_ref, l_ref):
        @pl.when(pl.program_id(0) == 0)
        def _():
            l_ref[...] = jnp.zeros_like(l_ref)

        err = x_ref[...] - t_ref[...]
        dx_ref[...] = err * (1.0 / d)
        l_ref[...] += 0.5 * jnp.sum(jnp.mean(err * err, axis=-1, keepdims=True), axis=0, keepdims=True)

    row = pl.BlockSpec((tr, d), lambda i: (i, 0))
    dx, l = pl.pallas_call(
        body, name=name, grid=(t // tr,), in_specs=[row, row],
        out_specs=(row, pl.BlockSpec((1, 1), lambda i: (0, 0))),
        out_shape=(jax.ShapeDtypeStruct((t, d), F32), jax.ShapeDtypeStruct((1, 1), F32)),
        compiler_params=_params("arbitrary"),
    )(xo, target)
    return dx, l[0, 0]


def _rope(xin, tabs, *, name, group, backward, out_dtype, col=0, ngroup=None):
    t = xin.shape[0]
    ngroup = xin.shape[1] // group if ngroup is None else ngroup
    wdt = ngroup * group
    tr = _pick(t, 256)
    assert col % wdt == 0
    cos_t, nsin_t, sin_t = tabs

    def body(x_ref, c_ref, n_ref, s_ref, o_ref):
        cv, nv, sv = c_ref[...], n_ref[...], s_ref[...]
        for g in range(ngroup):
            lo, hi = g * group, (g + 1) * group
            rot = x_ref[:, hi - LANE:hi].astype(F32)
            if backward:
                out = rot * cv + pltpu.roll(rot * nv, 32, 1) + pltpu.roll(rot * sv, LANE - 32, 1)
            else:
                out = rot * cv + pltpu.roll(rot, LANE - 32, 1) * nv + pltpu.roll(rot, 32, 1) * sv
            if group > LANE:
                o_ref[:, lo:hi - LANE] = x_ref[:, lo:hi - LANE].astype(out_dtype)
            o_ref[:, hi - LANE:hi] = out.astype(out_dtype)

    xspec = pl.BlockSpec((tr, wdt), lambda i: (i, 0))
    tspec = pl.BlockSpec((tr, LANE), lambda i: (i, 0))
    return pl.pallas_call(
        body, name=name, grid=(t // tr,),
        in_specs=[pl.BlockSpec((tr, wdt), lambda i: (i, col // wdt)), tspec, tspec, tspec], out_specs=xspec,
        out_shape=jax.ShapeDtypeStruct((t, wdt), out_dtype), compiler_params=_params("parallel"),
    )(xin, cos_t, nsin_t, sin_t)


def _scores(q, kv, kr, qi, tq, scale):
    kcat = jnp.concatenate([kv[:, :HEAD], kr], axis=1)
    s = lax.dot_general(q, kcat, (((1,), (1,)), ((), ())), preferred_element_type=F32) * scale
    row = qi * tq + lax.broadcasted_iota(jnp.int32, s.shape, 0)
    col = lax.broadcasted_iota(jnp.int32, s.shape, 1)
    s = jnp.where(col <= row, s, -jnp.inf)
    p = jnp.exp(s - jnp.max(s, axis=-1, keepdims=True))
    return p / jnp.sum(p, axis=-1, keepdims=True), kcat


def _attn_fwd(qcat, kv, kr, *, name, scale):
    t = qcat.shape[0]
    nh = qcat.shape[1] // QGROUP
    tq = _pick(t, 256)

    def body(q_ref, kv_ref, kr_ref, o_ref):
        for qi in range(t // tq):
            @pl.when(pl.program_id(1) == qi)
            def _(qi=qi):
                kvv = kv_ref[0:(qi + 1) * tq, :]
                p, _ = _scores(q_ref[...], kvv, kr_ref[0:(qi + 1) * tq, :], qi, tq, scale)
                o_ref[...] = jnp.dot(p.astype(BF16), kvv[:, HEAD:], preferred_element_type=F32).astype(BF16)

    return pl.pallas_call(
        body, name=name, grid=(nh, t // tq),
        in_specs=[pl.BlockSpec((tq, QGROUP), lambda h, i: (i, h)), pl.BlockSpec((t, QGROUP), lambda h, i: (0, h)),
                  pl.BlockSpec((t, LANE), lambda h, i: (0, 0))],
        out_specs=pl.BlockSpec((tq, HEAD), lambda h, i: (i, h)),
        out_shape=jax.ShapeDtypeStruct((t, nh * HEAD), BF16), compiler_params=_params("parallel", "parallel"),
    )(qcat, kv, kr)


def _attn_bwd(qcat, kv, kr, do, *, name, scale):
    t = qcat.shape[0]
    nh = qcat.shape[1] // QGROUP
    tq = _pick(t, 256)
    nq = t // tq

    def body(q_ref, kv_ref, kr_ref, do_ref, dq_ref, dkv_ref, dkr_ref, dk_acc, dv_acc):
        h, i = pl.program_id(0), pl.program_id(1)

        @pl.when(i == 0)
        def _():
            dk_acc[...] = jnp.zeros_like(dk_acc)
            dv_acc[...] = jnp.zeros_like(dv_acc)

        @pl.when((i == 0) & (h == 0))
        def _():
            dkr_ref[...] = jnp.zeros_like(dkr_ref)

        for qi in range(nq):
            @pl.when(i == qi)
            def _(qi=qi):
                keys = slice(0, (qi + 1) * tq)
                q = q_ref[...]
                kvv = kv_ref[keys, :]
                dov = do_ref[...].astype(BF16)
                p, kcat = _scores(q, kvv, kr_ref[keys, :], qi, tq, scale)
                dp = lax.dot_general(dov, kvv[:, HEAD:], (((1,), (1,)), ((), ())), preferred_element_type=F32)
                ds = (p * (dp - jnp.sum(p * dp, axis=-1, keepdims=True)) * scale).astype(BF16)
                dq_ref[...] = jnp.dot(ds, kcat, preferred_element_type=F32)
                dk_acc[keys, :] += lax.dot_general(ds, q, (((0,), (0,)), ((), ())), preferred_element_type=F32)
                dv_acc[keys, :] += lax.dot_general(p.astype(BF16), dov, (((0,), (0,)), ((), ())), preferred_element_type=F32)

        @pl.when(i == nq - 1)
        def _():
            dk = dk_acc[...]
            dkv_ref[...] = jnp.concatenate([dk[:, :HEAD], dv_acc[...]], axis=1)
            dkr_ref[...] += dk[:, HEAD:]

    return pl.pallas_call(
        body, name=name, grid=(nh, nq),
        in_specs=[pl.BlockSpec((tq, QGROUP), lambda h, i: (i, h)), pl.BlockSpec((t, QGROUP), lambda h, i: (0, h)),
                  pl.BlockSpec((t, LANE), lambda h, i: (0, 0)), pl.BlockSpec((tq, HEAD), lambda h, i: (i, h))],
        out_specs=(pl.BlockSpec((tq, QGROUP), lambda h, i: (i, h)), pl.BlockSpec((t, QGROUP), lambda h, i: (0, h)),
                   pl.BlockSpec((t, LANE), lambda h, i: (0, 0))),
        out_shape=(jax.ShapeDtypeStruct((t, nh * QGROUP), F32), jax.ShapeDtypeStruct((t, nh * QGROUP), F32),
                   jax.ShapeDtypeStruct((t, LANE), F32)),
        scratch_shapes=[pltpu.VMEM((t, QGROUP), F32), pltpu.VMEM((t, HEAD), F32)],
        compiler_params=_params("arbitrary", "arbitrary"),
    )(qcat, kv, kr, do)


def _split3(x):
    hi = x.astype(BF16)
    r1 = x - hi.astype(F32)
    mid = r1.astype(BF16)
    lo = (r1 - mid.astype(F32)).astype(BF16)
    return hi, mid, lo


def _tri_matmul(mask, x):
    m = mask.astype(BF16)
    return sum(jnp.dot(m, part, preferred_element_type=F32) for part in _split3(x))


def _sub_cumsum(g, tb):
    row = lax.broadcasted_iota(jnp.int32, (tb, tb), 0)
    col = lax.broadcasted_iota(jnp.int32, (tb, tb), 1)
    return _tri_matmul(jnp.where((col <= row) & (col // SUB == row // SUB), 1.0, 0.0), g)


def _sub_suffix_prefix(after, before, tb):
    row = lax.broadcasted_iota(jnp.int32, (tb, tb), 0)
    col = lax.broadcasted_iota(jnp.int32, (tb, tb), 1)
    same = col // SUB == row // SUB
    return (_tri_matmul(jnp.where((col >= row) & same, 1.0, 0.0), after)
            + _tri_matmul(jnp.where((col < row) & same, 1.0, 0.0), before))


def _lower_bound(logits):
    mx = jnp.max(logits, axis=0, keepdims=True)
    e = jnp.exp(logits - mx)
    return e[0:1, :] / jnp.sum(e, axis=0, keepdims=True)


def _hgrn_fwd(proj, cols, wdt, logits, out_norm, *, name):
    t = proj.shape[0]
    nh = wdt // HEAD
    tb = _pick(t, 128)
    ns = tb // SUB

    def body(hq_ref, hf_ref, hi_ref, hg_ref, lg_ref, w_ref, o_ref, yb_ref, st_ref, s_ref, q_s, k_s, b_s):
        @pl.when(pl.program_id(1) == 0)
        def _():
            s_ref[...] = jnp.zeros_like(s_ref)

        lb = _lower_bound(lg_ref[...])
        f = lb + (1.0 - lb) * _sigmoid(hf_ref[...])
        q_s[...] = _silu(hq_ref[...])
        k_s[...] = 1.0 - f
        b_s[...] = _sub_cumsum(jnp.log(f), tb)
        rowid = lax.broadcasted_iota(jnp.int32, (SUB, HEAD), 0)

        def sub(c, st):
            rows = pl.ds(pl.multiple_of(c * SUB, SUB), SUB)
            qc, kc, bc, vc = q_s[rows, :], k_s[rows, :], b_s[rows, :], hi_ref[rows, :]
            st_ref[0, c] = st
            bl = bc[SUB - 1:SUB, :]
            oc = lax.dot_general((qc * jnp.exp(bc)).astype(BF16), st.astype(BF16), (((1,), (1,)), ((), ())),
                                 preferred_element_type=F32)
            for s in range(SUB):
                e = jnp.where(rowid >= s, jnp.exp(bc - bc[s:s + 1, :]), 0.0)
                a = jnp.sum(qc * e * kc[s:s + 1, :], axis=1, keepdims=True)
                oc = oc + a * vc[s:s + 1, :]
            o_ref[rows, :] = oc
            kd = kc * jnp.exp(bl - bc)
            return jnp.exp(bl) * st + lax.dot_general(vc.astype(BF16), kd.astype(BF16), (((0,), (0,)), ((), ())),
                                                      preferred_element_type=F32)

        s_ref[...] = lax.fori_loop(0, ns, sub, s_ref[...], unroll=True)
        o = o_ref[...]
        r = lax.rsqrt(jnp.mean(o * o, axis=-1, keepdims=True) + NORM_EPS)
        yb_ref[...] = (o * r * w_ref[...] * _silu(hg_ref[...])).astype(BF16)

    blk = pl.BlockSpec((tb, HEAD), lambda h, j: (j, h))
    return pl.pallas_call(
        body, name=name, grid=(nh, t // tb),
        in_specs=[pl.BlockSpec((tb, HEAD), lambda h, j, o=c // HEAD: (j, h + o)) for c in cols]
        + [pl.BlockSpec((2, HEAD), lambda h, j: (0, h)), pl.BlockSpec((1, HEAD), lambda h, j: (0, 0))],
        out_specs=(blk, blk, pl.BlockSpec((1, ns, HEAD, HEAD), lambda h, j: (h, j, 0, 0))),
        out_shape=(jax.ShapeDtypeStruct((t, wdt), F32), jax.ShapeDtypeStruct((t, wdt), BF16),
                   jax.ShapeDtypeStruct((nh, t // SUB, HEAD, HEAD), F32)),
        scratch_shapes=[pltpu.VMEM((HEAD, HEAD), F32)] + [pltpu.VMEM((tb, HEAD), F32)] * 3,
        compiler_params=_params("parallel", "arbitrary"),
    )(proj, proj, proj, proj, logits, out_norm)


def _hgrn_bwd(proj, cols, wdt, o_raw, dyb, states, logits, out_norm, *, name):
    t = proj.shape[0]
    nh = wdt // HEAD
    tb = _pick(t, 128)
    ns = tb // SUB
    nb = t // tb

    def body(hq_ref, hf_ref, hi_ref, hg_ref, o_ref, dy_ref, st_ref, lg_ref, w_ref,
             dhq_ref, dhf_ref, dhi_ref, dhg_ref, dlb_ref, dw_ref,
             ds_ref, q_s, k_s, b_s, do_s, dq_s, dk_s, dv_s, after_s, before_s, thru_s):
        @pl.when(pl.program_id(1) == 0)
        def _():
            ds_ref[...] = jnp.zeros_like(ds_ref)
            dlb_ref[...] = jnp.zeros_like(dlb_ref)
            dw_ref[...] = jnp.zeros_like(dw_ref)

        lb = _lower_bound(lg_ref[...])
        hqv, hgv = hq_ref[...], hg_ref[...]
        sig = _sigmoid(hf_ref[...])
        f = lb + (1.0 - lb) * sig
        q_s[...] = _silu(hqv)
        k_s[...] = 1.0 - f
        b_s[...] = _sub_cumsum(jnp.log(f), tb)

        o = o_ref[...]
        r = lax.rsqrt(jnp.mean(o * o, axis=-1, keepdims=True) + NORM_EPS)
        nrm = o * r
        w = w_ref[...]
        dy = dy_ref[...].astype(F32)
        dhg_ref[...] = (dy * nrm * w * _dsilu(hgv)).astype(BF16)
        dnw = dy * _silu(hgv)
        dw_ref[0] += jnp.sum(dnw * nrm, axis=0, keepdims=True)
        tt = dnw * w
        do_s[...] = r * (tt - nrm * jnp.mean(tt * nrm, axis=-1, keepdims=True))
        rowid = lax.broadcasted_iota(jnp.int32, (SUB, HEAD), 0)

        def sub(cc, dst):
            c = ns - 1 - cc
            rows = pl.ds(pl.multiple_of(c * SUB, SUB), SUB)
            qc, kc, bc, vc, doc = q_s[rows, :], k_s[rows, :], b_s[rows, :], hi_ref[rows, :], do_s[rows, :]
            st = st_ref[0, c]
            bl = bc[SUB - 1:SUB, :]
            eb = jnp.exp(bc)
            ekd = jnp.exp(bl - bc)
            qe, kd = qc * eb, kc * ekd
            dob, vcb = doc.astype(BF16), vc.astype(BF16)
            dq_st = jnp.dot(dob, st.astype(BF16), preferred_element_type=F32) * eb
            dk_st = jnp.dot(vcb, dst.astype(BF16), preferred_element_type=F32) * ekd
            dv = lax.dot_general(kd.astype(BF16), dst.astype(BF16), (((1,), (1,)), ((), ())), preferred_element_type=F32)
            dq_in = jnp.zeros_like(qc)
            dk_in = jnp.zeros_like(qc)
            for s in range(SUB):
                e = jnp.where(rowid >= s, jnp.exp(bc - bc[s:s + 1, :]), 0.0)
                ek = e * kc[s:s + 1, :]
                a = jnp.sum(qc * ek, axis=1, keepdims=True)
                da = jnp.sum(doc * vc[s:s + 1, :], axis=1, keepdims=True)
                dq_in = dq_in + da * ek
                dk_in = dk_in + jnp.where(rowid == s, jnp.sum(da * e * qc, axis=0, keepdims=True), 0.0)
                dv = dv + jnp.where(rowid == s, jnp.sum(a * doc, axis=0, keepdims=True), 0.0)
            ebl = jnp.exp(bl)
            dq_s[rows, :] = dq_st + dq_in
            dk_s[rows, :] = dk_st + dk_in
            dv_s[rows, :] = dv
            after_s[rows, :] = qc * (dq_st + dq_in) - kc * dk_in
            before_s[rows, :] = kc * dk_st
            thru_s[rows, :] = jnp.broadcast_to(ebl * jnp.sum(st * dst, axis=0, keepdims=True), (SUB, HEAD))
            return ebl * dst + lax.dot_general(dob, qe.astype(BF16), (((0,), (0,)), ((), ())), preferred_element_type=F32)

        ds_ref[...] = lax.fori_loop(0, ns, sub, ds_ref[...], unroll=True)
        dg = _sub_suffix_prefix(after_s[...], before_s[...], tb) + thru_s[...]
        dhq_ref[...] = (dq_s[...] * _dsilu(hqv)).astype(BF16)
        dft = dg / f - dk_s[...]
        dhf_ref[...] = (dft * (1.0 - lb) * sig * (1.0 - sig)).astype(BF16)
        dlb_ref[0] += jnp.sum(dft * (1.0 - sig), axis=0, keepdims=True)
        dhi_ref[...] = dv_s[...].astype(BF16)

    blk = pl.BlockSpec((tb, HEAD), lambda h, j: (nb - 1 - j, h))
    vec = pl.BlockSpec((1, 1, HEAD), lambda h, j: (h, 0, 0))
    tok = jax.ShapeDtypeStruct((t, wdt), BF16)
    per_head = jax.ShapeDtypeStruct((nh, 1, HEAD), F32)
    return pl.pallas_call(
        body, name=name, grid=(nh, nb),
        in_specs=[pl.BlockSpec((tb, HEAD), lambda h, j, o=c // HEAD: (nb - 1 - j, h + o)) for c in cols]
        + [blk, blk] + [pl.BlockSpec((1, ns, HEAD, HEAD), lambda h, j: (h, nb - 1 - j, 0, 0)),
                              pl.BlockSpec((2, HEAD), lambda h, j: (0, h)), pl.BlockSpec((1, HEAD), lambda h, j: (0, 0))],
        out_specs=(blk, blk, blk, blk, vec, vec),
        out_shape=(tok, tok, tok, tok, per_head, per_head),
        scratch_shapes=[pltpu.VMEM((HEAD, HEAD), F32)] + [pltpu.VMEM((tb, HEAD), F32)] * 10,
        compiler_params=_params("arbitrary", "arbitrary"),
    )(proj, proj, proj, proj, o_raw, dyb, states, logits, out_norm)


def _lb_logits_grad(logits, dlb, *, name):
    def body(lg_ref, d_ref, o_ref):
        lg = lg_ref[...]
        e = jnp.exp(lg - jnp.max(lg, axis=0, keepdims=True))
        p = e / jnp.sum(e, axis=0, keepdims=True)
        d = d_ref[...]
        rowid = lax.broadcasted_iota(jnp.int32, lg.shape, 0)
        dp = jnp.where(rowid == 0, d, 0.0)
        o_ref[...] = p * (dp - jnp.sum(p * dp, axis=0, keepdims=True))

    return pl.pallas_call(body, name=name, out_shape=jax.ShapeDtypeStruct(logits.shape, F32))(logits, dlb)


def _adamw(w, g, m, v, *, name, deps=()):
    r, c = w.shape
    tc = _pick(c, 2048) if c % LANE == 0 else c
    tr = _row_tile(r, tc * 4)

    def body(w_ref, g_ref, m_ref, v_ref, *rest):
        d_ref, nm_ref, nv_ref = rest[-3:]
        gv = g_ref[...]
        nm = ADAM_B1 * m_ref[...] + (1.0 - ADAM_B1) * gv
        nv = ADAM_B2 * v_ref[...] + (1.0 - ADAM_B2) * (gv * gv)
        m_hat = nm / (1.0 - ADAM_B1 ** ADAM_STEP)
        v_hat = nv / (1.0 - ADAM_B2 ** ADAM_STEP)
        d_ref[...] = -ADAM_LR * (m_hat / (jnp.sqrt(v_hat) + ADAM_EPS) + ADAM_WD * w_ref[...])
        nm_ref[...] = nm
        nv_ref[...] = nv

    spec = pl.BlockSpec((tr, tc), lambda i, j: (i, j))
    shp = jax.ShapeDtypeStruct((r, c), F32)
    return pl.pallas_call(
        body, name=name, grid=(r // tr, c // tc), in_specs=[spec] * 4 + [ANY] * len(deps), out_specs=[spec] * 3,
        out_shape=[shp, shp, shp], compiler_params=_params("parallel", "parallel"),
    )(w, g, m, v, *deps)


def _coords():
    return lax.axis_index("x"), lax.axis_index("y"), lax.axis_index("c")


def _other_chips(x, y):
    return [(1 - x, y), (x, 1 - y), (1 - x, 1 - y)]


ANY = pl.BlockSpec(memory_space=pl.ANY)


class _Layout:
    def __init__(self, d, dff, in_cols, q_lora, kv_lora, nh):
        assert q_lora == kv_lora and nh % 4 == 0 and dff % (4 * LANE) == 0 and in_cols % 4 == 0 and d % 4 == 0
        self.d, self.dff, self.q_lora, self.nh = d, dff, q_lora, nh
        self.head = q_lora + kv_lora + ROPE
        self.pad = d - self.head
        self.nff, self.ncol, self.r_o, self.hps = dff // 4, in_cols // 4, d // 4, nh // 4
        assert self.head <= self.ncol
        self.off_q, self.off_kv, self.rows_narrow = 0, nh * QGROUP, 2 * nh * QGROUP


HBM = pl.BlockSpec(memory_space=pltpu.HBM)
SEMS = pl.BlockSpec(memory_space=pltpu.SEMAPHORE)
SPLIT = dict(has_side_effects=pltpu.SideEffectType.DATAFLOW_SIDE_EFFECTING)


def _in_hbm(a):
    return pltpu.with_memory_space_constraint(a, pltpu.HBM)


def _shard_rows(jobs, k):
    out, lrow = [], [0] * (1 + max(job.a for job in jobs))
    for job in jobs:
        for row, rows in job.pieces(k):
            out.append((job.a, lrow[job.a], row, rows))
            lrow[job.a] += rows
    return out


def _shard_total(jobs, a):
    return sum(rows for b, _, _, rows in _shard_rows(jobs, 0) if b == a)


def _gather_start(packs, lands, jobs, *, name, deps=()):
    n = len(packs)

    def body(*refs):
        p_refs, l_refs, send, recv, token = refs[:n], refs[n:2 * n], refs[-2 * n - 3], refs[-2 * n - 2], refs[-1]
        x, y, c = _coords()
        for a, lrow, row, rows in _shard_rows(jobs, 2 * x + y):
            pltpu.make_async_remote_copy(
                src_ref=p_refs[a].at[:, pl.ds(lrow, rows)], dst_ref=l_refs[a].at[:, pl.ds(row, rows)],
                send_sem=send.at[4 * a + 3], recv_sem=recv.at[4 * a + 3], device_id=(x, y, 1 - c), device_id_type=MESH).start()
            for j, (px, py) in enumerate(_other_chips(x, y)):
                pltpu.make_async_remote_copy(
                    src_ref=p_refs[a].at[c, pl.ds(lrow, rows)], dst_ref=l_refs[a].at[c, pl.ds(row, rows)],
                    send_sem=send.at[4 * a + j], recv_sem=recv.at[4 * a + j], device_id=(px, py, c), device_id_type=MESH).start()
        token[...] = jnp.zeros_like(token)

    thru = [pltpu.HBM(a.shape, a.dtype) for a in packs + lands]
    out = pl.pallas_call(
        body, name=name, in_specs=[HBM] * (2 * n) + [ANY] * len(deps),
        out_shape=(pltpu.SemaphoreType.DMA((4 * n,)), pltpu.SemaphoreType.DMA((4 * n,)), *thru, jax.ShapeDtypeStruct((8, LANE), F32)),
        out_specs=(SEMS, SEMS, *[HBM] * (2 * n), pl.BlockSpec(memory_space=pltpu.VMEM)),
        input_output_aliases={i: 2 + i for i in range(2 * n)}, compiler_params=pltpu.CompilerParams(**SPLIT),
    )(*[_in_hbm(a) for a in packs + lands], *deps)
    return dict(send=out[0], recv=out[1], bufs=list(out[2:2 + 2 * n]), n=n, jobs=jobs), out[-1]


def _gather_wait(handle, after, *, name):
    n, jobs = handle["n"], handle["jobs"]

    def body(*refs):
        l_refs, send, recv, token = refs[n:2 * n], refs[2 * n], refs[2 * n + 1], refs[-1]
        token[...] = jnp.zeros_like(token)
        x, y, c = _coords()
        for a in range(n):
            total = _shard_total(jobs, a)
            for j, like in enumerate([l_refs[a].at[0, pl.ds(0, total)]] * 3 + [l_refs[a].at[:, pl.ds(0, total)]]):
                cp = pltpu.make_async_remote_copy(src_ref=like, dst_ref=like, send_sem=send.at[4 * a + j],
                                                  recv_sem=recv.at[4 * a + j], device_id=(x, y, c), device_id_type=MESH)
                cp.wait_send()
                cp.wait_recv()

    out = pl.pallas_call(
        body, name=name, in_specs=[HBM] * (2 * n) + [SEMS, SEMS] + [ANY] * len(after),
        out_shape=[pltpu.HBM(a.shape, a.dtype) for a in handle["bufs"]] + [jax.ShapeDtypeStruct((8, LANE), F32)],
        out_specs=[HBM] * (2 * n) + [pl.BlockSpec(memory_space=pltpu.VMEM)],
        input_output_aliases={i: i for i in range(2 * n)}, compiler_params=pltpu.CompilerParams(**SPLIT),
    )(*handle["bufs"], handle["send"], handle["recv"], *after)
    return list(out[n:2 * n]), out[-1]


def _gather_forward(lands, jobs, *, name, deps=()):
    n = len(lands)

    def body(*refs):
        l_refs, send, recv = refs[n + len(deps):2 * n + len(deps)], refs[-2], refs[-1]
        x, y, c = _coords()
        for j, (px, py) in enumerate(_other_chips(x, y)):
            for a, _, row, rows in _shard_rows(jobs, 2 * px + py):
                blk = l_refs[a].at[c, pl.ds(row, rows)]
                pltpu.make_async_remote_copy(src_ref=blk, dst_ref=blk, send_sem=send.at[3 * a + j], recv_sem=recv.at[3 * a + j],
                                             device_id=(x, y, 1 - c), device_id_type=MESH).start()
        for a in range(n):
            like = l_refs[a].at[0, pl.ds(0, _shard_total(jobs, a))]
            for j in range(3):
                cp = pltpu.make_async_remote_copy(src_ref=like, dst_ref=like, send_sem=send.at[3 * a + j],
                                                  recv_sem=recv.at[3 * a + j], device_id=(x, y, c), device_id_type=MESH)
                cp.wait_send()
                cp.wait_recv()

    sem = pltpu.SemaphoreType.DMA((3 * n,))
    return pl.pallas_call(
        body, name=name, in_specs=[ANY] * (n + len(deps)), out_specs=[ANY] * n, input_output_aliases={i: i for i in range(n)},
        out_shape=[jax.ShapeDtypeStruct(a.shape, a.dtype) for a in lands], scratch_shapes=[sem, sem],
    )(*lands, *deps)


def _forward_start(lands, jobs, *, name, deps=()):
    n, nd = len(lands), len(deps)

    def body(*refs):
        l_refs, sems, token = refs[:n], refs[n + nd:n + nd + 2 * n], refs[-1]
        x, y, c = _coords()
        for j, (px, py) in enumerate(_other_chips(x, y)):
            for a, _, row, rows in _shard_rows(jobs, 2 * px + py):
                blk = l_refs[a].at[c, pl.ds(row, rows)]
                pltpu.make_async_remote_copy(src_ref=blk, dst_ref=blk, send_sem=sems[2 * a].at[j], recv_sem=sems[2 * a + 1].at[j],
                                             device_id=(x, y, 1 - c), device_id_type=MESH).start()
        token[...] = jnp.zeros_like(token)

    out = pl.pallas_call(
        body, name=name, in_specs=[HBM] * n + [ANY] * nd,
        out_shape=(*[pltpu.SemaphoreType.DMA((3,))] * (2 * n), *[pltpu.HBM(a.shape, a.dtype) for a in lands],
                   jax.ShapeDtypeStruct((8, LANE), F32)),
        out_specs=(*[SEMS] * (2 * n), *[HBM] * n, pl.BlockSpec(memory_space=pltpu.VMEM)),
        input_output_aliases={i: 2 * n + i for i in range(n)}, compiler_params=pltpu.CompilerParams(**SPLIT),
    )(*[_in_hbm(a) for a in lands], *deps)
    return [dict(send=out[2 * a], recv=out[2 * a + 1], buf=out[2 * n + a]) for a in range(n)], out[-1]


def _forward_wait(handle, jobs, a, after, *, name):
    total = _shard_total(jobs, a)

    def body(l_ref, send, recv, *rest):
        x, y, c = _coords()
        like = l_ref.at[0, pl.ds(0, total)]
        for j in range(3):
            cp = pltpu.make_async_remote_copy(src_ref=like, dst_ref=like, send_sem=send.at[j], recv_sem=recv.at[j],
                                              device_id=(x, y, c), device_id_type=MESH)
            cp.wait_send()
            cp.wait_recv()

    buf = handle["buf"]
    return pl.pallas_call(
        body, name=name, in_specs=[HBM, SEMS, SEMS] + [ANY] * len(after), out_shape=pltpu.HBM(buf.shape, buf.dtype),
        out_specs=HBM, input_output_aliases={0: 0}, compiler_params=pltpu.CompilerParams(**SPLIT),
    )(buf, handle["send"], handle["recv"], *after)


def _add_sibling(g, recv, sel, *, name):
    rows, hw = recv.shape
    tr = _row_tile(rows, hw * 4)

    def body(sel_ref, g_ref, r_ref, o_ref):
        o_ref[...] = (g_ref[...] + r_ref[...]).astype(BF16)

    return pl.pallas_call(
        body, name=name, out_shape=jax.ShapeDtypeStruct((rows, hw), BF16),
        grid_spec=pltpu.PrefetchScalarGridSpec(
            num_scalar_prefetch=1, grid=(rows // tr,),
            in_specs=[pl.BlockSpec((None, tr, hw), lambda i, s: (s[0], i, 0)), pl.BlockSpec((tr, hw), lambda i, s: (i, 0))],
            out_specs=pl.BlockSpec((tr, hw), lambda i, s: (i, 0))),
        compiler_params=_params("parallel"),
    )(sel, g, recv)


class _Job:
    def __init__(self, a, blk, n_outer, n_inner, stride, start):
        self.a, self.blk, self.n_outer, self.n_inner, self.stride, self.start = a, blk, n_outer, n_inner, stride, start
        self.rows_out = n_outer * n_inner * blk

    def pieces(self, k):
        return [(self.start(k) + o * self.stride * self.blk, self.n_inner * self.blk) for o in range(self.n_outer)]


def _block_rows(rows, cap, *also):
    best = None
    for b in range(16, min(rows, cap) + 1, 16):
        if rows % b == 0 and all(v % b == 0 for v in also):
            best = b
    assert best is not None, (rows, also)
    return best


def _ffn_jobs(lay):
    b = _block_rows(lay.nff, 704, lay.dff)
    return [_Job(0, b, 3, lay.nff // b, lay.dff // b, lambda k: lay.nff * k)]


def _ffn_weight_jobs(lay):
    b = _block_rows(lay.nff, 704)
    return [_Job(a, b, 1, lay.nff // b, 0, lambda k: lay.nff * k) for a in range(3)]


def _mix_jobs(lay):
    d, ncol, head, pad = lay.d, lay.ncol, lay.head, lay.pad
    first = lambda k, a, b: jnp.where(k == 0, a, b) if not isinstance(k, int) else (a if k == 0 else b)
    ba = _block_rows(head, 704, *[ncol * k + pad for k in (1, 2, 3)])
    bb = _block_rows(ncol - head, 704, *[ncol * k + d for k in (0, 1, 2, 3)])
    bo = _block_rows(lay.r_o, 704, d)
    bq = _block_rows(HEAD + ROPE, 704, QGROUP)
    bk = _block_rows(lay.hps * QGROUP, 704, lay.off_kv)
    return [_Job(0, ba, 1, head // ba, 0, lambda k: first(k, 0, ncol * k + pad)),
            _Job(0, bb, 1, (ncol - head) // bb, 0, lambda k: ncol * k + d),
            _Job(0, bo, 3, lay.r_o // bo, d // bo, lambda k: 7 * d + lay.r_o * k),
            _Job(1, bq, lay.hps, (HEAD + ROPE) // bq, QGROUP // bq, lambda k: QGROUP * lay.hps * k),
            _Job(1, bk, 1, lay.hps * QGROUP // bk, 0, lambda k: lay.off_kv + lay.hps * QGROUP * k)]


def _swap_start(gs, *, name):
    n = len(gs)
    lands = [lax.empty(g.shape[1:], g.dtype) for g in gs]

    def body(*refs):
        g_refs, land_refs, send, recv, token = refs[:n], refs[n:2 * n], refs[2 * n], refs[2 * n + 1], refs[-1]
        x, y, c = _coords()
        for a in range(n):
            pltpu.make_async_remote_copy(src_ref=g_refs[a].at[1 - c], dst_ref=land_refs[a], send_sem=send.at[a],
                                         recv_sem=recv.at[a], device_id=(x, y, 1 - c), device_id_type=MESH).start()
        token[...] = jnp.zeros_like(token)

    thru = [pltpu.HBM(a.shape, a.dtype) for a in gs + lands]
    out = pl.pallas_call(
        body, name=name, in_specs=[HBM] * (2 * n),
        out_shape=(pltpu.SemaphoreType.DMA((n,)), pltpu.SemaphoreType.DMA((n,)), *thru, jax.ShapeDtypeStruct((8, LANE), F32)),
        out_specs=(SEMS, SEMS, *[HBM] * (2 * n), pl.BlockSpec(memory_space=pltpu.VMEM)),
        input_output_aliases={i: 2 + i for i in range(2 * n)}, compiler_params=pltpu.CompilerParams(**SPLIT),
    )(*[_in_hbm(a) for a in gs + lands])
    return dict(send=out[0], recv=out[1], bufs=list(out[2:2 + 2 * n]), n=n), out[-1]


def _swap_wait(handle, after, *, name):
    n = handle["n"]

    def body(*refs):
        g_refs, land_refs, send, recv = refs[:n], refs[n:2 * n], refs[2 * n], refs[2 * n + 1]
        x, y, c = _coords()
        for a in range(n):
            cp = pltpu.make_async_remote_copy(src_ref=g_refs[a].at[1 - c], dst_ref=land_refs[a], send_sem=send.at[a],
                                              recv_sem=recv.at[a], device_id=(x, y, 1 - c), device_id_type=MESH)
            cp.wait_send()
            cp.wait_recv()

    out = pl.pallas_call(
        body, name=name, in_specs=[HBM] * (2 * n) + [SEMS, SEMS] + [ANY] * len(after),
        out_shape=[pltpu.HBM(a.shape, a.dtype) for a in handle["bufs"]], out_specs=[HBM] * (2 * n),
        input_output_aliases={i: i for i in range(2 * n)}, compiler_params=pltpu.CompilerParams(**SPLIT),
    )(*handle["bufs"], handle["send"], handle["recv"], *after)
    return list(out[:n]), list(out[n:])


def _exchange_start(ss, jobs, *, name):
    n = len(ss)
    lands = [lax.empty((3,) + s.shape, s.dtype) for s in ss]

    def body(*refs):
        s_refs, land_refs, send, recv, token = refs[:n], refs[n:2 * n], refs[2 * n], refs[2 * n + 1], refs[-1]
        x, y, c = _coords()
        for j, (px, py) in enumerate(_other_chips(x, y)):
            for job in jobs:
                for row, rows in job.pieces(2 * px + py):
                    pltpu.make_async_remote_copy(
                        src_ref=s_refs[job.a].at[pl.ds(row, rows)], dst_ref=land_refs[job.a].at[j, pl.ds(row, rows)],
                        send_sem=send.at[n * j + job.a], recv_sem=recv.at[n * j + job.a], device_id=(px, py, c),
                        device_id_type=MESH).start()
        token[...] = jnp.zeros_like(token)

    thru = [pltpu.HBM(a.shape, a.dtype) for a in ss + lands]
    out = pl.pallas_call(
        body, name=name, in_specs=[HBM] * (2 * n),
        out_shape=(pltpu.SemaphoreType.DMA((3 * n,)), pltpu.SemaphoreType.DMA((3 * n,)), *thru, jax.ShapeDtypeStruct((8, LANE), F32)),
        out_specs=(SEMS, SEMS, *[HBM] * (2 * n), pl.BlockSpec(memory_space=pltpu.VMEM)),
        input_output_aliases={i: 2 + i for i in range(2 * n)}, compiler_params=pltpu.CompilerParams(**SPLIT),
    )(*[_in_hbm(a) for a in ss + lands])
    return dict(send=out[0], recv=out[1], bufs=list(out[2:2 + 2 * n]), n=n, jobs=jobs), out[-1]


def _exchange_wait(handle, after, *, name):
    n, jobs = handle["n"], handle["jobs"]
    total = [sum(rows for job in jobs if job.a == a for _, rows in job.pieces(0)) for a in range(n)]

    def body(*refs):
        s_refs, land_refs, send, recv = refs[:n], refs[n:2 * n], refs[2 * n], refs[2 * n + 1]
        x, y, c = _coords()
        for a in range(n):
            for j in range(3):
                all_rows = land_refs[a].at[0, pl.ds(0, total[a])]
                cp = pltpu.make_async_remote_copy(src_ref=all_rows, dst_ref=all_rows, send_sem=send.at[n * j + a],
                                                  recv_sem=recv.at[n * j + a], device_id=(x, y, c), device_id_type=MESH)
                cp.wait_send()
                cp.wait_recv()

    out = pl.pallas_call(
        body, name=name, in_specs=[HBM] * (2 * n) + [SEMS, SEMS] + [ANY] * len(after),
        out_shape=[pltpu.HBM(a.shape, a.dtype) for a in handle["bufs"]], out_specs=[HBM] * (2 * n),
        input_output_aliases={i: i for i in range(2 * n)}, compiler_params=pltpu.CompilerParams(**SPLIT),
    )(*handle["bufs"], handle["send"], handle["recv"], *after)
    return list(out[:n]), list(out[n:])


def _add_shard(s, land, job, sel, k, *, name):
    hw = s.shape[1]
    blk, no, ni, stride = job.blk, job.n_outer, job.n_inner, job.stride
    scal = jnp.stack([sel, job.start(k) // blk]).astype(jnp.int32)

    def body(sc_ref, own_ref, r_ref, o_ref):
        o_ref[...] = ((own_ref[...].astype(F32) + r_ref[0].astype(F32)) + r_ref[1].astype(F32)) + r_ref[2].astype(F32)

    return pl.pallas_call(
        body, name=name, out_shape=jax.ShapeDtypeStruct((2, job.rows_out, hw), F32),
        grid_spec=pltpu.PrefetchScalarGridSpec(
            num_scalar_prefetch=1, grid=(no, ni),
            in_specs=[pl.BlockSpec((blk, hw), lambda o, b, sc: (sc[1] + o * stride + b, 0)),
                      pl.BlockSpec((3, blk, hw), lambda o, b, sc: (0, sc[1] + o * stride + b, 0))],
            out_specs=pl.BlockSpec((None, blk, hw), lambda o, b, sc: (sc[0], o * ni + b, 0))),
        compiler_params=_params("parallel", "parallel"),
    )(scal, s, land)


def _join_list(fs, *, name):
    n = len(fs)

    def body(*refs):
        f_refs, send_sems, recv_sems = refs[n:2 * n], refs[2 * n], refs[2 * n + 1]
        x, y, c = _coords()
        copies = [pltpu.make_async_remote_copy(
            src_ref=f.at[c], dst_ref=f.at[c], send_sem=send_sems.at[a], recv_sem=recv_sems.at[a],
            device_id=(x, y, 1 - c), device_id_type=MESH) for a, f in enumerate(f_refs)]
        for cp in copies:
            cp.start()
        for cp in copies:
            cp.wait()

    sem = pltpu.SemaphoreType.DMA((n,))
    return pl.pallas_call(
        body, name=name, in_specs=[ANY] * n, out_specs=[ANY] * n, input_output_aliases={i: i for i in range(n)},
        out_shape=[jax.ShapeDtypeStruct(f.shape, f.dtype) for f in fs], scratch_shapes=[sem, sem],
    )(*fs)


def _all_reduce_small(vec, *, name):
    n = vec.shape[1]

    def body(v_ref, o_ref, buf, send_sems, recv_sems):
        x, y, c = _coords()
        me = 4 * x + 2 * y + c
        buf[me] = v_ref[...]
        copies = []
        for m in range(1, 8):
            peer = (x ^ ((m >> 2) & 1), y ^ ((m >> 1) & 1), c ^ (m & 1))
            copies.append(pltpu.make_async_remote_copy(
                src_ref=v_ref, dst_ref=buf.at[me], send_sem=send_sems.at[m - 1], recv_sem=recv_sems.at[m - 1],
                device_id=peer, device_id_type=MESH))
        for cp in copies:
            cp.start()
        for cp in copies:
            cp.wait()
        acc = buf[0]
        for d in range(1, 8):
            acc = acc + buf[d]
        o_ref[...] = acc

    return pl.pallas_call(
        body, name=name, out_shape=jax.ShapeDtypeStruct((1, n), F32),
        in_specs=[pl.BlockSpec(memory_space=pltpu.VMEM)], out_specs=pl.BlockSpec(memory_space=pltpu.VMEM),
        scratch_shapes=[pltpu.VMEM((8, 1, n), F32), pltpu.SemaphoreType.DMA((7,)), pltpu.SemaphoreType.DMA((7,))],
    )(vec)


def _ffn_fwd(x, n_pre, n_post, weight, lay, tag):
    h = _norm_fwd(x, n_pre, name=f"{tag}_norm_pre", out_dtype=BF16)
    wg = (weight(0, [h]), 0, lay.dff)
    g = _mm([(h, wg)], name=f"{tag}_gate", mode="nt")
    wu = (weight(1, [g]), 0, lay.dff)
    u, a = _mm([(h, wu)], name=f"{tag}_up", mode="nt", extras=[g], out_dtypes=[F32, BF16], tm_cap=MM_TILE // 2,
               epilogue=lambda up, gate: (up, _silu(gate) * up))
    wd = (weight(2, [u]), 0, lay.dff)
    yv = _mm([(a, wd)], name=f"{tag}_down", mode="nn")
    out = _norm_fwd(yv, n_post, name=f"{tag}_norm_post", resid=x, scale=MACARON_SCALE)
    return out, (x, h, g, u, a, yv), (wg, wu, wd)


def _ffn_bwd(dout, saved, n_pre, n_post, weights, lay, tag, deps=(), after_act=None, after_dw=None):
    x, h, g, u, a, yv = saved
    dff = lay.dff
    gbuf = lax.empty((2, 3 * dff, lay.d // 2), F32)
    dy, dn_post = _norm_bwd(yv, n_post, dout, name=f"{tag}_norm_post_bwd", scale=MACARON_SCALE)
    wg, wu, wd = weights
    dg, du = _mm([(dy, wd)], name=f"{tag}_down_dx", mode="nt", deps=deps, extras=[g, u],
                 out_dtypes=[BF16, BF16], tm_cap=MM_TILE // 2,
                 epilogue=lambda da, gate, up: (da * up * _dsilu(gate), da * _silu(gate)))
    deps = after_act(du) if after_act is not None else ()
    gbuf = _mm([(a, dy)], name=f"{tag}_down_dw", mode="tn", into=(gbuf, 2 * dff), deps=deps)
    gbuf = _mm([(dg, h)], name=f"{tag}_gate_dw", mode="tn", into=(gbuf, 0))
    gbuf = _mm([(du, h)], name=f"{tag}_up_dw", mode="tn", into=(gbuf, dff))
    deps = after_dw(gbuf)
    dh = _mm([(dg, wg), (du, wu)], name=f"{tag}_up_dx", mode="nn", deps=deps)
    dx, dn_pre = _norm_bwd(x, n_pre, dh, name=f"{tag}_norm_pre_bwd", dres=dout)
    return dx, dn_pre, dn_post


def _rope_tables(positions):
    half = ROPE // 2
    inv_freq = ROPE_THETA ** (-jnp.arange(half, dtype=F32) / half)
    ang = positions.astype(F32)[:, None] * inv_freq
    cos, sin = jnp.cos(ang), jnp.sin(ang)
    z = jnp.zeros_like(cos)
    z2 = jnp.zeros((positions.shape[0], LANE - ROPE), F32)
    return (jnp.concatenate([cos, cos, z2], axis=1), jnp.concatenate([-sin, z, z2], axis=1),
            jnp.concatenate([z, sin, z2], axis=1))


def kernel(x, positions, ffn1_norm_pre, ffn1_w_gate, ffn1_w_up, ffn1_w_down, ffn1_norm_post, mix_norm_pre, w_in, mla_q_norm, mla_w_q_up, mla_kv_norm, mla_w_kv_up, mla_w_o, hgrn_lb_logits, hgrn_out_norm, hgrn_w_o, w_out, mix_norm_post, ffn2_norm_pre, ffn2_w_gate, ffn2_w_up, ffn2_w_down, ffn2_norm_post, loss_target, m_ffn1_norm_pre, m_ffn1_w_gate, m_ffn1_w_up, m_ffn1_w_down, m_ffn1_norm_post, m_mix_norm_pre, m_w_in, m_mla_q_norm, m_mla_w_q_up, m_mla_kv_norm, m_mla_w_kv_up, m_mla_w_o, m_hgrn_lb_logits, m_hgrn_out_norm, m_hgrn_w_o, m_w_out, m_mix_norm_post, m_ffn2_norm_pre, m_ffn2_w_gate, m_ffn2_w_up, m_ffn2_w_down, m_ffn2_norm_post, v_ffn1_norm_pre, v_ffn1_w_gate, v_ffn1_w_up, v_ffn1_w_down, v_ffn1_norm_post, v_mix_norm_pre, v_w_in, v_mla_q_norm, v_mla_w_q_up, v_mla_kv_norm, v_mla_w_kv_up, v_mla_w_o, v_hgrn_lb_logits, v_hgrn_out_norm, v_hgrn_w_o, v_w_out, v_mix_norm_post, v_ffn2_norm_pre, v_ffn2_w_gate, v_ffn2_w_up, v_ffn2_w_down, v_ffn2_norm_post):
    given = dict(locals())
    wts = {n: given[n] for n in ALL_WEIGHTS}
    mom = {n: given["m_" + n] for n in ALL_WEIGHTS}
    var = {n: given["v_" + n] for n in ALL_WEIGHTS}
    xin = x[0]
    target = loss_target[0]
    t, d = xin.shape
    cx, cy, cc = _coords()

    q_lora, kv_lora = mla_q_norm.shape[1], mla_kv_norm.shape[1]
    nh_mla = 4 * mla_w_kv_up.shape[2] // QGROUP
    lay = _Layout(d, 4 * ffn1_w_gate.shape[2], 4 * w_in.shape[2], q_lora, kv_lora, nh_mla)
    jobs_ffn, jobs_mix = _ffn_jobs(lay), _mix_jobs(lay)
    def pack(src, col_sharded, row_sharded=()):
        a = jnp.concatenate([src[n][0].T.astype(BF16) for n in col_sharded] + [src[n][0].astype(BF16) for n in row_sharded])
        return a.reshape(a.shape[0], 2, a.shape[1] // 2).transpose(1, 0, 2)

    jobs_w = _ffn_weight_jobs(lay)
    ffn_packs = lambda src, tag: [pack(src, [f"{tag}_w_gate"]), pack(src, [f"{tag}_w_up"]), pack(src, [], [f"{tag}_w_down"])]
    ffn_lands = lambda: [lax.empty((2, lay.dff, d // 2), BF16) for _ in range(3)]

    def handed_over(handles, tag):
        return lambda i, after: _forward_wait(handles[i], jobs_w, i, after, name=f"gather_{tag}_forward_wait_{i}")

    got1, tok = _gather_start(ffn_packs(wts, "ffn1"), ffn_lands(), jobs_w, name="gather_ffn1")
    later, _ = lax.optimization_barrier(({n: wts[n] for n in BIG_WEIGHTS if not n.startswith("ffn1")}, tok))
    packs_mix = [pack(later, ["w_in"], ["mla_w_o", "hgrn_w_o", "w_out"]), pack(later, ["mla_w_q_up", "mla_w_kv_up"])]
    packs_ffn2 = ffn_packs(later, "ffn2")
    lands_mix = [jnp.zeros((2, 10 * d, d // 2), BF16), jnp.zeros((2, lay.rows_narrow, q_lora // 2), BF16)]
    arrived, tok = _gather_wait(got1, packs_mix + packs_ffn2 + lands_mix, name="gather_ffn1_wait")
    got_m, tok = _gather_start(packs_mix, lands_mix, jobs_mix, name="gather_mix", deps=[tok])
    handing1, _ = _forward_start(arrived, jobs_w, name="gather_ffn1_forward", deps=[tok])
    col_kr = q_lora + kv_lora
    hgrn_cols = [d, 2 * d, 3 * d, 4 * d]
    col_ga, col_gb = 5 * d, 6 * d
    tabs = _rope_tables(positions[0])
    scale = (HEAD + ROPE) ** -0.5

    x1, saved1, w_ffn1 = _ffn_fwd(xin, ffn1_norm_pre, ffn1_norm_post, handed_over(handing1, "ffn1"), lay, "ffn1")

    arrived, tok = _gather_wait(got_m, [x1], name="gather_mix_wait")
    got2, tok = _gather_start(packs_ffn2, ffn_lands(), jobs_w, name="gather_ffn2", deps=[tok])
    wide, narrow = _gather_forward(arrived, jobs_mix, name="gather_mix_forward", deps=[tok])
    w_in_v = (wide, 0, 7 * d)
    w_o_v = {n: (wide, (7 + i) * d, d) for i, n in enumerate(("mla_w_o", "hgrn_w_o", "w_out"))}
    w_q_v = (narrow, lay.off_q, nh_mla * QGROUP)
    w_kv_v = (narrow, lay.off_kv, nh_mla * QGROUP)

    h2 = _norm_fwd(x1, mix_norm_pre, name="mix_norm_pre", out_dtype=BF16)
    proj = _mm([(h2, w_in_v)], name="mix_in", mode="nt", deps=[tok])
    cqn = _norm_fwd(proj, mla_q_norm, name="mla_q_norm", out_dtype=BF16, col=0)
    ckvn = _norm_fwd(proj, mla_kv_norm, name="mla_kv_norm", out_dtype=BF16, col=q_lora)
    qp = _mm([(cqn, w_q_v)], name="mla_q_up", mode="nt")
    kvb = _mm([(ckvn, w_kv_v)], name="mla_kv_up", mode="nt", out_dtype=BF16)
    qcat = _rope(qp, tabs, name="rope_q", group=QGROUP, backward=False, out_dtype=BF16)
    krot = _rope(proj, tabs, name="rope_k", group=LANE, backward=False, out_dtype=BF16, col=col_kr, ngroup=1)
    o_mla = _attn_fwd(qcat, kvb, krot, name="mla_attention", scale=scale)
    y_a = _mm([(o_mla, w_o_v["mla_w_o"])], name="mla_out", mode="nn")

    o_raw, yb, states = _hgrn_fwd(proj, hgrn_cols, d, hgrn_lb_logits, hgrn_out_norm, name="hgrn_scan")
    handing2, tok = _forward_start(_gather_wait(got2, [o_raw], name="gather_ffn2_wait")[0], jobs_w, name="gather_ffn2_forward")
    y_b = _mm([(yb, w_o_v["hgrn_w_o"])], name="hgrn_out", mode="nn", deps=[tok])

    merged = _merge_fwd(proj, col_ga, col_gb, y_a, y_b, name="mix_merge")
    y_mix = _mm([(merged, w_o_v["w_out"])], name="mix_out", mode="nn")
    x2 = _norm_fwd(y_mix, mix_norm_post, name="mix_norm_post", resid=x1, scale=1.0)

    x3, saved2, w_ffn2 = _ffn_fwd(x2, ffn2_norm_pre, ffn2_norm_post, handed_over(handing2, "ffn2"), lay, "ffn2")
    dx3, loss_local = _loss_head(x3, target, name="loss_head")

    grads, deltas, new_m, new_v = {}, {}, {}, {}
    sel = cc.astype(jnp.int32)
    sel1 = jnp.reshape(sel, (1,))
    me_chip = (2 * cx + cy).astype(jnp.int32)

    def reduce_mid(handle, after, jobs, tag):
        bufs, recvd = _swap_wait(handle, after, name=f"grad_swap_{tag}_wait")
        sums = [_add_sibling(b, r, sel1, name=f"grad_add_sibling_{tag}_{i}") for i, (b, r) in enumerate(zip(bufs, recvd))]
        return _exchange_start(sums, jobs, name=f"grad_exchange_{tag}")

    def reduce_end(handle, after, tag):
        sums, lands = _exchange_wait(handle, after, name=f"grad_exchange_{tag}_wait")
        parts = [_add_shard(sums[job.a], lands[job.a], job, sel, me_chip, name=f"grad_add_chips_{tag}_{i}")
                 for i, job in enumerate(handle["jobs"])]
        return _join_list(parts, name=f"grad_join_{tag}")

    def natural(part, lo, rows, transposed):
        g_n = part[:, lo:lo + rows]
        hw_n = g_n.shape[2]
        return g_n.transpose(0, 2, 1).reshape(2 * hw_n, rows) if transposed else g_n.transpose(1, 0, 2).reshape(rows, 2 * hw_n)

    def adam(names, deps=()):
        for i, n in enumerate(names):
            shp = wts[n].shape
            two_d = (lambda a: a[0]) if n in BIG_WEIGHTS else (lambda a: a)
            dl, nm, nv = _adamw(two_d(wts[n]), grads[n], two_d(mom[n]), two_d(var[n]), name=f"adamw_{n}",
                                deps=deps if i == 0 else ())
            grads[n] = grads[n].reshape(shp)
            deltas[n], new_m[n], new_v[n] = dl.reshape(shp), nm.reshape(shp), nv.reshape(shp)
        return [deltas[n] for n in names]

    def ffn_grads(joined, tag, deps=()):
        nff = lay.nff
        grads[f"{tag}_w_gate"] = natural(joined[0], 0, nff, True)
        grads[f"{tag}_w_up"] = natural(joined[0], nff, nff, True)
        grads[f"{tag}_w_down"] = natural(joined[0], 2 * nff, nff, False)
        return adam([f"{tag}_w_gate", f"{tag}_w_up", f"{tag}_w_down"], deps)

    swaps = {}

    def start_swap(tag):
        def hook(gbuf):
            swaps[tag], started = _swap_start([gbuf], name=f"grad_swap_{tag}")
            return [started]
        return hook

    dx2, grads["ffn2_norm_pre"], grads["ffn2_norm_post"] = _ffn_bwd(
        dx3, saved2, ffn2_norm_pre, ffn2_norm_post, w_ffn2, lay, "ffn2", after_dw=start_swap("ffn2"))

    gwide = lax.empty((2, 10 * d, d // 2), F32)
    gnarrow = lax.empty((2, lay.rows_narrow, q_lora // 2), F32)
    dy_mix, grads["mix_norm_post"] = _norm_bwd(y_mix, mix_norm_post, dx2, name="mix_norm_post_bwd")
    dmerged = _mm([(dy_mix, w_o_v["w_out"])], name="mix_out_dx", mode="nt")
    gwide = _mm([(merged, dy_mix)], name="mix_out_dw", mode="tn", into=(gwide, 9 * d))
    dga, dgb, dy_a, dy_b = _merge_bwd(dmerged, proj, col_ga, col_gb, y_a, y_b, name="mix_merge_bwd")

    do_mla = _mm([(dy_a, w_o_v["mla_w_o"])], name="mla_out_dx", mode="nt")
    gwide = _mm([(o_mla, dy_a)], name="mla_out_dw", mode="tn", into=(gwide, 7 * d))
    dqcat, dkv, dkr = _attn_bwd(qcat, kvb, krot, do_mla, name="mla_attention_bwd", scale=scale)
    exch2, tok = reduce_mid(swaps["ffn2"], [dkr], _ffn_jobs(lay), "ffn2")

    dqp = _rope(dqcat, tabs, name="rope_q_bwd", group=QGROUP, backward=True, out_dtype=BF16)
    dk_r = _rope(dkr, tabs, name="rope_k_bwd", group=LANE, backward=True, out_dtype=BF16)
    dcqn = _mm([(dqp, w_q_v)], name="mla_q_up_dx", mode="nn", deps=[tok])
    gnarrow = _mm([(dqp, cqn)], name="mla_q_up_dw", mode="tn", into=(gnarrow, lay.off_q))
    dkvb = dkv.astype(BF16)
    dckvn = _mm([(dkvb, w_kv_v)], name="mla_kv_up_dx", mode="nn")
    gnarrow = _mm([(dkvb, ckvn)], name="mla_kv_up_dw", mode="tn", into=(gnarrow, lay.off_kv))
    dc_q, grads["mla_q_norm"] = _norm_bwd(proj, mla_q_norm, dcqn, name="mla_q_norm_bwd", col=0, dx_dtype=BF16)
    dc_kv, grads["mla_kv_norm"] = _norm_bwd(proj, mla_kv_norm, dckvn, name="mla_kv_norm_bwd", col=q_lora, dx_dtype=BF16)

    dyb = _mm([(dy_b, w_o_v["hgrn_w_o"])], name="hgrn_out_dx", mode="nt")
    gwide = _mm([(yb, dy_b)], name="hgrn_out_dw", mode="tn", into=(gwide, 8 * d))
    dhq, dhf, dhi, dhg, dlb_h, dnorm_h = _hgrn_bwd(proj, hgrn_cols, d, o_raw, dyb, states, hgrn_lb_logits, hgrn_out_norm,
                                                   name="hgrn_scan_bwd")

    dhead = jnp.concatenate([dc_q, dc_kv, dk_r, jnp.zeros((t, d - col_kr - LANE), BF16)], axis=1)
    dparts = [dhead, dhq, dhf, dhi, dhg, dga, dgb]
    dh2 = _mm([(p, (wide, i * d, d)) for i, p in enumerate(dparts)], name="mix_in_dx", mode="nn")
    for i, p in enumerate(dparts):
        gwide = _mm([(p, h2)], name=f"mix_in_dw_{i}", mode="tn", into=(gwide, i * d))
    dx1, grads["mix_norm_pre"] = _norm_bwd(x1, mix_norm_pre, dh2, name="mix_norm_pre_bwd", dres=dx2)
    swap_m, tok = _swap_start([gwide, gnarrow], name="grad_swap_mix")
    joined2 = reduce_end(exch2, [dx1], "ffn2")

    exchanges = {}

    def mix_exchange(after):
        exchanges["mix"], started = reduce_mid(swap_m, [after], _mix_jobs(lay), "mix")
        return [started]

    dx0, grads["ffn1_norm_pre"], grads["ffn1_norm_post"] = _ffn_bwd(
        dx1, saved1, ffn1_norm_pre, ffn1_norm_post, w_ffn1, lay, "ffn1", deps=[tok], after_act=mix_exchange,
        after_dw=start_swap("ffn1"))
    exch1, tok = reduce_mid(swaps["ffn1"], [dx0], _ffn_jobs(lay), "ffn1")

    joined_m = reduce_end(exchanges["mix"], [dx0, tok], "mix")
    done = ffn_grads(joined2, "ffn2")
    grads["w_in"] = natural(jnp.concatenate([joined_m[0], joined_m[1]], axis=1), 0, lay.ncol, True)
    for i, n in enumerate(("mla_w_o", "hgrn_w_o", "w_out")):
        grads[n] = natural(joined_m[2], i * lay.r_o, lay.r_o, False)
    grads["mla_w_q_up"] = natural(joined_m[3], 0, lay.hps * (HEAD + ROPE), True)
    grads["mla_w_kv_up"] = natural(joined_m[4], 0, lay.hps * QGROUP, True)
    done += adam(["w_in", "mla_w_q_up", "mla_w_kv_up", "mla_w_o", "hgrn_w_o", "w_out"])

    joined1 = reduce_end(exch1, done, "ffn1")

    dlb = dlb_h.reshape(1, -1)
    dnorm = jnp.sum(dnorm_h, axis=0)
    small = {**{n: grads[n] for n in SMALL_WEIGHTS if n not in ("hgrn_lb_logits", "hgrn_out_norm")},
             "hgrn_lb_logits": dlb, "hgrn_out_norm": dnorm}
    vec, _ = lax.optimization_barrier((jnp.concatenate([small[n] for n in SMALL_WEIGHTS], axis=1), joined1[0]))
    vec = _all_reduce_small(vec, name="grad_all_reduce_small")
    off = 0
    for n in SMALL_WEIGHTS:
        w_n = small[n].shape[1]
        grads[n] = vec[:, off:off + w_n]
        off += w_n
    grads["hgrn_lb_logits"] = _lb_logits_grad(hgrn_lb_logits, grads["hgrn_lb_logits"], name="lb_logits_grad")

    adam(list(SMALL_WEIGHTS))
    ffn_grads(joined1, "ffn1")

    loss = lax.psum(loss_local, ("x", "y", "c"))
    dx_out = dx0.reshape(x.shape)
    return (loss, dx_out, *[grads[n] for n in ALL_WEIGHTS], *[deltas[n] for n in ALL_WEIGHTS],
            *[new_m[n] for n in ALL_WEIGHTS], *[new_v[n] for n in ALL_WEIGHTS])
```

```python
import jax
import jax.numpy as jnp
from jax import lax
from jax.experimental import pallas as pl
from jax.experimental.pallas import tpu as pltpu

F32 = jnp.float32
BF16 = jnp.bfloat16
MESH = pl.DeviceIdType.MESH

NORM_EPS = 1e-6
MACARON_SCALE = 0.5
ROPE_THETA = 10000.0
HEAD = 128
ROPE = 64
QGROUP = 2 * HEAD
SUB = 16
ADAM_LR, ADAM_B1, ADAM_B2, ADAM_EPS, ADAM_WD, ADAM_STEP = 0.001, 0.9, 0.999, 1e-08, 0.01, 10

LANE = 128
VMEM_LIMIT = 48 * 1024 * 1024
MM_TILE = 1024
MM_TILE_WIDE = 1536

BIG_WEIGHTS = ("ffn1_w_gate", "ffn1_w_up", "ffn1_w_down", "w_in", "mla_w_q_up", "mla_w_kv_up",
               "mla_w_o", "hgrn_w_o", "w_out", "ffn2_w_gate", "ffn2_w_up", "ffn2_w_down")
COL_SHARDED = ("ffn1_w_gate", "ffn1_w_up", "w_in", "mla_w_q_up", "mla_w_kv_up", "ffn2_w_gate", "ffn2_w_up")
SMALL_WEIGHTS = ("ffn1_norm_pre", "ffn1_norm_post", "mix_norm_pre", "mla_q_norm", "mla_kv_norm",
                 "hgrn_lb_logits", "hgrn_out_norm", "mix_norm_post", "ffn2_norm_pre", "ffn2_norm_post")
ALL_WEIGHTS = ("ffn1_norm_pre", "ffn1_w_gate", "ffn1_w_up", "ffn1_w_down", "ffn1_norm_post", "mix_norm_pre",
               "w_in", "mla_q_norm", "mla_w_q_up", "mla_kv_norm", "mla_w_kv_up", "mla_w_o", "hgrn_lb_logits",
               "hgrn_out_norm", "hgrn_w_o", "w_out", "mix_norm_post", "ffn2_norm_pre", "ffn2_w_gate",
               "ffn2_w_up", "ffn2_w_down", "ffn2_norm_post")


def _params(*sem):
    return pltpu.CompilerParams(dimension_semantics=sem or None, vmem_limit_bytes=VMEM_LIMIT)


def _pick(n, cap, offset=0):
    if n <= cap and offset % n == 0:
        return n
    best = None
    for t in range(LANE, min(n, cap) + 1, LANE):
        if n % t == 0 and offset % t == 0:
            best = t
    assert best is not None, (n, cap, offset)
    return best


def _row_tile(n, row_bytes, budget=1 << 20):
    best = None
    for t in range(8, n + 1, 8):
        if n % t == 0 and t * row_bytes <= budget:
            best = t
    return n if best is None else best


def _sigmoid(x):
    return 1.0 / (1.0 + jnp.exp(-x))


def _silu(x):
    return x * _sigmoid(x)


def _dsilu(x):
    s = _sigmoid(x)
    return s * (1.0 + x * (1.0 - s))


def _mm(pairs, *, name, mode="nn", out_dtype=F32, into=None, deps=(), extras=(), epilogue=None, out_dtypes=None, tm_cap=None):
    halves = isinstance(pairs[0][1], tuple)
    assert halves or mode == "tn"
    pairs = [(a, b if halves else (b, 0, b.shape[0])) for a, b in pairs]
    a0, (b0, b_off, b_rows) = pairs[0]
    hw = b0.shape[2] if halves else (into[0].shape[2] if into is not None else None)
    if mode == "nn":
        (m, kdim), n = a0.shape, 2 * hw
    elif mode == "nt":
        (m, kdim), n = a0.shape, b_rows
        assert kdim == 2 * hw
    else:
        (kdim, m), n = a0.shape, b0.shape[1]
    out_off = 0 if into is None else into[1]
    tm = _pick(m, tm_cap or (MM_TILE_WIDE if mode == "tn" else MM_TILE), out_off)
    tn = hw if (mode == "nn" or into is not None) else _pick(n, MM_TILE_WIDE, b_off if mode == "nt" else 0)
    tk = hw if mode == "nt" else _pick(kdim, MM_TILE if len(pairs) <= 2 else MM_TILE // 2, b_off if mode == "nn" else 0)
    assert n % tn == 0 and kdim % tk == 0
    nk = kdim // tk
    npair = len(pairs)
    dims = {"nn": (((1,), (0,)), ((), ())), "nt": (((1,), (1,)), ((), ())), "tn": (((0,), (0,)), ((), ()))}[mode]

    nout = 1 if epilogue is None else len(out_dtypes)

    def body(*refs):
        ins, x_refs = refs[:2 * npair], refs[2 * npair:2 * npair + len(extras)]
        o_refs, acc_ref = refs[-1 - nout:-1], refs[-1]
        k = pl.program_id(2)

        @pl.when(k == 0)
        def _():
            acc_ref[...] = jnp.zeros_like(acc_ref)

        for p in range(npair):
            a = ins[2 * p][...].astype(BF16)
            b = ins[2 * p + 1][...].astype(BF16)
            acc_ref[...] += lax.dot_general(a, b, dims, preferred_element_type=F32)

        @pl.when(k == nk - 1)
        def _():
            outs = (acc_ref[...],) if epilogue is None else epilogue(acc_ref[...], *[x[...] for x in x_refs])
            for o_ref, o in zip(o_refs, outs):
                o_ref[...] = o.astype(o_ref.dtype)

    a_spec = pl.BlockSpec((tk, tm), lambda i, j, k: (k, i)) if mode == "tn" else pl.BlockSpec((tm, tk), lambda i, j, k: (i, k))
    in_specs, flat = [], []
    for a, (b, off, _) in pairs:
        if mode == "nt":
            b_spec = pl.BlockSpec((None, tn, tk), lambda i, j, k, o=off // tn: (k, j + o, 0))
        elif mode == "nn":
            b_spec = pl.BlockSpec((None, tk, tn), lambda i, j, k, o=off // tk: (j, k + o, 0))
        else:
            b_spec = pl.BlockSpec((tk, tn), lambda i, j, k: (k, j))
        in_specs += [a_spec, b_spec]
        flat += [a, b]
    for extra in extras:
        in_specs.append(pl.BlockSpec((tm, tn), lambda i, j, k: (i, j)))
        flat.append(extra)
    for dep in deps:
        in_specs.append(pl.BlockSpec(memory_space=pl.ANY))
        flat.append(dep)
    if epilogue is not None:
        assert into is None
        out_shape, aliases = [jax.ShapeDtypeStruct((m, n), dt) for dt in out_dtypes], {}
        out_spec = [pl.BlockSpec((tm, tn), lambda i, j, k: (i, j))] * nout
    elif into is None:
        out_shape, aliases = jax.ShapeDtypeStruct((m, n), out_dtype), {}
        out_spec = pl.BlockSpec((tm, tn), lambda i, j, k: (i, j))
    else:
        out_shape, aliases = jax.ShapeDtypeStruct(into[0].shape, into[0].dtype), {len(flat): 0}
        out_spec = pl.BlockSpec((None, tm, tn), lambda i, j, k, o=out_off // tm: (j, i + o, 0))
        in_specs.append(pl.BlockSpec(memory_space=pl.ANY))
        flat.append(into[0])
    return pl.pallas_call(
        body, name=name, grid=(m // tm, n // tn, nk),
        in_specs=in_specs,
        out_specs=out_spec,
        out_shape=out_shape, input_output_aliases=aliases,
        scratch_shapes=[pltpu.VMEM((tm, tn), F32)],
        compiler_params=_params("parallel", "parallel", "arbitrary"),
    )(*flat)


def _norm_fwd(y, w, *, name, resid=None, scale=1.0, out_dtype=F32, col=0):
    t, d = y.shape[0], w.shape[1]
    tr = _pick(t, 256)
    assert col % d == 0

    def body(*refs):
        if resid is None:
            y_ref, w_ref, o_ref = refs
        else:
            y_ref, w_ref, r_ref, o_ref = refs
        yv = y_ref[...]
        out = yv * lax.rsqrt(jnp.mean(yv * yv, axis=-1, keepdims=True) + NORM_EPS) * w_ref[...]
        if resid is not None:
            out = r_ref[...] + scale * out
        o_ref[...] = out.astype(out_dtype)

    row = pl.BlockSpec((tr, d), lambda i: (i, 0))
    wspec = pl.BlockSpec((1, d), lambda i: (0, 0))
    ins, specs = [y, w], [pl.BlockSpec((tr, d), lambda i: (i, col // d)), wspec]
    if resid is not None:
        ins.append(resid)
        specs.append(row)
    return pl.pallas_call(
        body, name=name, grid=(t // tr,), in_specs=specs, out_specs=row,
        out_shape=jax.ShapeDtypeStruct((t, d), out_dtype), compiler_params=_params("parallel"),
    )(*ins)


def _norm_bwd(x, w, dy, *, name, scale=1.0, dres=None, col=0, dx_dtype=F32):
    t, d = x.shape[0], w.shape[1]
    tr = _pick(t, 256)
    assert col % d == 0

    def body(*refs):
        if dres is None:
            x_ref, w_ref, dy_ref, dx_ref, dw_ref = refs
        else:
            x_ref, w_ref, dy_ref, dr_ref, dx_ref, dw_ref = refs

        @pl.when(pl.program_id(0) == 0)
        def _():
            dw_ref[...] = jnp.zeros_like(dw_ref)

        xv = x_ref[...]
        r = lax.rsqrt(jnp.mean(xv * xv, axis=-1, keepdims=True) + NORM_EPS)
        xhat = xv * r
        dyv = dy_ref[...].astype(F32) * scale
        dw_ref[...] += jnp.sum(dyv * xhat, axis=0, keepdims=True)
        t_ = dyv * w_ref[...]
        dx = r * (t_ - xhat * jnp.mean(t_ * xhat, axis=-1, keepdims=True))
        if dres is not None:
            dx = dx + dr_ref[...]
        dx_ref[...] = dx.astype(dx_dtype)

    row = pl.BlockSpec((tr, d), lambda i: (i, 0))
    wspec = pl.BlockSpec((1, d), lambda i: (0, 0))
    ins, specs = [x, w, dy], [pl.BlockSpec((tr, d), lambda i: (i, col // d)), wspec, row]
    if dres is not None:
        ins.append(dres)
        specs.append(row)
    return pl.pallas_call(
        body, name=name, grid=(t // tr,), in_specs=specs, out_specs=(row, wspec),
        out_shape=(jax.ShapeDtypeStruct((t, d), dx_dtype), jax.ShapeDtypeStruct((1, d), F32)),
        compiler_params=_params("arbitrary"),
    )(*ins)


def _elementwise(fn, ins, out_dtypes, *, name, width=None, cols=None):
    t = ins[0].shape[0]
    d = ins[0].shape[1] if width is None else width
    cols = [0] * len(ins) if cols is None else cols
    tc = _pick(d, 2048)
    for c in cols:
        tc = _pick(d, tc, c)
    tr = _row_tile(t, tc * 4)
    nout = len(out_dtypes)

    def body(*refs):
        outs = fn(*[r[...].astype(F32) for r in refs[:len(ins)]])
        for o_ref, o in zip(refs[len(ins):], outs):
            o_ref[...] = o.astype(o_ref.dtype)

    spec = pl.BlockSpec((tr, tc), lambda i, j: (i, j))
    in_specs = [pl.BlockSpec((tr, tc), lambda i, j, o=c // tc: (i, j + o)) for c in cols]
    return pl.pallas_call(
        body, name=name, grid=(t // tr, d // tc), in_specs=in_specs, out_specs=[spec] * nout,
        out_shape=[jax.ShapeDtypeStruct((t, d), dt) for dt in out_dtypes],
        compiler_params=_params("parallel", "parallel"),
    )(*ins)


def _merge_fwd(proj, col_a, col_b, ya, yb, *, name):
    return _elementwise(lambda a, b, p, q: (_sigmoid(a) * p + _sigmoid(b) * q,), [proj, proj, ya, yb], [BF16],
                        name=name, width=ya.shape[1], cols=[col_a, col_b, 0, 0])[0]


def _merge_bwd(dm, proj, col_a, col_b, ya, yb, *, name):
    def fn(dmv, a, b, p, q):
        sa, sb = _sigmoid(a), _sigmoid(b)
        return dmv * p * sa * (1.0 - sa), dmv * q * sb * (1.0 - sb), dmv * sa, dmv * sb

    return _elementwise(fn, [dm, proj, proj, ya, yb], [BF16, BF16, BF16, BF16], name=name, width=ya.shape[1],
                        cols=[0, col_a, col_b, 0, 0])


def _loss_head(xo, target, *, name):
    t, d = xo.shape
    tr = _pick(t, 256)

    def body(x_ref, t_ref, dx_ref, l_ref):
        @pl.when(pl.program_id(0) == 0)
        def _():
            l_ref[...] = jnp.zeros_like(l_ref)

        err = x_ref[...] - t_ref[...]
        dx_ref[...] = err * (1.0 / d)
        l_ref[...] += 0.5 * jnp.sum(jnp.mean(err * err, axis=-1, keepdims=True), axis=0, keepdims=True)

    row = pl.BlockSpec((tr, d), lambda i: (i, 0))
    dx, l = pl.pallas_call(
        body, name=name, grid=(t // tr,), in_specs=[row, row],
        out_specs=(row, pl.BlockSpec((1, 1), lambda i: (0, 0))),
        out_shape=(jax.ShapeDtypeStruct((t, d), F32), jax.ShapeDtypeStruct((1, 1), F32)),
        compiler_params=_params("arbitrary"),
    )(xo, target)
    return dx, l[0, 0]


def _rope(xin, tabs, *, name, group, backward, out_dtype, col=0, ngroup=None):
    t = xin.shape[0]
    ngroup = xin.shape[1] // group if ngroup is None else ngroup
    wdt = ngroup * group
    tr = _pick(t, 256)
    assert col % wdt == 0
    cos_t, nsin_t, sin_t = tabs

    def body(x_ref, c_ref, n_ref, s_ref, o_ref):
        cv, nv, sv = c_ref[...], n_ref[...], s_ref[...]
        for g in range(ngroup):
            lo, hi = g * group, (g + 1) * group
            rot = x_ref[:, hi - LANE:hi].astype(F32)
            if backward:
                out = rot * cv + pltpu.roll(rot * nv, 32, 1) + pltpu.roll(rot * sv, LANE - 32, 1)
            else:
                out = rot * cv + pltpu.roll(rot, LANE - 32, 1) * nv + pltpu.roll(rot, 32, 1) * sv
            if group > LANE:
                o_ref[:, lo:hi - LANE] = x_ref[:, lo:hi - LANE].astype(out_dtype)
            o_ref[:, hi - LANE:hi] = out.astype(out_dtype)

    xspec = pl.BlockSpec((tr, wdt), lambda i: (i, 0))
    tspec = pl.BlockSpec((tr, LANE), lambda i: (i, 0))
    return pl.pallas_call(
        body, name=name, grid=(t // tr,),
        in_specs=[pl.BlockSpec((tr, wdt), lambda i: (i, col // wdt)), tspec, tspec, tspec], out_specs=xspec,
        out_shape=jax.ShapeDtypeStruct((t, wdt), out_dtype), compiler_params=_params("parallel"),
    )(xin, cos_t, nsin_t, sin_t)


def _scores(q, kv, kr, qi, tq, scale):
    kcat = jnp.concatenate([kv[:, :HEAD], kr], axis=1)
    s = lax.dot_general(q, kcat, (((1,), (1,)), ((), ())), preferred_element_type=F32) * scale
    row = qi * tq + lax.broadcasted_iota(jnp.int32, s.shape, 0)
    col = lax.broadcasted_iota(jnp.int32, s.shape, 1)
    s = jnp.where(col <= row, s, -jnp.inf)
    p = jnp.exp(s - jnp.max(s, axis=-1, keepdims=True))
    return p / jnp.sum(p, axis=-1, keepdims=True), kcat


def _attn_fwd(qcat, kv, kr, *, name, scale):
    t = qcat.shape[0]
    nh = qcat.shape[1] // QGROUP
    tq = _pick(t, 256)

    def body(q_ref, kv_ref, kr_ref, o_ref):
        for qi in range(t // tq):
            @pl.when(pl.program_id(1) == qi)
            def _(qi=qi):
                kvv = kv_ref[0:(qi + 1) * tq, :]
                p, _ = _scores(q_ref[...], kvv, kr_ref[0:(qi + 1) * tq, :], qi, tq, scale)
                o_ref[...] = jnp.dot(p.astype(BF16), kvv[:, HEAD:], preferred_element_type=F32).astype(BF16)

    return pl.pallas_call(
        body, name=name, grid=(nh, t // tq),
        in_specs=[pl.BlockSpec((tq, QGROUP), lambda h, i: (i, h)), pl.BlockSpec((t, QGROUP), lambda h, i: (0, h)),
                  pl.BlockSpec((t, LANE), lambda h, i: (0, 0))],
        out_specs=pl.BlockSpec((tq, HEAD), lambda h, i: (i, h)),
        out_shape=jax.ShapeDtypeStruct((t, nh * HEAD), BF16), compiler_params=_params("parallel", "parallel"),
    )(qcat, kv, kr)


def _attn_bwd(qcat, kv, kr, do, *, name, scale):
    t = qcat.shape[0]
    nh = qcat.shape[1] // QGROUP
    tq = _pick(t, 256)
    nq = t // tq

    def body(q_ref, kv_ref, kr_ref, do_ref, dq_ref, dkv_ref, dkr_ref, dk_acc, dv_acc):
        h, i = pl.program_id(0), pl.program_id(1)

        @pl.when(i == 0)
        def _():
            dk_acc[...] = jnp.zeros_like(dk_acc)
            dv_acc[...] = jnp.zeros_like(dv_acc)

        @pl.when((i == 0) & (h == 0))
        def _():
            dkr_ref[...] = jnp.zeros_like(dkr_ref)

        for qi in range(nq):
            @pl.when(i == qi)
            def _(qi=qi):
                keys = slice(0, (qi + 1) * tq)
                q = q_ref[...]
                kvv = kv_ref[keys, :]
                dov = do_ref[...].astype(BF16)
                p, kcat = _scores(q, kvv, kr_ref[keys, :], qi, tq, scale)
                dp = lax.dot_general(dov, kvv[:, HEAD:], (((1,), (1,)), ((), ())), preferred_element_type=F32)
                ds = (p * (dp - jnp.sum(p * dp, axis=-1, keepdims=True)) * scale).astype(BF16)
                dq_ref[...] = jnp.dot(ds, kcat, preferred_element_type=F32)
                dk_acc[keys, :] += lax.dot_general(ds, q, (((0,), (0,)), ((), ())), preferred_element_type=F32)
                dv_acc[keys, :] += lax.dot_general(p.astype(BF16), dov, (((0,), (0,)), ((), ())), preferred_element_type=F32)

        @pl.when(i == nq - 1)
        def _():
            dk = dk_acc[...]
            dkv_ref[...] = jnp.concatenate([dk[:, :HEAD], dv_acc[...]], axis=1)
            dkr_ref[...] += dk[:, HEAD:]

    return pl.pallas_call(
        body, name=name, grid=(nh, nq),
        in_specs=[pl.BlockSpec((tq, QGROUP), lambda h, i: (i, h)), pl.BlockSpec((t, QGROUP), lambda h, i: (0, h)),
                  pl.BlockSpec((t, LANE), lambda h, i: (0, 0)), pl.BlockSpec((tq, HEAD), lambda h, i: (i, h))],
        out_specs=(pl.BlockSpec((tq, QGROUP), lambda h, i: (i, h)), pl.BlockSpec((t, QGROUP), lambda h, i: (0, h)),
                   pl.BlockSpec((t, LANE), lambda h, i: (0, 0))),
        out_shape=(jax.ShapeDtypeStruct((t, nh * QGROUP), F32), jax.ShapeDtypeStruct((t, nh * QGROUP), F32),
                   jax.ShapeDtypeStruct((t, LANE), F32)),
        scratch_shapes=[pltpu.VMEM((t, QGROUP), F32), pltpu.VMEM((t, HEAD), F32)],
        compiler_params=_params("arbitrary", "arbitrary"),
    )(qcat, kv, kr, do)


def _split3(x):
    hi = x.astype(BF16)
    r1 = x - hi.astype(F32)
    mid = r1.astype(BF16)
    lo = (r1 - mid.astype(F32)).astype(BF16)
    return hi, mid, lo


def _tri_matmul(mask, x):
    m = mask.astype(BF16)
    return sum(jnp.dot(m, part, preferred_element_type=F32) for part in _split3(x))


def _sub_cumsum(g, tb):
    row = lax.broadcasted_iota(jnp.int32, (tb, tb), 0)
    col = lax.broadcasted_iota(jnp.int32, (tb, tb), 1)
    return _tri_matmul(jnp.where((col <= row) & (col // SUB == row // SUB), 1.0, 0.0), g)


def _sub_suffix_prefix(after, before, tb):
    row = lax.broadcasted_iota(jnp.int32, (tb, tb), 0)
    col = lax.broadcasted_iota(jnp.int32, (tb, tb), 1)
    same = col // SUB == row // SUB
    return (_tri_matmul(jnp.where((col >= row) & same, 1.0, 0.0), after)
            + _tri_matmul(jnp.where((col < row) & same, 1.0, 0.0), before))


def _lower_bound(logits):
    mx = jnp.max(logits, axis=0, keepdims=True)
    e = jnp.exp(logits - mx)
    return e[0:1, :] / jnp.sum(e, axis=0, keepdims=True)


def _hgrn_fwd(proj, cols, wdt, logits, out_norm, *, name):
    t = proj.shape[0]
    nh = wdt // HEAD
    tb = _pick(t, 128)
    ns = tb // SUB

    def body(hq_ref, hf_ref, hi_ref, hg_ref, lg_ref, w_ref, o_ref, yb_ref, st_ref, s_ref, q_s, k_s, b_s):
        @pl.when(pl.program_id(1) == 0)
        def _():
            s_ref[...] = jnp.zeros_like(s_ref)

        lb = _lower_bound(lg_ref[...])
        f = lb + (1.0 - lb) * _sigmoid(hf_ref[...])
        q_s[...] = _silu(hq_ref[...])
        k_s[...] = 1.0 - f
        b_s[...] = _sub_cumsum(jnp.log(f), tb)
        rowid = lax.broadcasted_iota(jnp.int32, (SUB, HEAD), 0)

        def sub(c, st):
            rows = pl.ds(pl.multiple_of(c * SUB, SUB), SUB)
            qc, kc, bc, vc = q_s[rows, :], k_s[rows, :], b_s[rows, :], hi_ref[rows, :]
            st_ref[0, c] = st
            bl = bc[SUB - 1:SUB, :]
            oc = lax.dot_general((qc * jnp.exp(bc)).astype(BF16), st.astype(BF16), (((1,), (1,)), ((), ())),
                                 preferred_element_type=F32)
            for s in range(SUB):
                e = jnp.where(rowid >= s, jnp.exp(bc - bc[s:s + 1, :]), 0.0)
                a = jnp.sum(qc * e * kc[s:s + 1, :], axis=1, keepdims=True)
                oc = oc + a * vc[s:s + 1, :]
            o_ref[rows, :] = oc
            kd = kc * jnp.exp(bl - bc)
            return jnp.exp(bl) * st + lax.dot_general(vc.astype(BF16), kd.astype(BF16), (((0,), (0,)), ((), ())),
                                                      preferred_element_type=F32)

        s_ref[...] = lax.fori_loop(0, ns, sub, s_ref[...], unroll=True)
        o = o_ref[...]
        r = lax.rsqrt(jnp.mean(o * o, axis=-1, keepdims=True) + NORM_EPS)
        yb_ref[...] = (o * r * w_ref[...] * _silu(hg_ref[...])).astype(BF16)

    blk = pl.BlockSpec((tb, HEAD), lambda h, j: (j, h))
    return pl.pallas_call(
        body, name=name, grid=(nh, t // tb),
        in_specs=[pl.BlockSpec((tb, HEAD), lambda h, j, o=c // HEAD: (j, h + o)) for c in cols]
        + [pl.BlockSpec((2, HEAD), lambda h, j: (0, h)), pl.BlockSpec((1, HEAD), lambda h, j: (0, 0))],
        out_specs=(blk, blk, pl.BlockSpec((1, ns, HEAD, HEAD), lambda h, j: (h, j, 0, 0))),
        out_shape=(jax.ShapeDtypeStruct((t, wdt), F32), jax.ShapeDtypeStruct((t, wdt), BF16),
                   jax.ShapeDtypeStruct((nh, t // SUB, HEAD, HEAD), F32)),
        scratch_shapes=[pltpu.VMEM((HEAD, HEAD), F32)] + [pltpu.VMEM((tb, HEAD), F32)] * 3,
        compiler_params=_params("parallel", "arbitrary"),
    )(proj, proj, proj, proj, logits, out_norm)


def _hgrn_bwd(proj, cols, wdt, o_raw, dyb, states, logits, out_norm, *, name):
    t = proj.shape[0]
    nh = wdt // HEAD
    tb = _pick(t, 128)
    ns = tb // SUB
    nb = t // tb

    def body(hq_ref, hf_ref, hi_ref, hg_ref, o_ref, dy_ref, st_ref, lg_ref, w_ref,
             dhq_ref, dhf_ref, dhi_ref, dhg_ref, dlb_ref, dw_ref,
             ds_ref, q_s, k_s, b_s, do_s, dq_s, dk_s, dv_s, after_s, before_s, thru_s):
        @pl.when(pl.program_id(1) == 0)
        def _():
            ds_ref[...] = jnp.zeros_like(ds_ref)
            dlb_ref[...] = jnp.zeros_like(dlb_ref)
            dw_ref[...] = jnp.zeros_like(dw_ref)

        lb = _lower_bound(lg_ref[...])
        hqv, hgv = hq_ref[...], hg_ref[...]
        sig = _sigmoid(hf_ref[...])
        f = lb + (1.0 - lb) * sig
        q_s[...] = _silu(hqv)
        k_s[...] = 1.0 - f
        b_s[...] = _sub_cumsum(jnp.log(f), tb)

        o = o_ref[...]
        r = lax.rsqrt(jnp.mean(o * o, axis=-1, keepdims=True) + NORM_EPS)
        nrm = o * r
        w = w_ref[...]
        dy = dy_ref[...].astype(F32)
        dhg_ref[...] = (dy * nrm * w * _dsilu(hgv)).astype(BF16)
        dnw = dy * _silu(hgv)
        dw_ref[0] += jnp.sum(dnw * nrm, axis=0, keepdims=True)
        tt = dnw * w
        do_s[...] = r * (tt - nrm * jnp.mean(tt * nrm, axis=-1, keepdims=True))
        rowid = lax.broadcasted_iota(jnp.int32, (SUB, HEAD), 0)

        def sub(cc, dst):
            c = ns - 1 - cc
            rows = pl.ds(pl.multiple_of(c * SUB, SUB), SUB)
            qc, kc, bc, vc, doc = q_s[rows, :], k_s[rows, :], b_s[rows, :], hi_ref[rows, :], do_s[rows, :]
            st = st_ref[0, c]
            bl = bc[SUB - 1:SUB, :]
            eb = jnp.exp(bc)
            ekd = jnp.exp(bl - bc)
            qe, kd = qc * eb, kc * ekd
            dob, vcb = doc.astype(BF16), vc.astype(BF16)
            dq_st = jnp.dot(dob, st.astype(BF16), preferred_element_type=F32) * eb
            dk_st = jnp.dot(vcb, dst.astype(BF16), preferred_element_type=F32) * ekd
            dv = lax.dot_general(kd.astype(BF16), dst.astype(BF16), (((1,), (1,)), ((), ())), preferred_element_type=F32)
            dq_in = jnp.zeros_like(qc)
            dk_in = jnp.zeros_like(qc)
            for s in range(SUB):
                e = jnp.where(rowid >= s, jnp.exp(bc - bc[s:s + 1, :]), 0.0)
                ek = e * kc[s:s + 1, :]
                a = jnp.sum(qc * ek, axis=1, keepdims=True)
                da = jnp.sum(doc * vc[s:s + 1, :], axis=1, keepdims=True)
                dq_in = dq_in + da * ek
                dk_in = dk_in + jnp.where(rowid == s, jnp.sum(da * e * qc, axis=0, keepdims=True), 0.0)
                dv = dv + jnp.where(rowid == s, jnp.sum(a * doc, axis=0, keepdims=True), 0.0)
            ebl = jnp.exp(bl)
            dq_s[rows, :] = dq_st + dq_in
            dk_s[rows, :] = dk_st + dk_in
            dv_s[rows, :] = dv
            after_s[rows, :] = qc * (dq_st + dq_in) - kc * dk_in
            before_s[rows, :] = kc * dk_st
            thru_s[rows, :] = jnp.broadcast_to(ebl * jnp.sum(st * dst, axis=0, keepdims=True), (SUB, HEAD))
            return ebl * dst + lax.dot_general(dob, qe.astype(BF16), (((0,), (0,)), ((), ())), preferred_element_type=F32)

        ds_ref[...] = lax.fori_loop(
            0, ns // 4, lambda i, dst: sub(4 * i + 3, sub(4 * i + 2, sub(4 * i + 1, sub(4 * i, dst)))), ds_ref[...])
        dg = _sub_suffix_prefix(after_s[...], before_s[...], tb) + thru_s[...]
        dhq_ref[...] = (dq_s[...] * _dsilu(hqv)).astype(BF16)
        dft = dg / f - dk_s[...]
        dhf_ref[...] = (dft * (1.0 - lb) * sig * (1.0 - sig)).astype(BF16)
        dlb_ref[0] += jnp.sum(dft * (1.0 - sig), axis=0, keepdims=True)
        dhi_ref[...] = dv_s[...].astype(BF16)

    blk = pl.BlockSpec((tb, HEAD), lambda h, j: (nb - 1 - j, h))
    vec = pl.BlockSpec((1, 1, HEAD), lambda h, j: (h, 0, 0))
    tok = jax.ShapeDtypeStruct((t, wdt), BF16)
    per_head = jax.ShapeDtypeStruct((nh, 1, HEAD), F32)
    return pl.pallas_call(
        body, name=name, grid=(nh, nb),
        in_specs=[pl.BlockSpec((tb, HEAD), lambda h, j, o=c // HEAD: (nb - 1 - j, h + o)) for c in cols]
        + [blk, blk] + [pl.BlockSpec((1, ns, HEAD, HEAD), lambda h, j: (h, nb - 1 - j, 0, 0)),
                              pl.BlockSpec((2, HEAD), lambda h, j: (0, h)), pl.BlockSpec((1, HEAD), lambda h, j: (0, 0))],
        out_specs=(blk, blk, blk, blk, vec, vec),
        out_shape=(tok, tok, tok, tok, per_head, per_head),
        scratch_shapes=[pltpu.VMEM((HEAD, HEAD), F32)] + [pltpu.VMEM((tb, HEAD), F32)] * 10,
        compiler_params=_params("arbitrary", "arbitrary"),
    )(proj, proj, proj, proj, o_raw, dyb, states, logits, out_norm)


def _lb_logits_grad(logits, dlb, *, name):
    def body(lg_ref, d_ref, o_ref):
        lg = lg_ref[...]
        e = jnp.exp(lg - jnp.max(lg, axis=0, keepdims=True))
        p = e / jnp.sum(e, axis=0, keepdims=True)
        d = d_ref[...]
        rowid = lax.broadcasted_iota(jnp.int32, lg.shape, 0)
        dp = jnp.where(rowid == 0, d, 0.0)
        o_ref[...] = p * (dp - jnp.sum(p * dp, axis=0, keepdims=True))

    return pl.pallas_call(body, name=name, out_shape=jax.ShapeDtypeStruct(logits.shape, F32))(logits, dlb)


def _adamw(w, g, m, v, *, name, deps=()):
    r, c = w.shape
    tc = _pick(c, 2048) if c % LANE == 0 else c
    tr = _row_tile(r, tc * 4)

    def body(w_ref, g_ref, m_ref, v_ref, *rest):
        d_ref, nm_ref, nv_ref = rest[-3:]
        gv = g_ref[...]
        nm = ADAM_B1 * m_ref[...] + (1.0 - ADAM_B1) * gv
        nv = ADAM_B2 * v_ref[...] + (1.0 - ADAM_B2) * (gv * gv)
        m_hat = nm / (1.0 - ADAM_B1 ** ADAM_STEP)
        v_hat = nv / (1.0 - ADAM_B2 ** ADAM_STEP)
        d_ref[...] = -ADAM_LR * (m_hat / (jnp.sqrt(v_hat) + ADAM_EPS) + ADAM_WD * w_ref[...])
        nm_ref[...] = nm
        nv_ref[...] = nv

    spec = pl.BlockSpec((tr, tc), lambda i, j: (i, j))
    shp = jax.ShapeDtypeStruct((r, c), F32)
    return pl.pallas_call(
        body, name=name, grid=(r // tr, c // tc), in_specs=[spec] * 4 + [ANY] * len(deps), out_specs=[spec] * 3,
        out_shape=[shp, shp, shp], compiler_params=_params("parallel", "parallel"),
    )(w, g, m, v, *deps)


def _coords():
    return lax.axis_index("x"), lax.axis_index("y"), lax.axis_index("c")


def _other_chips(x, y):
    return [(1 - x, y), (x, 1 - y), (1 - x, 1 - y)]


ANY = pl.BlockSpec(memory_space=pl.ANY)


class _Layout:
    def __init__(self, d, dff, in_cols, q_lora, kv_lora, nh):
        assert q_lora == kv_lora and nh % 4 == 0 and dff % (4 * LANE) == 0 and in_cols % 4 == 0 and d % 4 == 0
        self.d, self.dff, self.q_lora, self.nh = d, dff, q_lora, nh
        self.head = q_lora + kv_lora + ROPE
        self.pad = d - self.head
        self.nff, self.ncol, self.r_o, self.hps = dff // 4, in_cols // 4, d // 4, nh // 4
        assert self.head <= self.ncol
        self.off_q, self.off_kv, self.rows_narrow = 0, nh * QGROUP, 2 * nh * QGROUP


HBM = pl.BlockSpec(memory_space=pltpu.HBM)
SEMS = pl.BlockSpec(memory_space=pltpu.SEMAPHORE)
SPLIT = dict(has_side_effects=pltpu.SideEffectType.DATAFLOW_SIDE_EFFECTING)


def _in_hbm(a):
    return pltpu.with_memory_space_constraint(a, pltpu.HBM)


def _shard_rows(jobs, k):
    out, lrow = [], [0] * (1 + max(job.a for job in jobs))
    for job in jobs:
        for row, rows in job.pieces(k):
            out.append((job.a, lrow[job.a], row, rows))
            lrow[job.a] += rows
    return out


def _shard_total(jobs, a):
    return sum(rows for b, _, _, rows in _shard_rows(jobs, 0) if b == a)


def _gather_start(packs, lands, jobs, *, name, deps=()):
    n = len(packs)

    def body(*refs):
        p_refs, l_refs, send, recv, token = refs[:n], refs[n:2 * n], refs[-2 * n - 3], refs[-2 * n - 2], refs[-1]
        x, y, c = _coords()
        for a, lrow, row, rows in _shard_rows(jobs, 2 * x + y):
            pltpu.make_async_remote_copy(
                src_ref=p_refs[a].at[:, pl.ds(lrow, rows)], dst_ref=l_refs[a].at[:, pl.ds(row, rows)],
                send_sem=send.at[4 * a + 3], recv_sem=recv.at[4 * a + 3], device_id=(x, y, 1 - c), device_id_type=MESH).start()
            for j, (px, py) in enumerate(_other_chips(x, y)):
                pltpu.make_async_remote_copy(
                    src_ref=p_refs[a].at[c, pl.ds(lrow, rows)], dst_ref=l_refs[a].at[c, pl.ds(row, rows)],
                    send_sem=send.at[4 * a + j], recv_sem=recv.at[4 * a + j], device_id=(px, py, c), device_id_type=MESH).start()
        token[...] = jnp.zeros_like(token)

    thru = [pltpu.HBM(a.shape, a.dtype) for a in packs + lands]
    out = pl.pallas_call(
        body, name=name, in_specs=[HBM] * (2 * n) + [ANY] * len(deps),
        out_shape=(pltpu.SemaphoreType.DMA((4 * n,)), pltpu.SemaphoreType.DMA((4 * n,)), *thru, jax.ShapeDtypeStruct((8, LANE), F32)),
        out_specs=(SEMS, SEMS, *[HBM] * (2 * n), pl.BlockSpec(memory_space=pltpu.VMEM)),
        input_output_aliases={i: 2 + i for i in range(2 * n)}, compiler_params=pltpu.CompilerParams(**SPLIT),
    )(*[_in_hbm(a) for a in packs + lands], *deps)
    return dict(send=out[0], recv=out[1], bufs=list(out[2:2 + 2 * n]), n=n, jobs=jobs), out[-1]


def _gather_wait(handle, after, *, name):
    n, jobs = handle["n"], handle["jobs"]

    def body(*refs):
        l_refs, send, recv, token = refs[n:2 * n], refs[2 * n], refs[2 * n + 1], refs[-1]
        token[...] = jnp.zeros_like(token)
        x, y, c = _coords()
        for a in range(n):
            total = _shard_total(jobs, a)
            for j, like in enumerate([l_refs[a].at[0, pl.ds(0, total)]] * 3 + [l_refs[a].at[:, pl.ds(0, total)]]):
                cp = pltpu.make_async_remote_copy(src_ref=like, dst_ref=like, send_sem=send.at[4 * a + j],
                                                  recv_sem=recv.at[4 * a + j], device_id=(x, y, c), device_id_type=MESH)
                cp.wait_send()
                cp.wait_recv()

    out = pl.pallas_call(
        body, name=name, in_specs=[HBM] * (2 * n) + [SEMS, SEMS] + [ANY] * len(after),
        out_shape=[pltpu.HBM(a.shape, a.dtype) for a in handle["bufs"]] + [jax.ShapeDtypeStruct((8, LANE), F32)],
        out_specs=[HBM] * (2 * n) + [pl.BlockSpec(memory_space=pltpu.VMEM)],
        input_output_aliases={i: i for i in range(2 * n)}, compiler_params=pltpu.CompilerParams(**SPLIT),
    )(*handle["bufs"], handle["send"], handle["recv"], *after)
    return list(out[n:2 * n]), out[-1]


def _gather_forward(lands, jobs, *, name, deps=()):
    n = len(lands)

    def body(*refs):
        l_refs, send, recv = refs[n + len(deps):2 * n + len(deps)], refs[-2], refs[-1]
        x, y, c = _coords()
        for j, (px, py) in enumerate(_other_chips(x, y)):
            for a, _, row, rows in _shard_rows(jobs, 2 * px + py):
                blk = l_refs[a].at[c, pl.ds(row, rows)]
                pltpu.make_async_remote_copy(src_ref=blk, dst_ref=blk, send_sem=send.at[3 * a + j], recv_sem=recv.at[3 * a + j],
                                             device_id=(x, y, 1 - c), device_id_type=MESH).start()
        for a in range(n):
            like = l_refs[a].at[0, pl.ds(0, _shard_total(jobs, a))]
            for j in range(3):
                cp = pltpu.make_async_remote_copy(src_ref=like, dst_ref=like, send_sem=send.at[3 * a + j],
                                                  recv_sem=recv.at[3 * a + j], device_id=(x, y, c), device_id_type=MESH)
                cp.wait_send()
                cp.wait_recv()

    sem = pltpu.SemaphoreType.DMA((3 * n,))
    return pl.pallas_call(
        body, name=name, in_specs=[ANY] * (n + len(deps)), out_specs=[ANY] * n, input_output_aliases={i: i for i in range(n)},
        out_shape=[jax.ShapeDtypeStruct(a.shape, a.dtype) for a in lands], scratch_shapes=[sem, sem],
    )(*lands, *deps)


def _forward_start(lands, jobs, *, name, deps=()):
    n, nd = len(lands), len(deps)

    def body(*refs):
        l_refs, sems, token = refs[:n], refs[n + nd:n + nd + 2 * n], refs[-1]
        x, y, c = _coords()
        for j, (px, py) in enumerate(_other_chips(x, y)):
            for a, _, row, rows in _shard_rows(jobs, 2 * px + py):
                blk = l_refs[a].at[c, pl.ds(row, rows)]
                pltpu.make_async_remote_copy(src_ref=blk, dst_ref=blk, send_sem=sems[2 * a].at[j], recv_sem=sems[2 * a + 1].at[j],
                                             device_id=(x, y, 1 - c), device_id_type=MESH).start()
        token[...] = jnp.zeros_like(token)

    out = pl.pallas_call(
        body, name=name, in_specs=[HBM] * n + [ANY] * nd,
        out_shape=(*[pltpu.SemaphoreType.DMA((3,))] * (2 * n), *[pltpu.HBM(a.shape, a.dtype) for a in lands],
                   jax.ShapeDtypeStruct((8, LANE), F32)),
        out_specs=(*[SEMS] * (2 * n), *[HBM] * n, pl.BlockSpec(memory_space=pltpu.VMEM)),
        input_output_aliases={i: 2 * n + i for i in range(n)}, compiler_params=pltpu.CompilerParams(**SPLIT),
    )(*[_in_hbm(a) for a in lands], *deps)
    return [dict(send=out[2 * a], recv=out[2 * a + 1], buf=out[2 * n + a]) for a in range(n)], out[-1]


def _forward_wait(handle, jobs, a, after, *, name):
    total = _shard_total(jobs, a)

    def body(l_ref, send, recv, *rest):
        x, y, c = _coords()
        like = l_ref.at[0, pl.ds(0, total)]
        for j in range(3):
            cp = pltpu.make_async_remote_copy(src_ref=like, dst_ref=like, send_sem=send.at[j], recv_sem=recv.at[j],
                                              device_id=(x, y, c), device_id_type=MESH)
            cp.wait_send()
            cp.wait_recv()

    buf = handle["buf"]
    return pl.pallas_call(
        body, name=name, in_specs=[HBM, SEMS, SEMS] + [ANY] * len(after), out_shape=pltpu.HBM(buf.shape, buf.dtype),
        out_specs=HBM, input_output_aliases={0: 0}, compiler_params=pltpu.CompilerParams(**SPLIT),
    )(buf, handle["send"], handle["recv"], *after)


def _add_sibling(g, recv, sel, *, name):
    rows, hw = recv.shape
    tr = _row_tile(rows, hw * 4)

    def body(sel_ref, g_ref, r_ref, o_ref):
        o_ref[...] = (g_ref[...] + r_ref[...]).astype(BF16)

    return pl.pallas_call(
        body, name=name, out_shape=jax.ShapeDtypeStruct((rows, hw), BF16),
        grid_spec=pltpu.PrefetchScalarGridSpec(
            num_scalar_prefetch=1, grid=(rows // tr,),
            in_specs=[pl.BlockSpec((None, tr, hw), lambda i, s: (s[0], i, 0)), pl.BlockSpec((tr, hw), lambda i, s: (i, 0))],
            out_specs=pl.BlockSpec((tr, hw), lambda i, s: (i, 0))),
        compiler_params=_params("parallel"),
    )(sel, g, recv)


class _Job:
    def __init__(self, a, blk, n_outer, n_inner, stride, start):
        self.a, self.blk, self.n_outer, self.n_inner, self.stride, self.start = a, blk, n_outer, n_inner, stride, start
        self.rows_out = n_outer * n_inner * blk

    def pieces(self, k):
        return [(self.start(k) + o * self.stride * self.blk, self.n_inner * self.blk) for o in range(self.n_outer)]


def _block_rows(rows, cap, *also):
    best = None
    for b in range(16, min(rows, cap) + 1, 16):
        if rows % b == 0 and all(v % b == 0 for v in also):
            best = b
    assert best is not None, (rows, also)
    return best


def _ffn_jobs(lay):
    b = _block_rows(lay.nff, 704, lay.dff)
    return [_Job(0, b, 3, lay.nff // b, lay.dff // b, lambda k: lay.nff * k)]


def _ffn_weight_jobs(lay):
    b = _block_rows(lay.nff, 704)
    return [_Job(a, b, 1, lay.nff // b, 0, lambda k: lay.nff * k) for a in range(3)]


def _mix_jobs(lay):
    d, ncol, head, pad = lay.d, lay.ncol, lay.head, lay.pad
    first = lambda k, a, b: jnp.where(k == 0, a, b) if not isinstance(k, int) else (a if k == 0 else b)
    ba = _block_rows(head, 704, *[ncol * k + pad for k in (1, 2, 3)])
    bb = _block_rows(ncol - head, 704, *[ncol * k + d for k in (0, 1, 2, 3)])
    bo = _block_rows(lay.r_o, 704, d)
    bq = _block_rows(HEAD + ROPE, 704, QGROUP)
    bk = _block_rows(lay.hps * QGROUP, 704, lay.off_kv)
    return [_Job(0, ba, 1, head // ba, 0, lambda k: first(k, 0, ncol * k + pad)),
            _Job(0, bb, 1, (ncol - head) // bb, 0, lambda k: ncol * k + d),
            _Job(0, bo, 3, lay.r_o // bo, d // bo, lambda k: 7 * d + lay.r_o * k),
            _Job(1, bq, lay.hps, (HEAD + ROPE) // bq, QGROUP // bq, lambda k: QGROUP * lay.hps * k),
            _Job(1, bk, 1, lay.hps * QGROUP // bk, 0, lambda k: lay.off_kv + lay.hps * QGROUP * k)]


def _swap_start(gs, *, name):
    n = len(gs)
    lands = [lax.empty(g.shape[1:], g.dtype) for g in gs]

    def body(*refs):
        g_refs, land_refs, send, recv, token = refs[:n], refs[n:2 * n], refs[2 * n], refs[2 * n + 1], refs[-1]
        x, y, c = _coords()
        for a in range(n):
            pltpu.make_async_remote_copy(src_ref=g_refs[a].at[1 - c], dst_ref=land_refs[a], send_sem=send.at[a],
                                         recv_sem=recv.at[a], device_id=(x, y, 1 - c), device_id_type=MESH).start()
        token[...] = jnp.zeros_like(token)

    thru = [pltpu.HBM(a.shape, a.dtype) for a in gs + lands]
    out = pl.pallas_call(
        body, name=name, in_specs=[HBM] * (2 * n),
        out_shape=(pltpu.SemaphoreType.DMA((n,)), pltpu.SemaphoreType.DMA((n,)), *thru, jax.ShapeDtypeStruct((8, LANE), F32)),
        out_specs=(SEMS, SEMS, *[HBM] * (2 * n), pl.BlockSpec(memory_space=pltpu.VMEM)),
        input_output_aliases={i: 2 + i for i in range(2 * n)}, compiler_params=pltpu.CompilerParams(**SPLIT),
    )(*[_in_hbm(a) for a in gs + lands])
    return dict(send=out[0], recv=out[1], bufs=list(out[2:2 + 2 * n]), n=n), out[-1]


def _swap_wait(handle, after, *, name):
    n = handle["n"]

    def body(*refs):
        g_refs, land_refs, send, recv = refs[:n], refs[n:2 * n], refs[2 * n], refs[2 * n + 1]
        x, y, c = _coords()
        for a in range(n):
            cp = pltpu.make_async_remote_copy(src_ref=g_refs[a].at[1 - c], dst_ref=land_refs[a], send_sem=send.at[a],
                                              recv_sem=recv.at[a], device_id=(x, y, 1 - c), device_id_type=MESH)
            cp.wait_send()
            cp.wait_recv()

    out = pl.pallas_call(
        body, name=name, in_specs=[HBM] * (2 * n) + [SEMS, SEMS] + [ANY] * len(after),
        out_shape=[pltpu.HBM(a.shape, a.dtype) for a in handle["bufs"]], out_specs=[HBM] * (2 * n),
        input_output_aliases={i: i for i in range(2 * n)}, compiler_params=pltpu.CompilerParams(**SPLIT),
    )(*handle["bufs"], handle["send"], handle["recv"], *after)
    return list(out[:n]), list(out[n:])


def _exchange_start(ss, jobs, *, name):
    n = len(ss)
    lands = [lax.empty((3,) + s.shape, s.dtype) for s in ss]

    def body(*refs):
        s_refs, land_refs, send, recv, token = refs[:n], refs[n:2 * n], refs[2 * n], refs[2 * n + 1], refs[-1]
        x, y, c = _coords()
        for j, (px, py) in enumerate(_other_chips(x, y)):
            for job in jobs:
                for row, rows in job.pieces(2 * px + py):
                    pltpu.make_async_remote_copy(
                        src_ref=s_refs[job.a].at[pl.ds(row, rows)], dst_ref=land_refs[job.a].at[j, pl.ds(row, rows)],
                        send_sem=send.at[n * j + job.a], recv_sem=recv.at[n * j + job.a], device_id=(px, py, c),
                        device_id_type=MESH).start()
        token[...] = jnp.zeros_like(token)

    thru = [pltpu.HBM(a.shape, a.dtype) for a in ss + lands]
    out = pl.pallas_call(
        body, name=name, in_specs=[HBM] * (2 * n),
        out_shape=(pltpu.SemaphoreType.DMA((3 * n,)), pltpu.SemaphoreType.DMA((3 * n,)), *thru, jax.ShapeDtypeStruct((8, LANE), F32)),
        out_specs=(SEMS, SEMS, *[HBM] * (2 * n), pl.BlockSpec(memory_space=pltpu.VMEM)),
        input_output_aliases={i: 2 + i for i in range(2 * n)}, compiler_params=pltpu.CompilerParams(**SPLIT),
    )(*[_in_hbm(a) for a in ss + lands])
    return dict(send=out[0], recv=out[1], bufs=list(out[2:2 + 2 * n]), n=n, jobs=jobs), out[-1]


def _exchange_wait(handle, after, *, name):
    n, jobs = handle["n"], handle["jobs"]
    total = [sum(rows for job in jobs if job.a == a for _, rows in job.pieces(0)) for a in range(n)]

    def body(*refs):
        s_refs, land_refs, send, recv = refs[:n], refs[n:2 * n], refs[2 * n], refs[2 * n + 1]
        x, y, c = _coords()
        for a in range(n):
            for j in range(3):
                all_rows = land_refs[a].at[0, pl.ds(0, total[a])]
                cp = pltpu.make_async_remote_copy(src_ref=all_rows, dst_ref=all_rows, send_sem=send.at[n * j + a],
                                                  recv_sem=recv.at[n * j + a], device_id=(x, y, c), device_id_type=MESH)
                cp.wait_send()
                cp.wait_recv()

    out = pl.pallas_call(
        body, name=name, in_specs=[HBM] * (2 * n) + [SEMS, SEMS] + [ANY] * len(after),
        out_shape=[pltpu.HBM(a.shape, a.dtype) for a in handle["bufs"]], out_specs=[HBM] * (2 * n),
        input_output_aliases={i: i for i in range(2 * n)}, compiler_params=pltpu.CompilerParams(**SPLIT),
    )(*handle["bufs"], handle["send"], handle["recv"], *after)
    return list(out[:n]), list(out[n:])


def _add_shard(s, land, job, sel, k, *, name):
    hw = s.shape[1]
    blk, no, ni, stride = job.blk, job.n_outer, job.n_inner, job.stride
    scal = jnp.stack([sel, job.start(k) // blk]).astype(jnp.int32)

    def body(sc_ref, own_ref, r_ref, o_ref):
        o_ref[...] = ((own_ref[...].astype(F32) + r_ref[0].astype(F32)) + r_ref[1].astype(F32)) + r_ref[2].astype(F32)

    return pl.pallas_call(
        body, name=name, out_shape=jax.ShapeDtypeStruct((2, job.rows_out, hw), F32),
        grid_spec=pltpu.PrefetchScalarGridSpec(
            num_scalar_prefetch=1, grid=(no, ni),
            in_specs=[pl.BlockSpec((blk, hw), lambda o, b, sc: (sc[1] + o * stride + b, 0)),
                      pl.BlockSpec((3, blk, hw), lambda o, b, sc: (0, sc[1] + o * stride + b, 0))],
            out_specs=pl.BlockSpec((None, blk, hw), lambda o, b, sc: (sc[0], o * ni + b, 0))),
        compiler_params=_params("parallel", "parallel"),
    )(scal, s, land)


def _join_list(fs, *, name):
    n = len(fs)

    def body(*refs):
        f_refs, send_sems, recv_sems = refs[n:2 * n], refs[2 * n], refs[2 * n + 1]
        x, y, c = _coords()
        copies = [pltpu.make_async_remote_copy(
            src_ref=f.at[c], dst_ref=f.at[c], send_sem=send_sems.at[a], recv_sem=recv_sems.at[a],
            device_id=(x, y, 1 - c), device_id_type=MESH) for a, f in enumerate(f_refs)]
        for cp in copies:
            cp.start()
        for cp in copies:
            cp.wait()

    sem = pltpu.SemaphoreType.DMA((n,))
    return pl.pallas_call(
        body, name=name, in_specs=[ANY] * n, out_specs=[ANY] * n, input_output_aliases={i: i for i in range(n)},
        out_shape=[jax.ShapeDtypeStruct(f.shape, f.dtype) for f in fs], scratch_shapes=[sem, sem],
    )(*fs)


def _all_reduce_small(vec, *, name):
    n = vec.shape[1]

    def body(v_ref, o_ref, buf, send_sems, recv_sems):
        x, y, c = _coords()
        me = 4 * x + 2 * y + c
        buf[me] = v_ref[...]
        copies = []
        for m in range(1, 8):
            peer = (x ^ ((m >> 2) & 1), y ^ ((m >> 1) & 1), c ^ (m & 1))
            copies.append(pltpu.make_async_remote_copy(
                src_ref=v_ref, dst_ref=buf.at[me], send_sem=send_sems.at[m - 1], recv_sem=recv_sems.at[m - 1],
                device_id=peer, device_id_type=MESH))
        for cp in copies:
            cp.start()
        for cp in copies:
            cp.wait()
        acc = buf[0]
        for d in range(1, 8):
            acc = acc + buf[d]
        o_ref[...] = acc

    return pl.pallas_call(
        body, name=name, out_shape=jax.ShapeDtypeStruct((1, n), F32),
        in_specs=[pl.BlockSpec(memory_space=pltpu.VMEM)], out_specs=pl.BlockSpec(memory_space=pltpu.VMEM),
        scratch_shapes=[pltpu.VMEM((8, 1, n), F32), pltpu.SemaphoreType.DMA((7,)), pltpu.SemaphoreType.DMA((7,))],
    )(vec)


def _ffn_fwd(x, n_pre, n_post, weight, lay, tag):
    h = _norm_fwd(x, n_pre, name=f"{tag}_norm_pre", out_dtype=BF16)
    wg = (weight(0, [h]), 0, lay.dff)
    g = _mm([(h, wg)], name=f"{tag}_gate", mode="nt")
    wu = (weight(1, [g]), 0, lay.dff)
    u, a = _mm([(h, wu)], name=f"{tag}_up", mode="nt", extras=[g], out_dtypes=[F32, BF16], tm_cap=MM_TILE // 2,
               epilogue=lambda up, gate: (up, _silu(gate) * up))
    wd = (weight(2, [u]), 0, lay.dff)
    yv = _mm([(a, wd)], name=f"{tag}_down", mode="nn")
    out = _norm_fwd(yv, n_post, name=f"{tag}_norm_post", resid=x, scale=MACARON_SCALE)
    return out, (x, h, g, u, a, yv), (wg, wu, wd)


def _ffn_bwd(dout, saved, n_pre, n_post, weights, lay, tag, deps=(), after_act=None, after_dw=None):
    x, h, g, u, a, yv = saved
    dff = lay.dff
    gbuf = lax.empty((2, 3 * dff, lay.d // 2), F32)
    dy, dn_post = _norm_bwd(yv, n_post, dout, name=f"{tag}_norm_post_bwd", scale=MACARON_SCALE)
    wg, wu, wd = weights
    dg, du = _mm([(dy, wd)], name=f"{tag}_down_dx", mode="nt", deps=deps, extras=[g, u],
                 out_dtypes=[BF16, BF16], tm_cap=MM_TILE // 2,
                 epilogue=lambda da, gate, up: (da * up * _dsilu(gate), da * _silu(gate)))
    deps = after_act(du) if after_act is not None else ()
    gbuf = _mm([(a, dy)], name=f"{tag}_down_dw", mode="tn", into=(gbuf, 2 * dff), deps=deps)
    gbuf = _mm([(dg, h)], name=f"{tag}_gate_dw", mode="tn", into=(gbuf, 0))
    gbuf = _mm([(du, h)], name=f"{tag}_up_dw", mode="tn", into=(gbuf, dff))
    deps = after_dw(gbuf)
    dh = _mm([(dg, wg), (du, wu)], name=f"{tag}_up_dx", mode="nn", deps=deps)
    dx, dn_pre = _norm_bwd(x, n_pre, dh, name=f"{tag}_norm_pre_bwd", dres=dout)
    return dx, dn_pre, dn_post


def _rope_tables(positions):
    half = ROPE // 2
    inv_freq = ROPE_THETA ** (-jnp.arange(half, dtype=F32) / half)
    ang = positions.astype(F32)[:, None] * inv_freq
    cos, sin = jnp.cos(ang), jnp.sin(ang)
    z = jnp.zeros_like(cos)
    z2 = jnp.zeros((positions.shape[0], LANE - ROPE), F32)
    return (jnp.concatenate([cos, cos, z2], axis=1), jnp.concatenate([-sin, z, z2], axis=1),
            jnp.concatenate([z, sin, z2], axis=1))


def kernel(x, positions, ffn1_norm_pre, ffn1_w_gate, ffn1_w_up, ffn1_w_down, ffn1_norm_post, mix_norm_pre, w_in, mla_q_norm, mla_w_q_up, mla_kv_norm, mla_w_kv_up, mla_w_o, hgrn_lb_logits, hgrn_out_norm, hgrn_w_o, w_out, mix_norm_post, ffn2_norm_pre, ffn2_w_gate, ffn2_w_up, ffn2_w_down, ffn2_norm_post, loss_target, m_ffn1_norm_pre, m_ffn1_w_gate, m_ffn1_w_up, m_ffn1_w_down, m_ffn1_norm_post, m_mix_norm_pre, m_w_in, m_mla_q_norm, m_mla_w_q_up, m_mla_kv_norm, m_mla_w_kv_up, m_mla_w_o, m_hgrn_lb_logits, m_hgrn_out_norm, m_hgrn_w_o, m_w_out, m_mix_norm_post, m_ffn2_norm_pre, m_ffn2_w_gate, m_ffn2_w_up, m_ffn2_w_down, m_ffn2_norm_post, v_ffn1_norm_pre, v_ffn1_w_gate, v_ffn1_w_up, v_ffn1_w_down, v_ffn1_norm_post, v_mix_norm_pre, v_w_in, v_mla_q_norm, v_mla_w_q_up, v_mla_kv_norm, v_mla_w_kv_up, v_mla_w_o, v_hgrn_lb_logits, v_hgrn_out_norm, v_hgrn_w_o, v_w_out, v_mix_norm_post, v_ffn2_norm_pre, v_ffn2_w_gate, v_ffn2_w_up, v_ffn2_w_down, v_ffn2_norm_post):
    given = dict(locals())
    wts = {n: given[n] for n in ALL_WEIGHTS}
    mom = {n: given["m_" + n] for n in ALL_WEIGHTS}
    var = {n: given["v_" + n] for n in ALL_WEIGHTS}
    xin = x[0]
    target = loss_target[0]
    t, d = xin.shape
    cx, cy, cc = _coords()

    q_lora, kv_lora = mla_q_norm.shape[1], mla_kv_norm.shape[1]
    nh_mla = 4 * mla_w_kv_up.shape[2] // QGROUP
    lay = _Layout(d, 4 * ffn1_w_gate.shape[2], 4 * w_in.shape[2], q_lora, kv_lora, nh_mla)
    jobs_ffn, jobs_mix = _ffn_jobs(lay), _mix_jobs(lay)
    def pack(src, col_sharded, row_sharded=()):
        a = jnp.concatenate([src[n][0].T.astype(BF16) for n in col_sharded] + [src[n][0].astype(BF16) for n in row_sharded])
        return a.reshape(a.shape[0], 2, a.shape[1] // 2).transpose(1, 0, 2)

    jobs_w = _ffn_weight_jobs(lay)
    ffn_packs = lambda src, tag: [pack(src, [f"{tag}_w_gate"]), pack(src, [f"{tag}_w_up"]), pack(src, [], [f"{tag}_w_down"])]
    ffn_lands = lambda: [lax.empty((2, lay.dff, d // 2), BF16) for _ in range(3)]

    def handed_over(handles, tag):
        return lambda i, after: _forward_wait(handles[i], jobs_w, i, after, name=f"gather_{tag}_forward_wait_{i}")

    got1, tok = _gather_start(ffn_packs(wts, "ffn1"), ffn_lands(), jobs_w, name="gather_ffn1")
    later, _ = lax.optimization_barrier(({n: wts[n] for n in BIG_WEIGHTS if not n.startswith("ffn1")}, tok))
    packs_mix = [pack(later, ["w_in"], ["mla_w_o", "hgrn_w_o", "w_out"]), pack(later, ["mla_w_q_up", "mla_w_kv_up"])]
    packs_ffn2 = ffn_packs(later, "ffn2")
    lands_mix = [jnp.zeros((2, 10 * d, d // 2), BF16), jnp.zeros((2, lay.rows_narrow, q_lora // 2), BF16)]
    arrived, tok = _gather_wait(got1, packs_mix + packs_ffn2 + lands_mix, name="gather_ffn1_wait")
    got_m, tok = _gather_start(packs_mix, lands_mix, jobs_mix, name="gather_mix", deps=[tok])
    handing1, _ = _forward_start(arrived[:1], jobs_w[:1], name="gather_ffn1_forward_gate", deps=[tok])

    def ffn1_weight(i, after):
        if i == 0:
            gate = _forward_wait(handing1[0], jobs_w, 0, after, name="gather_ffn1_forward_wait_0")
            handing1.extend(_forward_start(arrived[1:], jobs_w[:2], name="gather_ffn1_forward_rest", deps=[gate])[0])
            return gate
        return _forward_wait(handing1[i], jobs_w, 0, after, name=f"gather_ffn1_forward_wait_{i}")
    col_kr = q_lora + kv_lora
    hgrn_cols = [d, 2 * d, 3 * d, 4 * d]
    col_ga, col_gb = 5 * d, 6 * d
    tabs = _rope_tables(positions[0])
    scale = (HEAD + ROPE) ** -0.5

    x1, saved1, w_ffn1 = _ffn_fwd(xin, ffn1_norm_pre, ffn1_norm_post, ffn1_weight, lay, "ffn1")

    arrived, tok = _gather_wait(got_m, [x1], name="gather_mix_wait")
    got2, tok = _gather_start(packs_ffn2, ffn_lands(), jobs_w, name="gather_ffn2", deps=[tok])
    wide, narrow = _gather_forward(arrived, jobs_mix, name="gather_mix_forward", deps=[tok])
    w_in_v = (wide, 0, 7 * d)
    w_o_v = {n: (wide, (7 + i) * d, d) for i, n in enumerate(("mla_w_o", "hgrn_w_o", "w_out"))}
    w_q_v = (narrow, lay.off_q, nh_mla * QGROUP)
    w_kv_v = (narrow, lay.off_kv, nh_mla * QGROUP)

    h2 = _norm_fwd(x1, mix_norm_pre, name="mix_norm_pre", out_dtype=BF16)
    proj = _mm([(h2, w_in_v)], name="mix_in", mode="nt", deps=[tok])
    cqn = _norm_fwd(proj, mla_q_norm, name="mla_q_norm", out_dtype=BF16, col=0)
    ckvn = _norm_fwd(proj, mla_kv_norm, name="mla_kv_norm", out_dtype=BF16, col=q_lora)
    qp = _mm([(cqn, w_q_v)], name="mla_q_up", mode="nt")
    kvb = _mm([(ckvn, w_kv_v)], name="mla_kv_up", mode="nt", out_dtype=BF16)
    qcat = _rope(qp, tabs, name="rope_q", group=QGROUP, backward=False, out_dtype=BF16)
    krot = _rope(proj, tabs, name="rope_k", group=LANE, backward=False, out_dtype=BF16, col=col_kr, ngroup=1)
    o_mla = _attn_fwd(qcat, kvb, krot, name="mla_attention", scale=scale)
    y_a = _mm([(o_mla, w_o_v["mla_w_o"])], name="mla_out", mode="nn")

    o_raw, yb, states = _hgrn_fwd(proj, hgrn_cols, d, hgrn_lb_logits, hgrn_out_norm, name="hgrn_scan")
    handing2, tok = _forward_start(_gather_wait(got2, [o_raw], name="gather_ffn2_wait")[0], jobs_w, name="gather_ffn2_forward")
    y_b = _mm([(yb, w_o_v["hgrn_w_o"])], name="hgrn_out", mode="nn", deps=[tok])

    merged = _merge_fwd(proj, col_ga, col_gb, y_a, y_b, name="mix_merge")
    y_mix = _mm([(merged, w_o_v["w_out"])], name="mix_out", mode="nn")
    x2 = _norm_fwd(y_mix, mix_norm_post, name="mix_norm_post", resid=x1, scale=1.0)

    x3, saved2, w_ffn2 = _ffn_fwd(x2, ffn2_norm_pre, ffn2_norm_post, handed_over(handing2, "ffn2"), lay, "ffn2")
    dx3, loss_local = _loss_head(x3, target, name="loss_head")

    grads, deltas, new_m, new_v = {}, {}, {}, {}
    sel = cc.astype(jnp.int32)
    sel1 = jnp.reshape(sel, (1,))
    me_chip = (2 * cx + cy).astype(jnp.int32)

    def reduce_mid(handle, after, jobs, tag):
        bufs, recvd = _swap_wait(handle, after, name=f"grad_swap_{tag}_wait")
        sums = [_add_sibling(b, r, sel1, name=f"grad_add_sibling_{tag}_{i}") for i, (b, r) in enumerate(zip(bufs, recvd))]
        return _exchange_start(sums, jobs, name=f"grad_exchange_{tag}")

    def reduce_end(handle, after, tag):
        sums, lands = _exchange_wait(handle, after, name=f"grad_exchange_{tag}_wait")
        parts = [_add_shard(sums[job.a], lands[job.a], job, sel, me_chip, name=f"grad_add_chips_{tag}_{i}")
                 for i, job in enumerate(handle["jobs"])]
        return _join_list(parts, name=f"grad_join_{tag}")

    def natural(part, lo, rows, transposed):
        g_n = part[:, lo:lo + rows]
        hw_n = g_n.shape[2]
        return g_n.transpose(0, 2, 1).reshape(2 * hw_n, rows) if transposed else g_n.transpose(1, 0, 2).reshape(rows, 2 * hw_n)

    def adam(names, deps=()):
        for i, n in enumerate(names):
            shp = wts[n].shape
            two_d = (lambda a: a[0]) if n in BIG_WEIGHTS else (lambda a: a)
            dl, nm, nv = _adamw(two_d(wts[n]), grads[n], two_d(mom[n]), two_d(var[n]), name=f"adamw_{n}",
                                deps=deps if i == 0 else ())
            grads[n] = grads[n].reshape(shp)
            deltas[n], new_m[n], new_v[n] = dl.reshape(shp), nm.reshape(shp), nv.reshape(shp)
        return [deltas[n] for n in names]

    def ffn_grads(joined, tag, deps=()):
        nff = lay.nff
        grads[f"{tag}_w_gate"] = natural(joined[0], 0, nff, True)
        grads[f"{tag}_w_up"] = natural(joined[0], nff, nff, True)
        grads[f"{tag}_w_down"] = natural(joined[0], 2 * nff, nff, False)
        return adam([f"{tag}_w_gate", f"{tag}_w_up", f"{tag}_w_down"], deps)

    swaps = {}

    def start_swap(tag):
        def hook(gbuf):
            swaps[tag], started = _swap_start([gbuf], name=f"grad_swap_{tag}")
            return [started]
        return hook

    dx2, grads["ffn2_norm_pre"], grads["ffn2_norm_post"] = _ffn_bwd(
        dx3, saved2, ffn2_norm_pre, ffn2_norm_post, w_ffn2, lay, "ffn2", after_dw=start_swap("ffn2"))

    gwide = lax.empty((2, 10 * d, d // 2), F32)
    gnarrow = lax.empty((2, lay.rows_narrow, q_lora // 2), F32)
    dy_mix, grads["mix_norm_post"] = _norm_bwd(y_mix, mix_norm_post, dx2, name="mix_norm_post_bwd")
    dmerged = _mm([(dy_mix, w_o_v["w_out"])], name="mix_out_dx", mode="nt")
    gwide = _mm([(merged, dy_mix)], name="mix_out_dw", mode="tn", into=(gwide, 9 * d))
    dga, dgb, dy_a, dy_b = _merge_bwd(dmerged, proj, col_ga, col_gb, y_a, y_b, name="mix_merge_bwd")

    do_mla = _mm([(dy_a, w_o_v["mla_w_o"])], name="mla_out_dx", mode="nt")
    gwide = _mm([(o_mla, dy_a)], name="mla_out_dw", mode="tn", into=(gwide, 7 * d))
    dqcat, dkv, dkr = _attn_bwd(qcat, kvb, krot, do_mla, name="mla_attention_bwd", scale=scale)
    exch2, tok = reduce_mid(swaps["ffn2"], [dkr], _ffn_jobs(lay), "ffn2")

    dqp = _rope(dqcat, tabs, name="rope_q_bwd", group=QGROUP, backward=True, out_dtype=BF16)
    dk_r = _rope(dkr, tabs, name="rope_k_bwd", group=LANE, backward=True, out_dtype=BF16)
    dcqn = _mm([(dqp, w_q_v)], name="mla_q_up_dx", mode="nn", deps=[tok])
    gnarrow = _mm([(dqp, cqn)], name="mla_q_up_dw", mode="tn", into=(gnarrow, lay.off_q))
    dkvb = dkv.astype(BF16)
    dckvn = _mm([(dkvb, w_kv_v)], name="mla_kv_up_dx", mode="nn")
    gnarrow = _mm([(dkvb, ckvn)], name="mla_kv_up_dw", mode="tn", into=(gnarrow, lay.off_kv))
    dc_q, grads["mla_q_norm"] = _norm_bwd(proj, mla_q_norm, dcqn, name="mla_q_norm_bwd", col=0, dx_dtype=BF16)
    dc_kv, grads["mla_kv_norm"] = _norm_bwd(proj, mla_kv_norm, dckvn, name="mla_kv_norm_bwd", col=q_lora, dx_dtype=BF16)

    dyb = _mm([(dy_b, w_o_v["hgrn_w_o"])], name="hgrn_out_dx", mode="nt")
    gwide = _mm([(yb, dy_b)], name="hgrn_out_dw", mode="tn", into=(gwide, 8 * d))
    dhq, dhf, dhi, dhg, dlb_h, dnorm_h = _hgrn_bwd(proj, hgrn_cols, d, o_raw, dyb, states, hgrn_lb_logits, hgrn_out_norm,
                                                   name="hgrn_scan_bwd")

    dhead = jnp.concatenate([dc_q, dc_kv, dk_r, jnp.zeros((t, d - col_kr - LANE), BF16)], axis=1)
    dparts = [dhead, dhq, dhf, dhi, dhg, dga, dgb]
    dh2 = _mm([(p, (wide, i * d, d)) for i, p in enumerate(dparts)], name="mix_in_dx", mode="nn")
    for i, p in enumerate(dparts):
        gwide = _mm([(p, h2)], name=f"mix_in_dw_{i}", mode="tn", into=(gwide, i * d))
    dx1, grads["mix_norm_pre"] = _norm_bwd(x1, mix_norm_pre, dh2, name="mix_norm_pre_bwd", dres=dx2)
    swap_m, tok = _swap_start([gwide, gnarrow], name="grad_swap_mix")
    joined2 = reduce_end(exch2, [dx1], "ffn2")

    exchanges = {}

    def mix_exchange(after):
        exchanges["mix"], started = reduce_mid(swap_m, [after], _mix_jobs(lay), "mix")
        return [started]

    dx0, grads["ffn1_norm_pre"], grads["ffn1_norm_post"] = _ffn_bwd(
        dx1, saved1, ffn1_norm_pre, ffn1_norm_post, w_ffn1, lay, "ffn1", deps=[tok], after_act=mix_exchange,
        after_dw=start_swap("ffn1"))
    exch1, tok = reduce_mid(swaps["ffn1"], [dx0], _ffn_jobs(lay), "ffn1")

    joined_m = reduce_end(exchanges["mix"], [dx0, tok], "mix")
    done = ffn_grads(joined2, "ffn2")
    grads["w_in"] = natural(jnp.concatenate([joined_m[0], joined_m[1]], axis=1), 0, lay.ncol, True)
    for i, n in enumerate(("mla_w_o", "hgrn_w_o", "w_out")):
        grads[n] = natural(joined_m[2], i * lay.r_o, lay.r_o, False)
    grads["mla_w_q_up"] = natural(joined_m[3], 0, lay.hps * (HEAD + ROPE), True)
    grads["mla_w_kv_up"] = natural(joined_m[4], 0, lay.hps * QGROUP, True)
    done += adam(["w_in", "mla_w_q_up", "mla_w_kv_up", "mla_w_o", "hgrn_w_o", "w_out"])

    joined1 = reduce_end(exch1, done, "ffn1")

    dlb = dlb_h.reshape(1, -1)
    dnorm = jnp.sum(dnorm_h, axis=0)
    small = {**{n: grads[n] for n in SMALL_WEIGHTS if n not in ("hgrn_lb_logits", "hgrn_out_norm")},
             "hgrn_lb_logits": dlb, "hgrn_out_norm": dnorm}
    vec, _ = lax.optimization_barrier((jnp.concatenate([small[n] for n in SMALL_WEIGHTS], axis=1), joined1[0]))
    vec = _all_reduce_small(vec, name="grad_all_reduce_small")
    off = 0
    for n in SMALL_WEIGHTS:
        w_n = small[n].shape[1]
        grads[n] = vec[:, off:off + w_n]
        off += w_n
    grads["hgrn_lb_logits"] = _lb_logits_grad(hgrn_lb_logits, grads["hgrn_lb_logits"], name="lb_logits_grad")

    adam(list(SMALL_WEIGHTS))
    ffn_grads(joined1, "ffn1")

    loss = lax.psum(loss_local, ("x", "y", "c"))
    dx_out = dx0.reshape(x.shape)
    return (loss, dx_out, *[grads[n] for n in ALL_WEIGHTS], *[deltas[n] for n in ALL_WEIGHTS],
            *[new_m[n] for n in ALL_WEIGHTS], *[new_v[n] for n in ALL_WEIGHTS])
```

```python
import jax
import jax.numpy as jnp
from jax import lax
from jax.experimental import pallas as pl
from jax.experimental.pallas import tpu as pltpu

F32 = jnp.float32
BF16 = jnp.bfloat16
MESH = pl.DeviceIdType.MESH

NORM_EPS = 1e-6
MACARON_SCALE = 0.5
ROPE_THETA = 10000.0
HEAD = 128
ROPE = 64
QGROUP = 2 * HEAD
SUB = 16
ADAM_LR, ADAM_B1, ADAM_B2, ADAM_EPS, ADAM_WD, ADAM_STEP = 0.001, 0.9, 0.999, 1e-08, 0.01, 10

LANE = 128
VMEM_LIMIT = 48 * 1024 * 1024
MM_TILE = 1024
MM_TILE_WIDE = 1536

BIG_WEIGHTS = ("ffn1_w_gate", "ffn1_w_up", "ffn1_w_down", "w_in", "mla_w_q_up", "mla_w_kv_up",
               "mla_w_o", "hgrn_w_o", "w_out", "ffn2_w_gate", "ffn2_w_up", "ffn2_w_down")
SMALL_WEIGHTS = ("ffn1_norm_pre", "ffn1_norm_post", "mix_norm_pre", "mla_q_norm", "mla_kv_norm",
                 "hgrn_lb_logits", "hgrn_out_norm", "mix_norm_post", "ffn2_norm_pre", "ffn2_norm_post")
ALL_WEIGHTS = ("ffn1_norm_pre", "ffn1_w_gate", "ffn1_w_up", "ffn1_w_down", "ffn1_norm_post", "mix_norm_pre",
               "w_in", "mla_q_norm", "mla_w_q_up", "mla_kv_norm", "mla_w_kv_up", "mla_w_o", "hgrn_lb_logits",
               "hgrn_out_norm", "hgrn_w_o", "w_out", "mix_norm_post", "ffn2_norm_pre", "ffn2_w_gate",
               "ffn2_w_up", "ffn2_w_down", "ffn2_norm_post")


def _params(*sem):
    return pltpu.CompilerParams(dimension_semantics=sem or None, vmem_limit_bytes=VMEM_LIMIT)


def _pick(n, cap, offset=0):
    if n <= cap and offset % n == 0:
        return n
    best = None
    for t in range(LANE, min(n, cap) + 1, LANE):
        if n % t == 0 and offset % t == 0:
            best = t
    assert best is not None, (n, cap, offset)
    return best


def _row_tile(n, row_bytes, budget=2 << 20):
    best = None
    for t in range(8, n + 1, 8):
        if n % t == 0 and t * row_bytes <= budget:
            best = t
    return n if best is None else best


def _sigmoid(x):
    return 1.0 / (1.0 + jnp.exp(-x))


def _silu(x):
    return x * _sigmoid(x)


def _dsilu(x):
    s = _sigmoid(x)
    return s * (1.0 + x * (1.0 - s))


def _mm(pairs, *, name, mode="nn", out_dtype=F32, into=None, deps=(), extras=(), epilogue=None, out_dtypes=None, tm_cap=None):
    halves = isinstance(pairs[0][1], tuple)
    assert halves or mode == "tn"
    pairs = [(a, b if halves else (b, 0, b.shape[0])) for a, b in pairs]
    a0, (b0, b_off, b_rows) = pairs[0]
    hw = b0.shape[2] if halves else (into[0].shape[2] if into is not None else None)
    if mode == "nn":
        (m, kdim), n = a0.shape, 2 * hw
    elif mode == "nt":
        (m, kdim), n = a0.shape, b_rows
        assert kdim == 2 * hw
    else:
        (kdim, m), n = a0.shape, b0.shape[1]
    out_off = 0 if into is None else into[1]
    tm = _pick(m, tm_cap or (MM_TILE_WIDE if mode == "tn" else MM_TILE), out_off)
    tn = hw if (mode == "nn" or into is not None) else _pick(n, MM_TILE_WIDE, b_off if mode == "nt" else 0)
    tk = hw if mode == "nt" else _pick(kdim, MM_TILE if len(pairs) <= 2 else MM_TILE // 2, b_off if mode == "nn" else 0)
    assert n % tn == 0 and kdim % tk == 0
    nk = kdim // tk
    npair = len(pairs)
    dims = {"nn": (((1,), (0,)), ((), ())), "nt": (((1,), (1,)), ((), ())), "tn": (((0,), (0,)), ((), ()))}[mode]

    nout = 1 if epilogue is None else len(out_dtypes)

    def body(*refs):
        ins, x_refs = refs[:2 * npair], refs[2 * npair:2 * npair + len(extras)]
        o_refs, acc_ref = refs[-1 - nout:-1], refs[-1]
        k = pl.program_id(2)

        @pl.when(k == 0)
        def _():
            acc_ref[...] = jnp.zeros_like(acc_ref)

        for p in range(npair):
            a = ins[2 * p][...].astype(BF16)
            b = ins[2 * p + 1][...].astype(BF16)
            acc_ref[...] += lax.dot_general(a, b, dims, preferred_element_type=F32)

        @pl.when(k == nk - 1)
        def _():
            outs = (acc_ref[...],) if epilogue is None else epilogue(acc_ref[...], *[x[...] for x in x_refs])
            for o_ref, o in zip(o_refs, outs):
                o_ref[...] = o.astype(o_ref.dtype)

    a_spec = pl.BlockSpec((tk, tm), lambda i, j, k: (k, i)) if mode == "tn" else pl.BlockSpec((tm, tk), lambda i, j, k: (i, k))
    in_specs, flat = [], []
    for a, (b, off, _) in pairs:
        if mode == "nt":
            b_spec = pl.BlockSpec((None, tn, tk), lambda i, j, k, o=off // tn: (k, j + o, 0))
        elif mode == "nn":
            b_spec = pl.BlockSpec((None, tk, tn), lambda i, j, k, o=off // tk: (j, k + o, 0))
        else:
            b_spec = pl.BlockSpec((tk, tn), lambda i, j, k: (k, j))
        in_specs += [a_spec, b_spec]
        flat += [a, b]
    for extra in extras:
        in_specs.append(pl.BlockSpec((tm, tn), lambda i, j, k: (i, j)))
        flat.append(extra)
    for dep in deps:
        in_specs.append(pl.BlockSpec(memory_space=pl.ANY))
        flat.append(dep)
    if epilogue is not None:
        assert into is None
        out_shape, aliases = [jax.ShapeDtypeStruct((m, n), dt) for dt in out_dtypes], {}
        out_spec = [pl.BlockSpec((tm, tn), lambda i, j, k: (i, j))] * nout
    elif into is None:
        out_shape, aliases = jax.ShapeDtypeStruct((m, n), out_dtype), {}
        out_spec = pl.BlockSpec((tm, tn), lambda i, j, k: (i, j))
    else:
        out_shape, aliases = jax.ShapeDtypeStruct(into[0].shape, into[0].dtype), {len(flat): 0}
        out_spec = pl.BlockSpec((None, tm, tn), lambda i, j, k, o=out_off // tm: (j, i + o, 0))
        in_specs.append(pl.BlockSpec(memory_space=pl.ANY))
        flat.append(into[0])
    return pl.pallas_call(
        body, name=name, grid=(m // tm, n // tn, nk),
        in_specs=in_specs,
        out_specs=out_spec,
        out_shape=out_shape, input_output_aliases=aliases,
        scratch_shapes=[pltpu.VMEM((tm, tn), F32)],
        compiler_params=_params("parallel", "parallel", "arbitrary"),
    )(*flat)


def _norm_fwd(y, w, *, name, resid=None, scale=1.0, out_dtype=F32, col=0):
    t, d = y.shape[0], w.shape[1]
    tr = _pick(t, 256)
    assert col % d == 0

    def body(*refs):
        if resid is None:
            y_ref, w_ref, o_ref = refs
        else:
            y_ref, w_ref, r_ref, o_ref = refs
        yv = y_ref[...]
        out = yv * lax.rsqrt(jnp.mean(yv * yv, axis=-1, keepdims=True) + NORM_EPS) * w_ref[...]
        if resid is not None:
            out = r_ref[...] + scale * out
        o_ref[...] = out.astype(out_dtype)

    row = pl.BlockSpec((tr, d), lambda i: (i, 0))
    wspec = pl.BlockSpec((1, d), lambda i: (0, 0))
    ins, specs = [y, w], [pl.BlockSpec((tr, d), lambda i: (i, col // d)), wspec]
    if resid is not None:
        ins.append(resid)
        specs.append(row)
    return pl.pallas_call(
        body, name=name, grid=(t // tr,), in_specs=specs, out_specs=row,
        out_shape=jax.ShapeDtypeStruct((t, d), out_dtype), compiler_params=_params("parallel"),
    )(*ins)


def _norm_bwd(x, w, dy, *, name, scale=1.0, dres=None, col=0, dx_dtype=F32):
    t, d = x.shape[0], w.shape[1]
    tr = _pick(t, 256)
    assert col % d == 0

    def body(*refs):
        if dres is None:
            x_ref, w_ref, dy_ref, dx_ref, dw_ref = refs
        else:
            x_ref, w_ref, dy_ref, dr_ref, dx_ref, dw_ref = refs

        @pl.when(pl.program_id(0) == 0)
        def _():
            dw_ref[...] = jnp.zeros_like(dw_ref)

        xv = x_ref[...]
        r = lax.rsqrt(jnp.mean(xv * xv, axis=-1, keepdims=True) + NORM_EPS)
        xhat = xv * r
        dyv = dy_ref[...].astype(F32) * scale
        dw_ref[...] += jnp.sum(dyv * xhat, axis=0, keepdims=True)
        t_ = dyv * w_ref[...]
        dx = r * (t_ - xhat * jnp.mean(t_ * xhat, axis=-1, keepdims=True))
        if dres is not None:
            dx = dx + dr_ref[...]
        dx_ref[...] = dx.astype(dx_dtype)

    row = pl.BlockSpec((tr, d), lambda i: (i, 0))
    wspec = pl.BlockSpec((1, d), lambda i: (0, 0))
    ins, specs = [x, w, dy], [pl.BlockSpec((tr, d), lambda i: (i, col // d)), wspec, row]
    if dres is not None:
        ins.append(dres)
        specs.append(row)
    return pl.pallas_call(
        body, name=name, grid=(t // tr,), in_specs=specs, out_specs=(row, wspec),
        out_shape=(jax.ShapeDtypeStruct((t, d), dx_dtype), jax.ShapeDtypeStruct((1, d), F32)),
        compiler_params=_params("arbitrary"),
    )(*ins)


def _elementwise(fn, ins, out_dtypes, *, name, width=None, cols=None):
    t = ins[0].shape[0]
    d = ins[0].shape[1] if width is None else width
    cols = [0] * len(ins) if cols is None else cols
    tc = _pick(d, 2048)
    for c in cols:
        tc = _pick(d, tc, c)
    tr = _row_tile(t, tc * 4)
    nout = len(out_dtypes)

    def body(*refs):
        outs = fn(*[r[...].astype(F32) for r in refs[:len(ins)]])
        for o_ref, o in zip(refs[len(ins):], outs):
            o_ref[...] = o.astype(o_ref.dtype)

    spec = pl.BlockSpec((tr, tc), lambda i, j: (i, j))
    in_specs = [pl.BlockSpec((tr, tc), lambda i, j, o=c // tc: (i, j + o)) for c in cols]
    return pl.pallas_call(
        body, name=name, grid=(t // tr, d // tc), in_specs=in_specs, out_specs=[spec] * nout,
        out_shape=[jax.ShapeDtypeStruct((t, d), dt) for dt in out_dtypes],
        compiler_params=_params("parallel", "parallel"),
    )(*ins)


def _merge_fwd(proj, col_a, col_b, ya, yb, *, name):
    return _elementwise(lambda a, b, p, q: (_sigmoid(a) * p + _sigmoid(b) * q,), [proj, proj, ya, yb], [BF16],
                        name=name, width=ya.shape[1], cols=[col_a, col_b, 0, 0])[0]


def _merge_bwd(dm, proj, col_a, col_b, ya, yb, *, name):
    def fn(dmv, a, b, p, q):
        sa, sb = _sigmoid(a), _sigmoid(b)
        return dmv * p * sa * (1.0 - sa), dmv * q * sb * (1.0 - sb), dmv * sa, dmv * sb

    return _elementwise(fn, [dm, proj, proj, ya, yb], [BF16, BF16, BF16, BF16], name=name, width=ya.shape[1],
                        cols=[0, col_a, col_b, 0, 0])


def _loss_head(xo, target, *, name):
    t, d = xo.shape
    tr = _pick(t, 256)

    def body(x_ref, t_ref, dx_ref, l_ref):
        @pl.when(pl.program_id(0) == 0)
        def _():
            l_ref[...] = jnp.zeros_like(l_ref)

        err = x_ref[...] - t_ref[...]
        dx_ref[...] = err * (1.0 / d)
        l_ref[...] += 0.5 * jnp.sum(jnp.mean(err * err, axis=-1, keepdims=True), axis=0, keepdims=True)

    row = pl.BlockSpec((tr, d), lambda i: (i, 0))
    dx, l = pl.pallas_call(
        body, name=name, grid=(t // tr,), in_specs=[row, row],
        out_specs=(row, pl.BlockSpec((1, 1), lambda i: (0, 0))),
        out_shape=(jax.ShapeDtypeStruct((t, d), F32), jax.ShapeDtypeStruct((1, 1), F32)),
        compiler_params=_params("arbitrary"),
    )(xo, target)
    return dx, l[0, 0]


def _rope(xin, tabs, *, name, group, backward, out_dtype, col=0, ngroup=None):
    t = xin.shape[0]
    ngroup = xin.shape[1] // group if ngroup is None else ngroup
    wdt = ngroup * group
    tr = _pick(t, 256)
    assert col % wdt == 0
    cos_t, nsin_t, sin_t = tabs

    def body(x_ref, c_ref, n_ref, s_ref, o_ref):
        cv, nv, sv = c_ref[...], n_ref[...], s_ref[...]
        for g in range(ngroup):
            lo, hi = g * group, (g + 1) * group
            rot = x_ref[:, hi - LANE:hi].astype(F32)
            if backward:
                out = rot * cv + pltpu.roll(rot * nv, 32, 1) + pltpu.roll(rot * sv, LANE - 32, 1)
            else:
                out = rot * cv + pltpu.roll(rot, LANE - 32, 1) * nv + pltpu.roll(rot, 32, 1) * sv
            if group > LANE:
                o_ref[:, lo:hi - LANE] = x_ref[:, lo:hi - LANE].astype(out_dtype)
            o_ref[:, hi - LANE:hi] = out.astype(out_dtype)

    xspec = pl.BlockSpec((tr, wdt), lambda i: (i, 0))
    tspec = pl.BlockSpec((tr, LANE), lambda i: (i, 0))
    return pl.pallas_call(
        body, name=name, grid=(t // tr,),
        in_specs=[pl.BlockSpec((tr, wdt), lambda i: (i, col // wdt)), tspec, tspec, tspec], out_specs=xspec,
        out_shape=jax.ShapeDtypeStruct((t, wdt), out_dtype), compiler_params=_params("parallel"),
    )(xin, cos_t, nsin_t, sin_t)


def _scores(q, kv, kr, qi, tq, scale):
    kcat = jnp.concatenate([kv[:, :HEAD], kr], axis=1)
    s = lax.dot_general(q, kcat, (((1,), (1,)), ((), ())), preferred_element_type=F32) * scale
    row = qi * tq + lax.broadcasted_iota(jnp.int32, s.shape, 0)
    col = lax.broadcasted_iota(jnp.int32, s.shape, 1)
    s = jnp.where(col <= row, s, -jnp.inf)
    p = jnp.exp(s - jnp.max(s, axis=-1, keepdims=True))
    return p / jnp.sum(p, axis=-1, keepdims=True), kcat


def _attn_fwd(qcat, kv, kr, *, name, scale):
    t = qcat.shape[0]
    nh = qcat.shape[1] // QGROUP
    tq = _pick(t, 256)

    def body(q_ref, kv_ref, kr_ref, o_ref):
        for qi in range(t // tq):
            @pl.when(pl.program_id(1) == qi)
            def _(qi=qi):
                kvv = kv_ref[0:(qi + 1) * tq, :]
                p, _ = _scores(q_ref[...], kvv, kr_ref[0:(qi + 1) * tq, :], qi, tq, scale)
                o_ref[...] = jnp.dot(p.astype(BF16), kvv[:, HEAD:], preferred_element_type=F32).astype(BF16)

    return pl.pallas_call(
        body, name=name, grid=(nh, t // tq),
        in_specs=[pl.BlockSpec((tq, QGROUP), lambda h, i: (i, h)), pl.BlockSpec((t, QGROUP), lambda h, i: (0, h)),
                  pl.BlockSpec((t, LANE), lambda h, i: (0, 0))],
        out_specs=pl.BlockSpec((tq, HEAD), lambda h, i: (i, h)),
        out_shape=jax.ShapeDtypeStruct((t, nh * HEAD), BF16), compiler_params=_params("parallel", "parallel"),
    )(qcat, kv, kr)


def _attn_bwd(qcat, kv, kr, do, *, name, scale):
    t = qcat.shape[0]
    nh = qcat.shape[1] // QGROUP
    tq = _pick(t, 256)
    nq = t // tq

    def body(q_ref, kv_ref, kr_ref, do_ref, dq_ref, dkv_ref, dkr_ref, dk_acc, dv_acc):
        h, i = pl.program_id(0), pl.program_id(1)

        @pl.when(i == 0)
        def _():
            dk_acc[...] = jnp.zeros_like(dk_acc)
            dv_acc[...] = jnp.zeros_like(dv_acc)

        @pl.when((i == 0) & (h == 0))
        def _():
            dkr_ref[...] = jnp.zeros_like(dkr_ref)

        for qi in range(nq):
            @pl.when(i == qi)
            def _(qi=qi):
                keys = slice(0, (qi + 1) * tq)
                q = q_ref[...]
                kvv = kv_ref[keys, :]
                dov = do_ref[...].astype(BF16)
                p, kcat = _scores(q, kvv, kr_ref[keys, :], qi, tq, scale)
                dp = lax.dot_general(dov, kvv[:, HEAD:], (((1,), (1,)), ((), ())), preferred_element_type=F32)
                ds = (p * (dp - jnp.sum(p * dp, axis=-1, keepdims=True)) * scale).astype(BF16)
                dq_ref[...] = jnp.dot(ds, kcat, preferred_element_type=F32)
                dk_acc[keys, :] += lax.dot_general(ds, q, (((0,), (0,)), ((), ())), preferred_element_type=F32)
                dv_acc[keys, :] += lax.dot_general(p.astype(BF16), dov, (((0,), (0,)), ((), ())), preferred_element_type=F32)

        @pl.when(i == nq - 1)
        def _():
            dk = dk_acc[...]
            dkv_ref[...] = jnp.concatenate([dk[:, :HEAD], dv_acc[...]], axis=1)
            dkr_ref[...] += dk[:, HEAD:]

    return pl.pallas_call(
        body, name=name, grid=(nh, nq),
        in_specs=[pl.BlockSpec((tq, QGROUP), lambda h, i: (i, h)), pl.BlockSpec((t, QGROUP), lambda h, i: (0, h)),
                  pl.BlockSpec((t, LANE), lambda h, i: (0, 0)), pl.BlockSpec((tq, HEAD), lambda h, i: (i, h))],
        out_specs=(pl.BlockSpec((tq, QGROUP), lambda h, i: (i, h)), pl.BlockSpec((t, QGROUP), lambda h, i: (0, h)),
                   pl.BlockSpec((t, LANE), lambda h, i: (0, 0))),
        out_shape=(jax.ShapeDtypeStruct((t, nh * QGROUP), F32), jax.ShapeDtypeStruct((t, nh * QGROUP), F32),
                   jax.ShapeDtypeStruct((t, LANE), F32)),
        scratch_shapes=[pltpu.VMEM((t, QGROUP), F32), pltpu.VMEM((t, HEAD), F32)],
        compiler_params=_params("arbitrary", "arbitrary"),
    )(qcat, kv, kr, do)


def _split3(x):
    hi = x.astype(BF16)
    r1 = x - hi.astype(F32)
    mid = r1.astype(BF16)
    lo = (r1 - mid.astype(F32)).astype(BF16)
    return hi, mid, lo


def _tri_matmul(mask, x):
    m = mask.astype(BF16)
    return sum(jnp.dot(m, part, preferred_element_type=F32) for part in _split3(x))


def _sub_cumsum(g, tb):
    row = lax.broadcasted_iota(jnp.int32, (tb, tb), 0)
    col = lax.broadcasted_iota(jnp.int32, (tb, tb), 1)
    return _tri_matmul(jnp.where((col <= row) & (col // SUB == row // SUB), 1.0, 0.0), g)


def _sub_suffix_prefix(after, before, tb):
    row = lax.broadcasted_iota(jnp.int32, (tb, tb), 0)
    col = lax.broadcasted_iota(jnp.int32, (tb, tb), 1)
    same = col // SUB == row // SUB
    return (_tri_matmul(jnp.where((col >= row) & same, 1.0, 0.0), after)
            + _tri_matmul(jnp.where((col < row) & same, 1.0, 0.0), before))


def _lower_bound(logits):
    mx = jnp.max(logits, axis=0, keepdims=True)
    e = jnp.exp(logits - mx)
    return e[0:1, :] / jnp.sum(e, axis=0, keepdims=True)


def _hgrn_fwd(proj, cols, wdt, logits, out_norm, *, name):
    t = proj.shape[0]
    nh = wdt // HEAD
    tb = _pick(t, 128)
    ns = tb // SUB

    def body(hq_ref, hf_ref, hi_ref, hg_ref, lg_ref, w_ref, o_ref, yb_ref, st_ref, s_ref, q_s, k_s, b_s):
        @pl.when(pl.program_id(1) == 0)
        def _():
            s_ref[...] = jnp.zeros_like(s_ref)

        lb = _lower_bound(lg_ref[...])
        f = lb + (1.0 - lb) * _sigmoid(hf_ref[...])
        q_s[...] = _silu(hq_ref[...])
        k_s[...] = 1.0 - f
        b_s[...] = _sub_cumsum(jnp.log(f), tb)
        rowid = lax.broadcasted_iota(jnp.int32, (SUB, HEAD), 0)

        def sub(c, st):
            rows = pl.ds(pl.multiple_of(c * SUB, SUB), SUB)
            qc, kc, bc, vc = q_s[rows, :], k_s[rows, :], b_s[rows, :], hi_ref[rows, :]
            st_ref[0, c] = st
            bl = bc[SUB - 1:SUB, :]
            oc = lax.dot_general((qc * jnp.exp(bc)).astype(BF16), st.astype(BF16), (((1,), (1,)), ((), ())),
                                 preferred_element_type=F32)
            for s in range(SUB):
                e = jnp.where(rowid >= s, jnp.exp(bc - bc[s:s + 1, :]), 0.0)
                a = jnp.sum(qc * e * kc[s:s + 1, :], axis=1, keepdims=True)
                oc = oc + a * vc[s:s + 1, :]
            o_ref[rows, :] = oc
            kd = kc * jnp.exp(bl - bc)
            return jnp.exp(bl) * st + lax.dot_general(vc.astype(BF16), kd.astype(BF16), (((0,), (0,)), ((), ())),
                                                      preferred_element_type=F32)

        s_ref[...] = lax.fori_loop(0, ns, sub, s_ref[...], unroll=True)
        o = o_ref[...]
        r = lax.rsqrt(jnp.mean(o * o, axis=-1, keepdims=True) + NORM_EPS)
        yb_ref[...] = (o * r * w_ref[...] * _silu(hg_ref[...])).astype(BF16)

    blk = pl.BlockSpec((tb, HEAD), lambda h, j: (j, h))
    return pl.pallas_call(
        body, name=name, grid=(nh, t // tb),
        in_specs=[pl.BlockSpec((tb, HEAD), lambda h, j, o=c // HEAD: (j, h + o)) for c in cols]
        + [pl.BlockSpec((2, HEAD), lambda h, j: (0, h)), pl.BlockSpec((1, HEAD), lambda h, j: (0, 0))],
        out_specs=(blk, blk, pl.BlockSpec((1, ns, HEAD, HEAD), lambda h, j: (h, j, 0, 0))),
        out_shape=(jax.ShapeDtypeStruct((t, wdt), F32), jax.ShapeDtypeStruct((t, wdt), BF16),
                   jax.ShapeDtypeStruct((nh, t // SUB, HEAD, HEAD), F32)),
        scratch_shapes=[pltpu.VMEM((HEAD, HEAD), F32)] + [pltpu.VMEM((tb, HEAD), F32)] * 3,
        compiler_params=_params("parallel", "arbitrary"),
    )(proj, proj, proj, proj, logits, out_norm)


def _hgrn_bwd(proj, cols, wdt, o_raw, dyb, states, logits, out_norm, *, name):
    t = proj.shape[0]
    nh = wdt // HEAD
    tb = _pick(t, 128)
    ns = tb // SUB
    nb = t // tb

    def body(hq_ref, hf_ref, hi_ref, hg_ref, o_ref, dy_ref, st_ref, lg_ref, w_ref,
             dhq_ref, dhf_ref, dhi_ref, dhg_ref, dlb_ref, dw_ref,
             ds_ref, q_s, k_s, b_s, do_s, dq_s, dk_s, dv_s, after_s, before_s, thru_s):
        @pl.when(pl.program_id(1) == 0)
        def _():
            ds_ref[...] = jnp.zeros_like(ds_ref)
            dlb_ref[...] = jnp.zeros_like(dlb_ref)
            dw_ref[...] = jnp.zeros_like(dw_ref)

        lb = _lower_bound(lg_ref[...])
        hqv, hgv = hq_ref[...], hg_ref[...]
        sig = _sigmoid(hf_ref[...])
        f = lb + (1.0 - lb) * sig
        q_s[...] = _silu(hqv)
        k_s[...] = 1.0 - f
        b_s[...] = _sub_cumsum(jnp.log(f), tb)

        o = o_ref[...]
        r = lax.rsqrt(jnp.mean(o * o, axis=-1, keepdims=True) + NORM_EPS)
        nrm = o * r
        w = w_ref[...]
        dy = dy_ref[...].astype(F32)
        dhg_ref[...] = (dy * nrm * w * _dsilu(hgv)).astype(BF16)
        dnw = dy * _silu(hgv)
        dw_ref[0] += jnp.sum(dnw * nrm, axis=0, keepdims=True)
        tt = dnw * w
        do_s[...] = r * (tt - nrm * jnp.mean(tt * nrm, axis=-1, keepdims=True))
        rowid = lax.broadcasted_iota(jnp.int32, (SUB, HEAD), 0)

        def sub(cc, dst):
            c = ns - 1 - cc
            rows = pl.ds(pl.multiple_of(c * SUB, SUB), SUB)
            qc, kc, bc, vc, doc = q_s[rows, :], k_s[rows, :], b_s[rows, :], hi_ref[rows, :], do_s[rows, :]
            st = st_ref[0, c]
            bl = bc[SUB - 1:SUB, :]
            eb = jnp.exp(bc)
            ekd = jnp.exp(bl - bc)
            qe, kd = qc * eb, kc * ekd
            dob, vcb = doc.astype(BF16), vc.astype(BF16)
            dq_st = jnp.dot(dob, st.astype(BF16), preferred_element_type=F32) * eb
            dk_st = jnp.dot(vcb, dst.astype(BF16), preferred_element_type=F32) * ekd
            dv = lax.dot_general(kd.astype(BF16), dst.astype(BF16), (((1,), (1,)), ((), ())), preferred_element_type=F32)
            dq_in = jnp.zeros_like(qc)
            dk_in = jnp.zeros_like(qc)
            for s in range(SUB):
                e = jnp.where(rowid >= s, jnp.exp(bc - bc[s:s + 1, :]), 0.0)
                ek = e * kc[s:s + 1, :]
                a = jnp.sum(qc * ek, axis=1, keepdims=True)
                da = jnp.sum(doc * vc[s:s + 1, :], axis=1, keepdims=True)
                dq_in = dq_in + da * ek
                dk_in = dk_in + jnp.where(rowid == s, jnp.sum(da * e * qc, axis=0, keepdims=True), 0.0)
                dv = dv + jnp.where(rowid == s, jnp.sum(a * doc, axis=0, keepdims=True), 0.0)
            ebl = jnp.exp(bl)
            dq_s[rows, :] = dq_st + dq_in
            dk_s[rows, :] = dk_st + dk_in
            dv_s[rows, :] = dv
            after_s[rows, :] = qc * (dq_st + dq_in) - kc * dk_in
            before_s[rows, :] = kc * dk_st
            thru_s[rows, :] = jnp.broadcast_to(ebl * jnp.sum(st * dst, axis=0, keepdims=True), (SUB, HEAD))
            return ebl * dst + lax.dot_general(dob, qe.astype(BF16), (((0,), (0,)), ((), ())), preferred_element_type=F32)

        ds_ref[...] = lax.fori_loop(0, ns, sub, ds_ref[...], unroll=True)
        dg = _sub_suffix_prefix(after_s[...], before_s[...], tb) + thru_s[...]
        dhq_ref[...] = (dq_s[...] * _dsilu(hqv)).astype(BF16)
        dft = dg / f - dk_s[...]
        dhf_ref[...] = (dft * (1.0 - lb) * sig * (1.0 - sig)).astype(BF16)
        dlb_ref[0] += jnp.sum(dft * (1.0 - sig), axis=0, keepdims=True)
        dhi_ref[...] = dv_s[...].astype(BF16)

    blk = pl.BlockSpec((tb, HEAD), lambda h, j: (nb - 1 - j, h))
    vec = pl.BlockSpec((1, 1, HEAD), lambda h, j: (h, 0, 0))
    tok = jax.ShapeDtypeStruct((t, wdt), BF16)
    per_head = jax.ShapeDtypeStruct((nh, 1, HEAD), F32)
    return pl.pallas_call(
        body, name=name, grid=(nh, nb),
        in_specs=[pl.BlockSpec((tb, HEAD), lambda h, j, o=c // HEAD: (nb - 1 - j, h + o)) for c in cols]
        + [blk, blk] + [pl.BlockSpec((1, ns, HEAD, HEAD), lambda h, j: (h, nb - 1 - j, 0, 0)),
                              pl.BlockSpec((2, HEAD), lambda h, j: (0, h)), pl.BlockSpec((1, HEAD), lambda h, j: (0, 0))],
        out_specs=(blk, blk, blk, blk, vec, vec),
        out_shape=(tok, tok, tok, tok, per_head, per_head),
        scratch_shapes=[pltpu.VMEM((HEAD, HEAD), F32)] + [pltpu.VMEM((tb, HEAD), F32)] * 10,
        compiler_params=_params("arbitrary", "arbitrary"),
    )(proj, proj, proj, proj, o_raw, dyb, states, logits, out_norm)


def _lb_logits_grad(logits, dlb, *, name):
    def body(lg_ref, d_ref, o_ref):
        lg = lg_ref[...]
        e = jnp.exp(lg - jnp.max(lg, axis=0, keepdims=True))
        p = e / jnp.sum(e, axis=0, keepdims=True)
        d = d_ref[...]
        rowid = lax.broadcasted_iota(jnp.int32, lg.shape, 0)
        dp = jnp.where(rowid == 0, d, 0.0)
        o_ref[...] = p * (dp - jnp.sum(p * dp, axis=0, keepdims=True))

    return pl.pallas_call(body, name=name, out_shape=jax.ShapeDtypeStruct(logits.shape, F32))(logits, dlb)


def _adamw(w, g, m, v, *, name, deps=()):
    r, c = w.shape
    tc = _pick(c, 2048) if c % LANE == 0 else c
    tr = _row_tile(r, tc * 4)

    def body(w_ref, g_ref, m_ref, v_ref, *rest):
        d_ref, nm_ref, nv_ref = rest[-3:]
        gv = g_ref[...]
        nm = ADAM_B1 * m_ref[...] + (1.0 - ADAM_B1) * gv
        nv = ADAM_B2 * v_ref[...] + (1.0 - ADAM_B2) * (gv * gv)
        m_hat = nm / (1.0 - ADAM_B1 ** ADAM_STEP)
        v_hat = nv / (1.0 - ADAM_B2 ** ADAM_STEP)
        d_ref[...] = -ADAM_LR * (m_hat / (jnp.sqrt(v_hat) + ADAM_EPS) + ADAM_WD * w_ref[...])
        nm_ref[...] = nm
        nv_ref[...] = nv

    spec = pl.BlockSpec((tr, tc), lambda i, j: (i, j))
    shp = jax.ShapeDtypeStruct((r, c), F32)
    return pl.pallas_call(
        body, name=name, grid=(r // tr, c // tc), in_specs=[spec] * 4 + [ANY] * len(deps), out_specs=[spec] * 3,
        out_shape=[shp, shp, shp], compiler_params=_params("parallel", "parallel"),
    )(w, g, m, v, *deps)


def _coords():
    return lax.axis_index("x"), lax.axis_index("y"), lax.axis_index("c")


def _other_chips(x, y):
    return [(1 - x, y), (x, 1 - y), (1 - x, 1 - y)]


ANY = pl.BlockSpec(memory_space=pl.ANY)


class _Layout:
    def __init__(self, d, dff, in_cols, q_lora, kv_lora, nh):
        assert q_lora == kv_lora and nh % 4 == 0 and dff % (4 * LANE) == 0 and in_cols % 4 == 0 and d % 4 == 0
        self.d, self.dff, self.q_lora, self.nh = d, dff, q_lora, nh
        self.head = q_lora + kv_lora + ROPE
        self.pad = d - self.head
        self.nff, self.ncol, self.r_o, self.hps = dff // 4, in_cols // 4, d // 4, nh // 4
        assert self.head <= self.ncol
        self.off_q, self.off_kv, self.rows_narrow = 0, nh * QGROUP, 2 * nh * QGROUP


HBM = pl.BlockSpec(memory_space=pltpu.HBM)
SEMS = pl.BlockSpec(memory_space=pltpu.SEMAPHORE)
SPLIT = dict(has_side_effects=pltpu.SideEffectType.DATAFLOW_SIDE_EFFECTING)


def _in_hbm(a):
    return pltpu.with_memory_space_constraint(a, pltpu.HBM)


def _shard_rows(jobs, k):
    out, lrow = [], [0] * (1 + max(job.a for job in jobs))
    for job in jobs:
        for row, rows in job.pieces(k):
            out.append((job.a, lrow[job.a], row, rows))
            lrow[job.a] += rows
    return out


def _shard_total(jobs, a):
    return sum(rows for b, _, _, rows in _shard_rows(jobs, 0) if b == a)


def _gather_start(packs, lands, jobs, *, name, deps=()):
    n = len(packs)

    def body(*refs):
        p_refs, l_refs, send, recv, token = refs[:n], refs[n:2 * n], refs[-2 * n - 3], refs[-2 * n - 2], refs[-1]
        x, y, c = _coords()
        for a, lrow, row, rows in _shard_rows(jobs, 2 * x + y):
            pltpu.make_async_remote_copy(
                src_ref=p_refs[a].at[:, pl.ds(lrow, rows)], dst_ref=l_refs[a].at[:, pl.ds(row, rows)],
                send_sem=send.at[4 * a + 3], recv_sem=recv.at[4 * a + 3], device_id=(x, y, 1 - c), device_id_type=MESH).start()
            for j, (px, py) in enumerate(_other_chips(x, y)):
                pltpu.make_async_remote_copy(
                    src_ref=p_refs[a].at[c, pl.ds(lrow, rows)], dst_ref=l_refs[a].at[c, pl.ds(row, rows)],
                    send_sem=send.at[4 * a + j], recv_sem=recv.at[4 * a + j], device_id=(px, py, c), device_id_type=MESH).start()
        token[...] = jnp.zeros_like(token)

    thru = [pltpu.HBM(a.shape, a.dtype) for a in packs + lands]
    out = pl.pallas_call(
        body, name=name, in_specs=[HBM] * (2 * n) + [ANY] * len(deps),
        out_shape=(pltpu.SemaphoreType.DMA((4 * n,)), pltpu.SemaphoreType.DMA((4 * n,)), *thru, jax.ShapeDtypeStruct((8, LANE), F32)),
        out_specs=(SEMS, SEMS, *[HBM] * (2 * n), pl.BlockSpec(memory_space=pltpu.VMEM)),
        input_output_aliases={i: 2 + i for i in range(2 * n)}, compiler_params=pltpu.CompilerParams(**SPLIT),
    )(*[_in_hbm(a) for a in packs + lands], *deps)
    return dict(send=out[0], recv=out[1], bufs=list(out[2:2 + 2 * n]), n=n, jobs=jobs), out[-1]


def _gather_wait(handle, after, *, name):
    n, jobs = handle["n"], handle["jobs"]

    def body(*refs):
        l_refs, send, recv, token = refs[n:2 * n], refs[2 * n], refs[2 * n + 1], refs[-1]
        token[...] = jnp.zeros_like(token)
        x, y, c = _coords()
        for a in range(n):
            total = _shard_total(jobs, a)
            for j, like in enumerate([l_refs[a].at[0, pl.ds(0, total)]] * 3 + [l_refs[a].at[:, pl.ds(0, total)]]):
                cp = pltpu.make_async_remote_copy(src_ref=like, dst_ref=like, send_sem=send.at[4 * a + j],
                                                  recv_sem=recv.at[4 * a + j], device_id=(x, y, c), device_id_type=MESH)
                cp.wait_send()
                cp.wait_recv()

    out = pl.pallas_call(
        body, name=name, in_specs=[HBM] * (2 * n) + [SEMS, SEMS] + [ANY] * len(after),
        out_shape=[pltpu.HBM(a.shape, a.dtype) for a in handle["bufs"]] + [jax.ShapeDtypeStruct((8, LANE), F32)],
        out_specs=[HBM] * (2 * n) + [pl.BlockSpec(memory_space=pltpu.VMEM)],
        input_output_aliases={i: i for i in range(2 * n)}, compiler_params=pltpu.CompilerParams(**SPLIT),
    )(*handle["bufs"], handle["send"], handle["recv"], *after)
    return list(out[n:2 * n]), out[-1]


def _gather_forward(lands, jobs, *, name, deps=()):
    n = len(lands)

    def body(*refs):
        l_refs, send, recv = refs[n + len(deps):2 * n + len(deps)], refs[-2], refs[-1]
        x, y, c = _coords()
        for j, (px, py) in enumerate(_other_chips(x, y)):
            for a, _, row, rows in _shard_rows(jobs, 2 * px + py):
                blk = l_refs[a].at[c, pl.ds(row, rows)]
                pltpu.make_async_remote_copy(src_ref=blk, dst_ref=blk, send_sem=send.at[3 * a + j], recv_sem=recv.at[3 * a + j],
                                             device_id=(x, y, 1 - c), device_id_type=MESH).start()
        for a in range(n):
            like = l_refs[a].at[0, pl.ds(0, _shard_total(jobs, a))]
            for j in range(3):
                cp = pltpu.make_async_remote_copy(src_ref=like, dst_ref=like, send_sem=send.at[3 * a + j],
                                                  recv_sem=recv.at[3 * a + j], device_id=(x, y, c), device_id_type=MESH)
                cp.wait_send()
                cp.wait_recv()

    sem = pltpu.SemaphoreType.DMA((3 * n,))
    return pl.pallas_call(
        body, name=name, in_specs=[ANY] * (n + len(deps)), out_specs=[ANY] * n, input_output_aliases={i: i for i in range(n)},
        out_shape=[jax.ShapeDtypeStruct(a.shape, a.dtype) for a in lands], scratch_shapes=[sem, sem],
    )(*lands, *deps)


def _forward_start(lands, jobs, *, name, deps=()):
    n, nd = len(lands), len(deps)

    def body(*refs):
        l_refs, sems, token = refs[:n], refs[n + nd:n + nd + 2 * n], refs[-1]
        x, y, c = _coords()
        for j, (px, py) in enumerate(_other_chips(x, y)):
            for a, _, row, rows in _shard_rows(jobs, 2 * px + py):
                blk = l_refs[a].at[c, pl.ds(row, rows)]
                pltpu.make_async_remote_copy(src_ref=blk, dst_ref=blk, send_sem=sems[2 * a].at[j], recv_sem=sems[2 * a + 1].at[j],
                                             device_id=(x, y, 1 - c), device_id_type=MESH).start()
        token[...] = jnp.zeros_like(token)

    out = pl.pallas_call(
        body, name=name, in_specs=[HBM] * n + [ANY] * nd,
        out_shape=(*[pltpu.SemaphoreType.DMA((3,))] * (2 * n), *[pltpu.HBM(a.shape, a.dtype) for a in lands],
                   jax.ShapeDtypeStruct((8, LANE), F32)),
        out_specs=(*[SEMS] * (2 * n), *[HBM] * n, pl.BlockSpec(memory_space=pltpu.VMEM)),
        input_output_aliases={i: 2 * n + i for i in range(n)}, compiler_params=pltpu.CompilerParams(**SPLIT),
    )(*[_in_hbm(a) for a in lands], *deps)
    return [dict(send=out[2 * a], recv=out[2 * a + 1], buf=out[2 * n + a]) for a in range(n)], out[-1]


def _forward_wait(handle, jobs, a, after, *, name):
    total = _shard_total(jobs, a)

    def body(l_ref, send, recv, *rest):
        x, y, c = _coords()
        like = l_ref.at[0, pl.ds(0, total)]
        for j in range(3):
            cp = pltpu.make_async_remote_copy(src_ref=like, dst_ref=like, send_sem=send.at[j], recv_sem=recv.at[j],
                                              device_id=(x, y, c), device_id_type=MESH)
            cp.wait_send()
            cp.wait_recv()

    buf = handle["buf"]
    return pl.pallas_call(
        body, name=name, in_specs=[HBM, SEMS, SEMS] + [ANY] * len(after), out_shape=pltpu.HBM(buf.shape, buf.dtype),
        out_specs=HBM, input_output_aliases={0: 0}, compiler_params=pltpu.CompilerParams(**SPLIT),
    )(buf, handle["send"], handle["recv"], *after)


def _add_sibling(g, recv, sel, *, name):
    rows, hw = recv.shape
    tr = _row_tile(rows, hw * 4)

    def body(sel_ref, g_ref, r_ref, o_ref):
        o_ref[...] = (g_ref[...] + r_ref[...]).astype(BF16)

    return pl.pallas_call(
        body, name=name, out_shape=jax.ShapeDtypeStruct((rows, hw), BF16),
        grid_spec=pltpu.PrefetchScalarGridSpec(
            num_scalar_prefetch=1, grid=(rows // tr,),
            in_specs=[pl.BlockSpec((None, tr, hw), lambda i, s: (s[0], i, 0)), pl.BlockSpec((tr, hw), lambda i, s: (i, 0))],
            out_specs=pl.BlockSpec((tr, hw), lambda i, s: (i, 0))),
        compiler_params=_params("parallel"),
    )(sel, g, recv)


class _Job:
    def __init__(self, a, blk, n_outer, n_inner, stride, start):
        self.a, self.blk, self.n_outer, self.n_inner, self.stride, self.start = a, blk, n_outer, n_inner, stride, start
        self.rows_out = n_outer * n_inner * blk

    def pieces(self, k):
        return [(self.start(k) + o * self.stride * self.blk, self.n_inner * self.blk) for o in range(self.n_outer)]


def _block_rows(rows, cap, *also):
    best = None
    for b in range(16, min(rows, cap) + 1, 16):
        if rows % b == 0 and all(v % b == 0 for v in also):
            best = b
    assert best is not None, (rows, also)
    return best


def _ffn_jobs(lay):
    b = _block_rows(lay.nff, 704, lay.dff)
    return [_Job(0, b, 3, lay.nff // b, lay.dff // b, lambda k: lay.nff * k)]


def _ffn_weight_jobs(lay):
    b = _block_rows(lay.nff, 704)
    return [_Job(a, b, 1, lay.nff // b, 0, lambda k: lay.nff * k) for a in range(3)]


def _mix_jobs(lay):
    d, ncol, head, pad = lay.d, lay.ncol, lay.head, lay.pad
    first = lambda k, a, b: jnp.where(k == 0, a, b) if not isinstance(k, int) else (a if k == 0 else b)
    ba = _block_rows(head, 704, *[ncol * k + pad for k in (1, 2, 3)])
    bb = _block_rows(ncol - head, 704, *[ncol * k + d for k in (0, 1, 2, 3)])
    bo = _block_rows(lay.r_o, 704, d)
    bq = _block_rows(HEAD + ROPE, 704, QGROUP)
    bk = _block_rows(lay.hps * QGROUP, 704, lay.off_kv)
    return [_Job(0, ba, 1, head // ba, 0, lambda k: first(k, 0, ncol * k + pad)),
            _Job(0, bb, 1, (ncol - head) // bb, 0, lambda k: ncol * k + d),
            _Job(0, bo, 3, lay.r_o // bo, d // bo, lambda k: 7 * d + lay.r_o * k),
            _Job(1, bq, lay.hps, (HEAD + ROPE) // bq, QGROUP // bq, lambda k: QGROUP * lay.hps * k),
            _Job(1, bk, 1, lay.hps * QGROUP // bk, 0, lambda k: lay.off_kv + lay.hps * QGROUP * k)]


def _swap_start(gs, *, name):
    n = len(gs)
    lands = [lax.empty(g.shape[1:], g.dtype) for g in gs]

    def body(*refs):
        g_refs, land_refs, send, recv, token = refs[:n], refs[n:2 * n], refs[2 * n], refs[2 * n + 1], refs[-1]
        x, y, c = _coords()
        for a in range(n):
            pltpu.make_async_remote_copy(src_ref=g_refs[a].at[1 - c], dst_ref=land_refs[a], send_sem=send.at[a],
                                         recv_sem=recv.at[a], device_id=(x, y, 1 - c), device_id_type=MESH).start()
        token[...] = jnp.zeros_like(token)

    thru = [pltpu.HBM(a.shape, a.dtype) for a in gs + lands]
    out = pl.pallas_call(
        body, name=name, in_specs=[HBM] * (2 * n),
        out_shape=(pltpu.SemaphoreType.DMA((n,)), pltpu.SemaphoreType.DMA((n,)), *thru, jax.ShapeDtypeStruct((8, LANE), F32)),
        out_specs=(SEMS, SEMS, *[HBM] * (2 * n), pl.BlockSpec(memory_space=pltpu.VMEM)),
        input_output_aliases={i: 2 + i for i in range(2 * n)}, compiler_params=pltpu.CompilerParams(**SPLIT),
    )(*[_in_hbm(a) for a in gs + lands])
    return dict(send=out[0], recv=out[1], bufs=list(out[2:2 + 2 * n]), n=n), out[-1]


def _swap_wait(handle, after, *, name):
    n = handle["n"]

    def body(*refs):
        g_refs, land_refs, send, recv = refs[:n], refs[n:2 * n], refs[2 * n], refs[2 * n + 1]
        x, y, c = _coords()
        for a in range(n):
            cp = pltpu.make_async_remote_copy(src_ref=g_refs[a].at[1 - c], dst_ref=land_refs[a], send_sem=send.at[a],
                                              recv_sem=recv.at[a], device_id=(x, y, 1 - c), device_id_type=MESH)
            cp.wait_send()
            cp.wait_recv()

    out = pl.pallas_call(
        body, name=name, in_specs=[HBM] * (2 * n) + [SEMS, SEMS] + [ANY] * len(after),
        out_shape=[pltpu.HBM(a.shape, a.dtype) for a in handle["bufs"]], out_specs=[HBM] * (2 * n),
        input_output_aliases={i: i for i in range(2 * n)}, compiler_params=pltpu.CompilerParams(**SPLIT),
    )(*handle["bufs"], handle["send"], handle["recv"], *after)
    return list(out[:n]), list(out[n:])


def _exchange_start(ss, jobs, *, name):
    n = len(ss)
    lands = [lax.empty((3,) + s.shape, s.dtype) for s in ss]

    def body(*refs):
        s_refs, land_refs, send, recv, token = refs[:n], refs[n:2 * n], refs[2 * n], refs[2 * n + 1], refs[-1]
        x, y, c = _coords()
        for j, (px, py) in enumerate(_other_chips(x, y)):
            for job in jobs:
                for row, rows in job.pieces(2 * px + py):
                    pltpu.make_async_remote_copy(
                        src_ref=s_refs[job.a].at[pl.ds(row, rows)], dst_ref=land_refs[job.a].at[j, pl.ds(row, rows)],
                        send_sem=send.at[n * j + job.a], recv_sem=recv.at[n * j + job.a], device_id=(px, py, c),
                        device_id_type=MESH).start()
        token[...] = jnp.zeros_like(token)

    thru = [pltpu.HBM(a.shape, a.dtype) for a in ss + lands]
    out = pl.pallas_call(
        body, name=name, in_specs=[HBM] * (2 * n),
        out_shape=(pltpu.SemaphoreType.DMA((3 * n,)), pltpu.SemaphoreType.DMA((3 * n,)), *thru, jax.ShapeDtypeStruct((8, LANE), F32)),
        out_specs=(SEMS, SEMS, *[HBM] * (2 * n), pl.BlockSpec(memory_space=pltpu.VMEM)),
        input_output_aliases={i: 2 + i for i in range(2 * n)}, compiler_params=pltpu.CompilerParams(**SPLIT),
    )(*[_in_hbm(a) for a in ss + lands])
    return dict(send=out[0], recv=out[1], bufs=list(out[2:2 + 2 * n]), n=n, jobs=jobs), out[-1]


def _exchange_wait(handle, after, *, name):
    n, jobs = handle["n"], handle["jobs"]
    total = [sum(rows for job in jobs if job.a == a for _, rows in job.pieces(0)) for a in range(n)]

    def body(*refs):
        s_refs, land_refs, send, recv = refs[:n], refs[n:2 * n], refs[2 * n], refs[2 * n + 1]
        x, y, c = _coords()
        for a in range(n):
            for j in range(3):
                all_rows = land_refs[a].at[0, pl.ds(0, total[a])]
                cp = pltpu.make_async_remote_copy(src_ref=all_rows, dst_ref=all_rows, send_sem=send.at[n * j + a],
                                                  recv_sem=recv.at[n * j + a], device_id=(x, y, c), device_id_type=MESH)
                cp.wait_send()
                cp.wait_recv()

    out = pl.pallas_call(
        body, name=name, in_specs=[HBM] * (2 * n) + [SEMS, SEMS] + [ANY] * len(after),
        out_shape=[pltpu.HBM(a.shape, a.dtype) for a in handle["bufs"]], out_specs=[HBM] * (2 * n),
        input_output_aliases={i: i for i in range(2 * n)}, compiler_params=pltpu.CompilerParams(**SPLIT),
    )(*handle["bufs"], handle["send"], handle["recv"], *after)
    return list(out[:n]), list(out[n:])


def _add_shard(s, land, job, sel, k, *, name):
    hw = s.shape[1]
    blk, no, ni, stride = job.blk, job.n_outer, job.n_inner, job.stride
    scal = jnp.stack([sel, job.start(k) // blk]).astype(jnp.int32)

    def body(sc_ref, own_ref, r_ref, o_ref):
        o_ref[...] = ((own_ref[...].astype(F32) + r_ref[0].astype(F32)) + r_ref[1].astype(F32)) + r_ref[2].astype(F32)

    return pl.pallas_call(
        body, name=name, out_shape=jax.ShapeDtypeStruct((2, job.rows_out, hw), F32),
        grid_spec=pltpu.PrefetchScalarGridSpec(
            num_scalar_prefetch=1, grid=(no, ni),
            in_specs=[pl.BlockSpec((blk, hw), lambda o, b, sc: (sc[1] + o * stride + b, 0)),
                      pl.BlockSpec((3, blk, hw), lambda o, b, sc: (0, sc[1] + o * stride + b, 0))],
            out_specs=pl.BlockSpec((None, blk, hw), lambda o, b, sc: (sc[0], o * ni + b, 0))),
        compiler_params=_params("parallel", "parallel"),
    )(scal, s, land)


def _join_list(fs, *, name):
    n = len(fs)

    def body(*refs):
        f_refs, send_sems, recv_sems = refs[n:2 * n], refs[2 * n], refs[2 * n + 1]
        x, y, c = _coords()
        copies = [pltpu.make_async_remote_copy(
            src_ref=f.at[c], dst_ref=f.at[c], send_sem=send_sems.at[a], recv_sem=recv_sems.at[a],
            device_id=(x, y, 1 - c), device_id_type=MESH) for a, f in enumerate(f_refs)]
        for cp in copies:
            cp.start()
        for cp in copies:
            cp.wait()

    sem = pltpu.SemaphoreType.DMA((n,))
    return pl.pallas_call(
        body, name=name, in_specs=[ANY] * n, out_specs=[ANY] * n, input_output_aliases={i: i for i in range(n)},
        out_shape=[jax.ShapeDtypeStruct(f.shape, f.dtype) for f in fs], scratch_shapes=[sem, sem],
    )(*fs)


def _all_reduce_small(vec, *, name):
    n = vec.shape[1]

    def body(v_ref, o_ref, buf, send_sems, recv_sems):
        x, y, c = _coords()
        me = 4 * x + 2 * y + c
        buf[me] = v_ref[...]
        copies = []
        for m in range(1, 8):
            peer = (x ^ ((m >> 2) & 1), y ^ ((m >> 1) & 1), c ^ (m & 1))
            copies.append(pltpu.make_async_remote_copy(
                src_ref=v_ref, dst_ref=buf.at[me], send_sem=send_sems.at[m - 1], recv_sem=recv_sems.at[m - 1],
                device_id=peer, device_id_type=MESH))
        for cp in copies:
            cp.start()
        for cp in copies:
            cp.wait()
        acc = buf[0]
        for d in range(1, 8):
            acc = acc + buf[d]
        o_ref[...] = acc

    return pl.pallas_call(
        body, name=name, out_shape=jax.ShapeDtypeStruct((1, n), F32),
        in_specs=[pl.BlockSpec(memory_space=pltpu.VMEM)], out_specs=pl.BlockSpec(memory_space=pltpu.VMEM),
        scratch_shapes=[pltpu.VMEM((8, 1, n), F32), pltpu.SemaphoreType.DMA((7,)), pltpu.SemaphoreType.DMA((7,))],
    )(vec)


def _ffn_fwd(x, n_pre, n_post, weight, lay, tag):
    h = _norm_fwd(x, n_pre, name=f"{tag}_norm_pre", out_dtype=BF16)
    wg = (weight(0, [h]), 0, lay.dff)
    g = _mm([(h, wg)], name=f"{tag}_gate", mode="nt")
    wu = (weight(1, [g]), 0, lay.dff)
    u, a = _mm([(h, wu)], name=f"{tag}_up", mode="nt", extras=[g], out_dtypes=[F32, BF16], tm_cap=MM_TILE // 2,
               epilogue=lambda up, gate: (up, _silu(gate) * up))
    wd = (weight(2, [u]), 0, lay.dff)
    yv = _mm([(a, wd)], name=f"{tag}_down", mode="nn")
    out = _norm_fwd(yv, n_post, name=f"{tag}_norm_post", resid=x, scale=MACARON_SCALE)
    return out, (x, h, g, u, a, yv), (wg, wu, wd)


def _ffn_bwd(dout, saved, n_pre, n_post, weights, lay, tag, deps=(), after_act=None, after_dw=None):
    x, h, g, u, a, yv = saved
    dff = lay.dff
    gbuf = lax.empty((2, 3 * dff, lay.d // 2), F32)
    dy, dn_post = _norm_bwd(yv, n_post, dout, name=f"{tag}_norm_post_bwd", scale=MACARON_SCALE)
    wg, wu, wd = weights
    dg, du = _mm([(dy, wd)], name=f"{tag}_down_dx", mode="nt", deps=deps, extras=[g, u],
                 out_dtypes=[BF16, BF16], tm_cap=MM_TILE // 2,
                 epilogue=lambda da, gate, up: (da * up * _dsilu(gate), da * _silu(gate)))
    deps = after_act(du) if after_act is not None else ()
    gbuf = _mm([(a, dy)], name=f"{tag}_down_dw", mode="tn", into=(gbuf, 2 * dff), deps=deps)
    gbuf = _mm([(dg, h)], name=f"{tag}_gate_dw", mode="tn", into=(gbuf, 0))
    gbuf = _mm([(du, h)], name=f"{tag}_up_dw", mode="tn", into=(gbuf, dff))
    deps = after_dw(gbuf)
    dh = _mm([(dg, wg), (du, wu)], name=f"{tag}_up_dx", mode="nn", deps=deps)
    dx, dn_pre = _norm_bwd(x, n_pre, dh, name=f"{tag}_norm_pre_bwd", dres=dout)
    return dx, dn_pre, dn_post


def _rope_tables(positions):
    half = ROPE // 2
    inv_freq = ROPE_THETA ** (-jnp.arange(half, dtype=F32) / half)
    ang = positions.astype(F32)[:, None] * inv_freq
    cos, sin = jnp.cos(ang), jnp.sin(ang)
    z = jnp.zeros_like(cos)
    z2 = jnp.zeros((positions.shape[0], LANE - ROPE), F32)
    return (jnp.concatenate([cos, cos, z2], axis=1), jnp.concatenate([-sin, z, z2], axis=1),
            jnp.concatenate([z, sin, z2], axis=1))


def kernel(x, positions, ffn1_norm_pre, ffn1_w_gate, ffn1_w_up, ffn1_w_down, ffn1_norm_post, mix_norm_pre, w_in, mla_q_norm, mla_w_q_up, mla_kv_norm, mla_w_kv_up, mla_w_o, hgrn_lb_logits, hgrn_out_norm, hgrn_w_o, w_out, mix_norm_post, ffn2_norm_pre, ffn2_w_gate, ffn2_w_up, ffn2_w_down, ffn2_norm_post, loss_target, m_ffn1_norm_pre, m_ffn1_w_gate, m_ffn1_w_up, m_ffn1_w_down, m_ffn1_norm_post, m_mix_norm_pre, m_w_in, m_mla_q_norm, m_mla_w_q_up, m_mla_kv_norm, m_mla_w_kv_up, m_mla_w_o, m_hgrn_lb_logits, m_hgrn_out_norm, m_hgrn_w_o, m_w_out, m_mix_norm_post, m_ffn2_norm_pre, m_ffn2_w_gate, m_ffn2_w_up, m_ffn2_w_down, m_ffn2_norm_post, v_ffn1_norm_pre, v_ffn1_w_gate, v_ffn1_w_up, v_ffn1_w_down, v_ffn1_norm_post, v_mix_norm_pre, v_w_in, v_mla_q_norm, v_mla_w_q_up, v_mla_kv_norm, v_mla_w_kv_up, v_mla_w_o, v_hgrn_lb_logits, v_hgrn_out_norm, v_hgrn_w_o, v_w_out, v_mix_norm_post, v_ffn2_norm_pre, v_ffn2_w_gate, v_ffn2_w_up, v_ffn2_w_down, v_ffn2_norm_post):
    given = dict(locals())
    wts = {n: given[n] for n in ALL_WEIGHTS}
    mom = {n: given["m_" + n] for n in ALL_WEIGHTS}
    var = {n: given["v_" + n] for n in ALL_WEIGHTS}
    xin = x[0]
    target = loss_target[0]
    t, d = xin.shape
    cx, cy, cc = _coords()

    q_lora, kv_lora = mla_q_norm.shape[1], mla_kv_norm.shape[1]
    nh_mla = 4 * mla_w_kv_up.shape[2] // QGROUP
    lay = _Layout(d, 4 * ffn1_w_gate.shape[2], 4 * w_in.shape[2], q_lora, kv_lora, nh_mla)
    jobs_mix = _mix_jobs(lay)
    def pack(src, col_sharded, row_sharded=()):
        a = jnp.concatenate([src[n][0].T.astype(BF16) for n in col_sharded] + [src[n][0].astype(BF16) for n in row_sharded])
        return a.reshape(a.shape[0], 2, a.shape[1] // 2).transpose(1, 0, 2)

    jobs_w = _ffn_weight_jobs(lay)
    ffn_packs = lambda src, tag: [pack(src, [f"{tag}_w_gate"]), pack(src, [f"{tag}_w_up"]), pack(src, [], [f"{tag}_w_down"])]
    ffn_lands = lambda: [lax.empty((2, lay.dff, d // 2), BF16) for _ in range(3)]

    def handed_over(handles, tag):
        return lambda i, after: _forward_wait(handles[i], jobs_w, i, after, name=f"gather_{tag}_forward_wait_{i}")

    got1, tok = _gather_start(ffn_packs(wts, "ffn1"), ffn_lands(), jobs_w, name="gather_ffn1")
    later, _ = lax.optimization_barrier(({n: wts[n] for n in BIG_WEIGHTS if not n.startswith("ffn1")}, tok))
    packs_mix = [pack(later, ["w_in"], ["mla_w_o", "hgrn_w_o", "w_out"]), pack(later, ["mla_w_q_up", "mla_w_kv_up"])]
    packs_ffn2 = ffn_packs(later, "ffn2")
    lands_mix = [jnp.zeros((2, 10 * d, d // 2), BF16), jnp.zeros((2, lay.rows_narrow, q_lora // 2), BF16)]
    arrived, tok = _gather_wait(got1, packs_mix + packs_ffn2 + lands_mix, name="gather_ffn1_wait")
    got_m, tok = _gather_start(packs_mix, lands_mix, jobs_mix, name="gather_mix", deps=[tok])
    handing1, _ = _forward_start(arrived[:1], jobs_w[:1], name="gather_ffn1_forward_gate", deps=[tok])

    def ffn1_weight(i, after):
        if i == 0:
            gate = _forward_wait(handing1[0], jobs_w, 0, after, name="gather_ffn1_forward_wait_0")
            handing1.extend(_forward_start(arrived[1:], jobs_w[:2], name="gather_ffn1_forward_rest", deps=[gate])[0])
            return gate
        return _forward_wait(handing1[i], jobs_w, 0, after, name=f"gather_ffn1_forward_wait_{i}")
    col_kr = q_lora + kv_lora
    hgrn_cols = [d, 2 * d, 3 * d, 4 * d]
    col_ga, col_gb = 5 * d, 6 * d
    tabs = _rope_tables(positions[0])
    scale = (HEAD + ROPE) ** -0.5

    x1, saved1, w_ffn1 = _ffn_fwd(xin, ffn1_norm_pre, ffn1_norm_post, ffn1_weight, lay, "ffn1")

    arrived, tok = _gather_wait(got_m, [x1], name="gather_mix_wait")
    got2, tok = _gather_start(packs_ffn2, ffn_lands(), jobs_w, name="gather_ffn2", deps=[tok])
    wide, narrow = _gather_forward(arrived, jobs_mix, name="gather_mix_forward", deps=[tok])
    w_in_v = (wide, 0, 7 * d)
    w_o_v = {n: (wide, (7 + i) * d, d) for i, n in enumerate(("mla_w_o", "hgrn_w_o", "w_out"))}
    w_q_v = (narrow, lay.off_q, nh_mla * QGROUP)
    w_kv_v = (narrow, lay.off_kv, nh_mla * QGROUP)

    h2 = _norm_fwd(x1, mix_norm_pre, name="mix_norm_pre", out_dtype=BF16)
    proj = _mm([(h2, w_in_v)], name="mix_in", mode="nt", deps=[tok])
    cqn = _norm_fwd(proj, mla_q_norm, name="mla_q_norm", out_dtype=BF16, col=0)
    ckvn = _norm_fwd(proj, mla_kv_norm, name="mla_kv_norm", out_dtype=BF16, col=q_lora)
    qp = _mm([(cqn, w_q_v)], name="mla_q_up", mode="nt")
    kvb = _mm([(ckvn, w_kv_v)], name="mla_kv_up", mode="nt", out_dtype=BF16)
    qcat = _rope(qp, tabs, name="rope_q", group=QGROUP, backward=False, out_dtype=BF16)
    krot = _rope(proj, tabs, name="rope_k", group=LANE, backward=False, out_dtype=BF16, col=col_kr, ngroup=1)
    o_mla = _attn_fwd(qcat, kvb, krot, name="mla_attention", scale=scale)
    y_a = _mm([(o_mla, w_o_v["mla_w_o"])], name="mla_out", mode="nn")

    o_raw, yb, states = _hgrn_fwd(proj, hgrn_cols, d, hgrn_lb_logits, hgrn_out_norm, name="hgrn_scan")
    handing2, tok = _forward_start(_gather_wait(got2, [o_raw], name="gather_ffn2_wait")[0], jobs_w, name="gather_ffn2_forward")
    y_b = _mm([(yb, w_o_v["hgrn_w_o"])], name="hgrn_out", mode="nn", deps=[tok])

    merged = _merge_fwd(proj, col_ga, col_gb, y_a, y_b, name="mix_merge")
    y_mix = _mm([(merged, w_o_v["w_out"])], name="mix_out", mode="nn")
    x2 = _norm_fwd(y_mix, mix_norm_post, name="mix_norm_post", resid=x1, scale=1.0)

    x3, saved2, w_ffn2 = _ffn_fwd(x2, ffn2_norm_pre, ffn2_norm_post, handed_over(handing2, "ffn2"), lay, "ffn2")
    dx3, loss_local = _loss_head(x3, target, name="loss_head")

    grads, deltas, new_m, new_v = {}, {}, {}, {}
    sel = cc.astype(jnp.int32)
    sel1 = jnp.reshape(sel, (1,))
    me_chip = (2 * cx + cy).astype(jnp.int32)

    def reduce_mid(handle, after, jobs, tag):
        bufs, recvd = _swap_wait(handle, after, name=f"grad_swap_{tag}_wait")
        sums = [_add_sibling(b, r, sel1, name=f"grad_add_sibling_{tag}_{i}") for i, (b, r) in enumerate(zip(bufs, recvd))]
        return _exchange_start(sums, jobs, name=f"grad_exchange_{tag}")

    def reduce_end(handle, after, tag):
        sums, lands = _exchange_wait(handle, after, name=f"grad_exchange_{tag}_wait")
        parts = [_add_shard(sums[job.a], lands[job.a], job, sel, me_chip, name=f"grad_add_chips_{tag}_{i}")
                 for i, job in enumerate(handle["jobs"])]
        return _join_list(parts, name=f"grad_join_{tag}")

    def natural(part, lo, rows, transposed):
        g_n = part[:, lo:lo + rows]
        hw_n = g_n.shape[2]
        return g_n.transpose(0, 2, 1).reshape(2 * hw_n, rows) if transposed else g_n.transpose(1, 0, 2).reshape(rows, 2 * hw_n)

    def adam(names, deps=()):
        for i, n in enumerate(names):
            shp = wts[n].shape
            two_d = (lambda a: a[0]) if n in BIG_WEIGHTS else (lambda a: a)
            dl, nm, nv = _adamw(two_d(wts[n]), grads[n], two_d(mom[n]), two_d(var[n]), name=f"adamw_{n}",
                                deps=deps if i == 0 else ())
            grads[n] = grads[n].reshape(shp)
            deltas[n], new_m[n], new_v[n] = dl.reshape(shp), nm.reshape(shp), nv.reshape(shp)
        return [deltas[n] for n in names]

    def ffn_grads(joined, tag, deps=()):
        nff = lay.nff
        grads[f"{tag}_w_gate"] = natural(joined[0], 0, nff, True)
        grads[f"{tag}_w_up"] = natural(joined[0], nff, nff, True)
        grads[f"{tag}_w_down"] = natural(joined[0], 2 * nff, nff, False)
        return adam([f"{tag}_w_gate", f"{tag}_w_up", f"{tag}_w_down"], deps)

    swaps = {}

    def start_swap(tag):
        def hook(gbuf):
            swaps[tag], started = _swap_start([gbuf], name=f"grad_swap_{tag}")
            return [started]
        return hook

    dx2, grads["ffn2_norm_pre"], grads["ffn2_norm_post"] = _ffn_bwd(
        dx3, saved2, ffn2_norm_pre, ffn2_norm_post, w_ffn2, lay, "ffn2", after_dw=start_swap("ffn2"))

    gwide = lax.empty((2, 10 * d, d // 2), F32)
    gnarrow = lax.empty((2, lay.rows_narrow, q_lora // 2), F32)
    dy_mix, grads["mix_norm_post"] = _norm_bwd(y_mix, mix_norm_post, dx2, name="mix_norm_post_bwd")
    dmerged = _mm([(dy_mix, w_o_v["w_out"])], name="mix_out_dx", mode="nt")
    gwide = _mm([(merged, dy_mix)], name="mix_out_dw", mode="tn", into=(gwide, 9 * d))
    dga, dgb, dy_a, dy_b = _merge_bwd(dmerged, proj, col_ga, col_gb, y_a, y_b, name="mix_merge_bwd")

    do_mla = _mm([(dy_a, w_o_v["mla_w_o"])], name="mla_out_dx", mode="nt")
    gwide = _mm([(o_mla, dy_a)], name="mla_out_dw", mode="tn", into=(gwide, 7 * d))
    dqcat, dkv, dkr = _attn_bwd(qcat, kvb, krot, do_mla, name="mla_attention_bwd", scale=scale)
    exch2, tok = reduce_mid(swaps["ffn2"], [dkr], _ffn_jobs(lay), "ffn2")

    dqp = _rope(dqcat, tabs, name="rope_q_bwd", group=QGROUP, backward=True, out_dtype=BF16)
    dk_r = _rope(dkr, tabs, name="rope_k_bwd", group=LANE, backward=True, out_dtype=BF16)
    dcqn = _mm([(dqp, w_q_v)], name="mla_q_up_dx", mode="nn", deps=[tok])
    gnarrow = _mm([(dqp, cqn)], name="mla_q_up_dw", mode="tn", into=(gnarrow, lay.off_q))
    dkvb = dkv.astype(BF16)
    dckvn = _mm([(dkvb, w_kv_v)], name="mla_kv_up_dx", mode="nn")
    gnarrow = _mm([(dkvb, ckvn)], name="mla_kv_up_dw", mode="tn", into=(gnarrow, lay.off_kv))
    dc_q, grads["mla_q_norm"] = _norm_bwd(proj, mla_q_norm, dcqn, name="mla_q_norm_bwd", col=0, dx_dtype=BF16)
    dc_kv, grads["mla_kv_norm"] = _norm_bwd(proj, mla_kv_norm, dckvn, name="mla_kv_norm_bwd", col=q_lora, dx_dtype=BF16)

    dyb = _mm([(dy_b, w_o_v["hgrn_w_o"])], name="hgrn_out_dx", mode="nt")
    gwide = _mm([(yb, dy_b)], name="hgrn_out_dw", mode="tn", into=(gwide, 8 * d))
    dhq, dhf, dhi, dhg, dlb_h, dnorm_h = _hgrn_bwd(proj, hgrn_cols, d, o_raw, dyb, states, hgrn_lb_logits, hgrn_out_norm,
                                                   name="hgrn_scan_bwd")

    dhead = jnp.concatenate([dc_q, dc_kv, dk_r, jnp.zeros((t, d - col_kr - LANE), BF16)], axis=1)
    dparts = [dhead, dhq, dhf, dhi, dhg, dga, dgb]
    dh2 = _mm([(p, (wide, i * d, d)) for i, p in enumerate(dparts)], name="mix_in_dx", mode="nn")
    for i, p in enumerate(dparts):
        gwide = _mm([(p, h2)], name=f"mix_in_dw_{i}", mode="tn", into=(gwide, i * d))
    dx1, grads["mix_norm_pre"] = _norm_bwd(x1, mix_norm_pre, dh2, name="mix_norm_pre_bwd", dres=dx2)
    swap_m, tok = _swap_start([gwide, gnarrow], name="grad_swap_mix")
    joined2 = reduce_end(exch2, [dx1], "ffn2")

    exchanges = {}

    def mix_exchange(after):
        exchanges["mix"], started = reduce_mid(swap_m, [after], _mix_jobs(lay), "mix")
        return [started]

    dx0, grads["ffn1_norm_pre"], grads["ffn1_norm_post"] = _ffn_bwd(
        dx1, saved1, ffn1_norm_pre, ffn1_norm_post, w_ffn1, lay, "ffn1", deps=[tok], after_act=mix_exchange,
        after_dw=start_swap("ffn1"))
    exch1, tok = reduce_mid(swaps["ffn1"], [dx0], _ffn_jobs(lay), "ffn1")

    joined_m = reduce_end(exchanges["mix"], [dx0, tok], "mix")
    done = ffn_grads(joined2, "ffn2")
    grads["w_in"] = natural(jnp.concatenate([joined_m[0], joined_m[1]], axis=1), 0, lay.ncol, True)
    for i, n in enumerate(("mla_w_o", "hgrn_w_o", "w_out")):
        grads[n] = natural(joined_m[2], i * lay.r_o, lay.r_o, False)
    grads["mla_w_q_up"] = natural(joined_m[3], 0, lay.hps * (HEAD + ROPE), True)
    grads["mla_w_kv_up"] = natural(joined_m[4], 0, lay.hps * QGROUP, True)
    done += adam(["w_in", "mla_w_q_up", "mla_w_kv_up", "mla_w_o", "hgrn_w_o", "w_out"])

    joined1 = reduce_end(exch1, done, "ffn1")

    dlb = dlb_h.reshape(1, -1)
    dnorm = jnp.sum(dnorm_h, axis=0)
    small = {**{n: grads[n] for n in SMALL_WEIGHTS if n not in ("hgrn_lb_logits", "hgrn_out_norm")},
             "hgrn_lb_logits": dlb, "hgrn_out_norm": dnorm}
    vec, _ = lax.optimization_barrier((jnp.concatenate([small[n] for n in SMALL_WEIGHTS], axis=1), joined1[0]))
    vec = _all_reduce_small(vec, name="grad_all_reduce_small")
    off = 0
    for n in SMALL_WEIGHTS:
        w_n = small[n].shape[1]
        grads[n] = vec[:, off:off + w_n]
        off += w_n
    grads["hgrn_lb_logits"] = _lb_logits_grad(hgrn_lb_logits, grads["hgrn_lb_logits"], name="lb_logits_grad")

    adam(list(SMALL_WEIGHTS))
    ffn_grads(joined1, "ffn1")

    loss = lax.psum(loss_local, ("x", "y", "c"))
    dx_out = dx0.reshape(x.shape)
    return (loss, dx_out, *[grads[n] for n in ALL_WEIGHTS], *[deltas[n] for n in ALL_WEIGHTS],
            *[new_m[n] for n in ALL_WEIGHTS], *[new_v[n] for n in ALL_WEIGHTS])
```

```python
import jax
import jax.numpy as jnp
from jax import lax
from jax.experimental import pallas as pl
from jax.experimental.pallas import tpu as pltpu

F32 = jnp.float32
BF16 = jnp.bfloat16
MESH = pl.DeviceIdType.MESH

NORM_EPS = 1e-6
MACARON_SCALE = 0.5
ROPE_THETA = 10000.0
HEAD = 128
ROPE = 64
QGROUP = 2 * HEAD
SUB = 16
ADAM_LR, ADAM_B1, ADAM_B2, ADAM_EPS, ADAM_WD, ADAM_STEP = 0.001, 0.9, 0.999, 1e-08, 0.01, 10

LANE = 128
VMEM_LIMIT = 48 * 1024 * 1024
VMEM_LIMIT_WIDE = 56 * 1024 * 1024
MM_TILE = 1024
MM_TILE_WIDE = 1536

BIG_WEIGHTS = ("ffn1_w_gate", "ffn1_w_up", "ffn1_w_down", "w_in", "mla_w_q_up", "mla_w_kv_up",
               "mla_w_o", "hgrn_w_o", "w_out", "ffn2_w_gate", "ffn2_w_up", "ffn2_w_down")
SMALL_WEIGHTS = ("ffn1_norm_pre", "ffn1_norm_post", "mix_norm_pre", "mla_q_norm", "mla_kv_norm",
                 "hgrn_lb_logits", "hgrn_out_norm", "mix_norm_post", "ffn2_norm_pre", "ffn2_norm_post")
ALL_WEIGHTS = ("ffn1_norm_pre", "ffn1_w_gate", "ffn1_w_up", "ffn1_w_down", "ffn1_norm_post", "mix_norm_pre",
               "w_in", "mla_q_norm", "mla_w_q_up", "mla_kv_norm", "mla_w_kv_up", "mla_w_o", "hgrn_lb_logits",
               "hgrn_out_norm", "hgrn_w_o", "w_out", "mix_norm_post", "ffn2_norm_pre", "ffn2_w_gate",
               "ffn2_w_up", "ffn2_w_down", "ffn2_norm_post")


def _params(*sem, vmem=VMEM_LIMIT):
    return pltpu.CompilerParams(dimension_semantics=sem or None, vmem_limit_bytes=vmem)


def _pick(n, cap, offset=0):
    if n <= cap and offset % n == 0:
        return n
    best = None
    for t in range(LANE, min(n, cap) + 1, LANE):
        if n % t == 0 and offset % t == 0:
            best = t
    assert best is not None, (n, cap, offset)
    return best


def _row_tile(n, row_bytes, budget=2 << 20):
    best = None
    for t in range(8, n + 1, 8):
        if n % t == 0 and t * row_bytes <= budget:
            best = t
    return n if best is None else best


def _sigmoid(x):
    return 1.0 / (1.0 + jnp.exp(-x))


def _silu(x):
    return x * _sigmoid(x)


def _dsilu(x):
    s = _sigmoid(x)
    return s * (1.0 + x * (1.0 - s))


def _mm(pairs, *, name, mode="nn", out_dtype=F32, into=None, deps=(), extras=(), epilogue=None, out_dtypes=None, tm_cap=None,
        wide_vmem=False):
    halves = isinstance(pairs[0][1], tuple)
    assert halves or mode == "tn"
    pairs = [(a, b if halves else (b, 0, b.shape[0])) for a, b in pairs]
    a0, (b0, b_off, b_rows) = pairs[0]
    hw = b0.shape[2] if halves else (into[0].shape[2] if into is not None else None)
    if mode == "nn":
        (m, kdim), n = a0.shape, 2 * hw
    elif mode == "nt":
        (m, kdim), n = a0.shape, b_rows
        assert kdim == 2 * hw
    else:
        (kdim, m), n = a0.shape, b0.shape[1]
    out_off = 0 if into is None else into[1]
    tm = _pick(m, tm_cap or (MM_TILE_WIDE if mode == "tn" else MM_TILE), out_off)
    tn = hw if (mode == "nn" or into is not None) else _pick(n, MM_TILE_WIDE, b_off if mode == "nt" else 0)
    tk = hw if mode == "nt" else _pick(kdim, MM_TILE if len(pairs) <= 2 else MM_TILE // 2, b_off if mode == "nn" else 0)
    assert n % tn == 0 and kdim % tk == 0
    nk = kdim // tk
    npair = len(pairs)
    dims = {"nn": (((1,), (0,)), ((), ())), "nt": (((1,), (1,)), ((), ())), "tn": (((0,), (0,)), ((), ()))}[mode]

    nout = 1 if epilogue is None else len(out_dtypes)

    def body(*refs):
        ins, x_refs = refs[:2 * npair], refs[2 * npair:2 * npair + len(extras)]
        o_refs, acc_ref = refs[-1 - nout:-1], refs[-1]
        k = pl.program_id(2)

        @pl.when(k == 0)
        def _():
            acc_ref[...] = jnp.zeros_like(acc_ref)

        for p in range(npair):
            a = ins[2 * p][...].astype(BF16)
            b = ins[2 * p + 1][...].astype(BF16)
            acc_ref[...] += lax.dot_general(a, b, dims, preferred_element_type=F32)

        @pl.when(k == nk - 1)
        def _():
            outs = (acc_ref[...],) if epilogue is None else epilogue(acc_ref[...], *[x[...] for x in x_refs])
            for o_ref, o in zip(o_refs, outs):
                o_ref[...] = o.astype(o_ref.dtype)

    a_spec = pl.BlockSpec((tk, tm), lambda i, j, k: (k, i)) if mode == "tn" else pl.BlockSpec((tm, tk), lambda i, j, k: (i, k))
    in_specs, flat = [], []
    for a, (b, off, _) in pairs:
        if mode == "nt":
            b_spec = pl.BlockSpec((None, tn, tk), lambda i, j, k, o=off // tn: (k, j + o, 0))
        elif mode == "nn":
            b_spec = pl.BlockSpec((None, tk, tn), lambda i, j, k, o=off // tk: (j, k + o, 0))
        else:
            b_spec = pl.BlockSpec((tk, tn), lambda i, j, k: (k, j))
        in_specs += [a_spec, b_spec]
        flat += [a, b]
    for extra in extras:
        in_specs.append(pl.BlockSpec((tm, tn), lambda i, j, k: (i, j)))
        flat.append(extra)
    for dep in deps:
        in_specs.append(pl.BlockSpec(memory_space=pl.ANY))
        flat.append(dep)
    if epilogue is not None:
        assert into is None
        out_shape, aliases = [jax.ShapeDtypeStruct((m, n), dt) for dt in out_dtypes], {}
        out_spec = [pl.BlockSpec((tm, tn), lambda i, j, k: (i, j))] * nout
    elif into is None:
        out_shape, aliases = jax.ShapeDtypeStruct((m, n), out_dtype), {}
        out_spec = pl.BlockSpec((tm, tn), lambda i, j, k: (i, j))
    else:
        out_shape, aliases = jax.ShapeDtypeStruct(into[0].shape, into[0].dtype), {len(flat): 0}
        out_spec = pl.BlockSpec((None, tm, tn), lambda i, j, k, o=out_off // tm: (j, i + o, 0))
        in_specs.append(pl.BlockSpec(memory_space=pl.ANY))
        flat.append(into[0])
    return pl.pallas_call(
        body, name=name, grid=(m // tm, n // tn, nk),
        in_specs=in_specs,
        out_specs=out_spec,
        out_shape=out_shape, input_output_aliases=aliases,
        scratch_shapes=[pltpu.VMEM((tm, tn), F32)],
        compiler_params=_params("parallel", "parallel", "arbitrary", vmem=VMEM_LIMIT_WIDE if wide_vmem else VMEM_LIMIT),
    )(*flat)


def _norm_fwd(y, w, *, name, resid=None, scale=1.0, out_dtype=F32, col=0):
    t, d = y.shape[0], w.shape[1]
    tr = _pick(t, 256)
    assert col % d == 0

    def body(*refs):
        if resid is None:
            y_ref, w_ref, o_ref = refs
        else:
            y_ref, w_ref, r_ref, o_ref = refs
        yv = y_ref[...]
        out = yv * lax.rsqrt(jnp.mean(yv * yv, axis=-1, keepdims=True) + NORM_EPS) * w_ref[...]
        if resid is not None:
            out = r_ref[...] + scale * out
        o_ref[...] = out.astype(out_dtype)

    row = pl.BlockSpec((tr, d), lambda i: (i, 0))
    wspec = pl.BlockSpec((1, d), lambda i: (0, 0))
    ins, specs = [y, w], [pl.BlockSpec((tr, d), lambda i: (i, col // d)), wspec]
    if resid is not None:
        ins.append(resid)
        specs.append(row)
    return pl.pallas_call(
        body, name=name, grid=(t // tr,), in_specs=specs, out_specs=row,
        out_shape=jax.ShapeDtypeStruct((t, d), out_dtype), compiler_params=_params("parallel"),
    )(*ins)


def _norm_bwd(x, w, dy, *, name, scale=1.0, dres=None, col=0, dx_dtype=F32):
    t, d = x.shape[0], w.shape[1]
    tr = _pick(t, 256)
    assert col % d == 0

    def body(*refs):
        if dres is None:
            x_ref, w_ref, dy_ref, dx_ref, dw_ref = refs
        else:
            x_ref, w_ref, dy_ref, dr_ref, dx_ref, dw_ref = refs

        @pl.when(pl.program_id(0) == 0)
        def _():
            dw_ref[...] = jnp.zeros_like(dw_ref)

        xv = x_ref[...]
        r = lax.rsqrt(jnp.mean(xv * xv, axis=-1, keepdims=True) + NORM_EPS)
        xhat = xv * r
        dyv = dy_ref[...].astype(F32) * scale
        dw_ref[...] += jnp.sum(dyv * xhat, axis=0, keepdims=True)
        t_ = dyv * w_ref[...]
        dx = r * (t_ - xhat * jnp.mean(t_ * xhat, axis=-1, keepdims=True))
        if dres is not None:
            dx = dx + dr_ref[...]
        dx_ref[...] = dx.astype(dx_dtype)

    row = pl.BlockSpec((tr, d), lambda i: (i, 0))
    wspec = pl.BlockSpec((1, d), lambda i: (0, 0))
    ins, specs = [x, w, dy], [pl.BlockSpec((tr, d), lambda i: (i, col // d)), wspec, row]
    if dres is not None:
        ins.append(dres)
        specs.append(row)
    return pl.pallas_call(
        body, name=name, grid=(t // tr,), in_specs=specs, out_specs=(row, wspec),
        out_shape=(jax.ShapeDtypeStruct((t, d), dx_dtype), jax.ShapeDtypeStruct((1, d), F32)),
        compiler_params=_params("arbitrary"),
    )(*ins)


def _elementwise(fn, ins, out_dtypes, *, name, width=None, cols=None):
    t = ins[0].shape[0]
    d = ins[0].shape[1] if width is None else width
    cols = [0] * len(ins) if cols is None else cols
    tc = _pick(d, 2048)
    for c in cols:
        tc = _pick(d, tc, c)
    tr = _row_tile(t, tc * 4)
    nout = len(out_dtypes)

    def body(*refs):
        outs = fn(*[r[...].astype(F32) for r in refs[:len(ins)]])
        for o_ref, o in zip(refs[len(ins):], outs):
            o_ref[...] = o.astype(o_ref.dtype)

    spec = pl.BlockSpec((tr, tc), lambda i, j: (i, j))
    in_specs = [pl.BlockSpec((tr, tc), lambda i, j, o=c // tc: (i, j + o)) for c in cols]
    return pl.pallas_call(
        body, name=name, grid=(t // tr, d // tc), in_specs=in_specs, out_specs=[spec] * nout,
        out_shape=[jax.ShapeDtypeStruct((t, d), dt) for dt in out_dtypes],
        compiler_params=_params("parallel", "parallel"),
    )(*ins)


def _merge_fwd(proj, col_a, col_b, ya, yb, *, name):
    return _elementwise(lambda a, b, p, q: (_sigmoid(a) * p + _sigmoid(b) * q,), [proj, proj, ya, yb], [BF16],
                        name=name, width=ya.shape[1], cols=[col_a, col_b, 0, 0])[0]


def _merge_bwd(dm, proj, col_a, col_b, ya, yb, *, name):
    def fn(dmv, a, b, p, q):
        sa, sb = _sigmoid(a), _sigmoid(b)
        return dmv * p * sa * (1.0 - sa), dmv * q * sb * (1.0 - sb), dmv * sa, dmv * sb

    return _elementwise(fn, [dm, proj, proj, ya, yb], [BF16, BF16, BF16, BF16], name=name, width=ya.shape[1],
                        cols=[0, col_a, col_b, 0, 0])


def _loss_head(xo, target, *, name):
    t, d = xo.shape
    tr = _pick(t, 256)

    def body(x_ref, t_ref, dx_ref, l_ref):
        @pl.when(pl.program_id(0) == 0)
        def _():
            l_ref[...] = jnp.zeros_like(l_ref)

        err = x_ref[...] - t_ref[...]
        dx_ref[...] = err * (1.0 / d)
        l_ref[...] += 0.5 * jnp.sum(jnp.mean(err * err, axis=-1, keepdims=True), axis=0, keepdims=True)

    row = pl.BlockSpec((tr, d), lambda i: (i, 0))
    dx, l = pl.pallas_call(
        body, name=name, grid=(t // tr,), in_specs=[row, row],
        out_specs=(row, pl.BlockSpec((1, 1), lambda i: (0, 0))),
        out_shape=(jax.ShapeDtypeStruct((t, d), F32), jax.ShapeDtypeStruct((1, 1), F32)),
        compiler_params=_params("arbitrary"),
    )(xo, target)
    return dx, l[0, 0]


def _rope(xin, tabs, *, name, group, backward, out_dtype, col=0, ngroup=None):
    t = xin.shape[0]
    ngroup = xin.shape[1] // group if ngroup is None else ngroup
    wdt = ngroup * group
    tr = _pick(t, 256)
    assert col % wdt == 0
    cos_t, nsin_t, sin_t = tabs

    def body(x_ref, c_ref, n_ref, s_ref, o_ref):
        cv, nv, sv = c_ref[...], n_ref[...], s_ref[...]
        for g in range(ngroup):
            lo, hi = g * group, (g + 1) * group
            rot = x_ref[:, hi - LANE:hi].astype(F32)
            if backward:
                out = rot * cv + pltpu.roll(rot * nv, 32, 1) + pltpu.roll(rot * sv, LANE - 32, 1)
            else:
                out = rot * cv + pltpu.roll(rot, LANE - 32, 1) * nv + pltpu.roll(rot, 32, 1) * sv
            if group > LANE:
                o_ref[:, lo:hi - LANE] = x_ref[:, lo:hi - LANE].astype(out_dtype)
            o_ref[:, hi - LANE:hi] = out.astype(out_dtype)

    xspec = pl.BlockSpec((tr, wdt), lambda i: (i, 0))
    tspec = pl.BlockSpec((tr, LANE), lambda i: (i, 0))
    return pl.pallas_call(
        body, name=name, grid=(t // tr,),
        in_specs=[pl.BlockSpec((tr, wdt), lambda i: (i, col // wdt)), tspec, tspec, tspec], out_specs=xspec,
        out_shape=jax.ShapeDtypeStruct((t, wdt), out_dtype), compiler_params=_params("parallel"),
    )(xin, cos_t, nsin_t, sin_t)


def _scores(q, kv, kr, qi, tq, scale):
    kcat = jnp.concatenate([kv[:, :HEAD], kr], axis=1)
    s = lax.dot_general(q, kcat, (((1,), (1,)), ((), ())), preferred_element_type=F32) * scale
    row = qi * tq + lax.broadcasted_iota(jnp.int32, s.shape, 0)
    col = lax.broadcasted_iota(jnp.int32, s.shape, 1)
    s = jnp.where(col <= row, s, -jnp.inf)
    p = jnp.exp(s - jnp.max(s, axis=-1, keepdims=True))
    return p / jnp.sum(p, axis=-1, keepdims=True), kcat


def _attn_fwd(qcat, kv, kr, *, name, scale):
    t = qcat.shape[0]
    nh = qcat.shape[1] // QGROUP
    tq = _pick(t, 256)

    def body(q_ref, kv_ref, kr_ref, o_ref):
        for qi in range(t // tq):
            @pl.when(pl.program_id(1) == qi)
            def _(qi=qi):
                kvv = kv_ref[0:(qi + 1) * tq, :]
                p, _ = _scores(q_ref[...], kvv, kr_ref[0:(qi + 1) * tq, :], qi, tq, scale)
                o_ref[...] = jnp.dot(p.astype(BF16), kvv[:, HEAD:], preferred_element_type=F32).astype(BF16)

    return pl.pallas_call(
        body, name=name, grid=(nh, t // tq),
        in_specs=[pl.BlockSpec((tq, QGROUP), lambda h, i: (i, h)), pl.BlockSpec((t, QGROUP), lambda h, i: (0, h)),
                  pl.BlockSpec((t, LANE), lambda h, i: (0, 0))],
        out_specs=pl.BlockSpec((tq, HEAD), lambda h, i: (i, h)),
        out_shape=jax.ShapeDtypeStruct((t, nh * HEAD), BF16), compiler_params=_params("parallel", "parallel"),
    )(qcat, kv, kr)


def _attn_bwd(qcat, kv, kr, do, *, name, scale):
    t = qcat.shape[0]
    nh = qcat.shape[1] // QGROUP
    tq = _pick(t, 256)
    nq = t // tq

    def body(q_ref, kv_ref, kr_ref, do_ref, dq_ref, dkv_ref, dkr_ref, dk_acc, dv_acc):
        h, i = pl.program_id(0), pl.program_id(1)

        @pl.when(i == 0)
        def _():
            dk_acc[...] = jnp.zeros_like(dk_acc)
            dv_acc[...] = jnp.zeros_like(dv_acc)

        @pl.when((i == 0) & (h == 0))
        def _():
            dkr_ref[...] = jnp.zeros_like(dkr_ref)

        for qi in range(nq):
            @pl.when(i == qi)
            def _(qi=qi):
                keys = slice(0, (qi + 1) * tq)
                q = q_ref[...]
                kvv = kv_ref[keys, :]
                dov = do_ref[...].astype(BF16)
                p, kcat = _scores(q, kvv, kr_ref[keys, :], qi, tq, scale)
                dp = lax.dot_general(dov, kvv[:, HEAD:], (((1,), (1,)), ((), ())), preferred_element_type=F32)
                ds = (p * (dp - jnp.sum(p * dp, axis=-1, keepdims=True)) * scale).astype(BF16)
                dq_ref[...] = jnp.dot(ds, kcat, preferred_element_type=F32)
                dk_acc[keys, :] += lax.dot_general(ds, q, (((0,), (0,)), ((), ())), preferred_element_type=F32)
                dv_acc[keys, :] += lax.dot_general(p.astype(BF16), dov, (((0,), (0,)), ((), ())), preferred_element_type=F32)

        @pl.when(i == nq - 1)
        def _():
            dk = dk_acc[...]
            dkv_ref[...] = jnp.concatenate([dk[:, :HEAD], dv_acc[...]], axis=1)
            dkr_ref[...] += dk[:, HEAD:]

    return pl.pallas_call(
        body, name=name, grid=(nh, nq),
        in_specs=[pl.BlockSpec((tq, QGROUP), lambda h, i: (i, h)), pl.BlockSpec((t, QGROUP), lambda h, i: (0, h)),
                  pl.BlockSpec((t, LANE), lambda h, i: (0, 0)), pl.BlockSpec((tq, HEAD), lambda h, i: (i, h))],
        out_specs=(pl.BlockSpec((tq, QGROUP), lambda h, i: (i, h)), pl.BlockSpec((t, QGROUP), lambda h, i: (0, h)),
                   pl.BlockSpec((t, LANE), lambda h, i: (0, 0))),
        out_shape=(jax.ShapeDtypeStruct((t, nh * QGROUP), F32), jax.ShapeDtypeStruct((t, nh * QGROUP), F32),
                   jax.ShapeDtypeStruct((t, LANE), F32)),
        scratch_shapes=[pltpu.VMEM((t, QGROUP), F32), pltpu.VMEM((t, HEAD), F32)],
        compiler_params=_params("arbitrary", "arbitrary"),
    )(qcat, kv, kr, do)


def _split3(x):
    hi = x.astype(BF16)
    r1 = x - hi.astype(F32)
    mid = r1.astype(BF16)
    lo = (r1 - mid.astype(F32)).astype(BF16)
    return hi, mid, lo


def _tri_matmul(mask, x):
    m = mask.astype(BF16)
    return sum(jnp.dot(m, part, preferred_element_type=F32) for part in _split3(x))


def _sub_cumsum(g, tb):
    row = lax.broadcasted_iota(jnp.int32, (tb, tb), 0)
    col = lax.broadcasted_iota(jnp.int32, (tb, tb), 1)
    return _tri_matmul(jnp.where((col <= row) & (col // SUB == row // SUB), 1.0, 0.0), g)


def _sub_suffix_prefix(after, before, tb):
    row = lax.broadcasted_iota(jnp.int32, (tb, tb), 0)
    col = lax.broadcasted_iota(jnp.int32, (tb, tb), 1)
    same = col // SUB == row // SUB
    return (_tri_matmul(jnp.where((col >= row) & same, 1.0, 0.0), after)
            + _tri_matmul(jnp.where((col < row) & same, 1.0, 0.0), before))


def _lower_bound(logits):
    mx = jnp.max(logits, axis=0, keepdims=True)
    e = jnp.exp(logits - mx)
    return e[0:1, :] / jnp.sum(e, axis=0, keepdims=True)


def _hgrn_fwd(proj, cols, wdt, logits, out_norm, *, name):
    t = proj.shape[0]
    nh = wdt // HEAD
    tb = _pick(t, 128)
    ns = tb // SUB

    def body(hq_ref, hf_ref, hi_ref, hg_ref, lg_ref, w_ref, o_ref, yb_ref, st_ref, s_ref, q_s, k_s, b_s):
        @pl.when(pl.program_id(1) == 0)
        def _():
            s_ref[...] = jnp.zeros_like(s_ref)

        lb = _lower_bound(lg_ref[...])
        f = lb + (1.0 - lb) * _sigmoid(hf_ref[...])
        q_s[...] = _silu(hq_ref[...])
        k_s[...] = 1.0 - f
        b_s[...] = _sub_cumsum(jnp.log(f), tb)
        rowid = lax.broadcasted_iota(jnp.int32, (SUB, HEAD), 0)

        def sub(c, st):
            rows = pl.ds(pl.multiple_of(c * SUB, SUB), SUB)
            qc, kc, bc, vc = q_s[rows, :], k_s[rows, :], b_s[rows, :], hi_ref[rows, :]
            st_ref[0, c] = st
            bl = bc[SUB - 1:SUB, :]
            oc = lax.dot_general((qc * jnp.exp(bc)).astype(BF16), st.astype(BF16), (((1,), (1,)), ((), ())),
                                 preferred_element_type=F32)
            for s in range(SUB):
                e = jnp.where(rowid >= s, jnp.exp(bc - bc[s:s + 1, :]), 0.0)
                a = jnp.sum(qc * e * kc[s:s + 1, :], axis=1, keepdims=True)
                oc = oc + a * vc[s:s + 1, :]
            o_ref[rows, :] = oc
            kd = kc * jnp.exp(bl - bc)
            return jnp.exp(bl) * st + lax.dot_general(vc.astype(BF16), kd.astype(BF16), (((0,), (0,)), ((), ())),
                                                      preferred_element_type=F32)

        s_ref[...] = lax.fori_loop(0, ns, sub, s_ref[...], unroll=True)
        o = o_ref[...]
        r = lax.rsqrt(jnp.mean(o * o, axis=-1, keepdims=True) + NORM_EPS)
        yb_ref[...] = (o * r * w_ref[...] * _silu(hg_ref[...])).astype(BF16)

    blk = pl.BlockSpec((tb, HEAD), lambda h, j: (j, h))
    return pl.pallas_call(
        body, name=name, grid=(nh, t // tb),
        in_specs=[pl.BlockSpec((tb, HEAD), lambda h, j, o=c // HEAD: (j, h + o)) for c in cols]
        + [pl.BlockSpec((2, HEAD), lambda h, j: (0, h)), pl.BlockSpec((1, HEAD), lambda h, j: (0, 0))],
        out_specs=(blk, blk, pl.BlockSpec((1, ns, HEAD, HEAD), lambda h, j: (h, j, 0, 0))),
        out_shape=(jax.ShapeDtypeStruct((t, wdt), F32), jax.ShapeDtypeStruct((t, wdt), BF16),
                   jax.ShapeDtypeStruct((nh, t // SUB, HEAD, HEAD), F32)),
        scratch_shapes=[pltpu.VMEM((HEAD, HEAD), F32)] + [pltpu.VMEM((tb, HEAD), F32)] * 3,
        compiler_params=_params("parallel", "arbitrary"),
    )(proj, proj, proj, proj, logits, out_norm)


def _hgrn_bwd(proj, cols, wdt, o_raw, dyb, states, logits, out_norm, *, name):
    t = proj.shape[0]
    nh = wdt // HEAD
    tb = _pick(t, 128)
    ns = tb // SUB
    nb = t // tb

    def body(hq_ref, hf_ref, hi_ref, hg_ref, o_ref, dy_ref, st_ref, lg_ref, w_ref,
             dhq_ref, dhf_ref, dhi_ref, dhg_ref, dlb_ref, dw_ref,
             ds_ref, q_s, k_s, b_s, do_s, dq_s, dk_s, dv_s, after_s, before_s, thru_s):
        @pl.when(pl.program_id(1) == 0)
        def _():
            ds_ref[...] = jnp.zeros_like(ds_ref)
            dlb_ref[...] = jnp.zeros_like(dlb_ref)
            dw_ref[...] = jnp.zeros_like(dw_ref)

        lb = _lower_bound(lg_ref[...])
        hqv, hgv = hq_ref[...], hg_ref[...]
        sig = _sigmoid(hf_ref[...])
        f = lb + (1.0 - lb) * sig
        q_s[...] = _silu(hqv)
        k_s[...] = 1.0 - f
        b_s[...] = _sub_cumsum(jnp.log(f), tb)

        o = o_ref[...]
        r = lax.rsqrt(jnp.mean(o * o, axis=-1, keepdims=True) + NORM_EPS)
        nrm = o * r
        w = w_ref[...]
        dy = dy_ref[...].astype(F32)
        dhg_ref[...] = (dy * nrm * w * _dsilu(hgv)).astype(BF16)
        dnw = dy * _silu(hgv)
        dw_ref[0] += jnp.sum(dnw * nrm, axis=0, keepdims=True)
        tt = dnw * w
        do_s[...] = r * (tt - nrm * jnp.mean(tt * nrm, axis=-1, keepdims=True))
        rowid = lax.broadcasted_iota(jnp.int32, (SUB, HEAD), 0)

        def sub(cc, dst):
            c = ns - 1 - cc
            rows = pl.ds(pl.multiple_of(c * SUB, SUB), SUB)
            qc, kc, bc, vc, doc = q_s[rows, :], k_s[rows, :], b_s[rows, :], hi_ref[rows, :], do_s[rows, :]
            st = st_ref[0, c]
            bl = bc[SUB - 1:SUB, :]
            eb = jnp.exp(bc)
            ekd = jnp.exp(bl - bc)
            qe, kd = qc * eb, kc * ekd
            dob, vcb = doc.astype(BF16), vc.astype(BF16)
            dq_st = jnp.dot(dob, st.astype(BF16), preferred_element_type=F32) * eb
            dk_st = jnp.dot(vcb, dst.astype(BF16), preferred_element_type=F32) * ekd
            dv = lax.dot_general(kd.astype(BF16), dst.astype(BF16), (((1,), (1,)), ((), ())), preferred_element_type=F32)
            dq_in = jnp.zeros_like(qc)
            dk_in = jnp.zeros_like(qc)
            for s in range(SUB):
                e = jnp.where(rowid >= s, jnp.exp(bc - bc[s:s + 1, :]), 0.0)
                ek = e * kc[s:s + 1, :]
                a = jnp.sum(qc * ek, axis=1, keepdims=True)
                da = jnp.sum(doc * vc[s:s + 1, :], axis=1, keepdims=True)
                dq_in = dq_in + da * ek
                dk_in = dk_in + jnp.where(rowid == s, jnp.sum(da * e * qc, axis=0, keepdims=True), 0.0)
                dv = dv + jnp.where(rowid == s, jnp.sum(a * doc, axis=0, keepdims=True), 0.0)
            ebl = jnp.exp(bl)
            dq_s[rows, :] = dq_st + dq_in
            dk_s[rows, :] = dk_st + dk_in
            dv_s[rows, :] = dv
            after_s[rows, :] = qc * (dq_st + dq_in) - kc * dk_in
            before_s[rows, :] = kc * dk_st
            thru_s[rows, :] = jnp.broadcast_to(ebl * jnp.sum(st * dst, axis=0, keepdims=True), (SUB, HEAD))
            return ebl * dst + lax.dot_general(dob, qe.astype(BF16), (((0,), (0,)), ((), ())), preferred_element_type=F32)

        ds_ref[...] = lax.fori_loop(0, ns, sub, ds_ref[...], unroll=True)
        dg = _sub_suffix_prefix(after_s[...], before_s[...], tb) + thru_s[...]
        dhq_ref[...] = (dq_s[...] * _dsilu(hqv)).astype(BF16)
        dft = dg / f - dk_s[...]
        dhf_ref[...] = (dft * (1.0 - lb) * sig * (1.0 - sig)).astype(BF16)
        dlb_ref[0] += jnp.sum(dft * (1.0 - sig), axis=0, keepdims=True)
        dhi_ref[...] = dv_s[...].astype(BF16)

    blk = pl.BlockSpec((tb, HEAD), lambda h, j: (nb - 1 - j, h))
    vec = pl.BlockSpec((1, 1, HEAD), lambda h, j: (h, 0, 0))
    tok = jax.ShapeDtypeStruct((t, wdt), BF16)
    per_head = jax.ShapeDtypeStruct((nh, 1, HEAD), F32)
    return pl.pallas_call(
        body, name=name, grid=(nh, nb),
        in_specs=[pl.BlockSpec((tb, HEAD), lambda h, j, o=c // HEAD: (nb - 1 - j, h + o)) for c in cols]
        + [blk, blk] + [pl.BlockSpec((1, ns, HEAD, HEAD), lambda h, j: (h, nb - 1 - j, 0, 0)),
                              pl.BlockSpec((2, HEAD), lambda h, j: (0, h)), pl.BlockSpec((1, HEAD), lambda h, j: (0, 0))],
        out_specs=(blk, blk, blk, blk, vec, vec),
        out_shape=(tok, tok, tok, tok, per_head, per_head),
        scratch_shapes=[pltpu.VMEM((HEAD, HEAD), F32)] + [pltpu.VMEM((tb, HEAD), F32)] * 10,
        compiler_params=_params("arbitrary", "arbitrary"),
    )(proj, proj, proj, proj, o_raw, dyb, states, logits, out_norm)


def _lb_logits_grad(logits, dlb, *, name):
    def body(lg_ref, d_ref, o_ref):
        lg = lg_ref[...]
        e = jnp.exp(lg - jnp.max(lg, axis=0, keepdims=True))
        p = e / jnp.sum(e, axis=0, keepdims=True)
        d = d_ref[...]
        rowid = lax.broadcasted_iota(jnp.int32, lg.shape, 0)
        dp = jnp.where(rowid == 0, d, 0.0)
        o_ref[...] = p * (dp - jnp.sum(p * dp, axis=0, keepdims=True))

    return pl.pallas_call(body, name=name, out_shape=jax.ShapeDtypeStruct(logits.shape, F32))(logits, dlb)


def _adamw(w, g, m, v, *, name, deps=()):
    r, c = w.shape
    tc = _pick(c, 2048) if c % LANE == 0 else c
    tr = _row_tile(r, tc * 4)

    def body(w_ref, g_ref, m_ref, v_ref, *rest):
        d_ref, nm_ref, nv_ref = rest[-3:]
        gv = g_ref[...]
        nm = ADAM_B1 * m_ref[...] + (1.0 - ADAM_B1) * gv
        nv = ADAM_B2 * v_ref[...] + (1.0 - ADAM_B2) * (gv * gv)
        m_hat = nm / (1.0 - ADAM_B1 ** ADAM_STEP)
        v_hat = nv / (1.0 - ADAM_B2 ** ADAM_STEP)
        d_ref[...] = -ADAM_LR * (m_hat / (jnp.sqrt(v_hat) + ADAM_EPS) + ADAM_WD * w_ref[...])
        nm_ref[...] = nm
        nv_ref[...] = nv

    spec = pl.BlockSpec((tr, tc), lambda i, j: (i, j))
    shp = jax.ShapeDtypeStruct((r, c), F32)
    return pl.pallas_call(
        body, name=name, grid=(r // tr, c // tc), in_specs=[spec] * 4 + [ANY] * len(deps), out_specs=[spec] * 3,
        out_shape=[shp, shp, shp], compiler_params=_params("parallel", "parallel"),
    )(w, g, m, v, *deps)


def _coords():
    return lax.axis_index("x"), lax.axis_index("y"), lax.axis_index("c")


def _other_chips(x, y):
    return [(1 - x, y), (x, 1 - y), (1 - x, 1 - y)]


ANY = pl.BlockSpec(memory_space=pl.ANY)


class _Layout:
    def __init__(self, d, dff, in_cols, q_lora, kv_lora, nh):
        assert q_lora == kv_lora and nh % 4 == 0 and dff % (4 * LANE) == 0 and in_cols % 4 == 0 and d % 4 == 0
        self.d, self.dff, self.q_lora, self.nh = d, dff, q_lora, nh
        self.head = q_lora + kv_lora + ROPE
        self.pad = d - self.head
        self.nff, self.ncol, self.r_o, self.hps = dff // 4, in_cols // 4, d // 4, nh // 4
        assert self.head <= self.ncol
        self.off_q, self.off_kv, self.rows_narrow = 0, nh * QGROUP, 2 * nh * QGROUP


HBM = pl.BlockSpec(memory_space=pltpu.HBM)
SEMS = pl.BlockSpec(memory_space=pltpu.SEMAPHORE)
SPLIT = dict(has_side_effects=pltpu.SideEffectType.DATAFLOW_SIDE_EFFECTING)


def _in_hbm(a):
    return pltpu.with_memory_space_constraint(a, pltpu.HBM)


def _shard_rows(jobs, k):
    out, lrow = [], [0] * (1 + max(job.a for job in jobs))
    for job in jobs:
        for row, rows in job.pieces(k):
            out.append((job.a, lrow[job.a], row, rows))
            lrow[job.a] += rows
    return out


def _shard_total(jobs, a):
    return sum(rows for b, _, _, rows in _shard_rows(jobs, 0) if b == a)


def _gather_start(packs, lands, jobs, *, name, deps=()):
    n = len(packs)

    def body(*refs):
        p_refs, l_refs, send, recv, token = refs[:n], refs[n:2 * n], refs[-2 * n - 3], refs[-2 * n - 2], refs[-1]
        x, y, c = _coords()
        for a, lrow, row, rows in _shard_rows(jobs, 2 * x + y):
            pltpu.make_async_remote_copy(
                src_ref=p_refs[a].at[:, pl.ds(lrow, rows)], dst_ref=l_refs[a].at[:, pl.ds(row, rows)],
                send_sem=send.at[4 * a + 3], recv_sem=recv.at[4 * a + 3], device_id=(x, y, 1 - c), device_id_type=MESH).start()
            for j, (px, py) in enumerate(_other_chips(x, y)):
                pltpu.make_async_remote_copy(
                    src_ref=p_refs[a].at[c, pl.ds(lrow, rows)], dst_ref=l_refs[a].at[c, pl.ds(row, rows)],
                    send_sem=send.at[4 * a + j], recv_sem=recv.at[4 * a + j], device_id=(px, py, c), device_id_type=MESH).start()
        token[...] = jnp.zeros_like(token)

    thru = [pltpu.HBM(a.shape, a.dtype) for a in packs + lands]
    out = pl.pallas_call(
        body, name=name, in_specs=[HBM] * (2 * n) + [ANY] * len(deps),
        out_shape=(pltpu.SemaphoreType.DMA((4 * n,)), pltpu.SemaphoreType.DMA((4 * n,)), *thru, jax.ShapeDtypeStruct((8, LANE), F32)),
        out_specs=(SEMS, SEMS, *[HBM] * (2 * n), pl.BlockSpec(memory_space=pltpu.VMEM)),
        input_output_aliases={i: 2 + i for i in range(2 * n)}, compiler_params=pltpu.CompilerParams(**SPLIT),
    )(*[_in_hbm(a) for a in packs + lands], *deps)
    return dict(send=out[0], recv=out[1], bufs=list(out[2:2 + 2 * n]), n=n, jobs=jobs), out[-1]


def _gather_wait(handle, after, *, name):
    n, jobs = handle["n"], handle["jobs"]

    def body(*refs):
        l_refs, send, recv, token = refs[n:2 * n], refs[2 * n], refs[2 * n + 1], refs[-1]
        token[...] = jnp.zeros_like(token)
        x, y, c = _coords()
        for a in range(n):
            total = _shard_total(jobs, a)
            for j, like in enumerate([l_refs[a].at[0, pl.ds(0, total)]] * 3 + [l_refs[a].at[:, pl.ds(0, total)]]):
                cp = pltpu.make_async_remote_copy(src_ref=like, dst_ref=like, send_sem=send.at[4 * a + j],
                                                  recv_sem=recv.at[4 * a + j], device_id=(x, y, c), device_id_type=MESH)
                cp.wait_send()
                cp.wait_recv()

    out = pl.pallas_call(
        body, name=name, in_specs=[HBM] * (2 * n) + [SEMS, SEMS] + [ANY] * len(after),
        out_shape=[pltpu.HBM(a.shape, a.dtype) for a in handle["bufs"]] + [jax.ShapeDtypeStruct((8, LANE), F32)],
        out_specs=[HBM] * (2 * n) + [pl.BlockSpec(memory_space=pltpu.VMEM)],
        input_output_aliases={i: i for i in range(2 * n)}, compiler_params=pltpu.CompilerParams(**SPLIT),
    )(*handle["bufs"], handle["send"], handle["recv"], *after)
    return list(out[n:2 * n]), out[-1]


def _gather_forward(lands, jobs, *, name, deps=()):
    n = len(lands)

    def body(*refs):
        l_refs, send, recv = refs[n + len(deps):2 * n + len(deps)], refs[-2], refs[-1]
        x, y, c = _coords()
        for j, (px, py) in enumerate(_other_chips(x, y)):
            for a, _, row, rows in _shard_rows(jobs, 2 * px + py):
                blk = l_refs[a].at[c, pl.ds(row, rows)]
                pltpu.make_async_remote_copy(src_ref=blk, dst_ref=blk, send_sem=send.at[3 * a + j], recv_sem=recv.at[3 * a + j],
                                             device_id=(x, y, 1 - c), device_id_type=MESH).start()
        for a in range(n):
            like = l_refs[a].at[0, pl.ds(0, _shard_total(jobs, a))]
            for j in range(3):
                cp = pltpu.make_async_remote_copy(src_ref=like, dst_ref=like, send_sem=send.at[3 * a + j],
                                                  recv_sem=recv.at[3 * a + j], device_id=(x, y, c), device_id_type=MESH)
                cp.wait_send()
                cp.wait_recv()

    sem = pltpu.SemaphoreType.DMA((3 * n,))
    return pl.pallas_call(
        body, name=name, in_specs=[ANY] * (n + len(deps)), out_specs=[ANY] * n, input_output_aliases={i: i for i in range(n)},
        out_shape=[jax.ShapeDtypeStruct(a.shape, a.dtype) for a in lands], scratch_shapes=[sem, sem],
    )(*lands, *deps)


def _forward_start(lands, jobs, *, name, deps=()):
    n, nd = len(lands), len(deps)

    def body(*refs):
        l_refs, sems, token = refs[:n], refs[n + nd:n + nd + 2 * n], refs[-1]
        x, y, c = _coords()
        for j, (px, py) in enumerate(_other_chips(x, y)):
            for a, _, row, rows in _shard_rows(jobs, 2 * px + py):
                blk = l_refs[a].at[c, pl.ds(row, rows)]
                pltpu.make_async_remote_copy(src_ref=blk, dst_ref=blk, send_sem=sems[2 * a].at[j], recv_sem=sems[2 * a + 1].at[j],
                                             device_id=(x, y, 1 - c), device_id_type=MESH).start()
        token[...] = jnp.zeros_like(token)

    out = pl.pallas_call(
        body, name=name, in_specs=[HBM] * n + [ANY] * nd,
        out_shape=(*[pltpu.SemaphoreType.DMA((3,))] * (2 * n), *[pltpu.HBM(a.shape, a.dtype) for a in lands],
                   jax.ShapeDtypeStruct((8, LANE), F32)),
        out_specs=(*[SEMS] * (2 * n), *[HBM] * n, pl.BlockSpec(memory_space=pltpu.VMEM)),
        input_output_aliases={i: 2 * n + i for i in range(n)}, compiler_params=pltpu.CompilerParams(**SPLIT),
    )(*[_in_hbm(a) for a in lands], *deps)
    return [dict(send=out[2 * a], recv=out[2 * a + 1], buf=out[2 * n + a]) for a in range(n)], out[-1]


def _forward_wait(handle, jobs, a, after, *, name):
    total = _shard_total(jobs, a)

    def body(l_ref, send, recv, *rest):
        x, y, c = _coords()
        like = l_ref.at[0, pl.ds(0, total)]
        for j in range(3):
            cp = pltpu.make_async_remote_copy(src_ref=like, dst_ref=like, send_sem=send.at[j], recv_sem=recv.at[j],
                                              device_id=(x, y, c), device_id_type=MESH)
            cp.wait_send()
            cp.wait_recv()

    buf = handle["buf"]
    return pl.pallas_call(
        body, name=name, in_specs=[HBM, SEMS, SEMS] + [ANY] * len(after), out_shape=pltpu.HBM(buf.shape, buf.dtype),
        out_specs=HBM, input_output_aliases={0: 0}, compiler_params=pltpu.CompilerParams(**SPLIT),
    )(buf, handle["send"], handle["recv"], *after)


def _add_sibling(g, recv, sel, *, name):
    rows, hw = recv.shape
    tr = _row_tile(rows, hw * 4)

    def body(sel_ref, g_ref, r_ref, o_ref):
        o_ref[...] = (g_ref[...] + r_ref[...]).astype(BF16)

    return pl.pallas_call(
        body, name=name, out_shape=jax.ShapeDtypeStruct((rows, hw), BF16),
        grid_spec=pltpu.PrefetchScalarGridSpec(
            num_scalar_prefetch=1, grid=(rows // tr,),
            in_specs=[pl.BlockSpec((None, tr, hw), lambda i, s: (s[0], i, 0)), pl.BlockSpec((tr, hw), lambda i, s: (i, 0))],
            out_specs=pl.BlockSpec((tr, hw), lambda i, s: (i, 0))),
        compiler_params=_params("parallel"),
    )(sel, g, recv)


class _Job:
    def __init__(self, a, blk, n_outer, n_inner, stride, start):
        self.a, self.blk, self.n_outer, self.n_inner, self.stride, self.start = a, blk, n_outer, n_inner, stride, start
        self.rows_out = n_outer * n_inner * blk

    def pieces(self, k):
        return [(self.start(k) + o * self.stride * self.blk, self.n_inner * self.blk) for o in range(self.n_outer)]


def _block_rows(rows, cap, *also):
    best = None
    for b in range(16, min(rows, cap) + 1, 16):
        if rows % b == 0 and all(v % b == 0 for v in also):
            best = b
    assert best is not None, (rows, also)
    return best


def _ffn_jobs(lay):
    b = _block_rows(lay.nff, 704, lay.dff)
    return [_Job(0, b, 3, lay.nff // b, lay.dff // b, lambda k: lay.nff * k)]


def _ffn_weight_jobs(lay):
    b = _block_rows(lay.nff, 704)
    return [_Job(a, b, 1, lay.nff // b, 0, lambda k: lay.nff * k) for a in range(3)]


def _mix_jobs(lay):
    d, ncol, head, pad = lay.d, lay.ncol, lay.head, lay.pad
    first = lambda k, a, b: jnp.where(k == 0, a, b) if not isinstance(k, int) else (a if k == 0 else b)
    ba = _block_rows(head, 704, *[ncol * k + pad for k in (1, 2, 3)])
    bb = _block_rows(ncol - head, 704, *[ncol * k + d for k in (0, 1, 2, 3)])
    bo = _block_rows(lay.r_o, 704, d)
    bq = _block_rows(HEAD + ROPE, 704, QGROUP)
    bk = _block_rows(lay.hps * QGROUP, 704, lay.off_kv)
    return [_Job(0, ba, 1, head // ba, 0, lambda k: first(k, 0, ncol * k + pad)),
            _Job(0, bb, 1, (ncol - head) // bb, 0, lambda k: ncol * k + d),
            _Job(0, bo, 3, lay.r_o // bo, d // bo, lambda k: 7 * d + lay.r_o * k),
            _Job(1, bq, lay.hps, (HEAD + ROPE) // bq, QGROUP // bq, lambda k: QGROUP * lay.hps * k),
            _Job(1, bk, 1, lay.hps * QGROUP // bk, 0, lambda k: lay.off_kv + lay.hps * QGROUP * k)]


def _swap_start(gs, *, name):
    n = len(gs)
    lands = [lax.empty(g.shape[1:], g.dtype) for g in gs]

    def body(*refs):
        g_refs, land_refs, send, recv, token = refs[:n], refs[n:2 * n], refs[2 * n], refs[2 * n + 1], refs[-1]
        x, y, c = _coords()
        for a in range(n):
            pltpu.make_async_remote_copy(src_ref=g_refs[a].at[1 - c], dst_ref=land_refs[a], send_sem=send.at[a],
                                         recv_sem=recv.at[a], device_id=(x, y, 1 - c), device_id_type=MESH).start()
        token[...] = jnp.zeros_like(token)

    thru = [pltpu.HBM(a.shape, a.dtype) for a in gs + lands]
    out = pl.pallas_call(
        body, name=name, in_specs=[HBM] * (2 * n),
        out_shape=(pltpu.SemaphoreType.DMA((n,)), pltpu.SemaphoreType.DMA((n,)), *thru, jax.ShapeDtypeStruct((8, LANE), F32)),
        out_specs=(SEMS, SEMS, *[HBM] * (2 * n), pl.BlockSpec(memory_space=pltpu.VMEM)),
        input_output_aliases={i: 2 + i for i in range(2 * n)}, compiler_params=pltpu.CompilerParams(**SPLIT),
    )(*[_in_hbm(a) for a in gs + lands])
    return dict(send=out[0], recv=out[1], bufs=list(out[2:2 + 2 * n]), n=n), out[-1]


def _swap_wait(handle, after, *, name):
    n = handle["n"]

    def body(*refs):
        g_refs, land_refs, send, recv = refs[:n], refs[n:2 * n], refs[2 * n], refs[2 * n + 1]
        x, y, c = _coords()
        for a in range(n):
            cp = pltpu.make_async_remote_copy(src_ref=g_refs[a].at[1 - c], dst_ref=land_refs[a], send_sem=send.at[a],
                                              recv_sem=recv.at[a], device_id=(x, y, 1 - c), device_id_type=MESH)
            cp.wait_send()
            cp.wait_recv()

    out = pl.pallas_call(
        body, name=name, in_specs=[HBM] * (2 * n) + [SEMS, SEMS] + [ANY] * len(after),
        out_shape=[pltpu.HBM(a.shape, a.dtype) for a in handle["bufs"]], out_specs=[HBM] * (2 * n),
        input_output_aliases={i: i for i in range(2 * n)}, compiler_params=pltpu.CompilerParams(**SPLIT),
    )(*handle["bufs"], handle["send"], handle["recv"], *after)
    return list(out[:n]), list(out[n:])


def _exchange_start(ss, jobs, *, name):
    n = len(ss)
    lands = [lax.empty((3,) + s.shape, s.dtype) for s in ss]

    def body(*refs):
        s_refs, land_refs, send, recv, token = refs[:n], refs[n:2 * n], refs[2 * n], refs[2 * n + 1], refs[-1]
        x, y, c = _coords()
        for j, (px, py) in enumerate(_other_chips(x, y)):
            for job in jobs:
                for row, rows in job.pieces(2 * px + py):
                    pltpu.make_async_remote_copy(
                        src_ref=s_refs[job.a].at[pl.ds(row, rows)], dst_ref=land_refs[job.a].at[j, pl.ds(row, rows)],
                        send_sem=send.at[n * j + job.a], recv_sem=recv.at[n * j + job.a], device_id=(px, py, c),
                        device_id_type=MESH).start()
        token[...] = jnp.zeros_like(token)

    thru = [pltpu.HBM(a.shape, a.dtype) for a in ss + lands]
    out = pl.pallas_call(
        body, name=name, in_specs=[HBM] * (2 * n),
        out_shape=(pltpu.SemaphoreType.DMA((3 * n,)), pltpu.SemaphoreType.DMA((3 * n,)), *thru, jax.ShapeDtypeStruct((8, LANE), F32)),
        out_specs=(SEMS, SEMS, *[HBM] * (2 * n), pl.BlockSpec(memory_space=pltpu.VMEM)),
        input_output_aliases={i: 2 + i for i in range(2 * n)}, compiler_params=pltpu.CompilerParams(**SPLIT),
    )(*[_in_hbm(a) for a in ss + lands])
    return dict(send=out[0], recv=out[1], bufs=list(out[2:2 + 2 * n]), n=n, jobs=jobs), out[-1]


def _exchange_wait(handle, after, *, name):
    n, jobs = handle["n"], handle["jobs"]
    total = [sum(rows for job in jobs if job.a == a for _, rows in job.pieces(0)) for a in range(n)]

    def body(*refs):
        s_refs, land_refs, send, recv = refs[:n], refs[n:2 * n], refs[2 * n], refs[2 * n + 1]
        x, y, c = _coords()
        for a in range(n):
            for j in range(3):
                all_rows = land_refs[a].at[0, pl.ds(0, total[a])]
                cp = pltpu.make_async_remote_copy(src_ref=all_rows, dst_ref=all_rows, send_sem=send.at[n * j + a],
                                                  recv_sem=recv.at[n * j + a], device_id=(x, y, c), device_id_type=MESH)
                cp.wait_send()
                cp.wait_recv()

    out = pl.pallas_call(
        body, name=name, in_specs=[HBM] * (2 * n) + [SEMS, SEMS] + [ANY] * len(after),
        out_shape=[pltpu.HBM(a.shape, a.dtype) for a in handle["bufs"]], out_specs=[HBM] * (2 * n),
        input_output_aliases={i: i for i in range(2 * n)}, compiler_params=pltpu.CompilerParams(**SPLIT),
    )(*handle["bufs"], handle["send"], handle["recv"], *after)
    return list(out[:n]), list(out[n:])


def _add_shard(s, land, job, sel, k, *, name):
    hw = s.shape[1]
    blk, no, ni, stride = job.blk, job.n_outer, job.n_inner, job.stride
    scal = jnp.stack([sel, job.start(k) // blk]).astype(jnp.int32)

    def body(sc_ref, own_ref, r_ref, o_ref):
        o_ref[...] = ((own_ref[...].astype(F32) + r_ref[0].astype(F32)) + r_ref[1].astype(F32)) + r_ref[2].astype(F32)

    return pl.pallas_call(
        body, name=name, out_shape=jax.ShapeDtypeStruct((2, job.rows_out, hw), F32),
        grid_spec=pltpu.PrefetchScalarGridSpec(
            num_scalar_prefetch=1, grid=(no, ni),
            in_specs=[pl.BlockSpec((blk, hw), lambda o, b, sc: (sc[1] + o * stride + b, 0)),
                      pl.BlockSpec((3, blk, hw), lambda o, b, sc: (0, sc[1] + o * stride + b, 0))],
            out_specs=pl.BlockSpec((None, blk, hw), lambda o, b, sc: (sc[0], o * ni + b, 0))),
        compiler_params=_params("parallel", "parallel"),
    )(scal, s, land)


def _join_list(fs, *, name):
    n = len(fs)

    def body(*refs):
        f_refs, send_sems, recv_sems = refs[n:2 * n], refs[2 * n], refs[2 * n + 1]
        x, y, c = _coords()
        copies = [pltpu.make_async_remote_copy(
            src_ref=f.at[c], dst_ref=f.at[c], send_sem=send_sems.at[a], recv_sem=recv_sems.at[a],
            device_id=(x, y, 1 - c), device_id_type=MESH) for a, f in enumerate(f_refs)]
        for cp in copies:
            cp.start()
        for cp in copies:
            cp.wait()

    sem = pltpu.SemaphoreType.DMA((n,))
    return pl.pallas_call(
        body, name=name, in_specs=[ANY] * n, out_specs=[ANY] * n, input_output_aliases={i: i for i in range(n)},
        out_shape=[jax.ShapeDtypeStruct(f.shape, f.dtype) for f in fs], scratch_shapes=[sem, sem],
    )(*fs)


def _all_reduce_small(vec, *, name):
    n = vec.shape[1]

    def body(v_ref, o_ref, buf, send_sems, recv_sems):
        x, y, c = _coords()
        me = 4 * x + 2 * y + c
        buf[me] = v_ref[...]
        copies = []
        for m in range(1, 8):
            peer = (x ^ ((m >> 2) & 1), y ^ ((m >> 1) & 1), c ^ (m & 1))
            copies.append(pltpu.make_async_remote_copy(
                src_ref=v_ref, dst_ref=buf.at[me], send_sem=send_sems.at[m - 1], recv_sem=recv_sems.at[m - 1],
                device_id=peer, device_id_type=MESH))
        for cp in copies:
            cp.start()
        for cp in copies:
            cp.wait()
        acc = buf[0]
        for d in range(1, 8):
            acc = acc + buf[d]
        o_ref[...] = acc

    return pl.pallas_call(
        body, name=name, out_shape=jax.ShapeDtypeStruct((1, n), F32),
        in_specs=[pl.BlockSpec(memory_space=pltpu.VMEM)], out_specs=pl.BlockSpec(memory_space=pltpu.VMEM),
        scratch_shapes=[pltpu.VMEM((8, 1, n), F32), pltpu.SemaphoreType.DMA((7,)), pltpu.SemaphoreType.DMA((7,))],
    )(vec)


def _ffn_fwd(x, n_pre, n_post, weight, lay, tag):
    h = _norm_fwd(x, n_pre, name=f"{tag}_norm_pre", out_dtype=BF16)
    wg = (weight(0, [h]), 0, lay.dff)
    g = _mm([(h, wg)], name=f"{tag}_gate", mode="nt")
    wu = (weight(1, [g]), 0, lay.dff)
    u, a = _mm([(h, wu)], name=f"{tag}_up", mode="nt", extras=[g], out_dtypes=[F32, BF16], wide_vmem=True,
               epilogue=lambda up, gate: (up, _silu(gate) * up))
    wd = (weight(2, [u]), 0, lay.dff)
    yv = _mm([(a, wd)], name=f"{tag}_down", mode="nn")
    out = _norm_fwd(yv, n_post, name=f"{tag}_norm_post", resid=x, scale=MACARON_SCALE)
    return out, (x, h, g, u, a, yv), (wg, wu, wd)


def _ffn_bwd(dout, saved, n_pre, n_post, weights, lay, tag, deps=(), after_act=None, after_dw=None):
    x, h, g, u, a, yv = saved
    dff = lay.dff
    gbuf = lax.empty((2, 3 * dff, lay.d // 2), F32)
    dy, dn_post = _norm_bwd(yv, n_post, dout, name=f"{tag}_norm_post_bwd", scale=MACARON_SCALE)
    wg, wu, wd = weights
    dg, du = _mm([(dy, wd)], name=f"{tag}_down_dx", mode="nt", deps=deps, extras=[g, u],
                 out_dtypes=[BF16, BF16], tm_cap=MM_TILE // 2,
                 epilogue=lambda da, gate, up: (da * up * _dsilu(gate), da * _silu(gate)))
    deps = after_act(du) if after_act is not None else ()
    gbuf = _mm([(a, dy)], name=f"{tag}_down_dw", mode="tn", into=(gbuf, 2 * dff), deps=deps)
    gbuf = _mm([(dg, h)], name=f"{tag}_gate_dw", mode="tn", into=(gbuf, 0))
    gbuf = _mm([(du, h)], name=f"{tag}_up_dw", mode="tn", into=(gbuf, dff))
    deps = after_dw(gbuf)
    dh = _mm([(dg, wg), (du, wu)], name=f"{tag}_up_dx", mode="nn", deps=deps)
    dx, dn_pre = _norm_bwd(x, n_pre, dh, name=f"{tag}_norm_pre_bwd", dres=dout)
    return dx, dn_pre, dn_post


def _rope_tables(positions):
    half = ROPE // 2
    inv_freq = ROPE_THETA ** (-jnp.arange(half, dtype=F32) / half)
    ang = positions.astype(F32)[:, None] * inv_freq
    cos, sin = jnp.cos(ang), jnp.sin(ang)
    z = jnp.zeros_like(cos)
    z2 = jnp.zeros((positions.shape[0], LANE - ROPE), F32)
    return (jnp.concatenate([cos, cos, z2], axis=1), jnp.concatenate([-sin, z, z2], axis=1),
            jnp.concatenate([z, sin, z2], axis=1))


def kernel(x, positions, ffn1_norm_pre, ffn1_w_gate, ffn1_w_up, ffn1_w_down, ffn1_norm_post, mix_norm_pre, w_in, mla_q_norm, mla_w_q_up, mla_kv_norm, mla_w_kv_up, mla_w_o, hgrn_lb_logits, hgrn_out_norm, hgrn_w_o, w_out, mix_norm_post, ffn2_norm_pre, ffn2_w_gate, ffn2_w_up, ffn2_w_down, ffn2_norm_post, loss_target, m_ffn1_norm_pre, m_ffn1_w_gate, m_ffn1_w_up, m_ffn1_w_down, m_ffn1_norm_post, m_mix_norm_pre, m_w_in, m_mla_q_norm, m_mla_w_q_up, m_mla_kv_norm, m_mla_w_kv_up, m_mla_w_o, m_hgrn_lb_logits, m_hgrn_out_norm, m_hgrn_w_o, m_w_out, m_mix_norm_post, m_ffn2_norm_pre, m_ffn2_w_gate, m_ffn2_w_up, m_ffn2_w_down, m_ffn2_norm_post, v_ffn1_norm_pre, v_ffn1_w_gate, v_ffn1_w_up, v_ffn1_w_down, v_ffn1_norm_post, v_mix_norm_pre, v_w_in, v_mla_q_norm, v_mla_w_q_up, v_mla_kv_norm, v_mla_w_kv_up, v_mla_w_o, v_hgrn_lb_logits, v_hgrn_out_norm, v_hgrn_w_o, v_w_out, v_mix_norm_post, v_ffn2_norm_pre, v_ffn2_w_gate, v_ffn2_w_up, v_ffn2_w_down, v_ffn2_norm_post):
    given = dict(locals())
    wts = {n: given[n] for n in ALL_WEIGHTS}
    mom = {n: given["m_" + n] for n in ALL_WEIGHTS}
    var = {n: given["v_" + n] for n in ALL_WEIGHTS}
    xin = x[0]
    target = loss_target[0]
    t, d = xin.shape
    cx, cy, cc = _coords()

    q_lora, kv_lora = mla_q_norm.shape[1], mla_kv_norm.shape[1]
    nh_mla = 4 * mla_w_kv_up.shape[2] // QGROUP
    lay = _Layout(d, 4 * ffn1_w_gate.shape[2], 4 * w_in.shape[2], q_lora, kv_lora, nh_mla)
    jobs_mix = _mix_jobs(lay)
    def pack(src, col_sharded, row_sharded=()):
        a = jnp.concatenate([src[n][0].T.astype(BF16) for n in col_sharded] + [src[n][0].astype(BF16) for n in row_sharded])
        return a.reshape(a.shape[0], 2, a.shape[1] // 2).transpose(1, 0, 2)

    jobs_w = _ffn_weight_jobs(lay)
    ffn_packs = lambda src, tag: [pack(src, [f"{tag}_w_gate"]), pack(src, [f"{tag}_w_up"]), pack(src, [], [f"{tag}_w_down"])]
    ffn_lands = lambda: [lax.empty((2, lay.dff, d // 2), BF16) for _ in range(3)]

    def handed_over(handles, tag):
        return lambda i, after: _forward_wait(handles[i], jobs_w, i, after, name=f"gather_{tag}_forward_wait_{i}")

    got1, tok = _gather_start(ffn_packs(wts, "ffn1"), ffn_lands(), jobs_w, name="gather_ffn1")
    later, _ = lax.optimization_barrier(({n: wts[n] for n in BIG_WEIGHTS if not n.startswith("ffn1")}, tok))
    packs_mix = [pack(later, ["w_in"], ["mla_w_o", "hgrn_w_o", "w_out"]), pack(later, ["mla_w_q_up", "mla_w_kv_up"])]
    packs_ffn2 = ffn_packs(later, "ffn2")
    lands_mix = [jnp.zeros((2, 10 * d, d // 2), BF16), jnp.zeros((2, lay.rows_narrow, q_lora // 2), BF16)]
    arrived, tok = _gather_wait(got1, packs_mix + packs_ffn2 + lands_mix, name="gather_ffn1_wait")
    got_m, tok = _gather_start(packs_mix, lands_mix, jobs_mix, name="gather_mix", deps=[tok])
    handing1, _ = _forward_start(arrived[:1], jobs_w[:1], name="gather_ffn1_forward_gate", deps=[tok])

    def ffn1_weight(i, after):
        if i == 0:
            gate = _forward_wait(handing1[0], jobs_w, 0, after, name="gather_ffn1_forward_wait_0")
            handing1.extend(_forward_start(arrived[1:], jobs_w[:2], name="gather_ffn1_forward_rest", deps=[gate])[0])
            return gate
        return _forward_wait(handing1[i], jobs_w, 0, after, name=f"gather_ffn1_forward_wait_{i}")
    col_kr = q_lora + kv_lora
    hgrn_cols = [d, 2 * d, 3 * d, 4 * d]
    col_ga, col_gb = 5 * d, 6 * d
    tabs = _rope_tables(positions[0])
    scale = (HEAD + ROPE) ** -0.5

    x1, saved1, w_ffn1 = _ffn_fwd(xin, ffn1_norm_pre, ffn1_norm_post, ffn1_weight, lay, "ffn1")

    arrived, tok = _gather_wait(got_m, [x1], name="gather_mix_wait")
    got2, tok = _gather_start(packs_ffn2, ffn_lands(), jobs_w, name="gather_ffn2", deps=[tok])
    wide, narrow = _gather_forward(arrived, jobs_mix, name="gather_mix_forward", deps=[tok])
    w_in_v = (wide, 0, 7 * d)
    w_o_v = {n: (wide, (7 + i) * d, d) for i, n in enumerate(("mla_w_o", "hgrn_w_o", "w_out"))}
    w_q_v = (narrow, lay.off_q, nh_mla * QGROUP)
    w_kv_v = (narrow, lay.off_kv, nh_mla * QGROUP)

    h2 = _norm_fwd(x1, mix_norm_pre, name="mix_norm_pre", out_dtype=BF16)
    proj = _mm([(h2, w_in_v)], name="mix_in", mode="nt", deps=[tok])
    cqn = _norm_fwd(proj, mla_q_norm, name="mla_q_norm", out_dtype=BF16, col=0)
    ckvn = _norm_fwd(proj, mla_kv_norm, name="mla_kv_norm", out_dtype=BF16, col=q_lora)
    qp = _mm([(cqn, w_q_v)], name="mla_q_up", mode="nt")
    kvb = _mm([(ckvn, w_kv_v)], name="mla_kv_up", mode="nt", out_dtype=BF16)
    qcat = _rope(qp, tabs, name="rope_q", group=QGROUP, backward=False, out_dtype=BF16)
    krot = _rope(proj, tabs, name="rope_k", group=LANE, backward=False, out_dtype=BF16, col=col_kr, ngroup=1)
    o_mla = _attn_fwd(qcat, kvb, krot, name="mla_attention", scale=scale)
    y_a = _mm([(o_mla, w_o_v["mla_w_o"])], name="mla_out", mode="nn")

    o_raw, yb, states = _hgrn_fwd(proj, hgrn_cols, d, hgrn_lb_logits, hgrn_out_norm, name="hgrn_scan")
    handing2, tok = _forward_start(_gather_wait(got2, [o_raw], name="gather_ffn2_wait")[0], jobs_w, name="gather_ffn2_forward")
    y_b = _mm([(yb, w_o_v["hgrn_w_o"])], name="hgrn_out", mode="nn", deps=[tok])

    merged = _merge_fwd(proj, col_ga, col_gb, y_a, y_b, name="mix_merge")
    y_mix = _mm([(merged, w_o_v["w_out"])], name="mix_out", mode="nn")
    x2 = _norm_fwd(y_mix, mix_norm_post, name="mix_norm_post", resid=x1, scale=1.0)

    x3, saved2, w_ffn2 = _ffn_fwd(x2, ffn2_norm_pre, ffn2_norm_post, handed_over(handing2, "ffn2"), lay, "ffn2")
    dx3, loss_local = _loss_head(x3, target, name="loss_head")

    grads, deltas, new_m, new_v = {}, {}, {}, {}
    sel = cc.astype(jnp.int32)
    sel1 = jnp.reshape(sel, (1,))
    me_chip = (2 * cx + cy).astype(jnp.int32)

    def reduce_mid(handle, after, jobs, tag):
        bufs, recvd = _swap_wait(handle, after, name=f"grad_swap_{tag}_wait")
        sums = [_add_sibling(b, r, sel1, name=f"grad_add_sibling_{tag}_{i}") for i, (b, r) in enumerate(zip(bufs, recvd))]
        return _exchange_start(sums, jobs, name=f"grad_exchange_{tag}")

    def reduce_end(handle, after, tag):
        sums, lands = _exchange_wait(handle, after, name=f"grad_exchange_{tag}_wait")
        parts = [_add_shard(sums[job.a], lands[job.a], job, sel, me_chip, name=f"grad_add_chips_{tag}_{i}")
                 for i, job in enumerate(handle["jobs"])]
        return _join_list(parts, name=f"grad_join_{tag}")

    def natural(part, lo, rows, transposed):
        g_n = part[:, lo:lo + rows]
        hw_n = g_n.shape[2]
        return g_n.transpose(0, 2, 1).reshape(2 * hw_n, rows) if transposed else g_n.transpose(1, 0, 2).reshape(rows, 2 * hw_n)

    def adam(names, deps=()):
        for i, n in enumerate(names):
            shp = wts[n].shape
            two_d = (lambda a: a[0]) if n in BIG_WEIGHTS else (lambda a: a)
            dl, nm, nv = _adamw(two_d(wts[n]), grads[n], two_d(mom[n]), two_d(var[n]), name=f"adamw_{n}",
                                deps=deps if i == 0 else ())
            grads[n] = grads[n].reshape(shp)
            deltas[n], new_m[n], new_v[n] = dl.reshape(shp), nm.reshape(shp), nv.reshape(shp)
        return [deltas[n] for n in names]

    def ffn_grads(joined, tag, deps=()):
        nff = lay.nff
        grads[f"{tag}_w_gate"] = natural(joined[0], 0, nff, True)
        grads[f"{tag}_w_up"] = natural(joined[0], nff, nff, True)
        grads[f"{tag}_w_down"] = natural(joined[0], 2 * nff, nff, False)
        return adam([f"{tag}_w_gate", f"{tag}_w_up", f"{tag}_w_down"], deps)

    swaps = {}

    def start_swap(tag):
        def hook(gbuf):
            swaps[tag], started = _swap_start([gbuf], name=f"grad_swap_{tag}")
            return [started]
        return hook

    dx2, grads["ffn2_norm_pre"], grads["ffn2_norm_post"] = _ffn_bwd(
        dx3, saved2, ffn2_norm_pre, ffn2_norm_post, w_ffn2, lay, "ffn2", after_dw=start_swap("ffn2"))

    gwide = lax.empty((2, 10 * d, d // 2), F32)
    gnarrow = lax.empty((2, lay.rows_narrow, q_lora // 2), F32)
    dy_mix, grads["mix_norm_post"] = _norm_bwd(y_mix, mix_norm_post, dx2, name="mix_norm_post_bwd")
    dmerged = _mm([(dy_mix, w_o_v["w_out"])], name="mix_out_dx", mode="nt")
    gwide = _mm([(merged, dy_mix)], name="mix_out_dw", mode="tn", into=(gwide, 9 * d))
    dga, dgb, dy_a, dy_b = _merge_bwd(dmerged, proj, col_ga, col_gb, y_a, y_b, name="mix_merge_bwd")

    do_mla = _mm([(dy_a, w_o_v["mla_w_o"])], name="mla_out_dx", mode="nt")
    gwide = _mm([(o_mla, dy_a)], name="mla_out_dw", mode="tn", into=(gwide, 7 * d))
    dqcat, dkv, dkr = _attn_bwd(qcat, kvb, krot, do_mla, name="mla_attention_bwd", scale=scale)
    exch2, tok = reduce_mid(swaps["ffn2"], [dkr], _ffn_jobs(lay), "ffn2")

    dqp = _rope(dqcat, tabs, name="rope_q_bwd", group=QGROUP, backward=True, out_dtype=BF16)
    dk_r = _rope(dkr, tabs, name="rope_k_bwd", group=LANE, backward=True, out_dtype=BF16)
    dcqn = _mm([(dqp, w_q_v)], name="mla_q_up_dx", mode="nn", deps=[tok])
    gnarrow = _mm([(dqp, cqn)], name="mla_q_up_dw", mode="tn", into=(gnarrow, lay.off_q))
    dkvb = dkv.astype(BF16)
    dckvn = _mm([(dkvb, w_kv_v)], name="mla_kv_up_dx", mode="nn")
    gnarrow = _mm([(dkvb, ckvn)], name="mla_kv_up_dw", mode="tn", into=(gnarrow, lay.off_kv))
    dc_q, grads["mla_q_norm"] = _norm_bwd(proj, mla_q_norm, dcqn, name="mla_q_norm_bwd", col=0, dx_dtype=BF16)
    dc_kv, grads["mla_kv_norm"] = _norm_bwd(proj, mla_kv_norm, dckvn, name="mla_kv_norm_bwd", col=q_lora, dx_dtype=BF16)

    dyb = _mm([(dy_b, w_o_v["hgrn_w_o"])], name="hgrn_out_dx", mode="nt")
    gwide = _mm([(yb, dy_b)], name="hgrn_out_dw", mode="tn", into=(gwide, 8 * d))
    dhq, dhf, dhi, dhg, dlb_h, dnorm_h = _hgrn_bwd(proj, hgrn_cols, d, o_raw, dyb, states, hgrn_lb_logits, hgrn_out_norm,
                                                   name="hgrn_scan_bwd")

    dhead = jnp.concatenate([dc_q, dc_kv, dk_r, jnp.zeros((t, d - col_kr - LANE), BF16)], axis=1)
    dparts = [dhead, dhq, dhf, dhi, dhg, dga, dgb]
    dh2 = _mm([(p, (wide, i * d, d)) for i, p in enumerate(dparts)], name="mix_in_dx", mode="nn")
    for i, p in enumerate(dparts):
        gwide = _mm([(p, h2)], name=f"mix_in_dw_{i}", mode="tn", into=(gwide, i * d))
    dx1, grads["mix_norm_pre"] = _norm_bwd(x1, mix_norm_pre, dh2, name="mix_norm_pre_bwd", dres=dx2)
    swap_m, tok = _swap_start([gwide, gnarrow], name="grad_swap_mix")
    joined2 = reduce_end(exch2, [dx1], "ffn2")

    exchanges = {}

    def mix_exchange(after):
        exchanges["mix"], started = reduce_mid(swap_m, [after], _mix_jobs(lay), "mix")
        return [started]

    dx0, grads["ffn1_norm_pre"], grads["ffn1_norm_post"] = _ffn_bwd(
        dx1, saved1, ffn1_norm_pre, ffn1_norm_post, w_ffn1, lay, "ffn1", deps=[tok], after_act=mix_exchange,
        after_dw=start_swap("ffn1"))
    exch1, tok = reduce_mid(swaps["ffn1"], [dx0], _ffn_jobs(lay), "ffn1")

    joined_m = reduce_end(exchanges["mix"], [dx0, tok], "mix")
    done = ffn_grads(joined2, "ffn2")
    grads["w_in"] = natural(jnp.concatenate([joined_m[0], joined_m[1]], axis=1), 0, lay.ncol, True)
    for i, n in enumerate(("mla_w_o", "hgrn_w_o", "w_out")):
        grads[n] = natural(joined_m[2], i * lay.r_o, lay.r_o, False)
    grads["mla_w_q_up"] = natural(joined_m[3], 0, lay.hps * (HEAD + ROPE), True)
    grads["mla_w_kv_up"] = natural(joined_m[4], 0, lay.hps * QGROUP, True)
    done += adam(["w_in", "mla_w_q_up", "mla_w_kv_up", "mla_w_o", "hgrn_w_o", "w_out"])

    joined1 = reduce_end(exch1, done, "ffn1")

    dlb = dlb_h.reshape(1, -1)
    dnorm = jnp.sum(dnorm_h, axis=0)
    small = {**{n: grads[n] for n in SMALL_WEIGHTS if n not in ("hgrn_lb_logits", "hgrn_out_norm")},
             "hgrn_lb_logits": dlb, "hgrn_out_norm": dnorm}
    vec, _ = lax.optimization_barrier((jnp.concatenate([small[n] for n in SMALL_WEIGHTS], axis=1), joined1[0]))
    vec = _all_reduce_small(vec, name="grad_all_reduce_small")
    off = 0
    for n in SMALL_WEIGHTS:
        w_n = small[n].shape[1]
        grads[n] = vec[:, off:off + w_n]
        off += w_n
    grads["hgrn_lb_logits"] = _lb_logits_grad(hgrn_lb_logits, grads["hgrn_lb_logits"], name="lb_logits_grad")

    adam(list(SMALL_WEIGHTS))
    ffn_grads(joined1, "ffn1")

    loss = lax.psum(loss_local, ("x", "y", "c"))
    dx_out = dx0.reshape(x.shape)
    return (loss, dx_out, *[grads[n] for n in ALL_WEIGHTS], *[deltas[n] for n in ALL_WEIGHTS],
            *[new_m[n] for n in ALL_WEIGHTS], *[new_v[n] for n in ALL_WEIGHTS])
```

```python
import jax
import jax.numpy as jnp
from jax import lax
from jax.experimental import pallas as pl
from jax.experimental.pallas import tpu as pltpu

F32 = jnp.float32
BF16 = jnp.bfloat16
MESH = pl.DeviceIdType.MESH

NORM_EPS = 1e-6
MACARON_SCALE = 0.5
ROPE_THETA = 10000.0
HEAD = 128
ROPE = 64
QGROUP = 2 * HEAD
SUB = 16
ADAM_LR, ADAM_B1, ADAM_B2, ADAM_EPS, ADAM_WD, ADAM_STEP = 0.001, 0.9, 0.999, 1e-08, 0.01, 10

LANE = 128
VMEM_LIMIT = 48 * 1024 * 1024
VMEM_LIMIT_WIDE = 56 * 1024 * 1024
MM_TILE = 1024
MM_TILE_WIDE = 1536

BIG_WEIGHTS = ("ffn1_w_gate", "ffn1_w_up", "ffn1_w_down", "w_in", "mla_w_q_up", "mla_w_kv_up",
               "mla_w_o", "hgrn_w_o", "w_out", "ffn2_w_gate", "ffn2_w_up", "ffn2_w_down")
SMALL_WEIGHTS = ("ffn1_norm_pre", "ffn1_norm_post", "mix_norm_pre", "mla_q_norm", "mla_kv_norm",
                 "hgrn_lb_logits", "hgrn_out_norm", "mix_norm_post", "ffn2_norm_pre", "ffn2_norm_post")
ALL_WEIGHTS = ("ffn1_norm_pre", "ffn1_w_gate", "ffn1_w_up", "ffn1_w_down", "ffn1_norm_post", "mix_norm_pre",
               "w_in", "mla_q_norm", "mla_w_q_up", "mla_kv_norm", "mla_w_kv_up", "mla_w_o", "hgrn_lb_logits",
               "hgrn_out_norm", "hgrn_w_o", "w_out", "mix_norm_post", "ffn2_norm_pre", "ffn2_w_gate",
               "ffn2_w_up", "ffn2_w_down", "ffn2_norm_post")


def _params(*sem, vmem=VMEM_LIMIT):
    return pltpu.CompilerParams(dimension_semantics=sem or None, vmem_limit_bytes=vmem)


def _pick(n, cap, offset=0):
    if n <= cap and offset % n == 0:
        return n
    best = None
    for t in range(LANE, min(n, cap) + 1, LANE):
        if n % t == 0 and offset % t == 0:
            best = t
    assert best is not None, (n, cap, offset)
    return best


def _row_tile(n, row_bytes, budget=2 << 20):
    best = None
    for t in range(8, n + 1, 8):
        if n % t == 0 and t * row_bytes <= budget:
            best = t
    return n if best is None else best


def _sigmoid(x):
    return 1.0 / (1.0 + jnp.exp(-x))


def _silu(x):
    return x * _sigmoid(x)


def _dsilu(x):
    s = _sigmoid(x)
    return s * (1.0 + x * (1.0 - s))


def _mm(pairs, *, name, mode="nn", out_dtype=F32, into=None, deps=(), extras=(), epilogue=None, out_dtypes=None, tm_cap=None,
        wide_vmem=False):
    halves = isinstance(pairs[0][1], tuple)
    assert halves or mode == "tn"
    pairs = [(a, b if halves else (b, 0, b.shape[0])) for a, b in pairs]
    a0, (b0, b_off, b_rows) = pairs[0]
    hw = b0.shape[2] if halves else (into[0].shape[2] if into is not None else None)
    if mode == "nn":
        (m, kdim), n = a0.shape, 2 * hw
    elif mode == "nt":
        (m, kdim), n = a0.shape, b_rows
        assert kdim == 2 * hw
    else:
        (kdim, m), n = a0.shape, b0.shape[1]
    out_off = 0 if into is None else into[1]
    tm = _pick(m, tm_cap or (MM_TILE_WIDE if mode == "tn" else MM_TILE), out_off)
    tn = hw if (mode == "nn" or into is not None) else _pick(n, MM_TILE_WIDE, b_off if mode == "nt" else 0)
    tk = hw if mode == "nt" else _pick(kdim, MM_TILE if len(pairs) <= 2 else MM_TILE // 2, b_off if mode == "nn" else 0)
    assert n % tn == 0 and kdim % tk == 0
    nk = kdim // tk
    npair = len(pairs)
    dims = {"nn": (((1,), (0,)), ((), ())), "nt": (((1,), (1,)), ((), ())), "tn": (((0,), (0,)), ((), ()))}[mode]

    nout = 1 if epilogue is None else len(out_dtypes)

    def body(*refs):
        ins, x_refs = refs[:2 * npair], refs[2 * npair:2 * npair + len(extras)]
        o_refs, acc_ref = refs[-1 - nout:-1], refs[-1]
        k = pl.program_id(2)

        @pl.when(k == 0)
        def _():
            acc_ref[...] = jnp.zeros_like(acc_ref)

        for p in range(npair):
            a = ins[2 * p][...].astype(BF16)
            b = ins[2 * p + 1][...].astype(BF16)
            acc_ref[...] += lax.dot_general(a, b, dims, preferred_element_type=F32)

        @pl.when(k == nk - 1)
        def _():
            outs = (acc_ref[...],) if epilogue is None else epilogue(acc_ref[...], *[x[...] for x in x_refs])
            for o_ref, o in zip(o_refs, outs):
                o_ref[...] = o.astype(o_ref.dtype)

    a_spec = pl.BlockSpec((tk, tm), lambda i, j, k: (k, i)) if mode == "tn" else pl.BlockSpec((tm, tk), lambda i, j, k: (i, k))
    in_specs, flat = [], []
    for a, (b, off, _) in pairs:
        if mode == "nt":
            b_spec = pl.BlockSpec((None, tn, tk), lambda i, j, k, o=off // tn: (k, j + o, 0))
        elif mode == "nn":
            b_spec = pl.BlockSpec((None, tk, tn), lambda i, j, k, o=off // tk: (j, k + o, 0))
        else:
            b_spec = pl.BlockSpec((tk, tn), lambda i, j, k: (k, j))
        in_specs += [a_spec, b_spec]
        flat += [a, b]
    for extra in extras:
        in_specs.append(pl.BlockSpec((tm, tn), lambda i, j, k: (i, j)))
        flat.append(extra)
    for dep in deps:
        in_specs.append(pl.BlockSpec(memory_space=pl.ANY))
        flat.append(dep)
    if epilogue is not None:
        assert into is None
        out_shape, aliases = [jax.ShapeDtypeStruct((m, n), dt) for dt in out_dtypes], {}
        out_spec = [pl.BlockSpec((tm, tn), lambda i, j, k: (i, j))] * nout
    elif into is None:
        out_shape, aliases = jax.ShapeDtypeStruct((m, n), out_dtype), {}
        out_spec = pl.BlockSpec((tm, tn), lambda i, j, k: (i, j))
    else:
        out_shape, aliases = jax.ShapeDtypeStruct(into[0].shape, into[0].dtype), {len(flat): 0}
        out_spec = pl.BlockSpec((None, tm, tn), lambda i, j, k, o=out_off // tm: (j, i + o, 0))
        in_specs.append(pl.BlockSpec(memory_space=pl.ANY))
        flat.append(into[0])
    return pl.pallas_call(
        body, name=name, grid=(m // tm, n // tn, nk),
        in_specs=in_specs,
        out_specs=out_spec,
        out_shape=out_shape, input_output_aliases=aliases,
        scratch_shapes=[pltpu.VMEM((tm, tn), F32)],
        compiler_params=_params("parallel", "parallel", "arbitrary", vmem=VMEM_LIMIT_WIDE if wide_vmem else VMEM_LIMIT),
    )(*flat)


def _norm_fwd(y, w, *, name, resid=None, scale=1.0, out_dtype=F32, col=0):
    t, d = y.shape[0], w.shape[1]
    tr = _pick(t, 256)
    assert col % d == 0

    def body(*refs):
        if resid is None:
            y_ref, w_ref, o_ref = refs
        else:
            y_ref, w_ref, r_ref, o_ref = refs
        yv = y_ref[...]
        out = yv * lax.rsqrt(jnp.mean(yv * yv, axis=-1, keepdims=True) + NORM_EPS) * w_ref[...]
        if resid is not None:
            out = r_ref[...] + scale * out
        o_ref[...] = out.astype(out_dtype)

    row = pl.BlockSpec((tr, d), lambda i: (i, 0))
    wspec = pl.BlockSpec((1, d), lambda i: (0, 0))
    ins, specs = [y, w], [pl.BlockSpec((tr, d), lambda i: (i, col // d)), wspec]
    if resid is not None:
        ins.append(resid)
        specs.append(row)
    return pl.pallas_call(
        body, name=name, grid=(t // tr,), in_specs=specs, out_specs=row,
        out_shape=jax.ShapeDtypeStruct((t, d), out_dtype), compiler_params=_params("parallel"),
    )(*ins)


def _norm_bwd(x, w, dy, *, name, scale=1.0, dres=None, col=0, dx_dtype=F32):
    t, d = x.shape[0], w.shape[1]
    tr = _pick(t, 256)
    assert col % d == 0

    def body(*refs):
        if dres is None:
            x_ref, w_ref, dy_ref, dx_ref, dw_ref = refs
        else:
            x_ref, w_ref, dy_ref, dr_ref, dx_ref, dw_ref = refs

        @pl.when(pl.program_id(0) == 0)
        def _():
            dw_ref[...] = jnp.zeros_like(dw_ref)

        xv = x_ref[...]
        r = lax.rsqrt(jnp.mean(xv * xv, axis=-1, keepdims=True) + NORM_EPS)
        xhat = xv * r
        dyv = dy_ref[...].astype(F32) * scale
        dw_ref[...] += jnp.sum(dyv * xhat, axis=0, keepdims=True)
        t_ = dyv * w_ref[...]
        dx = r * (t_ - xhat * jnp.mean(t_ * xhat, axis=-1, keepdims=True))
        if dres is not None:
            dx = dx + dr_ref[...]
        dx_ref[...] = dx.astype(dx_dtype)

    row = pl.BlockSpec((tr, d), lambda i: (i, 0))
    wspec = pl.BlockSpec((1, d), lambda i: (0, 0))
    ins, specs = [x, w, dy], [pl.BlockSpec((tr, d), lambda i: (i, col // d)), wspec, row]
    if dres is not None:
        ins.append(dres)
        specs.append(row)
    return pl.pallas_call(
        body, name=name, grid=(t // tr,), in_specs=specs, out_specs=(row, wspec),
        out_shape=(jax.ShapeDtypeStruct((t, d), dx_dtype), jax.ShapeDtypeStruct((1, d), F32)),
        compiler_params=_params("arbitrary"),
    )(*ins)


def _elementwise(fn, ins, out_dtypes, *, name, width=None, cols=None):
    t = ins[0].shape[0]
    d = ins[0].shape[1] if width is None else width
    cols = [0] * len(ins) if cols is None else cols
    tc = _pick(d, 2048)
    for c in cols:
        tc = _pick(d, tc, c)
    tr = _row_tile(t, tc * 4)
    nout = len(out_dtypes)

    def body(*refs):
        outs = fn(*[r[...].astype(F32) for r in refs[:len(ins)]])
        for o_ref, o in zip(refs[len(ins):], outs):
            o_ref[...] = o.astype(o_ref.dtype)

    spec = pl.BlockSpec((tr, tc), lambda i, j: (i, j))
    in_specs = [pl.BlockSpec((tr, tc), lambda i, j, o=c // tc: (i, j + o)) for c in cols]
    return pl.pallas_call(
        body, name=name, grid=(t // tr, d // tc), in_specs=in_specs, out_specs=[spec] * nout,
        out_shape=[jax.ShapeDtypeStruct((t, d), dt) for dt in out_dtypes],
        compiler_params=_params("parallel", "parallel"),
    )(*ins)


def _merge_fwd(proj, col_a, col_b, ya, yb, *, name):
    return _elementwise(lambda a, b, p, q: (_sigmoid(a) * p + _sigmoid(b) * q,), [proj, proj, ya, yb], [BF16],
                        name=name, width=ya.shape[1], cols=[col_a, col_b, 0, 0])[0]


def _merge_bwd(dm, proj, col_a, col_b, ya, yb, *, name):
    def fn(dmv, a, b, p, q):
        sa, sb = _sigmoid(a), _sigmoid(b)
        return dmv * p * sa * (1.0 - sa), dmv * q * sb * (1.0 - sb), dmv * sa, dmv * sb

    return _elementwise(fn, [dm, proj, proj, ya, yb], [BF16, BF16, BF16, BF16], name=name, width=ya.shape[1],
                        cols=[0, col_a, col_b, 0, 0])


def _loss_head(xo, target, *, name):
    t, d = xo.shape
    tr = _pick(t, 256)

    def body(x_ref, t_ref, dx_ref, l_ref):
        @pl.when(pl.program_id(0) == 0)
        def _():
            l_ref[...] = jnp.zeros_like(l_ref)

        err = x_ref[...] - t_ref[...]
        dx_ref[...] = err * (1.0 / d)
        l_ref[...] += 0.5 * jnp.sum(jnp.mean(err * err, axis=-1, keepdims=True), axis=0, keepdims=True)

    row = pl.BlockSpec((tr, d), lambda i: (i, 0))
    dx, l = pl.pallas_call(
        body, name=name, grid=(t // tr,), in_specs=[row, row],
        out_specs=(row, pl.BlockSpec((1, 1), lambda i: (0, 0))),
        out_shape=(jax.ShapeDtypeStruct((t, d), F32), jax.ShapeDtypeStruct((1, 1), F32)),
        compiler_params=_params("arbitrary"),
    )(xo, target)
    return dx, l[0, 0]


def _rope(xin, tabs, *, name, group, backward, out_dtype, col=0, ngroup=None):
    t = xin.shape[0]
    ngroup = xin.shape[1] // group if ngroup is None else ngroup
    wdt = ngroup * group
    tr = _pick(t, 256)
    assert col % wdt == 0
    cos_t, nsin_t, sin_t = tabs

    def body(x_ref, c_ref, n_ref, s_ref, o_ref):
        cv, nv, sv = c_ref[...], n_ref[...], s_ref[...]
        for g in range(ngroup):
            lo, hi = g * group, (g + 1) * group
            rot = x_ref[:, hi - LANE:hi].astype(F32)
            if backward:
                out = rot * cv + pltpu.roll(rot * nv, 32, 1) + pltpu.roll(rot * sv, LANE - 32, 1)
            else:
                out = rot * cv + pltpu.roll(rot, LANE - 32, 1) * nv + pltpu.roll(rot, 32, 1) * sv
            if group > LANE:
                o_ref[:, lo:hi - LANE] = x_ref[:, lo:hi - LANE].astype(out_dtype)
            o_ref[:, hi - LANE:hi] = out.astype(out_dtype)

    xspec = pl.BlockSpec((tr, wdt), lambda i: (i, 0))
    tspec = pl.BlockSpec((tr, LANE), lambda i: (i, 0))
    return pl.pallas_call(
        body, name=name, grid=(t // tr,),
        in_specs=[pl.BlockSpec((tr, wdt), lambda i: (i, col // wdt)), tspec, tspec, tspec], out_specs=xspec,
        out_shape=jax.ShapeDtypeStruct((t, wdt), out_dtype), compiler_params=_params("parallel"),
    )(xin, cos_t, nsin_t, sin_t)


def _scores(q, kv, kr, qi, tq, scale):
    kcat = jnp.concatenate([kv[:, :HEAD], kr], axis=1)
    s = lax.dot_general(q, kcat, (((1,), (1,)), ((), ())), preferred_element_type=F32) * scale
    row = qi * tq + lax.broadcasted_iota(jnp.int32, s.shape, 0)
    col = lax.broadcasted_iota(jnp.int32, s.shape, 1)
    s = jnp.where(col <= row, s, -jnp.inf)
    p = jnp.exp(s - jnp.max(s, axis=-1, keepdims=True))
    return p / jnp.sum(p, axis=-1, keepdims=True), kcat


def _attn_fwd(qcat, kv, kr, *, name, scale):
    t = qcat.shape[0]
    nh = qcat.shape[1] // QGROUP
    tq = _pick(t, 256)

    def body(q_ref, kv_ref, kr_ref, o_ref):
        for qi in range(t // tq):
            @pl.when(pl.program_id(1) == qi)
            def _(qi=qi):
                kvv = kv_ref[0:(qi + 1) * tq, :]
                p, _ = _scores(q_ref[...], kvv, kr_ref[0:(qi + 1) * tq, :], qi, tq, scale)
                o_ref[...] = jnp.dot(p.astype(BF16), kvv[:, HEAD:], preferred_element_type=F32).astype(BF16)

    return pl.pallas_call(
        body, name=name, grid=(nh, t // tq),
        in_specs=[pl.BlockSpec((tq, QGROUP), lambda h, i: (i, h)), pl.BlockSpec((t, QGROUP), lambda h, i: (0, h)),
                  pl.BlockSpec((t, LANE), lambda h, i: (0, 0))],
        out_specs=pl.BlockSpec((tq, HEAD), lambda h, i: (i, h)),
        out_shape=jax.ShapeDtypeStruct((t, nh * HEAD), BF16), compiler_params=_params("parallel", "parallel"),
    )(qcat, kv, kr)


def _attn_bwd(qcat, kv, kr, do, *, name, scale):
    t = qcat.shape[0]
    nh = qcat.shape[1] // QGROUP
    tq = _pick(t, 256)
    nq = t // tq

    def body(q_ref, kv_ref, kr_ref, do_ref, dq_ref, dkv_ref, dkr_ref, dk_acc, dv_acc):
        h, i = pl.program_id(0), pl.program_id(1)

        @pl.when(i == 0)
        def _():
            dk_acc[...] = jnp.zeros_like(dk_acc)
            dv_acc[...] = jnp.zeros_like(dv_acc)

        @pl.when((i == 0) & (h == 0))
        def _():
            dkr_ref[...] = jnp.zeros_like(dkr_ref)

        for qi in range(nq):
            @pl.when(i == qi)
            def _(qi=qi):
                keys = slice(0, (qi + 1) * tq)
                q = q_ref[...]
                kvv = kv_ref[keys, :]
                dov = do_ref[...].astype(BF16)
                p, kcat = _scores(q, kvv, kr_ref[keys, :], qi, tq, scale)
                dp = lax.dot_general(dov, kvv[:, HEAD:], (((1,), (1,)), ((), ())), preferred_element_type=F32)
                ds = (p * (dp - jnp.sum(p * dp, axis=-1, keepdims=True)) * scale).astype(BF16)
                dq_ref[...] = jnp.dot(ds, kcat, preferred_element_type=F32)
                dk_acc[keys, :] += lax.dot_general(ds, q, (((0,), (0,)), ((), ())), preferred_element_type=F32)
                dv_acc[keys, :] += lax.dot_general(p.astype(BF16), dov, (((0,), (0,)), ((), ())), preferred_element_type=F32)

        @pl.when(i == nq - 1)
        def _():
            dk = dk_acc[...]
            dkv_ref[...] = jnp.concatenate([dk[:, :HEAD], dv_acc[...]], axis=1)
            dkr_ref[...] += dk[:, HEAD:]

    return pl.pallas_call(
        body, name=name, grid=(nh, nq),
        in_specs=[pl.BlockSpec((tq, QGROUP), lambda h, i: (i, h)), pl.BlockSpec((t, QGROUP), lambda h, i: (0, h)),
                  pl.BlockSpec((t, LANE), lambda h, i: (0, 0)), pl.BlockSpec((tq, HEAD), lambda h, i: (i, h))],
        out_specs=(pl.BlockSpec((tq, QGROUP), lambda h, i: (i, h)), pl.BlockSpec((t, QGROUP), lambda h, i: (0, h)),
                   pl.BlockSpec((t, LANE), lambda h, i: (0, 0))),
        out_shape=(jax.ShapeDtypeStruct((t, nh * QGROUP), F32), jax.ShapeDtypeStruct((t, nh * QGROUP), F32),
                   jax.ShapeDtypeStruct((t, LANE), F32)),
        scratch_shapes=[pltpu.VMEM((t, QGROUP), F32), pltpu.VMEM((t, HEAD), F32)],
        compiler_params=_params("arbitrary", "arbitrary"),
    )(qcat, kv, kr, do)


def _split3(x):
    hi = x.astype(BF16)
    r1 = x - hi.astype(F32)
    mid = r1.astype(BF16)
    lo = (r1 - mid.astype(F32)).astype(BF16)
    return hi, mid, lo


def _tri_matmul(mask, x):
    m = mask.astype(BF16)
    return sum(jnp.dot(m, part, preferred_element_type=F32) for part in _split3(x))


def _tri_masks(tb):
    row = lax.broadcasted_iota(jnp.int32, (tb, tb), 0)
    col = lax.broadcasted_iota(jnp.int32, (tb, tb), 1)
    same = col // SUB == row // SUB
    return jnp.stack([(col <= row) & same, (col >= row) & same, (col < row) & same]).astype(BF16)


def _sub_cumsum(g, masks):
    return _tri_matmul(masks[0], g)


def _sub_suffix_prefix(after, before, masks):
    return _tri_matmul(masks[1], after) + _tri_matmul(masks[2], before)


def _lower_bound(logits):
    mx = jnp.max(logits, axis=0, keepdims=True)
    e = jnp.exp(logits - mx)
    return e[0:1, :] / jnp.sum(e, axis=0, keepdims=True)


def _hgrn_fwd(proj, cols, wdt, logits, out_norm, *, name):
    t = proj.shape[0]
    nh = wdt // HEAD
    tb = _pick(t, 128)
    ns = tb // SUB

    def body(hq_ref, hf_ref, hi_ref, hg_ref, lg_ref, w_ref, m_ref, o_ref, yb_ref, st_ref, s_ref, q_s, k_s, b_s):
        @pl.when(pl.program_id(1) == 0)
        def _():
            s_ref[...] = jnp.zeros_like(s_ref)

        lb = _lower_bound(lg_ref[...])
        f = lb + (1.0 - lb) * _sigmoid(hf_ref[...])
        q_s[...] = _silu(hq_ref[...])
        k_s[...] = 1.0 - f
        b_s[...] = _sub_cumsum(jnp.log(f), m_ref)
        rowid = lax.broadcasted_iota(jnp.int32, (SUB, HEAD), 0)

        def sub(c, st):
            rows = pl.ds(pl.multiple_of(c * SUB, SUB), SUB)
            qc, kc, bc, vc = q_s[rows, :], k_s[rows, :], b_s[rows, :], hi_ref[rows, :]
            st_ref[0, c] = st
            bl = bc[SUB - 1:SUB, :]
            oc = lax.dot_general((qc * jnp.exp(bc)).astype(BF16), st.astype(BF16), (((1,), (1,)), ((), ())),
                                 preferred_element_type=F32)
            for s in range(SUB):
                e = jnp.where(rowid >= s, jnp.exp(bc - bc[s:s + 1, :]), 0.0)
                a = jnp.sum(qc * e * kc[s:s + 1, :], axis=1, keepdims=True)
                oc = oc + a * vc[s:s + 1, :]
            o_ref[rows, :] = oc
            kd = kc * jnp.exp(bl - bc)
            return jnp.exp(bl) * st + lax.dot_general(vc.astype(BF16), kd.astype(BF16), (((0,), (0,)), ((), ())),
                                                      preferred_element_type=F32)

        s_ref[...] = lax.fori_loop(0, ns, sub, s_ref[...], unroll=True)
        o = o_ref[...]
        r = lax.rsqrt(jnp.mean(o * o, axis=-1, keepdims=True) + NORM_EPS)
        yb_ref[...] = (o * r * w_ref[...] * _silu(hg_ref[...])).astype(BF16)

    blk = pl.BlockSpec((tb, HEAD), lambda h, j: (j, h))
    return pl.pallas_call(
        body, name=name, grid=(nh, t // tb),
        in_specs=[pl.BlockSpec((tb, HEAD), lambda h, j, o=c // HEAD: (j, h + o)) for c in cols]
        + [pl.BlockSpec((2, HEAD), lambda h, j: (0, h)), pl.BlockSpec((1, HEAD), lambda h, j: (0, 0)),
           pl.BlockSpec((3, tb, tb), lambda h, j: (0, 0, 0))],
        out_specs=(blk, blk, pl.BlockSpec((1, ns, HEAD, HEAD), lambda h, j: (h, j, 0, 0))),
        out_shape=(jax.ShapeDtypeStruct((t, wdt), F32), jax.ShapeDtypeStruct((t, wdt), BF16),
                   jax.ShapeDtypeStruct((nh, t // SUB, HEAD, HEAD), F32)),
        scratch_shapes=[pltpu.VMEM((HEAD, HEAD), F32)] + [pltpu.VMEM((tb, HEAD), F32)] * 3,
        compiler_params=_params("parallel", "arbitrary"),
    )(proj, proj, proj, proj, logits, out_norm, _tri_masks(tb))


def _hgrn_bwd(proj, cols, wdt, o_raw, dyb, states, logits, out_norm, *, name):
    t = proj.shape[0]
    nh = wdt // HEAD
    tb = _pick(t, 128)
    ns = tb // SUB
    nb = t // tb

    def body(hq_ref, hf_ref, hi_ref, hg_ref, o_ref, dy_ref, st_ref, lg_ref, w_ref, m_ref,
             dhq_ref, dhf_ref, dhi_ref, dhg_ref, dlb_ref, dw_ref,
             ds_ref, q_s, k_s, b_s, do_s, dq_s, dk_s, dv_s, after_s, before_s, thru_s):
        @pl.when(pl.program_id(1) == 0)
        def _():
            ds_ref[...] = jnp.zeros_like(ds_ref)
            dlb_ref[...] = jnp.zeros_like(dlb_ref)
            dw_ref[...] = jnp.zeros_like(dw_ref)

        lb = _lower_bound(lg_ref[...])
        hqv, hgv = hq_ref[...], hg_ref[...]
        sig = _sigmoid(hf_ref[...])
        f = lb + (1.0 - lb) * sig
        q_s[...] = _silu(hqv)
        k_s[...] = 1.0 - f
        b_s[...] = _sub_cumsum(jnp.log(f), m_ref)

        o = o_ref[...]
        r = lax.rsqrt(jnp.mean(o * o, axis=-1, keepdims=True) + NORM_EPS)
        nrm = o * r
        w = w_ref[...]
        dy = dy_ref[...].astype(F32)
        dhg_ref[...] = (dy * nrm * w * _dsilu(hgv)).astype(BF16)
        dnw = dy * _silu(hgv)
        dw_ref[0] += jnp.sum(dnw * nrm, axis=0, keepdims=True)
        tt = dnw * w
        do_s[...] = r * (tt - nrm * jnp.mean(tt * nrm, axis=-1, keepdims=True))
        rowid = lax.broadcasted_iota(jnp.int32, (SUB, HEAD), 0)

        def sub(cc, dst):
            c = ns - 1 - cc
            rows = pl.ds(pl.multiple_of(c * SUB, SUB), SUB)
            qc, kc, bc, vc, doc = q_s[rows, :], k_s[rows, :], b_s[rows, :], hi_ref[rows, :], do_s[rows, :]
            st = st_ref[0, c]
            bl = bc[SUB - 1:SUB, :]
            eb = jnp.exp(bc)
            ekd = jnp.exp(bl - bc)
            qe, kd = qc * eb, kc * ekd
            dob, vcb = doc.astype(BF16), vc.astype(BF16)
            dq_st = jnp.dot(dob, st.astype(BF16), preferred_element_type=F32) * eb
            dk_st = jnp.dot(vcb, dst.astype(BF16), preferred_element_type=F32) * ekd
            dv = lax.dot_general(kd.astype(BF16), dst.astype(BF16), (((1,), (1,)), ((), ())), preferred_element_type=F32)
            dq_in = jnp.zeros_like(qc)
            dk_in = jnp.zeros_like(qc)
            for s in range(SUB):
                e = jnp.where(rowid >= s, jnp.exp(bc - bc[s:s + 1, :]), 0.0)
                ek = e * kc[s:s + 1, :]
                a = jnp.sum(qc * ek, axis=1, keepdims=True)
                da = jnp.sum(doc * vc[s:s + 1, :], axis=1, keepdims=True)
                dq_in = dq_in + da * ek
                dk_in = dk_in + jnp.where(rowid == s, jnp.sum(da * e * qc, axis=0, keepdims=True), 0.0)
                dv = dv + jnp.where(rowid == s, jnp.sum(a * doc, axis=0, keepdims=True), 0.0)
            ebl = jnp.exp(bl)
            dq_s[rows, :] = dq_st + dq_in
            dk_s[rows, :] = dk_st + dk_in
            dv_s[rows, :] = dv
            after_s[rows, :] = qc * (dq_st + dq_in) - kc * dk_in
            before_s[rows, :] = kc * dk_st
            thru_s[rows, :] = jnp.broadcast_to(ebl * jnp.sum(st * dst, axis=0, keepdims=True), (SUB, HEAD))
            return ebl * dst + lax.dot_general(dob, qe.astype(BF16), (((0,), (0,)), ((), ())), preferred_element_type=F32)

        ds_ref[...] = lax.fori_loop(0, ns, sub, ds_ref[...], unroll=True)
        dg = _sub_suffix_prefix(after_s[...], before_s[...], m_ref) + thru_s[...]
        dhq_ref[...] = (dq_s[...] * _dsilu(hqv)).astype(BF16)
        dft = dg / f - dk_s[...]
        dhf_ref[...] = (dft * (1.0 - lb) * sig * (1.0 - sig)).astype(BF16)
        dlb_ref[0] += jnp.sum(dft * (1.0 - sig), axis=0, keepdims=True)
        dhi_ref[...] = dv_s[...].astype(BF16)

    blk = pl.BlockSpec((tb, HEAD), lambda h, j: (nb - 1 - j, h))
    vec = pl.BlockSpec((1, 1, HEAD), lambda h, j: (h, 0, 0))
    tok = jax.ShapeDtypeStruct((t, wdt), BF16)
    per_head = jax.ShapeDtypeStruct((nh, 1, HEAD), F32)
    return pl.pallas_call(
        body, name=name, grid=(nh, nb),
        in_specs=[pl.BlockSpec((tb, HEAD), lambda h, j, o=c // HEAD: (nb - 1 - j, h + o)) for c in cols]
        + [blk, blk] + [pl.BlockSpec((1, ns, HEAD, HEAD), lambda h, j: (h, nb - 1 - j, 0, 0)),
                              pl.BlockSpec((2, HEAD), lambda h, j: (0, h)), pl.BlockSpec((1, HEAD), lambda h, j: (0, 0)),
                              pl.BlockSpec((3, tb, tb), lambda h, j: (0, 0, 0))],
        out_specs=(blk, blk, blk, blk, vec, vec),
        out_shape=(tok, tok, tok, tok, per_head, per_head),
        scratch_shapes=[pltpu.VMEM((HEAD, HEAD), F32)] + [pltpu.VMEM((tb, HEAD), F32)] * 10,
        compiler_params=_params("arbitrary", "arbitrary"),
    )(proj, proj, proj, proj, o_raw, dyb, states, logits, out_norm, _tri_masks(tb))


def _lb_logits_grad(logits, dlb, *, name):
    def body(lg_ref, d_ref, o_ref):
        lg = lg_ref[...]
        e = jnp.exp(lg - jnp.max(lg, axis=0, keepdims=True))
        p = e / jnp.sum(e, axis=0, keepdims=True)
        d = d_ref[...]
        rowid = lax.broadcasted_iota(jnp.int32, lg.shape, 0)
        dp = jnp.where(rowid == 0, d, 0.0)
        o_ref[...] = p * (dp - jnp.sum(p * dp, axis=0, keepdims=True))

    return pl.pallas_call(body, name=name, out_shape=jax.ShapeDtypeStruct(logits.shape, F32))(logits, dlb)


def _adamw(w, g, m, v, *, name, deps=()):
    r, c = w.shape
    tc = _pick(c, 2048) if c % LANE == 0 else c
    tr = _row_tile(r, tc * 4)

    def body(w_ref, g_ref, m_ref, v_ref, *rest):
        d_ref, nm_ref, nv_ref = rest[-3:]
        gv = g_ref[...]
        nm = ADAM_B1 * m_ref[...] + (1.0 - ADAM_B1) * gv
        nv = ADAM_B2 * v_ref[...] + (1.0 - ADAM_B2) * (gv * gv)
        m_hat = nm / (1.0 - ADAM_B1 ** ADAM_STEP)
        v_hat = nv / (1.0 - ADAM_B2 ** ADAM_STEP)
        d_ref[...] = -ADAM_LR * (m_hat / (jnp.sqrt(v_hat) + ADAM_EPS) + ADAM_WD * w_ref[...])
        nm_ref[...] = nm
        nv_ref[...] = nv

    spec = pl.BlockSpec((tr, tc), lambda i, j: (i, j))
    shp = jax.ShapeDtypeStruct((r, c), F32)
    return pl.pallas_call(
        body, name=name, grid=(r // tr, c // tc), in_specs=[spec] * 4 + [ANY] * len(deps), out_specs=[spec] * 3,
        out_shape=[shp, shp, shp], compiler_params=_params("parallel", "parallel"),
    )(w, g, m, v, *deps)


def _coords():
    return lax.axis_index("x"), lax.axis_index("y"), lax.axis_index("c")


def _other_chips(x, y):
    return [(1 - x, y), (x, 1 - y), (1 - x, 1 - y)]


ANY = pl.BlockSpec(memory_space=pl.ANY)


class _Layout:
    def __init__(self, d, dff, in_cols, q_lora, kv_lora, nh):
        assert q_lora == kv_lora and nh % 4 == 0 and dff % (4 * LANE) == 0 and in_cols % 4 == 0 and d % 4 == 0
        self.d, self.dff, self.q_lora, self.nh = d, dff, q_lora, nh
        self.head = q_lora + kv_lora + ROPE
        self.pad = d - self.head
        self.nff, self.ncol, self.r_o, self.hps = dff // 4, in_cols // 4, d // 4, nh // 4
        assert self.head <= self.ncol
        self.off_q, self.off_kv, self.rows_narrow = 0, nh * QGROUP, 2 * nh * QGROUP


HBM = pl.BlockSpec(memory_space=pltpu.HBM)
SEMS = pl.BlockSpec(memory_space=pltpu.SEMAPHORE)
SPLIT = dict(has_side_effects=pltpu.SideEffectType.DATAFLOW_SIDE_EFFECTING)


def _in_hbm(a):
    return pltpu.with_memory_space_constraint(a, pltpu.HBM)


def _shard_rows(jobs, k):
    out, lrow = [], [0] * (1 + max(job.a for job in jobs))
    for job in jobs:
        for row, rows in job.pieces(k):
            out.append((job.a, lrow[job.a], row, rows))
            lrow[job.a] += rows
    return out


def _shard_total(jobs, a):
    return sum(rows for b, _, _, rows in _shard_rows(jobs, 0) if b == a)


def _gather_start(packs, lands, jobs, *, name, deps=()):
    n = len(packs)

    def body(*refs):
        p_refs, l_refs, send, recv, token = refs[:n], refs[n:2 * n], refs[-2 * n - 3], refs[-2 * n - 2], refs[-1]
        x, y, c = _coords()
        for a, lrow, row, rows in _shard_rows(jobs, 2 * x + y):
            pltpu.make_async_remote_copy(
                src_ref=p_refs[a].at[:, pl.ds(lrow, rows)], dst_ref=l_refs[a].at[:, pl.ds(row, rows)],
                send_sem=send.at[4 * a + 3], recv_sem=recv.at[4 * a + 3], device_id=(x, y, 1 - c), device_id_type=MESH).start()
            for j, (px, py) in enumerate(_other_chips(x, y)):
                pltpu.make_async_remote_copy(
                    src_ref=p_refs[a].at[c, pl.ds(lrow, rows)], dst_ref=l_refs[a].at[c, pl.ds(row, rows)],
                    send_sem=send.at[4 * a + j], recv_sem=recv.at[4 * a + j], device_id=(px, py, c), device_id_type=MESH).start()
        token[...] = jnp.zeros_like(token)

    thru = [pltpu.HBM(a.shape, a.dtype) for a in packs + lands]
    out = pl.pallas_call(
        body, name=name, in_specs=[HBM] * (2 * n) + [ANY] * len(deps),
        out_shape=(pltpu.SemaphoreType.DMA((4 * n,)), pltpu.SemaphoreType.DMA((4 * n,)), *thru, jax.ShapeDtypeStruct((8, LANE), F32)),
        out_specs=(SEMS, SEMS, *[HBM] * (2 * n), pl.BlockSpec(memory_space=pltpu.VMEM)),
        input_output_aliases={i: 2 + i for i in range(2 * n)}, compiler_params=pltpu.CompilerParams(**SPLIT),
    )(*[_in_hbm(a) for a in packs + lands], *deps)
    return dict(send=out[0], recv=out[1], bufs=list(out[2:2 + 2 * n]), n=n, jobs=jobs), out[-1]


def _gather_wait(handle, after, *, name):
    n, jobs = handle["n"], handle["jobs"]

    def body(*refs):
        l_refs, send, recv, token = refs[n:2 * n], refs[2 * n], refs[2 * n + 1], refs[-1]
        token[...] = jnp.zeros_like(token)
        x, y, c = _coords()
        for a in range(n):
            total = _shard_total(jobs, a)
            for j, like in enumerate([l_refs[a].at[0, pl.ds(0, total)]] * 3 + [l_refs[a].at[:, pl.ds(0, total)]]):
                cp = pltpu.make_async_remote_copy(src_ref=like, dst_ref=like, send_sem=send.at[4 * a + j],
                                                  recv_sem=recv.at[4 * a + j], device_id=(x, y, c), device_id_type=MESH)
                cp.wait_send()
                cp.wait_recv()

    out = pl.pallas_call(
        body, name=name, in_specs=[HBM] * (2 * n) + [SEMS, SEMS] + [ANY] * len(after),
        out_shape=[pltpu.HBM(a.shape, a.dtype) for a in handle["bufs"]] + [jax.ShapeDtypeStruct((8, LANE), F32)],
        out_specs=[HBM] * (2 * n) + [pl.BlockSpec(memory_space=pltpu.VMEM)],
        input_output_aliases={i: i for i in range(2 * n)}, compiler_params=pltpu.CompilerParams(**SPLIT),
    )(*handle["bufs"], handle["send"], handle["recv"], *after)
    return list(out[n:2 * n]), out[-1]


def _gather_forward(lands, jobs, *, name, deps=()):
    n = len(lands)

    def body(*refs):
        l_refs, send, recv = refs[n + len(deps):2 * n + len(deps)], refs[-2], refs[-1]
        x, y, c = _coords()
        for j, (px, py) in enumerate(_other_chips(x, y)):
            for a, _, row, rows in _shard_rows(jobs, 2 * px + py):
                blk = l_refs[a].at[c, pl.ds(row, rows)]
                pltpu.make_async_remote_copy(src_ref=blk, dst_ref=blk, send_sem=send.at[3 * a + j], recv_sem=recv.at[3 * a + j],
                                             device_id=(x, y, 1 - c), device_id_type=MESH).start()
        for a in range(n):
            like = l_refs[a].at[0, pl.ds(0, _shard_total(jobs, a))]
            for j in range(3):
                cp = pltpu.make_async_remote_copy(src_ref=like, dst_ref=like, send_sem=send.at[3 * a + j],
                                                  recv_sem=recv.at[3 * a + j], device_id=(x, y, c), device_id_type=MESH)
                cp.wait_send()
                cp.wait_recv()

    sem = pltpu.SemaphoreType.DMA((3 * n,))
    return pl.pallas_call(
        body, name=name, in_specs=[ANY] * (n + len(deps)), out_specs=[ANY] * n, input_output_aliases={i: i for i in range(n)},
        out_shape=[jax.ShapeDtypeStruct(a.shape, a.dtype) for a in lands], scratch_shapes=[sem, sem],
    )(*lands, *deps)


def _forward_start(lands, jobs, *, name, deps=()):
    n, nd = len(lands), len(deps)

    def body(*refs):
        l_refs, sems, token = refs[:n], refs[n + nd:n + nd + 2 * n], refs[-1]
        x, y, c = _coords()
        for j, (px, py) in enumerate(_other_chips(x, y)):
            for a, _, row, rows in _shard_rows(jobs, 2 * px + py):
                blk = l_refs[a].at[c, pl.ds(row, rows)]
                pltpu.make_async_remote_copy(src_ref=blk, dst_ref=blk, send_sem=sems[2 * a].at[j], recv_sem=sems[2 * a + 1].at[j],
                                             device_id=(x, y, 1 - c), device_id_type=MESH).start()
        token[...] = jnp.zeros_like(token)

    out = pl.pallas_call(
        body, name=name, in_specs=[HBM] * n + [ANY] * nd,
        out_shape=(*[pltpu.SemaphoreType.DMA((3,))] * (2 * n), *[pltpu.HBM(a.shape, a.dtype) for a in lands],
                   jax.ShapeDtypeStruct((8, LANE), F32)),
        out_specs=(*[SEMS] * (2 * n), *[HBM] * n, pl.BlockSpec(memory_space=pltpu.VMEM)),
        input_output_aliases={i: 2 * n + i for i in range(n)}, compiler_params=pltpu.CompilerParams(**SPLIT),
    )(*[_in_hbm(a) for a in lands], *deps)
    return [dict(send=out[2 * a], recv=out[2 * a + 1], buf=out[2 * n + a]) for a in range(n)], out[-1]


def _forward_wait(handle, jobs, a, after, *, name):
    total = _shard_total(jobs, a)

    def body(l_ref, send, recv, *rest):
        x, y, c = _coords()
        like = l_ref.at[0, pl.ds(0, total)]
        for j in range(3):
            cp = pltpu.make_async_remote_copy(src_ref=like, dst_ref=like, send_sem=send.at[j], recv_sem=recv.at[j],
                                              device_id=(x, y, c), device_id_type=MESH)
            cp.wait_send()
            cp.wait_recv()

    buf = handle["buf"]
    return pl.pallas_call(
        body, name=name, in_specs=[HBM, SEMS, SEMS] + [ANY] * len(after), out_shape=pltpu.HBM(buf.shape, buf.dtype),
        out_specs=HBM, input_output_aliases={0: 0}, compiler_params=pltpu.CompilerParams(**SPLIT),
    )(buf, handle["send"], handle["recv"], *after)


def _add_sibling(g, recv, sel, *, name):
    rows, hw = recv.shape
    tr = _row_tile(rows, hw * 4)

    def body(sel_ref, g_ref, r_ref, o_ref):
        o_ref[...] = (g_ref[...] + r_ref[...]).astype(BF16)

    return pl.pallas_call(
        body, name=name, out_shape=jax.ShapeDtypeStruct((rows, hw), BF16),
        grid_spec=pltpu.PrefetchScalarGridSpec(
            num_scalar_prefetch=1, grid=(rows // tr,),
            in_specs=[pl.BlockSpec((None, tr, hw), lambda i, s: (s[0], i, 0)), pl.BlockSpec((tr, hw), lambda i, s: (i, 0))],
            out_specs=pl.BlockSpec((tr, hw), lambda i, s: (i, 0))),
        compiler_params=_params("parallel"),
    )(sel, g, recv)


class _Job:
    def __init__(self, a, blk, n_outer, n_inner, stride, start):
        self.a, self.blk, self.n_outer, self.n_inner, self.stride, self.start = a, blk, n_outer, n_inner, stride, start
        self.rows_out = n_outer * n_inner * blk

    def pieces(self, k):
        return [(self.start(k) + o * self.stride * self.blk, self.n_inner * self.blk) for o in range(self.n_outer)]


def _block_rows(rows, cap, *also):
    best = None
    for b in range(16, min(rows, cap) + 1, 16):
        if rows % b == 0 and all(v % b == 0 for v in also):
            best = b
    assert best is not None, (rows, also)
    return best


def _ffn_jobs(lay):
    b = _block_rows(lay.nff, 704, lay.dff)
    return [_Job(0, b, 3, lay.nff // b, lay.dff // b, lambda k: lay.nff * k)]


def _ffn_weight_jobs(lay):
    b = _block_rows(lay.nff, 704)
    return [_Job(a, b, 1, lay.nff // b, 0, lambda k: lay.nff * k) for a in range(3)]


def _mix_jobs(lay):
    d, ncol, head, pad = lay.d, lay.ncol, lay.head, lay.pad
    first = lambda k, a, b: jnp.where(k == 0, a, b) if not isinstance(k, int) else (a if k == 0 else b)
    ba = _block_rows(head, 704, *[ncol * k + pad for k in (1, 2, 3)])
    bb = _block_rows(ncol - head, 704, *[ncol * k + d for k in (0, 1, 2, 3)])
    bo = _block_rows(lay.r_o, 704, d)
    bq = _block_rows(HEAD + ROPE, 704, QGROUP)
    bk = _block_rows(lay.hps * QGROUP, 704, lay.off_kv)
    return [_Job(0, ba, 1, head // ba, 0, lambda k: first(k, 0, ncol * k + pad)),
            _Job(0, bb, 1, (ncol - head) // bb, 0, lambda k: ncol * k + d),
            _Job(0, bo, 3, lay.r_o // bo, d // bo, lambda k: 7 * d + lay.r_o * k),
            _Job(1, bq, lay.hps, (HEAD + ROPE) // bq, QGROUP // bq, lambda k: QGROUP * lay.hps * k),
            _Job(1, bk, 1, lay.hps * QGROUP // bk, 0, lambda k: lay.off_kv + lay.hps * QGROUP * k)]


def _swap_start(gs, *, name):
    n = len(gs)
    lands = [lax.empty(g.shape[1:], g.dtype) for g in gs]

    def body(*refs):
        g_refs, land_refs, send, recv, token = refs[:n], refs[n:2 * n], refs[2 * n], refs[2 * n + 1], refs[-1]
        x, y, c = _coords()
        for a in range(n):
            pltpu.make_async_remote_copy(src_ref=g_refs[a].at[1 - c], dst_ref=land_refs[a], send_sem=send.at[a],
                                         recv_sem=recv.at[a], device_id=(x, y, 1 - c), device_id_type=MESH).start()
        token[...] = jnp.zeros_like(token)

    thru = [pltpu.HBM(a.shape, a.dtype) for a in gs + lands]
    out = pl.pallas_call(
        body, name=name, in_specs=[HBM] * (2 * n),
        out_shape=(pltpu.SemaphoreType.DMA((n,)), pltpu.SemaphoreType.DMA((n,)), *thru, jax.ShapeDtypeStruct((8, LANE), F32)),
        out_specs=(SEMS, SEMS, *[HBM] * (2 * n), pl.BlockSpec(memory_space=pltpu.VMEM)),
        input_output_aliases={i: 2 + i for i in range(2 * n)}, compiler_params=pltpu.CompilerParams(**SPLIT),
    )(*[_in_hbm(a) for a in gs + lands])
    return dict(send=out[0], recv=out[1], bufs=list(out[2:2 + 2 * n]), n=n), out[-1]


def _swap_wait(handle, after, *, name):
    n = handle["n"]

    def body(*refs):
        g_refs, land_refs, send, recv = refs[:n], refs[n:2 * n], refs[2 * n], refs[2 * n + 1]
        x, y, c = _coords()
        for a in range(n):
            cp = pltpu.make_async_remote_copy(src_ref=g_refs[a].at[1 - c], dst_ref=land_refs[a], send_sem=send.at[a],
                                              recv_sem=recv.at[a], device_id=(x, y, 1 - c), device_id_type=MESH)
            cp.wait_send()
            cp.wait_recv()

    out = pl.pallas_call(
        body, name=name, in_specs=[HBM] * (2 * n) + [SEMS, SEMS] + [ANY] * len(after),
        out_shape=[pltpu.HBM(a.shape, a.dtype) for a in handle["bufs"]], out_specs=[HBM] * (2 * n),
        input_output_aliases={i: i for i in range(2 * n)}, compiler_params=pltpu.CompilerParams(**SPLIT),
    )(*handle["bufs"], handle["send"], handle["recv"], *after)
    return list(out[:n]), list(out[n:])


def _exchange_start(ss, jobs, *, name):
    n = len(ss)
    lands = [lax.empty((3,) + s.shape, s.dtype) for s in ss]

    def body(*refs):
        s_refs, land_refs, send, recv, token = refs[:n], refs[n:2 * n], refs[2 * n], refs[2 * n + 1], refs[-1]
        x, y, c = _coords()
        for j, (px, py) in enumerate(_other_chips(x, y)):
            for job in jobs:
                for row, rows in job.pieces(2 * px + py):
                    pltpu.make_async_remote_copy(
                        src_ref=s_refs[job.a].at[pl.ds(row, rows)], dst_ref=land_refs[job.a].at[j, pl.ds(row, rows)],
                        send_sem=send.at[n * j + job.a], recv_sem=recv.at[n * j + job.a], device_id=(px, py, c),
                        device_id_type=MESH).start()
        token[...] = jnp.zeros_like(token)

    thru = [pltpu.HBM(a.shape, a.dtype) for a in ss + lands]
    out = pl.pallas_call(
        body, name=name, in_specs=[HBM] * (2 * n),
        out_shape=(pltpu.SemaphoreType.DMA((3 * n,)), pltpu.SemaphoreType.DMA((3 * n,)), *thru, jax.ShapeDtypeStruct((8, LANE), F32)),
        out_specs=(SEMS, SEMS, *[HBM] * (2 * n), pl.BlockSpec(memory_space=pltpu.VMEM)),
        input_output_aliases={i: 2 + i for i in range(2 * n)}, compiler_params=pltpu.CompilerParams(**SPLIT),
    )(*[_in_hbm(a) for a in ss + lands])
    return dict(send=out[0], recv=out[1], bufs=list(out[2:2 + 2 * n]), n=n, jobs=jobs), out[-1]


def _exchange_wait(handle, after, *, name):
    n, jobs = handle["n"], handle["jobs"]
    total = [sum(rows for job in jobs if job.a == a for _, rows in job.pieces(0)) for a in range(n)]

    def body(*refs):
        s_refs, land_refs, send, recv = refs[:n], refs[n:2 * n], refs[2 * n], refs[2 * n + 1]
        x, y, c = _coords()
        for a in range(n):
            for j in range(3):
                all_rows = land_refs[a].at[0, pl.ds(0, total[a])]
                cp = pltpu.make_async_remote_copy(src_ref=all_rows, dst_ref=all_rows, send_sem=send.at[n * j + a],
                                                  recv_sem=recv.at[n * j + a], device_id=(x, y, c), device_id_type=MESH)
                cp.wait_send()
                cp.wait_recv()

    out = pl.pallas_call(
        body, name=name, in_specs=[HBM] * (2 * n) + [SEMS, SEMS] + [ANY] * len(after),
        out_shape=[pltpu.HBM(a.shape, a.dtype) for a in handle["bufs"]], out_specs=[HBM] * (2 * n),
        input_output_aliases={i: i for i in range(2 * n)}, compiler_params=pltpu.CompilerParams(**SPLIT),
    )(*handle["bufs"], handle["send"], handle["recv"], *after)
    return list(out[:n]), list(out[n:])


def _add_shard(s, land, job, sel, k, *, name):
    hw = s.shape[1]
    blk, no, ni, stride = job.blk, job.n_outer, job.n_inner, job.stride
    scal = jnp.stack([sel, job.start(k) // blk]).astype(jnp.int32)

    def body(sc_ref, own_ref, r_ref, o_ref):
        o_ref[...] = ((own_ref[...].astype(F32) + r_ref[0].astype(F32)) + r_ref[1].astype(F32)) + r_ref[2].astype(F32)

    return pl.pallas_call(
        body, name=name, out_shape=jax.ShapeDtypeStruct((2, job.rows_out, hw), F32),
        grid_spec=pltpu.PrefetchScalarGridSpec(
            num_scalar_prefetch=1, grid=(no, ni),
            in_specs=[pl.BlockSpec((blk, hw), lambda o, b, sc: (sc[1] + o * stride + b, 0)),
                      pl.BlockSpec((3, blk, hw), lambda o, b, sc: (0, sc[1] + o * stride + b, 0))],
            out_specs=pl.BlockSpec((None, blk, hw), lambda o, b, sc: (sc[0], o * ni + b, 0))),
        compiler_params=_params("parallel", "parallel"),
    )(scal, s, land)


def _join_list(fs, *, name):
    n = len(fs)

    def body(*refs):
        f_refs, send_sems, recv_sems = refs[n:2 * n], refs[2 * n], refs[2 * n + 1]
        x, y, c = _coords()
        copies = [pltpu.make_async_remote_copy(
            src_ref=f.at[c], dst_ref=f.at[c], send_sem=send_sems.at[a], recv_sem=recv_sems.at[a],
            device_id=(x, y, 1 - c), device_id_type=MESH) for a, f in enumerate(f_refs)]
        for cp in copies:
            cp.start()
        for cp in copies:
            cp.wait()

    sem = pltpu.SemaphoreType.DMA((n,))
    return pl.pallas_call(
        body, name=name, in_specs=[ANY] * n, out_specs=[ANY] * n, input_output_aliases={i: i for i in range(n)},
        out_shape=[jax.ShapeDtypeStruct(f.shape, f.dtype) for f in fs], scratch_shapes=[sem, sem],
    )(*fs)


def _all_reduce_small(vec, *, name):
    n = vec.shape[1]

    def body(v_ref, o_ref, buf, send_sems, recv_sems):
        x, y, c = _coords()
        me = 4 * x + 2 * y + c
        buf[me] = v_ref[...]
        copies = []
        for m in range(1, 8):
            peer = (x ^ ((m >> 2) & 1), y ^ ((m >> 1) & 1), c ^ (m & 1))
            copies.append(pltpu.make_async_remote_copy(
                src_ref=v_ref, dst_ref=buf.at[me], send_sem=send_sems.at[m - 1], recv_sem=recv_sems.at[m - 1],
                device_id=peer, device_id_type=MESH))
        for cp in copies:
            cp.start()
        for cp in copies:
            cp.wait()
        acc = buf[0]
        for d in range(1, 8):
            acc = acc + buf[d]
        o_ref[...] = acc

    return pl.pallas_call(
        body, name=name, out_shape=jax.ShapeDtypeStruct((1, n), F32),
        in_specs=[pl.BlockSpec(memory_space=pltpu.VMEM)], out_specs=pl.BlockSpec(memory_space=pltpu.VMEM),
        scratch_shapes=[pltpu.VMEM((8, 1, n), F32), pltpu.SemaphoreType.DMA((7,)), pltpu.SemaphoreType.DMA((7,))],
    )(vec)


def _ffn_fwd(x, n_pre, n_post, weight, lay, tag):
    h = _norm_fwd(x, n_pre, name=f"{tag}_norm_pre", out_dtype=BF16)
    wg = (weight(0, [h]), 0, lay.dff)
    g = _mm([(h, wg)], name=f"{tag}_gate", mode="nt")
    wu = (weight(1, [g]), 0, lay.dff)
    u, a = _mm([(h, wu)], name=f"{tag}_up", mode="nt", extras=[g], out_dtypes=[F32, BF16], wide_vmem=True,
               epilogue=lambda up, gate: (up, _silu(gate) * up))
    wd = (weight(2, [u]), 0, lay.dff)
    yv = _mm([(a, wd)], name=f"{tag}_down", mode="nn")
    out = _norm_fwd(yv, n_post, name=f"{tag}_norm_post", resid=x, scale=MACARON_SCALE)
    return out, (x, h, g, u, a, yv), (wg, wu, wd)


def _ffn_bwd(dout, saved, n_pre, n_post, weights, lay, tag, deps=(), after_act=None, after_dw=None):
    x, h, g, u, a, yv = saved
    dff = lay.dff
    gbuf = lax.empty((2, 3 * dff, lay.d // 2), F32)
    dy, dn_post = _norm_bwd(yv, n_post, dout, name=f"{tag}_norm_post_bwd", scale=MACARON_SCALE)
    wg, wu, wd = weights
    dg, du = _mm([(dy, wd)], name=f"{tag}_down_dx", mode="nt", deps=deps, extras=[g, u],
                 out_dtypes=[BF16, BF16], tm_cap=MM_TILE // 2,
                 epilogue=lambda da, gate, up: (da * up * _dsilu(gate), da * _silu(gate)))
    deps = after_act(du) if after_act is not None else ()
    gbuf = _mm([(a, dy)], name=f"{tag}_down_dw", mode="tn", into=(gbuf, 2 * dff), deps=deps)
    gbuf = _mm([(dg, h)], name=f"{tag}_gate_dw", mode="tn", into=(gbuf, 0))
    gbuf = _mm([(du, h)], name=f"{tag}_up_dw", mode="tn", into=(gbuf, dff))
    deps = after_dw(gbuf)
    dh = _mm([(dg, wg), (du, wu)], name=f"{tag}_up_dx", mode="nn", deps=deps)
    dx, dn_pre = _norm_bwd(x, n_pre, dh, name=f"{tag}_norm_pre_bwd", dres=dout)
    return dx, dn_pre, dn_post


def _rope_tables(positions):
    half = ROPE // 2
    inv_freq = ROPE_THETA ** (-jnp.arange(half, dtype=F32) / half)
    ang = positions.astype(F32)[:, None] * inv_freq
    cos, sin = jnp.cos(ang), jnp.sin(ang)
    z = jnp.zeros_like(cos)
    z2 = jnp.zeros((positions.shape[0], LANE - ROPE), F32)
    return (jnp.concatenate([cos, cos, z2], axis=1), jnp.concatenate([-sin, z, z2], axis=1),
            jnp.concatenate([z, sin, z2], axis=1))


def kernel(x, positions, ffn1_norm_pre, ffn1_w_gate, ffn1_w_up, ffn1_w_down, ffn1_norm_post, mix_norm_pre, w_in, mla_q_norm, mla_w_q_up, mla_kv_norm, mla_w_kv_up, mla_w_o, hgrn_lb_logits, hgrn_out_norm, hgrn_w_o, w_out, mix_norm_post, ffn2_norm_pre, ffn2_w_gate, ffn2_w_up, ffn2_w_down, ffn2_norm_post, loss_target, m_ffn1_norm_pre, m_ffn1_w_gate, m_ffn1_w_up, m_ffn1_w_down, m_ffn1_norm_post, m_mix_norm_pre, m_w_in, m_mla_q_norm, m_mla_w_q_up, m_mla_kv_norm, m_mla_w_kv_up, m_mla_w_o, m_hgrn_lb_logits, m_hgrn_out_norm, m_hgrn_w_o, m_w_out, m_mix_norm_post, m_ffn2_norm_pre, m_ffn2_w_gate, m_ffn2_w_up, m_ffn2_w_down, m_ffn2_norm_post, v_ffn1_norm_pre, v_ffn1_w_gate, v_ffn1_w_up, v_ffn1_w_down, v_ffn1_norm_post, v_mix_norm_pre, v_w_in, v_mla_q_norm, v_mla_w_q_up, v_mla_kv_norm, v_mla_w_kv_up, v_mla_w_o, v_hgrn_lb_logits, v_hgrn_out_norm, v_hgrn_w_o, v_w_out, v_mix_norm_post, v_ffn2_norm_pre, v_ffn2_w_gate, v_ffn2_w_up, v_ffn2_w_down, v_ffn2_norm_post):
    given = dict(locals())
    wts = {n: given[n] for n in ALL_WEIGHTS}
    mom = {n: given["m_" + n] for n in ALL_WEIGHTS}
    var = {n: given["v_" + n] for n in ALL_WEIGHTS}
    xin = x[0]
    target = loss_target[0]
    t, d = xin.shape
    cx, cy, cc = _coords()

    q_lora, kv_lora = mla_q_norm.shape[1], mla_kv_norm.shape[1]
    nh_mla = 4 * mla_w_kv_up.shape[2] // QGROUP
    lay = _Layout(d, 4 * ffn1_w_gate.shape[2], 4 * w_in.shape[2], q_lora, kv_lora, nh_mla)
    jobs_mix = _mix_jobs(lay)
    def pack(src, col_sharded, row_sharded=()):
        a = jnp.concatenate([src[n][0].T.astype(BF16) for n in col_sharded] + [src[n][0].astype(BF16) for n in row_sharded])
        return a.reshape(a.shape[0], 2, a.shape[1] // 2).transpose(1, 0, 2)

    jobs_w = _ffn_weight_jobs(lay)
    ffn_packs = lambda src, tag: [pack(src, [f"{tag}_w_gate"]), pack(src, [f"{tag}_w_up"]), pack(src, [], [f"{tag}_w_down"])]
    ffn_lands = lambda: [lax.empty((2, lay.dff, d // 2), BF16) for _ in range(3)]

    def handed_over(handles, tag):
        return lambda i, after: _forward_wait(handles[i], jobs_w, i, after, name=f"gather_{tag}_forward_wait_{i}")

    got1, tok = _gather_start(ffn_packs(wts, "ffn1"), ffn_lands(), jobs_w, name="gather_ffn1")
    later, _ = lax.optimization_barrier(({n: wts[n] for n in BIG_WEIGHTS if not n.startswith("ffn1")}, tok))
    packs_mix = [pack(later, ["w_in"], ["mla_w_o", "hgrn_w_o", "w_out"]), pack(later, ["mla_w_q_up", "mla_w_kv_up"])]
    packs_ffn2 = ffn_packs(later, "ffn2")
    lands_mix = [jnp.zeros((2, 10 * d, d // 2), BF16), jnp.zeros((2, lay.rows_narrow, q_lora // 2), BF16)]
    arrived, tok = _gather_wait(got1, packs_mix + packs_ffn2 + lands_mix, name="gather_ffn1_wait")
    got_m, tok = _gather_start(packs_mix, lands_mix, jobs_mix, name="gather_mix", deps=[tok])
    handing1, _ = _forward_start(arrived[:1], jobs_w[:1], name="gather_ffn1_forward_gate", deps=[tok])

    def ffn1_weight(i, after):
        if i == 0:
            gate = _forward_wait(handing1[0], jobs_w, 0, after, name="gather_ffn1_forward_wait_0")
            handing1.extend(_forward_start(arrived[1:], jobs_w[:2], name="gather_ffn1_forward_rest", deps=[gate])[0])
            return gate
        return _forward_wait(handing1[i], jobs_w, 0, after, name=f"gather_ffn1_forward_wait_{i}")
    col_kr = q_lora + kv_lora
    hgrn_cols = [d, 2 * d, 3 * d, 4 * d]
    col_ga, col_gb = 5 * d, 6 * d
    tabs = _rope_tables(positions[0])
    scale = (HEAD + ROPE) ** -0.5

    x1, saved1, w_ffn1 = _ffn_fwd(xin, ffn1_norm_pre, ffn1_norm_post, ffn1_weight, lay, "ffn1")

    arrived, tok = _gather_wait(got_m, [x1], name="gather_mix_wait")
    got2, tok = _gather_start(packs_ffn2, ffn_lands(), jobs_w, name="gather_ffn2", deps=[tok])
    wide, narrow = _gather_forward(arrived, jobs_mix, name="gather_mix_forward", deps=[tok])
    w_in_v = (wide, 0, 7 * d)
    w_o_v = {n: (wide, (7 + i) * d, d) for i, n in enumerate(("mla_w_o", "hgrn_w_o", "w_out"))}
    w_q_v = (narrow, lay.off_q, nh_mla * QGROUP)
    w_kv_v = (narrow, lay.off_kv, nh_mla * QGROUP)

    h2 = _norm_fwd(x1, mix_norm_pre, name="mix_norm_pre", out_dtype=BF16)
    proj = _mm([(h2, w_in_v)], name="mix_in", mode="nt", deps=[tok])
    cqn = _norm_fwd(proj, mla_q_norm, name="mla_q_norm", out_dtype=BF16, col=0)
    ckvn = _norm_fwd(proj, mla_kv_norm, name="mla_kv_norm", out_dtype=BF16, col=q_lora)
    qp = _mm([(cqn, w_q_v)], name="mla_q_up", mode="nt")
    kvb = _mm([(ckvn, w_kv_v)], name="mla_kv_up", mode="nt", out_dtype=BF16)
    qcat = _rope(qp, tabs, name="rope_q", group=QGROUP, backward=False, out_dtype=BF16)
    krot = _rope(proj, tabs, name="rope_k", group=LANE, backward=False, out_dtype=BF16, col=col_kr, ngroup=1)
    o_mla = _attn_fwd(qcat, kvb, krot, name="mla_attention", scale=scale)
    y_a = _mm([(o_mla, w_o_v["mla_w_o"])], name="mla_out", mode="nn")

    o_raw, yb, states = _hgrn_fwd(proj, hgrn_cols, d, hgrn_lb_logits, hgrn_out_norm, name="hgrn_scan")
    handing2, tok = _forward_start(_gather_wait(got2, [o_raw], name="gather_ffn2_wait")[0], jobs_w, name="gather_ffn2_forward")
    y_b = _mm([(yb, w_o_v["hgrn_w_o"])], name="hgrn_out", mode="nn", deps=[tok])

    merged = _merge_fwd(proj, col_ga, col_gb, y_a, y_b, name="mix_merge")
    y_mix = _mm([(merged, w_o_v["w_out"])], name="mix_out", mode="nn")
    x2 = _norm_fwd(y_mix, mix_norm_post, name="mix_norm_post", resid=x1, scale=1.0)

    x3, saved2, w_ffn2 = _ffn_fwd(x2, ffn2_norm_pre, ffn2_norm_post, handed_over(handing2, "ffn2"), lay, "ffn2")
    dx3, loss_local = _loss_head(x3, target, name="loss_head")

    grads, deltas, new_m, new_v = {}, {}, {}, {}
    sel = cc.astype(jnp.int32)
    sel1 = jnp.reshape(sel, (1,))
    me_chip = (2 * cx + cy).astype(jnp.int32)

    def reduce_mid(handle, after, jobs, tag):
        bufs, recvd = _swap_wait(handle, after, name=f"grad_swap_{tag}_wait")
        sums = [_add_sibling(b, r, sel1, name=f"grad_add_sibling_{tag}_{i}") for i, (b, r) in enumerate(zip(bufs, recvd))]
        return _exchange_start(sums, jobs, name=f"grad_exchange_{tag}")

    def reduce_end(handle, after, tag):
        sums, lands = _exchange_wait(handle, after, name=f"grad_exchange_{tag}_wait")
        parts = [_add_shard(sums[job.a], lands[job.a], job, sel, me_chip, name=f"grad_add_chips_{tag}_{i}")
                 for i, job in enumerate(handle["jobs"])]
        return _join_list(parts, name=f"grad_join_{tag}")

    def natural(part, lo, rows, transposed):
        g_n = part[:, lo:lo + rows]
        hw_n = g_n.shape[2]
        return g_n.transpose(0, 2, 1).reshape(2 * hw_n, rows) if transposed else g_n.transpose(1, 0, 2).reshape(rows, 2 * hw_n)

    def adam(names, deps=()):
        for i, n in enumerate(names):
            shp = wts[n].shape
            two_d = (lambda a: a[0]) if n in BIG_WEIGHTS else (lambda a: a)
            dl, nm, nv = _adamw(two_d(wts[n]), grads[n], two_d(mom[n]), two_d(var[n]), name=f"adamw_{n}",
                                deps=deps if i == 0 else ())
            grads[n] = grads[n].reshape(shp)
            deltas[n], new_m[n], new_v[n] = dl.reshape(shp), nm.reshape(shp), nv.reshape(shp)
        return [deltas[n] for n in names]

    def ffn_grads(joined, tag, deps=()):
        nff = lay.nff
        grads[f"{tag}_w_gate"] = natural(joined[0], 0, nff, True)
        grads[f"{tag}_w_up"] = natural(joined[0], nff, nff, True)
        grads[f"{tag}_w_down"] = natural(joined[0], 2 * nff, nff, False)
        return adam([f"{tag}_w_gate", f"{tag}_w_up", f"{tag}_w_down"], deps)

    swaps = {}

    def start_swap(tag):
        def hook(gbuf):
            swaps[tag], started = _swap_start([gbuf], name=f"grad_swap_{tag}")
            return [started]
        return hook

    dx2, grads["ffn2_norm_pre"], grads["ffn2_norm_post"] = _ffn_bwd(
        dx3, saved2, ffn2_norm_pre, ffn2_norm_post, w_ffn2, lay, "ffn2", after_dw=start_swap("ffn2"))

    gwide = lax.empty((2, 10 * d, d // 2), F32)
    gnarrow = lax.empty((2, lay.rows_narrow, q_lora // 2), F32)
    dy_mix, grads["mix_norm_post"] = _norm_bwd(y_mix, mix_norm_post, dx2, name="mix_norm_post_bwd")
    dmerged = _mm([(dy_mix, w_o_v["w_out"])], name="mix_out_dx", mode="nt")
    gwide = _mm([(merged, dy_mix)], name="mix_out_dw", mode="tn", into=(gwide, 9 * d))
    dga, dgb, dy_a, dy_b = _merge_bwd(dmerged, proj, col_ga, col_gb, y_a, y_b, name="mix_merge_bwd")

    do_mla = _mm([(dy_a, w_o_v["mla_w_o"])], name="mla_out_dx", mode="nt")
    gwide = _mm([(o_mla, dy_a)], name="mla_out_dw", mode="tn", into=(gwide, 7 * d))
    dqcat, dkv, dkr = _attn_bwd(qcat, kvb, krot, do_mla, name="mla_attention_bwd", scale=scale)
    exch2, tok = reduce_mid(swaps["ffn2"], [dkr], _ffn_jobs(lay), "ffn2")

    dqp = _rope(dqcat, tabs, name="rope_q_bwd", group=QGROUP, backward=True, out_dtype=BF16)
    dk_r = _rope(dkr, tabs, name="rope_k_bwd", group=LANE, backward=True, out_dtype=BF16)
    dcqn = _mm([(dqp, w_q_v)], name="mla_q_up_dx", mode="nn", deps=[tok])
    gnarrow = _mm([(dqp, cqn)], name="mla_q_up_dw", mode="tn", into=(gnarrow, lay.off_q))
    dkvb = dkv.astype(BF16)
    dckvn = _mm([(dkvb, w_kv_v)], name="mla_kv_up_dx", mode="nn")
    gnarrow = _mm([(dkvb, ckvn)], name="mla_kv_up_dw", mode="tn", into=(gnarrow, lay.off_kv))
    dc_q, grads["mla_q_norm"] = _norm_bwd(proj, mla_q_norm, dcqn, name="mla_q_norm_bwd", col=0, dx_dtype=BF16)
    dc_kv, grads["mla_kv_norm"] = _norm_bwd(proj, mla_kv_norm, dckvn, name="mla_kv_norm_bwd", col=q_lora, dx_dtype=BF16)

    dyb = _mm([(dy_b, w_o_v["hgrn_w_o"])], name="hgrn_out_dx", mode="nt")
    gwide = _mm([(yb, dy_b)], name="hgrn_out_dw", mode="tn", into=(gwide, 8 * d))
    dhq, dhf, dhi, dhg, dlb_h, dnorm_h = _hgrn_bwd(proj, hgrn_cols, d, o_raw, dyb, states, hgrn_lb_logits, hgrn_out_norm,
                                                   name="hgrn_scan_bwd")

    dhead = jnp.concatenate([dc_q, dc_kv, dk_r, jnp.zeros((t, d - col_kr - LANE), BF16)], axis=1)
    dparts = [dhead, dhq, dhf, dhi, dhg, dga, dgb]
    dh2 = _mm([(p, (wide, i * d, d)) for i, p in enumerate(dparts)], name="mix_in_dx", mode="nn")
    for i, p in enumerate(dparts):
        gwide = _mm([(p, h2)], name=f"mix_in_dw_{i}", mode="tn", into=(gwide, i * d))
    dx1, grads["mix_norm_pre"] = _norm_bwd(x1, mix_norm_pre, dh2, name="mix_norm_pre_bwd", dres=dx2)
    swap_m, tok = _swap_start([gwide, gnarrow], name="grad_swap_mix")
    joined2 = reduce_end(exch2, [dx1], "ffn2")

    exchanges = {}

    def mix_exchange(after):
        exchanges["mix"], started = reduce_mid(swap_m, [after], _mix_jobs(lay), "mix")
        return [started]

    dx0, grads["ffn1_norm_pre"], grads["ffn1_norm_post"] = _ffn_bwd(
        dx1, saved1, ffn1_norm_pre, ffn1_norm_post, w_ffn1, lay, "ffn1", deps=[tok], after_act=mix_exchange,
        after_dw=start_swap("ffn1"))
    exch1, tok = reduce_mid(swaps["ffn1"], [dx0], _ffn_jobs(lay), "ffn1")

    joined_m = reduce_end(exchanges["mix"], [dx0, tok], "mix")
    done = ffn_grads(joined2, "ffn2")
    grads["w_in"] = natural(jnp.concatenate([joined_m[0], joined_m[1]], axis=1), 0, lay.ncol, True)
    for i, n in enumerate(("mla_w_o", "hgrn_w_o", "w_out")):
        grads[n] = natural(joined_m[2], i * lay.r_o, lay.r_o, False)
    grads["mla_w_q_up"] = natural(joined_m[3], 0, lay.hps * (HEAD + ROPE), True)
    grads["mla_w_kv_up"] = natural(joined_m[4], 0, lay.hps * QGROUP, True)
    done += adam(["w_in", "mla_w_q_up", "mla_w_kv_up", "mla_w_o", "hgrn_w_o", "w_out"])

    joined1 = reduce_end(exch1, done, "ffn1")

    dlb = dlb_h.reshape(1, -1)
    dnorm = jnp.sum(dnorm_h, axis=0)
    small = {**{n: grads[n] for n in SMALL_WEIGHTS if n not in ("hgrn_lb_logits", "hgrn_out_norm")},
             "hgrn_lb_logits": dlb, "hgrn_out_norm": dnorm}
    vec, _ = lax.optimization_barrier((jnp.concatenate([small[n] for n in SMALL_WEIGHTS], axis=1), joined1[0]))
    vec = _all_reduce_small(vec, name="grad_all_reduce_small")
    off = 0
    for n in SMALL_WEIGHTS:
        w_n = small[n].shape[1]
        grads[n] = vec[:, off:off + w_n]
        off += w_n
    grads["hgrn_lb_logits"] = _lb_logits_grad(hgrn_lb_logits, grads["hgrn_lb_logits"], name="lb_logits_grad")

    adam(list(SMALL_WEIGHTS))
    ffn_grads(joined1, "ffn1")

    loss = lax.psum(loss_local, ("x", "y", "c"))
    dx_out = dx0.reshape(x.shape)
    return (loss, dx_out, *[grads[n] for n in ALL_WEIGHTS], *[deltas[n] for n in ALL_WEIGHTS],
            *[new_m[n] for n in ALL_WEIGHTS], *[new_v[n] for n in ALL_WEIGHTS])
```

```python
import jax
import jax.numpy as jnp
from jax import lax
from jax.experimental import pallas as pl
from jax.experimental.pallas import tpu as pltpu

F32 = jnp.float32
BF16 = jnp.bfloat16
MESH = pl.DeviceIdType.MESH

NORM_EPS = 1e-6
MACARON_SCALE = 0.5
ROPE_THETA = 10000.0
HEAD = 128
ROPE = 64
QGROUP = 2 * HEAD
SUB = 16
ADAM_LR, ADAM_B1, ADAM_B2, ADAM_EPS, ADAM_WD, ADAM_STEP = 0.001, 0.9, 0.999, 1e-08, 0.01, 10

LANE = 128
VMEM_LIMIT = 48 * 1024 * 1024
VMEM_LIMIT_WIDE = 56 * 1024 * 1024
MM_TILE = 1024
MM_TILE_WIDE = 1536

BIG_WEIGHTS = ("ffn1_w_gate", "ffn1_w_up", "ffn1_w_down", "w_in", "mla_w_q_up", "mla_w_kv_up",
               "mla_w_o", "hgrn_w_o", "w_out", "ffn2_w_gate", "ffn2_w_up", "ffn2_w_down")
SMALL_WEIGHTS = ("ffn1_norm_pre", "ffn1_norm_post", "mix_norm_pre", "mla_q_norm", "mla_kv_norm",
                 "hgrn_lb_logits", "hgrn_out_norm", "mix_norm_post", "ffn2_norm_pre", "ffn2_norm_post")
ALL_WEIGHTS = ("ffn1_norm_pre", "ffn1_w_gate", "ffn1_w_up", "ffn1_w_down", "ffn1_norm_post", "mix_norm_pre",
               "w_in", "mla_q_norm", "mla_w_q_up", "mla_kv_norm", "mla_w_kv_up", "mla_w_o", "hgrn_lb_logits",
               "hgrn_out_norm", "hgrn_w_o", "w_out", "mix_norm_post", "ffn2_norm_pre", "ffn2_w_gate",
               "ffn2_w_up", "ffn2_w_down", "ffn2_norm_post")


def _params(*sem, vmem=VMEM_LIMIT):
    return pltpu.CompilerParams(dimension_semantics=sem or None, vmem_limit_bytes=vmem)


def _pick(n, cap, offset=0):
    if n <= cap and offset % n == 0:
        return n
    best = None
    for t in range(LANE, min(n, cap) + 1, LANE):
        if n % t == 0 and offset % t == 0:
            best = t
    assert best is not None, (n, cap, offset)
    return best


def _row_tile(n, row_bytes, budget=2 << 20):
    best = None
    for t in range(8, n + 1, 8):
        if n % t == 0 and t * row_bytes <= budget:
            best = t
    return n if best is None else best


def _sigmoid(x):
    return 1.0 / (1.0 + jnp.exp(-x))


def _silu(x):
    return x * _sigmoid(x)


def _dsilu(x):
    s = _sigmoid(x)
    return s * (1.0 + x * (1.0 - s))


def _mm(pairs, *, name, mode="nn", out_dtype=F32, into=None, deps=(), extras=(), epilogue=None, out_dtypes=None, tm_cap=None,
        wide_vmem=False):
    halves = isinstance(pairs[0][1], tuple)
    assert halves or mode == "tn"
    pairs = [(a, b if halves else (b, 0, b.shape[0])) for a, b in pairs]
    a0, (b0, b_off, b_rows) = pairs[0]
    hw = b0.shape[2] if halves else (into[0].shape[2] if into is not None else None)
    if mode == "nn":
        (m, kdim), n = a0.shape, 2 * hw
    elif mode == "nt":
        (m, kdim), n = a0.shape, b_rows
        assert kdim == 2 * hw
    else:
        (kdim, m), n = a0.shape, b0.shape[1]
    out_off = 0 if into is None else into[1]
    tm = _pick(m, tm_cap or (MM_TILE_WIDE if mode == "tn" else MM_TILE), out_off)
    tn = hw if (mode == "nn" or into is not None) else _pick(n, MM_TILE_WIDE, b_off if mode == "nt" else 0)
    tk = hw if mode == "nt" else _pick(kdim, MM_TILE if len(pairs) <= 2 else MM_TILE // 2, b_off if mode == "nn" else 0)
    assert n % tn == 0 and kdim % tk == 0
    nk = kdim // tk
    npair = len(pairs)
    dims = {"nn": (((1,), (0,)), ((), ())), "nt": (((1,), (1,)), ((), ())), "tn": (((0,), (0,)), ((), ()))}[mode]

    nout = 1 if epilogue is None else len(out_dtypes)

    def body(*refs):
        ins, x_refs = refs[:2 * npair], refs[2 * npair:2 * npair + len(extras)]
        o_refs, acc_ref = refs[-1 - nout:-1], refs[-1]
        k = pl.program_id(2)

        @pl.when(k == 0)
        def _():
            acc_ref[...] = jnp.zeros_like(acc_ref)

        for p in range(npair):
            a = ins[2 * p][...].astype(BF16)
            b = ins[2 * p + 1][...].astype(BF16)
            acc_ref[...] += lax.dot_general(a, b, dims, preferred_element_type=F32)

        @pl.when(k == nk - 1)
        def _():
            outs = (acc_ref[...],) if epilogue is None else epilogue(acc_ref[...], *[x[...] for x in x_refs])
            for o_ref, o in zip(o_refs, outs):
                o_ref[...] = o.astype(o_ref.dtype)

    a_spec = pl.BlockSpec((tk, tm), lambda i, j, k: (k, i)) if mode == "tn" else pl.BlockSpec((tm, tk), lambda i, j, k: (i, k))
    in_specs, flat = [], []
    for a, (b, off, _) in pairs:
        if mode == "nt":
            b_spec = pl.BlockSpec((None, tn, tk), lambda i, j, k, o=off // tn: (k, j + o, 0))
        elif mode == "nn":
            b_spec = pl.BlockSpec((None, tk, tn), lambda i, j, k, o=off // tk: (j, k + o, 0))
        else:
            b_spec = pl.BlockSpec((tk, tn), lambda i, j, k: (k, j))
        in_specs += [a_spec, b_spec]
        flat += [a, b]
    for extra in extras:
        in_specs.append(pl.BlockSpec((tm, tn), lambda i, j, k: (i, j)))
        flat.append(extra)
    for dep in deps:
        in_specs.append(pl.BlockSpec(memory_space=pl.ANY))
        flat.append(dep)
    if epilogue is not None:
        assert into is None
        out_shape, aliases = [jax.ShapeDtypeStruct((m, n), dt) for dt in out_dtypes], {}
        out_spec = [pl.BlockSpec((tm, tn), lambda i, j, k: (i, j))] * nout
    elif into is None:
        out_shape, aliases = jax.ShapeDtypeStruct((m, n), out_dtype), {}
        out_spec = pl.BlockSpec((tm, tn), lambda i, j, k: (i, j))
    else:
        out_shape, aliases = jax.ShapeDtypeStruct(into[0].shape, into[0].dtype), {len(flat): 0}
        out_spec = pl.BlockSpec((None, tm, tn), lambda i, j, k, o=out_off // tm: (j, i + o, 0))
        in_specs.append(pl.BlockSpec(memory_space=pl.ANY))
        flat.append(into[0])
    return pl.pallas_call(
        body, name=name, grid=(m // tm, n // tn, nk),
        in_specs=in_specs,
        out_specs=out_spec,
        out_shape=out_shape, input_output_aliases=aliases,
        scratch_shapes=[pltpu.VMEM((tm, tn), F32)],
        compiler_params=_params("parallel", "parallel", "arbitrary", vmem=VMEM_LIMIT_WIDE if wide_vmem else VMEM_LIMIT),
    )(*flat)


def _norm_fwd(y, w, *, name, resid=None, scale=1.0, out_dtype=F32, col=0):
    t, d = y.shape[0], w.shape[1]
    tr = _pick(t, 256)
    assert col % d == 0

    def body(*refs):
        if resid is None:
            y_ref, w_ref, o_ref = refs
        else:
            y_ref, w_ref, r_ref, o_ref = refs
        yv = y_ref[...]
        out = yv * lax.rsqrt(jnp.mean(yv * yv, axis=-1, keepdims=True) + NORM_EPS) * w_ref[...]
        if resid is not None:
            out = r_ref[...] + scale * out
        o_ref[...] = out.astype(out_dtype)

    row = pl.BlockSpec((tr, d), lambda i: (i, 0))
    wspec = pl.BlockSpec((1, d), lambda i: (0, 0))
    ins, specs = [y, w], [pl.BlockSpec((tr, d), lambda i: (i, col // d)), wspec]
    if resid is not None:
        ins.append(resid)
        specs.append(row)
    return pl.pallas_call(
        body, name=name, grid=(t // tr,), in_specs=specs, out_specs=row,
        out_shape=jax.ShapeDtypeStruct((t, d), out_dtype), compiler_params=_params("parallel"),
    )(*ins)


def _norm_bwd(x, w, dy, *, name, scale=1.0, dres=None, col=0, dx_dtype=F32):
    t, d = x.shape[0], w.shape[1]
    tr = _pick(t, 256)
    assert col % d == 0

    def body(*refs):
        if dres is None:
            x_ref, w_ref, dy_ref, dx_ref, dw_ref = refs
        else:
            x_ref, w_ref, dy_ref, dr_ref, dx_ref, dw_ref = refs

        @pl.when(pl.program_id(0) == 0)
        def _():
            dw_ref[...] = jnp.zeros_like(dw_ref)

        xv = x_ref[...]
        r = lax.rsqrt(jnp.mean(xv * xv, axis=-1, keepdims=True) + NORM_EPS)
        xhat = xv * r
        dyv = dy_ref[...].astype(F32) * scale
        dw_ref[...] += jnp.sum(dyv * xhat, axis=0, keepdims=True)
        t_ = dyv * w_ref[...]
        dx = r * (t_ - xhat * jnp.mean(t_ * xhat, axis=-1, keepdims=True))
        if dres is not None:
            dx = dx + dr_ref[...]
        dx_ref[...] = dx.astype(dx_dtype)

    row = pl.BlockSpec((tr, d), lambda i: (i, 0))
    wspec = pl.BlockSpec((1, d), lambda i: (0, 0))
    ins, specs = [x, w, dy], [pl.BlockSpec((tr, d), lambda i: (i, col // d)), wspec, row]
    if dres is not None:
        ins.append(dres)
        specs.append(row)
    return pl.pallas_call(
        body, name=name, grid=(t // tr,), in_specs=specs, out_specs=(row, wspec),
        out_shape=(jax.ShapeDtypeStruct((t, d), dx_dtype), jax.ShapeDtypeStruct((1, d), F32)),
        compiler_params=_params("arbitrary"),
    )(*ins)


def _elementwise(fn, ins, out_dtypes, *, name, width=None, cols=None):
    t = ins[0].shape[0]
    d = ins[0].shape[1] if width is None else width
    cols = [0] * len(ins) if cols is None else cols
    tc = _pick(d, 2048)
    for c in cols:
        tc = _pick(d, tc, c)
    tr = _row_tile(t, tc * 4)
    nout = len(out_dtypes)

    def body(*refs):
        outs = fn(*[r[...].astype(F32) for r in refs[:len(ins)]])
        for o_ref, o in zip(refs[len(ins):], outs):
            o_ref[...] = o.astype(o_ref.dtype)

    spec = pl.BlockSpec((tr, tc), lambda i, j: (i, j))
    in_specs = [pl.BlockSpec((tr, tc), lambda i, j, o=c // tc: (i, j + o)) for c in cols]
    return pl.pallas_call(
        body, name=name, grid=(t // tr, d // tc), in_specs=in_specs, out_specs=[spec] * nout,
        out_shape=[jax.ShapeDtypeStruct((t, d), dt) for dt in out_dtypes],
        compiler_params=_params("parallel", "parallel"),
    )(*ins)


def _merge_fwd(proj, col_a, col_b, ya, yb, *, name):
    return _elementwise(lambda a, b, p, q: (_sigmoid(a) * p + _sigmoid(b) * q,), [proj, proj, ya, yb], [BF16],
                        name=name, width=ya.shape[1], cols=[col_a, col_b, 0, 0])[0]


def _merge_bwd(dm, proj, col_a, col_b, ya, yb, *, name):
    def fn(dmv, a, b, p, q):
        sa, sb = _sigmoid(a), _sigmoid(b)
        return dmv * p * sa * (1.0 - sa), dmv * q * sb * (1.0 - sb), dmv * sa, dmv * sb

    return _elementwise(fn, [dm, proj, proj, ya, yb], [BF16, BF16, BF16, BF16], name=name, width=ya.shape[1],
                        cols=[0, col_a, col_b, 0, 0])


def _loss_head(xo, target, *, name):
    t, d = xo.shape
    tr = _pick(t, 256)

    def body(x_ref, t_ref, dx_ref, l_ref):
        @pl.when(pl.program_id(0) == 0)
        def _():
            l_ref[...] = jnp.zeros_like(l_ref)

        err = x_ref[...] - t_ref[...]
        dx_ref[...] = err * (1.0 / d)
        l_ref[...] += 0.5 * jnp.sum(jnp.mean(err * err, axis=-1, keepdims=True), axis=0, keepdims=True)

    row = pl.BlockSpec((tr, d), lambda i: (i, 0))
    dx, l = pl.pallas_call(
        body, name=name, grid=(t // tr,), in_specs=[row, row],
        out_specs=(row, pl.BlockSpec((1, 1), lambda i: (0, 0))),
        out_shape=(jax.ShapeDtypeStruct((t, d), F32), jax.ShapeDtypeStruct((1, 1), F32)),
        compiler_params=_params("arbitrary"),
    )(xo, target)
    return dx, l[0, 0]


def _rope(xin, tabs, *, name, group, backward, out_dtype, col=0, ngroup=None):
    t = xin.shape[0]
    ngroup = xin.shape[1] // group if ngroup is None else ngroup
    wdt = ngroup * group
    tr = _pick(t, 256)
    assert col % wdt == 0
    cos_t, nsin_t, sin_t = tabs

    def body(x_ref, c_ref, n_ref, s_ref, o_ref):
        cv, nv, sv = c_ref[...], n_ref[...], s_ref[...]
        for g in range(ngroup):
            lo, hi = g * group, (g + 1) * group
            rot = x_ref[:, hi - LANE:hi].astype(F32)
            if backward:
                out = rot * cv + pltpu.roll(rot * nv, 32, 1) + pltpu.roll(rot * sv, LANE - 32, 1)
            else:
                out = rot * cv + pltpu.roll(rot, LANE - 32, 1) * nv + pltpu.roll(rot, 32, 1) * sv
            if group > LANE:
                o_ref[:, lo:hi - LANE] = x_ref[:, lo:hi - LANE].astype(out_dtype)
            o_ref[:, hi - LANE:hi] = out.astype(out_dtype)

    xspec = pl.BlockSpec((tr, wdt), lambda i: (i, 0))
    tspec = pl.BlockSpec((tr, LANE), lambda i: (i, 0))
    return pl.pallas_call(
        body, name=name, grid=(t // tr,),
        in_specs=[pl.BlockSpec((tr, wdt), lambda i: (i, col // wdt)), tspec, tspec, tspec], out_specs=xspec,
        out_shape=jax.ShapeDtypeStruct((t, wdt), out_dtype), compiler_params=_params("parallel"),
    )(xin, cos_t, nsin_t, sin_t)


def _scores(q, kv, kr, qi, tq, scale):
    kcat = jnp.concatenate([kv[:, :HEAD], kr], axis=1)
    s = lax.dot_general(q, kcat, (((1,), (1,)), ((), ())), preferred_element_type=F32) * scale
    row = qi * tq + lax.broadcasted_iota(jnp.int32, s.shape, 0)
    col = lax.broadcasted_iota(jnp.int32, s.shape, 1)
    s = jnp.where(col <= row, s, -jnp.inf)
    p = jnp.exp(s - jnp.max(s, axis=-1, keepdims=True))
    return p / jnp.sum(p, axis=-1, keepdims=True), kcat


def _attn_fwd(qcat, kv, kr, *, name, scale):
    t = qcat.shape[0]
    nh = qcat.shape[1] // QGROUP
    tq = _pick(t, 256)

    def body(q_ref, kv_ref, kr_ref, o_ref):
        for qi in range(t // tq):
            @pl.when(pl.program_id(1) == qi)
            def _(qi=qi):
                kvv = kv_ref[0:(qi + 1) * tq, :]
                p, _ = _scores(q_ref[...], kvv, kr_ref[0:(qi + 1) * tq, :], qi, tq, scale)
                o_ref[...] = jnp.dot(p.astype(BF16), kvv[:, HEAD:], preferred_element_type=F32).astype(BF16)

    return pl.pallas_call(
        body, name=name, grid=(nh, t // tq),
        in_specs=[pl.BlockSpec((tq, QGROUP), lambda h, i: (i, h)), pl.BlockSpec((t, QGROUP), lambda h, i: (0, h)),
                  pl.BlockSpec((t, LANE), lambda h, i: (0, 0))],
        out_specs=pl.BlockSpec((tq, HEAD), lambda h, i: (i, h)),
        out_shape=jax.ShapeDtypeStruct((t, nh * HEAD), BF16), compiler_params=_params("parallel", "parallel"),
    )(qcat, kv, kr)


def _attn_bwd(qcat, kv, kr, do, *, name, scale):
    t = qcat.shape[0]
    nh = qcat.shape[1] // QGROUP
    tq = _pick(t, 256)
    nq = t // tq

    def body(q_ref, kv_ref, kr_ref, do_ref, dq_ref, dkv_ref, dkr_ref, dk_acc, dv_acc):
        h, i = pl.program_id(0), pl.program_id(1)

        @pl.when(i == 0)
        def _():
            dk_acc[...] = jnp.zeros_like(dk_acc)
            dv_acc[...] = jnp.zeros_like(dv_acc)

        @pl.when((i == 0) & (h == 0))
        def _():
            dkr_ref[...] = jnp.zeros_like(dkr_ref)

        for qi in range(nq):
            @pl.when(i == qi)
            def _(qi=qi):
                keys = slice(0, (qi + 1) * tq)
                q = q_ref[...]
                kvv = kv_ref[keys, :]
                dov = do_ref[...].astype(BF16)
                p, kcat = _scores(q, kvv, kr_ref[keys, :], qi, tq, scale)
                dp = lax.dot_general(dov, kvv[:, HEAD:], (((1,), (1,)), ((), ())), preferred_element_type=F32)
                ds = (p * (dp - jnp.sum(p * dp, axis=-1, keepdims=True)) * scale).astype(BF16)
                dq_ref[...] = jnp.dot(ds, kcat, preferred_element_type=F32)
                dk_acc[keys, :] += lax.dot_general(ds, q, (((0,), (0,)), ((), ())), preferred_element_type=F32)
                dv_acc[keys, :] += lax.dot_general(p.astype(BF16), dov, (((0,), (0,)), ((), ())), preferred_element_type=F32)

        @pl.when(i == nq - 1)
        def _():
            dk = dk_acc[...]
            dkv_ref[...] = jnp.concatenate([dk[:, :HEAD], dv_acc[...]], axis=1)
            dkr_ref[...] += dk[:, HEAD:]

    return pl.pallas_call(
        body, name=name, grid=(nh, nq),
        in_specs=[pl.BlockSpec((tq, QGROUP), lambda h, i: (i, h)), pl.BlockSpec((t, QGROUP), lambda h, i: (0, h)),
                  pl.BlockSpec((t, LANE), lambda h, i: (0, 0)), pl.BlockSpec((tq, HEAD), lambda h, i: (i, h))],
        out_specs=(pl.BlockSpec((tq, QGROUP), lambda h, i: (i, h)), pl.BlockSpec((t, QGROUP), lambda h, i: (0, h)),
                   pl.BlockSpec((t, LANE), lambda h, i: (0, 0))),
        out_shape=(jax.ShapeDtypeStruct((t, nh * QGROUP), F32), jax.ShapeDtypeStruct((t, nh * QGROUP), F32),
                   jax.ShapeDtypeStruct((t, LANE), F32)),
        scratch_shapes=[pltpu.VMEM((t, QGROUP), F32), pltpu.VMEM((t, HEAD), F32)],
        compiler_params=_params("arbitrary", "arbitrary"),
    )(qcat, kv, kr, do)


def _split3(x):
    hi = x.astype(BF16)
    r1 = x - hi.astype(F32)
    mid = r1.astype(BF16)
    lo = (r1 - mid.astype(F32)).astype(BF16)
    return hi, mid, lo


def _tri_matmul(mask, x):
    m = mask.astype(BF16)
    return sum(jnp.dot(m, part, preferred_element_type=F32) for part in _split3(x))


def _tri_masks(tb):
    row = lax.broadcasted_iota(jnp.int32, (tb, tb), 0)
    col = lax.broadcasted_iota(jnp.int32, (tb, tb), 1)
    same = col // SUB == row // SUB
    return jnp.stack([(col <= row) & same, (col >= row) & same, (col < row) & same]).astype(BF16)


def _sub_cumsum(g, masks):
    return _tri_matmul(masks[0], g)


def _sub_suffix_prefix(after, before, masks):
    return _tri_matmul(masks[1], after) + _tri_matmul(masks[2], before)


def _lower_bound(logits):
    mx = jnp.max(logits, axis=0, keepdims=True)
    e = jnp.exp(logits - mx)
    return e[0:1, :] / jnp.sum(e, axis=0, keepdims=True)


def _hgrn_fwd(proj, cols, wdt, logits, out_norm, *, name):
    t = proj.shape[0]
    nh = wdt // HEAD
    tb = _pick(t, 128)
    ns = tb // SUB

    def body(hq_ref, hf_ref, hi_ref, hg_ref, lg_ref, w_ref, m_ref, o_ref, yb_ref, st_ref, s_ref, q_s, k_s, b_s):
        @pl.when(pl.program_id(1) == 0)
        def _():
            s_ref[...] = jnp.zeros_like(s_ref)

        lb = _lower_bound(lg_ref[...])
        f = lb + (1.0 - lb) * _sigmoid(hf_ref[...])
        q_s[...] = _silu(hq_ref[...])
        k_s[...] = 1.0 - f
        b_s[...] = _sub_cumsum(jnp.log(f), m_ref)
        rowid = lax.broadcasted_iota(jnp.int32, (SUB, HEAD), 0)

        def sub(c, st):
            rows = pl.ds(pl.multiple_of(c * SUB, SUB), SUB)
            qc, kc, bc, vc = q_s[rows, :], k_s[rows, :], b_s[rows, :], hi_ref[rows, :]
            st_ref[0, c] = st
            bl = bc[SUB - 1:SUB, :]
            oc = lax.dot_general((qc * jnp.exp(bc)).astype(BF16), st.astype(BF16), (((1,), (1,)), ((), ())),
                                 preferred_element_type=F32)
            for s in range(SUB):
                e = jnp.where(rowid >= s, jnp.exp(bc - bc[s:s + 1, :]), 0.0)
                a = jnp.sum(qc * e * kc[s:s + 1, :], axis=1, keepdims=True)
                oc = oc + a * vc[s:s + 1, :]
            o_ref[rows, :] = oc
            kd = kc * jnp.exp(bl - bc)
            return jnp.exp(bl) * st + lax.dot_general(vc.astype(BF16), kd.astype(BF16), (((0,), (0,)), ((), ())),
                                                      preferred_element_type=F32)

        s_ref[...] = lax.fori_loop(0, ns, sub, s_ref[...], unroll=True)
        o = o_ref[...]
        r = lax.rsqrt(jnp.mean(o * o, axis=-1, keepdims=True) + NORM_EPS)
        yb_ref[...] = (o * r * w_ref[...] * _silu(hg_ref[...])).astype(BF16)

    blk = pl.BlockSpec((tb, HEAD), lambda h, j: (j, h))
    return pl.pallas_call(
        body, name=name, grid=(nh, t // tb),
        in_specs=[pl.BlockSpec((tb, HEAD), lambda h, j, o=c // HEAD: (j, h + o)) for c in cols]
        + [pl.BlockSpec((2, HEAD), lambda h, j: (0, h)), pl.BlockSpec((1, HEAD), lambda h, j: (0, 0)),
           pl.BlockSpec((3, tb, tb), lambda h, j: (0, 0, 0))],
        out_specs=(blk, blk, pl.BlockSpec((1, ns, HEAD, HEAD), lambda h, j: (h, j, 0, 0))),
        out_shape=(jax.ShapeDtypeStruct((t, wdt), F32), jax.ShapeDtypeStruct((t, wdt), BF16),
                   jax.ShapeDtypeStruct((nh, t // SUB, HEAD, HEAD), F32)),
        scratch_shapes=[pltpu.VMEM((HEAD, HEAD), F32)] + [pltpu.VMEM((tb, HEAD), F32)] * 3,
        compiler_params=_params("parallel", "arbitrary"),
    )(proj, proj, proj, proj, logits, out_norm, _tri_masks(tb))


def _hgrn_bwd(proj, cols, wdt, o_raw, dyb, states, logits, out_norm, *, name):
    t = proj.shape[0]
    nh = wdt // HEAD
    tb = _pick(t, 128)
    ns = tb // SUB
    nb = t // tb

    def body(hq_ref, hf_ref, hi_ref, hg_ref, o_ref, dy_ref, st_ref, lg_ref, w_ref, m_ref,
             dhq_ref, dhf_ref, dhi_ref, dhg_ref, dlb_ref, dw_ref,
             ds_ref, q_s, k_s, b_s, do_s, dq_s, dk_s, dv_s, after_s, before_s, thru_s):
        @pl.when(pl.program_id(1) == 0)
        def _():
            ds_ref[...] = jnp.zeros_like(ds_ref)
            dlb_ref[...] = jnp.zeros_like(dlb_ref)
            dw_ref[...] = jnp.zeros_like(dw_ref)

        lb = _lower_bound(lg_ref[...])
        hqv, hgv = hq_ref[...], hg_ref[...]
        sig = _sigmoid(hf_ref[...])
        f = lb + (1.0 - lb) * sig
        q_s[...] = _silu(hqv)
        k_s[...] = 1.0 - f
        b_s[...] = _sub_cumsum(jnp.log(f), m_ref)

        o = o_ref[...]
        r = lax.rsqrt(jnp.mean(o * o, axis=-1, keepdims=True) + NORM_EPS)
        nrm = o * r
        w = w_ref[...]
        dy = dy_ref[...].astype(F32)
        dhg_ref[...] = (dy * nrm * w * _dsilu(hgv)).astype(BF16)
        dnw = dy * _silu(hgv)
        dw_ref[0] += jnp.sum(dnw * nrm, axis=0, keepdims=True)
        tt = dnw * w
        do_s[...] = r * (tt - nrm * jnp.mean(tt * nrm, axis=-1, keepdims=True))
        rowid = lax.broadcasted_iota(jnp.int32, (SUB, HEAD), 0)

        def sub(cc, dst):
            c = ns - 1 - cc
            rows = pl.ds(pl.multiple_of(c * SUB, SUB), SUB)
            qc, kc, bc, vc, doc = q_s[rows, :], k_s[rows, :], b_s[rows, :], hi_ref[rows, :], do_s[rows, :]
            st = st_ref[0, c]
            bl = bc[SUB - 1:SUB, :]
            eb = jnp.exp(bc)
            ekd = jnp.exp(bl - bc)
            qe, kd = qc * eb, kc * ekd
            dob, vcb = doc.astype(BF16), vc.astype(BF16)
            dq_st = jnp.dot(dob, st.astype(BF16), preferred_element_type=F32) * eb
            dk_st = jnp.dot(vcb, dst.astype(BF16), preferred_element_type=F32) * ekd
            dv = lax.dot_general(kd.astype(BF16), dst.astype(BF16), (((1,), (1,)), ((), ())), preferred_element_type=F32)
            dq_in = jnp.zeros_like(qc)
            dk_in = jnp.zeros_like(qc)
            for s in range(SUB):
                e = jnp.where(rowid >= s, jnp.exp(bc - bc[s:s + 1, :]), 0.0)
                ek = e * kc[s:s + 1, :]
                a = jnp.sum(qc * ek, axis=1, keepdims=True)
                da = jnp.sum(doc * vc[s:s + 1, :], axis=1, keepdims=True)
                dq_in = dq_in + da * ek
                dk_in = dk_in + jnp.where(rowid == s, jnp.sum(da * e * qc, axis=0, keepdims=True), 0.0)
                dv = dv + jnp.where(rowid == s, jnp.sum(a * doc, axis=0, keepdims=True), 0.0)
            ebl = jnp.exp(bl)
            dq_s[rows, :] = dq_st + dq_in
            dk_s[rows, :] = dk_st + dk_in
            dv_s[rows, :] = dv
            after_s[rows, :] = qc * (dq_st + dq_in) - kc * dk_in
            before_s[rows, :] = kc * dk_st
            thru_s[rows, :] = jnp.broadcast_to(ebl * jnp.sum(st * dst, axis=0, keepdims=True), (SUB, HEAD))
            return ebl * dst + lax.dot_general(dob, qe.astype(BF16), (((0,), (0,)), ((), ())), preferred_element_type=F32)

        ds_ref[...] = lax.fori_loop(0, ns, sub, ds_ref[...], unroll=True)
        dg = _sub_suffix_prefix(after_s[...], before_s[...], m_ref) + thru_s[...]
        dhq_ref[...] = (dq_s[...] * _dsilu(hqv)).astype(BF16)
        dft = dg / f - dk_s[...]
        dhf_ref[...] = (dft * (1.0 - lb) * sig * (1.0 - sig)).astype(BF16)
        dlb_ref[0] += jnp.sum(dft * (1.0 - sig), axis=0, keepdims=True)
        dhi_ref[...] = dv_s[...].astype(BF16)

    blk = pl.BlockSpec((tb, HEAD), lambda h, j: (nb - 1 - j, h))
    vec = pl.BlockSpec((1, 1, HEAD), lambda h, j: (h, 0, 0))
    tok = jax.ShapeDtypeStruct((t, wdt), BF16)
    per_head = jax.ShapeDtypeStruct((nh, 1, HEAD), F32)
    return pl.pallas_call(
        body, name=name, grid=(nh, nb),
        in_specs=[pl.BlockSpec((tb, HEAD), lambda h, j, o=c // HEAD: (nb - 1 - j, h + o)) for c in cols]
        + [blk, blk] + [pl.BlockSpec((1, ns, HEAD, HEAD), lambda h, j: (h, nb - 1 - j, 0, 0)),
                              pl.BlockSpec((2, HEAD), lambda h, j: (0, h)), pl.BlockSpec((1, HEAD), lambda h, j: (0, 0)),
                              pl.BlockSpec((3, tb, tb), lambda h, j: (0, 0, 0))],
        out_specs=(blk, blk, blk, blk, vec, vec),
        out_shape=(tok, tok, tok, tok, per_head, per_head),
        scratch_shapes=[pltpu.VMEM((HEAD, HEAD), F32)] + [pltpu.VMEM((tb, HEAD), F32)] * 10,
        compiler_params=_params("arbitrary", "arbitrary"),
    )(proj, proj, proj, proj, o_raw, dyb, states, logits, out_norm, _tri_masks(tb))


def _lb_logits_grad(logits, dlb, *, name):
    def body(lg_ref, d_ref, o_ref):
        lg = lg_ref[...]
        e = jnp.exp(lg - jnp.max(lg, axis=0, keepdims=True))
        p = e / jnp.sum(e, axis=0, keepdims=True)
        d = d_ref[...]
        rowid = lax.broadcasted_iota(jnp.int32, lg.shape, 0)
        dp = jnp.where(rowid == 0, d, 0.0)
        o_ref[...] = p * (dp - jnp.sum(p * dp, axis=0, keepdims=True))

    return pl.pallas_call(body, name=name, out_shape=jax.ShapeDtypeStruct(logits.shape, F32))(logits, dlb)


def _adamw(w, g, m, v, *, name, deps=()):
    r, c = w.shape
    tc = _pick(c, 2048) if c % LANE == 0 else c
    tr = _row_tile(r, tc * 4)

    def body(w_ref, g_ref, m_ref, v_ref, *rest):
        d_ref, nm_ref, nv_ref = rest[-3:]
        gv = g_ref[...]
        nm = ADAM_B1 * m_ref[...] + (1.0 - ADAM_B1) * gv
        nv = ADAM_B2 * v_ref[...] + (1.0 - ADAM_B2) * (gv * gv)
        m_hat = nm / (1.0 - ADAM_B1 ** ADAM_STEP)
        v_hat = nv / (1.0 - ADAM_B2 ** ADAM_STEP)
        d_ref[...] = -ADAM_LR * (m_hat / (jnp.sqrt(v_hat) + ADAM_EPS) + ADAM_WD * w_ref[...])
        nm_ref[...] = nm
        nv_ref[...] = nv

    spec = pl.BlockSpec((tr, tc), lambda i, j: (i, j))
    shp = jax.ShapeDtypeStruct((r, c), F32)
    return pl.pallas_call(
        body, name=name, grid=(r // tr, c // tc), in_specs=[spec] * 4 + [ANY] * len(deps), out_specs=[spec] * 3,
        out_shape=[shp, shp, shp], compiler_params=_params("parallel", "parallel"),
    )(w, g, m, v, *deps)


def _coords():
    return lax.axis_index("x"), lax.axis_index("y"), lax.axis_index("c")


def _other_chips(x, y):
    return [(1 - x, y), (x, 1 - y), (1 - x, 1 - y)]


ANY = pl.BlockSpec(memory_space=pl.ANY)


class _Layout:
    def __init__(self, d, dff, in_cols, q_lora, kv_lora, nh):
        assert q_lora == kv_lora and nh % 4 == 0 and dff % (4 * LANE) == 0 and in_cols % 4 == 0 and d % 4 == 0
        self.d, self.dff, self.q_lora, self.nh = d, dff, q_lora, nh
        self.head = q_lora + kv_lora + ROPE
        self.pad = d - self.head
        self.nff, self.ncol, self.r_o, self.hps = dff // 4, in_cols // 4, d // 4, nh // 4
        assert self.head <= self.ncol
        self.off_q, self.off_kv, self.rows_narrow = 0, nh * QGROUP, 2 * nh * QGROUP


HBM = pl.BlockSpec(memory_space=pltpu.HBM)
SEMS = pl.BlockSpec(memory_space=pltpu.SEMAPHORE)
SPLIT = dict(has_side_effects=pltpu.SideEffectType.DATAFLOW_SIDE_EFFECTING)


def _in_hbm(a):
    return pltpu.with_memory_space_constraint(a, pltpu.HBM)


def _shard_rows(jobs, k):
    out, lrow = [], [0] * (1 + max(job.a for job in jobs))
    for job in jobs:
        for row, rows in job.pieces(k):
            out.append((job.a, lrow[job.a], row, rows))
            lrow[job.a] += rows
    return out


def _shard_total(jobs, a):
    return sum(rows for b, _, _, rows in _shard_rows(jobs, 0) if b == a)


def _gather_start(packs, lands, jobs, *, name, deps=()):
    n = len(packs)

    def body(*refs):
        p_refs, l_refs, send, recv, token = refs[:n], refs[n:2 * n], refs[-2 * n - 3], refs[-2 * n - 2], refs[-1]
        x, y, c = _coords()
        for a, lrow, row, rows in _shard_rows(jobs, 2 * x + y):
            pltpu.make_async_remote_copy(
                src_ref=p_refs[a].at[:, pl.ds(lrow, rows)], dst_ref=l_refs[a].at[:, pl.ds(row, rows)],
                send_sem=send.at[4 * a + 3], recv_sem=recv.at[4 * a + 3], device_id=(x, y, 1 - c), device_id_type=MESH).start()
            for j, (px, py) in enumerate(_other_chips(x, y)):
                pltpu.make_async_remote_copy(
                    src_ref=p_refs[a].at[c, pl.ds(lrow, rows)], dst_ref=l_refs[a].at[c, pl.ds(row, rows)],
                    send_sem=send.at[4 * a + j], recv_sem=recv.at[4 * a + j], device_id=(px, py, c), device_id_type=MESH).start()
        token[...] = jnp.zeros_like(token)

    thru = [pltpu.HBM(a.shape, a.dtype) for a in packs + lands]
    out = pl.pallas_call(
        body, name=name, in_specs=[HBM] * (2 * n) + [ANY] * len(deps),
        out_shape=(pltpu.SemaphoreType.DMA((4 * n,)), pltpu.SemaphoreType.DMA((4 * n,)), *thru, jax.ShapeDtypeStruct((8, LANE), F32)),
        out_specs=(SEMS, SEMS, *[HBM] * (2 * n), pl.BlockSpec(memory_space=pltpu.VMEM)),
        input_output_aliases={i: 2 + i for i in range(2 * n)}, compiler_params=pltpu.CompilerParams(**SPLIT),
    )(*[_in_hbm(a) for a in packs + lands], *deps)
    return dict(send=out[0], recv=out[1], bufs=list(out[2:2 + 2 * n]), n=n, jobs=jobs), out[-1]


def _gather_wait(handle, after, *, name):
    n, jobs = handle["n"], handle["jobs"]

    def body(*refs):
        l_refs, send, recv, token = refs[n:2 * n], refs[2 * n], refs[2 * n + 1], refs[-1]
        token[...] = jnp.zeros_like(token)
        x, y, c = _coords()
        for a in range(n):
            total = _shard_total(jobs, a)
            for j, like in enumerate([l_refs[a].at[0, pl.ds(0, total)]] * 3 + [l_refs[a].at[:, pl.ds(0, total)]]):
                cp = pltpu.make_async_remote_copy(src_ref=like, dst_ref=like, send_sem=send.at[4 * a + j],
                                                  recv_sem=recv.at[4 * a + j], device_id=(x, y, c), device_id_type=MESH)
                cp.wait_send()
                cp.wait_recv()

    out = pl.pallas_call(
        body, name=name, in_specs=[HBM] * (2 * n) + [SEMS, SEMS] + [ANY] * len(after),
        out_shape=[pltpu.HBM(a.shape, a.dtype) for a in handle["bufs"]] + [jax.ShapeDtypeStruct((8, LANE), F32)],
        out_specs=[HBM] * (2 * n) + [pl.BlockSpec(memory_space=pltpu.VMEM)],
        input_output_aliases={i: i for i in range(2 * n)}, compiler_params=pltpu.CompilerParams(**SPLIT),
    )(*handle["bufs"], handle["send"], handle["recv"], *after)
    return list(out[n:2 * n]), out[-1]


def _gather_forward(lands, jobs, *, name, deps=()):
    n = len(lands)

    def body(*refs):
        l_refs, send, recv = refs[n + len(deps):2 * n + len(deps)], refs[-2], refs[-1]
        x, y, c = _coords()
        for j, (px, py) in enumerate(_other_chips(x, y)):
            for a, _, row, rows in _shard_rows(jobs, 2 * px + py):
                blk = l_refs[a].at[c, pl.ds(row, rows)]
                pltpu.make_async_remote_copy(src_ref=blk, dst_ref=blk, send_sem=send.at[3 * a + j], recv_sem=recv.at[3 * a + j],
                                             device_id=(x, y, 1 - c), device_id_type=MESH).start()
        for a in range(n):
            like = l_refs[a].at[0, pl.ds(0, _shard_total(jobs, a))]
            for j in range(3):
                cp = pltpu.make_async_remote_copy(src_ref=like, dst_ref=like, send_sem=send.at[3 * a + j],
                                                  recv_sem=recv.at[3 * a + j], device_id=(x, y, c), device_id_type=MESH)
                cp.wait_send()
                cp.wait_recv()

    sem = pltpu.SemaphoreType.DMA((3 * n,))
    return pl.pallas_call(
        body, name=name, in_specs=[ANY] * (n + len(deps)), out_specs=[ANY] * n, input_output_aliases={i: i for i in range(n)},
        out_shape=[jax.ShapeDtypeStruct(a.shape, a.dtype) for a in lands], scratch_shapes=[sem, sem],
    )(*lands, *deps)


def _forward_start(lands, jobs, *, name, deps=()):
    n, nd = len(lands), len(deps)

    def body(*refs):
        l_refs, sems, token = refs[:n], refs[n + nd:n + nd + 2 * n], refs[-1]
        x, y, c = _coords()
        for j, (px, py) in enumerate(_other_chips(x, y)):
            for a, _, row, rows in _shard_rows(jobs, 2 * px + py):
                blk = l_refs[a].at[c, pl.ds(row, rows)]
                pltpu.make_async_remote_copy(src_ref=blk, dst_ref=blk, send_sem=sems[2 * a].at[j], recv_sem=sems[2 * a + 1].at[j],
                                             device_id=(x, y, 1 - c), device_id_type=MESH).start()
        token[...] = jnp.zeros_like(token)

    out = pl.pallas_call(
        body, name=name, in_specs=[HBM] * n + [ANY] * nd,
        out_shape=(*[pltpu.SemaphoreType.DMA((3,))] * (2 * n), *[pltpu.HBM(a.shape, a.dtype) for a in lands],
                   jax.ShapeDtypeStruct((8, LANE), F32)),
        out_specs=(*[SEMS] * (2 * n), *[HBM] * n, pl.BlockSpec(memory_space=pltpu.VMEM)),
        input_output_aliases={i: 2 * n + i for i in range(n)}, compiler_params=pltpu.CompilerParams(**SPLIT),
    )(*[_in_hbm(a) for a in lands], *deps)
    return [dict(send=out[2 * a], recv=out[2 * a + 1], buf=out[2 * n + a]) for a in range(n)], out[-1]


def _forward_wait(handle, jobs, a, after, *, name):
    total = _shard_total(jobs, a)

    def body(l_ref, send, recv, *rest):
        x, y, c = _coords()
        like = l_ref.at[0, pl.ds(0, total)]
        for j in range(3):
            cp = pltpu.make_async_remote_copy(src_ref=like, dst_ref=like, send_sem=send.at[j], recv_sem=recv.at[j],
                                              device_id=(x, y, c), device_id_type=MESH)
            cp.wait_send()
            cp.wait_recv()

    buf = handle["buf"]
    return pl.pallas_call(
        body, name=name, in_specs=[HBM, SEMS, SEMS] + [ANY] * len(after), out_shape=pltpu.HBM(buf.shape, buf.dtype),
        out_specs=HBM, input_output_aliases={0: 0}, compiler_params=pltpu.CompilerParams(**SPLIT),
    )(buf, handle["send"], handle["recv"], *after)


def _add_sibling(g, recv, sel, *, name):
    rows, hw = recv.shape
    tr = _row_tile(rows, hw * 4, budget=4 << 20)

    def body(sel_ref, g_ref, r_ref, o_ref):
        o_ref[...] = (g_ref[...] + r_ref[...]).astype(BF16)

    return pl.pallas_call(
        body, name=name, out_shape=jax.ShapeDtypeStruct((rows, hw), BF16),
        grid_spec=pltpu.PrefetchScalarGridSpec(
            num_scalar_prefetch=1, grid=(rows // tr,),
            in_specs=[pl.BlockSpec((None, tr, hw), lambda i, s: (s[0], i, 0)), pl.BlockSpec((tr, hw), lambda i, s: (i, 0))],
            out_specs=pl.BlockSpec((tr, hw), lambda i, s: (i, 0))),
        compiler_params=_params("parallel"),
    )(sel, g, recv)


class _Job:
    def __init__(self, a, blk, n_outer, n_inner, stride, start):
        self.a, self.blk, self.n_outer, self.n_inner, self.stride, self.start = a, blk, n_outer, n_inner, stride, start
        self.rows_out = n_outer * n_inner * blk

    def pieces(self, k):
        return [(self.start(k) + o * self.stride * self.blk, self.n_inner * self.blk) for o in range(self.n_outer)]


def _block_rows(rows, cap, *also):
    best = None
    for b in range(16, min(rows, cap) + 1, 16):
        if rows % b == 0 and all(v % b == 0 for v in also):
            best = b
    assert best is not None, (rows, also)
    return best


def _ffn_jobs(lay):
    b = _block_rows(lay.nff, 704, lay.dff)
    return [_Job(0, b, 3, lay.nff // b, lay.dff // b, lambda k: lay.nff * k)]


def _ffn_weight_jobs(lay):
    b = _block_rows(lay.nff, 704)
    return [_Job(a, b, 1, lay.nff // b, 0, lambda k: lay.nff * k) for a in range(3)]


def _mix_jobs(lay):
    d, ncol, head, pad = lay.d, lay.ncol, lay.head, lay.pad
    first = lambda k, a, b: jnp.where(k == 0, a, b) if not isinstance(k, int) else (a if k == 0 else b)
    ba = _block_rows(head, 704, *[ncol * k + pad for k in (1, 2, 3)])
    bb = _block_rows(ncol - head, 704, *[ncol * k + d for k in (0, 1, 2, 3)])
    bo = _block_rows(lay.r_o, 704, d)
    bq = _block_rows(HEAD + ROPE, 704, QGROUP)
    bk = _block_rows(lay.hps * QGROUP, 704, lay.off_kv)
    return [_Job(0, ba, 1, head // ba, 0, lambda k: first(k, 0, ncol * k + pad)),
            _Job(0, bb, 1, (ncol - head) // bb, 0, lambda k: ncol * k + d),
            _Job(0, bo, 3, lay.r_o // bo, d // bo, lambda k: 7 * d + lay.r_o * k),
            _Job(1, bq, lay.hps, (HEAD + ROPE) // bq, QGROUP // bq, lambda k: QGROUP * lay.hps * k),
            _Job(1, bk, 1, lay.hps * QGROUP // bk, 0, lambda k: lay.off_kv + lay.hps * QGROUP * k)]


def _swap_start(gs, *, name):
    n = len(gs)
    lands = [lax.empty(g.shape[1:], g.dtype) for g in gs]

    def body(*refs):
        g_refs, land_refs, send, recv, token = refs[:n], refs[n:2 * n], refs[2 * n], refs[2 * n + 1], refs[-1]
        x, y, c = _coords()
        for a in range(n):
            pltpu.make_async_remote_copy(src_ref=g_refs[a].at[1 - c], dst_ref=land_refs[a], send_sem=send.at[a],
                                         recv_sem=recv.at[a], device_id=(x, y, 1 - c), device_id_type=MESH).start()
        token[...] = jnp.zeros_like(token)

    thru = [pltpu.HBM(a.shape, a.dtype) for a in gs + lands]
    out = pl.pallas_call(
        body, name=name, in_specs=[HBM] * (2 * n),
        out_shape=(pltpu.SemaphoreType.DMA((n,)), pltpu.SemaphoreType.DMA((n,)), *thru, jax.ShapeDtypeStruct((8, LANE), F32)),
        out_specs=(SEMS, SEMS, *[HBM] * (2 * n), pl.BlockSpec(memory_space=pltpu.VMEM)),
        input_output_aliases={i: 2 + i for i in range(2 * n)}, compiler_params=pltpu.CompilerParams(**SPLIT),
    )(*[_in_hbm(a) for a in gs + lands])
    return dict(send=out[0], recv=out[1], bufs=list(out[2:2 + 2 * n]), n=n), out[-1]


def _swap_wait(handle, after, *, name):
    n = handle["n"]

    def body(*refs):
        g_refs, land_refs, send, recv = refs[:n], refs[n:2 * n], refs[2 * n], refs[2 * n + 1]
        x, y, c = _coords()
        for a in range(n):
            cp = pltpu.make_async_remote_copy(src_ref=g_refs[a].at[1 - c], dst_ref=land_refs[a], send_sem=send.at[a],
                                              recv_sem=recv.at[a], device_id=(x, y, 1 - c), device_id_type=MESH)
            cp.wait_send()
            cp.wait_recv()

    out = pl.pallas_call(
        body, name=name, in_specs=[HBM] * (2 * n) + [SEMS, SEMS] + [ANY] * len(after),
        out_shape=[pltpu.HBM(a.shape, a.dtype) for a in handle["bufs"]], out_specs=[HBM] * (2 * n),
        input_output_aliases={i: i for i in range(2 * n)}, compiler_params=pltpu.CompilerParams(**SPLIT),
    )(*handle["bufs"], handle["send"], handle["recv"], *after)
    return list(out[:n]), list(out[n:])


def _exchange_start(ss, jobs, *, name):
    n = len(ss)
    lands = [lax.empty((3,) + s.shape, s.dtype) for s in ss]

    def body(*refs):
        s_refs, land_refs, send, recv, token = refs[:n], refs[n:2 * n], refs[2 * n], refs[2 * n + 1], refs[-1]
        x, y, c = _coords()
        for j, (px, py) in enumerate(_other_chips(x, y)):
            for job in jobs:
                for row, rows in job.pieces(2 * px + py):
                    pltpu.make_async_remote_copy(
                        src_ref=s_refs[job.a].at[pl.ds(row, rows)], dst_ref=land_refs[job.a].at[j, pl.ds(row, rows)],
                        send_sem=send.at[n * j + job.a], recv_sem=recv.at[n * j + job.a], device_id=(px, py, c),
                        device_id_type=MESH).start()
        token[...] = jnp.zeros_like(token)

    thru = [pltpu.HBM(a.shape, a.dtype) for a in ss + lands]
    out = pl.pallas_call(
        body, name=name, in_specs=[HBM] * (2 * n),
        out_shape=(pltpu.SemaphoreType.DMA((3 * n,)), pltpu.SemaphoreType.DMA((3 * n,)), *thru, jax.ShapeDtypeStruct((8, LANE), F32)),
        out_specs=(SEMS, SEMS, *[HBM] * (2 * n), pl.BlockSpec(memory_space=pltpu.VMEM)),
        input_output_aliases={i: 2 + i for i in range(2 * n)}, compiler_params=pltpu.CompilerParams(**SPLIT),
    )(*[_in_hbm(a) for a in ss + lands])
    return dict(send=out[0], recv=out[1], bufs=list(out[2:2 + 2 * n]), n=n, jobs=jobs), out[-1]


def _exchange_wait(handle, after, *, name):
    n, jobs = handle["n"], handle["jobs"]
    total = [sum(rows for job in jobs if job.a == a for _, rows in job.pieces(0)) for a in range(n)]

    def body(*refs):
        s_refs, land_refs, send, recv = refs[:n], refs[n:2 * n], refs[2 * n], refs[2 * n + 1]
        x, y, c = _coords()
        for a in range(n):
            for j in range(3):
                all_rows = land_refs[a].at[0, pl.ds(0, total[a])]
                cp = pltpu.make_async_remote_copy(src_ref=all_rows, dst_ref=all_rows, send_sem=send.at[n * j + a],
                                                  recv_sem=recv.at[n * j + a], device_id=(x, y, c), device_id_type=MESH)
                cp.wait_send()
                cp.wait_recv()

    out = pl.pallas_call(
        body, name=name, in_specs=[HBM] * (2 * n) + [SEMS, SEMS] + [ANY] * len(after),
        out_shape=[pltpu.HBM(a.shape, a.dtype) for a in handle["bufs"]], out_specs=[HBM] * (2 * n),
        input_output_aliases={i: i for i in range(2 * n)}, compiler_params=pltpu.CompilerParams(**SPLIT),
    )(*handle["bufs"], handle["send"], handle["recv"], *after)
    return list(out[:n]), list(out[n:])


def _add_shard(s, land, job, sel, k, *, name):
    hw = s.shape[1]
    blk, no, ni, stride = job.blk, job.n_outer, job.n_inner, job.stride
    scal = jnp.stack([sel, job.start(k) // blk]).astype(jnp.int32)

    def body(sc_ref, own_ref, r_ref, o_ref):
        o_ref[...] = ((own_ref[...].astype(F32) + r_ref[0].astype(F32)) + r_ref[1].astype(F32)) + r_ref[2].astype(F32)

    return pl.pallas_call(
        body, name=name, out_shape=jax.ShapeDtypeStruct((2, job.rows_out, hw), F32),
        grid_spec=pltpu.PrefetchScalarGridSpec(
            num_scalar_prefetch=1, grid=(no, ni),
            in_specs=[pl.BlockSpec((blk, hw), lambda o, b, sc: (sc[1] + o * stride + b, 0)),
                      pl.BlockSpec((3, blk, hw), lambda o, b, sc: (0, sc[1] + o * stride + b, 0))],
            out_specs=pl.BlockSpec((None, blk, hw), lambda o, b, sc: (sc[0], o * ni + b, 0))),
        compiler_params=_params("parallel", "parallel"),
    )(scal, s, land)


def _join_list(fs, *, name):
    n = len(fs)

    def body(*refs):
        f_refs, send_sems, recv_sems = refs[n:2 * n], refs[2 * n], refs[2 * n + 1]
        x, y, c = _coords()
        copies = [pltpu.make_async_remote_copy(
            src_ref=f.at[c], dst_ref=f.at[c], send_sem=send_sems.at[a], recv_sem=recv_sems.at[a],
            device_id=(x, y, 1 - c), device_id_type=MESH) for a, f in enumerate(f_refs)]
        for cp in copies:
            cp.start()
        for cp in copies:
            cp.wait()

    sem = pltpu.SemaphoreType.DMA((n,))
    return pl.pallas_call(
        body, name=name, in_specs=[ANY] * n, out_specs=[ANY] * n, input_output_aliases={i: i for i in range(n)},
        out_shape=[jax.ShapeDtypeStruct(f.shape, f.dtype) for f in fs], scratch_shapes=[sem, sem],
    )(*fs)


def _all_reduce_small(vec, *, name):
    n = vec.shape[1]

    def body(v_ref, o_ref, buf, send_sems, recv_sems):
        x, y, c = _coords()
        me = 4 * x + 2 * y + c
        buf[me] = v_ref[...]
        copies = []
        for m in range(1, 8):
            peer = (x ^ ((m >> 2) & 1), y ^ ((m >> 1) & 1), c ^ (m & 1))
            copies.append(pltpu.make_async_remote_copy(
                src_ref=v_ref, dst_ref=buf.at[me], send_sem=send_sems.at[m - 1], recv_sem=recv_sems.at[m - 1],
                device_id=peer, device_id_type=MESH))
        for cp in copies:
            cp.start()
        for cp in copies:
            cp.wait()
        acc = buf[0]
        for d in range(1, 8):
            acc = acc + buf[d]
        o_ref[...] = acc

    return pl.pallas_call(
        body, name=name, out_shape=jax.ShapeDtypeStruct((1, n), F32),
        in_specs=[pl.BlockSpec(memory_space=pltpu.VMEM)], out_specs=pl.BlockSpec(memory_space=pltpu.VMEM),
        scratch_shapes=[pltpu.VMEM((8, 1, n), F32), pltpu.SemaphoreType.DMA((7,)), pltpu.SemaphoreType.DMA((7,))],
    )(vec)


def _ffn_fwd(x, n_pre, n_post, weight, lay, tag):
    h = _norm_fwd(x, n_pre, name=f"{tag}_norm_pre", out_dtype=BF16)
    wg = (weight(0, [h]), 0, lay.dff)
    g = _mm([(h, wg)], name=f"{tag}_gate", mode="nt")
    wu = (weight(1, [g]), 0, lay.dff)
    u, a = _mm([(h, wu)], name=f"{tag}_up", mode="nt", extras=[g], out_dtypes=[F32, BF16], wide_vmem=True,
               epilogue=lambda up, gate: (up, _silu(gate) * up))
    wd = (weight(2, [u]), 0, lay.dff)
    yv = _mm([(a, wd)], name=f"{tag}_down", mode="nn")
    out = _norm_fwd(yv, n_post, name=f"{tag}_norm_post", resid=x, scale=MACARON_SCALE)
    return out, (x, h, g, u, a, yv), (wg, wu, wd)


def _ffn_bwd(dout, saved, n_pre, n_post, weights, lay, tag, deps=(), after_act=None, after_dw=None):
    x, h, g, u, a, yv = saved
    dff = lay.dff
    gbuf = lax.empty((2, 3 * dff, lay.d // 2), F32)
    dy, dn_post = _norm_bwd(yv, n_post, dout, name=f"{tag}_norm_post_bwd", scale=MACARON_SCALE)
    wg, wu, wd = weights
    dg, du = _mm([(dy, wd)], name=f"{tag}_down_dx", mode="nt", deps=deps, extras=[g, u],
                 out_dtypes=[BF16, BF16], tm_cap=MM_TILE // 2,
                 epilogue=lambda da, gate, up: (da * up * _dsilu(gate), da * _silu(gate)))
    deps = after_act(du) if after_act is not None else ()
    gbuf = _mm([(a, dy)], name=f"{tag}_down_dw", mode="tn", into=(gbuf, 2 * dff), deps=deps)
    gbuf = _mm([(dg, h)], name=f"{tag}_gate_dw", mode="tn", into=(gbuf, 0))
    gbuf = _mm([(du, h)], name=f"{tag}_up_dw", mode="tn", into=(gbuf, dff))
    deps = after_dw(gbuf)
    dh = _mm([(dg, wg), (du, wu)], name=f"{tag}_up_dx", mode="nn", deps=deps)
    dx, dn_pre = _norm_bwd(x, n_pre, dh, name=f"{tag}_norm_pre_bwd", dres=dout)
    return dx, dn_pre, dn_post


def _rope_tables(positions):
    half = ROPE // 2
    inv_freq = ROPE_THETA ** (-jnp.arange(half, dtype=F32) / half)
    ang = positions.astype(F32)[:, None] * inv_freq
    cos, sin = jnp.cos(ang), jnp.sin(ang)
    z = jnp.zeros_like(cos)
    z2 = jnp.zeros((positions.shape[0], LANE - ROPE), F32)
    return (jnp.concatenate([cos, cos, z2], axis=1), jnp.concatenate([-sin, z, z2], axis=1),
            jnp.concatenate([z, sin, z2], axis=1))


def kernel(x, positions, ffn1_norm_pre, ffn1_w_gate, ffn1_w_up, ffn1_w_down, ffn1_norm_post, mix_norm_pre, w_in, mla_q_norm, mla_w_q_up, mla_kv_norm, mla_w_kv_up, mla_w_o, hgrn_lb_logits, hgrn_out_norm, hgrn_w_o, w_out, mix_norm_post, ffn2_norm_pre, ffn2_w_gate, ffn2_w_up, ffn2_w_down, ffn2_norm_post, loss_target, m_ffn1_norm_pre, m_ffn1_w_gate, m_ffn1_w_up, m_ffn1_w_down, m_ffn1_norm_post, m_mix_norm_pre, m_w_in, m_mla_q_norm, m_mla_w_q_up, m_mla_kv_norm, m_mla_w_kv_up, m_mla_w_o, m_hgrn_lb_logits, m_hgrn_out_norm, m_hgrn_w_o, m_w_out, m_mix_norm_post, m_ffn2_norm_pre, m_ffn2_w_gate, m_ffn2_w_up, m_ffn2_w_down, m_ffn2_norm_post, v_ffn1_norm_pre, v_ffn1_w_gate, v_ffn1_w_up, v_ffn1_w_down, v_ffn1_norm_post, v_mix_norm_pre, v_w_in, v_mla_q_norm, v_mla_w_q_up, v_mla_kv_norm, v_mla_w_kv_up, v_mla_w_o, v_hgrn_lb_logits, v_hgrn_out_norm, v_hgrn_w_o, v_w_out, v_mix_norm_post, v_ffn2_norm_pre, v_ffn2_w_gate, v_ffn2_w_up, v_ffn2_w_down, v_ffn2_norm_post):
    given = dict(locals())
    wts = {n: given[n] for n in ALL_WEIGHTS}
    mom = {n: given["m_" + n] for n in ALL_WEIGHTS}
    var = {n: given["v_" + n] for n in ALL_WEIGHTS}
    xin = x[0]
    target = loss_target[0]
    t, d = xin.shape
    cx, cy, cc = _coords()

    q_lora, kv_lora = mla_q_norm.shape[1], mla_kv_norm.shape[1]
    nh_mla = 4 * mla_w_kv_up.shape[2] // QGROUP
    lay = _Layout(d, 4 * ffn1_w_gate.shape[2], 4 * w_in.shape[2], q_lora, kv_lora, nh_mla)
    jobs_mix = _mix_jobs(lay)
    def pack(src, col_sharded, row_sharded=()):
        a = jnp.concatenate([src[n][0].T.astype(BF16) for n in col_sharded] + [src[n][0].astype(BF16) for n in row_sharded])
        return a.reshape(a.shape[0], 2, a.shape[1] // 2).transpose(1, 0, 2)

    jobs_w = _ffn_weight_jobs(lay)
    ffn_packs = lambda src, tag: [pack(src, [f"{tag}_w_gate"]), pack(src, [f"{tag}_w_up"]), pack(src, [], [f"{tag}_w_down"])]
    ffn_lands = lambda: [lax.empty((2, lay.dff, d // 2), BF16) for _ in range(3)]

    def handed_over(handles, tag):
        return lambda i, after: _forward_wait(handles[i], jobs_w, i, after, name=f"gather_{tag}_forward_wait_{i}")

    got1, tok = _gather_start(ffn_packs(wts, "ffn1"), ffn_lands(), jobs_w, name="gather_ffn1")
    later, _ = lax.optimization_barrier(({n: wts[n] for n in BIG_WEIGHTS if not n.startswith("ffn1")}, tok))
    packs_mix = [pack(later, ["w_in"], ["mla_w_o", "hgrn_w_o", "w_out"]), pack(later, ["mla_w_q_up", "mla_w_kv_up"])]
    packs_ffn2 = ffn_packs(later, "ffn2")
    lands_mix = [jnp.zeros((2, 10 * d, d // 2), BF16), jnp.zeros((2, lay.rows_narrow, q_lora // 2), BF16)]
    arrived, tok = _gather_wait(got1, packs_mix + packs_ffn2 + lands_mix, name="gather_ffn1_wait")
    got_m, tok = _gather_start(packs_mix, lands_mix, jobs_mix, name="gather_mix", deps=[tok])
    handing1, _ = _forward_start(arrived[:1], jobs_w[:1], name="gather_ffn1_forward_gate", deps=[tok])

    def ffn1_weight(i, after):
        if i == 0:
            gate = _forward_wait(handing1[0], jobs_w, 0, after, name="gather_ffn1_forward_wait_0")
            handing1.extend(_forward_start(arrived[1:], jobs_w[:2], name="gather_ffn1_forward_rest", deps=[gate])[0])
            return gate
        return _forward_wait(handing1[i], jobs_w, 0, after, name=f"gather_ffn1_forward_wait_{i}")
    col_kr = q_lora + kv_lora
    hgrn_cols = [d, 2 * d, 3 * d, 4 * d]
    col_ga, col_gb = 5 * d, 6 * d
    tabs = _rope_tables(positions[0])
    scale = (HEAD + ROPE) ** -0.5

    x1, saved1, w_ffn1 = _ffn_fwd(xin, ffn1_norm_pre, ffn1_norm_post, ffn1_weight, lay, "ffn1")

    arrived, tok = _gather_wait(got_m, [x1], name="gather_mix_wait")
    got2, tok = _gather_start(packs_ffn2, ffn_lands(), jobs_w, name="gather_ffn2", deps=[tok])
    wide, narrow = _gather_forward(arrived, jobs_mix, name="gather_mix_forward", deps=[tok])
    w_in_v = (wide, 0, 7 * d)
    w_o_v = {n: (wide, (7 + i) * d, d) for i, n in enumerate(("mla_w_o", "hgrn_w_o", "w_out"))}
    w_q_v = (narrow, lay.off_q, nh_mla * QGROUP)
    w_kv_v = (narrow, lay.off_kv, nh_mla * QGROUP)

    h2 = _norm_fwd(x1, mix_norm_pre, name="mix_norm_pre", out_dtype=BF16)
    proj = _mm([(h2, w_in_v)], name="mix_in", mode="nt", deps=[tok])
    cqn = _norm_fwd(proj, mla_q_norm, name="mla_q_norm", out_dtype=BF16, col=0)
    ckvn = _norm_fwd(proj, mla_kv_norm, name="mla_kv_norm", out_dtype=BF16, col=q_lora)
    qp = _mm([(cqn, w_q_v)], name="mla_q_up", mode="nt")
    kvb = _mm([(ckvn, w_kv_v)], name="mla_kv_up", mode="nt", out_dtype=BF16)
    qcat = _rope(qp, tabs, name="rope_q", group=QGROUP, backward=False, out_dtype=BF16)
    krot = _rope(proj, tabs, name="rope_k", group=LANE, backward=False, out_dtype=BF16, col=col_kr, ngroup=1)
    o_mla = _attn_fwd(qcat, kvb, krot, name="mla_attention", scale=scale)
    y_a = _mm([(o_mla, w_o_v["mla_w_o"])], name="mla_out", mode="nn")

    o_raw, yb, states = _hgrn_fwd(proj, hgrn_cols, d, hgrn_lb_logits, hgrn_out_norm, name="hgrn_scan")
    handing2, tok = _forward_start(_gather_wait(got2, [o_raw], name="gather_ffn2_wait")[0], jobs_w, name="gather_ffn2_forward")
    y_b = _mm([(yb, w_o_v["hgrn_w_o"])], name="hgrn_out", mode="nn", deps=[tok])

    merged = _merge_fwd(proj, col_ga, col_gb, y_a, y_b, name="mix_merge")
    y_mix = _mm([(merged, w_o_v["w_out"])], name="mix_out", mode="nn")
    x2 = _norm_fwd(y_mix, mix_norm_post, name="mix_norm_post", resid=x1, scale=1.0)

    x3, saved2, w_ffn2 = _ffn_fwd(x2, ffn2_norm_pre, ffn2_norm_post, handed_over(handing2, "ffn2"), lay, "ffn2")
    dx3, loss_local = _loss_head(x3, target, name="loss_head")

    grads, deltas, new_m, new_v = {}, {}, {}, {}
    sel = cc.astype(jnp.int32)
    sel1 = jnp.reshape(sel, (1,))
    me_chip = (2 * cx + cy).astype(jnp.int32)

    def reduce_mid(handle, after, jobs, tag):
        bufs, recvd = _swap_wait(handle, after, name=f"grad_swap_{tag}_wait")
        sums = [_add_sibling(b, r, sel1, name=f"grad_add_sibling_{tag}_{i}") for i, (b, r) in enumerate(zip(bufs, recvd))]
        return _exchange_start(sums, jobs, name=f"grad_exchange_{tag}")

    def reduce_end(handle, after, tag):
        sums, lands = _exchange_wait(handle, after, name=f"grad_exchange_{tag}_wait")
        parts = [_add_shard(sums[job.a], lands[job.a], job, sel, me_chip, name=f"grad_add_chips_{tag}_{i}")
                 for i, job in enumerate(handle["jobs"])]
        return _join_list(parts, name=f"grad_join_{tag}")

    def natural(part, lo, rows, transposed):
        g_n = part[:, lo:lo + rows]
        hw_n = g_n.shape[2]
        return g_n.transpose(0, 2, 1).reshape(2 * hw_n, rows) if transposed else g_n.transpose(1, 0, 2).reshape(rows, 2 * hw_n)

    def adam(names, deps=()):
        for i, n in enumerate(names):
            shp = wts[n].shape
            two_d = (lambda a: a[0]) if n in BIG_WEIGHTS else (lambda a: a)
            dl, nm, nv = _adamw(two_d(wts[n]), grads[n], two_d(mom[n]), two_d(var[n]), name=f"adamw_{n}",
                                deps=deps if i == 0 else ())
            grads[n] = grads[n].reshape(shp)
            deltas[n], new_m[n], new_v[n] = dl.reshape(shp), nm.reshape(shp), nv.reshape(shp)
        return [deltas[n] for n in names]

    def ffn_grads(joined, tag, deps=()):
        nff = lay.nff
        grads[f"{tag}_w_gate"] = natural(joined[0], 0, nff, True)
        grads[f"{tag}_w_up"] = natural(joined[0], nff, nff, True)
        grads[f"{tag}_w_down"] = natural(joined[0], 2 * nff, nff, False)
        return adam([f"{tag}_w_gate", f"{tag}_w_up", f"{tag}_w_down"], deps)

    swaps = {}

    def start_swap(tag):
        def hook(gbuf):
            swaps[tag], started = _swap_start([gbuf], name=f"grad_swap_{tag}")
            return [started]
        return hook

    dx2, grads["ffn2_norm_pre"], grads["ffn2_norm_post"] = _ffn_bwd(
        dx3, saved2, ffn2_norm_pre, ffn2_norm_post, w_ffn2, lay, "ffn2", after_dw=start_swap("ffn2"))

    gwide = lax.empty((2, 10 * d, d // 2), F32)
    gnarrow = lax.empty((2, lay.rows_narrow, q_lora // 2), F32)
    dy_mix, grads["mix_norm_post"] = _norm_bwd(y_mix, mix_norm_post, dx2, name="mix_norm_post_bwd")
    dmerged = _mm([(dy_mix, w_o_v["w_out"])], name="mix_out_dx", mode="nt")
    gwide = _mm([(merged, dy_mix)], name="mix_out_dw", mode="tn", into=(gwide, 9 * d))
    dga, dgb, dy_a, dy_b = _merge_bwd(dmerged, proj, col_ga, col_gb, y_a, y_b, name="mix_merge_bwd")

    do_mla = _mm([(dy_a, w_o_v["mla_w_o"])], name="mla_out_dx", mode="nt")
    gwide = _mm([(o_mla, dy_a)], name="mla_out_dw", mode="tn", into=(gwide, 7 * d))
    dqcat, dkv, dkr = _attn_bwd(qcat, kvb, krot, do_mla, name="mla_attention_bwd", scale=scale)
    exch2, tok = reduce_mid(swaps["ffn2"], [dkr], _ffn_jobs(lay), "ffn2")

    dqp = _rope(dqcat, tabs, name="rope_q_bwd", group=QGROUP, backward=True, out_dtype=BF16)
    dk_r = _rope(dkr, tabs, name="rope_k_bwd", group=LANE, backward=True, out_dtype=BF16)
    dcqn = _mm([(dqp, w_q_v)], name="mla_q_up_dx", mode="nn", deps=[tok])
    gnarrow = _mm([(dqp, cqn)], name="mla_q_up_dw", mode="tn", into=(gnarrow, lay.off_q))
    dkvb = dkv.astype(BF16)
    dckvn = _mm([(dkvb, w_kv_v)], name="mla_kv_up_dx", mode="nn")
    gnarrow = _mm([(dkvb, ckvn)], name="mla_kv_up_dw", mode="tn", into=(gnarrow, lay.off_kv))
    dc_q, grads["mla_q_norm"] = _norm_bwd(proj, mla_q_norm, dcqn, name="mla_q_norm_bwd", col=0, dx_dtype=BF16)
    dc_kv, grads["mla_kv_norm"] = _norm_bwd(proj, mla_kv_norm, dckvn, name="mla_kv_norm_bwd", col=q_lora, dx_dtype=BF16)

    dyb = _mm([(dy_b, w_o_v["hgrn_w_o"])], name="hgrn_out_dx", mode="nt")
    gwide = _mm([(yb, dy_b)], name="hgrn_out_dw", mode="tn", into=(gwide, 8 * d))
    dhq, dhf, dhi, dhg, dlb_h, dnorm_h = _hgrn_bwd(proj, hgrn_cols, d, o_raw, dyb, states, hgrn_lb_logits, hgrn_out_norm,
                                                   name="hgrn_scan_bwd")

    dhead = jnp.concatenate([dc_q, dc_kv, dk_r, jnp.zeros((t, d - col_kr - LANE), BF16)], axis=1)
    dparts = [dhead, dhq, dhf, dhi, dhg, dga, dgb]
    dh2 = _mm([(p, (wide, i * d, d)) for i, p in enumerate(dparts)], name="mix_in_dx", mode="nn")
    for i, p in enumerate(dparts):
        gwide = _mm([(p, h2)], name=f"mix_in_dw_{i}", mode="tn", into=(gwide, i * d))
    dx1, grads["mix_norm_pre"] = _norm_bwd(x1, mix_norm_pre, dh2, name="mix_norm_pre_bwd", dres=dx2)
    swap_m, tok = _swap_start([gwide, gnarrow], name="grad_swap_mix")
    joined2 = reduce_end(exch2, [dx1], "ffn2")

    exchanges = {}

    def mix_exchange(after):
        exchanges["mix"], started = reduce_mid(swap_m, [after], _mix_jobs(lay), "mix")
        return [started]

    dx0, grads["ffn1_norm_pre"], grads["ffn1_norm_post"] = _ffn_bwd(
        dx1, saved1, ffn1_norm_pre, ffn1_norm_post, w_ffn1, lay, "ffn1", deps=[tok], after_act=mix_exchange,
        after_dw=start_swap("ffn1"))
    exch1, tok = reduce_mid(swaps["ffn1"], [dx0], _ffn_jobs(lay), "ffn1")

    joined_m = reduce_end(exchanges["mix"], [dx0, tok], "mix")
    done = ffn_grads(joined2, "ffn2")
    grads["w_in"] = natural(jnp.concatenate([joined_m[0], joined_m[1]], axis=1), 0, lay.ncol, True)
    for i, n in enumerate(("mla_w_o", "hgrn_w_o", "w_out")):
        grads[n] = natural(joined_m[2], i * lay.r_o, lay.r_o, False)
    grads["mla_w_q_up"] = natural(joined_m[3], 0, lay.hps * (HEAD + ROPE), True)
    grads["mla_w_kv_up"] = natural(joined_m[4], 0, lay.hps * QGROUP, True)
    done += adam(["w_in", "mla_w_q_up", "mla_w_kv_up", "mla_w_o", "hgrn_w_o", "w_out"])

    joined1 = reduce_end(exch1, done, "ffn1")

    dlb = dlb_h.reshape(1, -1)
    dnorm = jnp.sum(dnorm_h, axis=0)
    small = {**{n: grads[n] for n in SMALL_WEIGHTS if n not in ("hgrn_lb_logits", "hgrn_out_norm")},
             "hgrn_lb_logits": dlb, "hgrn_out_norm": dnorm}
    vec, _ = lax.optimization_barrier((jnp.concatenate([small[n] for n in SMALL_WEIGHTS], axis=1), joined1[0]))
    vec = _all_reduce_small(vec, name="grad_all_reduce_small")
    off = 0
    for n in SMALL_WEIGHTS:
        w_n = small[n].shape[1]
        grads[n] = vec[:, off:off + w_n]
        off += w_n
    grads["hgrn_lb_logits"] = _lb_logits_grad(hgrn_lb_logits, grads["hgrn_lb_logits"], name="lb_logits_grad")

    adam(list(SMALL_WEIGHTS))
    ffn_grads(joined1, "ffn1")

    loss = lax.psum(loss_local, ("x", "y", "c"))
    dx_out = dx0.reshape(x.shape)
    return (loss, dx_out, *[grads[n] for n in ALL_WEIGHTS], *[deltas[n] for n in ALL_WEIGHTS],
            *[new_m[n] for n in ALL_WEIGHTS], *[new_v[n] for n in ALL_WEIGHTS])
```
